```python
import jax, jax.numpy as jnp
from jax import lax
import numpy as np

D_MODEL = 1024
BATCH = 2
SEQ = 8192
DEPTH = 2
DEC_BATCH = 8
DEC_SEQ = 64
PAST_LEN = 4096

CHUNK = 64
Q_BLOCK = 128
ROPE_THETA = 500000.0
RMS_EPS = 1e-6
NEG_INF = -1e30

N_AB_LAYERS = (DEPTH + 1) // 2
N_C_LAYERS = DEPTH // 2

A_HEADS = 8
A_NOPE = 64
A_ROPE = 32
A_QK = A_NOPE + A_ROPE
A_V = 64
A_Q_RANK = 384
A_KV_RANK = 256
A_WIDTH = A_HEADS * A_V
A_SCALE = A_QK ** -0.5

B_HEADS = 8
B_HEAD_DIM = 64
B_WIDTH = B_HEADS * B_HEAD_DIM
B_PAST_CHUNKS = 8
B_BAND = B_PAST_CHUNKS * CHUNK
B_MAX_REL = 128
B_SCALE = B_HEAD_DIM ** -0.5

C_HEADS = 16
C_KV_HEADS = 2
C_GROUP = C_HEADS // C_KV_HEADS
C_HEAD_DIM = 64
C_WIDTH = C_HEADS * C_HEAD_DIM
C_WINDOW = 128
C_PAST_CHUNKS = C_WINDOW // CHUNK
C_ROT = C_HEAD_DIM // 4
C_SCALE = C_HEAD_DIM ** -0.5

AB_SIZES = (A_Q_RANK, A_KV_RANK, A_ROPE, A_WIDTH, B_WIDTH, B_WIDTH, B_WIDTH, B_WIDTH)
AB_IN = A_Q_RANK + A_KV_RANK + A_ROPE + A_WIDTH + 4 * B_WIDTH
C_SIZES = (C_WIDTH, C_KV_HEADS * C_HEAD_DIM, C_KV_HEADS * C_HEAD_DIM, C_WIDTH)
C_IN = 2 * C_WIDTH + 2 * C_KV_HEADS * C_HEAD_DIM

kernel_name = 'hybrid_chunk_stream_encoder_step'


def _split(z, sizes):
    idx, acc = [], 0
    for n in sizes[:-1]:
        acc += n
        idx.append(acc)
    return jnp.split(z, idx, axis=-1)


def _rms(x, g):
    xf = x.astype(jnp.float32)
    y = xf * lax.rsqrt(jnp.mean(xf * xf, axis=-1, keepdims=True) + RMS_EPS)
    return (y * g.astype(jnp.float32)).astype(x.dtype)


def _rope(x, pos, rot):
    half = rot // 2
    inv = jnp.power(ROPE_THETA, -jnp.arange(half, dtype=jnp.float32) * 2.0 / rot)
    ang = pos.astype(jnp.float32)[:, None] * inv[None, :]
    cos = jnp.cos(ang)[:, None, :].astype(x.dtype)
    sin = jnp.sin(ang)[:, None, :].astype(x.dtype)
    x1, x2 = x[..., :half], x[..., half:rot]
    return jnp.concatenate([x1 * cos - x2 * sin, x2 * cos + x1 * sin, x[..., rot:]], axis=-1)


def _grouped_attend(q, k, v, valid, scale, bias=None, sinks=None):
    s = jnp.einsum('...qhgd,...khd->...hgqk', q, k).astype(jnp.float32) * scale
    if bias is not None:
        s = s + bias.astype(jnp.float32)
    s = jnp.where(valid, s, NEG_INF)
    if sinks is not None:
        hk, g = q.shape[-3], q.shape[-2]
        sk = jnp.broadcast_to(sinks.astype(jnp.float32).reshape(hk, g, 1, 1), s.shape[:-1] + (1,))
        p = jax.nn.softmax(jnp.concatenate([s, sk], axis=-1), axis=-1)[..., :-1]
    else:
        p = jax.nn.softmax(s, axis=-1)
    return jnp.einsum('...hgqk,...khd->...qhgd', p.astype(v.dtype), v)


def _rel_bias(table, rel):
    idx = jnp.clip(rel, -B_MAX_REL, B_MAX_REL) + B_MAX_REL
    return jnp.expand_dims(jnp.moveaxis(table[:, idx], 0, -3), -3)


def _band_gather(k, n_past):
    b, s = k.shape[:2]
    nc = s // CHUNK
    kc = k.reshape((b, nc, CHUNK) + k.shape[2:])
    kp = jnp.pad(kc, [(0, 0), (n_past, 0)] + [(0, 0)] * (kc.ndim - 2))
    idx = jnp.arange(nc)[:, None] + jnp.arange(n_past + 1)[None, :]
    band = kp[:, idx]
    return band.reshape((b, nc, (n_past + 1) * CHUNK) + k.shape[2:])


def _band_prompt(q, k, v, n_past, scale, rel_table=None, sinks=None):
    b, s = q.shape[:2]
    nc = s // CHUNK
    band = (n_past + 1) * CHUNK
    qc = q.reshape((b, nc, CHUNK) + q.shape[2:])
    kb, vb = _band_gather(k, n_past), _band_gather(v, n_past)
    kpos = (jnp.arange(nc)[:, None] - n_past) * CHUNK + jnp.arange(band)[None, :]
    qpos = jnp.arange(nc)[:, None] * CHUNK + jnp.arange(CHUNK)[None, :]
    valid = (kpos >= 0)[:, None, None, None, :]
    bias = None
    if rel_table is not None:
        bias = _rel_bias(rel_table, qpos[:, :, None] - kpos[:, None, :])
    o = _grouped_attend(qc, kb, vb, valid, scale, bias, sinks)
    return o.reshape(b, s, -1)


def _band_sample(q, k_new, v_new, k_buf, v_buf, pos, scale, rel_table=None, sinks=None):
    b, t = q.shape[:2]
    w = k_buf.shape[1]
    k = jnp.concatenate([k_buf, k_new], axis=1)
    v = jnp.concatenate([v_buf, v_new], axis=1)
    kpos = jnp.concatenate([pos[0] - w + jnp.arange(w), pos])
    bias = None
    if rel_table is not None:
        bias = _rel_bias(rel_table, pos[:, None] - kpos[None, :])
    o = _grouped_attend(q, k, v, jnp.asarray(True), scale, bias, sinks)
    return o.reshape(b, t, -1), k[:, -w:], v[:, -w:]


def _mla_keys(c, kr, w_ukv):
    kv = jnp.einsum('bkr,rhe->bkhe', c, w_ukv.reshape(A_KV_RANK, A_HEADS, A_NOPE + A_V))
    k = jnp.concatenate([kv[..., :A_NOPE], jnp.broadcast_to(kr[:, :, None, :], kv.shape[:3] + (A_ROPE,))], axis=-1)
    return k, kv[..., A_NOPE:]


def _mla_prompt(q, k, v):
    b, s = q.shape[:2]
    kchunk = jnp.arange(s) // CHUNK

    def blk(i):
        qs = i * Q_BLOCK
        qb = lax.dynamic_slice_in_dim(q, qs, Q_BLOCK, axis=1)[:, :, :, None, :]
        qchunk = (qs + jnp.arange(Q_BLOCK)) // CHUNK
        mask = kchunk[None, :] <= qchunk[:, None]
        return _grouped_attend(qb, k, v, mask, A_SCALE)

    o = lax.map(blk, jnp.arange(s // Q_BLOCK))
    return jnp.moveaxis(o, 0, 1).reshape(b, s, A_WIDTH)


def _ab_layer(h, pos, cache, pre_g, post_g, w_in, q_norm, kv_norm, w_uq, w_ukv, rel_table, w_out):
    b, s, _ = h.shape
    xn = _rms(h, pre_g)
    q_lat, c_kv, k_r, g_a, q_b, k_b, v_b, g_b = _split(xn @ w_in, AB_SIZES)
    qa = (_rms(q_lat, q_norm) @ w_uq).reshape(b, s, A_HEADS, A_QK)
    qa = jnp.concatenate([qa[..., :A_NOPE], _rope(qa[..., A_NOPE:], pos, A_ROPE)], axis=-1)
    c_new = _rms(c_kv, kv_norm)
    kr_new = _rope(k_r[:, :, None, :], pos, A_ROPE)[:, :, 0]
    q_b = q_b.reshape(b, s, B_HEADS, 1, B_HEAD_DIM)
    k_b = k_b.reshape(b, s, B_HEADS, B_HEAD_DIM)
    v_b = v_b.reshape(b, s, B_HEADS, B_HEAD_DIM)
    if cache is None:
        ka, va = _mla_keys(c_new, kr_new, w_ukv)
        o_a = _mla_prompt(qa, ka, va)
        o_b = _band_prompt(q_b, k_b, v_b, B_PAST_CHUNKS, B_SCALE, rel_table=rel_table)
        wb = min(B_BAND, s)
        state = (c_new, kr_new, k_b[:, s - wb:], v_b[:, s - wb:])
    else:
        ckv_c, kr_c, bk_c, bv_c = cache
        ka, va = _mla_keys(jnp.concatenate([ckv_c, c_new], axis=1), jnp.concatenate([kr_c, kr_new], axis=1), w_ukv)
        o_a = _grouped_attend(qa[:, :, :, None, :], ka, va, jnp.asarray(True), A_SCALE).reshape(b, s, A_WIDTH)
        o_b, bk, bv = _band_sample(q_b, k_b, v_b, bk_c, bv_c, pos, B_SCALE, rel_table=rel_table)
        state = (c_new, kr_new, bk, bv)
    mixed = jnp.concatenate([o_a * jax.nn.silu(g_a), o_b * jax.nn.silu(g_b)], axis=-1)
    return h + _rms(mixed @ w_out, post_g), state


def _c_layer(h, pos, cache, pre_g, post_g, w_in, sinks, w_out):
    b, s, _ = h.shape
    xn = _rms(h, pre_g)
    q, k, v, g = _split(xn @ w_in, C_SIZES)
    q = _rope(q.reshape(b, s, C_HEADS, C_HEAD_DIM), pos, C_ROT).reshape(b, s, C_KV_HEADS, C_GROUP, C_HEAD_DIM)
    k = _rope(k.reshape(b, s, C_KV_HEADS, C_HEAD_DIM), pos, C_ROT)
    v = v.reshape(b, s, C_KV_HEADS, C_HEAD_DIM)
    if cache is None:
        o = _band_prompt(q, k, v, C_PAST_CHUNKS, C_SCALE, sinks=sinks)
        w = min(C_WINDOW, s)
        state = (k[:, s - w:], v[:, s - w:])
    else:
        o, kc, vc = _band_sample(q, k, v, cache[0], cache[1], pos, C_SCALE, sinks=sinks)
        state = (kc, vc)
    return h + _rms((o * jax.nn.silu(g)) @ w_out, post_g), state


def setup_inputs(seed: int = 0) -> dict:
    key = jax.random.key(seed)
    ks = jax.random.split(key, 22)
    f32 = jnp.float32

    def nrm(k, shape, scale=1.0):
        return jax.random.normal(k, shape, f32) * scale

    def gain(k, shape):
        return 1.0 + 0.02 * jax.random.normal(k, shape, f32)

    wb = min(B_BAND, PAST_LEN)
    wc = min(C_WINDOW, PAST_LEN)
    return {
        'x_prompt': nrm(ks[0], (BATCH, SEQ, D_MODEL)),
        'x_sample': nrm(ks[1], (DEC_BATCH, DEC_SEQ, D_MODEL)),
        'cache_a_ckv': nrm(ks[2], (N_AB_LAYERS, DEC_BATCH, PAST_LEN, A_KV_RANK)),
        'cache_a_krope': nrm(ks[3], (N_AB_LAYERS, DEC_BATCH, PAST_LEN, A_ROPE)),
        'cache_b_k': nrm(ks[4], (N_AB_LAYERS, DEC_BATCH, wb, B_HEADS, B_HEAD_DIM)),
        'cache_b_v': nrm(ks[5], (N_AB_LAYERS, DEC_BATCH, wb, B_HEADS, B_HEAD_DIM)),
        'cache_c_k': nrm(ks[6], (N_C_LAYERS, DEC_BATCH, wc, C_KV_HEADS, C_HEAD_DIM)),
        'cache_c_v': nrm(ks[7], (N_C_LAYERS, DEC_BATCH, wc, C_KV_HEADS, C_HEAD_DIM)),
        'ab_pre_norm': gain(ks[8], (N_AB_LAYERS, D_MODEL)),
        'ab_post_norm': gain(ks[9], (N_AB_LAYERS, D_MODEL)),
        'ab_w_in': nrm(ks[10], (N_AB_LAYERS, D_MODEL, AB_IN), D_MODEL ** -0.5),
        'ab_q_norm': gain(ks[11], (N_AB_LAYERS, A_Q_RANK)),
        'ab_kv_norm': gain(ks[12], (N_AB_LAYERS, A_KV_RANK)),
        'ab_w_uq': nrm(ks[13], (N_AB_LAYERS, A_Q_RANK, A_HEADS * A_QK), A_Q_RANK ** -0.5),
        'ab_w_ukv': nrm(ks[14], (N_AB_LAYERS, A_KV_RANK, A_HEADS * (A_NOPE + A_V)), A_KV_RANK ** -0.5),
        'ab_rel_bias': nrm(ks[15], (N_AB_LAYERS, B_HEADS, 2 * B_MAX_REL + 1), 0.1),
        'ab_w_out': nrm(ks[16], (N_AB_LAYERS, A_WIDTH + B_WIDTH, D_MODEL), (A_WIDTH + B_WIDTH) ** -0.5),
        'c_pre_norm': gain(ks[17], (N_C_LAYERS, D_MODEL)),
        'c_post_norm': gain(ks[18], (N_C_LAYERS, D_MODEL)),
        'c_w_in': nrm(ks[19], (N_C_LAYERS, D_MODEL, C_IN), D_MODEL ** -0.5),
        'c_sinks': nrm(ks[20], (N_C_LAYERS, C_HEADS), 0.5),
        'c_w_out': nrm(ks[21], (N_C_LAYERS, C_WIDTH, D_MODEL), C_WIDTH ** -0.5),
    }


def reference(x_prompt, x_sample, cache_a_ckv, cache_a_krope, cache_b_k, cache_b_v, cache_c_k, cache_c_v,
              ab_pre_norm, ab_post_norm, ab_w_in, ab_q_norm, ab_kv_norm, ab_w_uq, ab_w_ukv, ab_rel_bias, ab_w_out,
              c_pre_norm, c_post_norm, c_w_in, c_sinks, c_w_out):
    past_len = cache_a_ckv.shape[2]
    pos_p = jnp.arange(x_prompt.shape[1], dtype=jnp.int32)
    pos_s = past_len + jnp.arange(x_sample.shape[1], dtype=jnp.int32)
    hp, hs = x_prompt, x_sample
    ab_p, ab_s, c_p, c_s = [], [], [], []
    for layer in range(DEPTH):
        i = layer // 2
        if layer % 2 == 0:
            w = (ab_pre_norm[i], ab_post_norm[i], ab_w_in[i], ab_q_norm[i], ab_kv_norm[i],
                 ab_w_uq[i], ab_w_ukv[i], ab_rel_bias[i], ab_w_out[i])
            hp, st = _ab_layer(hp, pos_p, None, *w)
            ab_p.append(st)
            hs, st = _ab_layer(hs, pos_s, (cache_a_ckv[i], cache_a_krope[i], cache_b_k[i], cache_b_v[i]), *w)
            ab_s.append(st)
        else:
            w = (c_pre_norm[i], c_post_norm[i], c_w_in[i], c_sinks[i], c_w_out[i])
            hp, st = _c_layer(hp, pos_p, None, *w)
            c_p.append(st)
            hs, st = _c_layer(hs, pos_s, (cache_c_k[i], cache_c_v[i]), *w)
            c_s.append(st)

    def stk(states, j):
        return jnp.stack([st[j] for st in states])

    return (hp, hs,
            stk(ab_p, 0), stk(ab_p, 1), stk(ab_p, 2), stk(ab_p, 3), stk(c_p, 0), stk(c_p, 1),
            stk(ab_s, 0), stk(ab_s, 1), stk(ab_s, 2), stk(ab_s, 3), stk(c_s, 0), stk(c_s, 1))
```

```python
import functools

import jax
import jax.numpy as jnp
from jax import lax
from jax.experimental import pallas as pl
from jax.experimental.pallas import tpu as pltpu

F32 = jnp.float32
BF16 = jnp.bfloat16

D_MODEL = 1024
CHUNK = 64
ROPE_THETA = 500000.0
RMS_EPS = 1e-6
NEG_INF = -1e30

A_HEADS = 8
A_NOPE = 64
A_ROPE = 32
A_QK = A_NOPE + A_ROPE
A_V = 64
A_Q_RANK = 384
A_KV_RANK = 256
A_WIDTH = A_HEADS * A_V
A_SCALE = A_QK ** -0.5

B_HEADS = 8
B_HEAD_DIM = 64
B_WIDTH = B_HEADS * B_HEAD_DIM
B_PAST_CHUNKS = 8
B_MAX_REL = 128
B_SCALE = B_HEAD_DIM ** -0.5

C_HEADS = 16
C_KV_HEADS = 2
C_GROUP = C_HEADS // C_KV_HEADS
C_HEAD_DIM = 64
C_WIDTH = C_HEADS * C_HEAD_DIM
C_WINDOW = 128
C_PAST_CHUNKS = C_WINDOW // CHUNK
C_ROT = C_HEAD_DIM // 4
C_SCALE = C_HEAD_DIM ** -0.5

LANES = 128
HALF = LANES // 2
VMEM_LIMIT = 56 * 1024 * 1024

AB_Q0, AB_C0, AB_KR0, AB_G0, AB_QB0, AB_KB0, AB_VB0, AB_NZ = 0, 384, 640, 768, 1792, 2304, 2816, 3328
C_Q0, C_K0, C_V0, C_G0, C_NZ = 0, 1024, 1280, 1536, 2560


def _params(n_axes):
    return pltpu.CompilerParams(dimension_semantics=("arbitrary",) * n_axes, vmem_limit_bytes=VMEM_LIMIT)


def _rms(x, g):
    return x * lax.rsqrt(jnp.mean(x * x, axis=-1, keepdims=True) + RMS_EPS) * g


def _rope_block(blk, rope_ref, shift):
    return (blk * rope_ref[0] + pltpu.roll(blk, shift, 1) * rope_ref[1]
            + pltpu.roll(blk, LANES - shift, 1) * rope_ref[2])


def _tail_spec(n_tiles, tm, tail_len, width):
    tb = min(tail_len, tm)
    n_blk = tail_len // tb
    return tb, pl.BlockSpec((1, tb, width), lambda b, i: (b, jnp.maximum(i - (n_tiles - n_blk), 0), 0))


def _ab_in_body(x_ref, pre_ref, w_ref, qn_ref, kvn_ref, wuq_ref, wk_ref, wv_ref, rope_ref,
                qa_ref, ka_ref, va_ref, qb_ref, kb_ref, vb_ref, sg_ref, c_ref, kr_ref, kbt_ref, vbt_ref):
    tm = x_ref.shape[1]
    tb = kbt_ref.shape[1]
    xn = _rms(x_ref[0], pre_ref[...]).astype(BF16)
    z = jnp.dot(xn, w_ref[...], preferred_element_type=F32)

    qn = _rms(z[:, AB_Q0:AB_C0], qn_ref[...]).astype(BF16)
    qa = jnp.dot(qn, wuq_ref[...], preferred_element_type=F32) * A_SCALE
    for h in range(A_HEADS):
        blk = slice(h * LANES, (h + 1) * LANES)
        qa_ref[0, :, blk] = _rope_block(qa[:, blk], rope_ref, A_ROPE // 2).astype(BF16)

    c_new = _rms(z[:, AB_C0:AB_KR0], kvn_ref[...])
    c_ref[0] = c_new
    cb = c_new.astype(BF16)
    krot = _rope_block(z[:, AB_KR0:AB_G0], rope_ref, A_ROPE // 2)
    kr_ref[0] = krot[:, A_NOPE:A_NOPE + A_ROPE]
    kn = jnp.dot(cb, wk_ref[...], preferred_element_type=F32)
    vv = jnp.dot(cb, wv_ref[...], preferred_element_type=F32)
    lane = lax.broadcasted_iota(jnp.int32, (1, LANES), 1)
    one_col = jnp.where(lane == A_V, 1.0, 0.0).astype(F32)
    for h in range(A_HEADS):
        blk = slice(h * LANES, (h + 1) * LANES)
        ka_ref[0, :, blk] = (kn[:, blk] + krot).astype(BF16)
        va_ref[0, :, blk] = (vv[:, blk] + one_col).astype(BF16)

    g = z[:, AB_G0:AB_QB0]
    sg_ref[0] = (g * jax.nn.sigmoid(g)).astype(BF16)
    qb_ref[0] = (z[:, AB_QB0:AB_KB0] * B_SCALE).astype(BF16)
    kb = z[:, AB_KB0:AB_VB0]
    vb = z[:, AB_VB0:AB_NZ]
    kb_ref[0] = kb.astype(BF16)
    vb_ref[0] = vb.astype(BF16)
    kbt_ref[0] = kb[tm - tb:, :]
    vbt_ref[0] = vb[tm - tb:, :]


def _ab_in_proj(x, rope, w, tm, tail_len):
    bsz, s, _ = x.shape
    nt = s // tm
    tb, tail_spec = _tail_spec(nt, tm, tail_len, B_WIDTH)

    def full(a):
        return pl.BlockSpec(a.shape, lambda b, i: (0,) * a.ndim)

    def rows(width):
        return pl.BlockSpec((1, tm, width), lambda b, i: (b, i, 0))

    weights = (w["pre"], w["w_in"], w["q_norm"], w["kv_norm"], w["w_uq"], w["w_k"], w["w_v"])
    out_shape = (
        jax.ShapeDtypeStruct((bsz, s, A_HEADS * LANES), BF16),
        jax.ShapeDtypeStruct((bsz, s, A_HEADS * LANES), BF16),
        jax.ShapeDtypeStruct((bsz, s, A_HEADS * LANES), BF16),
        jax.ShapeDtypeStruct((bsz, s, B_WIDTH), BF16),
        jax.ShapeDtypeStruct((bsz, s, B_WIDTH), BF16),
        jax.ShapeDtypeStruct((bsz, s, B_WIDTH), BF16),
        jax.ShapeDtypeStruct((bsz, s, A_WIDTH + B_WIDTH), BF16),
        jax.ShapeDtypeStruct((bsz, s, A_KV_RANK), F32),
        jax.ShapeDtypeStruct((bsz, s, A_ROPE), F32),
        jax.ShapeDtypeStruct((bsz, tail_len, B_WIDTH), F32),
        jax.ShapeDtypeStruct((bsz, tail_len, B_WIDTH), F32),
    )
    out_specs = (rows(1024), rows(1024), rows(1024), rows(512), rows(512), rows(512), rows(1024),
                 rows(A_KV_RANK), rows(A_ROPE), tail_spec, tail_spec)
    return pl.pallas_call(
        _ab_in_body,
        grid=(bsz, nt),
        in_specs=[rows(D_MODEL)] + [full(a) for a in weights]
        + [pl.BlockSpec((3, tm, LANES), lambda b, i: (0, i, 0))],
        out_specs=out_specs,
        out_shape=out_shape,
        compiler_params=_params(2),
        name="ab_in_proj",
    )(x, *weights, rope)


def _c_in_body(x_ref, pre_ref, w_ref, rope_ref, q_ref, k_ref, v_ref, sg_ref, kt_ref, vt_ref):
    tm = x_ref.shape[1]
    tb = kt_ref.shape[1]
    xn = _rms(x_ref[0], pre_ref[...]).astype(BF16)
    z = jnp.dot(xn, w_ref[...], preferred_element_type=F32)
    half_rot = C_ROT // 2
    for j in range(C_WIDTH // LANES):
        blk = slice(C_Q0 + j * LANES, C_Q0 + (j + 1) * LANES)
        q_ref[0, :, j * LANES:(j + 1) * LANES] = (_rope_block(z[:, blk], rope_ref, half_rot) * C_SCALE).astype(BF16)
    kd = [_rope_block(z[:, C_K0 + j * LANES:C_K0 + (j + 1) * LANES], rope_ref, half_rot) for j in range(C_KV_HEADS)]
    vd = [z[:, C_V0 + j * LANES:C_V0 + (j + 1) * LANES] for j in range(C_KV_HEADS)]
    for j in range(C_KV_HEADS):
        k_ref[0, :, j * LANES:(j + 1) * LANES] = kd[j].astype(BF16)
        v_ref[0, :, j * LANES:(j + 1) * LANES] = vd[j].astype(BF16)
    g = z[:, C_G0:C_NZ]
    sg_ref[0] = (g * jax.nn.sigmoid(g)).astype(BF16)
    lo = lax.broadcasted_iota(jnp.int32, (1, LANES), 1) < HALF
    kt_ref[0] = jnp.where(lo, kd[0], kd[1])[tm - tb:, :]
    vt_ref[0] = jnp.where(lo, vd[0], vd[1])[tm - tb:, :]


def _c_in_proj(x, rope, w, tm, tail_len):
    bsz, s, _ = x.shape
    nt = s // tm
    tb, tail_spec = _tail_spec(nt, tm, tail_len, LANES)

    def full(a):
        return pl.BlockSpec(a.shape, lambda b, i: (0,) * a.ndim)

    def rows(width):
        return pl.BlockSpec((1, tm, width), lambda b, i: (b, i, 0))

    out_shape = (
        jax.ShapeDtypeStruct((bsz, s, C_WIDTH), BF16),
        jax.ShapeDtypeStruct((bsz, s, C_KV_HEADS * LANES), BF16),
        jax.ShapeDtypeStruct((bsz, s, C_KV_HEADS * LANES), BF16),
        jax.ShapeDtypeStruct((bsz, s, C_WIDTH), BF16),
        jax.ShapeDtypeStruct((bsz, tail_len, LANES), F32),
        jax.ShapeDtypeStruct((bsz, tail_len, LANES), F32),
    )
    return pl.pallas_call(
        _c_in_body,
        grid=(bsz, nt),
        in_specs=[rows(D_MODEL), full(w["pre"]), full(w["w_in"]),
                  pl.BlockSpec((3, tm, LANES), lambda b, i: (0, i, 0))],
        out_specs=(rows(C_WIDTH), rows(256), rows(256), rows(C_WIDTH), tail_spec, tail_spec),
        out_shape=out_shape,
        compiler_params=_params(2),
        name="c_in_proj",
    )(x, w["pre"], w["w_in"], rope)


def _out_body(*refs):
    o_refs, (sg_ref, w_ref, g_ref, h_ref, out_ref) = refs[:-5], refs[-5:]
    o = jnp.concatenate([r[0].astype(F32) for r in o_refs], axis=-1) if len(o_refs) > 1 else o_refs[0][0].astype(F32)
    mixed = (o * sg_ref[0].astype(F32)).astype(BF16)
    y = jnp.dot(mixed, w_ref[...], preferred_element_type=F32)
    out_ref[0] = h_ref[0] + _rms(y, g_ref[...])


def _out_proj(o_parts, sg, w_out, post_g, h, tm):
    bsz, s, _ = h.shape

    def rows(width):
        return pl.BlockSpec((1, tm, width), lambda b, i: (b, i, 0))

    def full(a):
        return pl.BlockSpec(a.shape, lambda b, i: (0,) * a.ndim)

    return pl.pallas_call(
        _out_body,
        grid=(bsz, s // tm),
        in_specs=[rows(o.shape[-1]) for o in o_parts] + [rows(D_MODEL), full(w_out), full(post_g), rows(D_MODEL)],
        out_specs=rows(D_MODEL),
        out_shape=jax.ShapeDtypeStruct(h.shape, F32),
        compiler_params=_params(2),
        name="out_proj",
    )(*o_parts, sg, w_out, post_g, h)


def _mla_prompt_body(q_ref, k_ref, v_ref, o_ref, m_sc, acc_sc):
    tq = q_ref.shape[1]
    qi = pl.program_id(2)
    m_sc[...] = jnp.full(m_sc.shape, NEG_INF, F32)
    acc_sc[...] = jnp.zeros(acc_sc.shape, F32)

    def step(j, masked):
        start = pl.multiple_of(j * tq, tq)
        for hh in range(2):
            blk = slice(hh * LANES, (hh + 1) * LANES)
            q = q_ref[0, :, blk]
            k = k_ref[0, pl.ds(start, tq), blk]
            v = v_ref[0, pl.ds(start, tq), blk]
            s = lax.dot_general(q, k, (((1,), (1,)), ((), ())), preferred_element_type=F32)
            if masked:
                qc = lax.broadcasted_iota(jnp.int32, (tq, tq), 0) // CHUNK
                kc = lax.broadcasted_iota(jnp.int32, (tq, tq), 1) // CHUNK
                s = jnp.where(kc <= qc, s, NEG_INF)
            m_old = m_sc[hh]
            m_new = jnp.maximum(m_old, jnp.max(s, axis=-1, keepdims=True))
            p = jnp.exp(s - m_new)
            acc_sc[hh] = jnp.exp(m_old - m_new) * acc_sc[hh] + jnp.dot(p.astype(BF16), v, preferred_element_type=F32)
            m_sc[hh] = m_new

    def loop_body(j, carry):
        step(j, False)
        return carry

    lax.fori_loop(0, qi, loop_body, 0)
    step(qi, True)

    outs = []
    for hh in range(2):
        acc = acc_sc[hh]
        outs.append(acc / acc[:, A_V:A_V + 1])
    lo = lax.broadcasted_iota(jnp.int32, (1, LANES), 1) < HALF
    o_ref[0] = jnp.where(lo, outs[0], pltpu.roll(outs[1], HALF, 1)).astype(o_ref.dtype)


def _mla_prompt(qa, ka, va, tq):
    bsz, s, _ = qa.shape
    pairs = A_HEADS // 2
    return pl.pallas_call(
        _mla_prompt_body,
        grid=(bsz, pairs, s // tq),
        in_specs=[pl.BlockSpec((1, tq, 2 * LANES), lambda b, h, i: (b, i, h)),
                  pl.BlockSpec((1, s, 2 * LANES), lambda b, h, i: (b, 0, h)),
                  pl.BlockSpec((1, s, 2 * LANES), lambda b, h, i: (b, 0, h))],
        out_specs=pl.BlockSpec((1, tq, LANES), lambda b, h, i: (b, i, h)),
        out_shape=jax.ShapeDtypeStruct((bsz, s, A_WIDTH), BF16),
        scratch_shapes=[pltpu.VMEM((2, tq, 1), F32), pltpu.VMEM((2, tq, LANES), F32)],
        compiler_params=_params(3),
        name="mla_prompt",
    )(qa, ka, va)


def _mla_sample_body(q_ref, cc_ref, ckr_ref, cn_ref, krn_ref, wkt_ref, wv_ref, sel_ref, o_ref,
                     qabs_sc, qr_sc, m_sc, l_sc, acc_sc, *, tk):
    t = q_ref.shape[1]
    past = cc_ref.shape[1]
    for h in range(A_HEADS):
        qh = q_ref[0, :, h * LANES:(h + 1) * LANES]
        rows = slice(h * t, (h + 1) * t)
        qabs_sc[rows, :] = jnp.dot(qh, wkt_ref[h], preferred_element_type=F32).astype(BF16)
        qr_sc[rows, :] = jnp.dot(qh, sel_ref[...], preferred_element_type=F32).astype(BF16)
    m_sc[...] = jnp.full(m_sc.shape, NEG_INF, F32)
    l_sc[...] = jnp.zeros(l_sc.shape, F32)
    acc_sc[...] = jnp.zeros(acc_sc.shape, F32)
    contract_last = (((1,), (1,)), ((), ()))

    def update(c_t, kr_t):
        s = (lax.dot_general(qabs_sc[...], c_t, contract_last, preferred_element_type=F32)
             + lax.dot_general(qr_sc[...], kr_t, contract_last, preferred_element_type=F32))
        m_old = m_sc[...]
        m_new = jnp.maximum(m_old, jnp.max(s, axis=-1, keepdims=True))
        p = jnp.exp(s - m_new)
        alpha = jnp.exp(m_old - m_new)
        l_sc[...] = alpha * l_sc[...] + jnp.sum(p, axis=-1, keepdims=True)
        acc_sc[...] = alpha * acc_sc[...] + jnp.dot(p.astype(BF16), c_t, preferred_element_type=F32)
        m_sc[...] = m_new

    def loop_body(j, carry):
        start = pl.multiple_of(j * tk, tk)
        update(cc_ref[0, pl.ds(start, tk), :].astype(BF16), ckr_ref[0, pl.ds(start, tk), :].astype(BF16))
        return carry

    lax.fori_loop(0, past // tk, loop_body, 0)
    update(cn_ref[0].astype(BF16), krn_ref[0].astype(BF16))

    o_lat = (acc_sc[...] / l_sc[...]).astype(BF16)
    out = jnp.zeros((t, A_WIDTH), F32)
    for h in range(A_HEADS):
        out = out + jnp.dot(o_lat[h * t:(h + 1) * t, :], wv_ref[h], preferred_element_type=F32)
    o_ref[0] = out.astype(o_ref.dtype)


def _mla_sample(qa, cache_c, cache_kr, c_new, kr_new, wkt, wv, sel, tk):
    bsz, t, _ = qa.shape
    past = cache_c.shape[1]

    def per_b(shape):
        return pl.BlockSpec((1,) + shape, lambda b: (b, 0, 0))

    def full(a):
        return pl.BlockSpec(a.shape, lambda b: (0,) * a.ndim)

    rows = A_HEADS * t
    return pl.pallas_call(
        functools.partial(_mla_sample_body, tk=tk),
        grid=(bsz,),
        in_specs=[per_b((t, A_HEADS * LANES)), per_b((past, A_KV_RANK)), per_b((past, A_ROPE)),
                  per_b((t, A_KV_RANK)), per_b((t, A_ROPE)), full(wkt), full(wv), full(sel)],
        out_specs=per_b((t, A_WIDTH)),
        out_shape=jax.ShapeDtypeStruct((bsz, t, A_WIDTH), BF16),
        scratch_shapes=[pltpu.VMEM((rows, A_KV_RANK), BF16), pltpu.VMEM((rows, A_ROPE), BF16),
                        pltpu.VMEM((rows, 1), F32), pltpu.VMEM((rows, 1), F32),
                        pltpu.VMEM((rows, A_KV_RANK), F32)],
        compiler_params=_params(1),
        name="mla_sample",
    )(qa, cache_c, cache_kr, c_new, kr_new, wkt, wv, sel)


def _band_body(*refs, n_past, win, n_qblk, std_bias, has_sink, whole_band):
    if has_sink:
        q_ref, k_ref, v_ref, bias_ref, sink_ref, o_ref = refs
    else:
        q_ref, k_ref, v_ref, bias_ref, o_ref = refs
    tq = q_ref.shape[1]
    t = pl.program_id(2)
    lo = lax.broadcasted_iota(jnp.int32, (CHUNK, LANES), 1) < HALF
    contract_last = (((1,), (1,)), ((), ()))

    def do_chunk(c, start, variant):
        rows = slice(c * CHUNK, (c + 1) * CHUNK)
        pieces = []
        for r in range(n_qblk):
            qblk = q_ref[0, rows, r * LANES:(r + 1) * LANES]
            zero = jnp.zeros_like(qblk)
            pieces += [jnp.where(lo, qblk, zero), jnp.where(lo, zero, qblk)]
        qs = jnp.concatenate(pieces, axis=0)
        kw = k_ref[0, pl.ds(start, win), :]
        vw = v_ref[0, pl.ds(start, win), :]
        s = lax.dot_general(qs, kw, contract_last, preferred_element_type=F32)
        if std_bias or variant != n_past:
            s = s + bias_ref[0, variant]
        m = jnp.max(s, axis=-1, keepdims=True)
        if has_sink:
            sink = sink_ref[0]
            m = jnp.maximum(m, sink)
        p = jnp.exp(s - m)
        l = jnp.sum(p, axis=-1, keepdims=True)
        if has_sink:
            l = l + jnp.exp(sink - m)
        o = jnp.dot(p.astype(BF16), vw, preferred_element_type=F32) / l
        for r in range(n_qblk):
            top = o[(2 * r) * CHUNK:(2 * r + 1) * CHUNK]
            bot = o[(2 * r + 1) * CHUNK:(2 * r + 2) * CHUNK]
            o_ref[0, rows, r * LANES:(r + 1) * LANES] = jnp.where(lo, top, bot).astype(o_ref.dtype)

    n_chunks = tq // CHUNK
    if whole_band:
        do_chunk(0, 0, n_past)
        return

    @pl.when(t == 0)
    def _():
        for c in range(n_chunks):
            do_chunk(c, max(c - n_past, 0) * CHUNK, min(c, n_past))

    @pl.when(t > 0)
    def _():
        for c in range(n_chunks):
            do_chunk(c, pl.multiple_of(t * tq + (c - n_past) * CHUNK, CHUNK), n_past)


def _band_attention(q, k, v, bias, sink, *, n_groups, n_past, n_qblk, std_bias, tq, whole_band):
    bsz, s, _ = q.shape
    sk = k.shape[1]
    win = (n_past + 1) * CHUNK
    qw = n_qblk * LANES
    in_specs = [pl.BlockSpec((1, tq, qw), lambda b, g, i: (b, i, g)),
                pl.BlockSpec((1, sk, LANES), lambda b, g, i: (b, 0, g)),
                pl.BlockSpec((1, sk, LANES), lambda b, g, i: (b, 0, g)),
                pl.BlockSpec((1,) + bias.shape[1:], lambda b, g, i: (g, 0, 0, 0))]
    args = [q, k, v, bias]
    if sink is not None:
        in_specs.append(pl.BlockSpec((1,) + sink.shape[1:], lambda b, g, i: (g, 0, 0)))
        args.append(sink)
    body = functools.partial(_band_body, n_past=n_past, win=win, n_qblk=n_qblk, std_bias=std_bias,
                             has_sink=sink is not None, whole_band=whole_band)
    return pl.pallas_call(
        body,
        grid=(bsz, n_groups, s // tq),
        in_specs=in_specs,
        out_specs=pl.BlockSpec((1, tq, qw), lambda b, g, i: (b, i, g)),
        out_shape=jax.ShapeDtypeStruct(q.shape, BF16),
        compiler_params=_params(3),
        name="band_attention",
    )(*args)


def _rope_tables(pos, rot, lane_pattern):
    half = rot // 2
    inv = jnp.power(ROPE_THETA, -jnp.arange(half, dtype=F32) * 2.0 / rot)
    ang = pos.astype(F32)[:, None] * inv[None, :]
    cos, sin = jnp.cos(ang), jnp.sin(ang)
    n = pos.shape[0]
    one, zero = jnp.ones((n, 1), F32), jnp.zeros((n, 1), F32)

    def build(first, second, fill):
        cols = []
        for kind, width in lane_pattern:
            if kind == "rot":
                cols += [first, second]
            else:
                cols.append(jnp.broadcast_to(fill, (n, width)))
        return jnp.concatenate(cols, axis=1)

    zeros_h = jnp.zeros_like(sin)
    return jnp.stack([build(cos, cos, one), build(zeros_h, sin, zero), build(-sin, zeros_h, zero)])


A_ROPE_PATTERN = (("pad", A_NOPE), ("rot", A_ROPE), ("pad", LANES - A_QK))
C_ROPE_PATTERN = (("rot", C_ROT), ("pad", HALF - C_ROT)) * 2


def _prep_ab(pre, post, w_in, q_norm, kv_norm, w_uq, w_ukv, rel_bias, w_out):
    d = w_in.shape[0]
    q_lat, c_kv, k_r, g_a, q_b, k_b, v_b, g_b = jnp.split(
        w_in, [384, 640, 672, 1184, 1696, 2208, 2720], axis=1)
    kr_blk = jnp.concatenate([jnp.zeros((d, A_NOPE), F32), k_r, jnp.zeros((d, LANES - A_QK), F32)], axis=1)
    w_in_p = jnp.concatenate([q_lat, c_kv, kr_blk, g_a, g_b, q_b, k_b, v_b], axis=1).astype(BF16)
    w_uq_p = jnp.pad(w_uq.reshape(A_Q_RANK, A_HEADS, A_QK), ((0, 0), (0, 0), (0, LANES - A_QK)))
    w_uq_p = w_uq_p.reshape(A_Q_RANK, A_HEADS * LANES).astype(BF16)
    ukv = w_ukv.reshape(A_KV_RANK, A_HEADS, A_NOPE + A_V)
    w_uk, w_uv = ukv[..., :A_NOPE], ukv[..., A_NOPE:]
    pad_half = ((0, 0), (0, 0), (0, LANES - A_NOPE))
    w_k = jnp.pad(w_uk, pad_half).reshape(A_KV_RANK, A_HEADS * LANES).astype(BF16)
    w_v = jnp.pad(w_uv, pad_half).reshape(A_KV_RANK, A_HEADS * LANES).astype(BF16)
    wkt = jnp.pad(jnp.transpose(w_uk, (1, 2, 0)), ((0, 0), (0, LANES - A_NOPE), (0, 0))).astype(BF16)
    eye = jnp.eye(A_HEADS, dtype=F32)
    wv_s = (jnp.transpose(w_uv, (1, 0, 2))[:, :, None, :] * eye[:, None, :, None]).reshape(
        A_HEADS, A_KV_RANK, A_WIDTH).astype(BF16)
    sel = (jnp.arange(LANES)[:, None] == A_NOPE + jnp.arange(A_ROPE)[None, :]).astype(BF16)
    win = (B_PAST_CHUNKS + 1) * CHUNK
    qi = jnp.arange(CHUNK)[:, None]
    kj = jnp.arange(win)[None, :]
    variants = []
    for c in range(B_PAST_CHUNKS + 1):
        rel = min(c, B_PAST_CHUNKS) * CHUNK + qi - kj
        bias = rel_bias[:, jnp.clip(rel, -B_MAX_REL, B_MAX_REL) + B_MAX_REL]
        if c < B_PAST_CHUNKS:
            bias = jnp.where(kj < (c + 1) * CHUNK, bias, NEG_INF)
        variants.append(bias.reshape(B_HEADS // 2, 2 * CHUNK, win))
    bias_all = jnp.stack(variants, axis=1)
    return dict(pre=pre[None], w_in=w_in_p, q_norm=q_norm[None], kv_norm=kv_norm[None], w_uq=w_uq_p, w_k=w_k,
                w_v=w_v, wkt=wkt, wv_s=wv_s, sel=sel, bias=bias_all, w_out=w_out.astype(BF16), post=post[None])


def _prep_c(pre, post, w_in, sinks, w_out):
    q, k, v, g = jnp.split(w_in, [1024, 1152, 1280], axis=1)
    k0, k1 = k[:, :C_HEAD_DIM], k[:, C_HEAD_DIM:]
    v0, v1 = v[:, :C_HEAD_DIM], v[:, C_HEAD_DIM:]
    w_in_p = jnp.concatenate([q, k0, k0, k1, k1, v0, v0, v1, v1, g], axis=1).astype(BF16)
    win = (C_PAST_CHUNKS + 1) * CHUNK
    kj = jnp.arange(win)[None, :]
    masks = [jnp.where(kj < (min(c, C_PAST_CHUNKS) + 1) * CHUNK, 0.0, NEG_INF).astype(F32)
             for c in range(C_PAST_CHUNKS + 1)]
    mask_all = jnp.broadcast_to(jnp.stack(masks)[None], (C_KV_HEADS, C_PAST_CHUNKS + 1, 1, win))
    sink_col = jnp.repeat(sinks.reshape(C_KV_HEADS, C_GROUP), CHUNK, axis=1)[..., None]
    return dict(pre=pre[None], w_in=w_in_p, mask=mask_all, sink=sink_col, w_out=w_out.astype(BF16), post=post[None])


def _dup_heads(x):
    return jnp.concatenate([x[:, :, 0], x[:, :, 0], x[:, :, 1], x[:, :, 1]], axis=-1)


def kernel(x_prompt, x_sample, cache_a_ckv, cache_a_krope, cache_b_k, cache_b_v, cache_c_k, cache_c_v,
           ab_pre_norm, ab_post_norm, ab_w_in, ab_q_norm, ab_kv_norm, ab_w_uq, ab_w_ukv, ab_rel_bias, ab_w_out,
           c_pre_norm, c_post_norm, c_w_in, c_sinks, c_w_out):
    bsz, seq, _ = x_prompt.shape
    dbs, dseq, _ = x_sample.shape
    past = cache_a_ckv.shape[2]
    n_s = dbs * dseq
    pos_p = jnp.arange(seq, dtype=jnp.int32)
    pos_s = jnp.tile(past + jnp.arange(dseq, dtype=jnp.int32), dbs)
    wab = _prep_ab(ab_pre_norm[0], ab_post_norm[0], ab_w_in[0], ab_q_norm[0], ab_kv_norm[0], ab_w_uq[0],
                   ab_w_ukv[0], ab_rel_bias[0], ab_w_out[0])
    wc = _prep_c(c_pre_norm[0], c_post_norm[0], c_w_in[0], c_sinks[0], c_w_out[0])
    b_tail = min(B_PAST_CHUNKS * CHUNK, seq)
    c_tail = min(C_WINDOW, seq)
    tile = 512

    rope_a_p = _rope_tables(pos_p, A_ROPE, A_ROPE_PATTERN)
    (qa, ka, va, qb, kb, vb, sg, c_new_p, kr_new_p, kb_tail, vb_tail) = _ab_in_proj(
        x_prompt, rope_a_p, wab, tm=256, tail_len=b_tail)
    o_a = _mla_prompt(qa, ka, va, tq=tile)
    o_b = _band_attention(qb, kb, vb, wab["bias"], None, n_groups=B_HEADS // 2, n_past=B_PAST_CHUNKS, n_qblk=1,
                          std_bias=True, tq=tile, whole_band=False)
    h1_p = _out_proj([o_a, o_b], sg, wab["w_out"], wab["post"], x_prompt, tm=tile)

    rope_a_s = _rope_tables(pos_s, A_ROPE, A_ROPE_PATTERN)
    xs = x_sample.reshape(1, n_s, D_MODEL)
    (qa_s, _, _, qb_s, kb_s, vb_s, sg_s, c_new_s, kr_new_s, kb_s32, vb_s32) = _ab_in_proj(
        xs, rope_a_s, wab, tm=n_s, tail_len=n_s)
    o_a_s = _mla_sample(qa_s.reshape(dbs, dseq, -1), cache_a_ckv[0], cache_a_krope[0],
                        c_new_s.reshape(dbs, dseq, -1), kr_new_s.reshape(dbs, dseq, -1),
                        wab["wkt"], wab["wv_s"], wab["sel"], tk=512)
    wb = cache_b_k.shape[2]
    kband = jnp.concatenate([cache_b_k[0].reshape(dbs, wb, B_WIDTH).astype(BF16), kb_s.reshape(dbs, dseq, -1)], 1)
    vband = jnp.concatenate([cache_b_v[0].reshape(dbs, wb, B_WIDTH).astype(BF16), vb_s.reshape(dbs, dseq, -1)], 1)
    o_b_s = _band_attention(qb_s.reshape(dbs, dseq, -1), kband, vband, wab["bias"], None, n_groups=B_HEADS // 2,
                            n_past=B_PAST_CHUNKS, n_qblk=1, std_bias=True, tq=dseq, whole_band=True)
    h1_s = _out_proj([o_a_s.reshape(1, n_s, -1), o_b_s.reshape(1, n_s, -1)], sg_s, wab["w_out"], wab["post"],
                     xs, tm=n_s)

    rope_c_p = _rope_tables(pos_p, C_ROT, C_ROPE_PATTERN)
    qc, kc, vc, sgc, kc_tail, vc_tail = _c_in_proj(h1_p, rope_c_p, wc, tm=tile, tail_len=c_tail)
    o_c = _band_attention(qc, kc, vc, wc["mask"], wc["sink"], n_groups=C_KV_HEADS, n_past=C_PAST_CHUNKS,
                          n_qblk=C_GROUP // 2, std_bias=False, tq=tile, whole_band=False)
    h2_p = _out_proj([o_c], sgc, wc["w_out"], wc["post"], h1_p, tm=tile)

    rope_c_s = _rope_tables(pos_s, C_ROT, C_ROPE_PATTERN)
    qc_s, kc_s, vc_s, sgc_s, kc_s32, vc_s32 = _c_in_proj(h1_s, rope_c_s, wc, tm=n_s, tail_len=n_s)
    kcb = jnp.concatenate([_dup_heads(cache_c_k[0]).astype(BF16), kc_s.reshape(dbs, dseq, -1)], axis=1)
    vcb = jnp.concatenate([_dup_heads(cache_c_v[0]).astype(BF16), vc_s.reshape(dbs, dseq, -1)], axis=1)
    o_c_s = _band_attention(qc_s.reshape(dbs, dseq, -1), kcb, vcb, wc["mask"], wc["sink"], n_groups=C_KV_HEADS,
                            n_past=C_PAST_CHUNKS, n_qblk=C_GROUP // 2, std_bias=False, tq=dseq, whole_band=True)
    h2_s = _out_proj([o_c_s.reshape(1, n_s, -1)], sgc_s, wc["w_out"], wc["post"], h1_s, tm=n_s)

    def roll_in(buf, new):
        return jnp.concatenate([buf, new], axis=1)[:, -buf.shape[1]:][None]

    return (h2_p, h2_s.reshape(dbs, dseq, D_MODEL),
            c_new_p[None], kr_new_p[None],
            kb_tail.reshape(1, bsz, b_tail, B_HEADS, B_HEAD_DIM), vb_tail.reshape(1, bsz, b_tail, B_HEADS, B_HEAD_DIM),
            kc_tail.reshape(1, bsz, c_tail, C_KV_HEADS, C_HEAD_DIM), vc_tail.reshape(1, bsz, c_tail, C_KV_HEADS, C_HEAD_DIM),
            c_new_s.reshape(1, dbs, dseq, A_KV_RANK), kr_new_s.reshape(1, dbs, dseq, A_ROPE),
            roll_in(cache_b_k[0], kb_s32.reshape(dbs, dseq, B_HEADS, B_HEAD_DIM)),
            roll_in(cache_b_v[0], vb_s32.reshape(dbs, dseq, B_HEADS, B_HEAD_DIM)),
            roll_in(cache_c_k[0], kc_s32.reshape(dbs, dseq, C_KV_HEADS, C_HEAD_DIM)),
            roll_in(cache_c_v[0], vc_s32.reshape(dbs, dseq, C_KV_HEADS, C_HEAD_DIM)))
```

```python
import functools

import jax
import jax.numpy as jnp
from jax import lax
from jax.experimental import pallas as pl
from jax.experimental.pallas import tpu as pltpu

F32 = jnp.float32
BF16 = jnp.bfloat16

D_MODEL = 1024
CHUNK = 64
ROPE_THETA = 500000.0
RMS_EPS = 1e-6
NEG_INF = -1e30

A_HEADS = 8
A_NOPE = 64
A_ROPE = 32
A_QK = A_NOPE + A_ROPE
A_V = 64
A_Q_RANK = 384
A_KV_RANK = 256
A_WIDTH = A_HEADS * A_V
A_SCALE = A_QK ** -0.5

B_HEADS = 8
B_HEAD_DIM = 64
B_WIDTH = B_HEADS * B_HEAD_DIM
B_PAST_CHUNKS = 8
B_MAX_REL = 128
B_SCALE = B_HEAD_DIM ** -0.5

C_HEADS = 16
C_KV_HEADS = 2
C_GROUP = C_HEADS // C_KV_HEADS
C_HEAD_DIM = 64
C_WIDTH = C_HEADS * C_HEAD_DIM
C_WINDOW = 128
C_PAST_CHUNKS = C_WINDOW // CHUNK
C_ROT = C_HEAD_DIM // 4
C_SCALE = C_HEAD_DIM ** -0.5

LANES = 128
HALF = LANES // 2
VMEM_LIMIT = 56 * 1024 * 1024

AB_Q0, AB_C0, AB_KR0, AB_G0, AB_QB0, AB_KB0, AB_VB0, AB_NZ = 0, 384, 640, 768, 1792, 2304, 2816, 3328
C_Q0, C_K0, C_V0, C_G0, C_NZ = 0, 1024, 1280, 1536, 2560


def _params(n_axes):
    return pltpu.CompilerParams(dimension_semantics=("arbitrary",) * n_axes, vmem_limit_bytes=VMEM_LIMIT)


def _rms(x, g):
    return x * lax.rsqrt(jnp.mean(x * x, axis=-1, keepdims=True) + RMS_EPS) * g


def _rope_block(blk, rope_ref, shift):
    return (blk * rope_ref[0] + pltpu.roll(blk, shift, 1) * rope_ref[1]
            + pltpu.roll(blk, LANES - shift, 1) * rope_ref[2])


def _tail_spec(n_tiles, tm, tail_len, width):
    tb = min(tail_len, tm)
    n_blk = tail_len // tb
    return tb, pl.BlockSpec((1, tb, width), lambda b, i: (b, jnp.maximum(i - (n_tiles - n_blk), 0), 0))


def _ab_in_body(x_ref, pre_ref, w_ref, qn_ref, kvn_ref, wuq_ref, wk_ref, wvt_ref, rope_ref,
                qa_ref, ka_ref, vt_ref, qb_ref, kb_ref, vb_ref, sg_ref, c_ref, kr_ref, kbt_ref, vbt_ref):
    tm = x_ref.shape[1]
    tb = kbt_ref.shape[1]
    xn = _rms(x_ref[0], pre_ref[...]).astype(BF16)
    z = jnp.dot(xn, w_ref[...], preferred_element_type=F32)

    qn = _rms(z[:, AB_Q0:AB_C0], qn_ref[...]).astype(BF16)
    qa = jnp.dot(qn, wuq_ref[...], preferred_element_type=F32) * A_SCALE
    for h in range(A_HEADS):
        blk = slice(h * LANES, (h + 1) * LANES)
        qa_ref[0, :, blk] = _rope_block(qa[:, blk], rope_ref, A_ROPE // 2).astype(BF16)

    c_new = _rms(z[:, AB_C0:AB_KR0], kvn_ref[...])
    c_ref[0] = c_new
    cb = c_new.astype(BF16)
    krot = _rope_block(z[:, AB_KR0:AB_G0], rope_ref, A_ROPE // 2)
    kr_ref[0] = krot[:, A_NOPE:A_NOPE + A_ROPE]
    kn = jnp.dot(cb, wk_ref[...], preferred_element_type=F32)
    for h in range(A_HEADS):
        blk = slice(h * LANES, (h + 1) * LANES)
        ka_ref[0, :, blk] = (kn[:, blk] + krot).astype(BF16)
    vt = jnp.dot(wvt_ref[...], c_new.T.astype(BF16), preferred_element_type=F32)
    row = lax.broadcasted_iota(jnp.int32, vt.shape, 0)
    vt_ref[0, 0] = jnp.where((row & (LANES - 1)) == A_V, 1.0, vt).astype(BF16)

    g = z[:, AB_G0:AB_QB0]
    sg_ref[0] = (g * jax.nn.sigmoid(g)).astype(BF16)
    qb_ref[0] = (z[:, AB_QB0:AB_KB0] * B_SCALE).astype(BF16)
    kb = z[:, AB_KB0:AB_VB0]
    vb = z[:, AB_VB0:AB_NZ]
    kb_ref[0] = kb.astype(BF16)
    vb_ref[0] = vb.astype(BF16)
    kbt_ref[0] = kb[tm - tb:, :]
    vbt_ref[0] = vb[tm - tb:, :]


def _ab_in_proj(x, rope, w, tm, tail_len, tkv):
    bsz, s, _ = x.shape
    nt = s // tm
    per_kv = tkv // tm
    tb, tail_spec = _tail_spec(nt, tm, tail_len, B_WIDTH)

    def full(a):
        return pl.BlockSpec(a.shape, lambda b, i: (0,) * a.ndim)

    def rows(width):
        return pl.BlockSpec((1, tm, width), lambda b, i: (b, i, 0))

    weights = (w["pre"], w["w_in"], w["q_norm"], w["kv_norm"], w["w_uq"], w["w_k"], w["w_vt"])
    vt_spec = pl.BlockSpec((1, 1, A_HEADS * LANES, tm), lambda b, i: (b, i // per_kv, 0, i % per_kv))
    out_shape = (
        jax.ShapeDtypeStruct((bsz, s, A_HEADS * LANES), BF16),
        jax.ShapeDtypeStruct((bsz, s, A_HEADS * LANES), BF16),
        jax.ShapeDtypeStruct((bsz, s // tkv, A_HEADS * LANES, tkv), BF16),
        jax.ShapeDtypeStruct((bsz, s, B_WIDTH), BF16),
        jax.ShapeDtypeStruct((bsz, s, B_WIDTH), BF16),
        jax.ShapeDtypeStruct((bsz, s, B_WIDTH), BF16),
        jax.ShapeDtypeStruct((bsz, s, A_WIDTH + B_WIDTH), BF16),
        jax.ShapeDtypeStruct((bsz, s, A_KV_RANK), F32),
        jax.ShapeDtypeStruct((bsz, s, A_ROPE), F32),
        jax.ShapeDtypeStruct((bsz, tail_len, B_WIDTH), F32),
        jax.ShapeDtypeStruct((bsz, tail_len, B_WIDTH), F32),
    )
    out_specs = (rows(1024), rows(1024), vt_spec, rows(512), rows(512), rows(512), rows(1024),
                 rows(A_KV_RANK), rows(A_ROPE), tail_spec, tail_spec)
    return pl.pallas_call(
        _ab_in_body,
        grid=(bsz, nt),
        in_specs=[rows(D_MODEL)] + [full(a) for a in weights]
        + [pl.BlockSpec((3, tm, LANES), lambda b, i: (0, i, 0))],
        out_specs=out_specs,
        out_shape=out_shape,
        compiler_params=_params(2),
        name="ab_in_proj",
    )(x, *weights, rope)


def _c_in_body(x_ref, pre_ref, w_ref, rope_ref, q_ref, k_ref, v_ref, sg_ref, kt_ref, vt_ref):
    tm = x_ref.shape[1]
    tb = kt_ref.shape[1]
    xn = _rms(x_ref[0], pre_ref[...]).astype(BF16)
    z = jnp.dot(xn, w_ref[...], preferred_element_type=F32)
    half_rot = C_ROT // 2
    for j in range(C_WIDTH // LANES):
        blk = slice(C_Q0 + j * LANES, C_Q0 + (j + 1) * LANES)
        q_ref[0, :, j * LANES:(j + 1) * LANES] = (_rope_block(z[:, blk], rope_ref, half_rot) * C_SCALE).astype(BF16)
    kd = [_rope_block(z[:, C_K0 + j * LANES:C_K0 + (j + 1) * LANES], rope_ref, half_rot) for j in range(C_KV_HEADS)]
    vd = [z[:, C_V0 + j * LANES:C_V0 + (j + 1) * LANES] for j in range(C_KV_HEADS)]
    for j in range(C_KV_HEADS):
        k_ref[0, :, j * LANES:(j + 1) * LANES] = kd[j].astype(BF16)
        v_ref[0, :, j * LANES:(j + 1) * LANES] = vd[j].astype(BF16)
    g = z[:, C_G0:C_NZ]
    sg_ref[0] = (g * jax.nn.sigmoid(g)).astype(BF16)
    lo = lax.broadcasted_iota(jnp.int32, (1, LANES), 1) < HALF
    kt_ref[0] = jnp.where(lo, kd[0], kd[1])[tm - tb:, :]
    vt_ref[0] = jnp.where(lo, vd[0], vd[1])[tm - tb:, :]


def _c_in_proj(x, rope, w, tm, tail_len):
    bsz, s, _ = x.shape
    nt = s // tm
    tb, tail_spec = _tail_spec(nt, tm, tail_len, LANES)

    def full(a):
        return pl.BlockSpec(a.shape, lambda b, i: (0,) * a.ndim)

    def rows(width):
        return pl.BlockSpec((1, tm, width), lambda b, i: (b, i, 0))

    out_shape = (
        jax.ShapeDtypeStruct((bsz, s, C_WIDTH), BF16),
        jax.ShapeDtypeStruct((bsz, s, C_KV_HEADS * LANES), BF16),
        jax.ShapeDtypeStruct((bsz, s, C_KV_HEADS * LANES), BF16),
        jax.ShapeDtypeStruct((bsz, s, C_WIDTH), BF16),
        jax.ShapeDtypeStruct((bsz, tail_len, LANES), F32),
        jax.ShapeDtypeStruct((bsz, tail_len, LANES), F32),
    )
    return pl.pallas_call(
        _c_in_body,
        grid=(bsz, nt),
        in_specs=[rows(D_MODEL), full(w["pre"]), full(w["w_in"]),
                  pl.BlockSpec((3, tm, LANES), lambda b, i: (0, i, 0))],
        out_specs=(rows(C_WIDTH), rows(256), rows(256), rows(C_WIDTH), tail_spec, tail_spec),
        out_shape=out_shape,
        compiler_params=_params(2),
        name="c_in_proj",
    )(x, w["pre"], w["w_in"], rope)


def _out_body(*refs):
    o_refs, (sg_ref, w_ref, g_ref, h_ref, out_ref) = refs[:-5], refs[-5:]
    o = jnp.concatenate([r[0].astype(F32) for r in o_refs], axis=-1) if len(o_refs) > 1 else o_refs[0][0].astype(F32)
    mixed = (o * sg_ref[0].astype(F32)).astype(BF16)
    y = jnp.dot(mixed, w_ref[...], preferred_element_type=F32)
    out_ref[0] = h_ref[0] + _rms(y, g_ref[...])


def _out_proj(o_parts, sg, w_out, post_g, h, tm):
    bsz, s, _ = h.shape

    def rows(width):
        return pl.BlockSpec((1, tm, width), lambda b, i: (b, i, 0))

    def full(a):
        return pl.BlockSpec(a.shape, lambda b, i: (0,) * a.ndim)

    return pl.pallas_call(
        _out_body,
        grid=(bsz, s // tm),
        in_specs=[rows(o.shape[-1]) for o in o_parts] + [rows(D_MODEL), full(w_out), full(post_g), rows(D_MODEL)],
        out_specs=rows(D_MODEL),
        out_shape=jax.ShapeDtypeStruct(h.shape, F32),
        compiler_params=_params(2),
        name="out_proj",
    )(*o_parts, sg, w_out, post_g, h)


def _mla_prompt_body(q_ref, k_ref, vt_ref, o_ref, m_sc, acc_sc):
    tq = q_ref.shape[1]
    qi = pl.program_id(2)
    m_sc[...] = jnp.full(m_sc.shape, NEG_INF, F32)
    acc_sc[...] = jnp.zeros(acc_sc.shape, F32)

    def step(j, masked):
        start = pl.multiple_of(j * tq, tq)
        for hh in range(2):
            blk = slice(hh * LANES, (hh + 1) * LANES)
            q = q_ref[0, :, blk]
            k = k_ref[0, pl.ds(start, tq), blk]
            vt = vt_ref[0, j, blk, :]
            s = lax.dot_general(k, q, (((1,), (1,)), ((), ())), preferred_element_type=F32)
            if masked:
                kc = lax.broadcasted_iota(jnp.int32, (tq, tq), 0) // CHUNK
                qc = lax.broadcasted_iota(jnp.int32, (tq, tq), 1) // CHUNK
                s = jnp.where(kc <= qc, s, NEG_INF)
            m_old = m_sc[hh]
            m_new = jnp.maximum(m_old, jnp.max(s, axis=0, keepdims=True))
            p = jnp.exp(s - m_new).astype(BF16)
            acc_sc[hh] = jnp.exp(m_old - m_new) * acc_sc[hh] + jnp.dot(vt, p, preferred_element_type=F32)
            m_sc[hh] = m_new

    def loop_body(j, carry):
        step(j, False)
        return carry

    lax.fori_loop(0, qi, loop_body, 0)
    step(qi, True)

    outs = []
    for hh in range(2):
        acc = acc_sc[hh]
        outs.append((acc / acc[A_V:A_V + 1, :]).T)
    lo = lax.broadcasted_iota(jnp.int32, (1, LANES), 1) < HALF
    o_ref[0] = jnp.where(lo, outs[0], pltpu.roll(outs[1], HALF, 1)).astype(o_ref.dtype)


def _mla_prompt(qa, ka, vt, tq):
    bsz, s, _ = qa.shape
    pairs = A_HEADS // 2
    assert vt.shape[3] == tq
    return pl.pallas_call(
        _mla_prompt_body,
        grid=(bsz, pairs, s // tq),
        in_specs=[pl.BlockSpec((1, tq, 2 * LANES), lambda b, h, i: (b, i, h)),
                  pl.BlockSpec((1, s, 2 * LANES), lambda b, h, i: (b, 0, h)),
                  pl.BlockSpec((1, s // tq, 2 * LANES, tq), lambda b, h, i: (b, 0, h, 0))],
        out_specs=pl.BlockSpec((1, tq, LANES), lambda b, h, i: (b, i, h)),
        out_shape=jax.ShapeDtypeStruct((bsz, s, A_WIDTH), BF16),
        scratch_shapes=[pltpu.VMEM((2, 1, tq), F32), pltpu.VMEM((2, LANES, tq), F32)],
        compiler_params=_params(3),
        name="mla_prompt",
    )(qa, ka, vt)


def _mla_sample_body(q_ref, cc_ref, ckr_ref, cn_ref, krn_ref, wkt_ref, wv_ref, sel_ref, o_ref,
                     qabs_sc, qr_sc, m_sc, l_sc, acc_sc, *, tk):
    t = q_ref.shape[1]
    past = cc_ref.shape[1]
    for h in range(A_HEADS):
        qh = q_ref[0, :, h * LANES:(h + 1) * LANES]
        rows = slice(h * t, (h + 1) * t)
        qabs_sc[rows, :] = jnp.dot(qh, wkt_ref[h], preferred_element_type=F32).astype(BF16)
        qr_sc[rows, :] = jnp.dot(qh, sel_ref[...], preferred_element_type=F32).astype(BF16)
    m_sc[...] = jnp.full(m_sc.shape, NEG_INF, F32)
    l_sc[...] = jnp.zeros(l_sc.shape, F32)
    acc_sc[...] = jnp.zeros(acc_sc.shape, F32)
    contract_last = (((1,), (1,)), ((), ()))

    def update(c_t, kr_t):
        s = (lax.dot_general(qabs_sc[...], c_t, contract_last, preferred_element_type=F32)
             + lax.dot_general(qr_sc[...], kr_t, contract_last, preferred_element_type=F32))
        m_old = m_sc[...]
        m_new = jnp.maximum(m_old, jnp.max(s, axis=-1, keepdims=True))
        p = jnp.exp(s - m_new)
        alpha = jnp.exp(m_old - m_new)
        l_sc[...] = alpha * l_sc[...] + jnp.sum(p, axis=-1, keepdims=True)
        acc_sc[...] = alpha * acc_sc[...] + jnp.dot(p.astype(BF16), c_t, preferred_element_type=F32)
        m_sc[...] = m_new

    def loop_body(j, carry):
        start = pl.multiple_of(j * tk, tk)
        update(cc_ref[0, pl.ds(start, tk), :].astype(BF16), ckr_ref[0, pl.ds(start, tk), :].astype(BF16))
        return carry

    lax.fori_loop(0, past // tk, loop_body, 0)
    update(cn_ref[0].astype(BF16), krn_ref[0].astype(BF16))

    o_lat = (acc_sc[...] / l_sc[...]).astype(BF16)
    out = jnp.zeros((t, A_WIDTH), F32)
    for h in range(A_HEADS):
        out = out + jnp.dot(o_lat[h * t:(h + 1) * t, :], wv_ref[h], preferred_element_type=F32)
    o_ref[0] = out.astype(o_ref.dtype)


def _mla_sample(qa, cache_c, cache_kr, c_new, kr_new, wkt, wv, sel, tk):
    bsz, t, _ = qa.shape
    past = cache_c.shape[1]

    def per_b(shape):
        return pl.BlockSpec((1,) + shape, lambda b: (b, 0, 0))

    def full(a):
        return pl.BlockSpec(a.shape, lambda b: (0,) * a.ndim)

    rows = A_HEADS * t
    return pl.pallas_call(
        functools.partial(_mla_sample_body, tk=tk),
        grid=(bsz,),
        in_specs=[per_b((t, A_HEADS * LANES)), per_b((past, A_KV_RANK)), per_b((past, A_ROPE)),
                  per_b((t, A_KV_RANK)), per_b((t, A_ROPE)), full(wkt), full(wv), full(sel)],
        out_specs=per_b((t, A_WIDTH)),
        out_shape=jax.ShapeDtypeStruct((bsz, t, A_WIDTH), BF16),
        scratch_shapes=[pltpu.VMEM((rows, A_KV_RANK), BF16), pltpu.VMEM((rows, A_ROPE), BF16),
                        pltpu.VMEM((rows, 1), F32), pltpu.VMEM((rows, 1), F32),
                        pltpu.VMEM((rows, A_KV_RANK), F32)],
        compiler_params=_params(1),
        name="mla_sample",
    )(qa, cache_c, cache_kr, c_new, kr_new, wkt, wv, sel)


def _band_body(*refs, n_past, win, n_qblk, std_bias, has_sink, whole_band):
    if has_sink:
        q_ref, k_ref, v_ref, bias_ref, sink_ref, o_ref = refs
    else:
        q_ref, k_ref, v_ref, bias_ref, o_ref = refs
    tq = q_ref.shape[1]
    t = pl.program_id(2)
    lo = lax.broadcasted_iota(jnp.int32, (CHUNK, LANES), 1) < HALF
    contract_last = (((1,), (1,)), ((), ()))

    def do_chunk(c, start, variant):
        rows = slice(c * CHUNK, (c + 1) * CHUNK)
        pieces = []
        for r in range(n_qblk):
            qblk = q_ref[0, rows, r * LANES:(r + 1) * LANES]
            zero = jnp.zeros_like(qblk)
            pieces += [jnp.where(lo, qblk, zero), jnp.where(lo, zero, qblk)]
        qs = jnp.concatenate(pieces, axis=0)
        kw = k_ref[0, pl.ds(start, win), :]
        vw = v_ref[0, pl.ds(start, win), :]
        s = lax.dot_general(qs, kw, contract_last, preferred_element_type=F32)
        if std_bias or variant != n_past:
            s = s + bias_ref[0, variant]
        m = jnp.max(s, axis=-1, keepdims=True)
        if has_sink:
            sink = sink_ref[0]
            m = jnp.maximum(m, sink)
        p = jnp.exp(s - m)
        l = jnp.sum(p, axis=-1, keepdims=True)
        if has_sink:
            l = l + jnp.exp(sink - m)
        o = jnp.dot(p.astype(BF16), vw, preferred_element_type=F32) / l
        for r in range(n_qblk):
            top = o[(2 * r) * CHUNK:(2 * r + 1) * CHUNK]
            bot = o[(2 * r + 1) * CHUNK:(2 * r + 2) * CHUNK]
            o_ref[0, rows, r * LANES:(r + 1) * LANES] = jnp.where(lo, top, bot).astype(o_ref.dtype)

    n_chunks = tq // CHUNK
    if whole_band:
        do_chunk(0, 0, n_past)
        return

    @pl.when(t == 0)
    def _():
        for c in range(n_chunks):
            do_chunk(c, max(c - n_past, 0) * CHUNK, min(c, n_past))

    @pl.when(t > 0)
    def _():
        for c in range(n_chunks):
            do_chunk(c, pl.multiple_of(t * tq + (c - n_past) * CHUNK, CHUNK), n_past)


def _band_attention(q, k, v, bias, sink, *, n_groups, n_past, n_qblk, std_bias, tq, whole_band):
    bsz, s, _ = q.shape
    sk = k.shape[1]
    win = (n_past + 1) * CHUNK
    qw = n_qblk * LANES
    in_specs = [pl.BlockSpec((1, tq, qw), lambda b, g, i: (b, i, g)),
                pl.BlockSpec((1, sk, LANES), lambda b, g, i: (b, 0, g)),
                pl.BlockSpec((1, sk, LANES), lambda b, g, i: (b, 0, g)),
                pl.BlockSpec((1,) + bias.shape[1:], lambda b, g, i: (g, 0, 0, 0))]
    args = [q, k, v, bias]
    if sink is not None:
        in_specs.append(pl.BlockSpec((1,) + sink.shape[1:], lambda b, g, i: (g, 0, 0)))
        args.append(sink)
    body = functools.partial(_band_body, n_past=n_past, win=win, n_qblk=n_qblk, std_bias=std_bias,
                             has_sink=sink is not None, whole_band=whole_band)
    return pl.pallas_call(
        body,
        grid=(bsz, n_groups, s // tq),
        in_specs=in_specs,
        out_specs=pl.BlockSpec((1, tq, qw), lambda b, g, i: (b, i, g)),
        out_shape=jax.ShapeDtypeStruct(q.shape, BF16),
        compiler_params=_params(3),
        name="band_attention",
    )(*args)


def _rope_tables(pos, rot, lane_pattern):
    half = rot // 2
    inv = jnp.power(ROPE_THETA, -jnp.arange(half, dtype=F32) * 2.0 / rot)
    ang = pos.astype(F32)[:, None] * inv[None, :]
    cos, sin = jnp.cos(ang), jnp.sin(ang)
    n = pos.shape[0]
    one, zero = jnp.ones((n, 1), F32), jnp.zeros((n, 1), F32)

    def build(first, second, fill):
        cols = []
        for kind, width in lane_pattern:
            if kind == "rot":
                cols += [first, second]
            else:
                cols.append(jnp.broadcast_to(fill, (n, width)))
        return jnp.concatenate(cols, axis=1)

    zeros_h = jnp.zeros_like(sin)
    return jnp.stack([build(cos, cos, one), build(zeros_h, sin, zero), build(-sin, zeros_h, zero)])


A_ROPE_PATTERN = (("pad", A_NOPE), ("rot", A_ROPE), ("pad", LANES - A_QK))
C_ROPE_PATTERN = (("rot", C_ROT), ("pad", HALF - C_ROT)) * 2


def _prep_ab(pre, post, w_in, q_norm, kv_norm, w_uq, w_ukv, rel_bias, w_out):
    d = w_in.shape[0]
    q_lat, c_kv, k_r, g_a, q_b, k_b, v_b, g_b = jnp.split(
        w_in, [384, 640, 672, 1184, 1696, 2208, 2720], axis=1)
    kr_blk = jnp.concatenate([jnp.zeros((d, A_NOPE), F32), k_r, jnp.zeros((d, LANES - A_QK), F32)], axis=1)
    w_in_p = jnp.concatenate([q_lat, c_kv, kr_blk, g_a, g_b, q_b, k_b, v_b], axis=1).astype(BF16)
    w_uq_p = jnp.pad(w_uq.reshape(A_Q_RANK, A_HEADS, A_QK), ((0, 0), (0, 0), (0, LANES - A_QK)))
    w_uq_p = w_uq_p.reshape(A_Q_RANK, A_HEADS * LANES).astype(BF16)
    ukv = w_ukv.reshape(A_KV_RANK, A_HEADS, A_NOPE + A_V)
    w_uk, w_uv = ukv[..., :A_NOPE], ukv[..., A_NOPE:]
    pad_half = ((0, 0), (0, 0), (0, LANES - A_NOPE))
    w_k = jnp.pad(w_uk, pad_half).reshape(A_KV_RANK, A_HEADS * LANES).astype(BF16)
    w_vt = jnp.pad(w_uv, pad_half).reshape(A_KV_RANK, A_HEADS * LANES).T.astype(BF16)
    wkt = jnp.pad(jnp.transpose(w_uk, (1, 2, 0)), ((0, 0), (0, LANES - A_NOPE), (0, 0))).astype(BF16)
    eye = jnp.eye(A_HEADS, dtype=F32)
    wv_s = (jnp.transpose(w_uv, (1, 0, 2))[:, :, None, :] * eye[:, None, :, None]).reshape(
        A_HEADS, A_KV_RANK, A_WIDTH).astype(BF16)
    sel = (jnp.arange(LANES)[:, None] == A_NOPE + jnp.arange(A_ROPE)[None, :]).astype(BF16)
    band = B_PAST_CHUNKS * CHUNK
    win = band + CHUNK
    top_rel = band + CHUNK - 1
    n_vec = 2 * band + 2 * CHUNK
    n_hi = top_rel - B_MAX_REL
    n_lo = n_vec - n_hi - (2 * B_MAX_REL + 1)
    vec = jnp.concatenate([jnp.broadcast_to(rel_bias[:, -1:], (B_HEADS, n_hi)), rel_bias[:, ::-1],
                           jnp.broadcast_to(rel_bias[:, :1], (B_HEADS, n_lo))], axis=1)
    skew = jnp.tile(vec, (1, CHUNK))[:, :CHUNK * (n_vec - 1)].reshape(B_HEADS, CHUNK, n_vec - 1)
    toep = skew[:, :, CHUNK - 1:]
    kj = jnp.arange(win)[None, None, :]
    variants = []
    for c in range(B_PAST_CHUNKS + 1):
        bias = toep[:, :, band - c * CHUNK:band - c * CHUNK + win]
        if c < B_PAST_CHUNKS:
            bias = jnp.where(kj < (c + 1) * CHUNK, bias, NEG_INF)
        variants.append(bias.reshape(B_HEADS // 2, 2 * CHUNK, win))
    bias_all = jnp.stack(variants, axis=1)
    return dict(pre=pre[None], w_in=w_in_p, q_norm=q_norm[None], kv_norm=kv_norm[None], w_uq=w_uq_p, w_k=w_k,
                w_vt=w_vt, wkt=wkt, wv_s=wv_s, sel=sel, bias=bias_all, w_out=w_out.astype(BF16), post=post[None])


def _prep_c(pre, post, w_in, sinks, w_out):
    q, k, v, g = jnp.split(w_in, [1024, 1152, 1280], axis=1)
    k0, k1 = k[:, :C_HEAD_DIM], k[:, C_HEAD_DIM:]
    v0, v1 = v[:, :C_HEAD_DIM], v[:, C_HEAD_DIM:]
    w_in_p = jnp.concatenate([q, k0, k0, k1, k1, v0, v0, v1, v1, g], axis=1).astype(BF16)
    win = (C_PAST_CHUNKS + 1) * CHUNK
    kj = jnp.arange(win)[None, :]
    masks = [jnp.where(kj < (min(c, C_PAST_CHUNKS) + 1) * CHUNK, 0.0, NEG_INF).astype(F32)
             for c in range(C_PAST_CHUNKS + 1)]
    mask_all = jnp.broadcast_to(jnp.stack(masks)[None], (C_KV_HEADS, C_PAST_CHUNKS + 1, 1, win))
    sink_col = jnp.repeat(sinks.reshape(C_KV_HEADS, C_GROUP), CHUNK, axis=1)[..., None]
    return dict(pre=pre[None], w_in=w_in_p, mask=mask_all, sink=sink_col, w_out=w_out.astype(BF16), post=post[None])


def _dup_heads(x):
    return jnp.concatenate([x[:, :, 0], x[:, :, 0], x[:, :, 1], x[:, :, 1]], axis=-1)


def kernel(x_prompt, x_sample, cache_a_ckv, cache_a_krope, cache_b_k, cache_b_v, cache_c_k, cache_c_v,
           ab_pre_norm, ab_post_norm, ab_w_in, ab_q_norm, ab_kv_norm, ab_w_uq, ab_w_ukv, ab_rel_bias, ab_w_out,
           c_pre_norm, c_post_norm, c_w_in, c_sinks, c_w_out):
    bsz, seq, _ = x_prompt.shape
    dbs, dseq, _ = x_sample.shape
    past = cache_a_ckv.shape[2]
    n_s = dbs * dseq
    pos_p = jnp.arange(seq, dtype=jnp.int32)
    pos_s = jnp.tile(past + jnp.arange(dseq, dtype=jnp.int32), dbs)
    wab = _prep_ab(ab_pre_norm[0], ab_post_norm[0], ab_w_in[0], ab_q_norm[0], ab_kv_norm[0], ab_w_uq[0],
                   ab_w_ukv[0], ab_rel_bias[0], ab_w_out[0])
    wc = _prep_c(c_pre_norm[0], c_post_norm[0], c_w_in[0], c_sinks[0], c_w_out[0])
    b_tail = min(B_PAST_CHUNKS * CHUNK, seq)
    c_tail = min(C_WINDOW, seq)
    tile = 512

    rope_a_p = _rope_tables(pos_p, A_ROPE, A_ROPE_PATTERN)
    (qa, ka, vt, qb, kb, vb, sg, c_new_p, kr_new_p, kb_tail, vb_tail) = _ab_in_proj(
        x_prompt, rope_a_p, wab, tm=256, tail_len=b_tail, tkv=tile)
    o_a = _mla_prompt(qa, ka, vt, tq=tile)
    o_b = _band_attention(qb, kb, vb, wab["bias"], None, n_groups=B_HEADS // 2, n_past=B_PAST_CHUNKS, n_qblk=1,
                          std_bias=True, tq=tile, whole_band=False)
    h1_p = _out_proj([o_a, o_b], sg, wab["w_out"], wab["post"], x_prompt, tm=tile)

    rope_a_s = _rope_tables(pos_s, A_ROPE, A_ROPE_PATTERN)
    xs = x_sample.reshape(1, n_s, D_MODEL)
    (qa_s, _, _, qb_s, kb_s, vb_s, sg_s, c_new_s, kr_new_s, kb_s32, vb_s32) = _ab_in_proj(
        xs, rope_a_s, wab, tm=n_s, tail_len=n_s, tkv=n_s)
    o_a_s = _mla_sample(qa_s.reshape(dbs, dseq, -1), cache_a_ckv[0], cache_a_krope[0],
                        c_new_s.reshape(dbs, dseq, -1), kr_new_s.reshape(dbs, dseq, -1),
                        wab["wkt"], wab["wv_s"], wab["sel"], tk=512)
    wb = cache_b_k.shape[2]
    kband = jnp.concatenate([cache_b_k[0].reshape(dbs, wb, B_WIDTH).astype(BF16), kb_s.reshape(dbs, dseq, -1)], 1)
    vband = jnp.concatenate([cache_b_v[0].reshape(dbs, wb, B_WIDTH).astype(BF16), vb_s.reshape(dbs, dseq, -1)], 1)
    o_b_s = _band_attention(qb_s.reshape(dbs, dseq, -1), kband, vband, wab["bias"], None, n_groups=B_HEADS // 2,
                            n_past=B_PAST_CHUNKS, n_qblk=1, std_bias=True, tq=dseq, whole_band=True)
    h1_s = _out_proj([o_a_s.reshape(1, n_s, -1), o_b_s.reshape(1, n_s, -1)], sg_s, wab["w_out"], wab["post"],
                     xs, tm=n_s)

    rope_c_p = _rope_tables(pos_p, C_ROT, C_ROPE_PATTERN)
    qc, kc, vc, sgc, kc_tail, vc_tail = _c_in_proj(h1_p, rope_c_p, wc, tm=tile, tail_len=c_tail)
    o_c = _band_attention(qc, kc, vc, wc["mask"], wc["sink"], n_groups=C_KV_HEADS, n_past=C_PAST_CHUNKS,
                          n_qblk=C_GROUP // 2, std_bias=False, tq=tile, whole_band=False)
    h2_p = _out_proj([o_c], sgc, wc["w_out"], wc["post"], h1_p, tm=tile)

    rope_c_s = _rope_tables(pos_s, C_ROT, C_ROPE_PATTERN)
    qc_s, kc_s, vc_s, sgc_s, kc_s32, vc_s32 = _c_in_proj(h1_s, rope_c_s, wc, tm=n_s, tail_len=n_s)
    kcb = jnp.concatenate([_dup_heads(cache_c_k[0]).astype(BF16), kc_s.reshape(dbs, dseq, -1)], axis=1)
    vcb = jnp.concatenate([_dup_heads(cache_c_v[0]).astype(BF16), vc_s.reshape(dbs, dseq, -1)], axis=1)
    o_c_s = _band_attention(qc_s.reshape(dbs, dseq, -1), kcb, vcb, wc["mask"], wc["sink"], n_groups=C_KV_HEADS,
                            n_past=C_PAST_CHUNKS, n_qblk=C_GROUP // 2, std_bias=False, tq=dseq, whole_band=True)
    h2_s = _out_proj([o_c_s.reshape(1, n_s, -1)], sgc_s, wc["w_out"], wc["post"], h1_s, tm=n_s)

    def roll_in(buf, new):
        return jnp.concatenate([buf, new], axis=1)[:, -buf.shape[1]:][None]

    return (h2_p, h2_s.reshape(dbs, dseq, D_MODEL),
            c_new_p[None], kr_new_p[None],
            kb_tail.reshape(1, bsz, b_tail, B_HEADS, B_HEAD_DIM), vb_tail.reshape(1, bsz, b_tail, B_HEADS, B_HEAD_DIM),
            kc_tail.reshape(1, bsz, c_tail, C_KV_HEADS, C_HEAD_DIM), vc_tail.reshape(1, bsz, c_tail, C_KV_HEADS, C_HEAD_DIM),
            c_new_s.reshape(1, dbs, dseq, A_KV_RANK), kr_new_s.reshape(1, dbs, dseq, A_ROPE),
            roll_in(cache_b_k[0], kb_s32.reshape(dbs, dseq, B_HEADS, B_HEAD_DIM)),
            roll_in(cache_b_v[0], vb_s32.reshape(dbs, dseq, B_HEADS, B_HEAD_DIM)),
            roll_in(cache_c_k[0], kc_s32.reshape(dbs, dseq, C_KV_HEADS, C_HEAD_DIM)),
            roll_in(cache_c_v[0], vc_s32.reshape(dbs, dseq, C_KV_HEADS, C_HEAD_DIM)))
```

```python
import functools

import jax
import jax.numpy as jnp
from jax import lax
from jax.experimental import pallas as pl
from jax.experimental.pallas import tpu as pltpu

F32 = jnp.float32
BF16 = jnp.bfloat16

D_MODEL = 1024
CHUNK = 64
ROPE_THETA = 500000.0
RMS_EPS = 1e-6
NEG_INF = -1e30

A_HEADS = 8
A_NOPE = 64
A_ROPE = 32
A_QK = A_NOPE + A_ROPE
A_V = 64
A_Q_RANK = 384
A_KV_RANK = 256
A_WIDTH = A_HEADS * A_V
A_SCALE = A_QK ** -0.5

B_HEADS = 8
B_HEAD_DIM = 64
B_WIDTH = B_HEADS * B_HEAD_DIM
B_PAST_CHUNKS = 8
B_MAX_REL = 128
B_SCALE = B_HEAD_DIM ** -0.5

C_HEADS = 16
C_KV_HEADS = 2
C_GROUP = C_HEADS // C_KV_HEADS
C_HEAD_DIM = 64
C_WIDTH = C_HEADS * C_HEAD_DIM
C_WINDOW = 128
C_PAST_CHUNKS = C_WINDOW // CHUNK
C_ROT = C_HEAD_DIM // 4
C_SCALE = C_HEAD_DIM ** -0.5

LANES = 128
HALF = LANES // 2
VMEM_LIMIT = 56 * 1024 * 1024

AB_Q0, AB_C0, AB_KR0, AB_G0, AB_QB0, AB_KB0, AB_VB0, AB_NZ = 0, 384, 640, 768, 1792, 2304, 2816, 3328
C_Q0, C_K0, C_V0, C_G0, C_NZ = 0, 1024, 1280, 1536, 2560


def _params(n_axes):
    return pltpu.CompilerParams(dimension_semantics=("arbitrary",) * n_axes, vmem_limit_bytes=VMEM_LIMIT)


def _rms(x, g):
    return x * lax.rsqrt(jnp.mean(x * x, axis=-1, keepdims=True) + RMS_EPS) * g


def _rope_block(blk, rope_ref, shift):
    return (blk * rope_ref[0] + pltpu.roll(blk, shift, 1) * rope_ref[1]
            + pltpu.roll(blk, LANES - shift, 1) * rope_ref[2])


def _tail_spec(n_tiles, tm, tail_len, width):
    tb = min(tail_len, tm)
    n_blk = tail_len // tb
    return tb, pl.BlockSpec((1, tb, width), lambda b, i: (b, jnp.maximum(i - (n_tiles - n_blk), 0), 0))


def _ab_in_body(x_ref, pre_ref, w_ref, qn_ref, kvn_ref, wuq_ref, wk_ref, wvt_ref, rope_ref,
                qa_ref, ka_ref, vt_ref, qb_ref, kb_ref, vb_ref, sg_ref, c_ref, kr_ref, kbt_ref, vbt_ref):
    tm = x_ref.shape[1]
    tb = kbt_ref.shape[1]
    xn = _rms(x_ref[0], pre_ref[...]).astype(BF16)
    z = jnp.dot(xn, w_ref[...], preferred_element_type=F32)

    qn = _rms(z[:, AB_Q0:AB_C0], qn_ref[...]).astype(BF16)
    qa = jnp.dot(qn, wuq_ref[...], preferred_element_type=F32) * A_SCALE
    for h in range(A_HEADS):
        blk = slice(h * LANES, (h + 1) * LANES)
        qa_ref[0, :, blk] = _rope_block(qa[:, blk], rope_ref, A_ROPE // 2).astype(BF16)

    c_new = _rms(z[:, AB_C0:AB_KR0], kvn_ref[...])
    c_ref[0] = c_new
    cb = c_new.astype(BF16)
    krot = _rope_block(z[:, AB_KR0:AB_G0], rope_ref, A_ROPE // 2)
    kr_ref[0] = krot[:, A_NOPE:A_NOPE + A_ROPE]
    kn = jnp.dot(cb, wk_ref[...], preferred_element_type=F32)
    for h in range(A_HEADS):
        blk = slice(h * LANES, (h + 1) * LANES)
        ka_ref[0, :, blk] = (kn[:, blk] + krot).astype(BF16)
    vt = jnp.dot(wvt_ref[...], c_new.T.astype(BF16), preferred_element_type=F32)
    row = lax.broadcasted_iota(jnp.int32, vt.shape, 0)
    vt_ref[0, 0] = jnp.where((row & (LANES - 1)) == A_V, 1.0, vt).astype(BF16)

    g = z[:, AB_G0:AB_QB0]
    sg_ref[0] = (g * jax.nn.sigmoid(g)).astype(BF16)
    qb_ref[0] = (z[:, AB_QB0:AB_KB0] * B_SCALE).astype(BF16)
    kb = z[:, AB_KB0:AB_VB0]
    vb = z[:, AB_VB0:AB_NZ]
    kb_ref[0] = kb.astype(BF16)
    vb_ref[0] = vb.astype(BF16)
    kbt_ref[0] = kb[tm - tb:, :]
    vbt_ref[0] = vb[tm - tb:, :]


def _ab_in_proj(x, rope, w, tm, tail_len, tkv):
    bsz, s, _ = x.shape
    nt = s // tm
    per_kv = tkv // tm
    tb, tail_spec = _tail_spec(nt, tm, tail_len, B_WIDTH)

    def full(a):
        return pl.BlockSpec(a.shape, lambda b, i: (0,) * a.ndim)

    def rows(width):
        return pl.BlockSpec((1, tm, width), lambda b, i: (b, i, 0))

    weights = (w["pre"], w["w_in"], w["q_norm"], w["kv_norm"], w["w_uq"], w["w_k"], w["w_vt"])
    vt_spec = pl.BlockSpec((1, 1, A_HEADS * LANES, tm), lambda b, i: (b, i // per_kv, 0, i % per_kv))
    out_shape = (
        jax.ShapeDtypeStruct((bsz, s, A_HEADS * LANES), BF16),
        jax.ShapeDtypeStruct((bsz, s, A_HEADS * LANES), BF16),
        jax.ShapeDtypeStruct((bsz, s // tkv, A_HEADS * LANES, tkv), BF16),
        jax.ShapeDtypeStruct((bsz, s, B_WIDTH), BF16),
        jax.ShapeDtypeStruct((bsz, s, B_WIDTH), BF16),
        jax.ShapeDtypeStruct((bsz, s, B_WIDTH), BF16),
        jax.ShapeDtypeStruct((bsz, s, A_WIDTH + B_WIDTH), BF16),
        jax.ShapeDtypeStruct((bsz, s, A_KV_RANK), F32),
        jax.ShapeDtypeStruct((bsz, s, A_ROPE), F32),
        jax.ShapeDtypeStruct((bsz, tail_len, B_WIDTH), F32),
        jax.ShapeDtypeStruct((bsz, tail_len, B_WIDTH), F32),
    )
    out_specs = (rows(1024), rows(1024), vt_spec, rows(512), rows(512), rows(512), rows(1024),
                 rows(A_KV_RANK), rows(A_ROPE), tail_spec, tail_spec)
    return pl.pallas_call(
        _ab_in_body,
        grid=(bsz, nt),
        in_specs=[rows(D_MODEL)] + [full(a) for a in weights]
        + [pl.BlockSpec((3, tm, LANES), lambda b, i: (0, i, 0))],
        out_specs=out_specs,
        out_shape=out_shape,
        compiler_params=_params(2),
        name="ab_in_proj",
    )(x, *weights, rope)


def _c_in_body(x_ref, pre_ref, w_ref, rope_ref, q_ref, k_ref, v_ref, sg_ref, kt_ref, vt_ref):
    tm = x_ref.shape[1]
    tb = kt_ref.shape[1]
    xn = _rms(x_ref[0], pre_ref[...]).astype(BF16)
    z = jnp.dot(xn, w_ref[...], preferred_element_type=F32)
    half_rot = C_ROT // 2
    for j in range(C_WIDTH // LANES):
        blk = slice(C_Q0 + j * LANES, C_Q0 + (j + 1) * LANES)
        q_ref[0, :, j * LANES:(j + 1) * LANES] = (_rope_block(z[:, blk], rope_ref, half_rot) * C_SCALE).astype(BF16)
    kd = [_rope_block(z[:, C_K0 + j * LANES:C_K0 + (j + 1) * LANES], rope_ref, half_rot) for j in range(C_KV_HEADS)]
    vd = [z[:, C_V0 + j * LANES:C_V0 + (j + 1) * LANES] for j in range(C_KV_HEADS)]
    for j in range(C_KV_HEADS):
        k_ref[0, :, j * LANES:(j + 1) * LANES] = kd[j].astype(BF16)
        v_ref[0, :, j * LANES:(j + 1) * LANES] = vd[j].astype(BF16)
    g = z[:, C_G0:C_NZ]
    sg_ref[0] = (g * jax.nn.sigmoid(g)).astype(BF16)
    lo = lax.broadcasted_iota(jnp.int32, (1, LANES), 1) < HALF
    kt_ref[0] = jnp.where(lo, kd[0], kd[1])[tm - tb:, :]
    vt_ref[0] = jnp.where(lo, vd[0], vd[1])[tm - tb:, :]


def _c_in_proj(x, rope, w, tm, tail_len):
    bsz, s, _ = x.shape
    nt = s // tm
    tb, tail_spec = _tail_spec(nt, tm, tail_len, LANES)

    def full(a):
        return pl.BlockSpec(a.shape, lambda b, i: (0,) * a.ndim)

    def rows(width):
        return pl.BlockSpec((1, tm, width), lambda b, i: (b, i, 0))

    out_shape = (
        jax.ShapeDtypeStruct((bsz, s, C_WIDTH), BF16),
        jax.ShapeDtypeStruct((bsz, s, C_KV_HEADS * LANES), BF16),
        jax.ShapeDtypeStruct((bsz, s, C_KV_HEADS * LANES), BF16),
        jax.ShapeDtypeStruct((bsz, s, C_WIDTH), BF16),
        jax.ShapeDtypeStruct((bsz, tail_len, LANES), F32),
        jax.ShapeDtypeStruct((bsz, tail_len, LANES), F32),
    )
    return pl.pallas_call(
        _c_in_body,
        grid=(bsz, nt),
        in_specs=[rows(D_MODEL), full(w["pre"]), full(w["w_in"]),
                  pl.BlockSpec((3, tm, LANES), lambda b, i: (0, i, 0))],
        out_specs=(rows(C_WIDTH), rows(256), rows(256), rows(C_WIDTH), tail_spec, tail_spec),
        out_shape=out_shape,
        compiler_params=_params(2),
        name="c_in_proj",
    )(x, w["pre"], w["w_in"], rope)


def _out_body(*refs):
    o_refs, (sg_ref, w_ref, g_ref, h_ref, out_ref) = refs[:-5], refs[-5:]
    o = jnp.concatenate([r[0].astype(F32) for r in o_refs], axis=-1) if len(o_refs) > 1 else o_refs[0][0].astype(F32)
    mixed = (o * sg_ref[0].astype(F32)).astype(BF16)
    y = jnp.dot(mixed, w_ref[...], preferred_element_type=F32)
    out_ref[0] = h_ref[0] + _rms(y, g_ref[...])


def _out_proj(o_parts, sg, w_out, post_g, h, tm):
    bsz, s, _ = h.shape

    def rows(width):
        return pl.BlockSpec((1, tm, width), lambda b, i: (b, i, 0))

    def full(a):
        return pl.BlockSpec(a.shape, lambda b, i: (0,) * a.ndim)

    return pl.pallas_call(
        _out_body,
        grid=(bsz, s // tm),
        in_specs=[rows(o.shape[-1]) for o in o_parts] + [rows(D_MODEL), full(w_out), full(post_g), rows(D_MODEL)],
        out_specs=rows(D_MODEL),
        out_shape=jax.ShapeDtypeStruct(h.shape, F32),
        compiler_params=_params(2),
        name="out_proj",
    )(*o_parts, sg, w_out, post_g, h)


def _mla_prompt_body(q_ref, k_ref, vt_ref, o_ref, m_sc, acc_sc, s_sc):
    tq = q_ref.shape[1]
    qi = pl.program_id(2)
    m_sc[...] = jnp.full(m_sc.shape, NEG_INF, F32)
    acc_sc[...] = jnp.zeros(acc_sc.shape, F32)

    def scores(j, slot, masked):
        start = pl.multiple_of(j * tq, tq)
        for hh in range(2):
            blk = slice(hh * LANES, (hh + 1) * LANES)
            s = lax.dot_general(k_ref[0, pl.ds(start, tq), blk], q_ref[0, :, blk], (((1,), (1,)), ((), ())),
                                preferred_element_type=F32)
            if masked:
                kc = lax.broadcasted_iota(jnp.int32, (tq, tq), 0) // CHUNK
                qc = lax.broadcasted_iota(jnp.int32, (tq, tq), 1) // CHUNK
                s = jnp.where(kc <= qc, s, NEG_INF)
            s_sc[slot, hh] = s

    def consume(j, slot):
        for hh in range(2):
            s = s_sc[slot, hh]
            vt = vt_ref[0, j, hh * LANES:(hh + 1) * LANES, :]
            m_old = m_sc[hh]
            m_new = jnp.maximum(m_old, jnp.max(s, axis=0, keepdims=True))
            p = jnp.exp(s - m_new).astype(BF16)
            acc_sc[hh] = jnp.exp(m_old - m_new) * acc_sc[hh] + jnp.dot(vt, p, preferred_element_type=F32)
            m_sc[hh] = m_new

    @pl.when(qi == 0)
    def _():
        scores(0, 0, True)
        consume(0, 0)

    def stage(j_next, masked_next, j_cur, cur_slot):
        scores(j_next, 1 - cur_slot, masked_next)
        consume(j_cur, cur_slot)

    @pl.when(qi > 0)
    def _():
        scores(0, 0, False)

        def pair_body(i, carry):
            j = 2 * i
            stage(j + 1, False, j, 0)
            stage(j + 2, False, j + 1, 1)
            return carry

        lax.fori_loop(0, (qi - 1) // 2, pair_body, 0)
        odd = ((qi - 1) & 1) == 1

        @pl.when(odd)
        def _():
            stage(qi - 1, False, qi - 2, 0)
            stage(qi, True, qi - 1, 1)
            consume(qi, 0)

        @pl.when(jnp.logical_not(odd))
        def _():
            stage(qi, True, qi - 1, 0)
            consume(qi, 1)

    outs = []
    for hh in range(2):
        acc = acc_sc[hh]
        outs.append((acc / acc[A_V:A_V + 1, :]).T)
    lo = lax.broadcasted_iota(jnp.int32, (1, LANES), 1) < HALF
    o_ref[0] = jnp.where(lo, outs[0], pltpu.roll(outs[1], HALF, 1)).astype(o_ref.dtype)


def _mla_prompt(qa, ka, vt, tq):
    bsz, s, _ = qa.shape
    pairs = A_HEADS // 2
    assert vt.shape[3] == tq
    return pl.pallas_call(
        _mla_prompt_body,
        grid=(bsz, pairs, s // tq),
        in_specs=[pl.BlockSpec((1, tq, 2 * LANES), lambda b, h, i: (b, i, h)),
                  pl.BlockSpec((1, s, 2 * LANES), lambda b, h, i: (b, 0, h)),
                  pl.BlockSpec((1, s // tq, 2 * LANES, tq), lambda b, h, i: (b, 0, h, 0))],
        out_specs=pl.BlockSpec((1, tq, LANES), lambda b, h, i: (b, i, h)),
        out_shape=jax.ShapeDtypeStruct((bsz, s, A_WIDTH), BF16),
        scratch_shapes=[pltpu.VMEM((2, 1, tq), F32), pltpu.VMEM((2, LANES, tq), F32),
                        pltpu.VMEM((2, 2, tq, tq), F32)],
        compiler_params=_params(3),
        name="mla_prompt",
    )(qa, ka, vt)


def _mla_sample_body(q_ref, cc_ref, ckr_ref, cn_ref, krn_ref, wkt_ref, wv_ref, sel_ref, o_ref,
                     qabs_sc, qr_sc, m_sc, l_sc, acc_sc, *, tk):
    t = q_ref.shape[1]
    past = cc_ref.shape[1]
    for h in range(A_HEADS):
        qh = q_ref[0, :, h * LANES:(h + 1) * LANES]
        rows = slice(h * t, (h + 1) * t)
        qabs_sc[rows, :] = jnp.dot(qh, wkt_ref[h], preferred_element_type=F32).astype(BF16)
        qr_sc[rows, :] = jnp.dot(qh, sel_ref[...], preferred_element_type=F32).astype(BF16)
    m_sc[...] = jnp.full(m_sc.shape, NEG_INF, F32)
    l_sc[...] = jnp.zeros(l_sc.shape, F32)
    acc_sc[...] = jnp.zeros(acc_sc.shape, F32)
    contract_last = (((1,), (1,)), ((), ()))

    def update(c_t, kr_t):
        s = (lax.dot_general(qabs_sc[...], c_t, contract_last, preferred_element_type=F32)
             + lax.dot_general(qr_sc[...], kr_t, contract_last, preferred_element_type=F32))
        m_old = m_sc[...]
        m_new = jnp.maximum(m_old, jnp.max(s, axis=-1, keepdims=True))
        p = jnp.exp(s - m_new)
        alpha = jnp.exp(m_old - m_new)
        l_sc[...] = alpha * l_sc[...] + jnp.sum(p, axis=-1, keepdims=True)
        acc_sc[...] = alpha * acc_sc[...] + jnp.dot(p.astype(BF16), c_t, preferred_element_type=F32)
        m_sc[...] = m_new

    def loop_body(j, carry):
        start = pl.multiple_of(j * tk, tk)
        update(cc_ref[0, pl.ds(start, tk), :].astype(BF16), ckr_ref[0, pl.ds(start, tk), :].astype(BF16))
        return carry

    lax.fori_loop(0, past // tk, loop_body, 0)
    update(cn_ref[0].astype(BF16), krn_ref[0].astype(BF16))

    o_lat = (acc_sc[...] / l_sc[...]).astype(BF16)
    out = jnp.zeros((t, A_WIDTH), F32)
    for h in range(A_HEADS):
        out = out + jnp.dot(o_lat[h * t:(h + 1) * t, :], wv_ref[h], preferred_element_type=F32)
    o_ref[0] = out.astype(o_ref.dtype)


def _mla_sample(qa, cache_c, cache_kr, c_new, kr_new, wkt, wv, sel, tk):
    bsz, t, _ = qa.shape
    past = cache_c.shape[1]

    def per_b(shape):
        return pl.BlockSpec((1,) + shape, lambda b: (b, 0, 0))

    def full(a):
        return pl.BlockSpec(a.shape, lambda b: (0,) * a.ndim)

    rows = A_HEADS * t
    return pl.pallas_call(
        functools.partial(_mla_sample_body, tk=tk),
        grid=(bsz,),
        in_specs=[per_b((t, A_HEADS * LANES)), per_b((past, A_KV_RANK)), per_b((past, A_ROPE)),
                  per_b((t, A_KV_RANK)), per_b((t, A_ROPE)), full(wkt), full(wv), full(sel)],
        out_specs=per_b((t, A_WIDTH)),
        out_shape=jax.ShapeDtypeStruct((bsz, t, A_WIDTH), BF16),
        scratch_shapes=[pltpu.VMEM((rows, A_KV_RANK), BF16), pltpu.VMEM((rows, A_ROPE), BF16),
                        pltpu.VMEM((rows, 1), F32), pltpu.VMEM((rows, 1), F32),
                        pltpu.VMEM((rows, A_KV_RANK), F32)],
        compiler_params=_params(1),
        name="mla_sample",
    )(qa, cache_c, cache_kr, c_new, kr_new, wkt, wv, sel)


def _band_body(*refs, n_past, win, n_qblk, std_bias, has_sink, whole_band):
    if has_sink:
        q_ref, k_ref, v_ref, bias_ref, sink_ref, o_ref = refs
    else:
        q_ref, k_ref, v_ref, bias_ref, o_ref = refs
    tq = q_ref.shape[1]
    t = pl.program_id(2)
    lo = lax.broadcasted_iota(jnp.int32, (CHUNK, LANES), 1) < HALF
    contract_last = (((1,), (1,)), ((), ()))

    def do_chunk(c, start, variant):
        rows = slice(c * CHUNK, (c + 1) * CHUNK)
        pieces = []
        for r in range(n_qblk):
            qblk = q_ref[0, rows, r * LANES:(r + 1) * LANES]
            zero = jnp.zeros_like(qblk)
            pieces += [jnp.where(lo, qblk, zero), jnp.where(lo, zero, qblk)]
        qs = jnp.concatenate(pieces, axis=0)
        kw = k_ref[0, pl.ds(start, win), :]
        vw = v_ref[0, pl.ds(start, win), :]
        s = lax.dot_general(qs, kw, contract_last, preferred_element_type=F32)
        if std_bias or variant != n_past:
            s = s + bias_ref[0, variant]
        m = jnp.max(s, axis=-1, keepdims=True)
        if has_sink:
            sink = sink_ref[0]
            m = jnp.maximum(m, sink)
        p = jnp.exp(s - m)
        l = jnp.sum(p, axis=-1, keepdims=True)
        if has_sink:
            l = l + jnp.exp(sink - m)
        o = jnp.dot(p.astype(BF16), vw, preferred_element_type=F32) / l
        for r in range(n_qblk):
            top = o[(2 * r) * CHUNK:(2 * r + 1) * CHUNK]
            bot = o[(2 * r + 1) * CHUNK:(2 * r + 2) * CHUNK]
            o_ref[0, rows, r * LANES:(r + 1) * LANES] = jnp.where(lo, top, bot).astype(o_ref.dtype)

    n_chunks = tq // CHUNK
    if whole_band:
        do_chunk(0, 0, n_past)
        return

    @pl.when(t == 0)
    def _():
        for c in range(n_chunks):
            do_chunk(c, max(c - n_past, 0) * CHUNK, min(c, n_past))

    @pl.when(t > 0)
    def _():
        for c in range(n_chunks):
            do_chunk(c, pl.multiple_of(t * tq + (c - n_past) * CHUNK, CHUNK), n_past)


def _band_attention(q, k, v, bias, sink, *, n_groups, n_past, n_qblk, std_bias, tq, whole_band):
    bsz, s, _ = q.shape
    sk = k.shape[1]
    win = (n_past + 1) * CHUNK
    qw = n_qblk * LANES
    in_specs = [pl.BlockSpec((1, tq, qw), lambda b, g, i: (b, i, g)),
                pl.BlockSpec((1, sk, LANES), lambda b, g, i: (b, 0, g)),
                pl.BlockSpec((1, sk, LANES), lambda b, g, i: (b, 0, g)),
                pl.BlockSpec((1,) + bias.shape[1:], lambda b, g, i: (g, 0, 0, 0))]
    args = [q, k, v, bias]
    if sink is not None:
        in_specs.append(pl.BlockSpec((1,) + sink.shape[1:], lambda b, g, i: (g, 0, 0)))
        args.append(sink)
    body = functools.partial(_band_body, n_past=n_past, win=win, n_qblk=n_qblk, std_bias=std_bias,
                             has_sink=sink is not None, whole_band=whole_band)
    return pl.pallas_call(
        body,
        grid=(bsz, n_groups, s // tq),
        in_specs=in_specs,
        out_specs=pl.BlockSpec((1, tq, qw), lambda b, g, i: (b, i, g)),
        out_shape=jax.ShapeDtypeStruct(q.shape, BF16),
        compiler_params=_params(3),
        name="band_attention",
    )(*args)


def _rope_tables(pos, rot, lane_pattern):
    half = rot // 2
    inv = jnp.power(ROPE_THETA, -jnp.arange(half, dtype=F32) * 2.0 / rot)
    ang = pos.astype(F32)[:, None] * inv[None, :]
    cos, sin = jnp.cos(ang), jnp.sin(ang)
    n = pos.shape[0]
    one, zero = jnp.ones((n, 1), F32), jnp.zeros((n, 1), F32)

    def build(first, second, fill):
        cols = []
        for kind, width in lane_pattern:
            if kind == "rot":
                cols += [first, second]
            else:
                cols.append(jnp.broadcast_to(fill, (n, width)))
        return jnp.concatenate(cols, axis=1)

    zeros_h = jnp.zeros_like(sin)
    return jnp.stack([build(cos, cos, one), build(zeros_h, sin, zero), build(-sin, zeros_h, zero)])


A_ROPE_PATTERN = (("pad", A_NOPE), ("rot", A_ROPE), ("pad", LANES - A_QK))
C_ROPE_PATTERN = (("rot", C_ROT), ("pad", HALF - C_ROT)) * 2


def _prep_ab(pre, post, w_in, q_norm, kv_norm, w_uq, w_ukv, rel_bias, w_out):
    d = w_in.shape[0]
    q_lat, c_kv, k_r, g_a, q_b, k_b, v_b, g_b = jnp.split(
        w_in, [384, 640, 672, 1184, 1696, 2208, 2720], axis=1)
    kr_blk = jnp.concatenate([jnp.zeros((d, A_NOPE), F32), k_r, jnp.zeros((d, LANES - A_QK), F32)], axis=1)
    w_in_p = jnp.concatenate([q_lat, c_kv, kr_blk, g_a, g_b, q_b, k_b, v_b], axis=1).astype(BF16)
    w_uq_p = jnp.pad(w_uq.reshape(A_Q_RANK, A_HEADS, A_QK), ((0, 0), (0, 0), (0, LANES - A_QK)))
    w_uq_p = w_uq_p.reshape(A_Q_RANK, A_HEADS * LANES).astype(BF16)
    ukv = w_ukv.reshape(A_KV_RANK, A_HEADS, A_NOPE + A_V)
    w_uk, w_uv = ukv[..., :A_NOPE], ukv[..., A_NOPE:]
    pad_half = ((0, 0), (0, 0), (0, LANES - A_NOPE))
    w_k = jnp.pad(w_uk, pad_half).reshape(A_KV_RANK, A_HEADS * LANES).astype(BF16)
    w_vt = jnp.pad(w_uv, pad_half).reshape(A_KV_RANK, A_HEADS * LANES).T.astype(BF16)
    wkt = jnp.pad(jnp.transpose(w_uk, (1, 2, 0)), ((0, 0), (0, LANES - A_NOPE), (0, 0))).astype(BF16)
    eye = jnp.eye(A_HEADS, dtype=F32)
    wv_s = (jnp.transpose(w_uv, (1, 0, 2))[:, :, None, :] * eye[:, None, :, None]).reshape(
        A_HEADS, A_KV_RANK, A_WIDTH).astype(BF16)
    sel = (jnp.arange(LANES)[:, None] == A_NOPE + jnp.arange(A_ROPE)[None, :]).astype(BF16)
    band = B_PAST_CHUNKS * CHUNK
    win = band + CHUNK
    top_rel = band + CHUNK - 1
    n_vec = 2 * band + 2 * CHUNK
    n_hi = top_rel - B_MAX_REL
    n_lo = n_vec - n_hi - (2 * B_MAX_REL + 1)
    vec = jnp.concatenate([jnp.broadcast_to(rel_bias[:, -1:], (B_HEADS, n_hi)), rel_bias[:, ::-1],
                           jnp.broadcast_to(rel_bias[:, :1], (B_HEADS, n_lo))], axis=1)
    skew = jnp.tile(vec, (1, CHUNK))[:, :CHUNK * (n_vec - 1)].reshape(B_HEADS, CHUNK, n_vec - 1)
    toep = skew[:, :, CHUNK - 1:]
    kj = jnp.arange(win)[None, None, :]
    variants = []
    for c in range(B_PAST_CHUNKS + 1):
        bias = toep[:, :, band - c * CHUNK:band - c * CHUNK + win]
        if c < B_PAST_CHUNKS:
            bias = jnp.where(kj < (c + 1) * CHUNK, bias, NEG_INF)
        variants.append(bias.reshape(B_HEADS // 2, 2 * CHUNK, win))
    bias_all = jnp.stack(variants, axis=1)
    return dict(pre=pre[None], w_in=w_in_p, q_norm=q_norm[None], kv_norm=kv_norm[None], w_uq=w_uq_p, w_k=w_k,
                w_vt=w_vt, wkt=wkt, wv_s=wv_s, sel=sel, bias=bias_all, w_out=w_out.astype(BF16), post=post[None])


def _prep_c(pre, post, w_in, sinks, w_out):
    q, k, v, g = jnp.split(w_in, [1024, 1152, 1280], axis=1)
    k0, k1 = k[:, :C_HEAD_DIM], k[:, C_HEAD_DIM:]
    v0, v1 = v[:, :C_HEAD_DIM], v[:, C_HEAD_DIM:]
    w_in_p = jnp.concatenate([q, k0, k0, k1, k1, v0, v0, v1, v1, g], axis=1).astype(BF16)
    win = (C_PAST_CHUNKS + 1) * CHUNK
    kj = jnp.arange(win)[None, :]
    masks = [jnp.where(kj < (min(c, C_PAST_CHUNKS) + 1) * CHUNK, 0.0, NEG_INF).astype(F32)
             for c in range(C_PAST_CHUNKS + 1)]
    mask_all = jnp.broadcast_to(jnp.stack(masks)[None], (C_KV_HEADS, C_PAST_CHUNKS + 1, 1, win))
    sink_col = jnp.repeat(sinks.reshape(C_KV_HEADS, C_GROUP), CHUNK, axis=1)[..., None]
    return dict(pre=pre[None], w_in=w_in_p, mask=mask_all, sink=sink_col, w_out=w_out.astype(BF16), post=post[None])


def _dup_heads(x):
    return jnp.concatenate([x[:, :, 0], x[:, :, 0], x[:, :, 1], x[:, :, 1]], axis=-1)


def kernel(x_prompt, x_sample, cache_a_ckv, cache_a_krope, cache_b_k, cache_b_v, cache_c_k, cache_c_v,
           ab_pre_norm, ab_post_norm, ab_w_in, ab_q_norm, ab_kv_norm, ab_w_uq, ab_w_ukv, ab_rel_bias, ab_w_out,
           c_pre_norm, c_post_norm, c_w_in, c_sinks, c_w_out):
    bsz, seq, _ = x_prompt.shape
    dbs, dseq, _ = x_sample.shape
    past = cache_a_ckv.shape[2]
    n_s = dbs * dseq
    pos_p = jnp.arange(seq, dtype=jnp.int32)
    pos_s = jnp.tile(past + jnp.arange(dseq, dtype=jnp.int32), dbs)
    wab = _prep_ab(ab_pre_norm[0], ab_post_norm[0], ab_w_in[0], ab_q_norm[0], ab_kv_norm[0], ab_w_uq[0],
                   ab_w_ukv[0], ab_rel_bias[0], ab_w_out[0])
    wc = _prep_c(c_pre_norm[0], c_post_norm[0], c_w_in[0], c_sinks[0], c_w_out[0])
    b_tail = min(B_PAST_CHUNKS * CHUNK, seq)
    c_tail = min(C_WINDOW, seq)
    tile = 512

    rope_a_p = _rope_tables(pos_p, A_ROPE, A_ROPE_PATTERN)
    (qa, ka, vt, qb, kb, vb, sg, c_new_p, kr_new_p, kb_tail, vb_tail) = _ab_in_proj(
        x_prompt, rope_a_p, wab, tm=256, tail_len=b_tail, tkv=tile)
    o_a = _mla_prompt(qa, ka, vt, tq=tile)
    o_b = _band_attention(qb, kb, vb, wab["bias"], None, n_groups=B_HEADS // 2, n_past=B_PAST_CHUNKS, n_qblk=1,
                          std_bias=True, tq=tile, whole_band=False)
    h1_p = _out_proj([o_a, o_b], sg, wab["w_out"], wab["post"], x_prompt, tm=tile)

    rope_a_s = _rope_tables(pos_s, A_ROPE, A_ROPE_PATTERN)
    xs = x_sample.reshape(1, n_s, D_MODEL)
    (qa_s, _, _, qb_s, kb_s, vb_s, sg_s, c_new_s, kr_new_s, kb_s32, vb_s32) = _ab_in_proj(
        xs, rope_a_s, wab, tm=n_s, tail_len=n_s, tkv=n_s)
    o_a_s = _mla_sample(qa_s.reshape(dbs, dseq, -1), cache_a_ckv[0], cache_a_krope[0],
                        c_new_s.reshape(dbs, dseq, -1), kr_new_s.reshape(dbs, dseq, -1),
                        wab["wkt"], wab["wv_s"], wab["sel"], tk=512)
    wb = cache_b_k.shape[2]
    kband = jnp.concatenate([cache_b_k[0].reshape(dbs, wb, B_WIDTH).astype(BF16), kb_s.reshape(dbs, dseq, -1)], 1)
    vband = jnp.concatenate([cache_b_v[0].reshape(dbs, wb, B_WIDTH).astype(BF16), vb_s.reshape(dbs, dseq, -1)], 1)
    o_b_s = _band_attention(qb_s.reshape(dbs, dseq, -1), kband, vband, wab["bias"], None, n_groups=B_HEADS // 2,
                            n_past=B_PAST_CHUNKS, n_qblk=1, std_bias=True, tq=dseq, whole_band=True)
    h1_s = _out_proj([o_a_s.reshape(1, n_s, -1), o_b_s.reshape(1, n_s, -1)], sg_s, wab["w_out"], wab["post"],
                     xs, tm=n_s)

    rope_c_p = _rope_tables(pos_p, C_ROT, C_ROPE_PATTERN)
    qc, kc, vc, sgc, kc_tail, vc_tail = _c_in_proj(h1_p, rope_c_p, wc, tm=tile, tail_len=c_tail)
    o_c = _band_attention(qc, kc, vc, wc["mask"], wc["sink"], n_groups=C_KV_HEADS, n_past=C_PAST_CHUNKS,
                          n_qblk=C_GROUP // 2, std_bias=False, tq=tile, whole_band=False)
    h2_p = _out_proj([o_c], sgc, wc["w_out"], wc["post"], h1_p, tm=tile)

    rope_c_s = _rope_tables(pos_s, C_ROT, C_ROPE_PATTERN)
    qc_s, kc_s, vc_s, sgc_s, kc_s32, vc_s32 = _c_in_proj(h1_s, rope_c_s, wc, tm=n_s, tail_len=n_s)
    kcb = jnp.concatenate([_dup_heads(cache_c_k[0]).astype(BF16), kc_s.reshape(dbs, dseq, -1)], axis=1)
    vcb = jnp.concatenate([_dup_heads(cache_c_v[0]).astype(BF16), vc_s.reshape(dbs, dseq, -1)], axis=1)
    o_c_s = _band_attention(qc_s.reshape(dbs, dseq, -1), kcb, vcb, wc["mask"], wc["sink"], n_groups=C_KV_HEADS,
                            n_past=C_PAST_CHUNKS, n_qblk=C_GROUP // 2, std_bias=False, tq=dseq, whole_band=True)
    h2_s = _out_proj([o_c_s.reshape(1, n_s, -1)], sgc_s, wc["w_out"], wc["post"], h1_s, tm=n_s)

    def roll_in(buf, new):
        return jnp.concatenate([buf, new], axis=1)[:, -buf.shape[1]:][None]

    return (h2_p, h2_s.reshape(dbs, dseq, D_MODEL),
            c_new_p[None], kr_new_p[None],
            kb_tail.reshape(1, bsz, b_tail, B_HEADS, B_HEAD_DIM), vb_tail.reshape(1, bsz, b_tail, B_HEADS, B_HEAD_DIM),
            kc_tail.reshape(1, bsz, c_tail, C_KV_HEADS, C_HEAD_DIM), vc_tail.reshape(1, bsz, c_tail, C_KV_HEADS, C_HEAD_DIM),
            c_new_s.reshape(1, dbs, dseq, A_KV_RANK), kr_new_s.reshape(1, dbs, dseq, A_ROPE),
            roll_in(cache_b_k[0], kb_s32.reshape(dbs, dseq, B_HEADS, B_HEAD_DIM)),
            roll_in(cache_b_v[0], vb_s32.reshape(dbs, dseq, B_HEADS, B_HEAD_DIM)),
            roll_in(cache_c_k[0], kc_s32.reshape(dbs, dseq, C_KV_HEADS, C_HEAD_DIM)),
            roll_in(cache_c_v[0], vc_s32.reshape(dbs, dseq, C_KV_HEADS, C_HEAD_DIM)))
```

```python
import functools

import jax
import jax.numpy as jnp
from jax import lax
from jax.experimental import pallas as pl
from jax.experimental.pallas import tpu as pltpu

F32 = jnp.float32
BF16 = jnp.bfloat16

D_MODEL = 1024
CHUNK = 64
ROPE_THETA = 500000.0
RMS_EPS = 1e-6
NEG_INF = -1e30

A_HEADS = 8
A_NOPE = 64
A_ROPE = 32
A_QK = A_NOPE + A_ROPE
A_V = 64
A_Q_RANK = 384
A_KV_RANK = 256
A_WIDTH = A_HEADS * A_V
A_SCALE = A_QK ** -0.5

B_HEADS = 8
B_HEAD_DIM = 64
B_WIDTH = B_HEADS * B_HEAD_DIM
B_PAST_CHUNKS = 8
B_MAX_REL = 128
B_SCALE = B_HEAD_DIM ** -0.5

C_HEADS = 16
C_KV_HEADS = 2
C_GROUP = C_HEADS // C_KV_HEADS
C_HEAD_DIM = 64
C_WIDTH = C_HEADS * C_HEAD_DIM
C_WINDOW = 128
C_PAST_CHUNKS = C_WINDOW // CHUNK
C_ROT = C_HEAD_DIM // 4
C_SCALE = C_HEAD_DIM ** -0.5

LANES = 128
HALF = LANES // 2
VMEM_LIMIT = 56 * 1024 * 1024
GROUP_CHUNKS = LANES // CHUNK
B_WIN_BLOCKS = (B_PAST_CHUNKS + GROUP_CHUNKS) * CHUNK // LANES
C_WIN_BLOCKS = (C_PAST_CHUNKS + GROUP_CHUNKS) * CHUNK // LANES

AB_Q0, AB_C0, AB_KR0, AB_G0, AB_QB0, AB_KB0, AB_VB0, AB_NZ = 0, 384, 640, 768, 1792, 2304, 2816, 3328
C_Q0, C_K0, C_V0, C_G0, C_NZ = 0, 1024, 1280, 1536, 2560


def _params(n_axes):
    return pltpu.CompilerParams(dimension_semantics=("arbitrary",) * n_axes, vmem_limit_bytes=VMEM_LIMIT)


def _rms(x, g):
    return x * lax.rsqrt(jnp.mean(x * x, axis=-1, keepdims=True) + RMS_EPS) * g


def _rope_block(blk, rope_ref, shift):
    return (blk * rope_ref[0] + pltpu.roll(blk, shift, 1) * rope_ref[1]
            + pltpu.roll(blk, LANES - shift, 1) * rope_ref[2])


def _tail_spec(n_tiles, tm, tail_len, width):
    tb = min(tail_len, tm)
    n_blk = tail_len // tb
    return tb, pl.BlockSpec((1, tb, width), lambda b, i: (b, jnp.maximum(i - (n_tiles - n_blk), 0), 0))


def _ab_in_body(x_ref, pre_ref, w_ref, qn_ref, kvn_ref, wuq_ref, wk_ref, wvt_ref, rope_ref,
                qa_ref, ka_ref, vt_ref, qb_ref, kb_ref, vbt_ref, sg_ref, c_ref, kr_ref, kb_tail_ref, vb_tail_ref):
    tm = x_ref.shape[1]
    tb = kb_tail_ref.shape[1]
    xn = _rms(x_ref[0], pre_ref[...]).astype(BF16)
    z = jnp.dot(xn, w_ref[...], preferred_element_type=F32)

    qn = _rms(z[:, AB_Q0:AB_C0], qn_ref[...]).astype(BF16)
    qa = jnp.dot(qn, wuq_ref[...], preferred_element_type=F32) * A_SCALE
    for h in range(A_HEADS):
        blk = slice(h * LANES, (h + 1) * LANES)
        qa_ref[0, :, blk] = _rope_block(qa[:, blk], rope_ref, A_ROPE // 2).astype(BF16)

    c_new = _rms(z[:, AB_C0:AB_KR0], kvn_ref[...])
    c_ref[0] = c_new
    cb = c_new.astype(BF16)
    krot = _rope_block(z[:, AB_KR0:AB_G0], rope_ref, A_ROPE // 2)
    kr_ref[0] = krot[:, A_NOPE:A_NOPE + A_ROPE]
    kn = jnp.dot(cb, wk_ref[...], preferred_element_type=F32)
    for h in range(A_HEADS):
        blk = slice(h * LANES, (h + 1) * LANES)
        ka_ref[0, :, blk] = (kn[:, blk] + krot).astype(BF16)
    vt = jnp.dot(wvt_ref[...], c_new.T.astype(BF16), preferred_element_type=F32)
    row = lax.broadcasted_iota(jnp.int32, vt.shape, 0)
    vt_ref[0, 0] = jnp.where((row & (LANES - 1)) == A_V, 1.0, vt).astype(BF16)

    g = z[:, AB_G0:AB_QB0]
    sg_ref[0] = (g * jax.nn.sigmoid(g)).astype(BF16)
    qb_ref[0] = (z[:, AB_QB0:AB_KB0] * B_SCALE).astype(BF16)
    kb = z[:, AB_KB0:AB_VB0]
    vb = z[:, AB_VB0:AB_NZ]
    kb_ref[0] = kb.astype(BF16)
    vbt = vb.T.astype(BF16)
    for hp in range(B_HEADS // 2):
        for j in range(tm // LANES):
            vbt_ref[0, hp, j] = vbt[hp * LANES:(hp + 1) * LANES, j * LANES:(j + 1) * LANES]
    kb_tail_ref[0] = kb[tm - tb:, :]
    vb_tail_ref[0] = vb[tm - tb:, :]


def _ab_in_proj(x, rope, w, tm, tail_len, tkv):
    bsz, s, _ = x.shape
    nt = s // tm
    per_kv = tkv // tm
    tb, tail_spec = _tail_spec(nt, tm, tail_len, B_WIDTH)

    def full(a):
        return pl.BlockSpec(a.shape, lambda b, i: (0,) * a.ndim)

    def rows(width):
        return pl.BlockSpec((1, tm, width), lambda b, i: (b, i, 0))

    weights = (w["pre"], w["w_in"], w["q_norm"], w["kv_norm"], w["w_uq"], w["w_k"], w["w_vt"])
    vt_spec = pl.BlockSpec((1, 1, A_HEADS * LANES, tm), lambda b, i: (b, i // per_kv, 0, i % per_kv))
    out_shape = (
        jax.ShapeDtypeStruct((bsz, s, A_HEADS * LANES), BF16),
        jax.ShapeDtypeStruct((bsz, s, A_HEADS * LANES), BF16),
        jax.ShapeDtypeStruct((bsz, s // tkv, A_HEADS * LANES, tkv), BF16),
        jax.ShapeDtypeStruct((bsz, s, B_WIDTH), BF16),
        jax.ShapeDtypeStruct((bsz, s, B_WIDTH), BF16),
        jax.ShapeDtypeStruct((bsz, B_HEADS // 2, s // LANES, LANES, LANES), BF16),
        jax.ShapeDtypeStruct((bsz, s, A_WIDTH + B_WIDTH), BF16),
        jax.ShapeDtypeStruct((bsz, s, A_KV_RANK), F32),
        jax.ShapeDtypeStruct((bsz, s, A_ROPE), F32),
        jax.ShapeDtypeStruct((bsz, tail_len, B_WIDTH), F32),
        jax.ShapeDtypeStruct((bsz, tail_len, B_WIDTH), F32),
    )
    vbt_spec = pl.BlockSpec((1, B_HEADS // 2, tm // LANES, LANES, LANES), lambda b, i: (b, 0, i, 0, 0))
    out_specs = (rows(1024), rows(1024), vt_spec, rows(512), rows(512), vbt_spec, rows(1024),
                 rows(A_KV_RANK), rows(A_ROPE), tail_spec, tail_spec)
    return pl.pallas_call(
        _ab_in_body,
        grid=(bsz, nt),
        in_specs=[rows(D_MODEL)] + [full(a) for a in weights]
        + [pl.BlockSpec((3, tm, LANES), lambda b, i: (0, i, 0))],
        out_specs=out_specs,
        out_shape=out_shape,
        compiler_params=_params(2),
        name="ab_in_proj",
    )(x, *weights, rope)


def _c_in_body(x_ref, pre_ref, w_ref, rope_ref, q_ref, k_ref, vt_ref, sg_ref, k_tail_ref, v_tail_ref):
    tm = x_ref.shape[1]
    tb = k_tail_ref.shape[1]
    xn = _rms(x_ref[0], pre_ref[...]).astype(BF16)
    z = jnp.dot(xn, w_ref[...], preferred_element_type=F32)
    half_rot = C_ROT // 2
    for j in range(C_WIDTH // LANES):
        blk = slice(C_Q0 + j * LANES, C_Q0 + (j + 1) * LANES)
        q_ref[0, :, j * LANES:(j + 1) * LANES] = (_rope_block(z[:, blk], rope_ref, half_rot) * C_SCALE).astype(BF16)
    kd = [_rope_block(z[:, C_K0 + j * LANES:C_K0 + (j + 1) * LANES], rope_ref, half_rot) for j in range(C_KV_HEADS)]
    vz = [z[:, C_V0 + j * LANES:C_V0 + (j + 1) * LANES] for j in range(C_KV_HEADS)]
    row = lax.broadcasted_iota(jnp.int32, (LANES, tm), 0)
    for j in range(C_KV_HEADS):
        k_ref[0, :, j * LANES:(j + 1) * LANES] = kd[j].astype(BF16)
        vt = jnp.where(row == C_HEAD_DIM, 1.0, vz[j].T).astype(BF16)
        for i in range(tm // LANES):
            vt_ref[0, j, i] = vt[:, i * LANES:(i + 1) * LANES]
    g = z[:, C_G0:C_NZ]
    sg_ref[0] = (g * jax.nn.sigmoid(g)).astype(BF16)
    lo = lax.broadcasted_iota(jnp.int32, (1, LANES), 1) < HALF
    k_tail_ref[0] = jnp.where(lo, kd[0], kd[1])[tm - tb:, :]
    v_tail_ref[0] = (vz[0] + pltpu.roll(vz[1], HALF, 1))[tm - tb:, :]


def _c_in_proj(x, rope, w, tm, tail_len):
    bsz, s, _ = x.shape
    nt = s // tm
    tb, tail_spec = _tail_spec(nt, tm, tail_len, LANES)

    def full(a):
        return pl.BlockSpec(a.shape, lambda b, i: (0,) * a.ndim)

    def rows(width):
        return pl.BlockSpec((1, tm, width), lambda b, i: (b, i, 0))

    out_shape = (
        jax.ShapeDtypeStruct((bsz, s, C_WIDTH), BF16),
        jax.ShapeDtypeStruct((bsz, s, C_KV_HEADS * LANES), BF16),
        jax.ShapeDtypeStruct((bsz, C_KV_HEADS, s // LANES, LANES, LANES), BF16),
        jax.ShapeDtypeStruct((bsz, s, C_WIDTH), BF16),
        jax.ShapeDtypeStruct((bsz, tail_len, LANES), F32),
        jax.ShapeDtypeStruct((bsz, tail_len, LANES), F32),
    )
    return pl.pallas_call(
        _c_in_body,
        grid=(bsz, nt),
        in_specs=[rows(D_MODEL), full(w["pre"]), full(w["w_in"]),
                  pl.BlockSpec((3, tm, LANES), lambda b, i: (0, i, 0))],
        out_specs=(rows(C_WIDTH), rows(256),
                   pl.BlockSpec((1, C_KV_HEADS, tm // LANES, LANES, LANES), lambda b, i: (b, 0, i, 0, 0)),
                   rows(C_WIDTH), tail_spec, tail_spec),
        out_shape=out_shape,
        compiler_params=_params(2),
        name="c_in_proj",
    )(x, w["pre"], w["w_in"], rope)


def _out_body(*refs):
    o_refs, (sg_ref, w_ref, g_ref, h_ref, out_ref) = refs[:-5], refs[-5:]
    o = jnp.concatenate([r[0].astype(F32) for r in o_refs], axis=-1) if len(o_refs) > 1 else o_refs[0][0].astype(F32)
    mixed = (o * sg_ref[0].astype(F32)).astype(BF16)
    y = jnp.dot(mixed, w_ref[...], preferred_element_type=F32)
    out_ref[0] = h_ref[0] + _rms(y, g_ref[...])


def _out_proj(o_parts, sg, w_out, post_g, h, tm):
    bsz, s, _ = h.shape

    def rows(width):
        return pl.BlockSpec((1, tm, width), lambda b, i: (b, i, 0))

    def full(a):
        return pl.BlockSpec(a.shape, lambda b, i: (0,) * a.ndim)

    return pl.pallas_call(
        _out_body,
        grid=(bsz, s // tm),
        in_specs=[rows(o.shape[-1]) for o in o_parts] + [rows(D_MODEL), full(w_out), full(post_g), rows(D_MODEL)],
        out_specs=rows(D_MODEL),
        out_shape=jax.ShapeDtypeStruct(h.shape, F32),
        compiler_params=_params(2),
        name="out_proj",
    )(*o_parts, sg, w_out, post_g, h)


def _mla_prompt_body(q_ref, k_ref, vt_ref, o_ref, m_sc, acc_sc, s_sc):
    tq = q_ref.shape[1]
    qi = pl.program_id(2)
    m_sc[...] = jnp.full(m_sc.shape, NEG_INF, F32)
    acc_sc[...] = jnp.zeros(acc_sc.shape, F32)

    def scores(j, slot, masked):
        start = pl.multiple_of(j * tq, tq)
        for hh in range(2):
            blk = slice(hh * LANES, (hh + 1) * LANES)
            s = lax.dot_general(k_ref[0, pl.ds(start, tq), blk], q_ref[0, :, blk], (((1,), (1,)), ((), ())),
                                preferred_element_type=F32)
            if masked:
                kc = lax.broadcasted_iota(jnp.int32, (tq, tq), 0) // CHUNK
                qc = lax.broadcasted_iota(jnp.int32, (tq, tq), 1) // CHUNK
                s = jnp.where(kc <= qc, s, NEG_INF)
            s_sc[slot, hh] = s

    def consume(j, slot):
        for hh in range(2):
            s = s_sc[slot, hh]
            vt = vt_ref[0, j, hh * LANES:(hh + 1) * LANES, :]
            m_old = m_sc[hh]
            m_new = jnp.maximum(m_old, jnp.max(s, axis=0, keepdims=True))
            p = jnp.exp(s - m_new).astype(BF16)
            acc_sc[hh] = jnp.exp(m_old - m_new) * acc_sc[hh] + jnp.dot(vt, p, preferred_element_type=F32)
            m_sc[hh] = m_new

    @pl.when(qi == 0)
    def _():
        scores(0, 0, True)
        consume(0, 0)

    def stage(j_next, masked_next, j_cur, cur_slot):
        scores(j_next, 1 - cur_slot, masked_next)
        consume(j_cur, cur_slot)

    @pl.when(qi > 0)
    def _():
        scores(0, 0, False)

        def pair_body(i, carry):
            j = 2 * i
            stage(j + 1, False, j, 0)
            stage(j + 2, False, j + 1, 1)
            return carry

        lax.fori_loop(0, (qi - 1) // 2, pair_body, 0)
        odd = ((qi - 1) & 1) == 1

        @pl.when(odd)
        def _():
            stage(qi - 1, False, qi - 2, 0)
            stage(qi, True, qi - 1, 1)
            consume(qi, 0)

        @pl.when(jnp.logical_not(odd))
        def _():
            stage(qi, True, qi - 1, 0)
            consume(qi, 1)

    outs = []
    for hh in range(2):
        acc = acc_sc[hh]
        outs.append((acc / acc[A_V:A_V + 1, :]).T)
    lo = lax.broadcasted_iota(jnp.int32, (1, LANES), 1) < HALF
    o_ref[0] = jnp.where(lo, outs[0], pltpu.roll(outs[1], HALF, 1)).astype(o_ref.dtype)


def _mla_prompt(qa, ka, vt, tq):
    bsz, s, _ = qa.shape
    pairs = A_HEADS // 2
    assert vt.shape[3] == tq
    return pl.pallas_call(
        _mla_prompt_body,
        grid=(bsz, pairs, s // tq),
        in_specs=[pl.BlockSpec((1, tq, 2 * LANES), lambda b, h, i: (b, i, h)),
                  pl.BlockSpec((1, s, 2 * LANES), lambda b, h, i: (b, 0, h)),
                  pl.BlockSpec((1, s // tq, 2 * LANES, tq), lambda b, h, i: (b, 0, h, 0))],
        out_specs=pl.BlockSpec((1, tq, LANES), lambda b, h, i: (b, i, h)),
        out_shape=jax.ShapeDtypeStruct((bsz, s, A_WIDTH), BF16),
        scratch_shapes=[pltpu.VMEM((2, 1, tq), F32), pltpu.VMEM((2, LANES, tq), F32),
                        pltpu.VMEM((2, 2, tq, tq), F32)],
        compiler_params=_params(3),
        name="mla_prompt",
    )(qa, ka, vt)


def _mla_sample_body(q_ref, cc_ref, ckr_ref, cn_ref, krn_ref, wkt_ref, wv_ref, sel_ref, o_ref,
                     qabs_sc, qr_sc, m_sc, l_sc, acc_sc, *, tk):
    t = q_ref.shape[1]
    past = cc_ref.shape[1]
    for h in range(A_HEADS):
        qh = q_ref[0, :, h * LANES:(h + 1) * LANES]
        rows = slice(h * t, (h + 1) * t)
        qabs_sc[rows, :] = jnp.dot(qh, wkt_ref[h], preferred_element_type=F32).astype(BF16)
        qr_sc[rows, :] = jnp.dot(qh, sel_ref[...], preferred_element_type=F32).astype(BF16)
    m_sc[...] = jnp.full(m_sc.shape, NEG_INF, F32)
    l_sc[...] = jnp.zeros(l_sc.shape, F32)
    acc_sc[...] = jnp.zeros(acc_sc.shape, F32)
    contract_last = (((1,), (1,)), ((), ()))

    def update(c_t, kr_t):
        s = (lax.dot_general(qabs_sc[...], c_t, contract_last, preferred_element_type=F32)
             + lax.dot_general(qr_sc[...], kr_t, contract_last, preferred_element_type=F32))
        m_old = m_sc[...]
        m_new = jnp.maximum(m_old, jnp.max(s, axis=-1, keepdims=True))
        p = jnp.exp(s - m_new)
        alpha = jnp.exp(m_old - m_new)
        l_sc[...] = alpha * l_sc[...] + jnp.sum(p, axis=-1, keepdims=True)
        acc_sc[...] = alpha * acc_sc[...] + jnp.dot(p.astype(BF16), c_t, preferred_element_type=F32)
        m_sc[...] = m_new

    def loop_body(j, carry):
        start = pl.multiple_of(j * tk, tk)
        update(cc_ref[0, pl.ds(start, tk), :].astype(BF16), ckr_ref[0, pl.ds(start, tk), :].astype(BF16))
        return carry

    lax.fori_loop(0, past // tk, loop_body, 0)
    update(cn_ref[0].astype(BF16), krn_ref[0].astype(BF16))

    o_lat = (acc_sc[...] / l_sc[...]).astype(BF16)
    out = jnp.zeros((t, A_WIDTH), F32)
    for h in range(A_HEADS):
        out = out + jnp.dot(o_lat[h * t:(h + 1) * t, :], wv_ref[h], preferred_element_type=F32)
    o_ref[0] = out.astype(o_ref.dtype)


def _mla_sample(qa, cache_c, cache_kr, c_new, kr_new, wkt, wv, sel, tk):
    bsz, t, _ = qa.shape
    past = cache_c.shape[1]

    def per_b(shape):
        return pl.BlockSpec((1,) + shape, lambda b: (b, 0, 0))

    def full(a):
        return pl.BlockSpec(a.shape, lambda b: (0,) * a.ndim)

    rows = A_HEADS * t
    return pl.pallas_call(
        functools.partial(_mla_sample_body, tk=tk),
        grid=(bsz,),
        in_specs=[per_b((t, A_HEADS * LANES)), per_b((past, A_KV_RANK)), per_b((past, A_ROPE)),
                  per_b((t, A_KV_RANK)), per_b((t, A_ROPE)), full(wkt), full(wv), full(sel)],
        out_specs=per_b((t, A_WIDTH)),
        out_shape=jax.ShapeDtypeStruct((bsz, t, A_WIDTH), BF16),
        scratch_shapes=[pltpu.VMEM((rows, A_KV_RANK), BF16), pltpu.VMEM((rows, A_ROPE), BF16),
                        pltpu.VMEM((rows, 1), F32), pltpu.VMEM((rows, 1), F32),
                        pltpu.VMEM((rows, A_KV_RANK), F32)],
        compiler_params=_params(1),
        name="mla_sample",
    )(qa, cache_c, cache_kr, c_new, kr_new, wkt, wv, sel)


def _band_body(*refs, n_blk, n_qblk, cpg, ones_row, has_sink):
    if has_sink:
        q_ref, k_ref, vt_ref, bias_ref, sink_ref, o_ref = refs
    else:
        q_ref, k_ref, vt_ref, bias_ref, o_ref = refs
    tq = q_ref.shape[1]
    win = n_blk * LANES
    t = pl.program_id(2)
    lo = lax.broadcasted_iota(jnp.int32, (CHUNK, LANES), 1) < HALF
    contract_last = (((1,), (1,)), ((), ()))

    def do_group(gl, wb, variant):
        pieces = []
        for cc in range(cpg):
            rows = slice((gl * cpg + cc) * CHUNK, (gl * cpg + cc + 1) * CHUNK)
            for r in range(n_qblk):
                qblk = q_ref[0, rows, r * LANES:(r + 1) * LANES]
                zero = jnp.zeros_like(qblk)
                pieces += [jnp.where(lo, qblk, zero), jnp.where(lo, zero, qblk)]
        qs = jnp.concatenate(pieces, axis=0)
        start = wb * LANES if isinstance(wb, int) else pl.multiple_of(wb * LANES, LANES)
        kw = k_ref[0, pl.ds(start, win), :]
        s = lax.dot_general(kw, qs, contract_last, preferred_element_type=F32) + bias_ref[0, variant]
        m = jnp.max(s, axis=0, keepdims=True)
        if has_sink:
            sink = sink_ref[0]
            m = jnp.maximum(m, sink)
        p = jnp.exp(s - m)
        vt = jnp.concatenate([vt_ref[0, 0, wb + i] for i in range(n_blk)], axis=1)
        o = jnp.dot(vt, p.astype(BF16), preferred_element_type=F32)
        l = o[CHUNK:CHUNK + 1, :] if ones_row else jnp.sum(p, axis=0, keepdims=True)
        if has_sink:
            l = l + jnp.exp(sink - m)
        o = (o / l).T
        idx = 0
        for cc in range(cpg):
            rows = slice((gl * cpg + cc) * CHUNK, (gl * cpg + cc + 1) * CHUNK)
            for r in range(n_qblk):
                top = o[idx * CHUNK:(idx + 1) * CHUNK]
                bot = o[(idx + 1) * CHUNK:(idx + 2) * CHUNK]
                if ones_row:
                    bot = pltpu.roll(bot, HALF, 1)
                o_ref[0, rows, r * LANES:(r + 1) * LANES] = jnp.where(lo, top, bot).astype(o_ref.dtype)
                idx += 2

    n_groups = tq // (cpg * CHUNK)
    if bias_ref.shape[1] == 1:
        do_group(0, 0, 0)
        return
    assert n_groups >= n_blk - 1

    @pl.when(t == 0)
    def _():
        for gl in range(n_groups):
            do_group(gl, max(gl - (n_blk - 1), 0), min(gl, n_blk - 1))

    @pl.when(t > 0)
    def _():
        for gl in range(n_groups):
            do_group(gl, t * n_groups + gl - (n_blk - 1), n_blk - 1)


def _band_attention(q, k, vt, bias, sink, *, n_groups, n_qblk, cpg, ones_row, tq):
    bsz, s, _ = q.shape
    sk = k.shape[1]
    n_blk = bias.shape[2] // LANES
    qw = n_qblk * LANES
    in_specs = [pl.BlockSpec((1, tq, qw), lambda b, g, i: (b, i, g)),
                pl.BlockSpec((1, sk, LANES), lambda b, g, i: (b, 0, g)),
                pl.BlockSpec((1, 1) + vt.shape[2:], lambda b, g, i: (b, g, 0, 0, 0)),
                pl.BlockSpec((1,) + bias.shape[1:], lambda b, g, i: (g, 0, 0, 0))]
    args = [q, k, vt, bias]
    if sink is not None:
        in_specs.append(pl.BlockSpec((1,) + sink.shape[1:], lambda b, g, i: (g, 0, 0)))
        args.append(sink)
    body = functools.partial(_band_body, n_blk=n_blk, n_qblk=n_qblk, cpg=cpg, ones_row=ones_row,
                             has_sink=sink is not None)
    return pl.pallas_call(
        body,
        grid=(bsz, n_groups, s // tq),
        in_specs=in_specs,
        out_specs=pl.BlockSpec((1, tq, qw), lambda b, g, i: (b, i, g)),
        out_shape=jax.ShapeDtypeStruct(q.shape, BF16),
        compiler_params=_params(3),
        name="band_attention",
    )(*args)


def _rope_tables(pos, rot, lane_pattern):
    half = rot // 2
    inv = jnp.power(ROPE_THETA, -jnp.arange(half, dtype=F32) * 2.0 / rot)
    ang = pos.astype(F32)[:, None] * inv[None, :]
    cos, sin = jnp.cos(ang), jnp.sin(ang)
    n = pos.shape[0]
    one, zero = jnp.ones((n, 1), F32), jnp.zeros((n, 1), F32)

    def build(first, second, fill):
        cols = []
        for kind, width in lane_pattern:
            if kind == "rot":
                cols += [first, second]
            else:
                cols.append(jnp.broadcast_to(fill, (n, width)))
        return jnp.concatenate(cols, axis=1)

    zeros_h = jnp.zeros_like(sin)
    return jnp.stack([build(cos, cos, one), build(zeros_h, sin, zero), build(-sin, zeros_h, zero)])


A_ROPE_PATTERN = (("pad", A_NOPE), ("rot", A_ROPE), ("pad", LANES - A_QK))
C_ROPE_PATTERN = (("rot", C_ROT), ("pad", HALF - C_ROT)) * 2


def _prep_ab(pre, post, w_in, q_norm, kv_norm, w_uq, w_ukv, rel_bias, w_out):
    d = w_in.shape[0]
    q_lat, c_kv, k_r, g_a, q_b, k_b, v_b, g_b = jnp.split(
        w_in, [384, 640, 672, 1184, 1696, 2208, 2720], axis=1)
    kr_blk = jnp.concatenate([jnp.zeros((d, A_NOPE), F32), k_r, jnp.zeros((d, LANES - A_QK), F32)], axis=1)
    w_in_p = jnp.concatenate([q_lat, c_kv, kr_blk, g_a, g_b, q_b, k_b, v_b], axis=1).astype(BF16)
    w_uq_p = jnp.pad(w_uq.reshape(A_Q_RANK, A_HEADS, A_QK), ((0, 0), (0, 0), (0, LANES - A_QK)))
    w_uq_p = w_uq_p.reshape(A_Q_RANK, A_HEADS * LANES).astype(BF16)
    ukv = w_ukv.reshape(A_KV_RANK, A_HEADS, A_NOPE + A_V)
    w_uk, w_uv = ukv[..., :A_NOPE], ukv[..., A_NOPE:]
    pad_half = ((0, 0), (0, 0), (0, LANES - A_NOPE))
    w_k = jnp.pad(w_uk, pad_half).reshape(A_KV_RANK, A_HEADS * LANES).astype(BF16)
    w_vt = jnp.pad(w_uv, pad_half).reshape(A_KV_RANK, A_HEADS * LANES).T.astype(BF16)
    wkt = jnp.pad(jnp.transpose(w_uk, (1, 2, 0)), ((0, 0), (0, LANES - A_NOPE), (0, 0))).astype(BF16)
    eye = jnp.eye(A_HEADS, dtype=F32)
    wv_s = (jnp.transpose(w_uv, (1, 0, 2))[:, :, None, :] * eye[:, None, :, None]).reshape(
        A_HEADS, A_KV_RANK, A_WIDTH).astype(BF16)
    sel = (jnp.arange(LANES)[:, None] == A_NOPE + jnp.arange(A_ROPE)[None, :]).astype(BF16)
    win = B_WIN_BLOCKS * LANES
    r0 = win - CHUNK
    x_len = r0 + win
    n_vec = x_len + CHUNK
    n_hi = r0 + CHUNK - 1 - B_MAX_REL
    n_lo = n_vec - n_hi - (2 * B_MAX_REL + 1)
    vec = jnp.concatenate([jnp.broadcast_to(rel_bias[:, -1:], (B_HEADS, n_hi)), rel_bias[:, ::-1],
                           jnp.broadcast_to(rel_bias[:, :1], (B_HEADS, n_lo))], axis=1)
    skew = jnp.tile(vec, (1, CHUNK))[:, :CHUNK * (n_vec - 1)].reshape(B_HEADS, CHUNK, n_vec - 1)
    toep = skew[:, :, CHUNK - 1:CHUNK - 1 + x_len]
    key_chunk = jnp.arange(win)[None, None, :] // CHUNK

    def window(first_rel, lo_chunk, hi_chunk):
        x0 = r0 - first_rel
        b = jnp.where((key_chunk >= lo_chunk) & (key_chunk <= hi_chunk), toep[:, :, x0:x0 + win], NEG_INF)
        b = jnp.transpose(b.reshape(B_HEADS // 2, 2, CHUNK, win), (0, 3, 1, 2))
        return b.reshape(B_HEADS // 2, win, 2 * CHUNK)

    variants = []
    for v in range(B_WIN_BLOCKS):
        chunks = [2 * v + cc for cc in range(2)]
        variants.append(jnp.concatenate([window(c * CHUNK, c - B_PAST_CHUNKS, c) for c in chunks], axis=-1))
    bias_p = jnp.stack(variants, axis=1)
    bias_s = window(r0, 1, B_PAST_CHUNKS + 1)[:, None]
    return dict(pre=pre[None], w_in=w_in_p, q_norm=q_norm[None], kv_norm=kv_norm[None], w_uq=w_uq_p, w_k=w_k,
                w_vt=w_vt, wkt=wkt, wv_s=wv_s, sel=sel, bias_p=bias_p, bias_s=bias_s, w_out=w_out.astype(BF16),
                post=post[None])


def _prep_c(pre, post, w_in, sinks, w_out):
    q, k, v, g = jnp.split(w_in, [1024, 1152, 1280], axis=1)
    k0, k1 = k[:, :C_HEAD_DIM], k[:, C_HEAD_DIM:]
    v0, v1 = v[:, :C_HEAD_DIM], v[:, C_HEAD_DIM:]
    zero = jnp.zeros_like(v0)
    w_in_p = jnp.concatenate([q, k0, k0, k1, k1, v0, zero, v1, zero, g], axis=1).astype(BF16)
    win = C_WIN_BLOCKS * LANES
    per_chunk = C_GROUP * CHUNK
    key_chunk = jnp.arange(win)[:, None] // CHUNK

    def window(lo_chunk, hi_chunk):
        m = jnp.where((key_chunk >= lo_chunk) & (key_chunk <= hi_chunk), 0.0, NEG_INF).astype(F32)
        return jnp.broadcast_to(m, (win, per_chunk))

    variants = [jnp.concatenate([window(c - C_PAST_CHUNKS, c) for c in (2 * v, 2 * v + 1)], axis=-1)
                for v in range(C_WIN_BLOCKS)]
    mask_p = jnp.broadcast_to(jnp.stack(variants)[None], (C_KV_HEADS, C_WIN_BLOCKS, win, 2 * per_chunk))
    mask_s = jnp.broadcast_to(window(1, C_PAST_CHUNKS + 1)[None, None], (C_KV_HEADS, 1, win, per_chunk))
    sink_row = jnp.repeat(sinks.reshape(C_KV_HEADS, C_GROUP), CHUNK, axis=1)[:, None, :]
    return dict(pre=pre[None], w_in=w_in_p, mask_p=mask_p, mask_s=mask_s, sink_s=sink_row,
                sink_p=jnp.tile(sink_row, (1, 1, 2)), w_out=w_out.astype(BF16), post=post[None])


def _dup_heads(x):
    return jnp.concatenate([x[:, :, 0], x[:, :, 0], x[:, :, 1], x[:, :, 1]], axis=-1)


def kernel(x_prompt, x_sample, cache_a_ckv, cache_a_krope, cache_b_k, cache_b_v, cache_c_k, cache_c_v,
           ab_pre_norm, ab_post_norm, ab_w_in, ab_q_norm, ab_kv_norm, ab_w_uq, ab_w_ukv, ab_rel_bias, ab_w_out,
           c_pre_norm, c_post_norm, c_w_in, c_sinks, c_w_out):
    bsz, seq, _ = x_prompt.shape
    dbs, dseq, _ = x_sample.shape
    past = cache_a_ckv.shape[2]
    n_s = dbs * dseq
    pos_p = jnp.arange(seq, dtype=jnp.int32)
    pos_s = jnp.tile(past + jnp.arange(dseq, dtype=jnp.int32), dbs)
    wab = _prep_ab(ab_pre_norm[0], ab_post_norm[0], ab_w_in[0], ab_q_norm[0], ab_kv_norm[0], ab_w_uq[0],
                   ab_w_ukv[0], ab_rel_bias[0], ab_w_out[0])
    wc = _prep_c(c_pre_norm[0], c_post_norm[0], c_w_in[0], c_sinks[0], c_w_out[0])
    b_tail = min(B_PAST_CHUNKS * CHUNK, seq)
    c_tail = min(C_WINDOW, seq)
    tile = 512

    rope_a_p = _rope_tables(pos_p, A_ROPE, A_ROPE_PATTERN)
    (qa, ka, vt, qb, kb, vbt, sg, c_new_p, kr_new_p, kb_tail, vb_tail) = _ab_in_proj(
        x_prompt, rope_a_p, wab, tm=256, tail_len=b_tail, tkv=tile)
    o_a = _mla_prompt(qa, ka, vt, tq=tile)
    o_b = _band_attention(qb, kb, vbt, wab["bias_p"], None, n_groups=B_HEADS // 2, n_qblk=1, cpg=GROUP_CHUNKS,
                          ones_row=False, tq=tile)
    h1_p = _out_proj([o_a, o_b], sg, wab["w_out"], wab["post"], x_prompt, tm=tile)

    rope_a_s = _rope_tables(pos_s, A_ROPE, A_ROPE_PATTERN)
    xs = x_sample.reshape(1, n_s, D_MODEL)
    (qa_s, _, _, qb_s, _, _, sg_s, c_new_s, kr_new_s, kb_s32, vb_s32) = _ab_in_proj(
        xs, rope_a_s, wab, tm=n_s, tail_len=n_s, tkv=n_s)
    o_a_s = _mla_sample(qa_s.reshape(dbs, dseq, -1), cache_a_ckv[0], cache_a_krope[0],
                        c_new_s.reshape(dbs, dseq, -1), kr_new_s.reshape(dbs, dseq, -1),
                        wab["wkt"], wab["wv_s"], wab["sel"], tk=512)
    wb = cache_b_k.shape[2]
    pad_b = jnp.zeros((dbs, B_WIN_BLOCKS * LANES - wb - dseq, B_WIDTH), F32)
    kband = jnp.concatenate([pad_b, cache_b_k[0].reshape(dbs, wb, B_WIDTH), kb_s32.reshape(dbs, dseq, -1)], 1)
    vband = jnp.concatenate([pad_b, cache_b_v[0].reshape(dbs, wb, B_WIDTH), vb_s32.reshape(dbs, dseq, -1)], 1)
    vband_t = jnp.transpose(vband.astype(BF16).reshape(dbs, B_WIN_BLOCKS, LANES, B_HEADS // 2, LANES), (0, 3, 1, 4, 2))
    o_b_s = _band_attention(qb_s.reshape(dbs, dseq, -1), kband.astype(BF16), vband_t, wab["bias_s"], None,
                            n_groups=B_HEADS // 2, n_qblk=1, cpg=1, ones_row=False, tq=dseq)
    h1_s = _out_proj([o_a_s.reshape(1, n_s, -1), o_b_s.reshape(1, n_s, -1)], sg_s, wab["w_out"], wab["post"],
                     xs, tm=n_s)

    rope_c_p = _rope_tables(pos_p, C_ROT, C_ROPE_PATTERN)
    qc, kc, vct, sgc, kc_tail, vc_tail = _c_in_proj(h1_p, rope_c_p, wc, tm=tile, tail_len=c_tail)
    o_c = _band_attention(qc, kc, vct, wc["mask_p"], wc["sink_p"], n_groups=C_KV_HEADS, n_qblk=C_GROUP // 2,
                          cpg=GROUP_CHUNKS, ones_row=True, tq=tile)
    h2_p = _out_proj([o_c], sgc, wc["w_out"], wc["post"], h1_p, tm=tile)

    rope_c_s = _rope_tables(pos_s, C_ROT, C_ROPE_PATTERN)
    qc_s, kc_s, _, sgc_s, kc_s32, vc_s32 = _c_in_proj(h1_s, rope_c_s, wc, tm=n_s, tail_len=n_s)
    wcw = cache_c_k.shape[2]
    win_c = C_WIN_BLOCKS * LANES
    n_pad = win_c - wcw - dseq
    kcb = jnp.concatenate([jnp.zeros((dbs, n_pad, C_KV_HEADS * LANES), BF16), _dup_heads(cache_c_k[0]).astype(BF16),
                           kc_s.reshape(dbs, dseq, -1)], axis=1)
    vcb = jnp.concatenate([jnp.zeros((dbs, n_pad, C_KV_HEADS, C_HEAD_DIM), F32), cache_c_v[0],
                           vc_s32.reshape(dbs, dseq, C_KV_HEADS, C_HEAD_DIM)], axis=1)
    vcb_t = jnp.concatenate([jnp.transpose(vcb, (0, 2, 3, 1)), jnp.ones((dbs, C_KV_HEADS, 1, win_c), F32),
                             jnp.zeros((dbs, C_KV_HEADS, LANES - C_HEAD_DIM - 1, win_c), F32)], axis=2)
    vcb_t = jnp.transpose(vcb_t.astype(BF16).reshape(dbs, C_KV_HEADS, LANES, C_WIN_BLOCKS, LANES), (0, 1, 3, 2, 4))
    o_c_s = _band_attention(qc_s.reshape(dbs, dseq, -1), kcb, vcb_t, wc["mask_s"], wc["sink_s"], n_groups=C_KV_HEADS,
                            n_qblk=C_GROUP // 2, cpg=1, ones_row=True, tq=dseq)
    h2_s = _out_proj([o_c_s.reshape(1, n_s, -1)], sgc_s, wc["w_out"], wc["post"], h1_s, tm=n_s)

    def roll_in(buf, new):
        return jnp.concatenate([buf, new], axis=1)[:, -buf.shape[1]:][None]

    return (h2_p, h2_s.reshape(dbs, dseq, D_MODEL),
            c_new_p[None], kr_new_p[None],
            kb_tail.reshape(1, bsz, b_tail, B_HEADS, B_HEAD_DIM), vb_tail.reshape(1, bsz, b_tail, B_HEADS, B_HEAD_DIM),
            kc_tail.reshape(1, bsz, c_tail, C_KV_HEADS, C_HEAD_DIM), vc_tail.reshape(1, bsz, c_tail, C_KV_HEADS, C_HEAD_DIM),
            c_new_s.reshape(1, dbs, dseq, A_KV_RANK), kr_new_s.reshape(1, dbs, dseq, A_ROPE),
            roll_in(cache_b_k[0], kb_s32.reshape(dbs, dseq, B_HEADS, B_HEAD_DIM)),
            roll_in(cache_b_v[0], vb_s32.reshape(dbs, dseq, B_HEADS, B_HEAD_DIM)),
            roll_in(cache_c_k[0], kc_s32.reshape(dbs, dseq, C_KV_HEADS, C_HEAD_DIM)),
            roll_in(cache_c_v[0], vc_s32.reshape(dbs, dseq, C_KV_HEADS, C_HEAD_DIM)))
```

```python
import functools

import jax
import jax.numpy as jnp
from jax import lax
from jax.experimental import pallas as pl
from jax.experimental.pallas import tpu as pltpu

F32 = jnp.float32
BF16 = jnp.bfloat16

D_MODEL = 1024
CHUNK = 64
ROPE_THETA = 500000.0
RMS_EPS = 1e-6
NEG_INF = -1e30

A_HEADS = 8
A_NOPE = 64
A_ROPE = 32
A_QK = A_NOPE + A_ROPE
A_V = 64
A_Q_RANK = 384
A_KV_RANK = 256
A_WIDTH = A_HEADS * A_V
LOG2E = 1.4426950408889634
A_SCALE = A_QK ** -0.5 * LOG2E

B_HEADS = 8
B_HEAD_DIM = 64
B_WIDTH = B_HEADS * B_HEAD_DIM
B_PAST_CHUNKS = 8
B_MAX_REL = 128
B_SCALE = B_HEAD_DIM ** -0.5 * LOG2E

C_HEADS = 16
C_KV_HEADS = 2
C_GROUP = C_HEADS // C_KV_HEADS
C_HEAD_DIM = 64
C_WIDTH = C_HEADS * C_HEAD_DIM
C_WINDOW = 128
C_PAST_CHUNKS = C_WINDOW // CHUNK
C_ROT = C_HEAD_DIM // 4
C_SCALE = C_HEAD_DIM ** -0.5 * LOG2E

LANES = 128
HALF = LANES // 2
VMEM_LIMIT = 56 * 1024 * 1024
GROUP_CHUNKS = LANES // CHUNK
B_WIN_BLOCKS = (B_PAST_CHUNKS + GROUP_CHUNKS) * CHUNK // LANES
C_WIN_BLOCKS = (C_PAST_CHUNKS + GROUP_CHUNKS) * CHUNK // LANES

AB_Q0, AB_C0, AB_KR0, AB_G0, AB_QB0, AB_KB0, AB_VB0, AB_NZ = 0, 384, 640, 768, 1792, 2304, 2816, 3328
C_Q0, C_K0, C_V0, C_G0, C_NZ = 0, 1024, 1280, 1536, 2560


def _params(n_axes):
    return pltpu.CompilerParams(dimension_semantics=("arbitrary",) * n_axes, vmem_limit_bytes=VMEM_LIMIT)


def _rms(x, g):
    return x * lax.rsqrt(jnp.mean(x * x, axis=-1, keepdims=True) + RMS_EPS) * g


def _rope_block(blk, rope_ref, shift):
    return (blk * rope_ref[0] + pltpu.roll(blk, shift, 1) * rope_ref[1]
            + pltpu.roll(blk, LANES - shift, 1) * rope_ref[2])


def _tail_spec(n_tiles, tm, tail_len, width):
    tb = min(tail_len, tm)
    n_blk = tail_len // tb
    return tb, pl.BlockSpec((1, tb, width), lambda b, i: (b, jnp.maximum(i - (n_tiles - n_blk), 0), 0))


def _ab_in_body(x_ref, pre_ref, w_ref, qn_ref, kvn_ref, wuq_ref, wk_ref, wvt_ref, rope_ref,
                qa_ref, ka_ref, vt_ref, qb_ref, kb_ref, vbt_ref, sg_ref, c_ref, kr_ref, kb_tail_ref, vb_tail_ref):
    tm = x_ref.shape[1]
    tb = kb_tail_ref.shape[1]
    xn = _rms(x_ref[0], pre_ref[...]).astype(BF16)
    z = jnp.dot(xn, w_ref[...], preferred_element_type=F32)

    qn = _rms(z[:, AB_Q0:AB_C0], qn_ref[...]).astype(BF16)
    qa = jnp.dot(qn, wuq_ref[...], preferred_element_type=F32) * A_SCALE
    for h in range(A_HEADS):
        blk = slice(h * LANES, (h + 1) * LANES)
        qa_ref[0, :, blk] = _rope_block(qa[:, blk], rope_ref, A_ROPE // 2).astype(BF16)

    c_new = _rms(z[:, AB_C0:AB_KR0], kvn_ref[...])
    c_ref[0] = c_new
    cb = c_new.astype(BF16)
    krot = _rope_block(z[:, AB_KR0:AB_G0], rope_ref, A_ROPE // 2)
    kr_ref[0] = krot[:, A_NOPE:A_NOPE + A_ROPE]
    kn = jnp.dot(cb, wk_ref[...], preferred_element_type=F32)
    for h in range(A_HEADS):
        blk = slice(h * LANES, (h + 1) * LANES)
        ka_ref[0, :, blk] = (kn[:, blk] + krot).astype(BF16)
    vt = jnp.dot(wvt_ref[...], c_new.T.astype(BF16), preferred_element_type=F32)
    row = lax.broadcasted_iota(jnp.int32, vt.shape, 0)
    vt_ref[0, 0] = jnp.where((row & (LANES - 1)) == A_V, 1.0, vt).astype(BF16)

    g = z[:, AB_G0:AB_QB0]
    sg_ref[0] = (g * jax.nn.sigmoid(g)).astype(BF16)
    qb_ref[0] = (z[:, AB_QB0:AB_KB0] * B_SCALE).astype(BF16)
    kb = z[:, AB_KB0:AB_VB0]
    vb = z[:, AB_VB0:AB_NZ]
    kb_ref[0] = kb.astype(BF16)
    vbt = vb.T.astype(BF16)
    for hp in range(B_HEADS // 2):
        for j in range(tm // LANES):
            vbt_ref[0, hp, j] = vbt[hp * LANES:(hp + 1) * LANES, j * LANES:(j + 1) * LANES]
    kb_tail_ref[0] = kb[tm - tb:, :]
    vb_tail_ref[0] = vb[tm - tb:, :]


def _ab_in_proj(x, rope, w, tm, tail_len, tkv):
    bsz, s, _ = x.shape
    nt = s // tm
    per_kv = tkv // tm
    tb, tail_spec = _tail_spec(nt, tm, tail_len, B_WIDTH)

    def full(a):
        return pl.BlockSpec(a.shape, lambda b, i: (0,) * a.ndim)

    def rows(width):
        return pl.BlockSpec((1, tm, width), lambda b, i: (b, i, 0))

    weights = (w["pre"], w["w_in"], w["q_norm"], w["kv_norm"], w["w_uq"], w["w_k"], w["w_vt"])
    vt_spec = pl.BlockSpec((1, 1, A_HEADS * LANES, tm), lambda b, i: (b, i // per_kv, 0, i % per_kv))
    out_shape = (
        jax.ShapeDtypeStruct((bsz, s, A_HEADS * LANES), BF16),
        jax.ShapeDtypeStruct((bsz, s, A_HEADS * LANES), BF16),
        jax.ShapeDtypeStruct((bsz, s // tkv, A_HEADS * LANES, tkv), BF16),
        jax.ShapeDtypeStruct((bsz, s, B_WIDTH), BF16),
        jax.ShapeDtypeStruct((bsz, s, B_WIDTH), BF16),
        jax.ShapeDtypeStruct((bsz, B_HEADS // 2, s // LANES, LANES, LANES), BF16),
        jax.ShapeDtypeStruct((bsz, s, A_WIDTH + B_WIDTH), BF16),
        jax.ShapeDtypeStruct((bsz, s, A_KV_RANK), F32),
        jax.ShapeDtypeStruct((bsz, s, A_ROPE), F32),
        jax.ShapeDtypeStruct((bsz, tail_len, B_WIDTH), F32),
        jax.ShapeDtypeStruct((bsz, tail_len, B_WIDTH), F32),
    )
    vbt_spec = pl.BlockSpec((1, B_HEADS // 2, tm // LANES, LANES, LANES), lambda b, i: (b, 0, i, 0, 0))
    out_specs = (rows(1024), rows(1024), vt_spec, rows(512), rows(512), vbt_spec, rows(1024),
                 rows(A_KV_RANK), rows(A_ROPE), tail_spec, tail_spec)
    return pl.pallas_call(
        _ab_in_body,
        grid=(bsz, nt),
        in_specs=[rows(D_MODEL)] + [full(a) for a in weights]
        + [pl.BlockSpec((3, tm, LANES), lambda b, i: (0, i, 0))],
        out_specs=out_specs,
        out_shape=out_shape,
        compiler_params=_params(2),
        name="ab_in_proj",
    )(x, *weights, rope)


def _c_in_body(x_ref, pre_ref, w_ref, rope_ref, q_ref, k_ref, vt_ref, sg_ref, k_tail_ref, v_tail_ref):
    tm = x_ref.shape[1]
    tb = k_tail_ref.shape[1]
    xn = _rms(x_ref[0], pre_ref[...]).astype(BF16)
    z = jnp.dot(xn, w_ref[...], preferred_element_type=F32)
    half_rot = C_ROT // 2
    for j in range(C_WIDTH // LANES):
        blk = slice(C_Q0 + j * LANES, C_Q0 + (j + 1) * LANES)
        q_ref[0, :, j * LANES:(j + 1) * LANES] = (_rope_block(z[:, blk], rope_ref, half_rot) * C_SCALE).astype(BF16)
    kd = [_rope_block(z[:, C_K0 + j * LANES:C_K0 + (j + 1) * LANES], rope_ref, half_rot) for j in range(C_KV_HEADS)]
    vz = [z[:, C_V0 + j * LANES:C_V0 + (j + 1) * LANES] for j in range(C_KV_HEADS)]
    row = lax.broadcasted_iota(jnp.int32, (LANES, tm), 0)
    for j in range(C_KV_HEADS):
        k_ref[0, :, j * LANES:(j + 1) * LANES] = kd[j].astype(BF16)
        vt = jnp.where(row == C_HEAD_DIM, 1.0, vz[j].T).astype(BF16)
        for i in range(tm // LANES):
            vt_ref[0, j, i] = vt[:, i * LANES:(i + 1) * LANES]
    g = z[:, C_G0:C_NZ]
    sg_ref[0] = (g * jax.nn.sigmoid(g)).astype(BF16)
    lo = lax.broadcasted_iota(jnp.int32, (1, LANES), 1) < HALF
    k_tail_ref[0] = jnp.where(lo, kd[0], kd[1])[tm - tb:, :]
    v_tail_ref[0] = (vz[0] + pltpu.roll(vz[1], HALF, 1))[tm - tb:, :]


def _c_in_proj(x, rope, w, tm, tail_len):
    bsz, s, _ = x.shape
    nt = s // tm
    tb, tail_spec = _tail_spec(nt, tm, tail_len, LANES)

    def full(a):
        return pl.BlockSpec(a.shape, lambda b, i: (0,) * a.ndim)

    def rows(width):
        return pl.BlockSpec((1, tm, width), lambda b, i: (b, i, 0))

    out_shape = (
        jax.ShapeDtypeStruct((bsz, s, C_WIDTH), BF16),
        jax.ShapeDtypeStruct((bsz, s, C_KV_HEADS * LANES), BF16),
        jax.ShapeDtypeStruct((bsz, C_KV_HEADS, s // LANES, LANES, LANES), BF16),
        jax.ShapeDtypeStruct((bsz, s, C_WIDTH), BF16),
        jax.ShapeDtypeStruct((bsz, tail_len, LANES), F32),
        jax.ShapeDtypeStruct((bsz, tail_len, LANES), F32),
    )
    return pl.pallas_call(
        _c_in_body,
        grid=(bsz, nt),
        in_specs=[rows(D_MODEL), full(w["pre"]), full(w["w_in"]),
                  pl.BlockSpec((3, tm, LANES), lambda b, i: (0, i, 0))],
        out_specs=(rows(C_WIDTH), rows(256),
                   pl.BlockSpec((1, C_KV_HEADS, tm // LANES, LANES, LANES), lambda b, i: (b, 0, i, 0, 0)),
                   rows(C_WIDTH), tail_spec, tail_spec),
        out_shape=out_shape,
        compiler_params=_params(2),
        name="c_in_proj",
    )(x, w["pre"], w["w_in"], rope)


def _out_body(*refs):
    o_refs, (sg_ref, w_ref, g_ref, h_ref, out_ref) = refs[:-5], refs[-5:]
    o = jnp.concatenate([r[0].astype(F32) for r in o_refs], axis=-1) if len(o_refs) > 1 else o_refs[0][0].astype(F32)
    mixed = (o * sg_ref[0].astype(F32)).astype(BF16)
    y = jnp.dot(mixed, w_ref[...], preferred_element_type=F32)
    out_ref[0] = h_ref[0] + _rms(y, g_ref[...])


def _out_proj(o_parts, sg, w_out, post_g, h, tm):
    bsz, s, _ = h.shape

    def rows(width):
        return pl.BlockSpec((1, tm, width), lambda b, i: (b, i, 0))

    def full(a):
        return pl.BlockSpec(a.shape, lambda b, i: (0,) * a.ndim)

    return pl.pallas_call(
        _out_body,
        grid=(bsz, s // tm),
        in_specs=[rows(o.shape[-1]) for o in o_parts] + [rows(D_MODEL), full(w_out), full(post_g), rows(D_MODEL)],
        out_specs=rows(D_MODEL),
        out_shape=jax.ShapeDtypeStruct(h.shape, F32),
        compiler_params=_params(2),
        name="out_proj",
    )(*o_parts, sg, w_out, post_g, h)


def _mla_prompt_body(q_ref, k_ref, vt_ref, o_ref, m_sc, acc_sc, s_sc):
    tq = q_ref.shape[1]
    qi = pl.program_id(2)
    m_sc[...] = jnp.full(m_sc.shape, NEG_INF, F32)
    acc_sc[...] = jnp.zeros(acc_sc.shape, F32)

    def scores(j, slot, masked):
        start = pl.multiple_of(j * tq, tq)
        for hh in range(2):
            blk = slice(hh * LANES, (hh + 1) * LANES)
            s = lax.dot_general(k_ref[0, pl.ds(start, tq), blk], q_ref[0, :, blk], (((1,), (1,)), ((), ())),
                                preferred_element_type=F32)
            if masked:
                kc = lax.broadcasted_iota(jnp.int32, (tq, tq), 0) // CHUNK
                qc = lax.broadcasted_iota(jnp.int32, (tq, tq), 1) // CHUNK
                s = jnp.where(kc <= qc, s, NEG_INF)
            s_sc[slot, hh] = s

    def consume(j, slot):
        for hh in range(2):
            s = s_sc[slot, hh]
            vt = vt_ref[0, j, hh * LANES:(hh + 1) * LANES, :]
            m_old = m_sc[hh]
            m_new = jnp.maximum(m_old, jnp.max(s, axis=0, keepdims=True))
            p = jnp.exp2(s - m_new).astype(BF16)
            acc_sc[hh] = jnp.exp2(m_old - m_new) * acc_sc[hh] + jnp.dot(vt, p, preferred_element_type=F32)
            m_sc[hh] = m_new

    @pl.when(qi == 0)
    def _():
        scores(0, 0, True)
        consume(0, 0)

    def stage(j_next, masked_next, j_cur, cur_slot):
        scores(j_next, 1 - cur_slot, masked_next)
        consume(j_cur, cur_slot)

    @pl.when(qi > 0)
    def _():
        scores(0, 0, False)

        def pair_body(i, carry):
            j = 2 * i
            stage(j + 1, False, j, 0)
            stage(j + 2, False, j + 1, 1)
            return carry

        lax.fori_loop(0, (qi - 1) // 2, pair_body, 0)
        odd = ((qi - 1) & 1) == 1

        @pl.when(odd)
        def _():
            stage(qi - 1, False, qi - 2, 0)
            stage(qi, True, qi - 1, 1)
            consume(qi, 0)

        @pl.when(jnp.logical_not(odd))
        def _():
            stage(qi, True, qi - 1, 0)
            consume(qi, 1)

    outs = []
    for hh in range(2):
        acc = acc_sc[hh]
        outs.append((acc / acc[A_V:A_V + 1, :]).T)
    lo = lax.broadcasted_iota(jnp.int32, (1, LANES), 1) < HALF
    o_ref[0] = jnp.where(lo, outs[0], pltpu.roll(outs[1], HALF, 1)).astype(o_ref.dtype)


def _mla_prompt(qa, ka, vt, tq):
    bsz, s, _ = qa.shape
    pairs = A_HEADS // 2
    assert vt.shape[3] == tq
    return pl.pallas_call(
        _mla_prompt_body,
        grid=(bsz, pairs, s // tq),
        in_specs=[pl.BlockSpec((1, tq, 2 * LANES), lambda b, h, i: (b, i, h)),
                  pl.BlockSpec((1, s, 2 * LANES), lambda b, h, i: (b, 0, h)),
                  pl.BlockSpec((1, s // tq, 2 * LANES, tq), lambda b, h, i: (b, 0, h, 0))],
        out_specs=pl.BlockSpec((1, tq, LANES), lambda b, h, i: (b, i, h)),
        out_shape=jax.ShapeDtypeStruct((bsz, s, A_WIDTH), BF16),
        scratch_shapes=[pltpu.VMEM((2, 1, tq), F32), pltpu.VMEM((2, LANES, tq), F32),
                        pltpu.VMEM((2, 2, tq, tq), F32)],
        compiler_params=_params(3),
        name="mla_prompt",
    )(qa, ka, vt)


def _mla_sample_body(q_ref, cc_ref, ckr_ref, cn_ref, krn_ref, wkt_ref, wv_ref, sel_ref, o_ref,
                     qabs_sc, qr_sc, m_sc, l_sc, acc_sc, *, tk):
    t = q_ref.shape[1]
    past = cc_ref.shape[1]
    for h in range(A_HEADS):
        qh = q_ref[0, :, h * LANES:(h + 1) * LANES]
        rows = slice(h * t, (h + 1) * t)
        qabs_sc[rows, :] = jnp.dot(qh, wkt_ref[h], preferred_element_type=F32).astype(BF16)
        qr_sc[rows, :] = jnp.dot(qh, sel_ref[...], preferred_element_type=F32).astype(BF16)
    m_sc[...] = jnp.full(m_sc.shape, NEG_INF, F32)
    l_sc[...] = jnp.zeros(l_sc.shape, F32)
    acc_sc[...] = jnp.zeros(acc_sc.shape, F32)
    contract_last = (((1,), (1,)), ((), ()))

    def update(c_t, kr_t):
        s = (lax.dot_general(qabs_sc[...], c_t, contract_last, preferred_element_type=F32)
             + lax.dot_general(qr_sc[...], kr_t, contract_last, preferred_element_type=F32))
        m_old = m_sc[...]
        m_new = jnp.maximum(m_old, jnp.max(s, axis=-1, keepdims=True))
        p = jnp.exp2(s - m_new)
        alpha = jnp.exp2(m_old - m_new)
        l_sc[...] = alpha * l_sc[...] + jnp.sum(p, axis=-1, keepdims=True)
        acc_sc[...] = alpha * acc_sc[...] + jnp.dot(p.astype(BF16), c_t, preferred_element_type=F32)
        m_sc[...] = m_new

    def loop_body(j, carry):
        start = pl.multiple_of(j * tk, tk)
        update(cc_ref[0, pl.ds(start, tk), :].astype(BF16), ckr_ref[0, pl.ds(start, tk), :].astype(BF16))
        return carry

    lax.fori_loop(0, past // tk, loop_body, 0)
    update(cn_ref[0].astype(BF16), krn_ref[0].astype(BF16))

    o_lat = (acc_sc[...] / l_sc[...]).astype(BF16)
    out = jnp.zeros((t, A_WIDTH), F32)
    for h in range(A_HEADS):
        out = out + jnp.dot(o_lat[h * t:(h + 1) * t, :], wv_ref[h], preferred_element_type=F32)
    o_ref[0] = out.astype(o_ref.dtype)


def _mla_sample(qa, cache_c, cache_kr, c_new, kr_new, wkt, wv, sel, tk):
    bsz, t, _ = qa.shape
    past = cache_c.shape[1]

    def per_b(shape):
        return pl.BlockSpec((1,) + shape, lambda b: (b, 0, 0))

    def full(a):
        return pl.BlockSpec(a.shape, lambda b: (0,) * a.ndim)

    rows = A_HEADS * t
    return pl.pallas_call(
        functools.partial(_mla_sample_body, tk=tk),
        grid=(bsz,),
        in_specs=[per_b((t, A_HEADS * LANES)), per_b((past, A_KV_RANK)), per_b((past, A_ROPE)),
                  per_b((t, A_KV_RANK)), per_b((t, A_ROPE)), full(wkt), full(wv), full(sel)],
        out_specs=per_b((t, A_WIDTH)),
        out_shape=jax.ShapeDtypeStruct((bsz, t, A_WIDTH), BF16),
        scratch_shapes=[pltpu.VMEM((rows, A_KV_RANK), BF16), pltpu.VMEM((rows, A_ROPE), BF16),
                        pltpu.VMEM((rows, 1), F32), pltpu.VMEM((rows, 1), F32),
                        pltpu.VMEM((rows, A_KV_RANK), F32)],
        compiler_params=_params(1),
        name="mla_sample",
    )(qa, cache_c, cache_kr, c_new, kr_new, wkt, wv, sel)


def _band_body(*refs, n_blk, n_qblk, cpg, ones_row, has_sink):
    if has_sink:
        q_ref, k_ref, vt_ref, bias_ref, sink_ref, o_ref = refs
    else:
        q_ref, k_ref, vt_ref, bias_ref, o_ref = refs
    tq = q_ref.shape[1]
    win = n_blk * LANES
    t = pl.program_id(2)
    lo = lax.broadcasted_iota(jnp.int32, (CHUNK, LANES), 1) < HALF
    contract_last = (((1,), (1,)), ((), ()))

    def scores(gl, wb, variant):
        pieces = []
        for cc in range(cpg):
            rows = slice((gl * cpg + cc) * CHUNK, (gl * cpg + cc + 1) * CHUNK)
            for r in range(n_qblk):
                qblk = q_ref[0, rows, r * LANES:(r + 1) * LANES]
                zero = jnp.zeros_like(qblk)
                pieces += [jnp.where(lo, qblk, zero), jnp.where(lo, zero, qblk)]
        qs = jnp.concatenate(pieces, axis=0)
        start = wb * LANES if isinstance(wb, int) else pl.multiple_of(wb * LANES, LANES)
        kw = k_ref[0, pl.ds(start, win), :]
        return lax.dot_general(kw, qs, contract_last, preferred_element_type=F32) + bias_ref[0, variant]

    def finish(gl, wb, s):
        m = jnp.max(s, axis=0, keepdims=True)
        if has_sink:
            sink = sink_ref[0]
            m = jnp.maximum(m, sink)
        p = jnp.exp2(s - m)
        vt = jnp.concatenate([vt_ref[0, 0, wb + i] for i in range(n_blk)], axis=1)
        o = jnp.dot(vt, p.astype(BF16), preferred_element_type=F32)
        l = o[CHUNK:CHUNK + 1, :] if ones_row else jnp.sum(p, axis=0, keepdims=True)
        if has_sink:
            l = l + jnp.exp2(sink - m)
        o = (o / l).T
        idx = 0
        for cc in range(cpg):
            rows = slice((gl * cpg + cc) * CHUNK, (gl * cpg + cc + 1) * CHUNK)
            for r in range(n_qblk):
                top = o[idx * CHUNK:(idx + 1) * CHUNK]
                bot = o[(idx + 1) * CHUNK:(idx + 2) * CHUNK]
                if ones_row:
                    bot = pltpu.roll(bot, HALF, 1)
                o_ref[0, rows, r * LANES:(r + 1) * LANES] = jnp.where(lo, top, bot).astype(o_ref.dtype)
                idx += 2

    n_groups = tq // (cpg * CHUNK)
    if bias_ref.shape[1] == 1:
        finish(0, 0, scores(0, 0, 0))
        return
    assert n_groups >= n_blk - 1

    def run(window_block, variant):
        s = scores(0, window_block(0), variant(0))
        for gl in range(n_groups):
            s_next = scores(gl + 1, window_block(gl + 1), variant(gl + 1)) if gl + 1 < n_groups else None
            finish(gl, window_block(gl), s)
            s = s_next

    @pl.when(t == 0)
    def _():
        run(lambda gl: max(gl - (n_blk - 1), 0), lambda gl: min(gl, n_blk - 1))

    @pl.when(t > 0)
    def _():
        run(lambda gl: t * n_groups + gl - (n_blk - 1), lambda gl: n_blk - 1)


def _band_attention(q, k, vt, bias, sink, *, n_groups, n_qblk, cpg, ones_row, tq):
    bsz, s, _ = q.shape
    sk = k.shape[1]
    n_blk = bias.shape[2] // LANES
    qw = n_qblk * LANES
    in_specs = [pl.BlockSpec((1, tq, qw), lambda b, g, i: (b, i, g)),
                pl.BlockSpec((1, sk, LANES), lambda b, g, i: (b, 0, g)),
                pl.BlockSpec((1, 1) + vt.shape[2:], lambda b, g, i: (b, g, 0, 0, 0)),
                pl.BlockSpec((1,) + bias.shape[1:], lambda b, g, i: (g, 0, 0, 0))]
    args = [q, k, vt, bias]
    if sink is not None:
        in_specs.append(pl.BlockSpec((1,) + sink.shape[1:], lambda b, g, i: (g, 0, 0)))
        args.append(sink)
    body = functools.partial(_band_body, n_blk=n_blk, n_qblk=n_qblk, cpg=cpg, ones_row=ones_row,
                             has_sink=sink is not None)
    return pl.pallas_call(
        body,
        grid=(bsz, n_groups, s // tq),
        in_specs=in_specs,
        out_specs=pl.BlockSpec((1, tq, qw), lambda b, g, i: (b, i, g)),
        out_shape=jax.ShapeDtypeStruct(q.shape, BF16),
        compiler_params=_params(3),
        name="band_attention",
    )(*args)


def _rope_tables(pos, rot, lane_pattern):
    half = rot // 2
    inv = jnp.power(ROPE_THETA, -jnp.arange(half, dtype=F32) * 2.0 / rot)
    ang = pos.astype(F32)[:, None] * inv[None, :]
    cos, sin = jnp.cos(ang), jnp.sin(ang)
    n = pos.shape[0]
    one, zero = jnp.ones((n, 1), F32), jnp.zeros((n, 1), F32)

    def build(first, second, fill):
        cols = []
        for kind, width in lane_pattern:
            if kind == "rot":
                cols += [first, second]
            else:
                cols.append(jnp.broadcast_to(fill, (n, width)))
        return jnp.concatenate(cols, axis=1)

    zeros_h = jnp.zeros_like(sin)
    return jnp.stack([build(cos, cos, one), build(zeros_h, sin, zero), build(-sin, zeros_h, zero)])


A_ROPE_PATTERN = (("pad", A_NOPE), ("rot", A_ROPE), ("pad", LANES - A_QK))
C_ROPE_PATTERN = (("rot", C_ROT), ("pad", HALF - C_ROT)) * 2


def _prep_ab(pre, post, w_in, q_norm, kv_norm, w_uq, w_ukv, rel_bias, w_out):
    d = w_in.shape[0]
    q_lat, c_kv, k_r, g_a, q_b, k_b, v_b, g_b = jnp.split(
        w_in, [384, 640, 672, 1184, 1696, 2208, 2720], axis=1)
    kr_blk = jnp.concatenate([jnp.zeros((d, A_NOPE), F32), k_r, jnp.zeros((d, LANES - A_QK), F32)], axis=1)
    w_in_p = jnp.concatenate([q_lat, c_kv, kr_blk, g_a, g_b, q_b, k_b, v_b], axis=1).astype(BF16)
    w_uq_p = jnp.pad(w_uq.reshape(A_Q_RANK, A_HEADS, A_QK), ((0, 0), (0, 0), (0, LANES - A_QK)))
    w_uq_p = w_uq_p.reshape(A_Q_RANK, A_HEADS * LANES).astype(BF16)
    ukv = w_ukv.reshape(A_KV_RANK, A_HEADS, A_NOPE + A_V)
    w_uk, w_uv = ukv[..., :A_NOPE], ukv[..., A_NOPE:]
    pad_half = ((0, 0), (0, 0), (0, LANES - A_NOPE))
    w_k = jnp.pad(w_uk, pad_half).reshape(A_KV_RANK, A_HEADS * LANES).astype(BF16)
    w_vt = jnp.pad(w_uv, pad_half).reshape(A_KV_RANK, A_HEADS * LANES).T.astype(BF16)
    wkt = jnp.pad(jnp.transpose(w_uk, (1, 2, 0)), ((0, 0), (0, LANES - A_NOPE), (0, 0))).astype(BF16)
    eye = jnp.eye(A_HEADS, dtype=F32)
    wv_s = (jnp.transpose(w_uv, (1, 0, 2))[:, :, None, :] * eye[:, None, :, None]).reshape(
        A_HEADS, A_KV_RANK, A_WIDTH).astype(BF16)
    sel = (jnp.arange(LANES)[:, None] == A_NOPE + jnp.arange(A_ROPE)[None, :]).astype(BF16)
    rel_bias = rel_bias * LOG2E
    win = B_WIN_BLOCKS * LANES
    r0 = win - CHUNK
    x_len = r0 + win
    n_vec = x_len + CHUNK
    n_hi = r0 + CHUNK - 1 - B_MAX_REL
    n_lo = n_vec - n_hi - (2 * B_MAX_REL + 1)
    vec = jnp.concatenate([jnp.broadcast_to(rel_bias[:, -1:], (B_HEADS, n_hi)), rel_bias[:, ::-1],
                           jnp.broadcast_to(rel_bias[:, :1], (B_HEADS, n_lo))], axis=1)
    skew = jnp.tile(vec, (1, CHUNK))[:, :CHUNK * (n_vec - 1)].reshape(B_HEADS, CHUNK, n_vec - 1)
    toep = skew[:, :, CHUNK - 1:CHUNK - 1 + x_len]
    key_chunk = jnp.arange(win)[None, None, :] // CHUNK

    def window(first_rel, lo_chunk, hi_chunk):
        x0 = r0 - first_rel
        b = jnp.where((key_chunk >= lo_chunk) & (key_chunk <= hi_chunk), toep[:, :, x0:x0 + win], NEG_INF)
        b = jnp.transpose(b.reshape(B_HEADS // 2, 2, CHUNK, win), (0, 3, 1, 2))
        return b.reshape(B_HEADS // 2, win, 2 * CHUNK)

    variants = []
    for v in range(B_WIN_BLOCKS):
        chunks = [2 * v + cc for cc in range(2)]
        variants.append(jnp.concatenate([window(c * CHUNK, c - B_PAST_CHUNKS, c) for c in chunks], axis=-1))
    bias_p = jnp.stack(variants, axis=1)
    bias_s = window(r0, 1, B_PAST_CHUNKS + 1)[:, None]
    return dict(pre=pre[None], w_in=w_in_p, q_norm=q_norm[None], kv_norm=kv_norm[None], w_uq=w_uq_p, w_k=w_k,
                w_vt=w_vt, wkt=wkt, wv_s=wv_s, sel=sel, bias_p=bias_p, bias_s=bias_s, w_out=w_out.astype(BF16),
                post=post[None])


def _prep_c(pre, post, w_in, sinks, w_out):
    q, k, v, g = jnp.split(w_in, [1024, 1152, 1280], axis=1)
    k0, k1 = k[:, :C_HEAD_DIM], k[:, C_HEAD_DIM:]
    v0, v1 = v[:, :C_HEAD_DIM], v[:, C_HEAD_DIM:]
    zero = jnp.zeros_like(v0)
    w_in_p = jnp.concatenate([q, k0, k0, k1, k1, v0, zero, v1, zero, g], axis=1).astype(BF16)
    win = C_WIN_BLOCKS * LANES
    per_chunk = C_GROUP * CHUNK
    key_chunk = jnp.arange(win)[:, None] // CHUNK

    def window(lo_chunk, hi_chunk):
        m = jnp.where((key_chunk >= lo_chunk) & (key_chunk <= hi_chunk), 0.0, NEG_INF).astype(F32)
        return jnp.broadcast_to(m, (win, per_chunk))

    variants = [jnp.concatenate([window(c - C_PAST_CHUNKS, c) for c in (2 * v, 2 * v + 1)], axis=-1)
                for v in range(C_WIN_BLOCKS)]
    mask_p = jnp.broadcast_to(jnp.stack(variants)[None], (C_KV_HEADS, C_WIN_BLOCKS, win, 2 * per_chunk))
    mask_s = jnp.broadcast_to(window(1, C_PAST_CHUNKS + 1)[None, None], (C_KV_HEADS, 1, win, per_chunk))
    sink_row = jnp.repeat((sinks * LOG2E).reshape(C_KV_HEADS, C_GROUP), CHUNK, axis=1)[:, None, :]
    return dict(pre=pre[None], w_in=w_in_p, mask_p=mask_p, mask_s=mask_s, sink_s=sink_row,
                sink_p=jnp.tile(sink_row, (1, 1, 2)), w_out=w_out.astype(BF16), post=post[None])


def _dup_heads(x):
    return jnp.concatenate([x[:, :, 0], x[:, :, 0], x[:, :, 1], x[:, :, 1]], axis=-1)


def kernel(x_prompt, x_sample, cache_a_ckv, cache_a_krope, cache_b_k, cache_b_v, cache_c_k, cache_c_v,
           ab_pre_norm, ab_post_norm, ab_w_in, ab_q_norm, ab_kv_norm, ab_w_uq, ab_w_ukv, ab_rel_bias, ab_w_out,
           c_pre_norm, c_post_norm, c_w_in, c_sinks, c_w_out):
    bsz, seq, _ = x_prompt.shape
    dbs, dseq, _ = x_sample.shape
    past = cache_a_ckv.shape[2]
    n_s = dbs * dseq
    pos_p = jnp.arange(seq, dtype=jnp.int32)
    pos_s = jnp.tile(past + jnp.arange(dseq, dtype=jnp.int32), dbs)
    wab = _prep_ab(ab_pre_norm[0], ab_post_norm[0], ab_w_in[0], ab_q_norm[0], ab_kv_norm[0], ab_w_uq[0],
                   ab_w_ukv[0], ab_rel_bias[0], ab_w_out[0])
    wc = _prep_c(c_pre_norm[0], c_post_norm[0], c_w_in[0], c_sinks[0], c_w_out[0])
    b_tail = min(B_PAST_CHUNKS * CHUNK, seq)
    c_tail = min(C_WINDOW, seq)
    tile = 512

    rope_a_p = _rope_tables(pos_p, A_ROPE, A_ROPE_PATTERN)
    (qa, ka, vt, qb, kb, vbt, sg, c_new_p, kr_new_p, kb_tail, vb_tail) = _ab_in_proj(
        x_prompt, rope_a_p, wab, tm=256, tail_len=b_tail, tkv=tile)
    o_a = _mla_prompt(qa, ka, vt, tq=tile)
    o_b = _band_attention(qb, kb, vbt, wab["bias_p"], None, n_groups=B_HEADS // 2, n_qblk=1, cpg=GROUP_CHUNKS,
                          ones_row=False, tq=tile)
    h1_p = _out_proj([o_a, o_b], sg, wab["w_out"], wab["post"], x_prompt, tm=tile)

    rope_a_s = _rope_tables(pos_s, A_ROPE, A_ROPE_PATTERN)
    xs = x_sample.reshape(1, n_s, D_MODEL)
    (qa_s, _, _, qb_s, _, _, sg_s, c_new_s, kr_new_s, kb_s32, vb_s32) = _ab_in_proj(
        xs, rope_a_s, wab, tm=n_s, tail_len=n_s, tkv=n_s)
    o_a_s = _mla_sample(qa_s.reshape(dbs, dseq, -1), cache_a_ckv[0], cache_a_krope[0],
                        c_new_s.reshape(dbs, dseq, -1), kr_new_s.reshape(dbs, dseq, -1),
                        wab["wkt"], wab["wv_s"], wab["sel"], tk=512)
    wb = cache_b_k.shape[2]
    pad_b = jnp.zeros((dbs, B_WIN_BLOCKS * LANES - wb - dseq, B_WIDTH), F32)
    kband = jnp.concatenate([pad_b, cache_b_k[0].reshape(dbs, wb, B_WIDTH), kb_s32.reshape(dbs, dseq, -1)], 1)
    vband = jnp.concatenate([pad_b, cache_b_v[0].reshape(dbs, wb, B_WIDTH), vb_s32.reshape(dbs, dseq, -1)], 1)
    vband_t = jnp.transpose(vband.astype(BF16).reshape(dbs, B_WIN_BLOCKS, LANES, B_HEADS // 2, LANES), (0, 3, 1, 4, 2))
    o_b_s = _band_attention(qb_s.reshape(dbs, dseq, -1), kband.astype(BF16), vband_t, wab["bias_s"], None,
                            n_groups=B_HEADS // 2, n_qblk=1, cpg=1, ones_row=False, tq=dseq)
    h1_s = _out_proj([o_a_s.reshape(1, n_s, -1), o_b_s.reshape(1, n_s, -1)], sg_s, wab["w_out"], wab["post"],
                     xs, tm=n_s)

    rope_c_p = _rope_tables(pos_p, C_ROT, C_ROPE_PATTERN)
    qc, kc, vct, sgc, kc_tail, vc_tail = _c_in_proj(h1_p, rope_c_p, wc, tm=tile, tail_len=c_tail)
    o_c = _band_attention(qc, kc, vct, wc["mask_p"], wc["sink_p"], n_groups=C_KV_HEADS, n_qblk=C_GROUP // 2,
                          cpg=GROUP_CHUNKS, ones_row=True, tq=tile)
    h2_p = _out_proj([o_c], sgc, wc["w_out"], wc["post"], h1_p, tm=tile)

    rope_c_s = _rope_tables(pos_s, C_ROT, C_ROPE_PATTERN)
    qc_s, kc_s, _, sgc_s, kc_s32, vc_s32 = _c_in_proj(h1_s, rope_c_s, wc, tm=n_s, tail_len=n_s)
    wcw = cache_c_k.shape[2]
    win_c = C_WIN_BLOCKS * LANES
    n_pad = win_c - wcw - dseq
    kcb = jnp.concatenate([jnp.zeros((dbs, n_pad, C_KV_HEADS * LANES), BF16), _dup_heads(cache_c_k[0]).astype(BF16),
                           kc_s.reshape(dbs, dseq, -1)], axis=1)
    vcb = jnp.concatenate([jnp.zeros((dbs, n_pad, C_KV_HEADS, C_HEAD_DIM), F32), cache_c_v[0],
                           vc_s32.reshape(dbs, dseq, C_KV_HEADS, C_HEAD_DIM)], axis=1)
    vcb_t = jnp.concatenate([jnp.transpose(vcb, (0, 2, 3, 1)), jnp.ones((dbs, C_KV_HEADS, 1, win_c), F32),
                             jnp.zeros((dbs, C_KV_HEADS, LANES - C_HEAD_DIM - 1, win_c), F32)], axis=2)
    vcb_t = jnp.transpose(vcb_t.astype(BF16).reshape(dbs, C_KV_HEADS, LANES, C_WIN_BLOCKS, LANES), (0, 1, 3, 2, 4))
    o_c_s = _band_attention(qc_s.reshape(dbs, dseq, -1), kcb, vcb_t, wc["mask_s"], wc["sink_s"], n_groups=C_KV_HEADS,
                            n_qblk=C_GROUP // 2, cpg=1, ones_row=True, tq=dseq)
    h2_s = _out_proj([o_c_s.reshape(1, n_s, -1)], sgc_s, wc["w_out"], wc["post"], h1_s, tm=n_s)

    def roll_in(buf, new):
        return jnp.concatenate([buf, new], axis=1)[:, -buf.shape[1]:][None]

    return (h2_p, h2_s.reshape(dbs, dseq, D_MODEL),
            c_new_p[None], kr_new_p[None],
            kb_tail.reshape(1, bsz, b_tail, B_HEADS, B_HEAD_DIM), vb_tail.reshape(1, bsz, b_tail, B_HEADS, B_HEAD_DIM),
            kc_tail.reshape(1, bsz, c_tail, C_KV_HEADS, C_HEAD_DIM), vc_tail.reshape(1, bsz, c_tail, C_KV_HEADS, C_HEAD_DIM),
            c_new_s.reshape(1, dbs, dseq, A_KV_RANK), kr_new_s.reshape(1, dbs, dseq, A_ROPE),
            roll_in(cache_b_k[0], kb_s32.reshape(dbs, dseq, B_HEADS, B_HEAD_DIM)),
            roll_in(cache_b_v[0], vb_s32.reshape(dbs, dseq, B_HEADS, B_HEAD_DIM)),
            roll_in(cache_c_k[0], kc_s32.reshape(dbs, dseq, C_KV_HEADS, C_HEAD_DIM)),
            roll_in(cache_c_v[0], vc_s32.reshape(dbs, dseq, C_KV_HEADS, C_HEAD_DIM)))
```

```python
import functools

import jax
import jax.numpy as jnp
import numpy as np
from jax import lax
from jax.experimental import pallas as pl
from jax.experimental.pallas import tpu as pltpu

F32 = jnp.float32
BF16 = jnp.bfloat16

D_MODEL = 1024
CHUNK = 64
ROPE_THETA = 500000.0
RMS_EPS = 1e-6
NEG_INF = -1e30

A_HEADS = 8
A_NOPE = 64
A_ROPE = 32
A_QK = A_NOPE + A_ROPE
A_V = 64
A_Q_RANK = 384
A_KV_RANK = 256
A_WIDTH = A_HEADS * A_V
LOG2E = 1.4426950408889634
A_SCALE = A_QK ** -0.5 * LOG2E

B_HEADS = 8
B_HEAD_DIM = 64
B_WIDTH = B_HEADS * B_HEAD_DIM
B_PAST_CHUNKS = 8
B_MAX_REL = 128
B_SCALE = B_HEAD_DIM ** -0.5 * LOG2E

C_HEADS = 16
C_KV_HEADS = 2
C_GROUP = C_HEADS // C_KV_HEADS
C_HEAD_DIM = 64
C_WIDTH = C_HEADS * C_HEAD_DIM
C_WINDOW = 128
C_PAST_CHUNKS = C_WINDOW // CHUNK
C_ROT = C_HEAD_DIM // 4
C_SCALE = C_HEAD_DIM ** -0.5 * LOG2E

LANES = 128
HALF = LANES // 2
VMEM_LIMIT = 56 * 1024 * 1024
GROUP_CHUNKS = LANES // CHUNK
B_WIN_BLOCKS = (B_PAST_CHUNKS + GROUP_CHUNKS) * CHUNK // LANES
C_WIN_BLOCKS = (C_PAST_CHUNKS + GROUP_CHUNKS) * CHUNK // LANES

AB_Q0, AB_C0, AB_KR0, AB_G0, AB_QB0, AB_KB0, AB_VB0, AB_NZ = 0, 384, 640, 768, 1792, 2304, 2816, 3328
C_Q0, C_K0, C_V0, C_G0, C_NZ = 0, 1024, 1280, 1536, 2560


def _params(n_axes):
    return pltpu.CompilerParams(dimension_semantics=("arbitrary",) * n_axes, vmem_limit_bytes=VMEM_LIMIT)


def _rms(x, g):
    return x * lax.rsqrt(jnp.mean(x * x, axis=-1, keepdims=True) + RMS_EPS) * g


def _rope_block(blk, rope_ref, shift):
    return (blk * rope_ref[0] + pltpu.roll(blk, shift, 1) * rope_ref[1]
            + pltpu.roll(blk, LANES - shift, 1) * rope_ref[2])


def _tail_spec(n_tiles, tm, tail_len, width):
    tb = min(tail_len, tm)
    n_blk = tail_len // tb
    return tb, pl.BlockSpec((1, tb, width), lambda b, i: (b, jnp.maximum(i - (n_tiles - n_blk), 0), 0))


def _ab_in_body(x_ref, pre_ref, w_ref, qn_ref, kvn_ref, wuq_ref, wk_ref, wvt_ref, rope_ref,
                qa_ref, ka_ref, vt_ref, qb_ref, kb_ref, vbt_ref, sg_ref, c_ref, kr_ref, kb_tail_ref, vb_tail_ref):
    tm = x_ref.shape[1]
    tb = kb_tail_ref.shape[1]
    xn = _rms(x_ref[0], pre_ref[...]).astype(BF16)
    z = jnp.dot(xn, w_ref[...], preferred_element_type=F32)

    qn = _rms(z[:, AB_Q0:AB_C0], qn_ref[...]).astype(BF16)
    qa = jnp.dot(qn, wuq_ref[...], preferred_element_type=F32) * A_SCALE
    for h in range(A_HEADS):
        blk = slice(h * LANES, (h + 1) * LANES)
        qa_ref[0, :, blk] = _rope_block(qa[:, blk], rope_ref, A_ROPE // 2).astype(BF16)

    c_new = _rms(z[:, AB_C0:AB_KR0], kvn_ref[...])
    c_ref[0] = c_new
    cb = c_new.astype(BF16)
    krot = _rope_block(z[:, AB_KR0:AB_G0], rope_ref, A_ROPE // 2)
    kr_ref[0] = krot[:, A_NOPE:A_NOPE + A_ROPE]
    kn = jnp.dot(cb, wk_ref[...], preferred_element_type=F32)
    for h in range(A_HEADS):
        blk = slice(h * LANES, (h + 1) * LANES)
        ka_ref[0, :, blk] = (kn[:, blk] + krot).astype(BF16)
    vt = jnp.dot(wvt_ref[...], c_new.T.astype(BF16), preferred_element_type=F32)
    row = lax.broadcasted_iota(jnp.int32, vt.shape, 0)
    vt_ref[0, 0] = jnp.where((row & (LANES - 1)) == A_V, 1.0, vt).astype(BF16)

    g = z[:, AB_G0:AB_QB0]
    sg_ref[0] = (g * jax.nn.sigmoid(g)).astype(BF16)
    qb_ref[0] = (z[:, AB_QB0:AB_KB0] * B_SCALE).astype(BF16)
    kb = z[:, AB_KB0:AB_VB0]
    vb = z[:, AB_VB0:AB_NZ]
    kb_ref[0] = kb.astype(BF16)
    vbt = vb.T.astype(BF16)
    for hp in range(B_HEADS // 2):
        for j in range(tm // LANES):
            vbt_ref[0, hp, j] = vbt[hp * LANES:(hp + 1) * LANES, j * LANES:(j + 1) * LANES]
    kb_tail_ref[0] = kb[tm - tb:, :]
    vb_tail_ref[0] = vb[tm - tb:, :]


def _ab_in_proj(x, rope, w, tm, tail_len, tkv):
    bsz, s, _ = x.shape
    nt = s // tm
    per_kv = tkv // tm
    tb, tail_spec = _tail_spec(nt, tm, tail_len, B_WIDTH)

    def full(a):
        return pl.BlockSpec(a.shape, lambda b, i: (0,) * a.ndim)

    def rows(width):
        return pl.BlockSpec((1, tm, width), lambda b, i: (b, i, 0))

    weights = (w["pre"], w["w_in"], w["q_norm"], w["kv_norm"], w["w_uq"], w["w_k"], w["w_vt"])
    vt_spec = pl.BlockSpec((1, 1, A_HEADS * LANES, tm), lambda b, i: (b, i // per_kv, 0, i % per_kv))
    out_shape = (
        jax.ShapeDtypeStruct((bsz, s, A_HEADS * LANES), BF16),
        jax.ShapeDtypeStruct((bsz, s, A_HEADS * LANES), BF16),
        jax.ShapeDtypeStruct((bsz, s // tkv, A_HEADS * LANES, tkv), BF16),
        jax.ShapeDtypeStruct((bsz, s, B_WIDTH), BF16),
        jax.ShapeDtypeStruct((bsz, s, B_WIDTH), BF16),
        jax.ShapeDtypeStruct((bsz, B_HEADS // 2, s // LANES, LANES, LANES), BF16),
        jax.ShapeDtypeStruct((bsz, s, A_WIDTH + B_WIDTH), BF16),
        jax.ShapeDtypeStruct((bsz, s, A_KV_RANK), F32),
        jax.ShapeDtypeStruct((bsz, s, A_ROPE), F32),
        jax.ShapeDtypeStruct((bsz, tail_len, B_WIDTH), F32),
        jax.ShapeDtypeStruct((bsz, tail_len, B_WIDTH), F32),
    )
    vbt_spec = pl.BlockSpec((1, B_HEADS // 2, tm // LANES, LANES, LANES), lambda b, i: (b, 0, i, 0, 0))
    out_specs = (rows(1024), rows(1024), vt_spec, rows(512), rows(512), vbt_spec, rows(1024),
                 rows(A_KV_RANK), rows(A_ROPE), tail_spec, tail_spec)
    return pl.pallas_call(
        _ab_in_body,
        grid=(bsz, nt),
        in_specs=[rows(D_MODEL)] + [full(a) for a in weights]
        + [pl.BlockSpec((3, tm, LANES), lambda b, i: (0, i, 0))],
        out_specs=out_specs,
        out_shape=out_shape,
        compiler_params=_params(2),
        name="ab_in_proj",
    )(x, *weights, rope)


def _mix_out(o_refs, sg_ref, w_ref, g_ref, h_ref):
    o = jnp.concatenate([r[0].astype(F32) for r in o_refs], axis=-1) if len(o_refs) > 1 else o_refs[0][0].astype(F32)
    mixed = (o * sg_ref[0].astype(F32)).astype(BF16)
    y = jnp.dot(mixed, w_ref[...], preferred_element_type=F32)
    return h_ref[0] + _rms(y, g_ref[...])


def _c_in_body(*refs):
    n_o = len(refs) - 14
    o_refs = refs[:n_o]
    (sg0_ref, w0_ref, post0_ref, x_ref, pre_ref, w_ref, rope_ref,
     h_ref, q_ref, k_ref, vt_ref, sg_ref, k_tail_ref, v_tail_ref) = refs[n_o:]
    tm = x_ref.shape[1]
    tb = k_tail_ref.shape[1]
    h = _mix_out(o_refs, sg0_ref, w0_ref, post0_ref, x_ref)
    h_ref[0] = h
    xn = _rms(h, pre_ref[...]).astype(BF16)
    z = jnp.dot(xn, w_ref[...], preferred_element_type=F32)
    half_rot = C_ROT // 2
    for j in range(C_WIDTH // LANES):
        blk = slice(C_Q0 + j * LANES, C_Q0 + (j + 1) * LANES)
        q_ref[0, :, j * LANES:(j + 1) * LANES] = (_rope_block(z[:, blk], rope_ref, half_rot) * C_SCALE).astype(BF16)
    kd = [_rope_block(z[:, C_K0 + j * LANES:C_K0 + (j + 1) * LANES], rope_ref, half_rot) for j in range(C_KV_HEADS)]
    vz = [z[:, C_V0 + j * LANES:C_V0 + (j + 1) * LANES] for j in range(C_KV_HEADS)]
    row = lax.broadcasted_iota(jnp.int32, (LANES, tm), 0)
    for j in range(C_KV_HEADS):
        k_ref[0, :, j * LANES:(j + 1) * LANES] = kd[j].astype(BF16)
        vt = jnp.where(row == C_HEAD_DIM, 1.0, vz[j].T).astype(BF16)
        for i in range(tm // LANES):
            vt_ref[0, j, i] = vt[:, i * LANES:(i + 1) * LANES]
    g = z[:, C_G0:C_NZ]
    sg_ref[0] = (g * jax.nn.sigmoid(g)).astype(BF16)
    lo = lax.broadcasted_iota(jnp.int32, (1, LANES), 1) < HALF
    k_tail_ref[0] = jnp.where(lo, kd[0], kd[1])[tm - tb:, :]
    v_tail_ref[0] = (vz[0] + pltpu.roll(vz[1], HALF, 1))[tm - tb:, :]


def _c_in_proj(o_parts, sg0, w_out0, post0, x, rope, w, tm, tail_len):
    bsz, s, _ = x.shape
    nt = s // tm
    tb, tail_spec = _tail_spec(nt, tm, tail_len, LANES)

    def full(a):
        return pl.BlockSpec(a.shape, lambda b, i: (0,) * a.ndim)

    def rows(width):
        return pl.BlockSpec((1, tm, width), lambda b, i: (b, i, 0))

    out_shape = (
        jax.ShapeDtypeStruct((bsz, s, D_MODEL), F32),
        jax.ShapeDtypeStruct((bsz, s, C_WIDTH), BF16),
        jax.ShapeDtypeStruct((bsz, s, C_KV_HEADS * LANES), BF16),
        jax.ShapeDtypeStruct((bsz, C_KV_HEADS, s // LANES, LANES, LANES), BF16),
        jax.ShapeDtypeStruct((bsz, s, C_WIDTH), BF16),
        jax.ShapeDtypeStruct((bsz, tail_len, LANES), F32),
        jax.ShapeDtypeStruct((bsz, tail_len, LANES), F32),
    )
    return pl.pallas_call(
        _c_in_body,
        grid=(bsz, nt),
        in_specs=[rows(o.shape[-1]) for o in o_parts]
        + [rows(D_MODEL), full(w_out0), full(post0), rows(D_MODEL), full(w["pre"]), full(w["w_in"]),
           pl.BlockSpec((3, tm, LANES), lambda b, i: (0, i, 0))],
        out_specs=(rows(D_MODEL), rows(C_WIDTH), rows(256),
                   pl.BlockSpec((1, C_KV_HEADS, tm // LANES, LANES, LANES), lambda b, i: (b, 0, i, 0, 0)),
                   rows(C_WIDTH), tail_spec, tail_spec),
        out_shape=out_shape,
        compiler_params=_params(2),
        name="c_in_proj",
    )(*o_parts, sg0, w_out0, post0, x, w["pre"], w["w_in"], rope)


def _out_body(*refs):
    o_refs, (sg_ref, w_ref, g_ref, h_ref, out_ref) = refs[:-5], refs[-5:]
    out_ref[0] = _mix_out(o_refs, sg_ref, w_ref, g_ref, h_ref)


def _out_proj(o_parts, sg, w_out, post_g, h, tm):
    bsz, s, _ = h.shape

    def rows(width):
        return pl.BlockSpec((1, tm, width), lambda b, i: (b, i, 0))

    def full(a):
        return pl.BlockSpec(a.shape, lambda b, i: (0,) * a.ndim)

    return pl.pallas_call(
        _out_body,
        grid=(bsz, s // tm),
        in_specs=[rows(o.shape[-1]) for o in o_parts] + [rows(D_MODEL), full(w_out), full(post_g), rows(D_MODEL)],
        out_specs=rows(D_MODEL),
        out_shape=jax.ShapeDtypeStruct(h.shape, F32),
        compiler_params=_params(2),
        name="out_proj",
    )(*o_parts, sg, w_out, post_g, h)


def _mla_prompt_body(q_ref, k_ref, vt_ref, o_ref, m_sc, acc_sc, s_sc):
    tq = q_ref.shape[1]
    qi = pl.program_id(2)
    m_sc[...] = jnp.full(m_sc.shape, NEG_INF, F32)
    acc_sc[...] = jnp.zeros(acc_sc.shape, F32)

    def scores(j, slot, masked):
        start = pl.multiple_of(j * tq, tq)
        for hh in range(2):
            blk = slice(hh * LANES, (hh + 1) * LANES)
            s = lax.dot_general(k_ref[0, pl.ds(start, tq), blk], q_ref[0, :, blk], (((1,), (1,)), ((), ())),
                                preferred_element_type=F32)
            if masked:
                kc = lax.broadcasted_iota(jnp.int32, (tq, tq), 0) // CHUNK
                qc = lax.broadcasted_iota(jnp.int32, (tq, tq), 1) // CHUNK
                s = jnp.where(kc <= qc, s, NEG_INF)
            s_sc[slot, hh] = s

    def consume(j, slot):
        for hh in range(2):
            s = s_sc[slot, hh]
            vt = vt_ref[0, j, hh * LANES:(hh + 1) * LANES, :]
            m_old = m_sc[hh]
            m_new = jnp.maximum(m_old, jnp.max(s, axis=0, keepdims=True))
            p = jnp.exp2(s - m_new).astype(BF16)
            acc_sc[hh] = jnp.exp2(m_old - m_new) * acc_sc[hh] + jnp.dot(vt, p, preferred_element_type=F32)
            m_sc[hh] = m_new

    @pl.when(qi == 0)
    def _():
        scores(0, 0, True)
        consume(0, 0)

    def stage(j_next, masked_next, j_cur, cur_slot):
        scores(j_next, 1 - cur_slot, masked_next)
        consume(j_cur, cur_slot)

    @pl.when(qi > 0)
    def _():
        scores(0, 0, False)

        def pair_body(i, carry):
            j = 2 * i
            stage(j + 1, False, j, 0)
            stage(j + 2, False, j + 1, 1)
            return carry

        lax.fori_loop(0, (qi - 1) // 2, pair_body, 0)
        odd = ((qi - 1) & 1) == 1

        @pl.when(odd)
        def _():
            stage(qi - 1, False, qi - 2, 0)
            stage(qi, True, qi - 1, 1)
            consume(qi, 0)

        @pl.when(jnp.logical_not(odd))
        def _():
            stage(qi, True, qi - 1, 0)
            consume(qi, 1)

    outs = []
    for hh in range(2):
        acc = acc_sc[hh]
        outs.append((acc / acc[A_V:A_V + 1, :]).T)
    lo = lax.broadcasted_iota(jnp.int32, (1, LANES), 1) < HALF
    o_ref[0] = jnp.where(lo, outs[0], pltpu.roll(outs[1], HALF, 1)).astype(o_ref.dtype)


def _mla_prompt(qa, ka, vt, tq):
    bsz, s, _ = qa.shape
    pairs = A_HEADS // 2
    assert vt.shape[3] == tq
    return pl.pallas_call(
        _mla_prompt_body,
        grid=(bsz, pairs, s // tq),
        in_specs=[pl.BlockSpec((1, tq, 2 * LANES), lambda b, h, i: (b, i, h)),
                  pl.BlockSpec((1, s, 2 * LANES), lambda b, h, i: (b, 0, h)),
                  pl.BlockSpec((1, s // tq, 2 * LANES, tq), lambda b, h, i: (b, 0, h, 0))],
        out_specs=pl.BlockSpec((1, tq, LANES), lambda b, h, i: (b, i, h)),
        out_shape=jax.ShapeDtypeStruct((bsz, s, A_WIDTH), BF16),
        scratch_shapes=[pltpu.VMEM((2, 1, tq), F32), pltpu.VMEM((2, LANES, tq), F32),
                        pltpu.VMEM((2, 2, tq, tq), F32)],
        compiler_params=_params(3),
        name="mla_prompt",
    )(qa, ka, vt)


def _mla_sample_body(q_ref, cc_ref, ckr_ref, cn_ref, krn_ref, wkt_ref, wv_ref, sel_ref, o_ref,
                     qabs_sc, qr_sc, m_sc, l_sc, acc_sc, *, tk):
    t = q_ref.shape[1]
    past = cc_ref.shape[1]
    for h in range(A_HEADS):
        qh = q_ref[0, :, h * LANES:(h + 1) * LANES]
        rows = slice(h * t, (h + 1) * t)
        qabs_sc[rows, :] = jnp.dot(qh, wkt_ref[h], preferred_element_type=F32).astype(BF16)
        qr_sc[rows, :] = jnp.dot(qh, sel_ref[...], preferred_element_type=F32).astype(BF16)
    m_sc[...] = jnp.full(m_sc.shape, NEG_INF, F32)
    l_sc[...] = jnp.zeros(l_sc.shape, F32)
    acc_sc[...] = jnp.zeros(acc_sc.shape, F32)
    contract_last = (((1,), (1,)), ((), ()))

    def update(c_t, kr_t):
        s = (lax.dot_general(qabs_sc[...], c_t, contract_last, preferred_element_type=F32)
             + lax.dot_general(qr_sc[...], kr_t, contract_last, preferred_element_type=F32))
        m_old = m_sc[...]
        m_new = jnp.maximum(m_old, jnp.max(s, axis=-1, keepdims=True))
        p = jnp.exp2(s - m_new)
        alpha = jnp.exp2(m_old - m_new)
        l_sc[...] = alpha * l_sc[...] + jnp.sum(p, axis=-1, keepdims=True)
        acc_sc[...] = alpha * acc_sc[...] + jnp.dot(p.astype(BF16), c_t, preferred_element_type=F32)
        m_sc[...] = m_new

    def loop_body(j, carry):
        start = pl.multiple_of(j * tk, tk)
        update(cc_ref[0, pl.ds(start, tk), :].astype(BF16), ckr_ref[0, pl.ds(start, tk), :].astype(BF16))
        return carry

    lax.fori_loop(0, past // tk, loop_body, 0)
    update(cn_ref[0].astype(BF16), krn_ref[0].astype(BF16))

    o_lat = (acc_sc[...] / l_sc[...]).astype(BF16)
    out = jnp.zeros((t, A_WIDTH), F32)
    for h in range(A_HEADS):
        out = out + jnp.dot(o_lat[h * t:(h + 1) * t, :], wv_ref[h], preferred_element_type=F32)
    o_ref[0] = out.astype(o_ref.dtype)


def _mla_sample(qa, cache_c, cache_kr, c_new, kr_new, wkt, wv, sel, tk):
    bsz, t, _ = qa.shape
    past = cache_c.shape[1]

    def per_b(shape):
        return pl.BlockSpec((1,) + shape, lambda b: (b, 0, 0))

    def full(a):
        return pl.BlockSpec(a.shape, lambda b: (0,) * a.ndim)

    rows = A_HEADS * t
    return pl.pallas_call(
        functools.partial(_mla_sample_body, tk=tk),
        grid=(bsz,),
        in_specs=[per_b((t, A_HEADS * LANES)), per_b((past, A_KV_RANK)), per_b((past, A_ROPE)),
                  per_b((t, A_KV_RANK)), per_b((t, A_ROPE)), full(wkt), full(wv), full(sel)],
        out_specs=per_b((t, A_WIDTH)),
        out_shape=jax.ShapeDtypeStruct((bsz, t, A_WIDTH), BF16),
        scratch_shapes=[pltpu.VMEM((rows, A_KV_RANK), BF16), pltpu.VMEM((rows, A_ROPE), BF16),
                        pltpu.VMEM((rows, 1), F32), pltpu.VMEM((rows, 1), F32),
                        pltpu.VMEM((rows, A_KV_RANK), F32)],
        compiler_params=_params(1),
        name="mla_sample",
    )(qa, cache_c, cache_kr, c_new, kr_new, wkt, wv, sel)


def _band_body(*refs, n_blk, n_qblk, cpg, ones_row, has_sink):
    if has_sink:
        q_ref, k_ref, vt_ref, bias_ref, sink_ref, o_ref = refs
    else:
        q_ref, k_ref, vt_ref, bias_ref, o_ref = refs
    tq = q_ref.shape[1]
    win = n_blk * LANES
    t = pl.program_id(2)
    lo = lax.broadcasted_iota(jnp.int32, (CHUNK, LANES), 1) < HALF
    contract_last = (((1,), (1,)), ((), ()))

    def scores(gl, wb, variant):
        pieces = []
        for cc in range(cpg):
            rows = slice((gl * cpg + cc) * CHUNK, (gl * cpg + cc + 1) * CHUNK)
            for r in range(n_qblk):
                qblk = q_ref[0, rows, r * LANES:(r + 1) * LANES]
                zero = jnp.zeros_like(qblk)
                pieces += [jnp.where(lo, qblk, zero), jnp.where(lo, zero, qblk)]
        qs = jnp.concatenate(pieces, axis=0)
        start = wb * LANES if isinstance(wb, int) else pl.multiple_of(wb * LANES, LANES)
        kw = k_ref[0, pl.ds(start, win), :]
        return lax.dot_general(kw, qs, contract_last, preferred_element_type=F32) + bias_ref[0, variant]

    def finish(gl, wb, s):
        m = jnp.max(s, axis=0, keepdims=True)
        if has_sink:
            sink = sink_ref[0]
            m = jnp.maximum(m, sink)
        p = jnp.exp2(s - m)
        vt = jnp.concatenate([vt_ref[0, 0, wb + i] for i in range(n_blk)], axis=1)
        o = jnp.dot(vt, p.astype(BF16), preferred_element_type=F32)
        l = o[CHUNK:CHUNK + 1, :] if ones_row else jnp.sum(p, axis=0, keepdims=True)
        if has_sink:
            l = l + jnp.exp2(sink - m)
        o = (o / l).T
        idx = 0
        for cc in range(cpg):
            rows = slice((gl * cpg + cc) * CHUNK, (gl * cpg + cc + 1) * CHUNK)
            for r in range(n_qblk):
                top = o[idx * CHUNK:(idx + 1) * CHUNK]
                bot = o[(idx + 1) * CHUNK:(idx + 2) * CHUNK]
                if ones_row:
                    bot = pltpu.roll(bot, HALF, 1)
                o_ref[0, rows, r * LANES:(r + 1) * LANES] = jnp.where(lo, top, bot).astype(o_ref.dtype)
                idx += 2

    n_groups = tq // (cpg * CHUNK)
    if bias_ref.shape[1] == 1:
        finish(0, 0, scores(0, 0, 0))
        return
    assert n_groups >= n_blk - 1

    def run(window_block, variant):
        s = scores(0, window_block(0), variant(0))
        for gl in range(n_groups):
            s_next = scores(gl + 1, window_block(gl + 1), variant(gl + 1)) if gl + 1 < n_groups else None
            finish(gl, window_block(gl), s)
            s = s_next

    @pl.when(t == 0)
    def _():
        run(lambda gl: max(gl - (n_blk - 1), 0), lambda gl: min(gl, n_blk - 1))

    @pl.when(t > 0)
    def _():
        run(lambda gl: t * n_groups + gl - (n_blk - 1), lambda gl: n_blk - 1)


def _band_attention(q, k, vt, bias, sink, *, n_groups, n_qblk, cpg, ones_row, tq):
    bsz, s, _ = q.shape
    sk = k.shape[1]
    n_blk = bias.shape[2] // LANES
    qw = n_qblk * LANES
    in_specs = [pl.BlockSpec((1, tq, qw), lambda b, g, i: (b, i, g)),
                pl.BlockSpec((1, sk, LANES), lambda b, g, i: (b, 0, g)),
                pl.BlockSpec((1, 1) + vt.shape[2:], lambda b, g, i: (b, g, 0, 0, 0)),
                pl.BlockSpec((1,) + bias.shape[1:], lambda b, g, i: (g, 0, 0, 0))]
    args = [q, k, vt, bias]
    if sink is not None:
        in_specs.append(pl.BlockSpec((1,) + sink.shape[1:], lambda b, g, i: (g, 0, 0)))
        args.append(sink)
    body = functools.partial(_band_body, n_blk=n_blk, n_qblk=n_qblk, cpg=cpg, ones_row=ones_row,
                             has_sink=sink is not None)
    return pl.pallas_call(
        body,
        grid=(bsz, n_groups, s // tq),
        in_specs=in_specs,
        out_specs=pl.BlockSpec((1, tq, qw), lambda b, g, i: (b, i, g)),
        out_shape=jax.ShapeDtypeStruct(q.shape, BF16),
        compiler_params=_params(3),
        name="band_attention",
    )(*args)


def _rope_tables(pos, rot, lane_pattern):
    half = rot // 2
    inv = jnp.power(ROPE_THETA, -jnp.arange(half, dtype=F32) * 2.0 / rot)
    inv_lane, first, second = [], [], []
    for kind, width in lane_pattern:
        if kind == "rot":
            inv_lane += [inv, inv]
            first += [1.0] * half + [0.0] * half
            second += [0.0] * half + [1.0] * half
        else:
            inv_lane.append(jnp.zeros((width,), F32))
            first += [0.0] * width
            second += [0.0] * width
    ang = pos.astype(F32)[:, None] * jnp.concatenate(inv_lane)[None, :]
    cos, sin = jnp.cos(ang), jnp.sin(ang)
    return jnp.stack([cos, sin * np.asarray(second, np.float32), -sin * np.asarray(first, np.float32)])


A_ROPE_PATTERN = (("pad", A_NOPE), ("rot", A_ROPE), ("pad", LANES - A_QK))
C_ROPE_PATTERN = (("rot", C_ROT), ("pad", HALF - C_ROT)) * 2


def _prep_ab(pre, post, w_in, q_norm, kv_norm, w_uq, w_ukv, rel_bias, w_out):
    d = w_in.shape[0]
    q_lat, c_kv, k_r, g_a, q_b, k_b, v_b, g_b = jnp.split(
        w_in, [384, 640, 672, 1184, 1696, 2208, 2720], axis=1)
    kr_blk = jnp.concatenate([jnp.zeros((d, A_NOPE), F32), k_r, jnp.zeros((d, LANES - A_QK), F32)], axis=1)
    w_in_p = jnp.concatenate([q_lat, c_kv, kr_blk, g_a, g_b, q_b, k_b, v_b], axis=1).astype(BF16)
    w_uq_p = jnp.pad(w_uq.reshape(A_Q_RANK, A_HEADS, A_QK), ((0, 0), (0, 0), (0, LANES - A_QK)))
    w_uq_p = w_uq_p.reshape(A_Q_RANK, A_HEADS * LANES).astype(BF16)
    ukv = w_ukv.reshape(A_KV_RANK, A_HEADS, A_NOPE + A_V)
    w_uk, w_uv = ukv[..., :A_NOPE], ukv[..., A_NOPE:]
    pad_half = ((0, 0), (0, 0), (0, LANES - A_NOPE))
    w_k = jnp.pad(w_uk, pad_half).reshape(A_KV_RANK, A_HEADS * LANES).astype(BF16)
    w_vt = jnp.pad(w_uv, pad_half).reshape(A_KV_RANK, A_HEADS * LANES).T.astype(BF16)
    wkt = jnp.pad(jnp.transpose(w_uk, (1, 2, 0)), ((0, 0), (0, LANES - A_NOPE), (0, 0))).astype(BF16)
    eye = jnp.eye(A_HEADS, dtype=F32)
    wv_s = (jnp.transpose(w_uv, (1, 0, 2))[:, :, None, :] * eye[:, None, :, None]).reshape(
        A_HEADS, A_KV_RANK, A_WIDTH).astype(BF16)
    sel = (jnp.arange(LANES)[:, None] == A_NOPE + jnp.arange(A_ROPE)[None, :]).astype(BF16)
    rel_bias = rel_bias * LOG2E
    win = B_WIN_BLOCKS * LANES
    r0 = win - CHUNK
    x_len = r0 + win
    n_vec = x_len + CHUNK
    n_hi = r0 + CHUNK - 1 - B_MAX_REL
    n_lo = n_vec - n_hi - (2 * B_MAX_REL + 1)
    vec = jnp.concatenate([jnp.broadcast_to(rel_bias[:, -1:], (B_HEADS, n_hi)), rel_bias[:, ::-1],
                           jnp.broadcast_to(rel_bias[:, :1], (B_HEADS, n_lo))], axis=1)
    skew = jnp.tile(vec, (1, CHUNK))[:, :CHUNK * (n_vec - 1)].reshape(B_HEADS, CHUNK, n_vec - 1)
    toep = skew[:, :, CHUNK - 1:CHUNK - 1 + x_len]
    toep = jnp.transpose(toep.reshape(B_HEADS // 2, 2, CHUNK, x_len), (0, 3, 1, 2)).reshape(B_HEADS // 2, x_len, LANES)
    key_chunk = jnp.arange(win)[None, :, None] // CHUNK

    def window(first_rel, lo_chunk, hi_chunk):
        x0 = r0 - first_rel
        return jnp.where((key_chunk >= lo_chunk) & (key_chunk <= hi_chunk), toep[:, x0:x0 + win, :], NEG_INF)

    variants = []
    for v in range(B_WIN_BLOCKS):
        chunks = [2 * v + cc for cc in range(2)]
        variants.append(jnp.concatenate([window(c * CHUNK, c - B_PAST_CHUNKS, c) for c in chunks], axis=-1))
    bias_p = jnp.stack(variants, axis=1)
    bias_s = window(r0, 1, B_PAST_CHUNKS + 1)[:, None]
    return dict(pre=pre[None], w_in=w_in_p, q_norm=q_norm[None], kv_norm=kv_norm[None], w_uq=w_uq_p, w_k=w_k,
                w_vt=w_vt, wkt=wkt, wv_s=wv_s, sel=sel, bias_p=bias_p, bias_s=bias_s, w_out=w_out.astype(BF16),
                post=post[None])


def _prep_c(pre, post, w_in, sinks, w_out):
    q, k, v, g = jnp.split(w_in, [1024, 1152, 1280], axis=1)
    k0, k1 = k[:, :C_HEAD_DIM], k[:, C_HEAD_DIM:]
    v0, v1 = v[:, :C_HEAD_DIM], v[:, C_HEAD_DIM:]
    zero = jnp.zeros_like(v0)
    w_in_p = jnp.concatenate([q, k0, k0, k1, k1, v0, zero, v1, zero, g], axis=1).astype(BF16)
    win = C_WIN_BLOCKS * LANES
    per_chunk = C_GROUP * CHUNK
    key_chunk = jnp.arange(win)[:, None] // CHUNK

    def window(lo_chunk, hi_chunk):
        m = jnp.where((key_chunk >= lo_chunk) & (key_chunk <= hi_chunk), 0.0, NEG_INF).astype(F32)
        return jnp.broadcast_to(m, (win, per_chunk))

    variants = [jnp.concatenate([window(c - C_PAST_CHUNKS, c) for c in (2 * v, 2 * v + 1)], axis=-1)
                for v in range(C_WIN_BLOCKS)]
    mask_p = jnp.broadcast_to(jnp.stack(variants)[None], (C_KV_HEADS, C_WIN_BLOCKS, win, 2 * per_chunk))
    mask_s = jnp.broadcast_to(window(1, C_PAST_CHUNKS + 1)[None, None], (C_KV_HEADS, 1, win, per_chunk))
    sink_row = jnp.repeat((sinks * LOG2E).reshape(C_KV_HEADS, C_GROUP), CHUNK, axis=1)[:, None, :]
    return dict(pre=pre[None], w_in=w_in_p, mask_p=mask_p, mask_s=mask_s, sink_s=sink_row,
                sink_p=jnp.tile(sink_row, (1, 1, 2)), w_out=w_out.astype(BF16), post=post[None])


def _dup_heads(x):
    return jnp.concatenate([x[:, :, 0], x[:, :, 0], x[:, :, 1], x[:, :, 1]], axis=-1)


def kernel(x_prompt, x_sample, cache_a_ckv, cache_a_krope, cache_b_k, cache_b_v, cache_c_k, cache_c_v,
           ab_pre_norm, ab_post_norm, ab_w_in, ab_q_norm, ab_kv_norm, ab_w_uq, ab_w_ukv, ab_rel_bias, ab_w_out,
           c_pre_norm, c_post_norm, c_w_in, c_sinks, c_w_out):
    bsz, seq, _ = x_prompt.shape
    dbs, dseq, _ = x_sample.shape
    past = cache_a_ckv.shape[2]
    n_s = dbs * dseq
    pos_p = jnp.arange(seq, dtype=jnp.int32)
    pos_s = jnp.tile(past + jnp.arange(dseq, dtype=jnp.int32), dbs)
    wab = _prep_ab(ab_pre_norm[0], ab_post_norm[0], ab_w_in[0], ab_q_norm[0], ab_kv_norm[0], ab_w_uq[0],
                   ab_w_ukv[0], ab_rel_bias[0], ab_w_out[0])
    wc = _prep_c(c_pre_norm[0], c_post_norm[0], c_w_in[0], c_sinks[0], c_w_out[0])
    b_tail = min(B_PAST_CHUNKS * CHUNK, seq)
    c_tail = min(C_WINDOW, seq)
    tile = 512

    rope_a_p = _rope_tables(pos_p, A_ROPE, A_ROPE_PATTERN)
    (qa, ka, vt, qb, kb, vbt, sg, c_new_p, kr_new_p, kb_tail, vb_tail) = _ab_in_proj(
        x_prompt, rope_a_p, wab, tm=256, tail_len=b_tail, tkv=tile)
    o_a = _mla_prompt(qa, ka, vt, tq=tile)
    o_b = _band_attention(qb, kb, vbt, wab["bias_p"], None, n_groups=B_HEADS // 2, n_qblk=1, cpg=GROUP_CHUNKS,
                          ones_row=False, tq=tile)

    rope_a_s = _rope_tables(pos_s, A_ROPE, A_ROPE_PATTERN)
    xs = x_sample.reshape(1, n_s, D_MODEL)
    (qa_s, _, _, qb_s, _, _, sg_s, c_new_s, kr_new_s, kb_s32, vb_s32) = _ab_in_proj(
        xs, rope_a_s, wab, tm=n_s, tail_len=n_s, tkv=n_s)
    o_a_s = _mla_sample(qa_s.reshape(dbs, dseq, -1), cache_a_ckv[0], cache_a_krope[0],
                        c_new_s.reshape(dbs, dseq, -1), kr_new_s.reshape(dbs, dseq, -1),
                        wab["wkt"], wab["wv_s"], wab["sel"], tk=512)
    wb = cache_b_k.shape[2]
    pad_b = jnp.zeros((dbs, B_WIN_BLOCKS * LANES - wb - dseq, B_WIDTH), F32)
    kband = jnp.concatenate([pad_b, cache_b_k[0].reshape(dbs, wb, B_WIDTH), kb_s32.reshape(dbs, dseq, -1)], 1)
    vband = jnp.concatenate([pad_b, cache_b_v[0].reshape(dbs, wb, B_WIDTH), vb_s32.reshape(dbs, dseq, -1)], 1)
    vband_t = jnp.transpose(vband.astype(BF16).reshape(dbs, B_WIN_BLOCKS, LANES, B_HEADS // 2, LANES), (0, 3, 1, 4, 2))
    o_b_s = _band_attention(qb_s.reshape(dbs, dseq, -1), kband.astype(BF16), vband_t, wab["bias_s"], None,
                            n_groups=B_HEADS // 2, n_qblk=1, cpg=1, ones_row=False, tq=dseq)

    rope_c_p = _rope_tables(pos_p, C_ROT, C_ROPE_PATTERN)
    h1_p, qc, kc, vct, sgc, kc_tail, vc_tail = _c_in_proj(
        [o_a, o_b], sg, wab["w_out"], wab["post"], x_prompt, rope_c_p, wc, tm=tile, tail_len=c_tail)
    o_c = _band_attention(qc, kc, vct, wc["mask_p"], wc["sink_p"], n_groups=C_KV_HEADS, n_qblk=C_GROUP // 2,
                          cpg=GROUP_CHUNKS, ones_row=True, tq=tile)
    h2_p = _out_proj([o_c], sgc, wc["w_out"], wc["post"], h1_p, tm=tile)

    rope_c_s = _rope_tables(pos_s, C_ROT, C_ROPE_PATTERN)
    h1_s, qc_s, kc_s, _, sgc_s, kc_s32, vc_s32 = _c_in_proj(
        [o_a_s.reshape(1, n_s, -1), o_b_s.reshape(1, n_s, -1)], sg_s, wab["w_out"], wab["post"], xs, rope_c_s, wc,
        tm=n_s, tail_len=n_s)
    wcw = cache_c_k.shape[2]
    win_c = C_WIN_BLOCKS * LANES
    n_pad = win_c - wcw - dseq
    kcb = jnp.concatenate([jnp.zeros((dbs, n_pad, C_KV_HEADS * LANES), BF16), _dup_heads(cache_c_k[0]).astype(BF16),
                           kc_s.reshape(dbs, dseq, -1)], axis=1)
    vcb = jnp.concatenate([jnp.zeros((dbs, n_pad, C_KV_HEADS, C_HEAD_DIM), F32), cache_c_v[0],
                           vc_s32.reshape(dbs, dseq, C_KV_HEADS, C_HEAD_DIM)], axis=1)
    vcb_t = jnp.concatenate([jnp.transpose(vcb, (0, 2, 3, 1)), jnp.ones((dbs, C_KV_HEADS, 1, win_c), F32),
                             jnp.zeros((dbs, C_KV_HEADS, LANES - C_HEAD_DIM - 1, win_c), F32)], axis=2)
    vcb_t = jnp.transpose(vcb_t.astype(BF16).reshape(dbs, C_KV_HEADS, LANES, C_WIN_BLOCKS, LANES), (0, 1, 3, 2, 4))
    o_c_s = _band_attention(qc_s.reshape(dbs, dseq, -1), kcb, vcb_t, wc["mask_s"], wc["sink_s"], n_groups=C_KV_HEADS,
                            n_qblk=C_GROUP // 2, cpg=1, ones_row=True, tq=dseq)
    h2_s = _out_proj([o_c_s.reshape(1, n_s, -1)], sgc_s, wc["w_out"], wc["post"], h1_s, tm=n_s)

    def roll_in(buf, new):
        return jnp.concatenate([buf, new], axis=1)[:, -buf.shape[1]:][None]

    return (h2_p, h2_s.reshape(dbs, dseq, D_MODEL),
            c_new_p[None], kr_new_p[None],
            kb_tail.reshape(1, bsz, b_tail, B_HEADS, B_HEAD_DIM), vb_tail.reshape(1, bsz, b_tail, B_HEADS, B_HEAD_DIM),
            kc_tail.reshape(1, bsz, c_tail, C_KV_HEADS, C_HEAD_DIM), vc_tail.reshape(1, bsz, c_tail, C_KV_HEADS, C_HEAD_DIM),
            c_new_s.reshape(1, dbs, dseq, A_KV_RANK), kr_new_s.reshape(1, dbs, dseq, A_ROPE),
            roll_in(cache_b_k[0], kb_s32.reshape(dbs, dseq, B_HEADS, B_HEAD_DIM)),
            roll_in(cache_b_v[0], vb_s32.reshape(dbs, dseq, B_HEADS, B_HEAD_DIM)),
            roll_in(cache_c_k[0], kc_s32.reshape(dbs, dseq, C_KV_HEADS, C_HEAD_DIM)),
            roll_in(cache_c_v[0], vc_s32.reshape(dbs, dseq, C_KV_HEADS, C_HEAD_DIM)))
```

```python
import functools

import jax
import jax.numpy as jnp
import numpy as np
from jax import lax
from jax.experimental import pallas as pl
from jax.experimental.pallas import tpu as pltpu

F32 = jnp.float32
BF16 = jnp.bfloat16

D_MODEL = 1024
CHUNK = 64
ROPE_THETA = 500000.0
RMS_EPS = 1e-6
NEG_INF = -1e30

A_HEADS = 8
A_NOPE = 64
A_ROPE = 32
A_QK = A_NOPE + A_ROPE
A_V = 64
A_Q_RANK = 384
A_KV_RANK = 256
A_WIDTH = A_HEADS * A_V
LOG2E = 1.4426950408889634
A_SCALE = A_QK ** -0.5 * LOG2E

B_HEADS = 8
B_HEAD_DIM = 64
B_WIDTH = B_HEADS * B_HEAD_DIM
B_PAST_CHUNKS = 8
B_MAX_REL = 128
B_SCALE = B_HEAD_DIM ** -0.5 * LOG2E

C_HEADS = 16
C_KV_HEADS = 2
C_GROUP = C_HEADS // C_KV_HEADS
C_HEAD_DIM = 64
C_WIDTH = C_HEADS * C_HEAD_DIM
C_WINDOW = 128
C_PAST_CHUNKS = C_WINDOW // CHUNK
C_ROT = C_HEAD_DIM // 4
C_SCALE = C_HEAD_DIM ** -0.5 * LOG2E

LANES = 128
HALF = LANES // 2
VMEM_LIMIT = 56 * 1024 * 1024
A_MASK_CHUNKS = 8
assert A_QK + A_MASK_CHUNKS <= LANES
GROUP_CHUNKS = LANES // CHUNK
B_WIN_BLOCKS = (B_PAST_CHUNKS + GROUP_CHUNKS) * CHUNK // LANES
C_WIN_BLOCKS = (C_PAST_CHUNKS + GROUP_CHUNKS) * CHUNK // LANES

AB_Q0, AB_C0, AB_KR0, AB_G0, AB_QB0, AB_KB0, AB_VB0, AB_NZ = 0, 384, 640, 768, 1792, 2304, 2816, 3328
C_Q0, C_K0, C_V0, C_G0, C_NZ = 0, 1024, 1280, 1536, 2560


def _params(n_axes):
    return pltpu.CompilerParams(dimension_semantics=("arbitrary",) * n_axes, vmem_limit_bytes=VMEM_LIMIT)


def _rms(x, g):
    return x * lax.rsqrt(jnp.mean(x * x, axis=-1, keepdims=True) + RMS_EPS) * g


def _rope_block(blk, rope_ref, shift):
    return (blk * rope_ref[0] + pltpu.roll(blk, shift, 1) * rope_ref[1]
            + pltpu.roll(blk, LANES - shift, 1) * rope_ref[2])


def _tail_spec(n_tiles, tm, tail_len, width):
    tb = min(tail_len, tm)
    n_blk = tail_len // tb
    return tb, pl.BlockSpec((1, tb, width), lambda b, i: (b, jnp.maximum(i - (n_tiles - n_blk), 0), 0))


def _ab_in_body(x_ref, pre_ref, w_ref, qn_ref, kvn_ref, wuq_ref, wk_ref, wvt_ref, rope_ref,
                qa_ref, ka_ref, vt_ref, qb_ref, kb_ref, vbt_ref, sg_ref, c_ref, kr_ref, kb_tail_ref, vb_tail_ref):
    tm = x_ref.shape[1]
    tb = kb_tail_ref.shape[1]
    xn = _rms(x_ref[0], pre_ref[...]).astype(BF16)
    z = jnp.dot(xn, w_ref[...], preferred_element_type=F32)

    qn = _rms(z[:, AB_Q0:AB_C0], qn_ref[...]).astype(BF16)
    qa = jnp.dot(qn, wuq_ref[...], preferred_element_type=F32) * A_SCALE
    for h in range(A_HEADS):
        blk = slice(h * LANES, (h + 1) * LANES)
        qa_ref[0, :, blk] = _rope_block(qa[:, blk], rope_ref, A_ROPE // 2).astype(BF16)

    c_new = _rms(z[:, AB_C0:AB_KR0], kvn_ref[...])
    c_ref[0] = c_new
    cb = c_new.astype(BF16)
    krot = _rope_block(z[:, AB_KR0:AB_G0], rope_ref, A_ROPE // 2)
    kr_ref[0] = krot[:, A_NOPE:A_NOPE + A_ROPE]
    kn = jnp.dot(cb, wk_ref[...], preferred_element_type=F32)
    row = pl.program_id(1) * tm + lax.broadcasted_iota(jnp.int32, (tm, LANES), 0)
    lane = lax.broadcasted_iota(jnp.int32, (tm, LANES), 1)
    k_shared = krot + jnp.where(lane - A_QK == (row // CHUNK) % A_MASK_CHUNKS, 1.0, 0.0)
    for h in range(A_HEADS):
        blk = slice(h * LANES, (h + 1) * LANES)
        ka_ref[0, :, blk] = (kn[:, blk] + k_shared).astype(BF16)
    vt = jnp.dot(wvt_ref[...], c_new.T.astype(BF16), preferred_element_type=F32)
    row = lax.broadcasted_iota(jnp.int32, vt.shape, 0)
    vt_ref[0, 0] = jnp.where((row & (LANES - 1)) == A_V, 1.0, vt).astype(BF16)

    g = z[:, AB_G0:AB_QB0]
    sg_ref[0] = (g * jax.nn.sigmoid(g)).astype(BF16)
    qb_ref[0] = (z[:, AB_QB0:AB_KB0] * B_SCALE).astype(BF16)
    kb = z[:, AB_KB0:AB_VB0]
    vb = z[:, AB_VB0:AB_NZ]
    kb_ref[0] = kb.astype(BF16)
    vbt = vb.T.astype(BF16)
    for hp in range(B_HEADS // 2):
        for j in range(tm // LANES):
            vbt_ref[0, hp, j] = vbt[hp * LANES:(hp + 1) * LANES, j * LANES:(j + 1) * LANES]
    kb_tail_ref[0] = kb[tm - tb:, :]
    vb_tail_ref[0] = vb[tm - tb:, :]


def _ab_in_proj(x, rope, w, tm, tail_len, tkv):
    bsz, s, _ = x.shape
    nt = s // tm
    per_kv = tkv // tm
    tb, tail_spec = _tail_spec(nt, tm, tail_len, B_WIDTH)

    def full(a):
        return pl.BlockSpec(a.shape, lambda b, i: (0,) * a.ndim)

    def rows(width):
        return pl.BlockSpec((1, tm, width), lambda b, i: (b, i, 0))

    weights = (w["pre"], w["w_in"], w["q_norm"], w["kv_norm"], w["w_uq"], w["w_k"], w["w_vt"])
    vt_spec = pl.BlockSpec((1, 1, A_HEADS * LANES, tm), lambda b, i: (b, i // per_kv, 0, i % per_kv))
    out_shape = (
        jax.ShapeDtypeStruct((bsz, s, A_HEADS * LANES), BF16),
        jax.ShapeDtypeStruct((bsz, s, A_HEADS * LANES), BF16),
        jax.ShapeDtypeStruct((bsz, s // tkv, A_HEADS * LANES, tkv), BF16),
        jax.ShapeDtypeStruct((bsz, s, B_WIDTH), BF16),
        jax.ShapeDtypeStruct((bsz, s, B_WIDTH), BF16),
        jax.ShapeDtypeStruct((bsz, B_HEADS // 2, s // LANES, LANES, LANES), BF16),
        jax.ShapeDtypeStruct((bsz, s, A_WIDTH + B_WIDTH), BF16),
        jax.ShapeDtypeStruct((bsz, s, A_KV_RANK), F32),
        jax.ShapeDtypeStruct((bsz, s, A_ROPE), F32),
        jax.ShapeDtypeStruct((bsz, tail_len, B_WIDTH), F32),
        jax.ShapeDtypeStruct((bsz, tail_len, B_WIDTH), F32),
    )
    vbt_spec = pl.BlockSpec((1, B_HEADS // 2, tm // LANES, LANES, LANES), lambda b, i: (b, 0, i, 0, 0))
    out_specs = (rows(1024), rows(1024), vt_spec, rows(512), rows(512), vbt_spec, rows(1024),
                 rows(A_KV_RANK), rows(A_ROPE), tail_spec, tail_spec)
    return pl.pallas_call(
        _ab_in_body,
        grid=(bsz, nt),
        in_specs=[rows(D_MODEL)] + [full(a) for a in weights]
        + [pl.BlockSpec((3, tm, LANES), lambda b, i: (0, i, 0))],
        out_specs=out_specs,
        out_shape=out_shape,
        compiler_params=_params(2),
        name="ab_in_proj",
    )(x, *weights, rope)


def _mix_out(o_refs, sg_ref, w_ref, g_ref, h_ref):
    o = jnp.concatenate([r[0].astype(F32) for r in o_refs], axis=-1) if len(o_refs) > 1 else o_refs[0][0].astype(F32)
    mixed = (o * sg_ref[0].astype(F32)).astype(BF16)
    y = jnp.dot(mixed, w_ref[...], preferred_element_type=F32)
    return h_ref[0] + _rms(y, g_ref[...])


def _c_in_body(*refs):
    n_o = len(refs) - 14
    o_refs = refs[:n_o]
    (sg0_ref, w0_ref, post0_ref, x_ref, pre_ref, w_ref, rope_ref,
     h_ref, q_ref, k_ref, vt_ref, sg_ref, k_tail_ref, v_tail_ref) = refs[n_o:]
    tm = x_ref.shape[1]
    tb = k_tail_ref.shape[1]
    h = _mix_out(o_refs, sg0_ref, w0_ref, post0_ref, x_ref)
    h_ref[0] = h
    xn = _rms(h, pre_ref[...]).astype(BF16)
    z = jnp.dot(xn, w_ref[...], preferred_element_type=F32)
    half_rot = C_ROT // 2
    for j in range(C_WIDTH // LANES):
        blk = slice(C_Q0 + j * LANES, C_Q0 + (j + 1) * LANES)
        q_ref[0, :, j * LANES:(j + 1) * LANES] = (_rope_block(z[:, blk], rope_ref, half_rot) * C_SCALE).astype(BF16)
    kd = [_rope_block(z[:, C_K0 + j * LANES:C_K0 + (j + 1) * LANES], rope_ref, half_rot) for j in range(C_KV_HEADS)]
    vz = [z[:, C_V0 + j * LANES:C_V0 + (j + 1) * LANES] for j in range(C_KV_HEADS)]
    row = lax.broadcasted_iota(jnp.int32, (LANES, tm), 0)
    for j in range(C_KV_HEADS):
        k_ref[0, :, j * LANES:(j + 1) * LANES] = kd[j].astype(BF16)
        vt = jnp.where(row == C_HEAD_DIM, 1.0, vz[j].T).astype(BF16)
        for i in range(tm // LANES):
            vt_ref[0, j, i] = vt[:, i * LANES:(i + 1) * LANES]
    g = z[:, C_G0:C_NZ]
    sg_ref[0] = (g * jax.nn.sigmoid(g)).astype(BF16)
    lo = lax.broadcasted_iota(jnp.int32, (1, LANES), 1) < HALF
    k_tail_ref[0] = jnp.where(lo, kd[0], kd[1])[tm - tb:, :]
    v_tail_ref[0] = (vz[0] + pltpu.roll(vz[1], HALF, 1))[tm - tb:, :]


def _c_in_proj(o_parts, sg0, w_out0, post0, x, rope, w, tm, tail_len):
    bsz, s, _ = x.shape
    nt = s // tm
    tb, tail_spec = _tail_spec(nt, tm, tail_len, LANES)

    def full(a):
        return pl.BlockSpec(a.shape, lambda b, i: (0,) * a.ndim)

    def rows(width):
        return pl.BlockSpec((1, tm, width), lambda b, i: (b, i, 0))

    out_shape = (
        jax.ShapeDtypeStruct((bsz, s, D_MODEL), F32),
        jax.ShapeDtypeStruct((bsz, s, C_WIDTH), BF16),
        jax.ShapeDtypeStruct((bsz, s, C_KV_HEADS * LANES), BF16),
        jax.ShapeDtypeStruct((bsz, C_KV_HEADS, s // LANES, LANES, LANES), BF16),
        jax.ShapeDtypeStruct((bsz, s, C_WIDTH), BF16),
        jax.ShapeDtypeStruct((bsz, tail_len, LANES), F32),
        jax.ShapeDtypeStruct((bsz, tail_len, LANES), F32),
    )
    return pl.pallas_call(
        _c_in_body,
        grid=(bsz, nt),
        in_specs=[rows(o.shape[-1]) for o in o_parts]
        + [rows(D_MODEL), full(w_out0), full(post0), rows(D_MODEL), full(w["pre"]), full(w["w_in"]),
           pl.BlockSpec((3, tm, LANES), lambda b, i: (0, i, 0))],
        out_specs=(rows(D_MODEL), rows(C_WIDTH), rows(256),
                   pl.BlockSpec((1, C_KV_HEADS, tm // LANES, LANES, LANES), lambda b, i: (b, 0, i, 0, 0)),
                   rows(C_WIDTH), tail_spec, tail_spec),
        out_shape=out_shape,
        compiler_params=_params(2),
        name="c_in_proj",
    )(*o_parts, sg0, w_out0, post0, x, w["pre"], w["w_in"], rope)


def _out_body(*refs):
    o_refs, (sg_ref, w_ref, g_ref, h_ref, out_ref) = refs[:-5], refs[-5:]
    out_ref[0] = _mix_out(o_refs, sg_ref, w_ref, g_ref, h_ref)


def _out_proj(o_parts, sg, w_out, post_g, h, tm):
    bsz, s, _ = h.shape

    def rows(width):
        return pl.BlockSpec((1, tm, width), lambda b, i: (b, i, 0))

    def full(a):
        return pl.BlockSpec(a.shape, lambda b, i: (0,) * a.ndim)

    return pl.pallas_call(
        _out_body,
        grid=(bsz, s // tm),
        in_specs=[rows(o.shape[-1]) for o in o_parts] + [rows(D_MODEL), full(w_out), full(post_g), rows(D_MODEL)],
        out_specs=rows(D_MODEL),
        out_shape=jax.ShapeDtypeStruct(h.shape, F32),
        compiler_params=_params(2),
        name="out_proj",
    )(*o_parts, sg, w_out, post_g, h)


def _mla_prompt_body(q_ref, k_ref, vt_ref, qmask_ref, o_ref, m_sc, acc_sc, s_sc, *, tq):
    n_tiles = q_ref.shape[1] // tq
    n_pairs = n_tiles * (n_tiles + 1) // 2
    assert n_pairs % 2 == 0 and n_pairs >= 4
    m_sc[...] = jnp.full(m_sc.shape, NEG_INF, F32)
    acc_sc[...] = jnp.zeros(acc_sc.shape, F32)
    contract_last = (((1,), (1,)), ((), ()))

    def scores(qi, j, slot):
        q_start = pl.multiple_of(qi * tq, tq)
        k_start = pl.multiple_of(j * tq, tq)
        qmask = jnp.where(j == qi, qmask_ref[...], jnp.zeros_like(qmask_ref[...]))
        for hh in range(2):
            blk = slice(hh * LANES, (hh + 1) * LANES)
            q = q_ref[0, pl.ds(q_start, tq), blk] + qmask
            s_sc[slot, hh] = lax.dot_general(k_ref[0, pl.ds(k_start, tq), blk], q, contract_last,
                                             preferred_element_type=F32)

    def consume(qi, j, slot):
        for hh in range(2):
            s = s_sc[slot, hh]
            vt = vt_ref[0, j, hh * LANES:(hh + 1) * LANES, :]
            m_old = m_sc[qi, hh]
            m_new = jnp.maximum(m_old, jnp.max(s, axis=0, keepdims=True))
            p = jnp.exp2(s - m_new).astype(BF16)
            acc_sc[qi, hh] = jnp.exp2(m_old - m_new) * acc_sc[qi, hh] + jnp.dot(vt, p, preferred_element_type=F32)
            m_sc[qi, hh] = m_new

    def advance(qi, j):
        last = j == qi
        return jnp.where(last, qi + 1, qi), jnp.where(last, 0, j + 1)

    def two_stages(cur):
        nxt = advance(*cur)
        scores(*nxt, 1)
        consume(*cur, 0)
        return nxt, advance(*nxt)

    def pair_body(i, cur):
        nxt, nxt2 = two_stages(cur)
        scores(*nxt2, 0)
        consume(*nxt, 1)
        return nxt2

    zero = jnp.int32(0)
    scores(zero, zero, 0)
    cur = lax.fori_loop(0, n_pairs // 2 - 1, pair_body, (zero, zero))
    nxt, _ = two_stages(cur)
    consume(*nxt, 1)

    lo = lax.broadcasted_iota(jnp.int32, (1, LANES), 1) < HALF

    def write_tile(qi, carry):
        outs = []
        for hh in range(2):
            acc = acc_sc[qi, hh]
            outs.append((acc / acc[A_V:A_V + 1, :]).T)
        o_ref[0, pl.ds(pl.multiple_of(qi * tq, tq), tq), :] = jnp.where(
            lo, outs[0], pltpu.roll(outs[1], HALF, 1)).astype(o_ref.dtype)
        return carry

    lax.fori_loop(0, n_tiles, write_tile, 0)


def _mla_prompt(qa, ka, vt, tq):
    bsz, s, _ = qa.shape
    pairs = A_HEADS // 2
    n_tiles = s // tq
    assert vt.shape[3] == tq and tq == A_MASK_CHUNKS * CHUNK
    q_chunk = np.arange(tq)[:, None] // CHUNK
    lane_chunk = np.arange(LANES)[None, :] - A_QK
    qmask = np.where((lane_chunk > q_chunk) & (lane_chunk < A_MASK_CHUNKS), NEG_INF, 0.0).astype(np.float32)
    return pl.pallas_call(
        functools.partial(_mla_prompt_body, tq=tq),
        grid=(bsz, pairs),
        in_specs=[pl.BlockSpec((1, s, 2 * LANES), lambda b, h: (b, 0, h)),
                  pl.BlockSpec((1, s, 2 * LANES), lambda b, h: (b, 0, h)),
                  pl.BlockSpec((1, n_tiles, 2 * LANES, tq), lambda b, h: (b, 0, h, 0)),
                  pl.BlockSpec((tq, LANES), lambda b, h: (0, 0))],
        out_specs=pl.BlockSpec((1, s, LANES), lambda b, h: (b, 0, h)),
        out_shape=jax.ShapeDtypeStruct((bsz, s, A_WIDTH), BF16),
        scratch_shapes=[pltpu.VMEM((n_tiles, 2, 1, tq), F32), pltpu.VMEM((n_tiles, 2, LANES, tq), F32),
                        pltpu.VMEM((2, 2, tq, tq), F32)],
        compiler_params=_params(2),
        name="mla_prompt",
    )(qa, ka, vt, jnp.asarray(qmask, BF16))


def _mla_sample_body(q_ref, cc_ref, ckr_ref, cn_ref, krn_ref, wkt_ref, wv_ref, sel_ref, o_ref,
                     qabs_sc, qr_sc, m_sc, l_sc, acc_sc, *, tk):
    t = q_ref.shape[1]
    past = cc_ref.shape[1]
    for h in range(A_HEADS):
        qh = q_ref[0, :, h * LANES:(h + 1) * LANES]
        rows = slice(h * t, (h + 1) * t)
        qabs_sc[rows, :] = jnp.dot(qh, wkt_ref[h], preferred_element_type=F32).astype(BF16)
        qr_sc[rows, :] = jnp.dot(qh, sel_ref[...], preferred_element_type=F32).astype(BF16)
    m_sc[...] = jnp.full(m_sc.shape, NEG_INF, F32)
    l_sc[...] = jnp.zeros(l_sc.shape, F32)
    acc_sc[...] = jnp.zeros(acc_sc.shape, F32)
    contract_last = (((1,), (1,)), ((), ()))

    def update(c_t, kr_t):
        s = (lax.dot_general(qabs_sc[...], c_t, contract_last, preferred_element_type=F32)
             + lax.dot_general(qr_sc[...], kr_t, contract_last, preferred_element_type=F32))
        m_old = m_sc[...]
        m_new = jnp.maximum(m_old, jnp.max(s, axis=-1, keepdims=True))
        p = jnp.exp2(s - m_new)
        alpha = jnp.exp2(m_old - m_new)
        l_sc[...] = alpha * l_sc[...] + jnp.sum(p, axis=-1, keepdims=True)
        acc_sc[...] = alpha * acc_sc[...] + jnp.dot(p.astype(BF16), c_t, preferred_element_type=F32)
        m_sc[...] = m_new

    def loop_body(j, carry):
        start = pl.multiple_of(j * tk, tk)
        update(cc_ref[0, pl.ds(start, tk), :].astype(BF16), ckr_ref[0, pl.ds(start, tk), :].astype(BF16))
        return carry

    lax.fori_loop(0, past // tk, loop_body, 0)
    update(cn_ref[0].astype(BF16), krn_ref[0].astype(BF16))

    o_lat = (acc_sc[...] / l_sc[...]).astype(BF16)
    out = jnp.zeros((t, A_WIDTH), F32)
    for h in range(A_HEADS):
        out = out + jnp.dot(o_lat[h * t:(h + 1) * t, :], wv_ref[h], preferred_element_type=F32)
    o_ref[0] = out.astype(o_ref.dtype)


def _mla_sample(qa, cache_c, cache_kr, c_new, kr_new, wkt, wv, sel, tk):
    bsz, t, _ = qa.shape
    past = cache_c.shape[1]

    def per_b(shape):
        return pl.BlockSpec((1,) + shape, lambda b: (b, 0, 0))

    def full(a):
        return pl.BlockSpec(a.shape, lambda b: (0,) * a.ndim)

    rows = A_HEADS * t
    return pl.pallas_call(
        functools.partial(_mla_sample_body, tk=tk),
        grid=(bsz,),
        in_specs=[per_b((t, A_HEADS * LANES)), per_b((past, A_KV_RANK)), per_b((past, A_ROPE)),
                  per_b((t, A_KV_RANK)), per_b((t, A_ROPE)), full(wkt), full(wv), full(sel)],
        out_specs=per_b((t, A_WIDTH)),
        out_shape=jax.ShapeDtypeStruct((bsz, t, A_WIDTH), BF16),
        scratch_shapes=[pltpu.VMEM((rows, A_KV_RANK), BF16), pltpu.VMEM((rows, A_ROPE), BF16),
                        pltpu.VMEM((rows, 1), F32), pltpu.VMEM((rows, 1), F32),
                        pltpu.VMEM((rows, A_KV_RANK), F32)],
        compiler_params=_params(1),
        name="mla_sample",
    )(qa, cache_c, cache_kr, c_new, kr_new, wkt, wv, sel)


def _band_body(*refs, n_blk, n_qblk, cpg, ones_row, has_sink):
    if has_sink:
        q_ref, k_ref, vt_ref, bias_ref, sink_ref, o_ref = refs
    else:
        q_ref, k_ref, vt_ref, bias_ref, o_ref = refs
    tq = q_ref.shape[1]
    win = n_blk * LANES
    t = pl.program_id(2)
    lo = lax.broadcasted_iota(jnp.int32, (CHUNK, LANES), 1) < HALF
    contract_last = (((1,), (1,)), ((), ()))

    def scores(gl, wb, variant):
        pieces = []
        for cc in range(cpg):
            rows = slice((gl * cpg + cc) * CHUNK, (gl * cpg + cc + 1) * CHUNK)
            for r in range(n_qblk):
                qblk = q_ref[0, rows, r * LANES:(r + 1) * LANES]
                zero = jnp.zeros_like(qblk)
                pieces += [jnp.where(lo, qblk, zero), jnp.where(lo, zero, qblk)]
        qs = jnp.concatenate(pieces, axis=0)
        start = wb * LANES if isinstance(wb, int) else pl.multiple_of(wb * LANES, LANES)
        kw = k_ref[0, pl.ds(start, win), :]
        return lax.dot_general(kw, qs, contract_last, preferred_element_type=F32) + bias_ref[0, variant]

    def finish(gl, wb, s):
        m = jnp.max(s, axis=0, keepdims=True)
        if has_sink:
            sink = sink_ref[0]
            m = jnp.maximum(m, sink)
        p = jnp.exp2(s - m)
        vt = jnp.concatenate([vt_ref[0, 0, wb + i] for i in range(n_blk)], axis=1)
        o = jnp.dot(vt, p.astype(BF16), preferred_element_type=F32)
        l = o[CHUNK:CHUNK + 1, :] if ones_row else jnp.sum(p, axis=0, keepdims=True)
        if has_sink:
            l = l + jnp.exp2(sink - m)
        o = (o / l).T
        idx = 0
        for cc in range(cpg):
            rows = slice((gl * cpg + cc) * CHUNK, (gl * cpg + cc + 1) * CHUNK)
            for r in range(n_qblk):
                top = o[idx * CHUNK:(idx + 1) * CHUNK]
                bot = o[(idx + 1) * CHUNK:(idx + 2) * CHUNK]
                if ones_row:
                    bot = pltpu.roll(bot, HALF, 1)
                o_ref[0, rows, r * LANES:(r + 1) * LANES] = jnp.where(lo, top, bot).astype(o_ref.dtype)
                idx += 2

    n_groups = tq // (cpg * CHUNK)
    if bias_ref.shape[1] == 1:
        finish(0, 0, scores(0, 0, 0))
        return
    assert n_groups >= n_blk - 1

    def run(window_block, variant):
        s = scores(0, window_block(0), variant(0))
        for gl in range(n_groups):
            s_next = scores(gl + 1, window_block(gl + 1), variant(gl + 1)) if gl + 1 < n_groups else None
            finish(gl, window_block(gl), s)
            s = s_next

    @pl.when(t == 0)
    def _():
        run(lambda gl: max(gl - (n_blk - 1), 0), lambda gl: min(gl, n_blk - 1))

    @pl.when(t > 0)
    def _():
        run(lambda gl: t * n_groups + gl - (n_blk - 1), lambda gl: n_blk - 1)


def _band_attention(q, k, vt, bias, sink, *, n_groups, n_qblk, cpg, ones_row, tq):
    bsz, s, _ = q.shape
    sk = k.shape[1]
    n_blk = bias.shape[2] // LANES
    qw = n_qblk * LANES
    in_specs = [pl.BlockSpec((1, tq, qw), lambda b, g, i: (b, i, g)),
                pl.BlockSpec((1, sk, LANES), lambda b, g, i: (b, 0, g)),
                pl.BlockSpec((1, 1) + vt.shape[2:], lambda b, g, i: (b, g, 0, 0, 0)),
                pl.BlockSpec((1,) + bias.shape[1:], lambda b, g, i: (g, 0, 0, 0))]
    args = [q, k, vt, bias]
    if sink is not None:
        in_specs.append(pl.BlockSpec((1,) + sink.shape[1:], lambda b, g, i: (g, 0, 0)))
        args.append(sink)
    body = functools.partial(_band_body, n_blk=n_blk, n_qblk=n_qblk, cpg=cpg, ones_row=ones_row,
                             has_sink=sink is not None)
    return pl.pallas_call(
        body,
        grid=(bsz, n_groups, s // tq),
        in_specs=in_specs,
        out_specs=pl.BlockSpec((1, tq, qw), lambda b, g, i: (b, i, g)),
        out_shape=jax.ShapeDtypeStruct(q.shape, BF16),
        compiler_params=_params(3),
        name="band_attention",
    )(*args)


def _rope_tables(pos, rot, lane_pattern):
    half = rot // 2
    inv = jnp.power(ROPE_THETA, -jnp.arange(half, dtype=F32) * 2.0 / rot)
    inv_lane, first, second = [], [], []
    for kind, width in lane_pattern:
        if kind == "rot":
            inv_lane += [inv, inv]
            first += [1.0] * half + [0.0] * half
            second += [0.0] * half + [1.0] * half
        else:
            inv_lane.append(jnp.zeros((width,), F32))
            first += [0.0] * width
            second += [0.0] * width
    ang = pos.astype(F32)[:, None] * jnp.concatenate(inv_lane)[None, :]
    cos, sin = jnp.cos(ang), jnp.sin(ang)
    return jnp.stack([cos, sin * np.asarray(second, np.float32), -sin * np.asarray(first, np.float32)])


A_ROPE_PATTERN = (("pad", A_NOPE), ("rot", A_ROPE), ("pad", LANES - A_QK))
C_ROPE_PATTERN = (("rot", C_ROT), ("pad", HALF - C_ROT)) * 2


def _prep_ab(pre, post, w_in, q_norm, kv_norm, w_uq, w_ukv, rel_bias, w_out):
    d = w_in.shape[0]
    q_lat, c_kv, k_r, g_a, q_b, k_b, v_b, g_b = jnp.split(
        w_in, [384, 640, 672, 1184, 1696, 2208, 2720], axis=1)
    kr_blk = jnp.concatenate([jnp.zeros((d, A_NOPE), F32), k_r, jnp.zeros((d, LANES - A_QK), F32)], axis=1)
    w_in_p = jnp.concatenate([q_lat, c_kv, kr_blk, g_a, g_b, q_b, k_b, v_b], axis=1).astype(BF16)
    w_uq_p = jnp.pad(w_uq.reshape(A_Q_RANK, A_HEADS, A_QK), ((0, 0), (0, 0), (0, LANES - A_QK)))
    w_uq_p = w_uq_p.reshape(A_Q_RANK, A_HEADS * LANES).astype(BF16)
    ukv = w_ukv.reshape(A_KV_RANK, A_HEADS, A_NOPE + A_V)
    w_uk, w_uv = ukv[..., :A_NOPE], ukv[..., A_NOPE:]
    pad_half = ((0, 0), (0, 0), (0, LANES - A_NOPE))
    w_k = jnp.pad(w_uk, pad_half).reshape(A_KV_RANK, A_HEADS * LANES).astype(BF16)
    w_vt = jnp.pad(w_uv, pad_half).reshape(A_KV_RANK, A_HEADS * LANES).T.astype(BF16)
    wkt = jnp.pad(jnp.transpose(w_uk, (1, 2, 0)), ((0, 0), (0, LANES - A_NOPE), (0, 0))).astype(BF16)
    eye = jnp.eye(A_HEADS, dtype=F32)
    wv_s = (jnp.transpose(w_uv, (1, 0, 2))[:, :, None, :] * eye[:, None, :, None]).reshape(
        A_HEADS, A_KV_RANK, A_WIDTH).astype(BF16)
    sel = (jnp.arange(LANES)[:, None] == A_NOPE + jnp.arange(A_ROPE)[None, :]).astype(BF16)
    rel_bias = rel_bias * LOG2E
    win = B_WIN_BLOCKS * LANES
    r0 = win - CHUNK
    x_len = r0 + win
    n_vec = x_len + CHUNK
    n_hi = r0 + CHUNK - 1 - B_MAX_REL
    n_lo = n_vec - n_hi - (2 * B_MAX_REL + 1)
    vec = jnp.concatenate([jnp.broadcast_to(rel_bias[:, -1:], (B_HEADS, n_hi)), rel_bias[:, ::-1],
                           jnp.broadcast_to(rel_bias[:, :1], (B_HEADS, n_lo))], axis=1)
    skew = jnp.tile(vec, (1, CHUNK))[:, :CHUNK * (n_vec - 1)].reshape(B_HEADS, CHUNK, n_vec - 1)
    toep = skew[:, :, CHUNK - 1:CHUNK - 1 + x_len]
    toep = jnp.transpose(toep.reshape(B_HEADS // 2, 2, CHUNK, x_len), (0, 3, 1, 2)).reshape(B_HEADS // 2, x_len, LANES)
    key_chunk = jnp.arange(win)[None, :, None] // CHUNK

    def window(first_rel, lo_chunk, hi_chunk):
        x0 = r0 - first_rel
        return jnp.where((key_chunk >= lo_chunk) & (key_chunk <= hi_chunk), toep[:, x0:x0 + win, :], NEG_INF)

    variants = []
    for v in range(B_WIN_BLOCKS):
        chunks = [2 * v + cc for cc in range(2)]
        variants.append(jnp.concatenate([window(c * CHUNK, c - B_PAST_CHUNKS, c) for c in chunks], axis=-1))
    bias_p = jnp.stack(variants, axis=1)
    bias_s = window(r0, 1, B_PAST_CHUNKS + 1)[:, None]
    return dict(pre=pre[None], w_in=w_in_p, q_norm=q_norm[None], kv_norm=kv_norm[None], w_uq=w_uq_p, w_k=w_k,
                w_vt=w_vt, wkt=wkt, wv_s=wv_s, sel=sel, bias_p=bias_p, bias_s=bias_s, w_out=w_out.astype(BF16),
                post=post[None])


def _prep_c(pre, post, w_in, sinks, w_out):
    q, k, v, g = jnp.split(w_in, [1024, 1152, 1280], axis=1)
    k0, k1 = k[:, :C_HEAD_DIM], k[:, C_HEAD_DIM:]
    v0, v1 = v[:, :C_HEAD_DIM], v[:, C_HEAD_DIM:]
    zero = jnp.zeros_like(v0)
    w_in_p = jnp.concatenate([q, k0, k0, k1, k1, v0, zero, v1, zero, g], axis=1).astype(BF16)
    win = C_WIN_BLOCKS * LANES
    per_chunk = C_GROUP * CHUNK
    key_chunk = jnp.arange(win)[:, None] // CHUNK

    def window(lo_chunk, hi_chunk):
        m = jnp.where((key_chunk >= lo_chunk) & (key_chunk <= hi_chunk), 0.0, NEG_INF).astype(F32)
        return jnp.broadcast_to(m, (win, per_chunk))

    variants = [jnp.concatenate([window(c - C_PAST_CHUNKS, c) for c in (2 * v, 2 * v + 1)], axis=-1)
                for v in range(C_WIN_BLOCKS)]
    mask_p = jnp.broadcast_to(jnp.stack(variants)[None], (C_KV_HEADS, C_WIN_BLOCKS, win, 2 * per_chunk))
    mask_s = jnp.broadcast_to(window(1, C_PAST_CHUNKS + 1)[None, None], (C_KV_HEADS, 1, win, per_chunk))
    sink_row = jnp.repeat((sinks * LOG2E).reshape(C_KV_HEADS, C_GROUP), CHUNK, axis=1)[:, None, :]
    return dict(pre=pre[None], w_in=w_in_p, mask_p=mask_p, mask_s=mask_s, sink_s=sink_row,
                sink_p=jnp.tile(sink_row, (1, 1, 2)), w_out=w_out.astype(BF16), post=post[None])


def _dup_heads(x):
    return jnp.concatenate([x[:, :, 0], x[:, :, 0], x[:, :, 1], x[:, :, 1]], axis=-1)


def kernel(x_prompt, x_sample, cache_a_ckv, cache_a_krope, cache_b_k, cache_b_v, cache_c_k, cache_c_v,
           ab_pre_norm, ab_post_norm, ab_w_in, ab_q_norm, ab_kv_norm, ab_w_uq, ab_w_ukv, ab_rel_bias, ab_w_out,
           c_pre_norm, c_post_norm, c_w_in, c_sinks, c_w_out):
    bsz, seq, _ = x_prompt.shape
    dbs, dseq, _ = x_sample.shape
    past = cache_a_ckv.shape[2]
    n_s = dbs * dseq
    pos_p = jnp.arange(seq, dtype=jnp.int32)
    pos_s = jnp.tile(past + jnp.arange(dseq, dtype=jnp.int32), dbs)
    wab = _prep_ab(ab_pre_norm[0], ab_post_norm[0], ab_w_in[0], ab_q_norm[0], ab_kv_norm[0], ab_w_uq[0],
                   ab_w_ukv[0], ab_rel_bias[0], ab_w_out[0])
    wc = _prep_c(c_pre_norm[0], c_post_norm[0], c_w_in[0], c_sinks[0], c_w_out[0])
    b_tail = min(B_PAST_CHUNKS * CHUNK, seq)
    c_tail = min(C_WINDOW, seq)
    tile = 512

    rope_a_p = _rope_tables(pos_p, A_ROPE, A_ROPE_PATTERN)
    (qa, ka, vt, qb, kb, vbt, sg, c_new_p, kr_new_p, kb_tail, vb_tail) = _ab_in_proj(
        x_prompt, rope_a_p, wab, tm=256, tail_len=b_tail, tkv=tile)
    o_a = _mla_prompt(qa, ka, vt, tq=tile)
    o_b = _band_attention(qb, kb, vbt, wab["bias_p"], None, n_groups=B_HEADS // 2, n_qblk=1, cpg=GROUP_CHUNKS,
                          ones_row=False, tq=tile)

    rope_a_s = _rope_tables(pos_s, A_ROPE, A_ROPE_PATTERN)
    xs = x_sample.reshape(1, n_s, D_MODEL)
    (qa_s, _, _, qb_s, _, _, sg_s, c_new_s, kr_new_s, kb_s32, vb_s32) = _ab_in_proj(
        xs, rope_a_s, wab, tm=n_s, tail_len=n_s, tkv=n_s)
    o_a_s = _mla_sample(qa_s.reshape(dbs, dseq, -1), cache_a_ckv[0], cache_a_krope[0],
                        c_new_s.reshape(dbs, dseq, -1), kr_new_s.reshape(dbs, dseq, -1),
                        wab["wkt"], wab["wv_s"], wab["sel"], tk=512)
    wb = cache_b_k.shape[2]
    pad_b = jnp.zeros((dbs, B_WIN_BLOCKS * LANES - wb - dseq, B_WIDTH), F32)
    kband = jnp.concatenate([pad_b, cache_b_k[0].reshape(dbs, wb, B_WIDTH), kb_s32.reshape(dbs, dseq, -1)], 1)
    vband = jnp.concatenate([pad_b, cache_b_v[0].reshape(dbs, wb, B_WIDTH), vb_s32.reshape(dbs, dseq, -1)], 1)
    vband_t = jnp.transpose(vband.astype(BF16).reshape(dbs, B_WIN_BLOCKS, LANES, B_HEADS // 2, LANES), (0, 3, 1, 4, 2))
    o_b_s = _band_attention(qb_s.reshape(dbs, dseq, -1), kband.astype(BF16), vband_t, wab["bias_s"], None,
                            n_groups=B_HEADS // 2, n_qblk=1, cpg=1, ones_row=False, tq=dseq)

    rope_c_p = _rope_tables(pos_p, C_ROT, C_ROPE_PATTERN)
    h1_p, qc, kc, vct, sgc, kc_tail, vc_tail = _c_in_proj(
        [o_a, o_b], sg, wab["w_out"], wab["post"], x_prompt, rope_c_p, wc, tm=tile, tail_len=c_tail)
    o_c = _band_attention(qc, kc, vct, wc["mask_p"], wc["sink_p"], n_groups=C_KV_HEADS, n_qblk=C_GROUP // 2,
                          cpg=GROUP_CHUNKS, ones_row=True, tq=tile)
    h2_p = _out_proj([o_c], sgc, wc["w_out"], wc["post"], h1_p, tm=tile)

    rope_c_s = _rope_tables(pos_s, C_ROT, C_ROPE_PATTERN)
    h1_s, qc_s, kc_s, _, sgc_s, kc_s32, vc_s32 = _c_in_proj(
        [o_a_s.reshape(1, n_s, -1), o_b_s.reshape(1, n_s, -1)], sg_s, wab["w_out"], wab["post"], xs, rope_c_s, wc,
        tm=n_s, tail_len=n_s)
    wcw = cache_c_k.shape[2]
    win_c = C_WIN_BLOCKS * LANES
    n_pad = win_c - wcw - dseq
    kcb = jnp.concatenate([jnp.zeros((dbs, n_pad, C_KV_HEADS * LANES), BF16), _dup_heads(cache_c_k[0]).astype(BF16),
                           kc_s.reshape(dbs, dseq, -1)], axis=1)
    vcb = jnp.concatenate([jnp.zeros((dbs, n_pad, C_KV_HEADS, C_HEAD_DIM), F32), cache_c_v[0],
                           vc_s32.reshape(dbs, dseq, C_KV_HEADS, C_HEAD_DIM)], axis=1)
    vcb_t = jnp.concatenate([jnp.transpose(vcb, (0, 2, 3, 1)), jnp.ones((dbs, C_KV_HEADS, 1, win_c), F32),
                             jnp.zeros((dbs, C_KV_HEADS, LANES - C_HEAD_DIM - 1, win_c), F32)], axis=2)
    vcb_t = jnp.transpose(vcb_t.astype(BF16).reshape(dbs, C_KV_HEADS, LANES, C_WIN_BLOCKS, LANES), (0, 1, 3, 2, 4))
    o_c_s = _band_attention(qc_s.reshape(dbs, dseq, -1), kcb, vcb_t, wc["mask_s"], wc["sink_s"], n_groups=C_KV_HEADS,
                            n_qblk=C_GROUP // 2, cpg=1, ones_row=True, tq=dseq)
    h2_s = _out_proj([o_c_s.reshape(1, n_s, -1)], sgc_s, wc["w_out"], wc["post"], h1_s, tm=n_s)

    def roll_in(buf, new):
        return jnp.concatenate([buf, new], axis=1)[:, -buf.shape[1]:][None]

    return (h2_p, h2_s.reshape(dbs, dseq, D_MODEL),
            c_new_p[None], kr_new_p[None],
            kb_tail.reshape(1, bsz, b_tail, B_HEADS, B_HEAD_DIM), vb_tail.reshape(1, bsz, b_tail, B_HEADS, B_HEAD_DIM),
            kc_tail.reshape(1, bsz, c_tail, C_KV_HEADS, C_HEAD_DIM), vc_tail.reshape(1, bsz, c_tail, C_KV_HEADS, C_HEAD_DIM),
            c_new_s.reshape(1, dbs, dseq, A_KV_RANK), kr_new_s.reshape(1, dbs, dseq, A_ROPE),
            roll_in(cache_b_k[0], kb_s32.reshape(dbs, dseq, B_HEADS, B_HEAD_DIM)),
            roll_in(cache_b_v[0], vb_s32.reshape(dbs, dseq, B_HEADS, B_HEAD_DIM)),
            roll_in(cache_c_k[0], kc_s32.reshape(dbs, dseq, C_KV_HEADS, C_HEAD_DIM)),
            roll_in(cache_c_v[0], vc_s32.reshape(dbs, dseq, C_KV_HEADS, C_HEAD_DIM)))
```

```python
import functools

import jax
import jax.numpy as jnp
import numpy as np
from jax import lax
from jax.experimental import pallas as pl
from jax.experimental.pallas import tpu as pltpu

F32 = jnp.float32
BF16 = jnp.bfloat16

D_MODEL = 1024
CHUNK = 64
ROPE_THETA = 500000.0
RMS_EPS = 1e-6
NEG_INF = -1e30

A_HEADS = 8
A_NOPE = 64
A_ROPE = 32
A_QK = A_NOPE + A_ROPE
A_V = 64
A_Q_RANK = 384
A_KV_RANK = 256
A_WIDTH = A_HEADS * A_V
LOG2E = 1.4426950408889634
A_SCALE = A_QK ** -0.5 * LOG2E

B_HEADS = 8
B_HEAD_DIM = 64
B_WIDTH = B_HEADS * B_HEAD_DIM
B_PAST_CHUNKS = 8
B_MAX_REL = 128
B_SCALE = B_HEAD_DIM ** -0.5 * LOG2E

C_HEADS = 16
C_KV_HEADS = 2
C_GROUP = C_HEADS // C_KV_HEADS
C_HEAD_DIM = 64
C_WIDTH = C_HEADS * C_HEAD_DIM
C_WINDOW = 128
C_PAST_CHUNKS = C_WINDOW // CHUNK
C_ROT = C_HEAD_DIM // 4
C_SCALE = C_HEAD_DIM ** -0.5 * LOG2E

LANES = 128
HALF = LANES // 2
BF16_SUBLANES = 16
VMEM_LIMIT = 56 * 1024 * 1024
A_MASK_CHUNKS = 8
assert A_QK + A_MASK_CHUNKS <= LANES
A_PIPE_UNROLL = 4
B_GROUP_CHUNKS = 4
C_GROUP_CHUNKS = 2
B_WIN_BLOCKS = (B_PAST_CHUNKS + B_GROUP_CHUNKS) * CHUNK // LANES
C_WIN_BLOCKS = (C_PAST_CHUNKS + C_GROUP_CHUNKS) * CHUNK // LANES
B_VARIANTS = B_PAST_CHUNKS // B_GROUP_CHUNKS + 1
B_SAMPLE_BLOCKS = (B_PAST_CHUNKS + 2) * CHUNK // LANES
C_SAMPLE_BLOCKS = (C_PAST_CHUNKS + 2) * CHUNK // LANES

AB_Q0, AB_C0, AB_KR0, AB_G0, AB_QB0, AB_KB0, AB_VB0, AB_NZ = 0, 384, 640, 768, 1792, 2304, 2816, 3328
C_Q0, C_K0, C_V0, C_G0, C_NZ = 0, 1024, 1280, 1536, 2560


def _params(n_axes):
    return pltpu.CompilerParams(dimension_semantics=("arbitrary",) * n_axes, vmem_limit_bytes=VMEM_LIMIT)


def _rms(x, g):
    return x * lax.rsqrt(jnp.mean(x * x, axis=-1, keepdims=True) + RMS_EPS) * g


def _rope_block(blk, rope_ref, shift):
    return (blk * rope_ref[0] + pltpu.roll(blk, shift, 1) * rope_ref[1]
            + pltpu.roll(blk, LANES - shift, 1) * rope_ref[2])


def _tail_spec(n_tiles, tm, tail_len, width):
    tb = min(tail_len, tm)
    n_blk = tail_len // tb
    return tb, pl.BlockSpec((1, tb, width), lambda b, i: (b, jnp.maximum(i - (n_tiles - n_blk), 0), 0))


def _ab_in_body(x_ref, pre_ref, w_ref, qn_ref, kvn_ref, wuq_ref, wk_ref, wvt_ref, rope_ref,
                qa_ref, ka_ref, vt_ref, qb_ref, kb_ref, vbt_ref, sg_ref, c_ref, kr_ref, kb_tail_ref, vb_tail_ref):
    tm = x_ref.shape[1]
    tb = kb_tail_ref.shape[1]
    xn = _rms(x_ref[0], pre_ref[...]).astype(BF16)
    z = jnp.dot(xn, w_ref[...], preferred_element_type=F32)

    qn = _rms(z[:, AB_Q0:AB_C0], qn_ref[...]).astype(BF16)
    qa = jnp.dot(qn, wuq_ref[...], preferred_element_type=F32) * A_SCALE
    for h in range(A_HEADS):
        blk = slice(h * LANES, (h + 1) * LANES)
        qa_ref[0, :, blk] = _rope_block(qa[:, blk], rope_ref, A_ROPE // 2).astype(BF16)

    c_new = _rms(z[:, AB_C0:AB_KR0], kvn_ref[...])
    c_ref[0] = c_new
    cb = c_new.astype(BF16)
    krot = _rope_block(z[:, AB_KR0:AB_G0], rope_ref, A_ROPE // 2)
    kr_ref[0] = krot[:, A_NOPE:A_NOPE + A_ROPE]
    kn = jnp.dot(cb, wk_ref[...], preferred_element_type=F32)
    row = pl.program_id(1) * tm + lax.broadcasted_iota(jnp.int32, (tm, LANES), 0)
    lane = lax.broadcasted_iota(jnp.int32, (tm, LANES), 1)
    k_shared = krot + jnp.where(lane - A_QK == (row // CHUNK) % A_MASK_CHUNKS, 1.0, 0.0)
    for h in range(A_HEADS):
        blk = slice(h * LANES, (h + 1) * LANES)
        ka_ref[0, :, blk] = (kn[:, blk] + k_shared).astype(BF16)
    vt = jnp.dot(wvt_ref[...], c_new.T.astype(BF16), preferred_element_type=F32)
    row = lax.broadcasted_iota(jnp.int32, vt.shape, 0)
    vt_ref[0, 0] = jnp.where((row & (LANES - 1)) == A_V, 1.0, vt).astype(BF16)

    g = z[:, AB_G0:AB_QB0]
    sg_ref[0] = (g * jax.nn.sigmoid(g)).astype(BF16)
    qb_ref[0] = (z[:, AB_QB0:AB_KB0] * B_SCALE).astype(BF16)
    kb = z[:, AB_KB0:AB_VB0]
    vb = z[:, AB_VB0:AB_NZ]
    kb_ref[0] = kb.astype(BF16)
    vbt = vb.T.astype(BF16)
    for hp in range(B_HEADS // 2):
        for j in range(tm // LANES):
            vbt_ref[0, hp, j] = vbt[hp * LANES:(hp + 1) * LANES, j * LANES:(j + 1) * LANES]
    kb_tail_ref[0] = kb[tm - tb:, :]
    vb_tail_ref[0] = vb[tm - tb:, :]


def _ab_in_proj(x, rope, w, tm, tail_len, tkv):
    bsz, s, _ = x.shape
    nt = s // tm
    per_kv = tkv // tm
    tb, tail_spec = _tail_spec(nt, tm, tail_len, B_WIDTH)

    def full(a):
        return pl.BlockSpec(a.shape, lambda b, i: (0,) * a.ndim)

    def rows(width):
        return pl.BlockSpec((1, tm, width), lambda b, i: (b, i, 0))

    weights = (w["pre"], w["w_in"], w["q_norm"], w["kv_norm"], w["w_uq"], w["w_k"], w["w_vt"])
    vt_spec = pl.BlockSpec((1, 1, A_HEADS * LANES, tm), lambda b, i: (b, i // per_kv, 0, i % per_kv))
    out_shape = (
        jax.ShapeDtypeStruct((bsz, s, A_HEADS * LANES), BF16),
        jax.ShapeDtypeStruct((bsz, s, A_HEADS * LANES), BF16),
        jax.ShapeDtypeStruct((bsz, s // tkv, A_HEADS * LANES, tkv), BF16),
        jax.ShapeDtypeStruct((bsz, s, B_WIDTH), BF16),
        jax.ShapeDtypeStruct((bsz, s, B_WIDTH), BF16),
        jax.ShapeDtypeStruct((bsz, B_HEADS // 2, s // LANES, LANES, LANES), BF16),
        jax.ShapeDtypeStruct((bsz, s, A_WIDTH + B_WIDTH), BF16),
        jax.ShapeDtypeStruct((bsz, s, A_KV_RANK), F32),
        jax.ShapeDtypeStruct((bsz, s, A_ROPE), F32),
        jax.ShapeDtypeStruct((bsz, tail_len, B_WIDTH), F32),
        jax.ShapeDtypeStruct((bsz, tail_len, B_WIDTH), F32),
    )
    vbt_spec = pl.BlockSpec((1, B_HEADS // 2, tm // LANES, LANES, LANES), lambda b, i: (b, 0, i, 0, 0))
    out_specs = (rows(1024), rows(1024), vt_spec, rows(512), rows(512), vbt_spec, rows(1024),
                 rows(A_KV_RANK), rows(A_ROPE), tail_spec, tail_spec)
    return pl.pallas_call(
        _ab_in_body,
        grid=(bsz, nt),
        in_specs=[rows(D_MODEL)] + [full(a) for a in weights]
        + [pl.BlockSpec((3, tm, LANES), lambda b, i: (0, i, 0))],
        out_specs=out_specs,
        out_shape=out_shape,
        compiler_params=_params(2),
        name="ab_in_proj",
    )(x, *weights, rope)


def _mix_out(o_refs, sg_ref, w_ref, g_ref, h_ref):
    o = jnp.concatenate([r[0].astype(F32) for r in o_refs], axis=-1) if len(o_refs) > 1 else o_refs[0][0].astype(F32)
    mixed = (o * sg_ref[0].astype(F32)).astype(BF16)
    y = jnp.dot(mixed, w_ref[...], preferred_element_type=F32)
    return h_ref[0] + _rms(y, g_ref[...])


def _c_in_body(*refs):
    n_o = len(refs) - 14
    o_refs = refs[:n_o]
    (sg0_ref, w0_ref, post0_ref, x_ref, pre_ref, w_ref, rope_ref,
     h_ref, q_ref, k_ref, vt_ref, sg_ref, k_tail_ref, v_tail_ref) = refs[n_o:]
    tm = x_ref.shape[1]
    tb = k_tail_ref.shape[1]
    h = _mix_out(o_refs, sg0_ref, w0_ref, post0_ref, x_ref)
    h_ref[0] = h
    xn = _rms(h, pre_ref[...]).astype(BF16)
    z = jnp.dot(xn, w_ref[...], preferred_element_type=F32)
    half_rot = C_ROT // 2
    for j in range(C_WIDTH // LANES):
        blk = slice(C_Q0 + j * LANES, C_Q0 + (j + 1) * LANES)
        q_ref[0, :, j * LANES:(j + 1) * LANES] = (_rope_block(z[:, blk], rope_ref, half_rot) * C_SCALE).astype(BF16)
    kd = [_rope_block(z[:, C_K0 + j * LANES:C_K0 + (j + 1) * LANES], rope_ref, half_rot) for j in range(C_KV_HEADS)]
    vz = [z[:, C_V0 + j * LANES:C_V0 + (j + 1) * LANES] for j in range(C_KV_HEADS)]
    row = lax.broadcasted_iota(jnp.int32, (LANES, tm), 0)
    for j in range(C_KV_HEADS):
        k_ref[0, :, j * LANES:(j + 1) * LANES] = kd[j].astype(BF16)
        vt = jnp.where(row == C_HEAD_DIM, 1.0, vz[j].T).astype(BF16)
        for i in range(tm // LANES):
            vt_ref[0, j, i] = vt[:, i * LANES:(i + 1) * LANES]
    g = z[:, C_G0:C_NZ]
    sg_ref[0] = (g * jax.nn.sigmoid(g)).astype(BF16)
    lo = lax.broadcasted_iota(jnp.int32, (1, LANES), 1) < HALF
    k_tail_ref[0] = jnp.where(lo, kd[0], kd[1])[tm - tb:, :]
    v_tail_ref[0] = (vz[0] + pltpu.roll(vz[1], HALF, 1))[tm - tb:, :]


def _c_in_proj(o_parts, sg0, w_out0, post0, x, rope, w, tm, tail_len):
    bsz, s, _ = x.shape
    nt = s // tm
    tb, tail_spec = _tail_spec(nt, tm, tail_len, LANES)

    def full(a):
        return pl.BlockSpec(a.shape, lambda b, i: (0,) * a.ndim)

    def rows(width):
        return pl.BlockSpec((1, tm, width), lambda b, i: (b, i, 0))

    out_shape = (
        jax.ShapeDtypeStruct((bsz, s, D_MODEL), F32),
        jax.ShapeDtypeStruct((bsz, s, C_WIDTH), BF16),
        jax.ShapeDtypeStruct((bsz, s, C_KV_HEADS * LANES), BF16),
        jax.ShapeDtypeStruct((bsz, C_KV_HEADS, s // LANES, LANES, LANES), BF16),
        jax.ShapeDtypeStruct((bsz, s, C_WIDTH), BF16),
        jax.ShapeDtypeStruct((bsz, tail_len, LANES), F32),
        jax.ShapeDtypeStruct((bsz, tail_len, LANES), F32),
    )
    return pl.pallas_call(
        _c_in_body,
        grid=(bsz, nt),
        in_specs=[rows(o.shape[-1]) for o in o_parts]
        + [rows(D_MODEL), full(w_out0), full(post0), rows(D_MODEL), full(w["pre"]), full(w["w_in"]),
           pl.BlockSpec((3, tm, LANES), lambda b, i: (0, i, 0))],
        out_specs=(rows(D_MODEL), rows(C_WIDTH), rows(256),
                   pl.BlockSpec((1, C_KV_HEADS, tm // LANES, LANES, LANES), lambda b, i: (b, 0, i, 0, 0)),
                   rows(C_WIDTH), tail_spec, tail_spec),
        out_shape=out_shape,
        compiler_params=_params(2),
        name="c_in_proj",
    )(*o_parts, sg0, w_out0, post0, x, w["pre"], w["w_in"], rope)


def _out_body(*refs):
    o_refs, (sg_ref, w_ref, g_ref, h_ref, out_ref) = refs[:-5], refs[-5:]
    out_ref[0] = _mix_out(o_refs, sg_ref, w_ref, g_ref, h_ref)


def _out_proj(o_parts, sg, w_out, post_g, h, tm):
    bsz, s, _ = h.shape

    def rows(width):
        return pl.BlockSpec((1, tm, width), lambda b, i: (b, i, 0))

    def full(a):
        return pl.BlockSpec(a.shape, lambda b, i: (0,) * a.ndim)

    return pl.pallas_call(
        _out_body,
        grid=(bsz, s // tm),
        in_specs=[rows(o.shape[-1]) for o in o_parts] + [rows(D_MODEL), full(w_out), full(post_g), rows(D_MODEL)],
        out_specs=rows(D_MODEL),
        out_shape=jax.ShapeDtypeStruct(h.shape, F32),
        compiler_params=_params(2),
        name="out_proj",
    )(*o_parts, sg, w_out, post_g, h)


def _mla_prompt_body(q_ref, k_ref, vt_ref, qmask_ref, o_ref, m_sc, acc_sc, s_sc, *, tq):
    n_tiles = q_ref.shape[1] // tq
    n_pairs = n_tiles * (n_tiles + 1) // 2
    assert A_PIPE_UNROLL % 2 == 0 and n_pairs % A_PIPE_UNROLL == 0 and n_pairs >= 2 * A_PIPE_UNROLL
    m_sc[...] = jnp.full(m_sc.shape, NEG_INF, F32)
    acc_sc[...] = jnp.zeros(acc_sc.shape, F32)
    contract_last = (((1,), (1,)), ((), ()))

    def scores(qi, j, slot):
        q_start = pl.multiple_of(qi * tq, tq)
        k_start = pl.multiple_of(j * tq, tq)
        qmask = jnp.where(j == qi, qmask_ref[...], jnp.zeros_like(qmask_ref[...]))
        for hh in range(2):
            blk = slice(hh * LANES, (hh + 1) * LANES)
            q = q_ref[0, pl.ds(q_start, tq), blk] + qmask
            s_sc[slot, hh] = lax.dot_general(k_ref[0, pl.ds(k_start, tq), blk], q, contract_last,
                                             preferred_element_type=F32)

    def consume(qi, j, slot):
        for hh in range(2):
            s = s_sc[slot, hh]
            vt = vt_ref[0, j, hh * LANES:(hh + 1) * LANES, :]
            m_old = m_sc[qi, hh]
            m_new = jnp.maximum(m_old, jnp.max(s, axis=0, keepdims=True))
            p = jnp.exp2(s - m_new).astype(BF16)
            acc_sc[qi, hh] = jnp.exp2(m_old - m_new) * acc_sc[qi, hh] + jnp.dot(vt, p, preferred_element_type=F32)
            m_sc[qi, hh] = m_new

    def advance(qi, j):
        last = j == qi
        return jnp.where(last, qi + 1, qi), jnp.where(last, 0, j + 1)

    def stages(cur, count):
        for st in range(count):
            nxt = advance(*cur)
            scores(*nxt, 1 - (st & 1))
            consume(*cur, st & 1)
            cur = nxt
        return cur

    zero = jnp.int32(0)
    scores(zero, zero, 0)
    cur = lax.fori_loop(0, n_pairs // A_PIPE_UNROLL - 1, lambda i, c: stages(c, A_PIPE_UNROLL), (zero, zero))
    cur = stages(cur, A_PIPE_UNROLL - 1)
    consume(*cur, (A_PIPE_UNROLL - 1) & 1)

    lo = lax.broadcasted_iota(jnp.int32, (1, LANES), 1) < HALF

    def write_tile(qi, carry):
        outs = []
        for hh in range(2):
            acc = acc_sc[qi, hh]
            outs.append((acc / acc[A_V:A_V + 1, :]).T)
        o_ref[0, pl.ds(pl.multiple_of(qi * tq, tq), tq), :] = jnp.where(
            lo, outs[0], pltpu.roll(outs[1], HALF, 1)).astype(o_ref.dtype)
        return carry

    lax.fori_loop(0, n_tiles, write_tile, 0)


def _mla_prompt(qa, ka, vt, tq):
    bsz, s, _ = qa.shape
    pairs = A_HEADS // 2
    n_tiles = s // tq
    assert vt.shape[3] == tq and tq == A_MASK_CHUNKS * CHUNK
    q_chunk = np.arange(tq)[:, None] // CHUNK
    lane_chunk = np.arange(LANES)[None, :] - A_QK
    qmask = np.where((lane_chunk > q_chunk) & (lane_chunk < A_MASK_CHUNKS), NEG_INF, 0.0).astype(np.float32)
    return pl.pallas_call(
        functools.partial(_mla_prompt_body, tq=tq),
        grid=(bsz, pairs),
        in_specs=[pl.BlockSpec((1, s, 2 * LANES), lambda b, h: (b, 0, h)),
                  pl.BlockSpec((1, s, 2 * LANES), lambda b, h: (b, 0, h)),
                  pl.BlockSpec((1, n_tiles, 2 * LANES, tq), lambda b, h: (b, 0, h, 0)),
                  pl.BlockSpec((tq, LANES), lambda b, h: (0, 0))],
        out_specs=pl.BlockSpec((1, s, LANES), lambda b, h: (b, 0, h)),
        out_shape=jax.ShapeDtypeStruct((bsz, s, A_WIDTH), BF16),
        scratch_shapes=[pltpu.VMEM((n_tiles, 2, 1, tq), F32), pltpu.VMEM((n_tiles, 2, LANES, tq), F32),
                        pltpu.VMEM((2, 2, tq, tq), F32)],
        compiler_params=_params(2),
        name="mla_prompt",
    )(qa, ka, vt, jnp.asarray(qmask, BF16))


def _mla_sample_body(q_ref, cc_ref, ckr_ref, cn_ref, krn_ref, wkt_ref, wv_ref, sel_ref, o_ref,
                     qabs_sc, qr_sc, m_sc, l_sc, acc_sc, *, tk):
    t = q_ref.shape[1]
    past = cc_ref.shape[1]
    for h in range(A_HEADS):
        qh = q_ref[0, :, h * LANES:(h + 1) * LANES]
        rows = slice(h * t, (h + 1) * t)
        qabs_sc[rows, :] = jnp.dot(qh, wkt_ref[h], preferred_element_type=F32).astype(BF16)
        qr_sc[rows, :] = jnp.dot(qh, sel_ref[...], preferred_element_type=F32).astype(BF16)
    m_sc[...] = jnp.full(m_sc.shape, NEG_INF, F32)
    l_sc[...] = jnp.zeros(l_sc.shape, F32)
    acc_sc[...] = jnp.zeros(acc_sc.shape, F32)
    contract_last = (((1,), (1,)), ((), ()))

    def update(c_t, kr_t):
        s = (lax.dot_general(qabs_sc[...], c_t, contract_last, preferred_element_type=F32)
             + lax.dot_general(qr_sc[...], kr_t, contract_last, preferred_element_type=F32))
        m_old = m_sc[...]
        m_new = jnp.maximum(m_old, jnp.max(s, axis=-1, keepdims=True))
        p = jnp.exp2(s - m_new)
        alpha = jnp.exp2(m_old - m_new)
        l_sc[...] = alpha * l_sc[...] + jnp.sum(p, axis=-1, keepdims=True)
        acc_sc[...] = alpha * acc_sc[...] + jnp.dot(p.astype(BF16), c_t, preferred_element_type=F32)
        m_sc[...] = m_new

    def loop_body(j, carry):
        start = pl.multiple_of(j * tk, tk)
        update(cc_ref[0, pl.ds(start, tk), :].astype(BF16), ckr_ref[0, pl.ds(start, tk), :].astype(BF16))
        return carry

    lax.fori_loop(0, past // tk, loop_body, 0)
    update(cn_ref[0].astype(BF16), krn_ref[0].astype(BF16))

    o_lat = (acc_sc[...] / l_sc[...]).astype(BF16)
    out = jnp.zeros((t, A_WIDTH), F32)
    for h in range(A_HEADS):
        out = out + jnp.dot(o_lat[h * t:(h + 1) * t, :], wv_ref[h], preferred_element_type=F32)
    o_ref[0] = out.astype(o_ref.dtype)


def _mla_sample(qa, cache_c, cache_kr, c_new, kr_new, wkt, wv, sel, tk):
    bsz, t, _ = qa.shape
    past = cache_c.shape[1]

    def per_b(shape):
        return pl.BlockSpec((1,) + shape, lambda b: (b, 0, 0))

    def full(a):
        return pl.BlockSpec(a.shape, lambda b: (0,) * a.ndim)

    rows = A_HEADS * t
    return pl.pallas_call(
        functools.partial(_mla_sample_body, tk=tk),
        grid=(bsz,),
        in_specs=[per_b((t, A_HEADS * LANES)), per_b((past, A_KV_RANK)), per_b((past, A_ROPE)),
                  per_b((t, A_KV_RANK)), per_b((t, A_ROPE)), full(wkt), full(wv), full(sel)],
        out_specs=per_b((t, A_WIDTH)),
        out_shape=jax.ShapeDtypeStruct((bsz, t, A_WIDTH), BF16),
        scratch_shapes=[pltpu.VMEM((rows, A_KV_RANK), BF16), pltpu.VMEM((rows, A_ROPE), BF16),
                        pltpu.VMEM((rows, 1), F32), pltpu.VMEM((rows, 1), F32),
                        pltpu.VMEM((rows, A_KV_RANK), F32)],
        compiler_params=_params(1),
        name="mla_sample",
    )(qa, cache_c, cache_kr, c_new, kr_new, wkt, wv, sel)


def _band_body(*refs, n_blk, n_qblk, cpg, ones_row, has_sink):
    if has_sink:
        q_ref, k_ref, vt_ref, bias_ref, sink_ref, o_ref, s_sc = refs
    else:
        q_ref, k_ref, vt_ref, bias_ref, o_ref, s_sc = refs
    win = n_blk * LANES
    rows_g = cpg * CHUNK
    lo = lax.broadcasted_iota(jnp.int32, (CHUNK, LANES), 1) < HALF
    contract_last = (((1,), (1,)), ((), ()))

    def aligned(x, n):
        return x if isinstance(x, int) else pl.multiple_of(x, n)

    def scores(a, wb, variant, slot):
        pieces = []
        for cc in range(cpg):
            rows = pl.ds(aligned(a * rows_g + cc * CHUNK, CHUNK), CHUNK)
            for r in range(n_qblk):
                qblk = q_ref[0, rows, r * LANES:(r + 1) * LANES]
                zero = jnp.zeros_like(qblk)
                pieces += [jnp.where(lo, qblk, zero), jnp.where(lo, zero, qblk)]
        qs = jnp.concatenate(pieces, axis=0)
        kw = k_ref[0, pl.ds(aligned(wb * LANES, LANES), win), :]
        s_sc[slot] = lax.dot_general(kw, qs, contract_last, preferred_element_type=F32) + bias_ref[0, variant]

    def finish(a, wb, slot):
        s = s_sc[slot]
        m = jnp.max(s, axis=0, keepdims=True)
        if has_sink:
            sink = sink_ref[0]
            m = jnp.maximum(m, sink)
        p = jnp.exp2(s - m)
        vt = jnp.concatenate([vt_ref[0, 0, wb + i] for i in range(n_blk)], axis=1)
        if not ones_row:
            vt = jnp.concatenate([vt, jnp.ones((BF16_SUBLANES, win), BF16)], axis=0)
        o = jnp.dot(vt, p.astype(BF16), preferred_element_type=F32)
        l = o[CHUNK:CHUNK + 1, :] if ones_row else o[LANES:LANES + 1, :]
        o = o[:LANES]
        if has_sink:
            l = l + jnp.exp2(sink - m)
        o = (o / l).T
        idx = 0
        for cc in range(cpg):
            rows = pl.ds(aligned(a * rows_g + cc * CHUNK, CHUNK), CHUNK)
            for r in range(n_qblk):
                top = o[idx * CHUNK:(idx + 1) * CHUNK]
                bot = o[(idx + 1) * CHUNK:(idx + 2) * CHUNK]
                if ones_row:
                    bot = pltpu.roll(bot, HALF, 1)
                o_ref[0, rows, r * LANES:(r + 1) * LANES] = jnp.where(lo, top, bot).astype(o_ref.dtype)
                idx += 2

    n_total = q_ref.shape[1] // rows_g
    n_var = bias_ref.shape[1]
    if n_total == 1:
        scores(0, 0, 0, 0)
        finish(0, 0, 0)
        return
    g_blk = rows_g // LANES
    n_lead = n_var - 1 + (n_var - 1) % 2
    assert (n_var - 1) * g_blk >= n_blk - g_blk and (n_total - n_lead) % 2 == 0 and n_total - n_lead >= 2

    def window_block(a):
        wb = (a + 1) * g_blk - n_blk
        return max(wb, 0) if isinstance(a, int) else wb

    def stage(a, slot):
        nxt = a + 1
        scores(nxt, window_block(nxt), min(nxt, n_var - 1) if isinstance(nxt, int) else n_var - 1, 1 - slot)
        finish(a, window_block(a), slot)

    scores(0, window_block(0), 0, 0)
    for a in range(n_lead):
        stage(a, a & 1)

    def pair_body(i, carry):
        a = n_lead + 2 * i
        stage(a, 0)
        stage(a + 1, 1)
        return carry

    lax.fori_loop(0, (n_total - n_lead) // 2 - 1, pair_body, 0)
    stage(n_total - 2, 0)
    finish(n_total - 1, window_block(n_total - 1), 1)


def _band_attention(q, k, vt, bias, sink, *, n_groups, n_qblk, cpg, ones_row):
    bsz, s, _ = q.shape
    sk = k.shape[1]
    n_blk = bias.shape[2] // LANES
    qw = n_qblk * LANES
    in_specs = [pl.BlockSpec((1, s, qw), lambda b, g: (b, 0, g)),
                pl.BlockSpec((1, sk, LANES), lambda b, g: (b, 0, g)),
                pl.BlockSpec((1, 1) + vt.shape[2:], lambda b, g: (b, g, 0, 0, 0)),
                pl.BlockSpec((1,) + bias.shape[1:], lambda b, g: (g, 0, 0, 0))]
    args = [q, k, vt, bias]
    if sink is not None:
        in_specs.append(pl.BlockSpec((1,) + sink.shape[1:], lambda b, g: (g, 0, 0)))
        args.append(sink)
    body = functools.partial(_band_body, n_blk=n_blk, n_qblk=n_qblk, cpg=cpg, ones_row=ones_row,
                             has_sink=sink is not None)
    return pl.pallas_call(
        body,
        grid=(bsz, n_groups),
        in_specs=in_specs,
        out_specs=pl.BlockSpec((1, s, qw), lambda b, g: (b, 0, g)),
        out_shape=jax.ShapeDtypeStruct(q.shape, BF16),
        scratch_shapes=[pltpu.VMEM((2,) + bias.shape[2:], F32)],
        compiler_params=_params(2),
        name="band_attention",
    )(*args)


def _rope_tables(pos, rot, lane_pattern):
    half = rot // 2
    inv = jnp.power(ROPE_THETA, -jnp.arange(half, dtype=F32) * 2.0 / rot)
    inv_lane, first, second = [], [], []
    for kind, width in lane_pattern:
        if kind == "rot":
            inv_lane += [inv, inv]
            first += [1.0] * half + [0.0] * half
            second += [0.0] * half + [1.0] * half
        else:
            inv_lane.append(jnp.zeros((width,), F32))
            first += [0.0] * width
            second += [0.0] * width
    ang = pos.astype(F32)[:, None] * jnp.concatenate(inv_lane)[None, :]
    cos, sin = jnp.cos(ang), jnp.sin(ang)
    return jnp.stack([cos, sin * np.asarray(second, np.float32), -sin * np.asarray(first, np.float32)])


A_ROPE_PATTERN = (("pad", A_NOPE), ("rot", A_ROPE), ("pad", LANES - A_QK))
C_ROPE_PATTERN = (("rot", C_ROT), ("pad", HALF - C_ROT)) * 2


def _prep_ab(pre, post, w_in, q_norm, kv_norm, w_uq, w_ukv, rel_bias, w_out):
    d = w_in.shape[0]
    q_lat, c_kv, k_r, g_a, q_b, k_b, v_b, g_b = jnp.split(
        w_in, [384, 640, 672, 1184, 1696, 2208, 2720], axis=1)
    kr_blk = jnp.concatenate([jnp.zeros((d, A_NOPE), F32), k_r, jnp.zeros((d, LANES - A_QK), F32)], axis=1)
    w_in_p = jnp.concatenate([q_lat, c_kv, kr_blk, g_a, g_b, q_b, k_b, v_b], axis=1).astype(BF16)
    w_uq_p = jnp.pad(w_uq.reshape(A_Q_RANK, A_HEADS, A_QK), ((0, 0), (0, 0), (0, LANES - A_QK)))
    w_uq_p = w_uq_p.reshape(A_Q_RANK, A_HEADS * LANES).astype(BF16)
    ukv = w_ukv.reshape(A_KV_RANK, A_HEADS, A_NOPE + A_V)
    w_uk, w_uv = ukv[..., :A_NOPE], ukv[..., A_NOPE:]
    pad_half = ((0, 0), (0, 0), (0, LANES - A_NOPE))
    w_k = jnp.pad(w_uk, pad_half).reshape(A_KV_RANK, A_HEADS * LANES).astype(BF16)
    w_vt = jnp.pad(w_uv, pad_half).reshape(A_KV_RANK, A_HEADS * LANES).T.astype(BF16)
    wkt = jnp.pad(jnp.transpose(w_uk, (1, 2, 0)), ((0, 0), (0, LANES - A_NOPE), (0, 0))).astype(BF16)
    eye = jnp.eye(A_HEADS, dtype=F32)
    wv_s = (jnp.transpose(w_uv, (1, 0, 2))[:, :, None, :] * eye[:, None, :, None]).reshape(
        A_HEADS, A_KV_RANK, A_WIDTH).astype(BF16)
    sel = (jnp.arange(LANES)[:, None] == A_NOPE + jnp.arange(A_ROPE)[None, :]).astype(BF16)
    rel_bias = rel_bias * LOG2E
    win_p, win_s = B_WIN_BLOCKS * LANES, B_SAMPLE_BLOCKS * LANES
    r0 = win_p - CHUNK
    x_len = r0 + win_p
    n_vec = x_len + CHUNK
    n_hi = r0 + CHUNK - 1 - B_MAX_REL
    n_lo = n_vec - n_hi - (2 * B_MAX_REL + 1)
    vec = jnp.concatenate([jnp.broadcast_to(rel_bias[:, -1:], (B_HEADS, n_hi)), rel_bias[:, ::-1],
                           jnp.broadcast_to(rel_bias[:, :1], (B_HEADS, n_lo))], axis=1)
    skew = jnp.tile(vec, (1, CHUNK))[:, :CHUNK * (n_vec - 1)].reshape(B_HEADS, CHUNK, n_vec - 1)
    toep = skew[:, :, CHUNK - 1:CHUNK - 1 + x_len]
    toep = jnp.transpose(toep.reshape(B_HEADS // 2, 2, CHUNK, x_len), (0, 3, 1, 2)).reshape(B_HEADS // 2, x_len, LANES)
    def window(win, first_rel, lo_chunk, hi_chunk):
        x0 = r0 - first_rel
        key_chunk = jnp.arange(win)[None, :, None] // CHUNK
        return jnp.where((key_chunk >= lo_chunk) & (key_chunk <= hi_chunk), toep[:, x0:x0 + win, :], NEG_INF)

    variants = []
    for v in range(B_VARIANTS):
        chunks = [B_GROUP_CHUNKS * v + cc for cc in range(B_GROUP_CHUNKS)]
        variants.append(jnp.concatenate([window(win_p, c * CHUNK, c - B_PAST_CHUNKS, c) for c in chunks], axis=-1))
    bias_p = jnp.stack(variants, axis=1)
    bias_s = window(win_s, win_s - CHUNK, 1, B_PAST_CHUNKS + 1)[:, None]
    return dict(pre=pre[None], w_in=w_in_p, q_norm=q_norm[None], kv_norm=kv_norm[None], w_uq=w_uq_p, w_k=w_k,
                w_vt=w_vt, wkt=wkt, wv_s=wv_s, sel=sel, bias_p=bias_p, bias_s=bias_s, w_out=w_out.astype(BF16),
                post=post[None])


def _prep_c(pre, post, w_in, sinks, w_out):
    q, k, v, g = jnp.split(w_in, [1024, 1152, 1280], axis=1)
    k0, k1 = k[:, :C_HEAD_DIM], k[:, C_HEAD_DIM:]
    v0, v1 = v[:, :C_HEAD_DIM], v[:, C_HEAD_DIM:]
    zero = jnp.zeros_like(v0)
    w_in_p = jnp.concatenate([q, k0, k0, k1, k1, v0, zero, v1, zero, g], axis=1).astype(BF16)
    win = C_WIN_BLOCKS * LANES
    per_chunk = C_GROUP * CHUNK
    key_chunk = jnp.arange(win)[:, None] // CHUNK

    def window(lo_chunk, hi_chunk):
        m = jnp.where((key_chunk >= lo_chunk) & (key_chunk <= hi_chunk), 0.0, NEG_INF).astype(F32)
        return jnp.broadcast_to(m, (win, per_chunk))

    variants = [jnp.concatenate([window(c - C_PAST_CHUNKS, c) for c in (2 * v, 2 * v + 1)], axis=-1)
                for v in range(C_WIN_BLOCKS)]
    mask_p = jnp.broadcast_to(jnp.stack(variants)[None], (C_KV_HEADS, C_WIN_BLOCKS, win, 2 * per_chunk))
    mask_s = jnp.broadcast_to(window(1, C_PAST_CHUNKS + 1)[None, None], (C_KV_HEADS, 1, win, per_chunk))
    sink_row = jnp.repeat((sinks * LOG2E).reshape(C_KV_HEADS, C_GROUP), CHUNK, axis=1)[:, None, :]
    return dict(pre=pre[None], w_in=w_in_p, mask_p=mask_p, mask_s=mask_s, sink_s=sink_row,
                sink_p=jnp.tile(sink_row, (1, 1, 2)), w_out=w_out.astype(BF16), post=post[None])


def _dup_heads(x):
    return jnp.concatenate([x[:, :, 0], x[:, :, 0], x[:, :, 1], x[:, :, 1]], axis=-1)


def kernel(x_prompt, x_sample, cache_a_ckv, cache_a_krope, cache_b_k, cache_b_v, cache_c_k, cache_c_v,
           ab_pre_norm, ab_post_norm, ab_w_in, ab_q_norm, ab_kv_norm, ab_w_uq, ab_w_ukv, ab_rel_bias, ab_w_out,
           c_pre_norm, c_post_norm, c_w_in, c_sinks, c_w_out):
    bsz, seq, _ = x_prompt.shape
    dbs, dseq, _ = x_sample.shape
    past = cache_a_ckv.shape[2]
    n_s = dbs * dseq
    pos_p = jnp.arange(seq, dtype=jnp.int32)
    pos_s = jnp.tile(past + jnp.arange(dseq, dtype=jnp.int32), dbs)
    wab = _prep_ab(ab_pre_norm[0], ab_post_norm[0], ab_w_in[0], ab_q_norm[0], ab_kv_norm[0], ab_w_uq[0],
                   ab_w_ukv[0], ab_rel_bias[0], ab_w_out[0])
    wc = _prep_c(c_pre_norm[0], c_post_norm[0], c_w_in[0], c_sinks[0], c_w_out[0])
    b_tail = min(B_PAST_CHUNKS * CHUNK, seq)
    c_tail = min(C_WINDOW, seq)
    tile = 512

    rope_a_p = _rope_tables(pos_p, A_ROPE, A_ROPE_PATTERN)
    (qa, ka, vt, qb, kb, vbt, sg, c_new_p, kr_new_p, kb_tail, vb_tail) = _ab_in_proj(
        x_prompt, rope_a_p, wab, tm=256, tail_len=b_tail, tkv=tile)
    o_a = _mla_prompt(qa, ka, vt, tq=tile)
    o_b = _band_attention(qb, kb, vbt, wab["bias_p"], None, n_groups=B_HEADS // 2, n_qblk=1, cpg=B_GROUP_CHUNKS,
                          ones_row=False)

    rope_a_s = _rope_tables(pos_s, A_ROPE, A_ROPE_PATTERN)
    xs = x_sample.reshape(1, n_s, D_MODEL)
    (qa_s, _, _, qb_s, _, _, sg_s, c_new_s, kr_new_s, kb_s32, vb_s32) = _ab_in_proj(
        xs, rope_a_s, wab, tm=n_s, tail_len=n_s, tkv=n_s)
    o_a_s = _mla_sample(qa_s.reshape(dbs, dseq, -1), cache_a_ckv[0], cache_a_krope[0],
                        c_new_s.reshape(dbs, dseq, -1), kr_new_s.reshape(dbs, dseq, -1),
                        wab["wkt"], wab["wv_s"], wab["sel"], tk=512)
    wb = cache_b_k.shape[2]
    pad_b = jnp.zeros((dbs, B_SAMPLE_BLOCKS * LANES - wb - dseq, B_WIDTH), F32)
    kband = jnp.concatenate([pad_b, cache_b_k[0].reshape(dbs, wb, B_WIDTH), kb_s32.reshape(dbs, dseq, -1)], 1)
    vband = jnp.concatenate([pad_b, cache_b_v[0].reshape(dbs, wb, B_WIDTH), vb_s32.reshape(dbs, dseq, -1)], 1)
    vband_t = jnp.transpose(vband.astype(BF16).reshape(dbs, B_SAMPLE_BLOCKS, LANES, B_HEADS // 2, LANES), (0, 3, 1, 4, 2))
    o_b_s = _band_attention(qb_s.reshape(dbs, dseq, -1), kband.astype(BF16), vband_t, wab["bias_s"], None,
                            n_groups=B_HEADS // 2, n_qblk=1, cpg=1, ones_row=False)

    rope_c_p = _rope_tables(pos_p, C_ROT, C_ROPE_PATTERN)
    h1_p, qc, kc, vct, sgc, kc_tail, vc_tail = _c_in_proj(
        [o_a, o_b], sg, wab["w_out"], wab["post"], x_prompt, rope_c_p, wc, tm=tile, tail_len=c_tail)
    o_c = _band_attention(qc, kc, vct, wc["mask_p"], wc["sink_p"], n_groups=C_KV_HEADS, n_qblk=C_GROUP // 2,
                          cpg=C_GROUP_CHUNKS, ones_row=True)
    h2_p = _out_proj([o_c], sgc, wc["w_out"], wc["post"], h1_p, tm=tile)

    rope_c_s = _rope_tables(pos_s, C_ROT, C_ROPE_PATTERN)
    h1_s, qc_s, kc_s, _, sgc_s, kc_s32, vc_s32 = _c_in_proj(
        [o_a_s.reshape(1, n_s, -1), o_b_s.reshape(1, n_s, -1)], sg_s, wab["w_out"], wab["post"], xs, rope_c_s, wc,
        tm=n_s, tail_len=n_s)
    wcw = cache_c_k.shape[2]
    win_c = C_SAMPLE_BLOCKS * LANES
    n_pad = win_c - wcw - dseq
    kcb = jnp.concatenate([jnp.zeros((dbs, n_pad, C_KV_HEADS * LANES), BF16), _dup_heads(cache_c_k[0]).astype(BF16),
                           kc_s.reshape(dbs, dseq, -1)], axis=1)
    vcb = jnp.concatenate([jnp.zeros((dbs, n_pad, C_KV_HEADS, C_HEAD_DIM), F32), cache_c_v[0],
                           vc_s32.reshape(dbs, dseq, C_KV_HEADS, C_HEAD_DIM)], axis=1)
    vcb_t = jnp.concatenate([jnp.transpose(vcb, (0, 2, 3, 1)), jnp.ones((dbs, C_KV_HEADS, 1, win_c), F32),
                             jnp.zeros((dbs, C_KV_HEADS, LANES - C_HEAD_DIM - 1, win_c), F32)], axis=2)
    vcb_t = jnp.transpose(vcb_t.astype(BF16).reshape(dbs, C_KV_HEADS, LANES, C_SAMPLE_BLOCKS, LANES), (0, 1, 3, 2, 4))
    o_c_s = _band_attention(qc_s.reshape(dbs, dseq, -1), kcb, vcb_t, wc["mask_s"], wc["sink_s"], n_groups=C_KV_HEADS,
                            n_qblk=C_GROUP // 2, cpg=1, ones_row=True)
    h2_s = _out_proj([o_c_s.reshape(1, n_s, -1)], sgc_s, wc["w_out"], wc["post"], h1_s, tm=n_s)

    def roll_in(buf, new):
        return jnp.concatenate([buf, new], axis=1)[:, -buf.shape[1]:][None]

    return (h2_p, h2_s.reshape(dbs, dseq, D_MODEL),
            c_new_p[None], kr_new_p[None],
            kb_tail.reshape(1, bsz, b_tail, B_HEADS, B_HEAD_DIM), vb_tail.reshape(1, bsz, b_tail, B_HEADS, B_HEAD_DIM),
            kc_tail.reshape(1, bsz, c_tail, C_KV_HEADS, C_HEAD_DIM), vc_tail.reshape(1, bsz, c_tail, C_KV_HEADS, C_HEAD_DIM),
            c_new_s.reshape(1, dbs, dseq, A_KV_RANK), kr_new_s.reshape(1, dbs, dseq, A_ROPE),
            roll_in(cache_b_k[0], kb_s32.reshape(dbs, dseq, B_HEADS, B_HEAD_DIM)),
            roll_in(cache_b_v[0], vb_s32.reshape(dbs, dseq, B_HEADS, B_HEAD_DIM)),
            roll_in(cache_c_k[0], kc_s32.reshape(dbs, dseq, C_KV_HEADS, C_HEAD_DIM)),
            roll_in(cache_c_v[0], vc_s32.reshape(dbs, dseq, C_KV_HEADS, C_HEAD_DIM)))
```

```python
import functools

import jax
import jax.numpy as jnp
import numpy as np
from jax import lax
from jax.experimental import pallas as pl
from jax.experimental.pallas import tpu as pltpu

F32 = jnp.float32
BF16 = jnp.bfloat16

D_MODEL = 1024
CHUNK = 64
ROPE_THETA = 500000.0
RMS_EPS = 1e-6
NEG_INF = -1e30

A_HEADS = 8
A_NOPE = 64
A_ROPE = 32
A_QK = A_NOPE + A_ROPE
A_V = 64
A_Q_RANK = 384
A_KV_RANK = 256
A_WIDTH = A_HEADS * A_V
LOG2E = 1.4426950408889634
A_SCALE = A_QK ** -0.5 * LOG2E

B_HEADS = 8
B_HEAD_DIM = 64
B_WIDTH = B_HEADS * B_HEAD_DIM
B_PAST_CHUNKS = 8
B_MAX_REL = 128
B_SCALE = B_HEAD_DIM ** -0.5 * LOG2E

C_HEADS = 16
C_KV_HEADS = 2
C_GROUP = C_HEADS // C_KV_HEADS
C_HEAD_DIM = 64
C_WIDTH = C_HEADS * C_HEAD_DIM
C_WINDOW = 128
C_PAST_CHUNKS = C_WINDOW // CHUNK
C_ROT = C_HEAD_DIM // 4
C_SCALE = C_HEAD_DIM ** -0.5 * LOG2E

LANES = 128
HALF = LANES // 2
BF16_SUBLANES = 16
VMEM_LIMIT = 56 * 1024 * 1024
A_MASK_CHUNKS = 8
assert A_QK + A_MASK_CHUNKS <= LANES
A_PIPE_UNROLL = 4
B_GROUP_CHUNKS = 4
C_GROUP_CHUNKS = 2
B_WIN_BLOCKS = (B_PAST_CHUNKS + B_GROUP_CHUNKS) * CHUNK // LANES
C_WIN_BLOCKS = (C_PAST_CHUNKS + C_GROUP_CHUNKS) * CHUNK // LANES
B_VARIANTS = B_PAST_CHUNKS // B_GROUP_CHUNKS + 1
C_VARIANTS = C_PAST_CHUNKS // C_GROUP_CHUNKS + 1
B_SAMPLE_BLOCKS =(B_PAST_CHUNKS + 2) * CHUNK // LANES
C_SAMPLE_BLOCKS = (C_PAST_CHUNKS + 2) * CHUNK // LANES

AB_Q0, AB_C0, AB_KR0, AB_G0, AB_QB0, AB_KB0, AB_VB0, AB_NZ = 0, 384, 640, 768, 1792, 2304, 2816, 3328
C_Q0, C_K0, C_V0, C_G0, C_NZ = 0, 1024, 1280, 1536, 2560


def _params(n_axes):
    return pltpu.CompilerParams(dimension_semantics=("arbitrary",) * n_axes, vmem_limit_bytes=VMEM_LIMIT)


def _rms(x, g):
    return x * lax.rsqrt(jnp.mean(x * x, axis=-1, keepdims=True) + RMS_EPS) * g


def _rope_block(blk, rope_ref, shift):
    return (blk * rope_ref[0] + pltpu.roll(blk, shift, 1) * rope_ref[1]
            + pltpu.roll(blk, LANES - shift, 1) * rope_ref[2])


def _tail_spec(n_tiles, tm, tail_len, width):
    tb = min(tail_len, tm)
    n_blk = tail_len // tb
    return tb, pl.BlockSpec((1, tb, width), lambda b, i: (b, jnp.maximum(i - (n_tiles - n_blk), 0), 0))


def _ab_in_body(x_ref, pre_ref, w_ref, qn_ref, kvn_ref, wuq_ref, wk_ref, wvt_ref, rope_ref,
                qa_ref, ka_ref, vt_ref, qb_ref, kb_ref, vbt_ref, sg_ref, c_ref, kr_ref, kb_tail_ref, vb_tail_ref):
    tm = x_ref.shape[1]
    tb = kb_tail_ref.shape[1]
    xn = _rms(x_ref[0], pre_ref[...]).astype(BF16)
    z = jnp.dot(xn, w_ref[...], preferred_element_type=F32)

    qn = _rms(z[:, AB_Q0:AB_C0], qn_ref[...]).astype(BF16)
    qa = jnp.dot(qn, wuq_ref[...], preferred_element_type=F32) * A_SCALE
    for h in range(A_HEADS):
        blk = slice(h * LANES, (h + 1) * LANES)
        qa_ref[0, :, blk] = _rope_block(qa[:, blk], rope_ref, A_ROPE // 2).astype(BF16)

    c_new = _rms(z[:, AB_C0:AB_KR0], kvn_ref[...])
    c_ref[0] = c_new
    cb = c_new.astype(BF16)
    krot = _rope_block(z[:, AB_KR0:AB_G0], rope_ref, A_ROPE // 2)
    kr_ref[0] = krot[:, A_NOPE:A_NOPE + A_ROPE]
    kn = jnp.dot(cb, wk_ref[...], preferred_element_type=F32)
    row = pl.program_id(1) * tm + lax.broadcasted_iota(jnp.int32, (tm, LANES), 0)
    lane = lax.broadcasted_iota(jnp.int32, (tm, LANES), 1)
    k_shared = krot + jnp.where(lane - A_QK == (row // CHUNK) % A_MASK_CHUNKS, 1.0, 0.0)
    for h in range(A_HEADS):
        blk = slice(h * LANES, (h + 1) * LANES)
        ka_ref[0, :, blk] = (kn[:, blk] + k_shared).astype(BF16)
    vt = jnp.dot(wvt_ref[...], c_new.T.astype(BF16), preferred_element_type=F32)
    row = lax.broadcasted_iota(jnp.int32, vt.shape, 0)
    vt_ref[0, 0] = jnp.where((row & (LANES - 1)) == A_V, 1.0, vt).astype(BF16)

    g = z[:, AB_G0:AB_QB0]
    sg_ref[0] = (g * jax.nn.sigmoid(g)).astype(BF16)
    qb_ref[0] = (z[:, AB_QB0:AB_KB0] * B_SCALE).astype(BF16)
    kb = z[:, AB_KB0:AB_VB0]
    vb = z[:, AB_VB0:AB_NZ]
    kb_ref[0] = kb.astype(BF16)
    vbt = vb.T.astype(BF16)
    for hp in range(B_HEADS // 2):
        for j in range(tm // LANES):
            vbt_ref[0, hp, j] = vbt[hp * LANES:(hp + 1) * LANES, j * LANES:(j + 1) * LANES]
    kb_tail_ref[0] = kb[tm - tb:, :]
    vb_tail_ref[0] = vb[tm - tb:, :]


def _ab_in_proj(x, rope, w, tm, tail_len, tkv):
    bsz, s, _ = x.shape
    nt = s // tm
    per_kv = tkv // tm
    tb, tail_spec = _tail_spec(nt, tm, tail_len, B_WIDTH)

    def full(a):
        return pl.BlockSpec(a.shape, lambda b, i: (0,) * a.ndim)

    def rows(width):
        return pl.BlockSpec((1, tm, width), lambda b, i: (b, i, 0))

    weights = (w["pre"], w["w_in"], w["q_norm"], w["kv_norm"], w["w_uq"], w["w_k"], w["w_vt"])
    vt_spec = pl.BlockSpec((1, 1, A_HEADS * LANES, tm), lambda b, i: (b, i // per_kv, 0, i % per_kv))
    out_shape = (
        jax.ShapeDtypeStruct((bsz, s, A_HEADS * LANES), BF16),
        jax.ShapeDtypeStruct((bsz, s, A_HEADS * LANES), BF16),
        jax.ShapeDtypeStruct((bsz, s // tkv, A_HEADS * LANES, tkv), BF16),
        jax.ShapeDtypeStruct((bsz, s, B_WIDTH), BF16),
        jax.ShapeDtypeStruct((bsz, s, B_WIDTH), BF16),
        jax.ShapeDtypeStruct((bsz, B_HEADS // 2, s // LANES, LANES, LANES), BF16),
        jax.ShapeDtypeStruct((bsz, s, A_WIDTH + B_WIDTH), BF16),
        jax.ShapeDtypeStruct((bsz, s, A_KV_RANK), F32),
        jax.ShapeDtypeStruct((bsz, s, A_ROPE), F32),
        jax.ShapeDtypeStruct((bsz, tail_len, B_WIDTH), F32),
        jax.ShapeDtypeStruct((bsz, tail_len, B_WIDTH), F32),
    )
    vbt_spec = pl.BlockSpec((1, B_HEADS // 2, tm // LANES, LANES, LANES), lambda b, i: (b, 0, i, 0, 0))
    out_specs = (rows(1024), rows(1024), vt_spec, rows(512), rows(512), vbt_spec, rows(1024),
                 rows(A_KV_RANK), rows(A_ROPE), tail_spec, tail_spec)
    return pl.pallas_call(
        _ab_in_body,
        grid=(bsz, nt),
        in_specs=[rows(D_MODEL)] + [full(a) for a in weights]
        + [pl.BlockSpec((3, tm, LANES), lambda b, i: (0, i, 0))],
        out_specs=out_specs,
        out_shape=out_shape,
        compiler_params=_params(2),
        name="ab_in_proj",
    )(x, *weights, rope)


def _mix_out(o_refs, sg_ref, w_ref, g_ref, h_ref):
    o = jnp.concatenate([r[0].astype(F32) for r in o_refs], axis=-1) if len(o_refs) > 1 else o_refs[0][0].astype(F32)
    mixed = (o * sg_ref[0].astype(F32)).astype(BF16)
    y = jnp.dot(mixed, w_ref[...], preferred_element_type=F32)
    return h_ref[0] + _rms(y, g_ref[...])


def _c_in_body(*refs):
    n_o = len(refs) - 14
    o_refs = refs[:n_o]
    (sg0_ref, w0_ref, post0_ref, x_ref, pre_ref, w_ref, rope_ref,
     h_ref, q_ref, k_ref, vt_ref, sg_ref, k_tail_ref, v_tail_ref) = refs[n_o:]
    tm = x_ref.shape[1]
    tb = k_tail_ref.shape[1]
    h = _mix_out(o_refs, sg0_ref, w0_ref, post0_ref, x_ref)
    h_ref[0] = h
    xn = _rms(h, pre_ref[...]).astype(BF16)
    z = jnp.dot(xn, w_ref[...], preferred_element_type=F32)
    half_rot = C_ROT // 2
    for j in range(C_WIDTH // LANES):
        blk = slice(C_Q0 + j * LANES, C_Q0 + (j + 1) * LANES)
        q_ref[0, :, j * LANES:(j + 1) * LANES] = (_rope_block(z[:, blk], rope_ref, half_rot) * C_SCALE).astype(BF16)
    kd = [_rope_block(z[:, C_K0 + j * LANES:C_K0 + (j + 1) * LANES], rope_ref, half_rot) for j in range(C_KV_HEADS)]
    vz = [z[:, C_V0 + j * LANES:C_V0 + (j + 1) * LANES] for j in range(C_KV_HEADS)]
    row = lax.broadcasted_iota(jnp.int32, (LANES, tm), 0)
    for j in range(C_KV_HEADS):
        k_ref[0, :, j * LANES:(j + 1) * LANES] = kd[j].astype(BF16)
        vt = jnp.where(row == C_HEAD_DIM, 1.0, vz[j].T).astype(BF16)
        for i in range(tm // LANES):
            vt_ref[0, j, i] = vt[:, i * LANES:(i + 1) * LANES]
    g = z[:, C_G0:C_NZ]
    sg_ref[0] = (g * jax.nn.sigmoid(g)).astype(BF16)
    lo = lax.broadcasted_iota(jnp.int32, (1, LANES), 1) < HALF
    k_tail_ref[0] = jnp.where(lo, kd[0], kd[1])[tm - tb:, :]
    v_tail_ref[0] = (vz[0] + pltpu.roll(vz[1], HALF, 1))[tm - tb:, :]


def _c_in_proj(o_parts, sg0, w_out0, post0, x, rope, w, tm, tail_len):
    bsz, s, _ = x.shape
    nt = s // tm
    tb, tail_spec = _tail_spec(nt, tm, tail_len, LANES)

    def full(a):
        return pl.BlockSpec(a.shape, lambda b, i: (0,) * a.ndim)

    def rows(width):
        return pl.BlockSpec((1, tm, width), lambda b, i: (b, i, 0))

    out_shape = (
        jax.ShapeDtypeStruct((bsz, s, D_MODEL), F32),
        jax.ShapeDtypeStruct((bsz, s, C_WIDTH), BF16),
        jax.ShapeDtypeStruct((bsz, s, C_KV_HEADS * LANES), BF16),
        jax.ShapeDtypeStruct((bsz, C_KV_HEADS, s // LANES, LANES, LANES), BF16),
        jax.ShapeDtypeStruct((bsz, s, C_WIDTH), BF16),
        jax.ShapeDtypeStruct((bsz, tail_len, LANES), F32),
        jax.ShapeDtypeStruct((bsz, tail_len, LANES), F32),
    )
    return pl.pallas_call(
        _c_in_body,
        grid=(bsz, nt),
        in_specs=[rows(o.shape[-1]) for o in o_parts]
        + [rows(D_MODEL), full(w_out0), full(post0), rows(D_MODEL), full(w["pre"]), full(w["w_in"]),
           pl.BlockSpec((3, tm, LANES), lambda b, i: (0, i, 0))],
        out_specs=(rows(D_MODEL), rows(C_WIDTH), rows(256),
                   pl.BlockSpec((1, C_KV_HEADS, tm // LANES, LANES, LANES), lambda b, i: (b, 0, i, 0, 0)),
                   rows(C_WIDTH), tail_spec, tail_spec),
        out_shape=out_shape,
        compiler_params=_params(2),
        name="c_in_proj",
    )(*o_parts, sg0, w_out0, post0, x, w["pre"], w["w_in"], rope)


def _out_body(*refs):
    o_refs, (sg_ref, w_ref, g_ref, h_ref, out_ref) = refs[:-5], refs[-5:]
    out_ref[0] = _mix_out(o_refs, sg_ref, w_ref, g_ref, h_ref)


def _out_proj(o_parts, sg, w_out, post_g, h, tm):
    bsz, s, _ = h.shape

    def rows(width):
        return pl.BlockSpec((1, tm, width), lambda b, i: (b, i, 0))

    def full(a):
        return pl.BlockSpec(a.shape, lambda b, i: (0,) * a.ndim)

    return pl.pallas_call(
        _out_body,
        grid=(bsz, s // tm),
        in_specs=[rows(o.shape[-1]) for o in o_parts] + [rows(D_MODEL), full(w_out), full(post_g), rows(D_MODEL)],
        out_specs=rows(D_MODEL),
        out_shape=jax.ShapeDtypeStruct(h.shape, F32),
        compiler_params=_params(2),
        name="out_proj",
    )(*o_parts, sg, w_out, post_g, h)


def _mla_prompt_body(q_ref, k_ref, vt_ref, qmask_ref, o_ref, m_sc, acc_sc, s_sc, *, tq):
    n_tiles = q_ref.shape[1] // tq
    n_pairs = n_tiles * (n_tiles + 1) // 2
    assert A_PIPE_UNROLL % 2 == 0 and n_pairs % A_PIPE_UNROLL == 0 and n_pairs >= 2 * A_PIPE_UNROLL
    m_sc[...] = jnp.full(m_sc.shape, NEG_INF, F32)
    acc_sc[...] = jnp.zeros(acc_sc.shape, F32)
    contract_last = (((1,), (1,)), ((), ()))

    def scores(qi, j, slot):
        q_start = pl.multiple_of(qi * tq, tq)
        k_start = pl.multiple_of(j * tq, tq)
        qmask = jnp.where(j == qi, qmask_ref[...], jnp.zeros_like(qmask_ref[...]))
        for hh in range(2):
            blk = slice(hh * LANES, (hh + 1) * LANES)
            q = q_ref[0, pl.ds(q_start, tq), blk] + qmask
            s_sc[slot, hh] = lax.dot_general(k_ref[0, pl.ds(k_start, tq), blk], q, contract_last,
                                             preferred_element_type=F32)

    def consume(qi, j, slot):
        for hh in range(2):
            s = s_sc[slot, hh]
            vt = vt_ref[0, j, hh * LANES:(hh + 1) * LANES, :]
            m_old = m_sc[qi, hh]
            m_new = jnp.maximum(m_old, jnp.max(s, axis=0, keepdims=True))
            p = jnp.exp2(s - m_new).astype(BF16)
            acc_sc[qi, hh] = jnp.exp2(m_old - m_new) * acc_sc[qi, hh] + jnp.dot(vt, p, preferred_element_type=F32)
            m_sc[qi, hh] = m_new

    def advance(qi, j):
        last = j == qi
        return jnp.where(last, qi + 1, qi), jnp.where(last, 0, j + 1)

    def stages(cur, count):
        for st in range(count):
            nxt = advance(*cur)
            scores(*nxt, 1 - (st & 1))
            consume(*cur, st & 1)
            cur = nxt
        return cur

    zero = jnp.int32(0)
    scores(zero, zero, 0)
    cur = lax.fori_loop(0, n_pairs // A_PIPE_UNROLL - 1, lambda i, c: stages(c, A_PIPE_UNROLL), (zero, zero))
    cur = stages(cur, A_PIPE_UNROLL - 1)
    consume(*cur, (A_PIPE_UNROLL - 1) & 1)

    lo = lax.broadcasted_iota(jnp.int32, (1, LANES), 1) < HALF

    def write_tile(qi, carry):
        outs = []
        for hh in range(2):
            acc = acc_sc[qi, hh]
            outs.append((acc / acc[A_V:A_V + 1, :]).T)
        o_ref[0, pl.ds(pl.multiple_of(qi * tq, tq), tq), :] = jnp.where(
            lo, outs[0], pltpu.roll(outs[1], HALF, 1)).astype(o_ref.dtype)
        return carry

    lax.fori_loop(0, n_tiles, write_tile, 0)


def _mla_prompt(qa, ka, vt, tq):
    bsz, s, _ = qa.shape
    pairs = A_HEADS // 2
    n_tiles = s // tq
    assert vt.shape[3] == tq and tq == A_MASK_CHUNKS * CHUNK
    q_chunk = np.arange(tq)[:, None] // CHUNK
    lane_chunk = np.arange(LANES)[None, :] - A_QK
    qmask = np.where((lane_chunk > q_chunk) & (lane_chunk < A_MASK_CHUNKS), NEG_INF, 0.0).astype(np.float32)
    return pl.pallas_call(
        functools.partial(_mla_prompt_body, tq=tq),
        grid=(bsz, pairs),
        in_specs=[pl.BlockSpec((1, s, 2 * LANES), lambda b, h: (b, 0, h)),
                  pl.BlockSpec((1, s, 2 * LANES), lambda b, h: (b, 0, h)),
                  pl.BlockSpec((1, n_tiles, 2 * LANES, tq), lambda b, h: (b, 0, h, 0)),
                  pl.BlockSpec((tq, LANES), lambda b, h: (0, 0))],
        out_specs=pl.BlockSpec((1, s, LANES), lambda b, h: (b, 0, h)),
        out_shape=jax.ShapeDtypeStruct((bsz, s, A_WIDTH), BF16),
        scratch_shapes=[pltpu.VMEM((n_tiles, 2, 1, tq), F32), pltpu.VMEM((n_tiles, 2, LANES, tq), F32),
                        pltpu.VMEM((2, 2, tq, tq), F32)],
        compiler_params=_params(2),
        name="mla_prompt",
    )(qa, ka, vt, jnp.asarray(qmask, BF16))


def _mla_sample_body(q_ref, cc_ref, ckr_ref, cn_ref, krn_ref, wkt_ref, wv_ref, sel_ref, o_ref,
                     qabs_sc, qr_sc, m_sc, acc_sc, *, tk):
    t = q_ref.shape[1]
    past = cc_ref.shape[1]
    for h in range(A_HEADS):
        qh = q_ref[0, :, h * LANES:(h + 1) * LANES]
        rows = slice(h * t, (h + 1) * t)
        qabs_sc[rows, :] = jnp.dot(qh, wkt_ref[h], preferred_element_type=F32).astype(BF16)
        qr_sc[rows, :] = jnp.dot(qh, sel_ref[...], preferred_element_type=F32).astype(BF16)
    m_sc[...] = jnp.full(m_sc.shape, NEG_INF, F32)
    acc_sc[...] = jnp.zeros(acc_sc.shape, F32)
    contract_last = (((1,), (1,)), ((), ()))

    def scores(c_t, kr_t):
        return (lax.dot_general(c_t.astype(BF16), qabs_sc[...], contract_last, preferred_element_type=F32)
                + lax.dot_general(kr_t.astype(BF16), qr_sc[...], contract_last, preferred_element_type=F32))

    def update(s, c_t):
        n_keys = c_t.shape[0]
        m_old = m_sc[...]
        m_new = jnp.maximum(m_old, jnp.max(s, axis=0, keepdims=True))
        p = jnp.exp2(s - m_new).astype(BF16)
        ct = jnp.concatenate([c_t.T.astype(BF16), jnp.ones((BF16_SUBLANES, n_keys), BF16)], axis=0)
        acc_sc[...] = jnp.exp2(m_old - m_new) * acc_sc[...] + jnp.dot(ct, p, preferred_element_type=F32)
        m_sc[...] = m_new

    def tile(j):
        rows = pl.ds(j * tk, tk)
        return cc_ref[0, rows, :], ckr_ref[0, rows, :]

    n_tiles = past // tk
    cur = tile(0)
    s = scores(*cur)
    for j in range(n_tiles):
        nxt = tile(j + 1) if j + 1 < n_tiles else (cn_ref[0], krn_ref[0])
        s_next = scores(*nxt)
        update(s, cur[0])
        cur, s = nxt, s_next
    update(s, cur[0])

    acc = acc_sc[...]
    o_lat = (acc[:A_KV_RANK] / acc[A_KV_RANK:A_KV_RANK + 1]).T.astype(BF16)
    out = jnp.zeros((t, A_WIDTH), F32)
    for h in range(A_HEADS):
        out = out + jnp.dot(o_lat[h * t:(h + 1) * t, :], wv_ref[h], preferred_element_type=F32)
    o_ref[0] = out.astype(o_ref.dtype)


def _mla_sample(qa, cache_c, cache_kr, c_new, kr_new, wkt, wv, sel, tk):
    bsz, t, _ = qa.shape
    past = cache_c.shape[1]

    def per_b(shape):
        return pl.BlockSpec((1,) + shape, lambda b: (b, 0, 0))

    def full(a):
        return pl.BlockSpec(a.shape, lambda b: (0,) * a.ndim)

    rows = A_HEADS * t
    return pl.pallas_call(
        functools.partial(_mla_sample_body, tk=tk),
        grid=(bsz,),
        in_specs=[per_b((t, A_HEADS * LANES)), per_b((past, A_KV_RANK)), per_b((past, A_ROPE)),
                  per_b((t, A_KV_RANK)), per_b((t, A_ROPE)), full(wkt), full(wv), full(sel)],
        out_specs=per_b((t, A_WIDTH)),
        out_shape=jax.ShapeDtypeStruct((bsz, t, A_WIDTH), BF16),
        scratch_shapes=[pltpu.VMEM((rows, A_KV_RANK), BF16), pltpu.VMEM((rows, A_ROPE), BF16),
                        pltpu.VMEM((1, rows), F32), pltpu.VMEM((A_KV_RANK + BF16_SUBLANES, rows), F32)],
        compiler_params=_params(1),
        name="mla_sample",
    )(qa, cache_c, cache_kr, c_new, kr_new, wkt, wv, sel)


def _band_body(*refs, n_blk, n_qblk, cpg, ones_row, has_sink, variants):
    refs = list(refs)
    q_ref, k_ref, vt_ref = refs[:3]
    o_ref, s_sc, bias_sc = refs[-3:]
    extra = refs[3:-3]
    sink_ref = extra.pop() if has_sink else None
    toep_ref = extra.pop() if extra else None
    win = n_blk * LANES
    rows_g = cpg * CHUNK
    n_var = len(variants)
    qpc = 2 * n_qblk * CHUNK
    lo = lax.broadcasted_iota(jnp.int32, (CHUNK, LANES), 1) < HALF
    contract_last = (((1,), (1,)), ((), ()))

    key_chunk = lax.broadcasted_iota(jnp.int32, (win, LANES), 0) // CHUNK
    for v, chunks in enumerate(variants):
        for cc, (x0, lo_chunk, hi_chunk) in enumerate(chunks):
            tile = toep_ref[0, x0:x0 + win, :] if toep_ref is not None else jnp.zeros((win, LANES), F32)
            tile = jnp.where((key_chunk >= lo_chunk) & (key_chunk <= hi_chunk), tile, NEG_INF)
            for j in range(qpc // LANES):
                bias_sc[v, :, cc * qpc + j * LANES:cc * qpc + (j + 1) * LANES] = tile

    def aligned(x, n):
        return x if isinstance(x, int) else pl.multiple_of(x, n)

    def scores(a, wb, variant, slot):
        pieces = []
        for cc in range(cpg):
            rows = pl.ds(aligned(a * rows_g + cc * CHUNK, CHUNK), CHUNK)
            for r in range(n_qblk):
                qblk = q_ref[0, rows, r * LANES:(r + 1) * LANES]
                zero = jnp.zeros_like(qblk)
                pieces += [jnp.where(lo, qblk, zero), jnp.where(lo, zero, qblk)]
        qs = jnp.concatenate(pieces, axis=0)
        kw = k_ref[0, pl.ds(aligned(wb * LANES, LANES), win), :]
        s_sc[slot] = lax.dot_general(kw, qs, contract_last, preferred_element_type=F32) + bias_sc[variant]

    def finish(a, wb, slot):
        s = s_sc[slot]
        m = jnp.max(s, axis=0, keepdims=True)
        if has_sink:
            sink = sink_ref[0]
            m = jnp.maximum(m, sink)
        p = jnp.exp2(s - m)
        vt = jnp.concatenate([vt_ref[0, 0, wb + i] for i in range(n_blk)], axis=1)
        if not ones_row:
            vt = jnp.concatenate([vt, jnp.ones((BF16_SUBLANES, win), BF16)], axis=0)
        o = jnp.dot(vt, p.astype(BF16), preferred_element_type=F32)
        l = o[CHUNK:CHUNK + 1, :] if ones_row else o[LANES:LANES + 1, :]
        o = o[:LANES]
        if has_sink:
            l = l + jnp.exp2(sink - m)
        o = (o / l).T
        idx = 0
        for cc in range(cpg):
            rows = pl.ds(aligned(a * rows_g + cc * CHUNK, CHUNK), CHUNK)
            for r in range(n_qblk):
                top = o[idx * CHUNK:(idx + 1) * CHUNK]
                bot = o[(idx + 1) * CHUNK:(idx + 2) * CHUNK]
                if ones_row:
                    bot = pltpu.roll(bot, HALF, 1)
                o_ref[0, rows, r * LANES:(r + 1) * LANES] = jnp.where(lo, top, bot).astype(o_ref.dtype)
                idx += 2

    n_total = q_ref.shape[1] // rows_g
    if n_total == 1:
        scores(0, 0, 0, 0)
        finish(0, 0, 0)
        return
    g_blk = rows_g // LANES
    n_lead = n_var - 1 + (n_var - 1) % 2
    assert (n_var - 1) * g_blk >= n_blk - g_blk and (n_total - n_lead) % 2 == 0 and n_total - n_lead >= 2

    def window_block(a):
        wb = (a + 1) * g_blk - n_blk
        return max(wb, 0) if isinstance(a, int) else wb

    def stage(a, slot):
        nxt = a + 1
        scores(nxt, window_block(nxt), min(nxt, n_var - 1) if isinstance(nxt, int) else n_var - 1, 1 - slot)
        finish(a, window_block(a), slot)

    scores(0, window_block(0), 0, 0)
    for a in range(n_lead):
        stage(a, a & 1)

    def pair_body(i, carry):
        a = n_lead + 2 * i
        stage(a, 0)
        stage(a + 1, 1)
        return carry

    lax.fori_loop(0, (n_total - n_lead) // 2 - 1, pair_body, 0)
    stage(n_total - 2, 0)
    finish(n_total - 1, window_block(n_total - 1), 1)


def _band_attention(q, k, vt, toep, sink, *, variants, n_blk, n_groups, n_qblk, cpg, ones_row):
    bsz, s, _ = q.shape
    sk = k.shape[1]
    qw = n_qblk * LANES
    in_specs = [pl.BlockSpec((1, s, qw), lambda b, g: (b, 0, g)),
                pl.BlockSpec((1, sk, LANES), lambda b, g: (b, 0, g)),
                pl.BlockSpec((1, 1) + vt.shape[2:], lambda b, g: (b, g, 0, 0, 0))]
    args = [q, k, vt]
    for extra in (toep, sink):
        if extra is not None:
            in_specs.append(pl.BlockSpec((1,) + extra.shape[1:], lambda b, g: (g, 0, 0)))
            args.append(extra)
    bias_shape = (n_blk * LANES, cpg * 2 * n_qblk * CHUNK)
    body = functools.partial(_band_body, n_blk=n_blk, n_qblk=n_qblk, cpg=cpg, ones_row=ones_row,
                             has_sink=sink is not None, variants=variants)
    return pl.pallas_call(
        body,
        grid=(bsz, n_groups),
        in_specs=in_specs,
        out_specs=pl.BlockSpec((1, s, qw), lambda b, g: (b, 0, g)),
        out_shape=jax.ShapeDtypeStruct(q.shape, BF16),
        scratch_shapes=[pltpu.VMEM((2,) + bias_shape, F32), pltpu.VMEM((len(variants),) + bias_shape, F32)],
        compiler_params=_params(2),
        name="band_attention",
    )(*args)


def _rope_tables(pos, rot, lane_pattern):
    half = rot // 2
    inv = jnp.power(ROPE_THETA, -jnp.arange(half, dtype=F32) * 2.0 / rot)
    inv_lane, first, second = [], [], []
    for kind, width in lane_pattern:
        if kind == "rot":
            inv_lane += [inv, inv]
            first += [1.0] * half + [0.0] * half
            second += [0.0] * half + [1.0] * half
        else:
            inv_lane.append(jnp.zeros((width,), F32))
            first += [0.0] * width
            second += [0.0] * width
    ang = pos.astype(F32)[:, None] * jnp.concatenate(inv_lane)[None, :]
    cos, sin = jnp.cos(ang), jnp.sin(ang)
    return jnp.stack([cos, sin * np.asarray(second, np.float32), -sin * np.asarray(first, np.float32)])


A_ROPE_PATTERN = (("pad", A_NOPE), ("rot", A_ROPE), ("pad", LANES - A_QK))
C_ROPE_PATTERN = (("rot", C_ROT), ("pad", HALF - C_ROT)) * 2


def _prep_ab(pre, post, w_in, q_norm, kv_norm, w_uq, w_ukv, rel_bias, w_out):
    d = w_in.shape[0]
    q_lat, c_kv, k_r, g_a, q_b, k_b, v_b, g_b = jnp.split(
        w_in, [384, 640, 672, 1184, 1696, 2208, 2720], axis=1)
    kr_blk = jnp.concatenate([jnp.zeros((d, A_NOPE), F32), k_r, jnp.zeros((d, LANES - A_QK), F32)], axis=1)
    w_in_p = jnp.concatenate([q_lat, c_kv, kr_blk, g_a, g_b, q_b, k_b, v_b], axis=1).astype(BF16)
    w_uq_p = jnp.pad(w_uq.reshape(A_Q_RANK, A_HEADS, A_QK), ((0, 0), (0, 0), (0, LANES - A_QK)))
    w_uq_p = w_uq_p.reshape(A_Q_RANK, A_HEADS * LANES).astype(BF16)
    ukv = w_ukv.reshape(A_KV_RANK, A_HEADS, A_NOPE + A_V)
    w_uk, w_uv = ukv[..., :A_NOPE], ukv[..., A_NOPE:]
    pad_half = ((0, 0), (0, 0), (0, LANES - A_NOPE))
    w_k = jnp.pad(w_uk, pad_half).reshape(A_KV_RANK, A_HEADS * LANES).astype(BF16)
    w_vt = jnp.pad(w_uv, pad_half).reshape(A_KV_RANK, A_HEADS * LANES).T.astype(BF16)
    wkt = jnp.pad(jnp.transpose(w_uk, (1, 2, 0)), ((0, 0), (0, LANES - A_NOPE), (0, 0))).astype(BF16)
    eye = jnp.eye(A_HEADS, dtype=F32)
    wv_s = (jnp.transpose(w_uv, (1, 0, 2))[:, :, None, :] * eye[:, None, :, None]).reshape(
        A_HEADS, A_KV_RANK, A_WIDTH).astype(BF16)
    sel = (jnp.arange(LANES)[:, None] == A_NOPE + jnp.arange(A_ROPE)[None, :]).astype(BF16)
    rel_bias = rel_bias * LOG2E
    win_p, win_s = B_WIN_BLOCKS * LANES, B_SAMPLE_BLOCKS * LANES
    r0 = win_p - CHUNK
    x_len = r0 + win_p
    n_vec = x_len + CHUNK
    n_hi = r0 + CHUNK - 1 - B_MAX_REL
    n_lo = n_vec - n_hi - (2 * B_MAX_REL + 1)
    vec = jnp.concatenate([jnp.broadcast_to(rel_bias[:, -1:], (B_HEADS, n_hi)), rel_bias[:, ::-1],
                           jnp.broadcast_to(rel_bias[:, :1], (B_HEADS, n_lo))], axis=1)
    skew = jnp.tile(vec, (1, CHUNK))[:, :CHUNK * (n_vec - 1)].reshape(B_HEADS, CHUNK, n_vec - 1)
    toep = skew[:, :, CHUNK - 1:CHUNK - 1 + x_len]
    toep = jnp.transpose(toep.reshape(B_HEADS // 2, 2, CHUNK, x_len), (0, 3, 1, 2)).reshape(B_HEADS // 2, x_len, LANES)
    var_p = tuple(tuple((r0 - c * CHUNK, c - B_PAST_CHUNKS, c)
                        for c in range(B_GROUP_CHUNKS * v, B_GROUP_CHUNKS * (v + 1))) for v in range(B_VARIANTS))
    var_s = (((r0 - (win_s - CHUNK), 1, B_PAST_CHUNKS + 1),),)
    return dict(pre=pre[None], w_in=w_in_p, q_norm=q_norm[None], kv_norm=kv_norm[None], w_uq=w_uq_p, w_k=w_k,
                w_vt=w_vt, wkt=wkt, wv_s=wv_s, sel=sel, toep=toep, var_p=var_p, var_s=var_s,
                w_out=w_out.astype(BF16), post=post[None])


def _prep_c(pre, post, w_in, sinks, w_out):
    q, k, v, g = jnp.split(w_in, [1024, 1152, 1280], axis=1)
    k0, k1 = k[:, :C_HEAD_DIM], k[:, C_HEAD_DIM:]
    v0, v1 = v[:, :C_HEAD_DIM], v[:, C_HEAD_DIM:]
    zero = jnp.zeros_like(v0)
    w_in_p = jnp.concatenate([q, k0, k0, k1, k1, v0, zero, v1, zero, g], axis=1).astype(BF16)
    var_p = tuple(tuple((0, c - C_PAST_CHUNKS, c) for c in range(C_GROUP_CHUNKS * v, C_GROUP_CHUNKS * (v + 1)))
                  for v in range(C_VARIANTS))
    var_s = (((0, 1, C_PAST_CHUNKS + 1),),)
    sink_row = jnp.repeat((sinks * LOG2E).reshape(C_KV_HEADS, C_GROUP), CHUNK, axis=1)[:, None, :]
    return dict(pre=pre[None], w_in=w_in_p, var_p=var_p, var_s=var_s, sink_s=sink_row,
                sink_p=jnp.tile(sink_row, (1, 1, C_GROUP_CHUNKS)), w_out=w_out.astype(BF16), post=post[None])


def _dup_heads(x):
    return jnp.concatenate([x[:, :, 0], x[:, :, 0], x[:, :, 1], x[:, :, 1]], axis=-1)


def kernel(x_prompt, x_sample, cache_a_ckv, cache_a_krope, cache_b_k, cache_b_v, cache_c_k, cache_c_v,
           ab_pre_norm, ab_post_norm, ab_w_in, ab_q_norm, ab_kv_norm, ab_w_uq, ab_w_ukv, ab_rel_bias, ab_w_out,
           c_pre_norm, c_post_norm, c_w_in, c_sinks, c_w_out):
    bsz, seq, _ = x_prompt.shape
    dbs, dseq, _ = x_sample.shape
    past = cache_a_ckv.shape[2]
    n_s = dbs * dseq
    pos_p = jnp.arange(seq, dtype=jnp.int32)
    pos_s = jnp.tile(past + jnp.arange(dseq, dtype=jnp.int32), dbs)
    wab = _prep_ab(ab_pre_norm[0], ab_post_norm[0], ab_w_in[0], ab_q_norm[0], ab_kv_norm[0], ab_w_uq[0],
                   ab_w_ukv[0], ab_rel_bias[0], ab_w_out[0])
    wc = _prep_c(c_pre_norm[0], c_post_norm[0], c_w_in[0], c_sinks[0], c_w_out[0])
    b_tail = min(B_PAST_CHUNKS * CHUNK, seq)
    c_tail = min(C_WINDOW, seq)
    tile = 512

    rope_a_p = _rope_tables(pos_p, A_ROPE, A_ROPE_PATTERN)
    (qa, ka, vt, qb, kb, vbt, sg, c_new_p, kr_new_p, kb_tail, vb_tail) = _ab_in_proj(
        x_prompt, rope_a_p, wab, tm=256, tail_len=b_tail, tkv=tile)
    o_a = _mla_prompt(qa, ka, vt, tq=tile)
    o_b = _band_attention(qb, kb, vbt, wab["toep"], None, variants=wab["var_p"], n_blk=B_WIN_BLOCKS,
                          n_groups=B_HEADS // 2, n_qblk=1, cpg=B_GROUP_CHUNKS, ones_row=False)

    rope_a_s = _rope_tables(pos_s, A_ROPE, A_ROPE_PATTERN)
    xs = x_sample.reshape(1, n_s, D_MODEL)
    (qa_s, _, _, qb_s, _, _, sg_s, c_new_s, kr_new_s, kb_s32, vb_s32) = _ab_in_proj(
        xs, rope_a_s, wab, tm=n_s, tail_len=n_s, tkv=n_s)
    o_a_s = _mla_sample(qa_s.reshape(dbs, dseq, -1), cache_a_ckv[0], cache_a_krope[0],
                        c_new_s.reshape(dbs, dseq, -1), kr_new_s.reshape(dbs, dseq, -1),
                        wab["wkt"], wab["wv_s"], wab["sel"], tk=512)
    wb = cache_b_k.shape[2]
    pad_b = jnp.zeros((dbs, B_SAMPLE_BLOCKS * LANES - wb - dseq, B_WIDTH), F32)
    kband = jnp.concatenate([pad_b, cache_b_k[0].reshape(dbs, wb, B_WIDTH), kb_s32.reshape(dbs, dseq, -1)], 1)
    vband = jnp.concatenate([pad_b, cache_b_v[0].reshape(dbs, wb, B_WIDTH), vb_s32.reshape(dbs, dseq, -1)], 1)
    vband_t = jnp.transpose(vband.astype(BF16).reshape(dbs, B_SAMPLE_BLOCKS, LANES, B_HEADS // 2, LANES), (0, 3, 1, 4, 2))
    o_b_s = _band_attention(qb_s.reshape(dbs, dseq, -1), kband.astype(BF16), vband_t, wab["toep"], None,
                            variants=wab["var_s"], n_blk=B_SAMPLE_BLOCKS, n_groups=B_HEADS // 2, n_qblk=1, cpg=1,
                            ones_row=False)

    rope_c_p = _rope_tables(pos_p, C_ROT, C_ROPE_PATTERN)
    h1_p, qc, kc, vct, sgc, kc_tail, vc_tail = _c_in_proj(
        [o_a, o_b], sg, wab["w_out"], wab["post"], x_prompt, rope_c_p, wc, tm=tile, tail_len=c_tail)
    o_c = _band_attention(qc, kc, vct, None, wc["sink_p"], variants=wc["var_p"], n_blk=C_WIN_BLOCKS,
                          n_groups=C_KV_HEADS, n_qblk=C_GROUP // 2, cpg=C_GROUP_CHUNKS, ones_row=True)
    h2_p = _out_proj([o_c], sgc, wc["w_out"], wc["post"], h1_p, tm=tile)

    rope_c_s = _rope_tables(pos_s, C_ROT, C_ROPE_PATTERN)
    h1_s, qc_s, kc_s, _, sgc_s, kc_s32, vc_s32 = _c_in_proj(
        [o_a_s.reshape(1, n_s, -1), o_b_s.reshape(1, n_s, -1)], sg_s, wab["w_out"], wab["post"], xs, rope_c_s, wc,
        tm=n_s, tail_len=n_s)
    wcw = cache_c_k.shape[2]
    win_c = C_SAMPLE_BLOCKS * LANES
    n_pad = win_c - wcw - dseq
    kcb = jnp.concatenate([jnp.zeros((dbs, n_pad, C_KV_HEADS * LANES), BF16), _dup_heads(cache_c_k[0]).astype(BF16),
                           kc_s.reshape(dbs, dseq, -1)], axis=1)
    vcb = jnp.concatenate([jnp.zeros((dbs, n_pad, C_KV_HEADS, C_HEAD_DIM), F32), cache_c_v[0],
                           vc_s32.reshape(dbs, dseq, C_KV_HEADS, C_HEAD_DIM)], axis=1)
    vcb_t = jnp.concatenate([jnp.transpose(vcb, (0, 2, 3, 1)), jnp.ones((dbs, C_KV_HEADS, 1, win_c), F32),
                             jnp.zeros((dbs, C_KV_HEADS, LANES - C_HEAD_DIM - 1, win_c), F32)], axis=2)
    vcb_t = jnp.transpose(vcb_t.astype(BF16).reshape(dbs, C_KV_HEADS, LANES, C_SAMPLE_BLOCKS, LANES), (0, 1, 3, 2, 4))
    o_c_s = _band_attention(qc_s.reshape(dbs, dseq, -1), kcb, vcb_t, None, wc["sink_s"], variants=wc["var_s"],
                            n_blk=C_SAMPLE_BLOCKS, n_groups=C_KV_HEADS, n_qblk=C_GROUP // 2, cpg=1, ones_row=True)
    h2_s = _out_proj([o_c_s.reshape(1, n_s, -1)], sgc_s, wc["w_out"], wc["post"], h1_s, tm=n_s)

    def roll_in(buf, new):
        return jnp.concatenate([buf, new], axis=1)[:, -buf.shape[1]:][None]

    return (h2_p, h2_s.reshape(dbs, dseq, D_MODEL),
            c_new_p[None], kr_new_p[None],
            kb_tail.reshape(1, bsz, b_tail, B_HEADS, B_HEAD_DIM), vb_tail.reshape(1, bsz, b_tail, B_HEADS, B_HEAD_DIM),
            kc_tail.reshape(1, bsz, c_tail, C_KV_HEADS, C_HEAD_DIM), vc_tail.reshape(1, bsz, c_tail, C_KV_HEADS, C_HEAD_DIM),
            c_new_s.reshape(1, dbs, dseq, A_KV_RANK), kr_new_s.reshape(1, dbs, dseq, A_ROPE),
            roll_in(cache_b_k[0], kb_s32.reshape(dbs, dseq, B_HEADS, B_HEAD_DIM)),
            roll_in(cache_b_v[0], vb_s32.reshape(dbs, dseq, B_HEADS, B_HEAD_DIM)),
            roll_in(cache_c_k[0], kc_s32.reshape(dbs, dseq, C_KV_HEADS, C_HEAD_DIM)),
            roll_in(cache_c_v[0], vc_s32.reshape(dbs, dseq, C_KV_HEADS, C_HEAD_DIM)))
```

```python
import functools

import jax
import jax.numpy as jnp
import numpy as np
from jax import lax
from jax.experimental import pallas as pl
from jax.experimental.pallas import tpu as pltpu

F32 = jnp.float32
BF16 = jnp.bfloat16

D_MODEL = 1024
CHUNK = 64
ROPE_THETA = 500000.0
RMS_EPS = 1e-6
NEG_INF = -1e30

A_HEADS = 8
A_NOPE = 64
A_ROPE = 32
A_QK = A_NOPE + A_ROPE
A_V = 64
A_Q_RANK = 384
A_KV_RANK = 256
A_WIDTH = A_HEADS * A_V
LOG2E = 1.4426950408889634
A_SCALE = A_QK ** -0.5 * LOG2E

B_HEADS = 8
B_HEAD_DIM = 64
B_WIDTH = B_HEADS * B_HEAD_DIM
B_PAST_CHUNKS = 8
B_MAX_REL = 128
B_SCALE = B_HEAD_DIM ** -0.5 * LOG2E

C_HEADS = 16
C_KV_HEADS = 2
C_GROUP = C_HEADS // C_KV_HEADS
C_HEAD_DIM = 64
C_WIDTH = C_HEADS * C_HEAD_DIM
C_WINDOW = 128
C_PAST_CHUNKS = C_WINDOW // CHUNK
C_ROT = C_HEAD_DIM // 4
C_SCALE = C_HEAD_DIM ** -0.5 * LOG2E

LANES = 128
HALF = LANES // 2
BF16_SUBLANES = 16
VMEM_LIMIT = 56 * 1024 * 1024
A_MASK_CHUNKS = 8
assert A_QK + A_MASK_CHUNKS <= LANES
A_PIPE_UNROLL = 4
BAND_PIPE_UNROLL = 4
B_GROUP_CHUNKS = 4
C_GROUP_CHUNKS = 2
B_WIN_BLOCKS = (B_PAST_CHUNKS + B_GROUP_CHUNKS) * CHUNK // LANES
C_WIN_BLOCKS = (C_PAST_CHUNKS + C_GROUP_CHUNKS) * CHUNK // LANES
B_VARIANTS = B_PAST_CHUNKS // B_GROUP_CHUNKS + 1
C_VARIANTS = C_PAST_CHUNKS // C_GROUP_CHUNKS + 1
B_SAMPLE_BLOCKS =(B_PAST_CHUNKS + 2) * CHUNK // LANES
C_SAMPLE_BLOCKS = (C_PAST_CHUNKS + 2) * CHUNK // LANES

AB_Q0, AB_C0, AB_KR0, AB_G0, AB_QB0, AB_KB0, AB_VB0, AB_NZ = 0, 384, 640, 768, 1792, 2304, 2816, 3328
C_Q0, C_K0, C_V0, C_G0, C_NZ = 0, 1024, 1280, 1536, 2560


def _params(n_axes):
    return pltpu.CompilerParams(dimension_semantics=("arbitrary",) * n_axes, vmem_limit_bytes=VMEM_LIMIT)


def _rms(x, g):
    return x * lax.rsqrt(jnp.mean(x * x, axis=-1, keepdims=True) + RMS_EPS) * g


def _rope_block(blk, rope_ref, shift):
    return (blk * rope_ref[0] + pltpu.roll(blk, shift, 1) * rope_ref[1]
            + pltpu.roll(blk, LANES - shift, 1) * rope_ref[2])


def _tail_spec(n_tiles, tm, tail_len, width):
    tb = min(tail_len, tm)
    n_blk = tail_len // tb
    return tb, pl.BlockSpec((1, tb, width), lambda b, i: (b, jnp.maximum(i - (n_tiles - n_blk), 0), 0))


def _ab_in_body(x_ref, pre_ref, w_ref, qn_ref, kvn_ref, wuq_ref, wk_ref, wvt_ref, rope_ref,
                qa_ref, ka_ref, vt_ref, qb_ref, kb_ref, vbt_ref, sg_ref, c_ref, kr_ref, kb_tail_ref, vb_tail_ref):
    tm = x_ref.shape[1]
    tb = kb_tail_ref.shape[1]
    xn = _rms(x_ref[0], pre_ref[...]).astype(BF16)
    z = jnp.dot(xn, w_ref[...], preferred_element_type=F32)

    qn = _rms(z[:, AB_Q0:AB_C0], qn_ref[...]).astype(BF16)
    qa = jnp.dot(qn, wuq_ref[...], preferred_element_type=F32) * A_SCALE
    for h in range(A_HEADS):
        blk = slice(h * LANES, (h + 1) * LANES)
        qa_ref[0, :, blk] = _rope_block(qa[:, blk], rope_ref, A_ROPE // 2).astype(BF16)

    c_new = _rms(z[:, AB_C0:AB_KR0], kvn_ref[...])
    c_ref[0] = c_new
    cb = c_new.astype(BF16)
    krot = _rope_block(z[:, AB_KR0:AB_G0], rope_ref, A_ROPE // 2)
    kr_ref[0] = krot[:, A_NOPE:A_NOPE + A_ROPE]
    kn = jnp.dot(cb, wk_ref[...], preferred_element_type=F32)
    row = pl.program_id(1) * tm + lax.broadcasted_iota(jnp.int32, (tm, LANES), 0)
    lane = lax.broadcasted_iota(jnp.int32, (tm, LANES), 1)
    k_shared = krot + jnp.where(lane - A_QK == (row // CHUNK) % A_MASK_CHUNKS, 1.0, 0.0)
    for h in range(A_HEADS):
        blk = slice(h * LANES, (h + 1) * LANES)
        ka_ref[0, :, blk] = (kn[:, blk] + k_shared).astype(BF16)
    vt = jnp.dot(wvt_ref[...], c_new.T.astype(BF16), preferred_element_type=F32)
    row = lax.broadcasted_iota(jnp.int32, vt.shape, 0)
    vt_ref[0, 0] = jnp.where((row & (LANES - 1)) == A_V, 1.0, vt).astype(BF16)

    g = z[:, AB_G0:AB_QB0]
    sg_ref[0] = (g * jax.nn.sigmoid(g)).astype(BF16)
    qb_ref[0] = (z[:, AB_QB0:AB_KB0] * B_SCALE).astype(BF16)
    kb = z[:, AB_KB0:AB_VB0]
    vb = z[:, AB_VB0:AB_NZ]
    kb_ref[0] = kb.astype(BF16)
    vbt = vb.T.astype(BF16)
    for hp in range(B_HEADS // 2):
        for j in range(tm // LANES):
            vbt_ref[0, hp, j] = vbt[hp * LANES:(hp + 1) * LANES, j * LANES:(j + 1) * LANES]
    kb_tail_ref[0] = kb[tm - tb:, :]
    vb_tail_ref[0] = vb[tm - tb:, :]


def _ab_in_proj(x, rope, w, tm, tail_len, tkv):
    bsz, s, _ = x.shape
    nt = s // tm
    per_kv = tkv // tm
    tb, tail_spec = _tail_spec(nt, tm, tail_len, B_WIDTH)

    def full(a):
        return pl.BlockSpec(a.shape, lambda b, i: (0,) * a.ndim)

    def rows(width):
        return pl.BlockSpec((1, tm, width), lambda b, i: (b, i, 0))

    weights = (w["pre"], w["w_in"], w["q_norm"], w["kv_norm"], w["w_uq"], w["w_k"], w["w_vt"])
    vt_spec = pl.BlockSpec((1, 1, A_HEADS * LANES, tm), lambda b, i: (b, i // per_kv, 0, i % per_kv))
    out_shape = (
        jax.ShapeDtypeStruct((bsz, s, A_HEADS * LANES), BF16),
        jax.ShapeDtypeStruct((bsz, s, A_HEADS * LANES), BF16),
        jax.ShapeDtypeStruct((bsz, s // tkv, A_HEADS * LANES, tkv), BF16),
        jax.ShapeDtypeStruct((bsz, s, B_WIDTH), BF16),
        jax.ShapeDtypeStruct((bsz, s, B_WIDTH), BF16),
        jax.ShapeDtypeStruct((bsz, B_HEADS // 2, s // LANES, LANES, LANES), BF16),
        jax.ShapeDtypeStruct((bsz, s, A_WIDTH + B_WIDTH), BF16),
        jax.ShapeDtypeStruct((bsz, s, A_KV_RANK), F32),
        jax.ShapeDtypeStruct((bsz, s, A_ROPE), F32),
        jax.ShapeDtypeStruct((bsz, tail_len, B_WIDTH), F32),
        jax.ShapeDtypeStruct((bsz, tail_len, B_WIDTH), F32),
    )
    vbt_spec = pl.BlockSpec((1, B_HEADS // 2, tm // LANES, LANES, LANES), lambda b, i: (b, 0, i, 0, 0))
    out_specs = (rows(1024), rows(1024), vt_spec, rows(512), rows(512), vbt_spec, rows(1024),
                 rows(A_KV_RANK), rows(A_ROPE), tail_spec, tail_spec)
    return pl.pallas_call(
        _ab_in_body,
        grid=(bsz, nt),
        in_specs=[rows(D_MODEL)] + [full(a) for a in weights]
        + [pl.BlockSpec((3, tm, LANES), lambda b, i: (0, i, 0))],
        out_specs=out_specs,
        out_shape=out_shape,
        compiler_params=_params(2),
        name="ab_in_proj",
    )(x, *weights, rope)


def _mix_out(o_refs, sg_ref, w_ref, g_ref, h_ref):
    o = jnp.concatenate([r[0].astype(F32) for r in o_refs], axis=-1) if len(o_refs) > 1 else o_refs[0][0].astype(F32)
    mixed = (o * sg_ref[0].astype(F32)).astype(BF16)
    y = jnp.dot(mixed, w_ref[...], preferred_element_type=F32)
    return h_ref[0] + _rms(y, g_ref[...])


def _c_in_body(*refs):
    n_o = len(refs) - 14
    o_refs = refs[:n_o]
    (sg0_ref, w0_ref, post0_ref, x_ref, pre_ref, w_ref, rope_ref,
     h_ref, q_ref, k_ref, vt_ref, sg_ref, k_tail_ref, v_tail_ref) = refs[n_o:]
    tm = x_ref.shape[1]
    tb = k_tail_ref.shape[1]
    h = _mix_out(o_refs, sg0_ref, w0_ref, post0_ref, x_ref)
    h_ref[0] = h
    xn = _rms(h, pre_ref[...]).astype(BF16)
    z = jnp.dot(xn, w_ref[...], preferred_element_type=F32)
    half_rot = C_ROT // 2
    for j in range(C_WIDTH // LANES):
        blk = slice(C_Q0 + j * LANES, C_Q0 + (j + 1) * LANES)
        q_ref[0, :, j * LANES:(j + 1) * LANES] = (_rope_block(z[:, blk], rope_ref, half_rot) * C_SCALE).astype(BF16)
    kd = [_rope_block(z[:, C_K0 + j * LANES:C_K0 + (j + 1) * LANES], rope_ref, half_rot) for j in range(C_KV_HEADS)]
    vz = [z[:, C_V0 + j * LANES:C_V0 + (j + 1) * LANES] for j in range(C_KV_HEADS)]
    row = lax.broadcasted_iota(jnp.int32, (LANES, tm), 0)
    for j in range(C_KV_HEADS):
        k_ref[0, :, j * LANES:(j + 1) * LANES] = kd[j].astype(BF16)
        vt = jnp.where(row == C_HEAD_DIM, 1.0, vz[j].T).astype(BF16)
        for i in range(tm // LANES):
            vt_ref[0, j, i] = vt[:, i * LANES:(i + 1) * LANES]
    g = z[:, C_G0:C_NZ]
    sg_ref[0] = (g * jax.nn.sigmoid(g)).astype(BF16)
    lo = lax.broadcasted_iota(jnp.int32, (1, LANES), 1) < HALF
    k_tail_ref[0] = jnp.where(lo, kd[0], kd[1])[tm - tb:, :]
    v_tail_ref[0] = (vz[0] + pltpu.roll(vz[1], HALF, 1))[tm - tb:, :]


def _c_in_proj(o_parts, sg0, w_out0, post0, x, rope, w, tm, tail_len):
    bsz, s, _ = x.shape
    nt = s // tm
    tb, tail_spec = _tail_spec(nt, tm, tail_len, LANES)

    def full(a):
        return pl.BlockSpec(a.shape, lambda b, i: (0,) * a.ndim)

    def rows(width):
        return pl.BlockSpec((1, tm, width), lambda b, i: (b, i, 0))

    out_shape = (
        jax.ShapeDtypeStruct((bsz, s, D_MODEL), F32),
        jax.ShapeDtypeStruct((bsz, s, C_WIDTH), BF16),
        jax.ShapeDtypeStruct((bsz, s, C_KV_HEADS * LANES), BF16),
        jax.ShapeDtypeStruct((bsz, C_KV_HEADS, s // LANES, LANES, LANES), BF16),
        jax.ShapeDtypeStruct((bsz, s, C_WIDTH), BF16),
        jax.ShapeDtypeStruct((bsz, tail_len, LANES), F32),
        jax.ShapeDtypeStruct((bsz, tail_len, LANES), F32),
    )
    return pl.pallas_call(
        _c_in_body,
        grid=(bsz, nt),
        in_specs=[rows(o.shape[-1]) for o in o_parts]
        + [rows(D_MODEL), full(w_out0), full(post0), rows(D_MODEL), full(w["pre"]), full(w["w_in"]),
           pl.BlockSpec((3, tm, LANES), lambda b, i: (0, i, 0))],
        out_specs=(rows(D_MODEL), rows(C_WIDTH), rows(256),
                   pl.BlockSpec((1, C_KV_HEADS, tm // LANES, LANES, LANES), lambda b, i: (b, 0, i, 0, 0)),
                   rows(C_WIDTH), tail_spec, tail_spec),
        out_shape=out_shape,
        compiler_params=_params(2),
        name="c_in_proj",
    )(*o_parts, sg0, w_out0, post0, x, w["pre"], w["w_in"], rope)


def _out_body(*refs):
    o_refs, (sg_ref, w_ref, g_ref, h_ref, out_ref) = refs[:-5], refs[-5:]
    out_ref[0] = _mix_out(o_refs, sg_ref, w_ref, g_ref, h_ref)


def _out_proj(o_parts, sg, w_out, post_g, h, tm):
    bsz, s, _ = h.shape

    def rows(width):
        return pl.BlockSpec((1, tm, width), lambda b, i: (b, i, 0))

    def full(a):
        return pl.BlockSpec(a.shape, lambda b, i: (0,) * a.ndim)

    return pl.pallas_call(
        _out_body,
        grid=(bsz, s // tm),
        in_specs=[rows(o.shape[-1]) for o in o_parts] + [rows(D_MODEL), full(w_out), full(post_g), rows(D_MODEL)],
        out_specs=rows(D_MODEL),
        out_shape=jax.ShapeDtypeStruct(h.shape, F32),
        compiler_params=_params(2),
        name="out_proj",
    )(*o_parts, sg, w_out, post_g, h)


def _mla_prompt_body(q_ref, k_ref, vt_ref, qmask_ref, o_ref, m_sc, acc_sc, s_sc, *, tq):
    n_tiles = q_ref.shape[1] // tq
    n_pairs = n_tiles * (n_tiles + 1) // 2
    assert A_PIPE_UNROLL % 2 == 0 and n_pairs % A_PIPE_UNROLL == 0 and n_pairs >= 2 * A_PIPE_UNROLL
    m_sc[...] = jnp.full(m_sc.shape, NEG_INF, F32)
    acc_sc[...] = jnp.zeros(acc_sc.shape, F32)
    contract_last = (((1,), (1,)), ((), ()))

    def scores(qi, j, slot):
        q_start = pl.multiple_of(qi * tq, tq)
        k_start = pl.multiple_of(j * tq, tq)
        qmask = jnp.where(j == qi, qmask_ref[...], jnp.zeros_like(qmask_ref[...]))
        for hh in range(2):
            blk = slice(hh * LANES, (hh + 1) * LANES)
            q = q_ref[0, pl.ds(q_start, tq), blk] + qmask
            s_sc[slot, hh] = lax.dot_general(k_ref[0, pl.ds(k_start, tq), blk], q, contract_last,
                                             preferred_element_type=F32)

    def consume(qi, j, slot):
        for hh in range(2):
            s = s_sc[slot, hh]
            vt = vt_ref[0, j, hh * LANES:(hh + 1) * LANES, :]
            m_old = m_sc[qi, hh]
            m_new = jnp.maximum(m_old, jnp.max(s, axis=0, keepdims=True))
            p = jnp.exp2(s - m_new).astype(BF16)
            acc_sc[qi, hh] = jnp.exp2(m_old - m_new) * acc_sc[qi, hh] + jnp.dot(vt, p, preferred_element_type=F32)
            m_sc[qi, hh] = m_new

    def advance(qi, j):
        last = j == qi
        return jnp.where(last, qi + 1, qi), jnp.where(last, 0, j + 1)

    def stages(cur, count):
        for st in range(count):
            nxt = advance(*cur)
            scores(*nxt, 1 - (st & 1))
            consume(*cur, st & 1)
            cur = nxt
        return cur

    zero = jnp.int32(0)
    scores(zero, zero, 0)
    cur = lax.fori_loop(0, n_pairs // A_PIPE_UNROLL - 1, lambda i, c: stages(c, A_PIPE_UNROLL), (zero, zero))
    cur = stages(cur, A_PIPE_UNROLL - 1)
    consume(*cur, (A_PIPE_UNROLL - 1) & 1)

    lo = lax.broadcasted_iota(jnp.int32, (1, LANES), 1) < HALF

    def write_tile(qi, carry):
        outs = []
        for hh in range(2):
            acc = acc_sc[qi, hh]
            outs.append((acc / acc[A_V:A_V + 1, :]).T)
        o_ref[0, pl.ds(pl.multiple_of(qi * tq, tq), tq), :] = jnp.where(
            lo, outs[0], pltpu.roll(outs[1], HALF, 1)).astype(o_ref.dtype)
        return carry

    lax.fori_loop(0, n_tiles, write_tile, 0)


def _mla_prompt(qa, ka, vt, tq):
    bsz, s, _ = qa.shape
    pairs = A_HEADS // 2
    n_tiles = s // tq
    assert vt.shape[3] == tq and tq == A_MASK_CHUNKS * CHUNK
    q_chunk = np.arange(tq)[:, None] // CHUNK
    lane_chunk = np.arange(LANES)[None, :] - A_QK
    qmask = np.where((lane_chunk > q_chunk) & (lane_chunk < A_MASK_CHUNKS), NEG_INF, 0.0).astype(np.float32)
    return pl.pallas_call(
        functools.partial(_mla_prompt_body, tq=tq),
        grid=(bsz, pairs),
        in_specs=[pl.BlockSpec((1, s, 2 * LANES), lambda b, h: (b, 0, h)),
                  pl.BlockSpec((1, s, 2 * LANES), lambda b, h: (b, 0, h)),
                  pl.BlockSpec((1, n_tiles, 2 * LANES, tq), lambda b, h: (b, 0, h, 0)),
                  pl.BlockSpec((tq, LANES), lambda b, h: (0, 0))],
        out_specs=pl.BlockSpec((1, s, LANES), lambda b, h: (b, 0, h)),
        out_shape=jax.ShapeDtypeStruct((bsz, s, A_WIDTH), BF16),
        scratch_shapes=[pltpu.VMEM((n_tiles, 2, 1, tq), F32), pltpu.VMEM((n_tiles, 2, LANES, tq), F32),
                        pltpu.VMEM((2, 2, tq, tq), F32)],
        compiler_params=_params(2),
        name="mla_prompt",
    )(qa, ka, vt, jnp.asarray(qmask, BF16))


def _mla_sample_body(q_ref, cc_ref, ckr_ref, cn_ref, krn_ref, wkt_ref, wv_ref, sel_ref, o_ref,
                     qabs_sc, qr_sc, m_sc, acc_sc, *, tk):
    t = q_ref.shape[1]
    past = cc_ref.shape[1]
    for h in range(A_HEADS):
        qh = q_ref[0, :, h * LANES:(h + 1) * LANES]
        rows = slice(h * t, (h + 1) * t)
        qabs_sc[rows, :] = jnp.dot(qh, wkt_ref[h], preferred_element_type=F32).astype(BF16)
        qr_sc[rows, :] = jnp.dot(qh, sel_ref[...], preferred_element_type=F32).astype(BF16)
    m_sc[...] = jnp.full(m_sc.shape, NEG_INF, F32)
    acc_sc[...] = jnp.zeros(acc_sc.shape, F32)
    contract_last = (((1,), (1,)), ((), ()))

    def scores(c_t, kr_t):
        return (lax.dot_general(c_t.astype(BF16), qabs_sc[...], contract_last, preferred_element_type=F32)
                + lax.dot_general(kr_t.astype(BF16), qr_sc[...], contract_last, preferred_element_type=F32))

    def update(s, c_t):
        n_keys = c_t.shape[0]
        m_old = m_sc[...]
        m_new = jnp.maximum(m_old, jnp.max(s, axis=0, keepdims=True))
        p = jnp.exp2(s - m_new).astype(BF16)
        ct = jnp.concatenate([c_t.T.astype(BF16), jnp.ones((BF16_SUBLANES, n_keys), BF16)], axis=0)
        acc_sc[...] = jnp.exp2(m_old - m_new) * acc_sc[...] + jnp.dot(ct, p, preferred_element_type=F32)
        m_sc[...] = m_new

    def tile(j):
        rows = pl.ds(j * tk, tk)
        return cc_ref[0, rows, :], ckr_ref[0, rows, :]

    n_tiles = past // tk
    cur = tile(0)
    s = scores(*cur)
    for j in range(n_tiles):
        nxt = tile(j + 1) if j + 1 < n_tiles else (cn_ref[0], krn_ref[0])
        s_next = scores(*nxt)
        update(s, cur[0])
        cur, s = nxt, s_next
    update(s, cur[0])

    acc = acc_sc[...]
    o_lat = (acc[:A_KV_RANK] / acc[A_KV_RANK:A_KV_RANK + 1]).T.astype(BF16)
    out = jnp.zeros((t, A_WIDTH), F32)
    for h in range(A_HEADS):
        out = out + jnp.dot(o_lat[h * t:(h + 1) * t, :], wv_ref[h], preferred_element_type=F32)
    o_ref[0] = out.astype(o_ref.dtype)


def _mla_sample(qa, cache_c, cache_kr, c_new, kr_new, wkt, wv, sel, tk):
    bsz, t, _ = qa.shape
    past = cache_c.shape[1]

    def per_b(shape):
        return pl.BlockSpec((1,) + shape, lambda b: (b, 0, 0))

    def full(a):
        return pl.BlockSpec(a.shape, lambda b: (0,) * a.ndim)

    rows = A_HEADS * t
    return pl.pallas_call(
        functools.partial(_mla_sample_body, tk=tk),
        grid=(bsz,),
        in_specs=[per_b((t, A_HEADS * LANES)), per_b((past, A_KV_RANK)), per_b((past, A_ROPE)),
                  per_b((t, A_KV_RANK)), per_b((t, A_ROPE)), full(wkt), full(wv), full(sel)],
        out_specs=per_b((t, A_WIDTH)),
        out_shape=jax.ShapeDtypeStruct((bsz, t, A_WIDTH), BF16),
        scratch_shapes=[pltpu.VMEM((rows, A_KV_RANK), BF16), pltpu.VMEM((rows, A_ROPE), BF16),
                        pltpu.VMEM((1, rows), F32), pltpu.VMEM((A_KV_RANK + BF16_SUBLANES, rows), F32)],
        compiler_params=_params(1),
        name="mla_sample",
    )(qa, cache_c, cache_kr, c_new, kr_new, wkt, wv, sel)


def _band_body(*refs, n_blk, n_qblk, cpg, ones_row, has_sink, variants):
    refs = list(refs)
    q_ref, k_ref, vt_ref = refs[:3]
    o_ref, s_sc, bias_sc = refs[-3:]
    extra = refs[3:-3]
    sink_ref = extra.pop() if has_sink else None
    toep_ref = extra.pop() if extra else None
    win = n_blk * LANES
    rows_g = cpg * CHUNK
    n_var = len(variants)
    qpc = 2 * n_qblk * CHUNK
    lo = lax.broadcasted_iota(jnp.int32, (CHUNK, LANES), 1) < HALF
    contract_last = (((1,), (1,)), ((), ()))

    key_chunk = lax.broadcasted_iota(jnp.int32, (win, LANES), 0) // CHUNK
    for v, chunks in enumerate(variants):
        for cc, (x0, lo_chunk, hi_chunk) in enumerate(chunks):
            tile = toep_ref[0, x0:x0 + win, :] if toep_ref is not None else jnp.zeros((win, LANES), F32)
            tile = jnp.where((key_chunk >= lo_chunk) & (key_chunk <= hi_chunk), tile, NEG_INF)
            for j in range(qpc // LANES):
                bias_sc[v, :, cc * qpc + j * LANES:cc * qpc + (j + 1) * LANES] = tile

    def aligned(x, n):
        return x if isinstance(x, int) else pl.multiple_of(x, n)

    def scores(a, wb, variant, slot):
        pieces = []
        for cc in range(cpg):
            rows = pl.ds(aligned(a * rows_g + cc * CHUNK, CHUNK), CHUNK)
            for r in range(n_qblk):
                qblk = q_ref[0, rows, r * LANES:(r + 1) * LANES]
                zero = jnp.zeros_like(qblk)
                pieces += [jnp.where(lo, qblk, zero), jnp.where(lo, zero, qblk)]
        qs = jnp.concatenate(pieces, axis=0)
        kw = k_ref[0, pl.ds(aligned(wb * LANES, LANES), win), :]
        s_sc[slot] = lax.dot_general(kw, qs, contract_last, preferred_element_type=F32) + bias_sc[variant]

    def finish(a, wb, slot):
        s = s_sc[slot]
        m = jnp.max(s, axis=0, keepdims=True)
        if has_sink:
            sink = sink_ref[0]
            m = jnp.maximum(m, sink)
        p = jnp.exp2(s - m)
        vt = jnp.concatenate([vt_ref[0, 0, wb + i] for i in range(n_blk)], axis=1)
        if not ones_row:
            vt = jnp.concatenate([vt, jnp.ones((BF16_SUBLANES, win), BF16)], axis=0)
        o = jnp.dot(vt, p.astype(BF16), preferred_element_type=F32)
        l = o[CHUNK:CHUNK + 1, :] if ones_row else o[LANES:LANES + 1, :]
        o = o[:LANES]
        if has_sink:
            l = l + jnp.exp2(sink - m)
        o = (o / l).T
        idx = 0
        for cc in range(cpg):
            rows = pl.ds(aligned(a * rows_g + cc * CHUNK, CHUNK), CHUNK)
            for r in range(n_qblk):
                top = o[idx * CHUNK:(idx + 1) * CHUNK]
                bot = o[(idx + 1) * CHUNK:(idx + 2) * CHUNK]
                if ones_row:
                    bot = pltpu.roll(bot, HALF, 1)
                o_ref[0, rows, r * LANES:(r + 1) * LANES] = jnp.where(lo, top, bot).astype(o_ref.dtype)
                idx += 2

    n_total = q_ref.shape[1] // rows_g
    if n_total == 1:
        scores(0, 0, 0, 0)
        finish(0, 0, 0)
        return
    g_blk = rows_g // LANES
    n_lead = n_var - 1 + (n_var - 1) % 2
    assert (n_var - 1) * g_blk >= n_blk - g_blk and n_total - n_lead >= 2 and BAND_PIPE_UNROLL % 2 == 0

    def window_block(a):
        wb = (a + 1) * g_blk - n_blk
        return max(wb, 0) if isinstance(a, int) else wb

    def stage(a, slot):
        nxt = a + 1
        scores(nxt, window_block(nxt), min(nxt, n_var - 1) if isinstance(nxt, int) else n_var - 1, 1 - slot)
        finish(a, window_block(a), slot)

    scores(0, window_block(0), 0, 0)
    for a in range(n_lead):
        stage(a, a & 1)

    def trip(i, carry):
        for st in range(BAND_PIPE_UNROLL):
            stage(n_lead + BAND_PIPE_UNROLL * i + st, st & 1)
        return carry

    n_stages = n_total - 1 - n_lead
    lax.fori_loop(0, n_stages // BAND_PIPE_UNROLL, trip, 0)
    for a in range(n_total - 1 - n_stages % BAND_PIPE_UNROLL, n_total - 1):
        stage(a, (a - n_lead) & 1)
    finish(n_total - 1, window_block(n_total - 1), (n_total - 1 - n_lead) & 1)


def _band_attention(q, k, vt, toep, sink, *, variants, n_blk, n_groups, n_qblk, cpg, ones_row):
    bsz, s, _ = q.shape
    sk = k.shape[1]
    qw = n_qblk * LANES
    in_specs = [pl.BlockSpec((1, s, qw), lambda b, g: (b, 0, g)),
                pl.BlockSpec((1, sk, LANES), lambda b, g: (b, 0, g)),
                pl.BlockSpec((1, 1) + vt.shape[2:], lambda b, g: (b, g, 0, 0, 0))]
    args = [q, k, vt]
    for extra in (toep, sink):
        if extra is not None:
            in_specs.append(pl.BlockSpec((1,) + extra.shape[1:], lambda b, g: (g, 0, 0)))
            args.append(extra)
    bias_shape = (n_blk * LANES, cpg * 2 * n_qblk * CHUNK)
    body = functools.partial(_band_body, n_blk=n_blk, n_qblk=n_qblk, cpg=cpg, ones_row=ones_row,
                             has_sink=sink is not None, variants=variants)
    return pl.pallas_call(
        body,
        grid=(bsz, n_groups),
        in_specs=in_specs,
        out_specs=pl.BlockSpec((1, s, qw), lambda b, g: (b, 0, g)),
        out_shape=jax.ShapeDtypeStruct(q.shape, BF16),
        scratch_shapes=[pltpu.VMEM((2,) + bias_shape, F32), pltpu.VMEM((len(variants),) + bias_shape, F32)],
        compiler_params=_params(2),
        name="band_attention",
    )(*args)


def _rope_tables(pos, rot, lane_pattern, from_zero=False):
    half = rot // 2
    inv = jnp.power(ROPE_THETA, -jnp.arange(half, dtype=F32) * 2.0 / rot)
    inv_lane, first, second = [], [], []
    for kind, width in lane_pattern:
        if kind == "rot":
            inv_lane += [inv, inv]
            first += [1.0] * half + [0.0] * half
            second += [0.0] * half + [1.0] * half
        else:
            inv_lane.append(jnp.zeros((width,), F32))
            first += [0.0] * width
            second += [0.0] * width
    inv_lane = jnp.concatenate(inv_lane)[None, :]
    n = pos.shape[0]
    if from_zero and n % CHUNK == 0:
        a_hi = (jnp.arange(n // CHUNK, dtype=F32) * CHUNK)[:, None] * inv_lane
        a_lo = jnp.arange(CHUNK, dtype=F32)[:, None] * inv_lane
        c_hi, s_hi = jnp.cos(a_hi)[:, None, :], jnp.sin(a_hi)[:, None, :]
        c_lo, s_lo = jnp.cos(a_lo)[None], jnp.sin(a_lo)[None]
        cos = (c_hi * c_lo - s_hi * s_lo).reshape(n, LANES)
        sin = (s_hi * c_lo + c_hi * s_lo).reshape(n, LANES)
    else:
        ang = pos.astype(F32)[:, None] * inv_lane
        cos, sin = jnp.cos(ang), jnp.sin(ang)
    return jnp.stack([cos, sin * np.asarray(second, np.float32), -sin * np.asarray(first, np.float32)])


A_ROPE_PATTERN = (("pad", A_NOPE), ("rot", A_ROPE), ("pad", LANES - A_QK))
C_ROPE_PATTERN = (("rot", C_ROT), ("pad", HALF - C_ROT)) * 2


def _prep_ab(pre, post, w_in, q_norm, kv_norm, w_uq, w_ukv, rel_bias, w_out):
    d = w_in.shape[0]
    q_lat, c_kv, k_r, g_a, q_b, k_b, v_b, g_b = jnp.split(
        w_in, [384, 640, 672, 1184, 1696, 2208, 2720], axis=1)
    kr_blk = jnp.concatenate([jnp.zeros((d, A_NOPE), F32), k_r, jnp.zeros((d, LANES - A_QK), F32)], axis=1)
    w_in_p = jnp.concatenate([q_lat, c_kv, kr_blk, g_a, g_b, q_b, k_b, v_b], axis=1).astype(BF16)
    w_uq_p = jnp.pad(w_uq.reshape(A_Q_RANK, A_HEADS, A_QK), ((0, 0), (0, 0), (0, LANES - A_QK)))
    w_uq_p = w_uq_p.reshape(A_Q_RANK, A_HEADS * LANES).astype(BF16)
    ukv = w_ukv.reshape(A_KV_RANK, A_HEADS, A_NOPE + A_V)
    w_uk, w_uv = ukv[..., :A_NOPE], ukv[..., A_NOPE:]
    pad_half = ((0, 0), (0, 0), (0, LANES - A_NOPE))
    w_k = jnp.pad(w_uk, pad_half).reshape(A_KV_RANK, A_HEADS * LANES).astype(BF16)
    w_vt = jnp.pad(w_uv, pad_half).reshape(A_KV_RANK, A_HEADS * LANES).T.astype(BF16)
    wkt = jnp.pad(jnp.transpose(w_uk, (1, 2, 0)), ((0, 0), (0, LANES - A_NOPE), (0, 0))).astype(BF16)
    eye = jnp.eye(A_HEADS, dtype=F32)
    wv_s = (jnp.transpose(w_uv, (1, 0, 2))[:, :, None, :] * eye[:, None, :, None]).reshape(
        A_HEADS, A_KV_RANK, A_WIDTH).astype(BF16)
    sel = (jnp.arange(LANES)[:, None] == A_NOPE + jnp.arange(A_ROPE)[None, :]).astype(BF16)
    rel_bias = rel_bias * LOG2E
    win_p, win_s = B_WIN_BLOCKS * LANES, B_SAMPLE_BLOCKS * LANES
    r0 = win_p - CHUNK
    x_len = r0 + win_p
    n_vec = x_len + CHUNK
    n_hi = r0 + CHUNK - 1 - B_MAX_REL
    n_lo = n_vec - n_hi - (2 * B_MAX_REL + 1)
    vec = jnp.concatenate([jnp.broadcast_to(rel_bias[:, -1:], (B_HEADS, n_hi)), rel_bias[:, ::-1],
                           jnp.broadcast_to(rel_bias[:, :1], (B_HEADS, n_lo))], axis=1)
    skew = jnp.tile(vec, (1, CHUNK))[:, :CHUNK * (n_vec - 1)].reshape(B_HEADS, CHUNK, n_vec - 1)
    toep = skew[:, :, CHUNK - 1:CHUNK - 1 + x_len]
    toep = jnp.transpose(toep.reshape(B_HEADS // 2, 2, CHUNK, x_len), (0, 3, 1, 2)).reshape(B_HEADS // 2, x_len, LANES)
    var_p = tuple(tuple((r0 - c * CHUNK, c - B_PAST_CHUNKS, c)
                        for c in range(B_GROUP_CHUNKS * v, B_GROUP_CHUNKS * (v + 1))) for v in range(B_VARIANTS))
    var_s = (((r0 - (win_s - CHUNK), 1, B_PAST_CHUNKS + 1),),)
    return dict(pre=pre[None], w_in=w_in_p, q_norm=q_norm[None], kv_norm=kv_norm[None], w_uq=w_uq_p, w_k=w_k,
                w_vt=w_vt, wkt=wkt, wv_s=wv_s, sel=sel, toep=toep, var_p=var_p, var_s=var_s,
                w_out=w_out.astype(BF16), post=post[None])


def _prep_c(pre, post, w_in, sinks, w_out):
    q, k, v, g = jnp.split(w_in, [1024, 1152, 1280], axis=1)
    k0, k1 = k[:, :C_HEAD_DIM], k[:, C_HEAD_DIM:]
    v0, v1 = v[:, :C_HEAD_DIM], v[:, C_HEAD_DIM:]
    zero = jnp.zeros_like(v0)
    w_in_p = jnp.concatenate([q, k0, k0, k1, k1, v0, zero, v1, zero, g], axis=1).astype(BF16)
    var_p = tuple(tuple((0, c - C_PAST_CHUNKS, c) for c in range(C_GROUP_CHUNKS * v, C_GROUP_CHUNKS * (v + 1)))
                  for v in range(C_VARIANTS))
    var_s = (((0, 1, C_PAST_CHUNKS + 1),),)
    sink_row = jnp.repeat((sinks * LOG2E).reshape(C_KV_HEADS, C_GROUP), CHUNK, axis=1)[:, None, :]
    return dict(pre=pre[None], w_in=w_in_p, var_p=var_p, var_s=var_s, sink_s=sink_row,
                sink_p=jnp.tile(sink_row, (1, 1, C_GROUP_CHUNKS)), w_out=w_out.astype(BF16), post=post[None])


def _dup_heads(x):
    return jnp.concatenate([x[:, :, 0], x[:, :, 0], x[:, :, 1], x[:, :, 1]], axis=-1)


def kernel(x_prompt, x_sample, cache_a_ckv, cache_a_krope, cache_b_k, cache_b_v, cache_c_k, cache_c_v,
           ab_pre_norm, ab_post_norm, ab_w_in, ab_q_norm, ab_kv_norm, ab_w_uq, ab_w_ukv, ab_rel_bias, ab_w_out,
           c_pre_norm, c_post_norm, c_w_in, c_sinks, c_w_out):
    bsz, seq, _ = x_prompt.shape
    dbs, dseq, _ = x_sample.shape
    past = cache_a_ckv.shape[2]
    n_s = dbs * dseq
    pos_p = jnp.arange(seq, dtype=jnp.int32)
    pos_s = jnp.tile(past + jnp.arange(dseq, dtype=jnp.int32), dbs)
    wab = _prep_ab(ab_pre_norm[0], ab_post_norm[0], ab_w_in[0], ab_q_norm[0], ab_kv_norm[0], ab_w_uq[0],
                   ab_w_ukv[0], ab_rel_bias[0], ab_w_out[0])
    wc = _prep_c(c_pre_norm[0], c_post_norm[0], c_w_in[0], c_sinks[0], c_w_out[0])
    b_tail = min(B_PAST_CHUNKS * CHUNK, seq)
    c_tail = min(C_WINDOW, seq)
    tile = 512

    rope_a_p = _rope_tables(pos_p, A_ROPE, A_ROPE_PATTERN, from_zero=True)
    (qa, ka, vt, qb, kb, vbt, sg, c_new_p, kr_new_p, kb_tail, vb_tail) = _ab_in_proj(
        x_prompt, rope_a_p, wab, tm=256, tail_len=b_tail, tkv=tile)
    o_a = _mla_prompt(qa, ka, vt, tq=tile)
    o_b = _band_attention(qb, kb, vbt, wab["toep"], None, variants=wab["var_p"], n_blk=B_WIN_BLOCKS,
                          n_groups=B_HEADS // 2, n_qblk=1, cpg=B_GROUP_CHUNKS, ones_row=False)

    rope_a_s = _rope_tables(pos_s, A_ROPE, A_ROPE_PATTERN)
    xs = x_sample.reshape(1, n_s, D_MODEL)
    (qa_s, _, _, qb_s, _, _, sg_s, c_new_s, kr_new_s, kb_s32, vb_s32) = _ab_in_proj(
        xs, rope_a_s, wab, tm=n_s, tail_len=n_s, tkv=n_s)
    o_a_s = _mla_sample(qa_s.reshape(dbs, dseq, -1), cache_a_ckv[0], cache_a_krope[0],
                        c_new_s.reshape(dbs, dseq, -1), kr_new_s.reshape(dbs, dseq, -1),
                        wab["wkt"], wab["wv_s"], wab["sel"], tk=512)
    wb = cache_b_k.shape[2]
    pad_b = jnp.zeros((dbs, B_SAMPLE_BLOCKS * LANES - wb - dseq, B_WIDTH), F32)
    kband = jnp.concatenate([pad_b, cache_b_k[0].reshape(dbs, wb, B_WIDTH), kb_s32.reshape(dbs, dseq, -1)], 1)
    vband = jnp.concatenate([pad_b, cache_b_v[0].reshape(dbs, wb, B_WIDTH), vb_s32.reshape(dbs, dseq, -1)], 1)
    vband_t = jnp.transpose(vband.astype(BF16).reshape(dbs, B_SAMPLE_BLOCKS, LANES, B_HEADS // 2, LANES), (0, 3, 1, 4, 2))
    o_b_s = _band_attention(qb_s.reshape(dbs, dseq, -1), kband.astype(BF16), vband_t, wab["toep"], None,
                            variants=wab["var_s"], n_blk=B_SAMPLE_BLOCKS, n_groups=B_HEADS // 2, n_qblk=1, cpg=1,
                            ones_row=False)

    rope_c_p = _rope_tables(pos_p, C_ROT, C_ROPE_PATTERN, from_zero=True)
    h1_p, qc, kc, vct, sgc, kc_tail, vc_tail = _c_in_proj(
        [o_a, o_b], sg, wab["w_out"], wab["post"], x_prompt, rope_c_p, wc, tm=tile, tail_len=c_tail)
    o_c = _band_attention(qc, kc, vct, None, wc["sink_p"], variants=wc["var_p"], n_blk=C_WIN_BLOCKS,
                          n_groups=C_KV_HEADS, n_qblk=C_GROUP // 2, cpg=C_GROUP_CHUNKS, ones_row=True)
    h2_p = _out_proj([o_c], sgc, wc["w_out"], wc["post"], h1_p, tm=tile)

    rope_c_s = _rope_tables(pos_s, C_ROT, C_ROPE_PATTERN)
    h1_s, qc_s, kc_s, _, sgc_s, kc_s32, vc_s32 = _c_in_proj(
        [o_a_s.reshape(1, n_s, -1), o_b_s.reshape(1, n_s, -1)], sg_s, wab["w_out"], wab["post"], xs, rope_c_s, wc,
        tm=n_s, tail_len=n_s)
    wcw = cache_c_k.shape[2]
    win_c = C_SAMPLE_BLOCKS * LANES
    n_pad = win_c - wcw - dseq
    kcb = jnp.concatenate([jnp.zeros((dbs, n_pad, C_KV_HEADS * LANES), BF16), _dup_heads(cache_c_k[0]).astype(BF16),
                           kc_s.reshape(dbs, dseq, -1)], axis=1)
    vcb = jnp.concatenate([jnp.zeros((dbs, n_pad, C_KV_HEADS, C_HEAD_DIM), F32), cache_c_v[0],
                           vc_s32.reshape(dbs, dseq, C_KV_HEADS, C_HEAD_DIM)], axis=1)
    vcb_t = jnp.concatenate([jnp.transpose(vcb, (0, 2, 3, 1)), jnp.ones((dbs, C_KV_HEADS, 1, win_c), F32),
                             jnp.zeros((dbs, C_KV_HEADS, LANES - C_HEAD_DIM - 1, win_c), F32)], axis=2)
    vcb_t = jnp.transpose(vcb_t.astype(BF16).reshape(dbs, C_KV_HEADS, LANES, C_SAMPLE_BLOCKS, LANES), (0, 1, 3, 2, 4))
    o_c_s = _band_attention(qc_s.reshape(dbs, dseq, -1), kcb, vcb_t, None, wc["sink_s"], variants=wc["var_s"],
                            n_blk=C_SAMPLE_BLOCKS, n_groups=C_KV_HEADS, n_qblk=C_GROUP // 2, cpg=1, ones_row=True)
    h2_s = _out_proj([o_c_s.reshape(1, n_s, -1)], sgc_s, wc["w_out"], wc["post"], h1_s, tm=n_s)

    def roll_in(buf, new):
        return jnp.concatenate([buf, new], axis=1)[:, -buf.shape[1]:][None]

    return (h2_p, h2_s.reshape(dbs, dseq, D_MODEL),
            c_new_p[None], kr_new_p[None],
            kb_tail.reshape(1, bsz, b_tail, B_HEADS, B_HEAD_DIM), vb_tail.reshape(1, bsz, b_tail, B_HEADS, B_HEAD_DIM),
            kc_tail.reshape(1, bsz, c_tail, C_KV_HEADS, C_HEAD_DIM), vc_tail.reshape(1, bsz, c_tail, C_KV_HEADS, C_HEAD_DIM),
            c_new_s.reshape(1, dbs, dseq, A_KV_RANK), kr_new_s.reshape(1, dbs, dseq, A_ROPE),
            roll_in(cache_b_k[0], kb_s32.reshape(dbs, dseq, B_HEADS, B_HEAD_DIM)),
            roll_in(cache_b_v[0], vb_s32.reshape(dbs, dseq, B_HEADS, B_HEAD_DIM)),
            roll_in(cache_c_k[0], kc_s32.reshape(dbs, dseq, C_KV_HEADS, C_HEAD_DIM)),
            roll_in(cache_c_v[0], vc_s32.reshape(dbs, dseq, C_KV_HEADS, C_HEAD_DIM)))
```

```python
import functools

import jax
import jax.numpy as jnp
import numpy as np
from jax import lax
from jax.experimental import pallas as pl
from jax.experimental.pallas import tpu as pltpu

F32 = jnp.float32
BF16 = jnp.bfloat16

D_MODEL = 1024
CHUNK = 64
ROPE_THETA = 500000.0
RMS_EPS = 1e-6
NEG_INF = -1e30

A_HEADS = 8
A_NOPE = 64
A_ROPE = 32
A_QK = A_NOPE + A_ROPE
A_V = 64
A_Q_RANK = 384
A_KV_RANK = 256
A_WIDTH = A_HEADS * A_V
LOG2E = 1.4426950408889634
A_SCALE = A_QK ** -0.5 * LOG2E

B_HEADS = 8
B_HEAD_DIM = 64
B_WIDTH = B_HEADS * B_HEAD_DIM
B_PAST_CHUNKS = 8
B_MAX_REL = 128
B_SCALE = B_HEAD_DIM ** -0.5 * LOG2E

C_HEADS = 16
C_KV_HEADS = 2
C_GROUP = C_HEADS // C_KV_HEADS
C_HEAD_DIM = 64
C_WIDTH = C_HEADS * C_HEAD_DIM
C_WINDOW = 128
C_PAST_CHUNKS = C_WINDOW // CHUNK
C_ROT = C_HEAD_DIM // 4
C_SCALE = C_HEAD_DIM ** -0.5 * LOG2E

LANES = 128
HALF = LANES // 2
BF16_SUBLANES = 16
A_VT_ROWS = A_V + BF16_SUBLANES
VMEM_LIMIT = 56 * 1024 * 1024
A_MASK_CHUNKS = 8
assert A_QK + A_MASK_CHUNKS <= LANES
A_PIPE_UNROLL = 4
BAND_PIPE_UNROLL = 4
B_GROUP_CHUNKS = 4
C_GROUP_CHUNKS = 2
B_WIN_BLOCKS = (B_PAST_CHUNKS + B_GROUP_CHUNKS) * CHUNK // LANES
C_WIN_BLOCKS = (C_PAST_CHUNKS + C_GROUP_CHUNKS) * CHUNK // LANES
B_VARIANTS = B_PAST_CHUNKS // B_GROUP_CHUNKS + 1
C_VARIANTS = C_PAST_CHUNKS // C_GROUP_CHUNKS + 1
B_SAMPLE_BLOCKS =(B_PAST_CHUNKS + 2) * CHUNK // LANES
C_SAMPLE_BLOCKS = (C_PAST_CHUNKS + 2) * CHUNK // LANES

AB_Q0, AB_C0, AB_KR0, AB_G0, AB_QB0, AB_KB0, AB_VB0, AB_NZ = 0, 384, 640, 768, 1792, 2304, 2816, 3328
C_Q0, C_K0, C_V0, C_G0, C_NZ = 0, 1024, 1280, 1536, 2560


def _params(n_axes):
    return pltpu.CompilerParams(dimension_semantics=("arbitrary",) * n_axes, vmem_limit_bytes=VMEM_LIMIT)


def _rms(x, g):
    return x * lax.rsqrt(jnp.mean(x * x, axis=-1, keepdims=True) + RMS_EPS) * g


def _rope_block(blk, rope_ref, shift):
    return (blk * rope_ref[0] + pltpu.roll(blk, shift, 1) * rope_ref[1]
            + pltpu.roll(blk, LANES - shift, 1) * rope_ref[2])


def _tail_spec(n_tiles, tm, tail_len, width):
    tb = min(tail_len, tm)
    n_blk = tail_len // tb
    return tb, pl.BlockSpec((1, tb, width), lambda b, i: (b, jnp.maximum(i - (n_tiles - n_blk), 0), 0))


def _ab_in_body(x_ref, pre_ref, w_ref, qn_ref, kvn_ref, wuq_ref, wk_ref, wvt_ref, vt_ones_ref, rope_ref,
                qa_ref, ka_ref, vt_ref, qb_ref, kb_ref, vbt_ref, sg_ref, c_ref, kr_ref, kb_tail_ref, vb_tail_ref):
    tm = x_ref.shape[1]
    tb = kb_tail_ref.shape[1]
    xn = _rms(x_ref[0], pre_ref[...]).astype(BF16)
    z = jnp.dot(xn, w_ref[...], preferred_element_type=F32)

    qn = _rms(z[:, AB_Q0:AB_C0], qn_ref[...]).astype(BF16)
    qa = jnp.dot(qn, wuq_ref[...], preferred_element_type=F32) * A_SCALE
    for h in range(A_HEADS):
        blk = slice(h * LANES, (h + 1) * LANES)
        qa_ref[0, :, blk] = _rope_block(qa[:, blk], rope_ref, A_ROPE // 2).astype(BF16)

    c_new = _rms(z[:, AB_C0:AB_KR0], kvn_ref[...])
    c_ref[0] = c_new
    cb = c_new.astype(BF16)
    krot = _rope_block(z[:, AB_KR0:AB_G0], rope_ref, A_ROPE // 2)
    kr_ref[0] = krot[:, A_NOPE:A_NOPE + A_ROPE]
    kn = jnp.dot(cb, wk_ref[...], preferred_element_type=F32)
    row = pl.program_id(1) * tm + lax.broadcasted_iota(jnp.int32, (tm, LANES), 0)
    lane = lax.broadcasted_iota(jnp.int32, (tm, LANES), 1)
    k_shared = krot + jnp.where(lane - A_QK == (row // CHUNK) % A_MASK_CHUNKS, 1.0, 0.0)
    for h in range(A_HEADS):
        blk = slice(h * LANES, (h + 1) * LANES)
        ka_ref[0, :, blk] = (kn[:, blk] + k_shared).astype(BF16)
    vt = jnp.dot(wvt_ref[...], c_new.T.astype(BF16), preferred_element_type=F32)
    vt_ref[0, 0] = (vt + vt_ones_ref[...]).astype(BF16)

    g = z[:, AB_G0:AB_QB0]
    sg_ref[0] = (g * jax.nn.sigmoid(g)).astype(BF16)
    qb_ref[0] = (z[:, AB_QB0:AB_KB0] * B_SCALE).astype(BF16)
    kb = z[:, AB_KB0:AB_VB0]
    vb = z[:, AB_VB0:AB_NZ]
    kb_ref[0] = kb.astype(BF16)
    vbt = vb.T.astype(BF16)
    for hp in range(B_HEADS // 2):
        for j in range(tm // LANES):
            vbt_ref[0, hp, j] = vbt[hp * LANES:(hp + 1) * LANES, j * LANES:(j + 1) * LANES]
    kb_tail_ref[0] = kb[tm - tb:, :]
    vb_tail_ref[0] = vb[tm - tb:, :]


def _ab_in_proj(x, rope, w, tm, tail_len, tkv):
    bsz, s, _ = x.shape
    nt = s // tm
    per_kv = tkv // tm
    tb, tail_spec = _tail_spec(nt, tm, tail_len, B_WIDTH)

    def full(a):
        return pl.BlockSpec(a.shape, lambda b, i: (0,) * a.ndim)

    def rows(width):
        return pl.BlockSpec((1, tm, width), lambda b, i: (b, i, 0))

    weights = (w["pre"], w["w_in"], w["q_norm"], w["kv_norm"], w["w_uq"], w["w_k"], w["w_vt"], w["vt_ones"])
    vt_spec = pl.BlockSpec((1, 1, A_HEADS * A_VT_ROWS, tm), lambda b, i: (b, i // per_kv, 0, i % per_kv))
    out_shape = (
        jax.ShapeDtypeStruct((bsz, s, A_HEADS * LANES), BF16),
        jax.ShapeDtypeStruct((bsz, s, A_HEADS * LANES), BF16),
        jax.ShapeDtypeStruct((bsz, s // tkv, A_HEADS * A_VT_ROWS, tkv), BF16),
        jax.ShapeDtypeStruct((bsz, s, B_WIDTH), BF16),
        jax.ShapeDtypeStruct((bsz, s, B_WIDTH), BF16),
        jax.ShapeDtypeStruct((bsz, B_HEADS // 2, s // LANES, LANES, LANES), BF16),
        jax.ShapeDtypeStruct((bsz, s, A_WIDTH + B_WIDTH), BF16),
        jax.ShapeDtypeStruct((bsz, s, A_KV_RANK), F32),
        jax.ShapeDtypeStruct((bsz, s, A_ROPE), F32),
        jax.ShapeDtypeStruct((bsz, tail_len, B_WIDTH), F32),
        jax.ShapeDtypeStruct((bsz, tail_len, B_WIDTH), F32),
    )
    vbt_spec = pl.BlockSpec((1, B_HEADS // 2, tm // LANES, LANES, LANES), lambda b, i: (b, 0, i, 0, 0))
    out_specs = (rows(1024), rows(1024), vt_spec, rows(512), rows(512), vbt_spec, rows(1024),
                 rows(A_KV_RANK), rows(A_ROPE), tail_spec, tail_spec)
    return pl.pallas_call(
        _ab_in_body,
        grid=(bsz, nt),
        in_specs=[rows(D_MODEL)] + [full(a) for a in weights]
        + [pl.BlockSpec((3, tm, LANES), lambda b, i: (0, i, 0))],
        out_specs=out_specs,
        out_shape=out_shape,
        compiler_params=_params(2),
        name="ab_in_proj",
    )(x, *weights, rope)


def _mix_out(o_refs, sg_ref, w_ref, g_ref, h_ref):
    o = jnp.concatenate([r[0].astype(F32) for r in o_refs], axis=-1) if len(o_refs) > 1 else o_refs[0][0].astype(F32)
    mixed = (o * sg_ref[0].astype(F32)).astype(BF16)
    y = jnp.dot(mixed, w_ref[...], preferred_element_type=F32)
    return h_ref[0] + _rms(y, g_ref[...])


def _c_in_body(*refs):
    n_o = len(refs) - 14
    o_refs = refs[:n_o]
    (sg0_ref, w0_ref, post0_ref, x_ref, pre_ref, w_ref, rope_ref,
     h_ref, q_ref, k_ref, vt_ref, sg_ref, k_tail_ref, v_tail_ref) = refs[n_o:]
    tm = x_ref.shape[1]
    tb = k_tail_ref.shape[1]
    h = _mix_out(o_refs, sg0_ref, w0_ref, post0_ref, x_ref)
    h_ref[0] = h
    xn = _rms(h, pre_ref[...]).astype(BF16)
    z = jnp.dot(xn, w_ref[...], preferred_element_type=F32)
    half_rot = C_ROT // 2
    for j in range(C_WIDTH // LANES):
        blk = slice(C_Q0 + j * LANES, C_Q0 + (j + 1) * LANES)
        q_ref[0, :, j * LANES:(j + 1) * LANES] = (_rope_block(z[:, blk], rope_ref, half_rot) * C_SCALE).astype(BF16)
    kd = [_rope_block(z[:, C_K0 + j * LANES:C_K0 + (j + 1) * LANES], rope_ref, half_rot) for j in range(C_KV_HEADS)]
    vz = [z[:, C_V0 + j * LANES:C_V0 + (j + 1) * LANES] for j in range(C_KV_HEADS)]
    row = lax.broadcasted_iota(jnp.int32, (LANES, tm), 0)
    for j in range(C_KV_HEADS):
        k_ref[0, :, j * LANES:(j + 1) * LANES] = kd[j].astype(BF16)
        vt = jnp.where(row == C_HEAD_DIM, 1.0, vz[j].T).astype(BF16)
        for i in range(tm // LANES):
            vt_ref[0, j, i] = vt[:, i * LANES:(i + 1) * LANES]
    g = z[:, C_G0:C_NZ]
    sg_ref[0] = (g * jax.nn.sigmoid(g)).astype(BF16)
    lo = lax.broadcasted_iota(jnp.int32, (1, LANES), 1) < HALF
    k_tail_ref[0] = jnp.where(lo, kd[0], kd[1])[tm - tb:, :]
    v_tail_ref[0] = (vz[0] + pltpu.roll(vz[1], HALF, 1))[tm - tb:, :]


def _c_in_proj(o_parts, sg0, w_out0, post0, x, rope, w, tm, tail_len):
    bsz, s, _ = x.shape
    nt = s // tm
    tb, tail_spec = _tail_spec(nt, tm, tail_len, LANES)

    def full(a):
        return pl.BlockSpec(a.shape, lambda b, i: (0,) * a.ndim)

    def rows(width):
        return pl.BlockSpec((1, tm, width), lambda b, i: (b, i, 0))

    out_shape = (
        jax.ShapeDtypeStruct((bsz, s, D_MODEL), F32),
        jax.ShapeDtypeStruct((bsz, s, C_WIDTH), BF16),
        jax.ShapeDtypeStruct((bsz, s, C_KV_HEADS * LANES), BF16),
        jax.ShapeDtypeStruct((bsz, C_KV_HEADS, s // LANES, LANES, LANES), BF16),
        jax.ShapeDtypeStruct((bsz, s, C_WIDTH), BF16),
        jax.ShapeDtypeStruct((bsz, tail_len, LANES), F32),
        jax.ShapeDtypeStruct((bsz, tail_len, LANES), F32),
    )
    return pl.pallas_call(
        _c_in_body,
        grid=(bsz, nt),
        in_specs=[rows(o.shape[-1]) for o in o_parts]
        + [rows(D_MODEL), full(w_out0), full(post0), rows(D_MODEL), full(w["pre"]), full(w["w_in"]),
           pl.BlockSpec((3, tm, LANES), lambda b, i: (0, i, 0))],
        out_specs=(rows(D_MODEL), rows(C_WIDTH), rows(256),
                   pl.BlockSpec((1, C_KV_HEADS, tm // LANES, LANES, LANES), lambda b, i: (b, 0, i, 0, 0)),
                   rows(C_WIDTH), tail_spec, tail_spec),
        out_shape=out_shape,
        compiler_params=_params(2),
        name="c_in_proj",
    )(*o_parts, sg0, w_out0, post0, x, w["pre"], w["w_in"], rope)


def _out_body(*refs):
    o_refs, (sg_ref, w_ref, g_ref, h_ref, out_ref) = refs[:-5], refs[-5:]
    out_ref[0] = _mix_out(o_refs, sg_ref, w_ref, g_ref, h_ref)


def _out_proj(o_parts, sg, w_out, post_g, h, tm):
    bsz, s, _ = h.shape

    def rows(width):
        return pl.BlockSpec((1, tm, width), lambda b, i: (b, i, 0))

    def full(a):
        return pl.BlockSpec(a.shape, lambda b, i: (0,) * a.ndim)

    return pl.pallas_call(
        _out_body,
        grid=(bsz, s // tm),
        in_specs=[rows(o.shape[-1]) for o in o_parts] + [rows(D_MODEL), full(w_out), full(post_g), rows(D_MODEL)],
        out_specs=rows(D_MODEL),
        out_shape=jax.ShapeDtypeStruct(h.shape, F32),
        compiler_params=_params(2),
        name="out_proj",
    )(*o_parts, sg, w_out, post_g, h)


def _mla_prompt_body(q_ref, k_ref, vt_ref, qmask_ref, o_ref, m_sc, acc_sc, s_sc, *, tq):
    n_tiles = q_ref.shape[1] // tq
    n_pairs = n_tiles * (n_tiles + 1) // 2
    assert A_PIPE_UNROLL % 2 == 0 and n_pairs % A_PIPE_UNROLL == 0 and n_pairs >= 2 * A_PIPE_UNROLL
    m_sc[...] = jnp.full(m_sc.shape, NEG_INF, F32)
    acc_sc[...] = jnp.zeros(acc_sc.shape, F32)
    contract_last = (((1,), (1,)), ((), ()))

    def scores(qi, j, slot):
        q_start = pl.multiple_of(qi * tq, tq)
        k_start = pl.multiple_of(j * tq, tq)
        qmask = jnp.where(j == qi, qmask_ref[...], jnp.zeros_like(qmask_ref[...]))
        for hh in range(2):
            blk = slice(hh * LANES, (hh + 1) * LANES)
            q = q_ref[0, pl.ds(q_start, tq), blk] + qmask
            s_sc[slot, hh] = lax.dot_general(k_ref[0, pl.ds(k_start, tq), blk], q, contract_last,
                                             preferred_element_type=F32)

    def consume(qi, j, slot):
        for hh in range(2):
            s = s_sc[slot, hh]
            vt = vt_ref[0, j, hh * A_VT_ROWS:(hh + 1) * A_VT_ROWS, :]
            m_old = m_sc[qi, hh]
            m_new = jnp.maximum(m_old, jnp.max(s, axis=0, keepdims=True))
            p = jnp.exp2(s - m_new).astype(BF16)
            acc_sc[qi, hh] = jnp.exp2(m_old - m_new) * acc_sc[qi, hh] + jnp.dot(vt, p, preferred_element_type=F32)
            m_sc[qi, hh] = m_new

    def advance(qi, j):
        last = j == qi
        return jnp.where(last, qi + 1, qi), jnp.where(last, 0, j + 1)

    def stages(cur, count):
        for st in range(count):
            nxt = advance(*cur)
            scores(*nxt, 1 - (st & 1))
            consume(*cur, st & 1)
            cur = nxt
        return cur

    zero = jnp.int32(0)
    scores(zero, zero, 0)
    cur = lax.fori_loop(0, n_pairs // A_PIPE_UNROLL - 1, lambda i, c: stages(c, A_PIPE_UNROLL), (zero, zero))
    cur = stages(cur, A_PIPE_UNROLL - 1)
    consume(*cur, (A_PIPE_UNROLL - 1) & 1)

    lo = lax.broadcasted_iota(jnp.int32, (1, LANES), 1) < HALF

    def write_tile(qi, carry):
        outs = []
        for hh in range(2):
            acc = acc_sc[qi, hh]
            o_t = jnp.concatenate([acc / acc[A_V:A_V + 1, :], jnp.zeros((LANES - A_VT_ROWS, tq), F32)], axis=0)
            outs.append(o_t.T)
        o_ref[0, pl.ds(pl.multiple_of(qi * tq, tq), tq), :] = jnp.where(
            lo, outs[0], pltpu.roll(outs[1], HALF, 1)).astype(o_ref.dtype)
        return carry

    lax.fori_loop(0, n_tiles, write_tile, 0)


def _mla_prompt(qa, ka, vt, tq):
    bsz, s, _ = qa.shape
    pairs = A_HEADS // 2
    n_tiles = s // tq
    assert vt.shape[3] == tq and tq == A_MASK_CHUNKS * CHUNK
    q_chunk = np.arange(tq)[:, None] // CHUNK
    lane_chunk = np.arange(LANES)[None, :] - A_QK
    qmask = np.where((lane_chunk > q_chunk) & (lane_chunk < A_MASK_CHUNKS), NEG_INF, 0.0).astype(np.float32)
    return pl.pallas_call(
        functools.partial(_mla_prompt_body, tq=tq),
        grid=(bsz, pairs),
        in_specs=[pl.BlockSpec((1, s, 2 * LANES), lambda b, h: (b, 0, h)),
                  pl.BlockSpec((1, s, 2 * LANES), lambda b, h: (b, 0, h)),
                  pl.BlockSpec((1, n_tiles, 2 * A_VT_ROWS, tq), lambda b, h: (b, 0, h, 0)),
                  pl.BlockSpec((tq, LANES), lambda b, h: (0, 0))],
        out_specs=pl.BlockSpec((1, s, LANES), lambda b, h: (b, 0, h)),
        out_shape=jax.ShapeDtypeStruct((bsz, s, A_WIDTH), BF16),
        scratch_shapes=[pltpu.VMEM((n_tiles, 2, 1, tq), F32), pltpu.VMEM((n_tiles, 2, A_VT_ROWS, tq), F32),
                        pltpu.VMEM((2, 2, tq, tq), F32)],
        compiler_params=_params(2),
        name="mla_prompt",
    )(qa, ka, vt, jnp.asarray(qmask, BF16))


def _mla_sample_body(q_ref, cc_ref, ckr_ref, cn_ref, krn_ref, wkt_ref, wv_ref, sel_ref, o_ref,
                     qabs_sc, qr_sc, m_sc, acc_sc, *, tk):
    t = q_ref.shape[1]
    past = cc_ref.shape[1]
    for h in range(A_HEADS):
        qh = q_ref[0, :, h * LANES:(h + 1) * LANES]
        rows = slice(h * t, (h + 1) * t)
        qabs_sc[rows, :] = jnp.dot(qh, wkt_ref[h], preferred_element_type=F32).astype(BF16)
        qr_sc[rows, :] = jnp.dot(qh, sel_ref[...], preferred_element_type=F32).astype(BF16)
    m_sc[...] = jnp.full(m_sc.shape, NEG_INF, F32)
    acc_sc[...] = jnp.zeros(acc_sc.shape, F32)
    contract_last = (((1,), (1,)), ((), ()))

    def scores(c_t, kr_t):
        return (lax.dot_general(c_t.astype(BF16), qabs_sc[...], contract_last, preferred_element_type=F32)
                + lax.dot_general(kr_t.astype(BF16), qr_sc[...], contract_last, preferred_element_type=F32))

    def update(s, c_t):
        n_keys = c_t.shape[0]
        m_old = m_sc[...]
        m_new = jnp.maximum(m_old, jnp.max(s, axis=0, keepdims=True))
        p = jnp.exp2(s - m_new).astype(BF16)
        ct = jnp.concatenate([c_t.T.astype(BF16), jnp.ones((BF16_SUBLANES, n_keys), BF16)], axis=0)
        acc_sc[...] = jnp.exp2(m_old - m_new) * acc_sc[...] + jnp.dot(ct, p, preferred_element_type=F32)
        m_sc[...] = m_new

    def tile(j):
        rows = pl.ds(j * tk, tk)
        return cc_ref[0, rows, :], ckr_ref[0, rows, :]

    n_tiles = past // tk
    cur = tile(0)
    s = scores(*cur)
    for j in range(n_tiles):
        nxt = tile(j + 1) if j + 1 < n_tiles else (cn_ref[0], krn_ref[0])
        s_next = scores(*nxt)
        update(s, cur[0])
        cur, s = nxt, s_next
    update(s, cur[0])

    acc = acc_sc[...]
    o_lat = (acc[:A_KV_RANK] / acc[A_KV_RANK:A_KV_RANK + 1]).T.astype(BF16)
    out = jnp.zeros((t, A_WIDTH), F32)
    for h in range(A_HEADS):
        out = out + jnp.dot(o_lat[h * t:(h + 1) * t, :], wv_ref[h], preferred_element_type=F32)
    o_ref[0] = out.astype(o_ref.dtype)


def _mla_sample(qa, cache_c, cache_kr, c_new, kr_new, wkt, wv, sel, tk):
    bsz, t, _ = qa.shape
    past = cache_c.shape[1]

    def per_b(shape):
        return pl.BlockSpec((1,) + shape, lambda b: (b, 0, 0))

    def full(a):
        return pl.BlockSpec(a.shape, lambda b: (0,) * a.ndim)

    rows = A_HEADS * t
    return pl.pallas_call(
        functools.partial(_mla_sample_body, tk=tk),
        grid=(bsz,),
        in_specs=[per_b((t, A_HEADS * LANES)), per_b((past, A_KV_RANK)), per_b((past, A_ROPE)),
                  per_b((t, A_KV_RANK)), per_b((t, A_ROPE)), full(wkt), full(wv), full(sel)],
        out_specs=per_b((t, A_WIDTH)),
        out_shape=jax.ShapeDtypeStruct((bsz, t, A_WIDTH), BF16),
        scratch_shapes=[pltpu.VMEM((rows, A_KV_RANK), BF16), pltpu.VMEM((rows, A_ROPE), BF16),
                        pltpu.VMEM((1, rows), F32), pltpu.VMEM((A_KV_RANK + BF16_SUBLANES, rows), F32)],
        compiler_params=_params(1),
        name="mla_sample",
    )(qa, cache_c, cache_kr, c_new, kr_new, wkt, wv, sel)


def _band_body(*refs, n_blk, n_qblk, cpg, ones_row, has_sink, variants):
    refs = list(refs)
    q_ref, k_ref, vt_ref = refs[:3]
    o_ref, s_sc, bias_sc = refs[-3:]
    extra = refs[3:-3]
    sink_ref = extra.pop() if has_sink else None
    toep_ref = extra.pop() if extra else None
    win = n_blk * LANES
    rows_g = cpg * CHUNK
    n_var = len(variants)
    qpc = 2 * n_qblk * CHUNK
    lo = lax.broadcasted_iota(jnp.int32, (CHUNK, LANES), 1) < HALF
    contract_last = (((1,), (1,)), ((), ()))

    key_chunk = lax.broadcasted_iota(jnp.int32, (win, LANES), 0) // CHUNK
    for v, chunks in enumerate(variants):
        for cc, (x0, lo_chunk, hi_chunk) in enumerate(chunks):
            tile = toep_ref[0, x0:x0 + win, :] if toep_ref is not None else jnp.zeros((win, LANES), F32)
            tile = jnp.where((key_chunk >= lo_chunk) & (key_chunk <= hi_chunk), tile, NEG_INF)
            for j in range(qpc // LANES):
                bias_sc[v, :, cc * qpc + j * LANES:cc * qpc + (j + 1) * LANES] = tile

    def aligned(x, n):
        return x if isinstance(x, int) else pl.multiple_of(x, n)

    def scores(a, wb, variant, slot):
        pieces = []
        for cc in range(cpg):
            rows = pl.ds(aligned(a * rows_g + cc * CHUNK, CHUNK), CHUNK)
            for r in range(n_qblk):
                qblk = q_ref[0, rows, r * LANES:(r + 1) * LANES]
                zero = jnp.zeros_like(qblk)
                pieces += [jnp.where(lo, qblk, zero), jnp.where(lo, zero, qblk)]
        qs = jnp.concatenate(pieces, axis=0)
        kw = k_ref[0, pl.ds(aligned(wb * LANES, LANES), win), :]
        s_sc[slot] = lax.dot_general(kw, qs, contract_last, preferred_element_type=F32) + bias_sc[variant]

    def finish(a, wb, slot):
        s = s_sc[slot]
        m = jnp.max(s, axis=0, keepdims=True)
        if has_sink:
            sink = sink_ref[0]
            m = jnp.maximum(m, sink)
        p = jnp.exp2(s - m)
        vt = jnp.concatenate([vt_ref[0, 0, wb + i] for i in range(n_blk)], axis=1)
        if not ones_row:
            vt = jnp.concatenate([vt, jnp.ones((BF16_SUBLANES, win), BF16)], axis=0)
        o = jnp.dot(vt, p.astype(BF16), preferred_element_type=F32)
        l = o[CHUNK:CHUNK + 1, :] if ones_row else o[LANES:LANES + 1, :]
        o = o[:LANES]
        if has_sink:
            l = l + jnp.exp2(sink - m)
        o = (o / l).T
        idx = 0
        for cc in range(cpg):
            rows = pl.ds(aligned(a * rows_g + cc * CHUNK, CHUNK), CHUNK)
            for r in range(n_qblk):
                top = o[idx * CHUNK:(idx + 1) * CHUNK]
                bot = o[(idx + 1) * CHUNK:(idx + 2) * CHUNK]
                if ones_row:
                    bot = pltpu.roll(bot, HALF, 1)
                o_ref[0, rows, r * LANES:(r + 1) * LANES] = jnp.where(lo, top, bot).astype(o_ref.dtype)
                idx += 2

    n_total = q_ref.shape[1] // rows_g
    if n_total == 1:
        scores(0, 0, 0, 0)
        finish(0, 0, 0)
        return
    g_blk = rows_g // LANES
    n_lead = n_var - 1 + (n_var - 1) % 2
    assert (n_var - 1) * g_blk >= n_blk - g_blk and n_total - n_lead >= 2 and BAND_PIPE_UNROLL % 2 == 0

    def window_block(a):
        wb = (a + 1) * g_blk - n_blk
        return max(wb, 0) if isinstance(a, int) else wb

    def stage(a, slot):
        nxt = a + 1
        scores(nxt, window_block(nxt), min(nxt, n_var - 1) if isinstance(nxt, int) else n_var - 1, 1 - slot)
        finish(a, window_block(a), slot)

    scores(0, window_block(0), 0, 0)
    for a in range(n_lead):
        stage(a, a & 1)

    def trip(i, carry):
        for st in range(BAND_PIPE_UNROLL):
            stage(n_lead + BAND_PIPE_UNROLL * i + st, st & 1)
        return carry

    n_stages = n_total - 1 - n_lead
    lax.fori_loop(0, n_stages // BAND_PIPE_UNROLL, trip, 0)
    for a in range(n_total - 1 - n_stages % BAND_PIPE_UNROLL, n_total - 1):
        stage(a, (a - n_lead) & 1)
    finish(n_total - 1, window_block(n_total - 1), (n_total - 1 - n_lead) & 1)


def _band_attention(q, k, vt, toep, sink, *, variants, n_blk, n_groups, n_qblk, cpg, ones_row):
    bsz, s, _ = q.shape
    sk = k.shape[1]
    qw = n_qblk * LANES
    in_specs = [pl.BlockSpec((1, s, qw), lambda b, g: (b, 0, g)),
                pl.BlockSpec((1, sk, LANES), lambda b, g: (b, 0, g)),
                pl.BlockSpec((1, 1) + vt.shape[2:], lambda b, g: (b, g, 0, 0, 0))]
    args = [q, k, vt]
    for extra in (toep, sink):
        if extra is not None:
            in_specs.append(pl.BlockSpec((1,) + extra.shape[1:], lambda b, g: (g, 0, 0)))
            args.append(extra)
    bias_shape = (n_blk * LANES, cpg * 2 * n_qblk * CHUNK)
    body = functools.partial(_band_body, n_blk=n_blk, n_qblk=n_qblk, cpg=cpg, ones_row=ones_row,
                             has_sink=sink is not None, variants=variants)
    return pl.pallas_call(
        body,
        grid=(bsz, n_groups),
        in_specs=in_specs,
        out_specs=pl.BlockSpec((1, s, qw), lambda b, g: (b, 0, g)),
        out_shape=jax.ShapeDtypeStruct(q.shape, BF16),
        scratch_shapes=[pltpu.VMEM((2,) + bias_shape, F32), pltpu.VMEM((len(variants),) + bias_shape, F32)],
        compiler_params=_params(2),
        name="band_attention",
    )(*args)


def _rope_tables(pos, rot, lane_pattern, from_zero=False):
    half = rot // 2
    inv = jnp.power(ROPE_THETA, -jnp.arange(half, dtype=F32) * 2.0 / rot)
    inv_lane, first, second = [], [], []
    for kind, width in lane_pattern:
        if kind == "rot":
            inv_lane += [inv, inv]
            first += [1.0] * half + [0.0] * half
            second += [0.0] * half + [1.0] * half
        else:
            inv_lane.append(jnp.zeros((width,), F32))
            first += [0.0] * width
            second += [0.0] * width
    inv_lane = jnp.concatenate(inv_lane)[None, :]
    n = pos.shape[0]
    if from_zero and n % CHUNK == 0:
        a_hi = (jnp.arange(n // CHUNK, dtype=F32) * CHUNK)[:, None] * inv_lane
        a_lo = jnp.arange(CHUNK, dtype=F32)[:, None] * inv_lane
        c_hi, s_hi = jnp.cos(a_hi)[:, None, :], jnp.sin(a_hi)[:, None, :]
        c_lo, s_lo = jnp.cos(a_lo)[None], jnp.sin(a_lo)[None]
        cos = (c_hi * c_lo - s_hi * s_lo).reshape(n, LANES)
        sin = (s_hi * c_lo + c_hi * s_lo).reshape(n, LANES)
    else:
        ang = pos.astype(F32)[:, None] * inv_lane
        cos, sin = jnp.cos(ang), jnp.sin(ang)
    return jnp.stack([cos, sin * np.asarray(second, np.float32), -sin * np.asarray(first, np.float32)])


A_ROPE_PATTERN = (("pad", A_NOPE), ("rot", A_ROPE), ("pad", LANES - A_QK))
C_ROPE_PATTERN = (("rot", C_ROT), ("pad", HALF - C_ROT)) * 2


def _prep_ab(pre, post, w_in, q_norm, kv_norm, w_uq, w_ukv, rel_bias, w_out):
    d = w_in.shape[0]
    q_lat, c_kv, k_r, g_a, q_b, k_b, v_b, g_b = jnp.split(
        w_in, [384, 640, 672, 1184, 1696, 2208, 2720], axis=1)
    kr_blk = jnp.concatenate([jnp.zeros((d, A_NOPE), F32), k_r, jnp.zeros((d, LANES - A_QK), F32)], axis=1)
    w_in_p = jnp.concatenate([q_lat, c_kv, kr_blk, g_a, g_b, q_b, k_b, v_b], axis=1).astype(BF16)
    w_uq_p = jnp.pad(w_uq.reshape(A_Q_RANK, A_HEADS, A_QK), ((0, 0), (0, 0), (0, LANES - A_QK)))
    w_uq_p = w_uq_p.reshape(A_Q_RANK, A_HEADS * LANES).astype(BF16)
    ukv = w_ukv.reshape(A_KV_RANK, A_HEADS, A_NOPE + A_V)
    w_uk, w_uv = ukv[..., :A_NOPE], ukv[..., A_NOPE:]
    pad_half = ((0, 0), (0, 0), (0, LANES - A_NOPE))
    w_k = jnp.pad(w_uk, pad_half).reshape(A_KV_RANK, A_HEADS * LANES).astype(BF16)
    w_vt = jnp.pad(w_uv, ((0, 0), (0, 0), (0, A_VT_ROWS - A_V))).reshape(A_KV_RANK, A_HEADS * A_VT_ROWS).T.astype(BF16)
    vt_ones = jnp.asarray((np.arange(A_HEADS * A_VT_ROWS) % A_VT_ROWS == A_V).astype(np.float32))[:, None]
    wkt = jnp.pad(jnp.transpose(w_uk, (1, 2, 0)), ((0, 0), (0, LANES - A_NOPE), (0, 0))).astype(BF16)
    eye = jnp.eye(A_HEADS, dtype=F32)
    wv_s = (jnp.transpose(w_uv, (1, 0, 2))[:, :, None, :] * eye[:, None, :, None]).reshape(
        A_HEADS, A_KV_RANK, A_WIDTH).astype(BF16)
    sel = (jnp.arange(LANES)[:, None] == A_NOPE + jnp.arange(A_ROPE)[None, :]).astype(BF16)
    rel_bias = rel_bias * LOG2E
    win_p, win_s = B_WIN_BLOCKS * LANES, B_SAMPLE_BLOCKS * LANES
    r0 = win_p - CHUNK
    x_len = r0 + win_p
    n_vec = x_len + CHUNK
    n_hi = r0 + CHUNK - 1 - B_MAX_REL
    n_lo = n_vec - n_hi - (2 * B_MAX_REL + 1)
    vec = jnp.concatenate([jnp.broadcast_to(rel_bias[:, -1:], (B_HEADS, n_hi)), rel_bias[:, ::-1],
                           jnp.broadcast_to(rel_bias[:, :1], (B_HEADS, n_lo))], axis=1)
    skew = jnp.tile(vec, (1, CHUNK))[:, :CHUNK * (n_vec - 1)].reshape(B_HEADS, CHUNK, n_vec - 1)
    toep = skew[:, :, CHUNK - 1:CHUNK - 1 + x_len]
    toep = jnp.transpose(toep.reshape(B_HEADS // 2, 2, CHUNK, x_len), (0, 3, 1, 2)).reshape(B_HEADS // 2, x_len, LANES)
    var_p = tuple(tuple((r0 - c * CHUNK, c - B_PAST_CHUNKS, c)
                        for c in range(B_GROUP_CHUNKS * v, B_GROUP_CHUNKS * (v + 1))) for v in range(B_VARIANTS))
    var_s = (((r0 - (win_s - CHUNK), 1, B_PAST_CHUNKS + 1),),)
    return dict(pre=pre[None], w_in=w_in_p, q_norm=q_norm[None], kv_norm=kv_norm[None], w_uq=w_uq_p, w_k=w_k,
                w_vt=w_vt, vt_ones=vt_ones, wkt=wkt, wv_s=wv_s, sel=sel, toep=toep, var_p=var_p, var_s=var_s,
                w_out=w_out.astype(BF16), post=post[None])


def _prep_c(pre, post, w_in, sinks, w_out):
    q, k, v, g = jnp.split(w_in, [1024, 1152, 1280], axis=1)
    k0, k1 = k[:, :C_HEAD_DIM], k[:, C_HEAD_DIM:]
    v0, v1 = v[:, :C_HEAD_DIM], v[:, C_HEAD_DIM:]
    zero = jnp.zeros_like(v0)
    w_in_p = jnp.concatenate([q, k0, k0, k1, k1, v0, zero, v1, zero, g], axis=1).astype(BF16)
    var_p = tuple(tuple((0, c - C_PAST_CHUNKS, c) for c in range(C_GROUP_CHUNKS * v, C_GROUP_CHUNKS * (v + 1)))
                  for v in range(C_VARIANTS))
    var_s = (((0, 1, C_PAST_CHUNKS + 1),),)
    sink_row = jnp.repeat((sinks * LOG2E).reshape(C_KV_HEADS, C_GROUP), CHUNK, axis=1)[:, None, :]
    return dict(pre=pre[None], w_in=w_in_p, var_p=var_p, var_s=var_s, sink_s=sink_row,
                sink_p=jnp.tile(sink_row, (1, 1, C_GROUP_CHUNKS)), w_out=w_out.astype(BF16), post=post[None])


def _dup_heads(x):
    return jnp.concatenate([x[:, :, 0], x[:, :, 0], x[:, :, 1], x[:, :, 1]], axis=-1)


def kernel(x_prompt, x_sample, cache_a_ckv, cache_a_krope, cache_b_k, cache_b_v, cache_c_k, cache_c_v,
           ab_pre_norm, ab_post_norm, ab_w_in, ab_q_norm, ab_kv_norm, ab_w_uq, ab_w_ukv, ab_rel_bias, ab_w_out,
           c_pre_norm, c_post_norm, c_w_in, c_sinks, c_w_out):
    bsz, seq, _ = x_prompt.shape
    dbs, dseq, _ = x_sample.shape
    past = cache_a_ckv.shape[2]
    n_s = dbs * dseq
    pos_p = jnp.arange(seq, dtype=jnp.int32)
    pos_s = jnp.tile(past + jnp.arange(dseq, dtype=jnp.int32), dbs)
    wab = _prep_ab(ab_pre_norm[0], ab_post_norm[0], ab_w_in[0], ab_q_norm[0], ab_kv_norm[0], ab_w_uq[0],
                   ab_w_ukv[0], ab_rel_bias[0], ab_w_out[0])
    wc = _prep_c(c_pre_norm[0], c_post_norm[0], c_w_in[0], c_sinks[0], c_w_out[0])
    b_tail = min(B_PAST_CHUNKS * CHUNK, seq)
    c_tail = min(C_WINDOW, seq)
    tile = 512

    rope_a_p = _rope_tables(pos_p, A_ROPE, A_ROPE_PATTERN, from_zero=True)
    (qa, ka, vt, qb, kb, vbt, sg, c_new_p, kr_new_p, kb_tail, vb_tail) = _ab_in_proj(
        x_prompt, rope_a_p, wab, tm=256, tail_len=b_tail, tkv=tile)
    o_a = _mla_prompt(qa, ka, vt, tq=tile)
    o_b = _band_attention(qb, kb, vbt, wab["toep"], None, variants=wab["var_p"], n_blk=B_WIN_BLOCKS,
                          n_groups=B_HEADS // 2, n_qblk=1, cpg=B_GROUP_CHUNKS, ones_row=False)

    rope_a_s = _rope_tables(pos_s, A_ROPE, A_ROPE_PATTERN)
    xs = x_sample.reshape(1, n_s, D_MODEL)
    (qa_s, _, _, qb_s, _, _, sg_s, c_new_s, kr_new_s, kb_s32, vb_s32) = _ab_in_proj(
        xs, rope_a_s, wab, tm=n_s, tail_len=n_s, tkv=n_s)
    o_a_s = _mla_sample(qa_s.reshape(dbs, dseq, -1), cache_a_ckv[0], cache_a_krope[0],
                        c_new_s.reshape(dbs, dseq, -1), kr_new_s.reshape(dbs, dseq, -1),
                        wab["wkt"], wab["wv_s"], wab["sel"], tk=512)
    wb = cache_b_k.shape[2]
    pad_b = jnp.zeros((dbs, B_SAMPLE_BLOCKS * LANES - wb - dseq, B_WIDTH), F32)
    kband = jnp.concatenate([pad_b, cache_b_k[0].reshape(dbs, wb, B_WIDTH), kb_s32.reshape(dbs, dseq, -1)], 1)
    vband = jnp.concatenate([pad_b, cache_b_v[0].reshape(dbs, wb, B_WIDTH), vb_s32.reshape(dbs, dseq, -1)], 1)
    vband_t = jnp.transpose(vband.astype(BF16).reshape(dbs, B_SAMPLE_BLOCKS, LANES, B_HEADS // 2, LANES), (0, 3, 1, 4, 2))
    o_b_s = _band_attention(qb_s.reshape(dbs, dseq, -1), kband.astype(BF16), vband_t, wab["toep"], None,
                            variants=wab["var_s"], n_blk=B_SAMPLE_BLOCKS, n_groups=B_HEADS // 2, n_qblk=1, cpg=1,
                            ones_row=False)

    rope_c_p = _rope_tables(pos_p, C_ROT, C_ROPE_PATTERN, from_zero=True)
    h1_p, qc, kc, vct, sgc, kc_tail, vc_tail = _c_in_proj(
        [o_a, o_b], sg, wab["w_out"], wab["post"], x_prompt, rope_c_p, wc, tm=tile, tail_len=c_tail)
    o_c = _band_attention(qc, kc, vct, None, wc["sink_p"], variants=wc["var_p"], n_blk=C_WIN_BLOCKS,
                          n_groups=C_KV_HEADS, n_qblk=C_GROUP // 2, cpg=C_GROUP_CHUNKS, ones_row=True)
    h2_p = _out_proj([o_c], sgc, wc["w_out"], wc["post"], h1_p, tm=tile)

    rope_c_s = _rope_tables(pos_s, C_ROT, C_ROPE_PATTERN)
    h1_s, qc_s, kc_s, _, sgc_s, kc_s32, vc_s32 = _c_in_proj(
        [o_a_s.reshape(1, n_s, -1), o_b_s.reshape(1, n_s, -1)], sg_s, wab["w_out"], wab["post"], xs, rope_c_s, wc,
        tm=n_s, tail_len=n_s)
    wcw = cache_c_k.shape[2]
    win_c = C_SAMPLE_BLOCKS * LANES
    n_pad = win_c - wcw - dseq
    kcb = jnp.concatenate([jnp.zeros((dbs, n_pad, C_KV_HEADS * LANES), BF16), _dup_heads(cache_c_k[0]).astype(BF16),
                           kc_s.reshape(dbs, dseq, -1)], axis=1)
    vcb = jnp.concatenate([jnp.zeros((dbs, n_pad, C_KV_HEADS, C_HEAD_DIM), F32), cache_c_v[0],
                           vc_s32.reshape(dbs, dseq, C_KV_HEADS, C_HEAD_DIM)], axis=1)
    vcb_t = jnp.concatenate([jnp.transpose(vcb, (0, 2, 3, 1)), jnp.ones((dbs, C_KV_HEADS, 1, win_c), F32),
                             jnp.zeros((dbs, C_KV_HEADS, LANES - C_HEAD_DIM - 1, win_c), F32)], axis=2)
    vcb_t = jnp.transpose(vcb_t.astype(BF16).reshape(dbs, C_KV_HEADS, LANES, C_SAMPLE_BLOCKS, LANES), (0, 1, 3, 2, 4))
    o_c_s = _band_attention(qc_s.reshape(dbs, dseq, -1), kcb, vcb_t, None, wc["sink_s"], variants=wc["var_s"],
                            n_blk=C_SAMPLE_BLOCKS, n_groups=C_KV_HEADS, n_qblk=C_GROUP // 2, cpg=1, ones_row=True)
    h2_s = _out_proj([o_c_s.reshape(1, n_s, -1)], sgc_s, wc["w_out"], wc["post"], h1_s, tm=n_s)

    def roll_in(buf, new):
        return jnp.concatenate([buf, new], axis=1)[:, -buf.shape[1]:][None]

    return (h2_p, h2_s.reshape(dbs, dseq, D_MODEL),
            c_new_p[None], kr_new_p[None],
            kb_tail.reshape(1, bsz, b_tail, B_HEADS, B_HEAD_DIM), vb_tail.reshape(1, bsz, b_tail, B_HEADS, B_HEAD_DIM),
            kc_tail.reshape(1, bsz, c_tail, C_KV_HEADS, C_HEAD_DIM), vc_tail.reshape(1, bsz, c_tail, C_KV_HEADS, C_HEAD_DIM),
            c_new_s.reshape(1, dbs, dseq, A_KV_RANK), kr_new_s.reshape(1, dbs, dseq, A_ROPE),
            roll_in(cache_b_k[0], kb_s32.reshape(dbs, dseq, B_HEADS, B_HEAD_DIM)),
            roll_in(cache_b_v[0], vb_s32.reshape(dbs, dseq, B_HEADS, B_HEAD_DIM)),
            roll_in(cache_c_k[0], kc_s32.reshape(dbs, dseq, C_KV_HEADS, C_HEAD_DIM)),
            roll_in(cache_c_v[0], vc_s32.reshape(dbs, dseq, C_KV_HEADS, C_HEAD_DIM)))
```

```python
import functools

import jax
import jax.numpy as jnp
import numpy as np
from jax import lax
from jax.experimental import pallas as pl
from jax.experimental.pallas import tpu as pltpu

F32 = jnp.float32
BF16 = jnp.bfloat16

D_MODEL = 1024
CHUNK = 64
ROPE_THETA = 500000.0
RMS_EPS = 1e-6
NEG_INF = -1e30

A_HEADS = 8
A_NOPE = 64
A_ROPE = 32
A_QK = A_NOPE + A_ROPE
A_V = 64
A_Q_RANK = 384
A_KV_RANK = 256
A_WIDTH = A_HEADS * A_V
LOG2E = 1.4426950408889634
A_SCALE = A_QK ** -0.5 * LOG2E

B_HEADS = 8
B_HEAD_DIM = 64
B_WIDTH = B_HEADS * B_HEAD_DIM
B_PAST_CHUNKS = 8
B_MAX_REL = 128
B_SCALE = B_HEAD_DIM ** -0.5 * LOG2E

C_HEADS = 16
C_KV_HEADS = 2
C_GROUP = C_HEADS // C_KV_HEADS
C_HEAD_DIM = 64
C_WIDTH = C_HEADS * C_HEAD_DIM
C_WINDOW = 128
C_PAST_CHUNKS = C_WINDOW // CHUNK
C_ROT = C_HEAD_DIM // 4
C_SCALE = C_HEAD_DIM ** -0.5 * LOG2E

LANES = 128
HALF = LANES // 2
BF16_SUBLANES = 16
VMEM_LIMIT = 56 * 1024 * 1024
A_MASK_CHUNKS = 8
assert A_QK + A_MASK_CHUNKS <= LANES
A_PIPE_UNROLL = 4
BAND_PIPE_UNROLL = 4
B_GROUP_CHUNKS = 4
C_GROUP_CHUNKS = 2
B_WIN_BLOCKS = (B_PAST_CHUNKS + B_GROUP_CHUNKS) * CHUNK // LANES
C_WIN_BLOCKS = (C_PAST_CHUNKS + C_GROUP_CHUNKS) * CHUNK // LANES
B_VARIANTS = B_PAST_CHUNKS // B_GROUP_CHUNKS + 1
C_VARIANTS = C_PAST_CHUNKS // C_GROUP_CHUNKS + 1
B_SAMPLE_BLOCKS =(B_PAST_CHUNKS + 2) * CHUNK // LANES
C_SAMPLE_BLOCKS = (C_PAST_CHUNKS + 2) * CHUNK // LANES

AB_Q0, AB_C0, AB_KR0, AB_G0, AB_QB0, AB_KB0, AB_VB0, AB_NZ = 0, 384, 640, 768, 1792, 2304, 2816, 3328
C_Q0, C_K0, C_V0, C_G0, C_NZ = 0, 1024, 1280, 1536, 2560


def _params(n_axes):
    return pltpu.CompilerParams(dimension_semantics=("arbitrary",) * n_axes, vmem_limit_bytes=VMEM_LIMIT)


def _rms(x, g):
    return x * lax.rsqrt(jnp.mean(x * x, axis=-1, keepdims=True) + RMS_EPS) * g


def _rope_block(blk, rope_ref, shift):
    return (blk * rope_ref[0] + pltpu.roll(blk, shift, 1) * rope_ref[1]
            + pltpu.roll(blk, LANES - shift, 1) * rope_ref[2])


def _tail_spec(n_tiles, tm, tail_len, width):
    tb = min(tail_len, tm)
    n_blk = tail_len // tb
    return tb, pl.BlockSpec((1, tb, width), lambda b, i: (b, jnp.maximum(i - (n_tiles - n_blk), 0), 0))


def _ab_in_body(x_ref, pre_ref, w_ref, qn_ref, kvn_ref, wuq_ref, wk_ref, wvt_ref, rope_ref,
                qa_ref, ka_ref, vt_ref, qb_ref, kb_ref, vbt_ref, sg_ref, c_ref, kr_ref, kb_tail_ref, vb_tail_ref):
    tm = x_ref.shape[1]
    tb = kb_tail_ref.shape[1]
    xn = _rms(x_ref[0], pre_ref[...]).astype(BF16)
    z = jnp.dot(xn, w_ref[...], preferred_element_type=F32)

    qn = _rms(z[:, AB_Q0:AB_C0], qn_ref[...]).astype(BF16)
    qa = jnp.dot(qn, wuq_ref[...], preferred_element_type=F32) * A_SCALE
    for h in range(A_HEADS):
        blk = slice(h * LANES, (h + 1) * LANES)
        qa_ref[0, :, blk] = _rope_block(qa[:, blk], rope_ref, A_ROPE // 2).astype(BF16)

    c_new = _rms(z[:, AB_C0:AB_KR0], kvn_ref[...])
    c_ref[0] = c_new
    cb = c_new.astype(BF16)
    krot = _rope_block(z[:, AB_KR0:AB_G0], rope_ref, A_ROPE // 2)
    kr_ref[0] = krot[:, A_NOPE:A_NOPE + A_ROPE]
    kn = jnp.dot(cb, wk_ref[...], preferred_element_type=F32)
    row = pl.program_id(1) * tm + lax.broadcasted_iota(jnp.int32, (tm, LANES), 0)
    lane = lax.broadcasted_iota(jnp.int32, (tm, LANES), 1)
    k_shared = krot + jnp.where(lane - A_QK == (row // CHUNK) % A_MASK_CHUNKS, 1.0, 0.0)
    for h in range(A_HEADS):
        blk = slice(h * LANES, (h + 1) * LANES)
        ka_ref[0, :, blk] = (kn[:, blk] + k_shared).astype(BF16)
    vt = jnp.dot(wvt_ref[...], c_new.T.astype(BF16), preferred_element_type=F32)
    row = lax.broadcasted_iota(jnp.int32, vt.shape, 0)
    vt_ref[0, 0] = jnp.where((row & (LANES - 1)) == A_V, 1.0, vt).astype(BF16)

    g = z[:, AB_G0:AB_QB0]
    sg_ref[0] = (g * jax.nn.sigmoid(g)).astype(BF16)
    qb_ref[0] = (z[:, AB_QB0:AB_KB0] * B_SCALE).astype(BF16)
    kb = z[:, AB_KB0:AB_VB0]
    vb = z[:, AB_VB0:AB_NZ]
    kb_ref[0] = kb.astype(BF16)
    vbt = vb.T.astype(BF16)
    for hp in range(B_HEADS // 2):
        for j in range(tm // LANES):
            vbt_ref[0, hp, j] = vbt[hp * LANES:(hp + 1) * LANES, j * LANES:(j + 1) * LANES]
    kb_tail_ref[0] = kb[tm - tb:, :]
    vb_tail_ref[0] = vb[tm - tb:, :]


def _ab_in_proj(x, rope, w, tm, tail_len, tkv):
    bsz, s, _ = x.shape
    nt = s // tm
    per_kv = tkv // tm
    tb, tail_spec = _tail_spec(nt, tm, tail_len, B_WIDTH)

    def full(a):
        return pl.BlockSpec(a.shape, lambda b, i: (0,) * a.ndim)

    def rows(width):
        return pl.BlockSpec((1, tm, width), lambda b, i: (b, i, 0))

    weights = (w["pre"], w["w_in"], w["q_norm"], w["kv_norm"], w["w_uq"], w["w_k"], w["w_vt"])
    vt_spec = pl.BlockSpec((1, 1, A_HEADS * LANES, tm), lambda b, i: (b, i // per_kv, 0, i % per_kv))
    out_shape = (
        jax.ShapeDtypeStruct((bsz, s, A_HEADS * LANES), BF16),
        jax.ShapeDtypeStruct((bsz, s, A_HEADS * LANES), BF16),
        jax.ShapeDtypeStruct((bsz, s // tkv, A_HEADS * LANES, tkv), BF16),
        jax.ShapeDtypeStruct((bsz, s, B_WIDTH), BF16),
        jax.ShapeDtypeStruct((bsz, s, B_WIDTH), BF16),
        jax.ShapeDtypeStruct((bsz, B_HEADS // 2, s // LANES, LANES, LANES), BF16),
        jax.ShapeDtypeStruct((bsz, s, A_WIDTH + B_WIDTH), BF16),
        jax.ShapeDtypeStruct((bsz, s, A_KV_RANK), F32),
        jax.ShapeDtypeStruct((bsz, s, A_ROPE), F32),
        jax.ShapeDtypeStruct((bsz, tail_len, B_WIDTH), F32),
        jax.ShapeDtypeStruct((bsz, tail_len, B_WIDTH), F32),
    )
    vbt_spec = pl.BlockSpec((1, B_HEADS // 2, tm // LANES, LANES, LANES), lambda b, i: (b, 0, i, 0, 0))
    out_specs = (rows(1024), rows(1024), vt_spec, rows(512), rows(512), vbt_spec, rows(1024),
                 rows(A_KV_RANK), rows(A_ROPE), tail_spec, tail_spec)
    return pl.pallas_call(
        _ab_in_body,
        grid=(bsz, nt),
        in_specs=[rows(D_MODEL)] + [full(a) for a in weights]
        + [pl.BlockSpec((3, tm, LANES), lambda b, i: (0, i, 0))],
        out_specs=out_specs,
        out_shape=out_shape,
        compiler_params=_params(2),
        name="ab_in_proj",
    )(x, *weights, rope)


def _mix_out(o_refs, sg_ref, w_ref, g_ref, h_ref):
    o = jnp.concatenate([r[0].astype(F32) for r in o_refs], axis=-1) if len(o_refs) > 1 else o_refs[0][0].astype(F32)
    mixed = (o * sg_ref[0].astype(F32)).astype(BF16)
    y = jnp.dot(mixed, w_ref[...], preferred_element_type=F32)
    return h_ref[0] + _rms(y, g_ref[...])


def _c_in_body(*refs):
    n_o = len(refs) - 14
    o_refs = refs[:n_o]
    (sg0_ref, w0_ref, post0_ref, x_ref, pre_ref, w_ref, rope_ref,
     h_ref, q_ref, k_ref, vt_ref, sg_ref, k_tail_ref, v_tail_ref) = refs[n_o:]
    tm = x_ref.shape[1]
    tb = k_tail_ref.shape[1]
    h = _mix_out(o_refs, sg0_ref, w0_ref, post0_ref, x_ref)
    h_ref[0] = h
    xn = _rms(h, pre_ref[...]).astype(BF16)
    z = jnp.dot(xn, w_ref[...], preferred_element_type=F32)
    half_rot = C_ROT // 2
    for j in range(C_WIDTH // LANES):
        blk = slice(C_Q0 + j * LANES, C_Q0 + (j + 1) * LANES)
        q_ref[0, :, j * LANES:(j + 1) * LANES] = (_rope_block(z[:, blk], rope_ref, half_rot) * C_SCALE).astype(BF16)
    kd = [_rope_block(z[:, C_K0 + j * LANES:C_K0 + (j + 1) * LANES], rope_ref, half_rot) for j in range(C_KV_HEADS)]
    vz = [z[:, C_V0 + j * LANES:C_V0 + (j + 1) * LANES] for j in range(C_KV_HEADS)]
    row = lax.broadcasted_iota(jnp.int32, (LANES, tm), 0)
    for j in range(C_KV_HEADS):
        k_ref[0, :, j * LANES:(j + 1) * LANES] = kd[j].astype(BF16)
        vt = jnp.where(row == C_HEAD_DIM, 1.0, vz[j].T).astype(BF16)
        for i in range(tm // LANES):
            vt_ref[0, j, i] = vt[:, i * LANES:(i + 1) * LANES]
    g = z[:, C_G0:C_NZ]
    sg_ref[0] = (g * jax.nn.sigmoid(g)).astype(BF16)
    lo = lax.broadcasted_iota(jnp.int32, (1, LANES), 1) < HALF
    k_tail_ref[0] = jnp.where(lo, kd[0], kd[1])[tm - tb:, :]
    v_tail_ref[0] = (vz[0] + pltpu.roll(vz[1], HALF, 1))[tm - tb:, :]


def _c_in_proj(o_parts, sg0, w_out0, post0, x, rope, w, tm, tail_len):
    bsz, s, _ = x.shape
    nt = s // tm
    tb, tail_spec = _tail_spec(nt, tm, tail_len, LANES)

    def full(a):
        return pl.BlockSpec(a.shape, lambda b, i: (0,) * a.ndim)

    def rows(width):
        return pl.BlockSpec((1, tm, width), lambda b, i: (b, i, 0))

    out_shape = (
        jax.ShapeDtypeStruct((bsz, s, D_MODEL), F32),
        jax.ShapeDtypeStruct((bsz, s, C_WIDTH), BF16),
        jax.ShapeDtypeStruct((bsz, s, C_KV_HEADS * LANES), BF16),
        jax.ShapeDtypeStruct((bsz, C_KV_HEADS, s // LANES, LANES, LANES), BF16),
        jax.ShapeDtypeStruct((bsz, s, C_WIDTH), BF16),
        jax.ShapeDtypeStruct((bsz, tail_len, LANES), F32),
        jax.ShapeDtypeStruct((bsz, tail_len, LANES), F32),
    )
    return pl.pallas_call(
        _c_in_body,
        grid=(bsz, nt),
        in_specs=[rows(o.shape[-1]) for o in o_parts]
        + [rows(D_MODEL), full(w_out0), full(post0), rows(D_MODEL), full(w["pre"]), full(w["w_in"]),
           pl.BlockSpec((3, tm, LANES), lambda b, i: (0, i, 0))],
        out_specs=(rows(D_MODEL), rows(C_WIDTH), rows(256),
                   pl.BlockSpec((1, C_KV_HEADS, tm // LANES, LANES, LANES), lambda b, i: (b, 0, i, 0, 0)),
                   rows(C_WIDTH), tail_spec, tail_spec),
        out_shape=out_shape,
        compiler_params=_params(2),
        name="c_in_proj",
    )(*o_parts, sg0, w_out0, post0, x, w["pre"], w["w_in"], rope)


def _out_body(*refs):
    o_refs, (sg_ref, w_ref, g_ref, h_ref, out_ref) = refs[:-5], refs[-5:]
    out_ref[0] = _mix_out(o_refs, sg_ref, w_ref, g_ref, h_ref)


def _out_proj(o_parts, sg, w_out, post_g, h, tm):
    bsz, s, _ = h.shape

    def rows(width):
        return pl.BlockSpec((1, tm, width), lambda b, i: (b, i, 0))

    def full(a):
        return pl.BlockSpec(a.shape, lambda b, i: (0,) * a.ndim)

    return pl.pallas_call(
        _out_body,
        grid=(bsz, s // tm),
        in_specs=[rows(o.shape[-1]) for o in o_parts] + [rows(D_MODEL), full(w_out), full(post_g), rows(D_MODEL)],
        out_specs=rows(D_MODEL),
        out_shape=jax.ShapeDtypeStruct(h.shape, F32),
        compiler_params=_params(2),
        name="out_proj",
    )(*o_parts, sg, w_out, post_g, h)


def _mla_prompt_body(q_ref, k_ref, vt_ref, qmask_ref, o_ref, m_sc, acc_sc, s_sc, *, tq):
    n_tiles = q_ref.shape[1] // tq
    n_pairs = n_tiles * (n_tiles + 1) // 2
    assert A_PIPE_UNROLL % 2 == 0 and n_pairs % A_PIPE_UNROLL == 0 and n_pairs >= 2 * A_PIPE_UNROLL
    m_sc[...] = jnp.full(m_sc.shape, NEG_INF, F32)
    acc_sc[...] = jnp.zeros(acc_sc.shape, F32)
    contract_last = (((1,), (1,)), ((), ()))

    def scores(qi, j, slot):
        q_start = pl.multiple_of(qi * tq, tq)
        k_start = pl.multiple_of(j * tq, tq)
        qmask = jnp.where(j == qi, qmask_ref[...], jnp.zeros_like(qmask_ref[...]))
        for hh in range(2):
            blk = slice(hh * LANES, (hh + 1) * LANES)
            q = q_ref[0, pl.ds(q_start, tq), blk] + qmask
            s_sc[slot, hh] = lax.dot_general(k_ref[0, pl.ds(k_start, tq), blk], q, contract_last,
                                             preferred_element_type=F32)

    def consume(qi, j, slot):
        for hh in range(2):
            s = s_sc[slot, hh]
            vt = vt_ref[0, j, hh * LANES:(hh + 1) * LANES, :]
            m_old = m_sc[qi, hh]
            m_new = jnp.maximum(m_old, jnp.max(s, axis=0, keepdims=True))
            p = jnp.exp2(s - m_new).astype(BF16)
            acc_sc[qi, hh] = jnp.exp2(m_old - m_new) * acc_sc[qi, hh] + jnp.dot(vt, p, preferred_element_type=F32)
            m_sc[qi, hh] = m_new

    def advance(qi, j):
        last = j == qi
        return jnp.where(last, qi + 1, qi), jnp.where(last, 0, j + 1)

    def stages(cur, count):
        for st in range(count):
            nxt = advance(*cur)
            scores(*nxt, 1 - (st & 1))
            consume(*cur, st & 1)
            cur = nxt
        return cur

    zero = jnp.int32(0)
    scores(zero, zero, 0)
    cur = lax.fori_loop(0, n_pairs // A_PIPE_UNROLL - 1, lambda i, c: stages(c, A_PIPE_UNROLL), (zero, zero))
    cur = stages(cur, A_PIPE_UNROLL - 1)
    consume(*cur, (A_PIPE_UNROLL - 1) & 1)

    lo = lax.broadcasted_iota(jnp.int32, (1, LANES), 1) < HALF

    def write_tile(qi, carry):
        outs = []
        for hh in range(2):
            acc = acc_sc[qi, hh]
            outs.append((acc / acc[A_V:A_V + 1, :]).T)
        o_ref[0, pl.ds(pl.multiple_of(qi * tq, tq), tq), :] = jnp.where(
            lo, outs[0], pltpu.roll(outs[1], HALF, 1)).astype(o_ref.dtype)
        return carry

    lax.fori_loop(0, n_tiles, write_tile, 0)


def _mla_prompt(qa, ka, vt, tq):
    bsz, s, _ = qa.shape
    pairs = A_HEADS // 2
    n_tiles = s // tq
    assert vt.shape[3] == tq and tq == A_MASK_CHUNKS * CHUNK
    q_chunk = np.arange(tq)[:, None] // CHUNK
    lane_chunk = np.arange(LANES)[None, :] - A_QK
    qmask = np.where((lane_chunk > q_chunk) & (lane_chunk < A_MASK_CHUNKS), NEG_INF, 0.0).astype(np.float32)
    return pl.pallas_call(
        functools.partial(_mla_prompt_body, tq=tq),
        grid=(bsz, pairs),
        in_specs=[pl.BlockSpec((1, s, 2 * LANES), lambda b, h: (b, 0, h)),
                  pl.BlockSpec((1, s, 2 * LANES), lambda b, h: (b, 0, h)),
                  pl.BlockSpec((1, n_tiles, 2 * LANES, tq), lambda b, h: (b, 0, h, 0)),
                  pl.BlockSpec((tq, LANES), lambda b, h: (0, 0))],
        out_specs=pl.BlockSpec((1, s, LANES), lambda b, h: (b, 0, h)),
        out_shape=jax.ShapeDtypeStruct((bsz, s, A_WIDTH), BF16),
        scratch_shapes=[pltpu.VMEM((n_tiles, 2, 1, tq), F32), pltpu.VMEM((n_tiles, 2, LANES, tq), F32),
                        pltpu.VMEM((2, 2, tq, tq), F32)],
        compiler_params=_params(2),
        name="mla_prompt",
    )(qa, ka, vt, jnp.asarray(qmask, BF16))


def _mla_sample_body(q_ref, cc_ref, ckr_ref, cn_ref, krn_ref, wkt_ref, wv_ref, sel_ref, o_ref,
                     qabs_sc, qr_sc, m_sc, acc_sc, *, tk):
    t = q_ref.shape[1]
    past = cc_ref.shape[1]
    for h in range(A_HEADS):
        qh = q_ref[0, :, h * LANES:(h + 1) * LANES]
        rows = slice(h * t, (h + 1) * t)
        qabs_sc[rows, :] = jnp.dot(qh, wkt_ref[h], preferred_element_type=F32).astype(BF16)
        qr_sc[rows, :] = jnp.dot(qh, sel_ref[...], preferred_element_type=F32).astype(BF16)
    m_sc[...] = jnp.full(m_sc.shape, NEG_INF, F32)
    acc_sc[...] = jnp.zeros(acc_sc.shape, F32)
    contract_last = (((1,), (1,)), ((), ()))

    def scores(c_t, kr_t):
        return (lax.dot_general(c_t.astype(BF16), qabs_sc[...], contract_last, preferred_element_type=F32)
                + lax.dot_general(kr_t.astype(BF16), qr_sc[...], contract_last, preferred_element_type=F32))

    def update(s, c_t):
        n_keys = c_t.shape[0]
        m_old = m_sc[...]
        m_new = jnp.maximum(m_old, jnp.max(s, axis=0, keepdims=True))
        p = jnp.exp2(s - m_new).astype(BF16)
        ct = jnp.concatenate([c_t.T.astype(BF16), jnp.ones((BF16_SUBLANES, n_keys), BF16)], axis=0)
        acc_sc[...] = jnp.exp2(m_old - m_new) * acc_sc[...] + jnp.dot(ct, p, preferred_element_type=F32)
        m_sc[...] = m_new

    def tile(j):
        rows = pl.ds(j * tk, tk)
        return cc_ref[0, rows, :], ckr_ref[0, rows, :]

    n_tiles = past // tk
    cur = tile(0)
    s = scores(*cur)
    for j in range(n_tiles):
        nxt = tile(j + 1) if j + 1 < n_tiles else (cn_ref[0], krn_ref[0])
        s_next = scores(*nxt)
        update(s, cur[0])
        cur, s = nxt, s_next
    update(s, cur[0])

    acc = acc_sc[...]
    o_lat = (acc[:A_KV_RANK] / acc[A_KV_RANK:A_KV_RANK + 1]).T.astype(BF16)
    out = jnp.zeros((t, A_WIDTH), F32)
    for h in range(A_HEADS):
        out = out + jnp.dot(o_lat[h * t:(h + 1) * t, :], wv_ref[h], preferred_element_type=F32)
    o_ref[0] = out.astype(o_ref.dtype)


def _mla_sample(qa, cache_c, cache_kr, c_new, kr_new, wkt, wv, sel, tk):
    bsz, t, _ = qa.shape
    past = cache_c.shape[1]

    def per_b(shape):
        return pl.BlockSpec((1,) + shape, lambda b: (b, 0, 0))

    def full(a):
        return pl.BlockSpec(a.shape, lambda b: (0,) * a.ndim)

    rows = A_HEADS * t
    return pl.pallas_call(
        functools.partial(_mla_sample_body, tk=tk),
        grid=(bsz,),
        in_specs=[per_b((t, A_HEADS * LANES)), per_b((past, A_KV_RANK)), per_b((past, A_ROPE)),
                  per_b((t, A_KV_RANK)), per_b((t, A_ROPE)), full(wkt), full(wv), full(sel)],
        out_specs=per_b((t, A_WIDTH)),
        out_shape=jax.ShapeDtypeStruct((bsz, t, A_WIDTH), BF16),
        scratch_shapes=[pltpu.VMEM((rows, A_KV_RANK), BF16), pltpu.VMEM((rows, A_ROPE), BF16),
                        pltpu.VMEM((1, rows), F32), pltpu.VMEM((A_KV_RANK + BF16_SUBLANES, rows), F32)],
        compiler_params=_params(1),
        name="mla_sample",
    )(qa, cache_c, cache_kr, c_new, kr_new, wkt, wv, sel)


def _band_body(*refs, n_blk, n_qblk, cpg, ones_row, has_sink, variants, buffered=False):
    refs = list(refs)
    n_in = 5 if buffered else 3
    q_ref = refs[0]
    o_ref, s_sc, bias_sc = refs[-3:]
    extra = refs[n_in:-3]
    sink_ref = extra.pop() if has_sink else None
    toep_ref = extra.pop() if extra else None
    win = n_blk * LANES
    rows_g = cpg * CHUNK
    n_var = len(variants)
    qpc = 2 * n_qblk * CHUNK
    lo = lax.broadcasted_iota(jnp.int32, (CHUNK, LANES), 1) < HALF
    contract_last = (((1,), (1,)), ((), ()))

    key_chunk = lax.broadcasted_iota(jnp.int32, (win, LANES), 0) // CHUNK
    for v, chunks in enumerate(variants):
        for cc, (x0, lo_chunk, hi_chunk) in enumerate(chunks):
            tile = toep_ref[0, x0:x0 + win, :] if toep_ref is not None else jnp.zeros((win, LANES), F32)
            tile = jnp.where((key_chunk >= lo_chunk) & (key_chunk <= hi_chunk), tile, NEG_INF)
            for j in range(qpc // LANES):
                bias_sc[v, :, cc * qpc + j * LANES:cc * qpc + (j + 1) * LANES] = tile

    def aligned(x, n):
        return x if isinstance(x, int) else pl.multiple_of(x, n)

    def buffered_window(buf_ref, new_ref):
        buf, new = buf_ref[0], new_ref[0]
        pad = jnp.zeros((win - buf.shape[0] - new.shape[0], LANES), F32)
        return jnp.concatenate([pad, buf, new], axis=0)

    def scores(a, wb, variant, slot):
        pieces = []
        for cc in range(cpg):
            rows = pl.ds(aligned(a * rows_g + cc * CHUNK, CHUNK), CHUNK)
            for r in range(n_qblk):
                qblk = q_ref[0, rows, r * LANES:(r + 1) * LANES]
                zero = jnp.zeros_like(qblk)
                pieces += [jnp.where(lo, qblk, zero), jnp.where(lo, zero, qblk)]
        qs = jnp.concatenate(pieces, axis=0)
        if buffered:
            kw = buffered_window(refs[1], refs[2]).astype(BF16)
        else:
            kw = refs[1][0, pl.ds(aligned(wb * LANES, LANES), win), :]
        s_sc[slot] = lax.dot_general(kw, qs, contract_last, preferred_element_type=F32) + bias_sc[variant]

    def finish(a, wb, slot):
        s = s_sc[slot]
        m = jnp.max(s, axis=0, keepdims=True)
        if has_sink:
            sink = sink_ref[0]
            m = jnp.maximum(m, sink)
        p = jnp.exp2(s - m)
        if buffered:
            vt = buffered_window(refs[3], refs[4]).T.astype(BF16)
        else:
            vt = jnp.concatenate([refs[2][0, 0, wb + i] for i in range(n_blk)], axis=1)
        if not ones_row:
            vt = jnp.concatenate([vt, jnp.ones((BF16_SUBLANES, win), BF16)], axis=0)
        o = jnp.dot(vt, p.astype(BF16), preferred_element_type=F32)
        l = o[CHUNK:CHUNK + 1, :] if ones_row else o[LANES:LANES + 1, :]
        o = o[:LANES]
        if has_sink:
            l = l + jnp.exp2(sink - m)
        o = (o / l).T
        idx = 0
        for cc in range(cpg):
            rows = pl.ds(aligned(a * rows_g + cc * CHUNK, CHUNK), CHUNK)
            for r in range(n_qblk):
                top = o[idx * CHUNK:(idx + 1) * CHUNK]
                bot = o[(idx + 1) * CHUNK:(idx + 2) * CHUNK]
                if ones_row:
                    bot = pltpu.roll(bot, HALF, 1)
                o_ref[0, rows, r * LANES:(r + 1) * LANES] = jnp.where(lo, top, bot).astype(o_ref.dtype)
                idx += 2

    n_total = q_ref.shape[1] // rows_g
    if n_total == 1:
        scores(0, 0, 0, 0)
        finish(0, 0, 0)
        return
    g_blk = rows_g // LANES
    n_lead = n_var - 1 + (n_var - 1) % 2
    assert (n_var - 1) * g_blk >= n_blk - g_blk and n_total - n_lead >= 2 and BAND_PIPE_UNROLL % 2 == 0

    def window_block(a):
        wb = (a + 1) * g_blk - n_blk
        return max(wb, 0) if isinstance(a, int) else wb

    def stage(a, slot):
        nxt = a + 1
        scores(nxt, window_block(nxt), min(nxt, n_var - 1) if isinstance(nxt, int) else n_var - 1, 1 - slot)
        finish(a, window_block(a), slot)

    scores(0, window_block(0), 0, 0)
    for a in range(n_lead):
        stage(a, a & 1)

    def trip(i, carry):
        for st in range(BAND_PIPE_UNROLL):
            stage(n_lead + BAND_PIPE_UNROLL * i + st, st & 1)
        return carry

    n_stages = n_total - 1 - n_lead
    lax.fori_loop(0, n_stages // BAND_PIPE_UNROLL, trip, 0)
    for a in range(n_total - 1 - n_stages % BAND_PIPE_UNROLL, n_total - 1):
        stage(a, (a - n_lead) & 1)
    finish(n_total - 1, window_block(n_total - 1), (n_total - 1 - n_lead) & 1)


def _band_attention(q, keys, values, toep, sink, *, variants, n_blk, n_groups, n_qblk, cpg, ones_row):
    bsz, s, _ = q.shape
    qw = n_qblk * LANES
    buffered = isinstance(keys, tuple)

    def per_group(a):
        return pl.BlockSpec((1, a.shape[1], LANES), lambda g, b: (b, 0, g))

    in_specs = [pl.BlockSpec((1, s, qw), lambda g, b: (b, 0, g))]
    if buffered:
        args = [q, *keys, *values]
        in_specs += [per_group(a) for a in args[1:]]
    else:
        args = [q, keys, values]
        in_specs += [per_group(keys), pl.BlockSpec((1, 1) + values.shape[2:], lambda g, b: (b, g, 0, 0, 0))]
    for extra in (toep, sink):
        if extra is not None:
            in_specs.append(pl.BlockSpec((1,) + extra.shape[1:], lambda g, b: (g, 0, 0)))
            args.append(extra)
    bias_shape = (n_blk * LANES, cpg * 2 * n_qblk * CHUNK)
    body = functools.partial(_band_body, n_blk=n_blk, n_qblk=n_qblk, cpg=cpg, ones_row=ones_row,
                             has_sink=sink is not None, variants=variants, buffered=buffered)
    return pl.pallas_call(
        body,
        grid=(n_groups, bsz),
        in_specs=in_specs,
        out_specs=pl.BlockSpec((1, s, qw), lambda g, b: (b, 0, g)),
        out_shape=jax.ShapeDtypeStruct(q.shape, BF16),
        scratch_shapes=[pltpu.VMEM((2,) + bias_shape, F32), pltpu.VMEM((len(variants),) + bias_shape, F32)],
        compiler_params=_params(2),
        name="band_attention",
    )(*args)


def _rope_tables(pos, rot, lane_pattern, from_zero=False):
    half = rot // 2
    inv = jnp.power(ROPE_THETA, -jnp.arange(half, dtype=F32) * 2.0 / rot)
    inv_lane, first, second = [], [], []
    for kind, width in lane_pattern:
        if kind == "rot":
            inv_lane += [inv, inv]
            first += [1.0] * half + [0.0] * half
            second += [0.0] * half + [1.0] * half
        else:
            inv_lane.append(jnp.zeros((width,), F32))
            first += [0.0] * width
            second += [0.0] * width
    inv_lane = jnp.concatenate(inv_lane)[None, :]
    n = pos.shape[0]
    if from_zero and n % CHUNK == 0:
        a_hi = (jnp.arange(n // CHUNK, dtype=F32) * CHUNK)[:, None] * inv_lane
        a_lo = jnp.arange(CHUNK, dtype=F32)[:, None] * inv_lane
        c_hi, s_hi = jnp.cos(a_hi)[:, None, :], jnp.sin(a_hi)[:, None, :]
        c_lo, s_lo = jnp.cos(a_lo)[None], jnp.sin(a_lo)[None]
        cos = (c_hi * c_lo - s_hi * s_lo).reshape(n, LANES)
        sin = (s_hi * c_lo + c_hi * s_lo).reshape(n, LANES)
    else:
        ang = pos.astype(F32)[:, None] * inv_lane
        cos, sin = jnp.cos(ang), jnp.sin(ang)
    return jnp.stack([cos, sin * np.asarray(second, np.float32), -sin * np.asarray(first, np.float32)])


A_ROPE_PATTERN = (("pad", A_NOPE), ("rot", A_ROPE), ("pad", LANES - A_QK))
C_ROPE_PATTERN = (("rot", C_ROT), ("pad", HALF - C_ROT)) * 2


def _prep_ab(pre, post, w_in, q_norm, kv_norm, w_uq, w_ukv, rel_bias, w_out):
    d = w_in.shape[0]
    q_lat, c_kv, k_r, g_a, q_b, k_b, v_b, g_b = jnp.split(
        w_in, [384, 640, 672, 1184, 1696, 2208, 2720], axis=1)
    kr_blk = jnp.concatenate([jnp.zeros((d, A_NOPE), F32), k_r, jnp.zeros((d, LANES - A_QK), F32)], axis=1)
    w_in_p = jnp.concatenate([q_lat, c_kv, kr_blk, g_a, g_b, q_b, k_b, v_b], axis=1).astype(BF16)
    w_uq_p = jnp.pad(w_uq.reshape(A_Q_RANK, A_HEADS, A_QK), ((0, 0), (0, 0), (0, LANES - A_QK)))
    w_uq_p = w_uq_p.reshape(A_Q_RANK, A_HEADS * LANES).astype(BF16)
    ukv = w_ukv.reshape(A_KV_RANK, A_HEADS, A_NOPE + A_V)
    w_uk, w_uv = ukv[..., :A_NOPE], ukv[..., A_NOPE:]
    pad_half = ((0, 0), (0, 0), (0, LANES - A_NOPE))
    w_k = jnp.pad(w_uk, pad_half).reshape(A_KV_RANK, A_HEADS * LANES).astype(BF16)
    w_vt = jnp.pad(w_uv, pad_half).reshape(A_KV_RANK, A_HEADS * LANES).T.astype(BF16)
    wkt = jnp.pad(jnp.transpose(w_uk, (1, 2, 0)), ((0, 0), (0, LANES - A_NOPE), (0, 0))).astype(BF16)
    eye = jnp.eye(A_HEADS, dtype=F32)
    wv_s = (jnp.transpose(w_uv, (1, 0, 2))[:, :, None, :] * eye[:, None, :, None]).reshape(
        A_HEADS, A_KV_RANK, A_WIDTH).astype(BF16)
    sel = (jnp.arange(LANES)[:, None] == A_NOPE + jnp.arange(A_ROPE)[None, :]).astype(BF16)
    rel_bias = rel_bias * LOG2E
    win_p, win_s = B_WIN_BLOCKS * LANES, B_SAMPLE_BLOCKS * LANES
    r0 = win_p - CHUNK
    x_len = r0 + win_p
    n_vec = x_len + CHUNK
    n_hi = r0 + CHUNK - 1 - B_MAX_REL
    n_lo = n_vec - n_hi - (2 * B_MAX_REL + 1)
    vec = jnp.concatenate([jnp.broadcast_to(rel_bias[:, -1:], (B_HEADS, n_hi)), rel_bias[:, ::-1],
                           jnp.broadcast_to(rel_bias[:, :1], (B_HEADS, n_lo))], axis=1)
    skew = jnp.tile(vec, (1, CHUNK))[:, :CHUNK * (n_vec - 1)].reshape(B_HEADS, CHUNK, n_vec - 1)
    toep = skew[:, :, CHUNK - 1:CHUNK - 1 + x_len]
    toep = jnp.transpose(toep.reshape(B_HEADS // 2, 2, CHUNK, x_len), (0, 3, 1, 2)).reshape(B_HEADS // 2, x_len, LANES)
    var_p = tuple(tuple((r0 - c * CHUNK, c - B_PAST_CHUNKS, c)
                        for c in range(B_GROUP_CHUNKS * v, B_GROUP_CHUNKS * (v + 1))) for v in range(B_VARIANTS))
    var_s = (((r0 - (win_s - CHUNK), 1, B_PAST_CHUNKS + 1),),)
    return dict(pre=pre[None], w_in=w_in_p, q_norm=q_norm[None], kv_norm=kv_norm[None], w_uq=w_uq_p, w_k=w_k,
                w_vt=w_vt, wkt=wkt, wv_s=wv_s, sel=sel, toep=toep, var_p=var_p, var_s=var_s,
                w_out=w_out.astype(BF16), post=post[None])


def _prep_c(pre, post, w_in, sinks, w_out):
    q, k, v, g = jnp.split(w_in, [1024, 1152, 1280], axis=1)
    k0, k1 = k[:, :C_HEAD_DIM], k[:, C_HEAD_DIM:]
    v0, v1 = v[:, :C_HEAD_DIM], v[:, C_HEAD_DIM:]
    zero = jnp.zeros_like(v0)
    w_in_p = jnp.concatenate([q, k0, k0, k1, k1, v0, zero, v1, zero, g], axis=1).astype(BF16)
    var_p = tuple(tuple((0, c - C_PAST_CHUNKS, c) for c in range(C_GROUP_CHUNKS * v, C_GROUP_CHUNKS * (v + 1)))
                  for v in range(C_VARIANTS))
    var_s = (((0, 1, C_PAST_CHUNKS + 1),),)
    sink_row = jnp.repeat((sinks * LOG2E).reshape(C_KV_HEADS, C_GROUP), CHUNK, axis=1)[:, None, :]
    return dict(pre=pre[None], w_in=w_in_p, var_p=var_p, var_s=var_s, sink_s=sink_row,
                sink_p=jnp.tile(sink_row, (1, 1, C_GROUP_CHUNKS)), w_out=w_out.astype(BF16), post=post[None])


def _dup_heads(x):
    return jnp.concatenate([x[:, :, 0], x[:, :, 0], x[:, :, 1], x[:, :, 1]], axis=-1)


def kernel(x_prompt, x_sample, cache_a_ckv, cache_a_krope, cache_b_k, cache_b_v, cache_c_k, cache_c_v,
           ab_pre_norm, ab_post_norm, ab_w_in, ab_q_norm, ab_kv_norm, ab_w_uq, ab_w_ukv, ab_rel_bias, ab_w_out,
           c_pre_norm, c_post_norm, c_w_in, c_sinks, c_w_out):
    bsz, seq, _ = x_prompt.shape
    dbs, dseq, _ = x_sample.shape
    past = cache_a_ckv.shape[2]
    n_s = dbs * dseq
    pos_p = jnp.arange(seq, dtype=jnp.int32)
    pos_s = jnp.tile(past + jnp.arange(dseq, dtype=jnp.int32), dbs)
    wab = _prep_ab(ab_pre_norm[0], ab_post_norm[0], ab_w_in[0], ab_q_norm[0], ab_kv_norm[0], ab_w_uq[0],
                   ab_w_ukv[0], ab_rel_bias[0], ab_w_out[0])
    wc = _prep_c(c_pre_norm[0], c_post_norm[0], c_w_in[0], c_sinks[0], c_w_out[0])
    b_tail = min(B_PAST_CHUNKS * CHUNK, seq)
    c_tail = min(C_WINDOW, seq)
    tile = 512

    rope_a_p = _rope_tables(pos_p, A_ROPE, A_ROPE_PATTERN, from_zero=True)
    (qa, ka, vt, qb, kb, vbt, sg, c_new_p, kr_new_p, kb_tail, vb_tail) = _ab_in_proj(
        x_prompt, rope_a_p, wab, tm=256, tail_len=b_tail, tkv=tile)
    o_a = _mla_prompt(qa, ka, vt, tq=tile)
    o_b = _band_attention(qb, kb, vbt, wab["toep"], None, variants=wab["var_p"], n_blk=B_WIN_BLOCKS,
                          n_groups=B_HEADS // 2, n_qblk=1, cpg=B_GROUP_CHUNKS, ones_row=False)

    rope_a_s = _rope_tables(pos_s, A_ROPE, A_ROPE_PATTERN)
    xs = x_sample.reshape(1, n_s, D_MODEL)
    (qa_s, _, _, qb_s, _, _, sg_s, c_new_s, kr_new_s, kb_s32, vb_s32) = _ab_in_proj(
        xs, rope_a_s, wab, tm=n_s, tail_len=n_s, tkv=n_s)
    o_a_s = _mla_sample(qa_s.reshape(dbs, dseq, -1), cache_a_ckv[0], cache_a_krope[0],
                        c_new_s.reshape(dbs, dseq, -1), kr_new_s.reshape(dbs, dseq, -1),
                        wab["wkt"], wab["wv_s"], wab["sel"], tk=512)
    wb = cache_b_k.shape[2]
    o_b_s = _band_attention(qb_s.reshape(dbs, dseq, -1),
                            (cache_b_k[0].reshape(dbs, wb, B_WIDTH), kb_s32.reshape(dbs, dseq, -1)),
                            (cache_b_v[0].reshape(dbs, wb, B_WIDTH), vb_s32.reshape(dbs, dseq, -1)),
                            wab["toep"], None, variants=wab["var_s"], n_blk=B_SAMPLE_BLOCKS, n_groups=B_HEADS // 2,
                            n_qblk=1, cpg=1, ones_row=False)

    rope_c_p = _rope_tables(pos_p, C_ROT, C_ROPE_PATTERN, from_zero=True)
    h1_p, qc, kc, vct, sgc, kc_tail, vc_tail = _c_in_proj(
        [o_a, o_b], sg, wab["w_out"], wab["post"], x_prompt, rope_c_p, wc, tm=tile, tail_len=c_tail)
    o_c = _band_attention(qc, kc, vct, None, wc["sink_p"], variants=wc["var_p"], n_blk=C_WIN_BLOCKS,
                          n_groups=C_KV_HEADS, n_qblk=C_GROUP // 2, cpg=C_GROUP_CHUNKS, ones_row=True)
    h2_p = _out_proj([o_c], sgc, wc["w_out"], wc["post"], h1_p, tm=tile)

    rope_c_s = _rope_tables(pos_s, C_ROT, C_ROPE_PATTERN)
    h1_s, qc_s, kc_s, _, sgc_s, kc_s32, vc_s32 = _c_in_proj(
        [o_a_s.reshape(1, n_s, -1), o_b_s.reshape(1, n_s, -1)], sg_s, wab["w_out"], wab["post"], xs, rope_c_s, wc,
        tm=n_s, tail_len=n_s)
    wcw = cache_c_k.shape[2]
    win_c = C_SAMPLE_BLOCKS * LANES
    n_pad = win_c - wcw - dseq
    kcb = jnp.concatenate([jnp.zeros((dbs, n_pad, C_KV_HEADS * LANES), BF16), _dup_heads(cache_c_k[0]).astype(BF16),
                           kc_s.reshape(dbs, dseq, -1)], axis=1)
    vcb = jnp.concatenate([jnp.zeros((dbs, n_pad, C_KV_HEADS, C_HEAD_DIM), F32), cache_c_v[0],
                           vc_s32.reshape(dbs, dseq, C_KV_HEADS, C_HEAD_DIM)], axis=1)
    vcb_t = jnp.concatenate([jnp.transpose(vcb, (0, 2, 3, 1)), jnp.ones((dbs, C_KV_HEADS, 1, win_c), F32),
                             jnp.zeros((dbs, C_KV_HEADS, LANES - C_HEAD_DIM - 1, win_c), F32)], axis=2)
    vcb_t = jnp.transpose(vcb_t.astype(BF16).reshape(dbs, C_KV_HEADS, LANES, C_SAMPLE_BLOCKS, LANES), (0, 1, 3, 2, 4))
    o_c_s = _band_attention(qc_s.reshape(dbs, dseq, -1), kcb, vcb_t, None, wc["sink_s"], variants=wc["var_s"],
                            n_blk=C_SAMPLE_BLOCKS, n_groups=C_KV_HEADS, n_qblk=C_GROUP // 2, cpg=1, ones_row=True)
    h2_s = _out_proj([o_c_s.reshape(1, n_s, -1)], sgc_s, wc["w_out"], wc["post"], h1_s, tm=n_s)

    def roll_in(buf, new):
        return jnp.concatenate([buf, new], axis=1)[:, -buf.shape[1]:][None]

    return (h2_p, h2_s.reshape(dbs, dseq, D_MODEL),
            c_new_p[None], kr_new_p[None],
            kb_tail.reshape(1, bsz, b_tail, B_HEADS, B_HEAD_DIM), vb_tail.reshape(1, bsz, b_tail, B_HEADS, B_HEAD_DIM),
            kc_tail.reshape(1, bsz, c_tail, C_KV_HEADS, C_HEAD_DIM), vc_tail.reshape(1, bsz, c_tail, C_KV_HEADS, C_HEAD_DIM),
            c_new_s.reshape(1, dbs, dseq, A_KV_RANK), kr_new_s.reshape(1, dbs, dseq, A_ROPE),
            roll_in(cache_b_k[0], kb_s32.reshape(dbs, dseq, B_HEADS, B_HEAD_DIM)),
            roll_in(cache_b_v[0], vb_s32.reshape(dbs, dseq, B_HEADS, B_HEAD_DIM)),
            roll_in(cache_c_k[0], kc_s32.reshape(dbs, dseq, C_KV_HEADS, C_HEAD_DIM)),
            roll_in(cache_c_v[0], vc_s32.reshape(dbs, dseq, C_KV_HEADS, C_HEAD_DIM)))
```

```python
import functools

import jax
import jax.numpy as jnp
import numpy as np
from jax import lax
from jax.experimental import pallas as pl
from jax.experimental.pallas import tpu as pltpu

F32 = jnp.float32
BF16 = jnp.bfloat16

D_MODEL = 1024
CHUNK = 64
ROPE_THETA = 500000.0
RMS_EPS = 1e-6
NEG_INF = -1e30

A_HEADS = 8
A_NOPE = 64
A_ROPE = 32
A_QK = A_NOPE + A_ROPE
A_V = 64
A_Q_RANK = 384
A_KV_RANK = 256
A_WIDTH = A_HEADS * A_V
LOG2E = 1.4426950408889634
A_SCALE = A_QK ** -0.5 * LOG2E

B_HEADS = 8
B_HEAD_DIM = 64
B_WIDTH = B_HEADS * B_HEAD_DIM
B_PAST_CHUNKS = 8
B_MAX_REL = 128
B_SCALE = B_HEAD_DIM ** -0.5 * LOG2E

C_HEADS = 16
C_KV_HEADS = 2
C_GROUP = C_HEADS // C_KV_HEADS
C_HEAD_DIM = 64
C_WIDTH = C_HEADS * C_HEAD_DIM
C_WINDOW = 128
C_PAST_CHUNKS = C_WINDOW // CHUNK
C_ROT = C_HEAD_DIM // 4
C_SCALE = C_HEAD_DIM ** -0.5 * LOG2E

LANES = 128
HALF = LANES // 2
BF16_SUBLANES = 16
VMEM_LIMIT = 56 * 1024 * 1024
A_MASK_CHUNKS = 8
assert A_QK + A_MASK_CHUNKS <= LANES
A_PIPE_UNROLL = 4
BAND_PIPE_UNROLL = 4
B_GROUP_CHUNKS = 4
C_GROUP_CHUNKS = 2
B_WIN_BLOCKS = (B_PAST_CHUNKS + B_GROUP_CHUNKS) * CHUNK // LANES
C_WIN_BLOCKS = (C_PAST_CHUNKS + C_GROUP_CHUNKS) * CHUNK // LANES
B_VARIANTS = B_PAST_CHUNKS // B_GROUP_CHUNKS + 1
C_VARIANTS = C_PAST_CHUNKS // C_GROUP_CHUNKS + 1
B_SAMPLE_BLOCKS = (B_PAST_CHUNKS + 2) * CHUNK // LANES
C_SAMPLE_BLOCKS = (C_PAST_CHUNKS + 2) * CHUNK // LANES

AB_Q0, AB_C0, AB_KR0, AB_G0, AB_QB0, AB_KB0, AB_VB0, AB_NZ = 0, 384, 640, 768, 1792, 2304, 2816, 3328
C_Q0, C_K0, C_V0, C_G0, C_NZ = 0, 1024, 1280, 1536, 2560


def _params(n_axes):
    return pltpu.CompilerParams(dimension_semantics=("arbitrary",) * n_axes, vmem_limit_bytes=VMEM_LIMIT)


def _rms(x, g):
    return x * lax.rsqrt(jnp.mean(x * x, axis=-1, keepdims=True) + RMS_EPS) * g


def _rope_block(blk, rope_ref, shift):
    return (blk * rope_ref[0] + pltpu.roll(blk, shift, 1) * rope_ref[1]
            + pltpu.roll(blk, LANES - shift, 1) * rope_ref[2])


def _tail_spec(n_tiles, tm, tail_len, width):
    tb = min(tail_len, tm)
    n_blk = tail_len // tb
    return tb, pl.BlockSpec((1, tb, width), lambda b, i: (b, jnp.maximum(i - (n_tiles - n_blk), 0), 0))


def _ab_in_body(x_ref, pre_ref, w_ref, qn_ref, kvn_ref, wuq_ref, wk_ref, wvt_ref, rope_ref,
                qa_ref, ka_ref, vt_ref, qb_ref, kb_ref, vbt_ref, sg_ref, c_ref, kr_ref, kb_tail_ref, vb_tail_ref):
    tm = x_ref.shape[1]
    tb = kb_tail_ref.shape[1]
    xn = _rms(x_ref[0], pre_ref[...]).astype(BF16)
    z = jnp.dot(xn, w_ref[...], preferred_element_type=F32)

    qn = _rms(z[:, AB_Q0:AB_C0], qn_ref[...]).astype(BF16)
    qa = jnp.dot(qn, wuq_ref[...], preferred_element_type=F32) * A_SCALE
    for h in range(A_HEADS):
        blk = slice(h * LANES, (h + 1) * LANES)
        qa_ref[0, :, blk] = _rope_block(qa[:, blk], rope_ref, A_ROPE // 2).astype(BF16)

    c_new = _rms(z[:, AB_C0:AB_KR0], kvn_ref[...])
    c_ref[0] = c_new
    cb = c_new.astype(BF16)
    krot = _rope_block(z[:, AB_KR0:AB_G0], rope_ref, A_ROPE // 2)
    kr_ref[0] = krot[:, A_NOPE:A_NOPE + A_ROPE]
    kn = jnp.dot(cb, wk_ref[...], preferred_element_type=F32)
    row = pl.program_id(1) * tm + lax.broadcasted_iota(jnp.int32, (tm, LANES), 0)
    lane = lax.broadcasted_iota(jnp.int32, (tm, LANES), 1)
    k_shared = krot + jnp.where(lane - A_QK == (row // CHUNK) % A_MASK_CHUNKS, 1.0, 0.0)
    for h in range(A_HEADS):
        blk = slice(h * LANES, (h + 1) * LANES)
        ka_ref[0, :, blk] = (kn[:, blk] + k_shared).astype(BF16)
    vt = jnp.dot(wvt_ref[...], c_new.T.astype(BF16), preferred_element_type=F32)
    row = lax.broadcasted_iota(jnp.int32, vt.shape, 0)
    vt_ref[0, 0] = jnp.where((row & (LANES - 1)) == A_V, 1.0, vt).astype(BF16)

    g = z[:, AB_G0:AB_QB0]
    sg_ref[0] = (g * jax.nn.sigmoid(g)).astype(BF16)
    qb_ref[0] = (z[:, AB_QB0:AB_KB0] * B_SCALE).astype(BF16)
    kb = z[:, AB_KB0:AB_VB0]
    vb = z[:, AB_VB0:AB_NZ]
    kb_ref[0] = kb.astype(BF16)
    vbt = vb.T.astype(BF16)
    for hp in range(B_HEADS // 2):
        for j in range(tm // LANES):
            vbt_ref[0, hp, j] = vbt[hp * LANES:(hp + 1) * LANES, j * LANES:(j + 1) * LANES]
    kb_tail_ref[0] = kb[tm - tb:, :]
    vb_tail_ref[0] = vb[tm - tb:, :]


def _ab_in_proj(x, rope, w, tm, tail_len, tkv):
    bsz, s, _ = x.shape
    nt = s // tm
    per_kv = tkv // tm
    tb, tail_spec = _tail_spec(nt, tm, tail_len, B_WIDTH)

    def full(a):
        return pl.BlockSpec(a.shape, lambda b, i: (0,) * a.ndim)

    def rows(width):
        return pl.BlockSpec((1, tm, width), lambda b, i: (b, i, 0))

    weights = (w["pre"], w["w_in"], w["q_norm"], w["kv_norm"], w["w_uq"], w["w_k"], w["w_vt"])
    vt_spec = pl.BlockSpec((1, 1, A_HEADS * LANES, tm), lambda b, i: (b, i // per_kv, 0, i % per_kv))
    out_shape = (
        jax.ShapeDtypeStruct((bsz, s, A_HEADS * LANES), BF16),
        jax.ShapeDtypeStruct((bsz, s, A_HEADS * LANES), BF16),
        jax.ShapeDtypeStruct((bsz, s // tkv, A_HEADS * LANES, tkv), BF16),
        jax.ShapeDtypeStruct((bsz, s, B_WIDTH), BF16),
        jax.ShapeDtypeStruct((bsz, s, B_WIDTH), BF16),
        jax.ShapeDtypeStruct((bsz, B_HEADS // 2, s // LANES, LANES, LANES), BF16),
        jax.ShapeDtypeStruct((bsz, s, A_WIDTH + B_WIDTH), BF16),
        jax.ShapeDtypeStruct((bsz, s, A_KV_RANK), F32),
        jax.ShapeDtypeStruct((bsz, s, A_ROPE), F32),
        jax.ShapeDtypeStruct((bsz, tail_len, B_WIDTH), F32),
        jax.ShapeDtypeStruct((bsz, tail_len, B_WIDTH), F32),
    )
    vbt_spec = pl.BlockSpec((1, B_HEADS // 2, tm // LANES, LANES, LANES), lambda b, i: (b, 0, i, 0, 0))
    out_specs = (rows(1024), rows(1024), vt_spec, rows(512), rows(512), vbt_spec, rows(1024),
                 rows(A_KV_RANK), rows(A_ROPE), tail_spec, tail_spec)
    return pl.pallas_call(
        _ab_in_body,
        grid=(bsz, nt),
        in_specs=[rows(D_MODEL)] + [full(a) for a in weights]
        + [pl.BlockSpec((3, tm, LANES), lambda b, i: (0, i, 0))],
        out_specs=out_specs,
        out_shape=out_shape,
        compiler_params=_params(2),
        name="ab_in_proj",
    )(x, *weights, rope)


def _mix_out(o_refs, sg_ref, w_ref, g_ref, h_ref):
    o = jnp.concatenate([r[0].astype(F32) for r in o_refs], axis=-1) if len(o_refs) > 1 else o_refs[0][0].astype(F32)
    mixed = (o * sg_ref[0].astype(F32)).astype(BF16)
    y = jnp.dot(mixed, w_ref[...], preferred_element_type=F32)
    return h_ref[0] + _rms(y, g_ref[...])


def _c_in_body(*refs):
    n_o = len(refs) - 14
    o_refs = refs[:n_o]
    (sg0_ref, w0_ref, post0_ref, x_ref, pre_ref, w_ref, rope_ref,
     h_ref, q_ref, k_ref, vt_ref, sg_ref, k_tail_ref, v_tail_ref) = refs[n_o:]
    tm = x_ref.shape[1]
    tb = k_tail_ref.shape[1]
    h = _mix_out(o_refs, sg0_ref, w0_ref, post0_ref, x_ref)
    h_ref[0] = h
    xn = _rms(h, pre_ref[...]).astype(BF16)
    z = jnp.dot(xn, w_ref[...], preferred_element_type=F32)
    half_rot = C_ROT // 2
    for j in range(C_WIDTH // LANES):
        blk = slice(C_Q0 + j * LANES, C_Q0 + (j + 1) * LANES)
        q_ref[0, :, j * LANES:(j + 1) * LANES] = (_rope_block(z[:, blk], rope_ref, half_rot) * C_SCALE).astype(BF16)
    kd = [_rope_block(z[:, C_K0 + j * LANES:C_K0 + (j + 1) * LANES], rope_ref, half_rot) for j in range(C_KV_HEADS)]
    vz = [z[:, C_V0 + j * LANES:C_V0 + (j + 1) * LANES] for j in range(C_KV_HEADS)]
    row = lax.broadcasted_iota(jnp.int32, (LANES, tm), 0)
    for j in range(C_KV_HEADS):
        k_ref[0, :, j * LANES:(j + 1) * LANES] = kd[j].astype(BF16)
        vt = jnp.where(row == C_HEAD_DIM, 1.0, vz[j].T).astype(BF16)
        for i in range(tm // LANES):
            vt_ref[0, j, i] = vt[:, i * LANES:(i + 1) * LANES]
    g = z[:, C_G0:C_NZ]
    sg_ref[0] = (g * jax.nn.sigmoid(g)).astype(BF16)
    lo = lax.broadcasted_iota(jnp.int32, (1, LANES), 1) < HALF
    k_tail_ref[0] = jnp.where(lo, kd[0], kd[1])[tm - tb:, :]
    v_tail_ref[0] = (vz[0] + pltpu.roll(vz[1], HALF, 1))[tm - tb:, :]


def _c_in_proj(o_parts, sg0, w_out0, post0, x, rope, w, tm, tail_len):
    bsz, s, _ = x.shape
    nt = s // tm
    tb, tail_spec = _tail_spec(nt, tm, tail_len, LANES)

    def full(a):
        return pl.BlockSpec(a.shape, lambda b, i: (0,) * a.ndim)

    def rows(width):
        return pl.BlockSpec((1, tm, width), lambda b, i: (b, i, 0))

    out_shape = (
        jax.ShapeDtypeStruct((bsz, s, D_MODEL), F32),
        jax.ShapeDtypeStruct((bsz, s, C_WIDTH), BF16),
        jax.ShapeDtypeStruct((bsz, s, C_KV_HEADS * LANES), BF16),
        jax.ShapeDtypeStruct((bsz, C_KV_HEADS, s // LANES, LANES, LANES), BF16),
        jax.ShapeDtypeStruct((bsz, s, C_WIDTH), BF16),
        jax.ShapeDtypeStruct((bsz, tail_len, LANES), F32),
        jax.ShapeDtypeStruct((bsz, tail_len, LANES), F32),
    )
    return pl.pallas_call(
        _c_in_body,
        grid=(bsz, nt),
        in_specs=[rows(o.shape[-1]) for o in o_parts]
        + [rows(D_MODEL), full(w_out0), full(post0), rows(D_MODEL), full(w["pre"]), full(w["w_in"]),
           pl.BlockSpec((3, tm, LANES), lambda b, i: (0, i, 0))],
        out_specs=(rows(D_MODEL), rows(C_WIDTH), rows(256),
                   pl.BlockSpec((1, C_KV_HEADS, tm // LANES, LANES, LANES), lambda b, i: (b, 0, i, 0, 0)),
                   rows(C_WIDTH), tail_spec, tail_spec),
        out_shape=out_shape,
        compiler_params=_params(2),
        name="c_in_proj",
    )(*o_parts, sg0, w_out0, post0, x, w["pre"], w["w_in"], rope)


def _out_body(*refs):
    o_refs, (sg_ref, w_ref, g_ref, h_ref, out_ref) = refs[:-5], refs[-5:]
    out_ref[0] = _mix_out(o_refs, sg_ref, w_ref, g_ref, h_ref)


def _out_proj(o_parts, sg, w_out, post_g, h, tm):
    bsz, s, _ = h.shape

    def rows(width):
        return pl.BlockSpec((1, tm, width), lambda b, i: (b, i, 0))

    def full(a):
        return pl.BlockSpec(a.shape, lambda b, i: (0,) * a.ndim)

    return pl.pallas_call(
        _out_body,
        grid=(bsz, s // tm),
        in_specs=[rows(o.shape[-1]) for o in o_parts] + [rows(D_MODEL), full(w_out), full(post_g), rows(D_MODEL)],
        out_specs=rows(D_MODEL),
        out_shape=jax.ShapeDtypeStruct(h.shape, F32),
        compiler_params=_params(2),
        name="out_proj",
    )(*o_parts, sg, w_out, post_g, h)


def _mla_prompt_body(q_ref, k_ref, vt_ref, qmask_ref, o_ref, m_sc, acc_sc, s_sc, *, tq):
    n_tiles = q_ref.shape[1] // tq
    n_pairs = n_tiles * (n_tiles + 1) // 2
    assert A_PIPE_UNROLL % 2 == 0 and n_pairs % A_PIPE_UNROLL == 0 and n_pairs >= 2 * A_PIPE_UNROLL
    m_sc[...] = jnp.full(m_sc.shape, NEG_INF, F32)
    acc_sc[...] = jnp.zeros(acc_sc.shape, F32)
    contract_last = (((1,), (1,)), ((), ()))

    def scores(qi, j, slot):
        q_start = pl.multiple_of(qi * tq, tq)
        k_start = pl.multiple_of(j * tq, tq)
        qmask = jnp.where(j == qi, qmask_ref[...], jnp.zeros_like(qmask_ref[...]))
        for hh in range(2):
            blk = slice(hh * LANES, (hh + 1) * LANES)
            q = q_ref[0, pl.ds(q_start, tq), blk] + qmask
            s_sc[slot, hh] = lax.dot_general(k_ref[0, pl.ds(k_start, tq), blk], q, contract_last,
                                             preferred_element_type=F32)

    def consume(qi, j, slot):
        for hh in range(2):
            s = s_sc[slot, hh]
            vt = vt_ref[0, j, hh * LANES:(hh + 1) * LANES, :]
            m_old = m_sc[qi, hh]
            m_new = jnp.maximum(m_old, jnp.max(s, axis=0, keepdims=True))
            p = jnp.exp2(s - m_new).astype(BF16)
            acc_sc[qi, hh] = jnp.exp2(m_old - m_new) * acc_sc[qi, hh] + jnp.dot(vt, p, preferred_element_type=F32)
            m_sc[qi, hh] = m_new

    def advance(qi, j):
        last = j == qi
        return jnp.where(last, qi + 1, qi), jnp.where(last, 0, j + 1)

    def stages(cur, count):
        for st in range(count):
            nxt = advance(*cur)
            scores(*nxt, 1 - (st & 1))
            consume(*cur, st & 1)
            cur = nxt
        return cur

    zero = jnp.int32(0)
    scores(zero, zero, 0)
    cur = lax.fori_loop(0, n_pairs // A_PIPE_UNROLL - 1, lambda i, c: stages(c, A_PIPE_UNROLL), (zero, zero))
    cur = stages(cur, A_PIPE_UNROLL - 1)
    consume(*cur, (A_PIPE_UNROLL - 1) & 1)

    lo = lax.broadcasted_iota(jnp.int32, (1, LANES), 1) < HALF

    def write_tile(qi, carry):
        outs = []
        for hh in range(2):
            acc = acc_sc[qi, hh]
            outs.append((acc / acc[A_V:A_V + 1, :]).T)
        o_ref[0, pl.ds(pl.multiple_of(qi * tq, tq), tq), :] = jnp.where(
            lo, outs[0], pltpu.roll(outs[1], HALF, 1)).astype(o_ref.dtype)
        return carry

    lax.fori_loop(0, n_tiles, write_tile, 0)


def _mla_prompt(qa, ka, vt, tq):
    bsz, s, _ = qa.shape
    pairs = A_HEADS // 2
    n_tiles = s // tq
    assert vt.shape[3] == tq and tq == A_MASK_CHUNKS * CHUNK
    q_chunk = np.arange(tq)[:, None] // CHUNK
    lane_chunk = np.arange(LANES)[None, :] - A_QK
    qmask = np.where((lane_chunk > q_chunk) & (lane_chunk < A_MASK_CHUNKS), NEG_INF, 0.0).astype(np.float32)
    return pl.pallas_call(
        functools.partial(_mla_prompt_body, tq=tq),
        grid=(bsz, pairs),
        in_specs=[pl.BlockSpec((1, s, 2 * LANES), lambda b, h: (b, 0, h)),
                  pl.BlockSpec((1, s, 2 * LANES), lambda b, h: (b, 0, h)),
                  pl.BlockSpec((1, n_tiles, 2 * LANES, tq), lambda b, h: (b, 0, h, 0)),
                  pl.BlockSpec((tq, LANES), lambda b, h: (0, 0))],
        out_specs=pl.BlockSpec((1, s, LANES), lambda b, h: (b, 0, h)),
        out_shape=jax.ShapeDtypeStruct((bsz, s, A_WIDTH), BF16),
        scratch_shapes=[pltpu.VMEM((n_tiles, 2, 1, tq), F32), pltpu.VMEM((n_tiles, 2, LANES, tq), F32),
                        pltpu.VMEM((2, 2, tq, tq), F32)],
        compiler_params=_params(2),
        name="mla_prompt",
    )(qa, ka, vt, jnp.asarray(qmask, BF16))


def _mla_sample_body(q_ref, cc_ref, ckr_ref, cn_ref, krn_ref, wkt_ref, wv_ref, sel_ref, o_ref,
                     qabs_sc, qr_sc, m_sc, acc_sc, *, tk):
    t = q_ref.shape[1]
    past = cc_ref.shape[1]
    for h in range(A_HEADS):
        qh = q_ref[0, :, h * LANES:(h + 1) * LANES]
        rows = slice(h * t, (h + 1) * t)
        qabs_sc[rows, :] = jnp.dot(qh, wkt_ref[h], preferred_element_type=F32).astype(BF16)
        qr_sc[rows, :] = jnp.dot(qh, sel_ref[...], preferred_element_type=F32).astype(BF16)
    m_sc[...] = jnp.full(m_sc.shape, NEG_INF, F32)
    acc_sc[...] = jnp.zeros(acc_sc.shape, F32)
    contract_last = (((1,), (1,)), ((), ()))

    def scores(c_t, kr_t):
        return (lax.dot_general(c_t.astype(BF16), qabs_sc[...], contract_last, preferred_element_type=F32)
                + lax.dot_general(kr_t.astype(BF16), qr_sc[...], contract_last, preferred_element_type=F32))

    def update(s, c_t):
        n_keys = c_t.shape[0]
        m_old = m_sc[...]
        m_new = jnp.maximum(m_old, jnp.max(s, axis=0, keepdims=True))
        p = jnp.exp2(s - m_new).astype(BF16)
        ct = jnp.concatenate([c_t.T.astype(BF16), jnp.ones((BF16_SUBLANES, n_keys), BF16)], axis=0)
        acc_sc[...] = jnp.exp2(m_old - m_new) * acc_sc[...] + jnp.dot(ct, p, preferred_element_type=F32)
        m_sc[...] = m_new

    def tile(j):
        rows = pl.ds(j * tk, tk)
        return cc_ref[0, rows, :], ckr_ref[0, rows, :]

    n_tiles = past // tk
    cur = tile(0)
    s = scores(*cur)
    for j in range(n_tiles):
        nxt = tile(j + 1) if j + 1 < n_tiles else (cn_ref[0], krn_ref[0])
        s_next = scores(*nxt)
        update(s, cur[0])
        cur, s = nxt, s_next
    update(s, cur[0])

    acc = acc_sc[...]
    o_lat = (acc[:A_KV_RANK] / acc[A_KV_RANK:A_KV_RANK + 1]).T.astype(BF16)
    out = jnp.zeros((t, A_WIDTH), F32)
    for h in range(A_HEADS):
        out = out + jnp.dot(o_lat[h * t:(h + 1) * t, :], wv_ref[h], preferred_element_type=F32)
    o_ref[0] = out.astype(o_ref.dtype)


def _mla_sample(qa, cache_c, cache_kr, c_new, kr_new, wkt, wv, sel, tk):
    bsz, t, _ = qa.shape
    past = cache_c.shape[1]

    def per_b(shape):
        return pl.BlockSpec((1,) + shape, lambda b: (b, 0, 0))

    def full(a):
        return pl.BlockSpec(a.shape, lambda b: (0,) * a.ndim)

    rows = A_HEADS * t
    return pl.pallas_call(
        functools.partial(_mla_sample_body, tk=tk),
        grid=(bsz,),
        in_specs=[per_b((t, A_HEADS * LANES)), per_b((past, A_KV_RANK)), per_b((past, A_ROPE)),
                  per_b((t, A_KV_RANK)), per_b((t, A_ROPE)), full(wkt), full(wv), full(sel)],
        out_specs=per_b((t, A_WIDTH)),
        out_shape=jax.ShapeDtypeStruct((bsz, t, A_WIDTH), BF16),
        scratch_shapes=[pltpu.VMEM((rows, A_KV_RANK), BF16), pltpu.VMEM((rows, A_ROPE), BF16),
                        pltpu.VMEM((1, rows), F32), pltpu.VMEM((A_KV_RANK + BF16_SUBLANES, rows), F32)],
        compiler_params=_params(1),
        name="mla_sample",
    )(qa, cache_c, cache_kr, c_new, kr_new, wkt, wv, sel)


def _band_body(*refs, n_blk, n_qblk, cpg, ones_row, has_sink, variants, buffered=False):
    refs = list(refs)
    n_in = 5 if buffered else 3
    q_ref = refs[0]
    o_ref, s_sc, bias_sc = refs[-3:]
    extra = refs[n_in:-3]
    sink_ref = extra.pop() if has_sink else None
    toep_ref = extra.pop() if extra else None
    win = n_blk * LANES
    rows_g = cpg * CHUNK
    n_var = len(variants)
    qpc = 2 * n_qblk * CHUNK
    lo = lax.broadcasted_iota(jnp.int32, (CHUNK, LANES), 1) < HALF
    contract_last = (((1,), (1,)), ((), ()))

    if toep_ref is not None:
        key_chunk = lax.broadcasted_iota(jnp.int32, (win, LANES), 0) // CHUNK
        for v, chunks in enumerate(variants):
            for cc, (x0, lo_chunk, hi_chunk) in enumerate(chunks):
                tile = jnp.where((key_chunk >= lo_chunk) & (key_chunk <= hi_chunk), toep_ref[0, x0:x0 + win, :], NEG_INF)
                for j in range(qpc // LANES):
                    bias_sc[v, :, cc * qpc + j * LANES:cc * qpc + (j + 1) * LANES] = tile

    def store_scores(slot, variant, s):
        if toep_ref is not None:
            s_sc[slot] = s + bias_sc[variant]
            return
        s_sc[slot] = s
        for cc, (_, lo_chunk, hi_chunk) in enumerate(variants[variant]):
            cols = slice(cc * qpc, (cc + 1) * qpc)
            for r0, r1 in ((0, lo_chunk * CHUNK), ((hi_chunk + 1) * CHUNK, win)):
                if r1 > max(r0, 0):
                    s_sc[slot, max(r0, 0):r1, cols] = jnp.full((r1 - max(r0, 0), qpc), NEG_INF, F32)

    def aligned(x, n):
        return x if isinstance(x, int) else pl.multiple_of(x, n)

    def buffered_window(buf_ref, new_ref):
        buf, new = buf_ref[0], new_ref[0]
        pad = jnp.zeros((win - buf.shape[0] - new.shape[0], LANES), F32)
        return jnp.concatenate([pad, buf, new], axis=0)

    def scores(a, wb, variant, slot):
        pieces = []
        for cc in range(cpg):
            rows = pl.ds(aligned(a * rows_g + cc * CHUNK, CHUNK), CHUNK)
            for r in range(n_qblk):
                qblk = q_ref[0, rows, r * LANES:(r + 1) * LANES]
                zero = jnp.zeros_like(qblk)
                pieces += [jnp.where(lo, qblk, zero), jnp.where(lo, zero, qblk)]
        qs = jnp.concatenate(pieces, axis=0)
        if buffered:
            kw = buffered_window(refs[1], refs[2]).astype(BF16)
        else:
            kw = refs[1][0, pl.ds(aligned(wb * LANES, LANES), win), :]
        store_scores(slot, variant, lax.dot_general(kw, qs, contract_last, preferred_element_type=F32))

    def finish(a, wb, slot):
        s = s_sc[slot]
        m = jnp.max(s, axis=0, keepdims=True)
        if has_sink:
            sink = sink_ref[0]
            m = jnp.maximum(m, sink)
        p = jnp.exp2(s - m)
        if buffered:
            vt = buffered_window(refs[3], refs[4]).T.astype(BF16)
        else:
            vt = jnp.concatenate([refs[2][0, 0, wb + i] for i in range(n_blk)], axis=1)
        if not ones_row:
            vt = jnp.concatenate([vt, jnp.ones((BF16_SUBLANES, win), BF16)], axis=0)
        o = jnp.dot(vt, p.astype(BF16), preferred_element_type=F32)
        l = o[CHUNK:CHUNK + 1, :] if ones_row else o[LANES:LANES + 1, :]
        o = o[:LANES]
        if has_sink:
            l = l + jnp.exp2(sink - m)
        o = (o / l).T
        idx = 0
        for cc in range(cpg):
            rows = pl.ds(aligned(a * rows_g + cc * CHUNK, CHUNK), CHUNK)
            for r in range(n_qblk):
                top = o[idx * CHUNK:(idx + 1) * CHUNK]
                bot = o[(idx + 1) * CHUNK:(idx + 2) * CHUNK]
                if ones_row:
                    bot = pltpu.roll(bot, HALF, 1)
                o_ref[0, rows, r * LANES:(r + 1) * LANES] = jnp.where(lo, top, bot).astype(o_ref.dtype)
                idx += 2

    n_total = q_ref.shape[1] // rows_g
    if n_total == 1:
        scores(0, 0, 0, 0)
        finish(0, 0, 0)
        return
    g_blk = rows_g // LANES
    n_lead = n_var - 1 + (n_var - 1) % 2
    assert (n_var - 1) * g_blk >= n_blk - g_blk and n_total - n_lead >= 2 and BAND_PIPE_UNROLL % 2 == 0

    def window_block(a):
        wb = (a + 1) * g_blk - n_blk
        return max(wb, 0) if isinstance(a, int) else wb

    def stage(a, slot):
        nxt = a + 1
        scores(nxt, window_block(nxt), min(nxt, n_var - 1) if isinstance(nxt, int) else n_var - 1, 1 - slot)
        finish(a, window_block(a), slot)

    scores(0, window_block(0), 0, 0)
    for a in range(n_lead):
        stage(a, a & 1)

    def trip(i, carry):
        for st in range(BAND_PIPE_UNROLL):
            stage(n_lead + BAND_PIPE_UNROLL * i + st, st & 1)
        return carry

    n_stages = n_total - 1 - n_lead
    lax.fori_loop(0, n_stages // BAND_PIPE_UNROLL, trip, 0)
    for a in range(n_total - 1 - n_stages % BAND_PIPE_UNROLL, n_total - 1):
        stage(a, (a - n_lead) & 1)
    finish(n_total - 1, window_block(n_total - 1), (n_total - 1 - n_lead) & 1)


def _band_attention(q, keys, values, toep, sink, *, variants, n_blk, n_groups, n_qblk, cpg, ones_row):
    bsz, s, _ = q.shape
    qw = n_qblk * LANES
    buffered = isinstance(keys, tuple)

    def per_group(a):
        return pl.BlockSpec((1, a.shape[1], LANES), lambda g, b: (b, 0, g))

    in_specs = [pl.BlockSpec((1, s, qw), lambda g, b: (b, 0, g))]
    if buffered:
        args = [q, *keys, *values]
        in_specs += [per_group(a) for a in args[1:]]
    else:
        args = [q, keys, values]
        in_specs += [per_group(keys), pl.BlockSpec((1, 1) + values.shape[2:], lambda g, b: (b, g, 0, 0, 0))]
    for extra in (toep, sink):
        if extra is not None:
            in_specs.append(pl.BlockSpec((1,) + extra.shape[1:], lambda g, b: (g, 0, 0)))
            args.append(extra)
    bias_shape = (n_blk * LANES, cpg * 2 * n_qblk * CHUNK)
    body = functools.partial(_band_body, n_blk=n_blk, n_qblk=n_qblk, cpg=cpg, ones_row=ones_row,
                             has_sink=sink is not None, variants=variants, buffered=buffered)
    return pl.pallas_call(
        body,
        grid=(n_groups, bsz),
        in_specs=in_specs,
        out_specs=pl.BlockSpec((1, s, qw), lambda g, b: (b, 0, g)),
        out_shape=jax.ShapeDtypeStruct(q.shape, BF16),
        scratch_shapes=[pltpu.VMEM((2,) + bias_shape, F32), pltpu.VMEM((len(variants),) + bias_shape, F32)],
        compiler_params=_params(2),
        name="band_attention",
    )(*args)


def _rope_tables(pos, rot, lane_pattern, from_zero=False):
    half = rot // 2
    inv = jnp.power(ROPE_THETA, -jnp.arange(half, dtype=F32) * 2.0 / rot)
    inv_lane, first, second = [], [], []
    for kind, width in lane_pattern:
        if kind == "rot":
            inv_lane += [inv, inv]
            first += [1.0] * half + [0.0] * half
            second += [0.0] * half + [1.0] * half
        else:
            inv_lane.append(jnp.zeros((width,), F32))
            first += [0.0] * width
            second += [0.0] * width
    inv_lane = jnp.concatenate(inv_lane)[None, :]
    n = pos.shape[0]
    if from_zero and n % CHUNK == 0:
        a_hi = (jnp.arange(n // CHUNK, dtype=F32) * CHUNK)[:, None] * inv_lane
        a_lo = jnp.arange(CHUNK, dtype=F32)[:, None] * inv_lane
        c_hi, s_hi = jnp.cos(a_hi)[:, None, :], jnp.sin(a_hi)[:, None, :]
        c_lo, s_lo = jnp.cos(a_lo)[None], jnp.sin(a_lo)[None]
        cos = (c_hi * c_lo - s_hi * s_lo).reshape(n, LANES)
        sin = (s_hi * c_lo + c_hi * s_lo).reshape(n, LANES)
    else:
        ang = pos.astype(F32)[:, None] * inv_lane
        cos, sin = jnp.cos(ang), jnp.sin(ang)
    return jnp.stack([cos, sin * np.asarray(second, np.float32), -sin * np.asarray(first, np.float32)])


A_ROPE_PATTERN = (("pad", A_NOPE), ("rot", A_ROPE), ("pad", LANES - A_QK))
C_ROPE_PATTERN = (("rot", C_ROT), ("pad", HALF - C_ROT)) * 2


def _prep_ab(pre, post, w_in, q_norm, kv_norm, w_uq, w_ukv, rel_bias, w_out):
    d = w_in.shape[0]
    q_lat, c_kv, k_r, g_a, q_b, k_b, v_b, g_b = jnp.split(
        w_in, [384, 640, 672, 1184, 1696, 2208, 2720], axis=1)
    kr_blk = jnp.concatenate([jnp.zeros((d, A_NOPE), F32), k_r, jnp.zeros((d, LANES - A_QK), F32)], axis=1)
    w_in_p = jnp.concatenate([q_lat, c_kv, kr_blk, g_a, g_b, q_b, k_b, v_b], axis=1).astype(BF16)
    w_uq_p = jnp.pad(w_uq.reshape(A_Q_RANK, A_HEADS, A_QK), ((0, 0), (0, 0), (0, LANES - A_QK)))
    w_uq_p = w_uq_p.reshape(A_Q_RANK, A_HEADS * LANES).astype(BF16)
    ukv = w_ukv.reshape(A_KV_RANK, A_HEADS, A_NOPE + A_V)
    w_uk, w_uv = ukv[..., :A_NOPE], ukv[..., A_NOPE:]
    pad_half = ((0, 0), (0, 0), (0, LANES - A_NOPE))
    w_k = jnp.pad(w_uk, pad_half).reshape(A_KV_RANK, A_HEADS * LANES).astype(BF16)
    w_vt = jnp.pad(w_uv, pad_half).reshape(A_KV_RANK, A_HEADS * LANES).T.astype(BF16)
    wkt = jnp.pad(jnp.transpose(w_uk, (1, 2, 0)), ((0, 0), (0, LANES - A_NOPE), (0, 0))).astype(BF16)
    eye = jnp.eye(A_HEADS, dtype=F32)
    wv_s = (jnp.transpose(w_uv, (1, 0, 2))[:, :, None, :] * eye[:, None, :, None]).reshape(
        A_HEADS, A_KV_RANK, A_WIDTH).astype(BF16)
    sel = (jnp.arange(LANES)[:, None] == A_NOPE + jnp.arange(A_ROPE)[None, :]).astype(BF16)
    rel_bias = rel_bias * LOG2E
    win_p, win_s = B_WIN_BLOCKS * LANES, B_SAMPLE_BLOCKS * LANES
    r0 = win_p - CHUNK
    x_len = r0 + win_p
    n_vec = x_len + CHUNK
    n_hi = r0 + CHUNK - 1 - B_MAX_REL
    n_lo = n_vec - n_hi - (2 * B_MAX_REL + 1)
    vec = jnp.concatenate([jnp.broadcast_to(rel_bias[:, -1:], (B_HEADS, n_hi)), rel_bias[:, ::-1],
                           jnp.broadcast_to(rel_bias[:, :1], (B_HEADS, n_lo))], axis=1)
    skew = jnp.tile(vec, (1, CHUNK))[:, :CHUNK * (n_vec - 1)].reshape(B_HEADS, CHUNK, n_vec - 1)
    toep = skew[:, :, CHUNK - 1:CHUNK - 1 + x_len]
    toep = jnp.transpose(toep.reshape(B_HEADS // 2, 2, CHUNK, x_len), (0, 3, 1, 2)).reshape(B_HEADS // 2, x_len, LANES)
    var_p = tuple(tuple((r0 - c * CHUNK, c - B_PAST_CHUNKS, c)
                        for c in range(B_GROUP_CHUNKS * v, B_GROUP_CHUNKS * (v + 1))) for v in range(B_VARIANTS))
    var_s = (((r0 - (win_s - CHUNK), 1, B_PAST_CHUNKS + 1),),)
    return dict(pre=pre[None], w_in=w_in_p, q_norm=q_norm[None], kv_norm=kv_norm[None], w_uq=w_uq_p, w_k=w_k,
                w_vt=w_vt, wkt=wkt, wv_s=wv_s, sel=sel, toep=toep, var_p=var_p, var_s=var_s,
                w_out=w_out.astype(BF16), post=post[None])


def _prep_c(pre, post, w_in, sinks, w_out):
    q, k, v, g = jnp.split(w_in, [1024, 1152, 1280], axis=1)
    k0, k1 = k[:, :C_HEAD_DIM], k[:, C_HEAD_DIM:]
    v0, v1 = v[:, :C_HEAD_DIM], v[:, C_HEAD_DIM:]
    zero = jnp.zeros_like(v0)
    w_in_p = jnp.concatenate([q, k0, k0, k1, k1, v0, zero, v1, zero, g], axis=1).astype(BF16)
    var_p = tuple(tuple((0, c - C_PAST_CHUNKS, c) for c in range(C_GROUP_CHUNKS * v, C_GROUP_CHUNKS * (v + 1)))
                  for v in range(C_VARIANTS))
    var_s = (((0, 1, C_PAST_CHUNKS + 1),),)
    sink_row = jnp.repeat((sinks * LOG2E).reshape(C_KV_HEADS, C_GROUP), CHUNK, axis=1)[:, None, :]
    return dict(pre=pre[None], w_in=w_in_p, var_p=var_p, var_s=var_s, sink_s=sink_row,
                sink_p=jnp.tile(sink_row, (1, 1, C_GROUP_CHUNKS)), w_out=w_out.astype(BF16), post=post[None])


def _dup_heads(x):
    return jnp.concatenate([x[:, :, 0], x[:, :, 0], x[:, :, 1], x[:, :, 1]], axis=-1)


def kernel(x_prompt, x_sample, cache_a_ckv, cache_a_krope, cache_b_k, cache_b_v, cache_c_k, cache_c_v,
           ab_pre_norm, ab_post_norm, ab_w_in, ab_q_norm, ab_kv_norm, ab_w_uq, ab_w_ukv, ab_rel_bias, ab_w_out,
           c_pre_norm, c_post_norm, c_w_in, c_sinks, c_w_out):
    bsz, seq, _ = x_prompt.shape
    dbs, dseq, _ = x_sample.shape
    past = cache_a_ckv.shape[2]
    n_s = dbs * dseq
    pos_p = jnp.arange(seq, dtype=jnp.int32)
    pos_s = jnp.tile(past + jnp.arange(dseq, dtype=jnp.int32), dbs)
    wab = _prep_ab(ab_pre_norm[0], ab_post_norm[0], ab_w_in[0], ab_q_norm[0], ab_kv_norm[0], ab_w_uq[0],
                   ab_w_ukv[0], ab_rel_bias[0], ab_w_out[0])
    wc = _prep_c(c_pre_norm[0], c_post_norm[0], c_w_in[0], c_sinks[0], c_w_out[0])
    b_tail = min(B_PAST_CHUNKS * CHUNK, seq)
    c_tail = min(C_WINDOW, seq)
    tile = 512

    rope_a_p = _rope_tables(pos_p, A_ROPE, A_ROPE_PATTERN, from_zero=True)
    (qa, ka, vt, qb, kb, vbt, sg, c_new_p, kr_new_p, kb_tail, vb_tail) = _ab_in_proj(
        x_prompt, rope_a_p, wab, tm=256, tail_len=b_tail, tkv=tile)
    o_a = _mla_prompt(qa, ka, vt, tq=tile)
    o_b = _band_attention(qb, kb, vbt, wab["toep"], None, variants=wab["var_p"], n_blk=B_WIN_BLOCKS,
                          n_groups=B_HEADS // 2, n_qblk=1, cpg=B_GROUP_CHUNKS, ones_row=False)

    rope_a_s = _rope_tables(pos_s, A_ROPE, A_ROPE_PATTERN)
    xs = x_sample.reshape(1, n_s, D_MODEL)
    (qa_s, _, _, qb_s, _, _, sg_s, c_new_s, kr_new_s, kb_s32, vb_s32) = _ab_in_proj(
        xs, rope_a_s, wab, tm=n_s, tail_len=n_s, tkv=n_s)
    o_a_s = _mla_sample(qa_s.reshape(dbs, dseq, -1), cache_a_ckv[0], cache_a_krope[0],
                        c_new_s.reshape(dbs, dseq, -1), kr_new_s.reshape(dbs, dseq, -1),
                        wab["wkt"], wab["wv_s"], wab["sel"], tk=512)
    wb = cache_b_k.shape[2]
    o_b_s = _band_attention(qb_s.reshape(dbs, dseq, -1),
                            (cache_b_k[0].reshape(dbs, wb, B_WIDTH), kb_s32.reshape(dbs, dseq, -1)),
                            (cache_b_v[0].reshape(dbs, wb, B_WIDTH), vb_s32.reshape(dbs, dseq, -1)),
                            wab["toep"], None, variants=wab["var_s"], n_blk=B_SAMPLE_BLOCKS, n_groups=B_HEADS // 2,
                            n_qblk=1, cpg=1, ones_row=False)

    rope_c_p = _rope_tables(pos_p, C_ROT, C_ROPE_PATTERN, from_zero=True)
    h1_p, qc, kc, vct, sgc, kc_tail, vc_tail = _c_in_proj(
        [o_a, o_b], sg, wab["w_out"], wab["post"], x_prompt, rope_c_p, wc, tm=tile, tail_len=c_tail)
    o_c = _band_attention(qc, kc, vct, None, wc["sink_p"], variants=wc["var_p"], n_blk=C_WIN_BLOCKS,
                          n_groups=C_KV_HEADS, n_qblk=C_GROUP // 2, cpg=C_GROUP_CHUNKS, ones_row=True)
    h2_p = _out_proj([o_c], sgc, wc["w_out"], wc["post"], h1_p, tm=tile)

    rope_c_s = _rope_tables(pos_s, C_ROT, C_ROPE_PATTERN)
    h1_s, qc_s, kc_s, _, sgc_s, kc_s32, vc_s32 = _c_in_proj(
        [o_a_s.reshape(1, n_s, -1), o_b_s.reshape(1, n_s, -1)], sg_s, wab["w_out"], wab["post"], xs, rope_c_s, wc,
        tm=n_s, tail_len=n_s)
    wcw = cache_c_k.shape[2]
    win_c = C_SAMPLE_BLOCKS * LANES
    n_pad = win_c - wcw - dseq
    kcb = jnp.concatenate([jnp.zeros((dbs, n_pad, C_KV_HEADS * LANES), BF16), _dup_heads(cache_c_k[0]).astype(BF16),
                           kc_s.reshape(dbs, dseq, -1)], axis=1)
    vcb = jnp.concatenate([jnp.zeros((dbs, n_pad, C_KV_HEADS, C_HEAD_DIM), F32), cache_c_v[0],
                           vc_s32.reshape(dbs, dseq, C_KV_HEADS, C_HEAD_DIM)], axis=1)
    vcb_t = jnp.concatenate([jnp.transpose(vcb, (0, 2, 3, 1)), jnp.ones((dbs, C_KV_HEADS, 1, win_c), F32),
                             jnp.zeros((dbs, C_KV_HEADS, LANES - C_HEAD_DIM - 1, win_c), F32)], axis=2)
    vcb_t = jnp.transpose(vcb_t.astype(BF16).reshape(dbs, C_KV_HEADS, LANES, C_SAMPLE_BLOCKS, LANES), (0, 1, 3, 2, 4))
    o_c_s = _band_attention(qc_s.reshape(dbs, dseq, -1), kcb, vcb_t, None, wc["sink_s"], variants=wc["var_s"],
                            n_blk=C_SAMPLE_BLOCKS, n_groups=C_KV_HEADS, n_qblk=C_GROUP // 2, cpg=1, ones_row=True)
    h2_s = _out_proj([o_c_s.reshape(1, n_s, -1)], sgc_s, wc["w_out"], wc["post"], h1_s, tm=n_s)

    def roll_in(buf, new):
        return jnp.concatenate([buf, new], axis=1)[:, -buf.shape[1]:][None]

    return (h2_p, h2_s.reshape(dbs, dseq, D_MODEL),
            c_new_p[None], kr_new_p[None],
            kb_tail.reshape(1, bsz, b_tail, B_HEADS, B_HEAD_DIM), vb_tail.reshape(1, bsz, b_tail, B_HEADS, B_HEAD_DIM),
            kc_tail.reshape(1, bsz, c_tail, C_KV_HEADS, C_HEAD_DIM), vc_tail.reshape(1, bsz, c_tail, C_KV_HEADS, C_HEAD_DIM),
            c_new_s.reshape(1, dbs, dseq, A_KV_RANK), kr_new_s.reshape(1, dbs, dseq, A_ROPE),
            roll_in(cache_b_k[0], kb_s32.reshape(dbs, dseq, B_HEADS, B_HEAD_DIM)),
            roll_in(cache_b_v[0], vb_s32.reshape(dbs, dseq, B_HEADS, B_HEAD_DIM)),
            roll_in(cache_c_k[0], kc_s32.reshape(dbs, dseq, C_KV_HEADS, C_HEAD_DIM)),
            roll_in(cache_c_v[0], vc_s32.reshape(dbs, dseq, C_KV_HEADS, C_HEAD_DIM)))
```

```python
import functools

import jax
import jax.numpy as jnp
import numpy as np
from jax import lax
from jax.experimental import pallas as pl
from jax.experimental.pallas import tpu as pltpu

F32 = jnp.float32
BF16 = jnp.bfloat16

D_MODEL = 1024
CHUNK = 64
ROPE_THETA = 500000.0
RMS_EPS = 1e-6
NEG_INF = -1e30

A_HEADS = 8
A_NOPE = 64
A_ROPE = 32
A_QK = A_NOPE + A_ROPE
A_V = 64
A_Q_RANK = 384
A_KV_RANK = 256
A_WIDTH = A_HEADS * A_V
LOG2E = 1.4426950408889634
A_SCALE = A_QK ** -0.5 * LOG2E

B_HEADS = 8
B_HEAD_DIM = 64
B_WIDTH = B_HEADS * B_HEAD_DIM
B_PAST_CHUNKS = 8
B_MAX_REL = 128
B_SCALE = B_HEAD_DIM ** -0.5 * LOG2E

C_HEADS = 16
C_KV_HEADS = 2
C_GROUP = C_HEADS // C_KV_HEADS
C_HEAD_DIM = 64
C_WIDTH = C_HEADS * C_HEAD_DIM
C_WINDOW = 128
C_PAST_CHUNKS = C_WINDOW // CHUNK
C_ROT = C_HEAD_DIM // 4
C_SCALE = C_HEAD_DIM ** -0.5 * LOG2E

LANES = 128
HALF = LANES // 2
BF16_SUBLANES = 16
VMEM_LIMIT = 56 * 1024 * 1024
A_MASK_CHUNKS = 8
assert A_QK + A_MASK_CHUNKS <= LANES
A_PIPE_UNROLL = 4
BAND_PIPE_UNROLL = 4
B_GROUP_CHUNKS = 4
C_GROUP_CHUNKS = 2
B_WIN_BLOCKS = (B_PAST_CHUNKS + B_GROUP_CHUNKS) * CHUNK // LANES
C_WIN_BLOCKS = (C_PAST_CHUNKS + C_GROUP_CHUNKS) * CHUNK // LANES
B_VARIANTS = B_PAST_CHUNKS // B_GROUP_CHUNKS + 1
C_VARIANTS = C_PAST_CHUNKS // C_GROUP_CHUNKS + 1
B_SAMPLE_BLOCKS = (B_PAST_CHUNKS + 2) * CHUNK // LANES
C_SAMPLE_BLOCKS = (C_PAST_CHUNKS + 2) * CHUNK // LANES

AB_Q0, AB_C0, AB_KR0, AB_G0, AB_QB0, AB_KB0, AB_VB0, AB_NZ = 0, 384, 640, 768, 1792, 2304, 2816, 3328
C_Q0, C_K0, C_V0, C_G0, C_NZ = 0, 1024, 1280, 1536, 2560


def _params(n_axes):
    return pltpu.CompilerParams(dimension_semantics=("arbitrary",) * n_axes, vmem_limit_bytes=VMEM_LIMIT)


def _rms(x, g):
    return x * lax.rsqrt(jnp.mean(x * x, axis=-1, keepdims=True) + RMS_EPS) * g


def _rope_block(blk, rope_ref, shift):
    return (blk * rope_ref[0] + pltpu.roll(blk, shift, 1) * rope_ref[1]
            + pltpu.roll(blk, LANES - shift, 1) * rope_ref[2])


def _tail_spec(n_tiles, tm, tail_len, width):
    tb = min(tail_len, tm)
    n_blk = tail_len // tb
    return tb, pl.BlockSpec((1, tb, width), lambda b, i: (b, jnp.maximum(i - (n_tiles - n_blk), 0), 0))


def _ab_in_body(x_ref, pre_ref, w_ref, qn_ref, kvn_ref, wuq_ref, wk_ref, wvt_ref, rope_ref,
                qa_ref, ka_ref, vt_ref, qb_ref, kb_ref, vbt_ref, sg_ref, c_ref, kr_ref, kb_tail_ref, vb_tail_ref):
    tm = x_ref.shape[1]
    tb = kb_tail_ref.shape[1]
    xn = _rms(x_ref[0], pre_ref[...]).astype(BF16)
    z = jnp.dot(xn, w_ref[...], preferred_element_type=F32)

    qn = _rms(z[:, AB_Q0:AB_C0], qn_ref[...]).astype(BF16)
    qa = jnp.dot(qn, wuq_ref[...], preferred_element_type=F32) * A_SCALE
    for h in range(A_HEADS):
        blk = slice(h * LANES, (h + 1) * LANES)
        qa_ref[0, :, blk] = _rope_block(qa[:, blk], rope_ref, A_ROPE // 2).astype(BF16)

    c_new = _rms(z[:, AB_C0:AB_KR0], kvn_ref[...])
    c_ref[0] = c_new
    cb = c_new.astype(BF16)
    krot = _rope_block(z[:, AB_KR0:AB_G0], rope_ref, A_ROPE // 2)
    kr_ref[0] = krot[:, A_NOPE:A_NOPE + A_ROPE]
    kn = jnp.dot(cb, wk_ref[...], preferred_element_type=F32)
    row = pl.program_id(1) * tm + lax.broadcasted_iota(jnp.int32, (tm, LANES), 0)
    lane = lax.broadcasted_iota(jnp.int32, (tm, LANES), 1)
    k_shared = krot + jnp.where(lane - A_QK == (row // CHUNK) % A_MASK_CHUNKS, 1.0, 0.0)
    for h in range(A_HEADS):
        blk = slice(h * LANES, (h + 1) * LANES)
        ka_ref[0, :, blk] = (kn[:, blk] + k_shared).astype(BF16)
    vt = jnp.dot(wvt_ref[...], c_new.T.astype(BF16), preferred_element_type=F32)
    row = lax.broadcasted_iota(jnp.int32, vt.shape, 0)
    vt_ref[0, 0] = jnp.where((row & (LANES - 1)) == A_V, 1.0, vt).astype(BF16)

    g = z[:, AB_G0:AB_QB0]
    sg_ref[0] = (g * jax.nn.sigmoid(g)).astype(BF16)
    qb_ref[0] = (z[:, AB_QB0:AB_KB0] * B_SCALE).astype(BF16)
    kb = z[:, AB_KB0:AB_VB0]
    vb = z[:, AB_VB0:AB_NZ]
    kb_ref[0] = kb.astype(BF16)
    vbt = vb.T.astype(BF16)
    for hp in range(B_HEADS // 2):
        for j in range(tm // LANES):
            vbt_ref[0, hp, j] = vbt[hp * LANES:(hp + 1) * LANES, j * LANES:(j + 1) * LANES]
    kb_tail_ref[0] = kb[tm - tb:, :]
    vb_tail_ref[0] = vb[tm - tb:, :]


def _ab_in_proj(x, rope, w, tm, tail_len, tkv):
    bsz, s, _ = x.shape
    nt = s // tm
    per_kv = tkv // tm
    tb, tail_spec = _tail_spec(nt, tm, tail_len, B_WIDTH)

    def full(a):
        return pl.BlockSpec(a.shape, lambda b, i: (0,) * a.ndim, pipeline_mode=pl.Buffered(1))

    def rows(width):
        return pl.BlockSpec((1, tm, width), lambda b, i: (b, i, 0))

    weights = (w["pre"], w["w_in"], w["q_norm"], w["kv_norm"], w["w_uq"], w["w_k"], w["w_vt"])
    vt_spec = pl.BlockSpec((1, 1, A_HEADS * LANES, tm), lambda b, i: (b, i // per_kv, 0, i % per_kv))
    out_shape = (
        jax.ShapeDtypeStruct((bsz, s, A_HEADS * LANES), BF16),
        jax.ShapeDtypeStruct((bsz, s, A_HEADS * LANES), BF16),
        jax.ShapeDtypeStruct((bsz, s // tkv, A_HEADS * LANES, tkv), BF16),
        jax.ShapeDtypeStruct((bsz, s, B_WIDTH), BF16),
        jax.ShapeDtypeStruct((bsz, s, B_WIDTH), BF16),
        jax.ShapeDtypeStruct((bsz, B_HEADS // 2, s // LANES, LANES, LANES), BF16),
        jax.ShapeDtypeStruct((bsz, s, A_WIDTH + B_WIDTH), BF16),
        jax.ShapeDtypeStruct((bsz, s, A_KV_RANK), F32),
        jax.ShapeDtypeStruct((bsz, s, A_ROPE), F32),
        jax.ShapeDtypeStruct((bsz, tail_len, B_WIDTH), F32),
        jax.ShapeDtypeStruct((bsz, tail_len, B_WIDTH), F32),
    )
    vbt_spec = pl.BlockSpec((1, B_HEADS // 2, tm // LANES, LANES, LANES), lambda b, i: (b, 0, i, 0, 0))
    out_specs = (rows(1024), rows(1024), vt_spec, rows(512), rows(512), vbt_spec, rows(1024),
                 rows(A_KV_RANK), rows(A_ROPE), tail_spec, tail_spec)
    return pl.pallas_call(
        _ab_in_body,
        grid=(bsz, nt),
        in_specs=[rows(D_MODEL)] + [full(a) for a in weights]
        + [pl.BlockSpec((3, tm, LANES), lambda b, i: (0, i, 0))],
        out_specs=out_specs,
        out_shape=out_shape,
        compiler_params=_params(2),
        name="ab_in_proj",
    )(x, *weights, rope)


def _mix_out(o_refs, sg_ref, w_ref, g_ref, h_ref):
    o = jnp.concatenate([r[0].astype(F32) for r in o_refs], axis=-1) if len(o_refs) > 1 else o_refs[0][0].astype(F32)
    mixed = (o * sg_ref[0].astype(F32)).astype(BF16)
    y = jnp.dot(mixed, w_ref[...], preferred_element_type=F32)
    return h_ref[0] + _rms(y, g_ref[...])


def _c_in_body(*refs):
    n_o = len(refs) - 14
    o_refs = refs[:n_o]
    (sg0_ref, w0_ref, post0_ref, x_ref, pre_ref, w_ref, rope_ref,
     h_ref, q_ref, k_ref, vt_ref, sg_ref, k_tail_ref, v_tail_ref) = refs[n_o:]
    tm = x_ref.shape[1]
    tb = k_tail_ref.shape[1]
    h = _mix_out(o_refs, sg0_ref, w0_ref, post0_ref, x_ref)
    h_ref[0] = h
    xn = _rms(h, pre_ref[...]).astype(BF16)
    z = jnp.dot(xn, w_ref[...], preferred_element_type=F32)
    half_rot = C_ROT // 2
    for j in range(C_WIDTH // LANES):
        blk = slice(C_Q0 + j * LANES, C_Q0 + (j + 1) * LANES)
        q_ref[0, :, j * LANES:(j + 1) * LANES] = (_rope_block(z[:, blk], rope_ref, half_rot) * C_SCALE).astype(BF16)
    kd = [_rope_block(z[:, C_K0 + j * LANES:C_K0 + (j + 1) * LANES], rope_ref, half_rot) for j in range(C_KV_HEADS)]
    vz = [z[:, C_V0 + j * LANES:C_V0 + (j + 1) * LANES] for j in range(C_KV_HEADS)]
    row = lax.broadcasted_iota(jnp.int32, (LANES, tm), 0)
    for j in range(C_KV_HEADS):
        k_ref[0, :, j * LANES:(j + 1) * LANES] = kd[j].astype(BF16)
        vt = jnp.where(row == C_HEAD_DIM, 1.0, vz[j].T).astype(BF16)
        for i in range(tm // LANES):
            vt_ref[0, j, i] = vt[:, i * LANES:(i + 1) * LANES]
    g = z[:, C_G0:C_NZ]
    sg_ref[0] = (g * jax.nn.sigmoid(g)).astype(BF16)
    lo = lax.broadcasted_iota(jnp.int32, (1, LANES), 1) < HALF
    k_tail_ref[0] = jnp.where(lo, kd[0], kd[1])[tm - tb:, :]
    v_tail_ref[0] = (vz[0] + pltpu.roll(vz[1], HALF, 1))[tm - tb:, :]


def _c_in_proj(o_parts, sg0, w_out0, post0, x, rope, w, tm, tail_len):
    bsz, s, _ = x.shape
    nt = s // tm
    tb, tail_spec = _tail_spec(nt, tm, tail_len, LANES)

    def full(a):
        return pl.BlockSpec(a.shape, lambda b, i: (0,) * a.ndim)

    def rows(width):
        return pl.BlockSpec((1, tm, width), lambda b, i: (b, i, 0))

    out_shape = (
        jax.ShapeDtypeStruct((bsz, s, D_MODEL), F32),
        jax.ShapeDtypeStruct((bsz, s, C_WIDTH), BF16),
        jax.ShapeDtypeStruct((bsz, s, C_KV_HEADS * LANES), BF16),
        jax.ShapeDtypeStruct((bsz, C_KV_HEADS, s // LANES, LANES, LANES), BF16),
        jax.ShapeDtypeStruct((bsz, s, C_WIDTH), BF16),
        jax.ShapeDtypeStruct((bsz, tail_len, LANES), F32),
        jax.ShapeDtypeStruct((bsz, tail_len, LANES), F32),
    )
    return pl.pallas_call(
        _c_in_body,
        grid=(bsz, nt),
        in_specs=[rows(o.shape[-1]) for o in o_parts]
        + [rows(D_MODEL), full(w_out0), full(post0), rows(D_MODEL), full(w["pre"]), full(w["w_in"]),
           pl.BlockSpec((3, tm, LANES), lambda b, i: (0, i, 0))],
        out_specs=(rows(D_MODEL), rows(C_WIDTH), rows(256),
                   pl.BlockSpec((1, C_KV_HEADS, tm // LANES, LANES, LANES), lambda b, i: (b, 0, i, 0, 0)),
                   rows(C_WIDTH), tail_spec, tail_spec),
        out_shape=out_shape,
        compiler_params=_params(2),
        name="c_in_proj",
    )(*o_parts, sg0, w_out0, post0, x, w["pre"], w["w_in"], rope)


def _out_body(*refs):
    o_refs, (sg_ref, w_ref, g_ref, h_ref, out_ref) = refs[:-5], refs[-5:]
    out_ref[0] = _mix_out(o_refs, sg_ref, w_ref, g_ref, h_ref)


def _out_proj(o_parts, sg, w_out, post_g, h, tm):
    bsz, s, _ = h.shape

    def rows(width):
        return pl.BlockSpec((1, tm, width), lambda b, i: (b, i, 0))

    def full(a):
        return pl.BlockSpec(a.shape, lambda b, i: (0,) * a.ndim)

    return pl.pallas_call(
        _out_body,
        grid=(bsz, s // tm),
        in_specs=[rows(o.shape[-1]) for o in o_parts] + [rows(D_MODEL), full(w_out), full(post_g), rows(D_MODEL)],
        out_specs=rows(D_MODEL),
        out_shape=jax.ShapeDtypeStruct(h.shape, F32),
        compiler_params=_params(2),
        name="out_proj",
    )(*o_parts, sg, w_out, post_g, h)


def _mla_prompt_body(q_ref, k_ref, vt_ref, qmask_ref, o_ref, m_sc, acc_sc, s_sc, *, tq):
    n_tiles = q_ref.shape[1] // tq
    n_pairs = n_tiles * (n_tiles + 1) // 2
    assert A_PIPE_UNROLL % 2 == 0 and n_pairs % A_PIPE_UNROLL == 0 and n_pairs >= 2 * A_PIPE_UNROLL
    m_sc[...] = jnp.full(m_sc.shape, NEG_INF, F32)
    acc_sc[...] = jnp.zeros(acc_sc.shape, F32)
    contract_last = (((1,), (1,)), ((), ()))

    def scores(qi, j, slot):
        q_start = pl.multiple_of(qi * tq, tq)
        k_start = pl.multiple_of(j * tq, tq)
        qmask = jnp.where(j == qi, qmask_ref[...], jnp.zeros_like(qmask_ref[...]))
        for hh in range(2):
            blk = slice(hh * LANES, (hh + 1) * LANES)
            q = q_ref[0, pl.ds(q_start, tq), blk] + qmask
            s_sc[slot, hh] = lax.dot_general(k_ref[0, pl.ds(k_start, tq), blk], q, contract_last,
                                             preferred_element_type=F32)

    def consume(qi, j, slot):
        for hh in range(2):
            s = s_sc[slot, hh]
            vt = vt_ref[0, j, hh * LANES:(hh + 1) * LANES, :]
            m_old = m_sc[qi, hh]
            m_new = jnp.maximum(m_old, jnp.max(s, axis=0, keepdims=True))
            p = jnp.exp2(s - m_new).astype(BF16)
            acc_sc[qi, hh] = jnp.exp2(m_old - m_new) * acc_sc[qi, hh] + jnp.dot(vt, p, preferred_element_type=F32)
            m_sc[qi, hh] = m_new

    def advance(qi, j):
        last = j == qi
        return jnp.where(last, qi + 1, qi), jnp.where(last, 0, j + 1)

    def stages(cur, count):
        for st in range(count):
            nxt = advance(*cur)
            scores(*nxt, 1 - (st & 1))
            consume(*cur, st & 1)
            cur = nxt
        return cur

    zero = jnp.int32(0)
    scores(zero, zero, 0)
    cur = lax.fori_loop(0, n_pairs // A_PIPE_UNROLL - 1, lambda i, c: stages(c, A_PIPE_UNROLL), (zero, zero))
    cur = stages(cur, A_PIPE_UNROLL - 1)
    consume(*cur, (A_PIPE_UNROLL - 1) & 1)

    lo = lax.broadcasted_iota(jnp.int32, (1, LANES), 1) < HALF

    def write_tile(qi, carry):
        outs = []
        for hh in range(2):
            acc = acc_sc[qi, hh]
            outs.append((acc / acc[A_V:A_V + 1, :]).T)
        o_ref[0, pl.ds(pl.multiple_of(qi * tq, tq), tq), :] = jnp.where(
            lo, outs[0], pltpu.roll(outs[1], HALF, 1)).astype(o_ref.dtype)
        return carry

    lax.fori_loop(0, n_tiles, write_tile, 0)


def _mla_prompt(qa, ka, vt, tq):
    bsz, s, _ = qa.shape
    pairs = A_HEADS // 2
    n_tiles = s // tq
    assert vt.shape[3] == tq and tq == A_MASK_CHUNKS * CHUNK
    q_chunk = np.arange(tq)[:, None] // CHUNK
    lane_chunk = np.arange(LANES)[None, :] - A_QK
    qmask = np.where((lane_chunk > q_chunk) & (lane_chunk < A_MASK_CHUNKS), NEG_INF, 0.0).astype(np.float32)
    return pl.pallas_call(
        functools.partial(_mla_prompt_body, tq=tq),
        grid=(bsz, pairs),
        in_specs=[pl.BlockSpec((1, s, 2 * LANES), lambda b, h: (b, 0, h)),
                  pl.BlockSpec((1, s, 2 * LANES), lambda b, h: (b, 0, h)),
                  pl.BlockSpec((1, n_tiles, 2 * LANES, tq), lambda b, h: (b, 0, h, 0)),
                  pl.BlockSpec((tq, LANES), lambda b, h: (0, 0))],
        out_specs=pl.BlockSpec((1, s, LANES), lambda b, h: (b, 0, h)),
        out_shape=jax.ShapeDtypeStruct((bsz, s, A_WIDTH), BF16),
        scratch_shapes=[pltpu.VMEM((n_tiles, 2, 1, tq), F32), pltpu.VMEM((n_tiles, 2, LANES, tq), F32),
                        pltpu.VMEM((2, 2, tq, tq), F32)],
        compiler_params=_params(2),
        name="mla_prompt",
    )(qa, ka, vt, jnp.asarray(qmask, BF16))


def _mla_sample_body(q_ref, cc_ref, ckr_ref, cn_ref, krn_ref, wkt_ref, wv_ref, sel_ref, o_ref,
                     qabs_sc, qr_sc, m_sc, acc_sc, *, tk):
    t = q_ref.shape[1]
    past = cc_ref.shape[1]
    for h in range(A_HEADS):
        qh = q_ref[0, :, h * LANES:(h + 1) * LANES]
        rows = slice(h * t, (h + 1) * t)
        qabs_sc[rows, :] = jnp.dot(qh, wkt_ref[h], preferred_element_type=F32).astype(BF16)
        qr_sc[rows, :] = jnp.dot(qh, sel_ref[...], preferred_element_type=F32).astype(BF16)
    m_sc[...] = jnp.full(m_sc.shape, NEG_INF, F32)
    acc_sc[...] = jnp.zeros(acc_sc.shape, F32)
    contract_last = (((1,), (1,)), ((), ()))

    def scores(c_t, kr_t):
        return (lax.dot_general(c_t.astype(BF16), qabs_sc[...], contract_last, preferred_element_type=F32)
                + lax.dot_general(kr_t.astype(BF16), qr_sc[...], contract_last, preferred_element_type=F32))

    def update(s, c_t):
        n_keys = c_t.shape[0]
        m_old = m_sc[...]
        m_new = jnp.maximum(m_old, jnp.max(s, axis=0, keepdims=True))
        p = jnp.exp2(s - m_new).astype(BF16)
        ct = jnp.concatenate([c_t.T.astype(BF16), jnp.ones((BF16_SUBLANES, n_keys), BF16)], axis=0)
        acc_sc[...] = jnp.exp2(m_old - m_new) * acc_sc[...] + jnp.dot(ct, p, preferred_element_type=F32)
        m_sc[...] = m_new

    def tile(j):
        rows = pl.ds(j * tk, tk)
        return cc_ref[0, rows, :], ckr_ref[0, rows, :]

    n_tiles = past // tk
    cur = tile(0)
    s = scores(*cur)
    for j in range(n_tiles):
        nxt = tile(j + 1) if j + 1 < n_tiles else (cn_ref[0], krn_ref[0])
        s_next = scores(*nxt)
        update(s, cur[0])
        cur, s = nxt, s_next
    update(s, cur[0])

    acc = acc_sc[...]
    o_lat = (acc[:A_KV_RANK] / acc[A_KV_RANK:A_KV_RANK + 1]).T.astype(BF16)
    out = jnp.zeros((t, A_WIDTH), F32)
    for h in range(A_HEADS):
        out = out + jnp.dot(o_lat[h * t:(h + 1) * t, :], wv_ref[h], preferred_element_type=F32)
    o_ref[0] = out.astype(o_ref.dtype)


def _mla_sample(qa, cache_c, cache_kr, c_new, kr_new, wkt, wv, sel, tk):
    bsz, t, _ = qa.shape
    past = cache_c.shape[1]

    def per_b(shape):
        return pl.BlockSpec((1,) + shape, lambda b: (b, 0, 0))

    def full(a):
        return pl.BlockSpec(a.shape, lambda b: (0,) * a.ndim)

    rows = A_HEADS * t
    return pl.pallas_call(
        functools.partial(_mla_sample_body, tk=tk),
        grid=(bsz,),
        in_specs=[per_b((t, A_HEADS * LANES)), per_b((past, A_KV_RANK)), per_b((past, A_ROPE)),
                  per_b((t, A_KV_RANK)), per_b((t, A_ROPE)), full(wkt), full(wv), full(sel)],
        out_specs=per_b((t, A_WIDTH)),
        out_shape=jax.ShapeDtypeStruct((bsz, t, A_WIDTH), BF16),
        scratch_shapes=[pltpu.VMEM((rows, A_KV_RANK), BF16), pltpu.VMEM((rows, A_ROPE), BF16),
                        pltpu.VMEM((1, rows), F32), pltpu.VMEM((A_KV_RANK + BF16_SUBLANES, rows), F32)],
        compiler_params=_params(1),
        name="mla_sample",
    )(qa, cache_c, cache_kr, c_new, kr_new, wkt, wv, sel)


def _band_body(*refs, n_blk, n_qblk, cpg, ones_row, has_sink, variants, buffered=False):
    refs = list(refs)
    n_in = 5 if buffered else 3
    q_ref = refs[0]
    o_ref, s_sc, bias_sc = refs[-3:]
    extra = refs[n_in:-3]
    sink_ref = extra.pop() if has_sink else None
    toep_ref = extra.pop() if extra else None
    win = n_blk * LANES
    rows_g = cpg * CHUNK
    n_var = len(variants)
    qpc = 2 * n_qblk * CHUNK
    lo = lax.broadcasted_iota(jnp.int32, (CHUNK, LANES), 1) < HALF
    contract_last = (((1,), (1,)), ((), ()))

    if toep_ref is not None:
        key_chunk = lax.broadcasted_iota(jnp.int32, (win, LANES), 0) // CHUNK
        for v, chunks in enumerate(variants):
            for cc, (x0, lo_chunk, hi_chunk) in enumerate(chunks):
                tile = jnp.where((key_chunk >= lo_chunk) & (key_chunk <= hi_chunk), toep_ref[0, x0:x0 + win, :], NEG_INF)
                for j in range(qpc // LANES):
                    bias_sc[v, :, cc * qpc + j * LANES:cc * qpc + (j + 1) * LANES] = tile

    def store_scores(slot, variant, s):
        if toep_ref is not None:
            s_sc[slot] = s + bias_sc[variant]
            return
        s_sc[slot] = s
        for cc, (_, lo_chunk, hi_chunk) in enumerate(variants[variant]):
            cols = slice(cc * qpc, (cc + 1) * qpc)
            for r0, r1 in ((0, lo_chunk * CHUNK), ((hi_chunk + 1) * CHUNK, win)):
                if r1 > max(r0, 0):
                    s_sc[slot, max(r0, 0):r1, cols] = jnp.full((r1 - max(r0, 0), qpc), NEG_INF, F32)

    def aligned(x, n):
        return x if isinstance(x, int) else pl.multiple_of(x, n)

    def buffered_window(buf_ref, new_ref):
        buf, new = buf_ref[0], new_ref[0]
        pad = jnp.zeros((win - buf.shape[0] - new.shape[0], LANES), F32)
        return jnp.concatenate([pad, buf, new], axis=0)

    def scores(a, wb, variant, slot):
        pieces = []
        for cc in range(cpg):
            rows = pl.ds(aligned(a * rows_g + cc * CHUNK, CHUNK), CHUNK)
            for r in range(n_qblk):
                qblk = q_ref[0, rows, r * LANES:(r + 1) * LANES]
                zero = jnp.zeros_like(qblk)
                pieces += [jnp.where(lo, qblk, zero), jnp.where(lo, zero, qblk)]
        qs = jnp.concatenate(pieces, axis=0)
        if buffered:
            kw = buffered_window(refs[1], refs[2]).astype(BF16)
        else:
            kw = refs[1][0, pl.ds(aligned(wb * LANES, LANES), win), :]
        store_scores(slot, variant, lax.dot_general(kw, qs, contract_last, preferred_element_type=F32))

    def finish(a, wb, slot):
        s = s_sc[slot]
        m = jnp.max(s, axis=0, keepdims=True)
        if has_sink:
            sink = sink_ref[0]
            m = jnp.maximum(m, sink)
        p = jnp.exp2(s - m)
        if buffered:
            vt = buffered_window(refs[3], refs[4]).T.astype(BF16)
        else:
            vt = jnp.concatenate([refs[2][0, 0, wb + i] for i in range(n_blk)], axis=1)
        if not ones_row:
            vt = jnp.concatenate([vt, jnp.ones((BF16_SUBLANES, win), BF16)], axis=0)
        o = jnp.dot(vt, p.astype(BF16), preferred_element_type=F32)
        l = o[CHUNK:CHUNK + 1, :] if ones_row else o[LANES:LANES + 1, :]
        o = o[:LANES]
        if has_sink:
            l = l + jnp.exp2(sink - m)
        o = (o / l).T
        idx = 0
        for cc in range(cpg):
            rows = pl.ds(aligned(a * rows_g + cc * CHUNK, CHUNK), CHUNK)
            for r in range(n_qblk):
                top = o[idx * CHUNK:(idx + 1) * CHUNK]
                bot = o[(idx + 1) * CHUNK:(idx + 2) * CHUNK]
                if ones_row:
                    bot = pltpu.roll(bot, HALF, 1)
                o_ref[0, rows, r * LANES:(r + 1) * LANES] = jnp.where(lo, top, bot).astype(o_ref.dtype)
                idx += 2

    n_total = q_ref.shape[1] // rows_g
    if n_total == 1:
        scores(0, 0, 0, 0)
        finish(0, 0, 0)
        return
    g_blk = rows_g // LANES
    n_lead = n_var - 1 + (n_var - 1) % 2
    assert (n_var - 1) * g_blk >= n_blk - g_blk and n_total - n_lead >= 2 and BAND_PIPE_UNROLL % 2 == 0

    def window_block(a):
        wb = (a + 1) * g_blk - n_blk
        return max(wb, 0) if isinstance(a, int) else wb

    def stage(a, slot):
        nxt = a + 1
        scores(nxt, window_block(nxt), min(nxt, n_var - 1) if isinstance(nxt, int) else n_var - 1, 1 - slot)
        finish(a, window_block(a), slot)

    scores(0, window_block(0), 0, 0)
    for a in range(n_lead):
        stage(a, a & 1)

    def trip(i, carry):
        for st in range(BAND_PIPE_UNROLL):
            stage(n_lead + BAND_PIPE_UNROLL * i + st, st & 1)
        return carry

    n_stages = n_total - 1 - n_lead
    lax.fori_loop(0, n_stages // BAND_PIPE_UNROLL, trip, 0)
    for a in range(n_total - 1 - n_stages % BAND_PIPE_UNROLL, n_total - 1):
        stage(a, (a - n_lead) & 1)
    finish(n_total - 1, window_block(n_total - 1), (n_total - 1 - n_lead) & 1)


def _band_attention(q, keys, values, toep, sink, *, variants, n_blk, n_groups, n_qblk, cpg, ones_row):
    bsz, s, _ = q.shape
    qw = n_qblk * LANES
    buffered = isinstance(keys, tuple)

    def per_group(a):
        return pl.BlockSpec((1, a.shape[1], LANES), lambda g, b: (b, 0, g))

    in_specs = [pl.BlockSpec((1, s, qw), lambda g, b: (b, 0, g))]
    if buffered:
        args = [q, *keys, *values]
        in_specs += [per_group(a) for a in args[1:]]
    else:
        args = [q, keys, values]
        in_specs += [per_group(keys), pl.BlockSpec((1, 1) + values.shape[2:], lambda g, b: (b, g, 0, 0, 0))]
    for extra in (toep, sink):
        if extra is not None:
            in_specs.append(pl.BlockSpec((1,) + extra.shape[1:], lambda g, b: (g, 0, 0)))
            args.append(extra)
    bias_shape = (n_blk * LANES, cpg * 2 * n_qblk * CHUNK)
    body = functools.partial(_band_body, n_blk=n_blk, n_qblk=n_qblk, cpg=cpg, ones_row=ones_row,
                             has_sink=sink is not None, variants=variants, buffered=buffered)
    return pl.pallas_call(
        body,
        grid=(n_groups, bsz),
        in_specs=in_specs,
        out_specs=pl.BlockSpec((1, s, qw), lambda g, b: (b, 0, g)),
        out_shape=jax.ShapeDtypeStruct(q.shape, BF16),
        scratch_shapes=[pltpu.VMEM((2,) + bias_shape, F32), pltpu.VMEM((len(variants),) + bias_shape, F32)],
        compiler_params=_params(2),
        name="band_attention",
    )(*args)


def _rope_tables(pos, rot, lane_pattern, from_zero=False):
    half = rot // 2
    inv = jnp.power(ROPE_THETA, -jnp.arange(half, dtype=F32) * 2.0 / rot)
    inv_lane, first, second = [], [], []
    for kind, width in lane_pattern:
        if kind == "rot":
            inv_lane += [inv, inv]
            first += [1.0] * half + [0.0] * half
            second += [0.0] * half + [1.0] * half
        else:
            inv_lane.append(jnp.zeros((width,), F32))
            first += [0.0] * width
            second += [0.0] * width
    inv_lane = jnp.concatenate(inv_lane)[None, :]
    n = pos.shape[0]
    if from_zero and n % CHUNK == 0:
        a_hi = (jnp.arange(n // CHUNK, dtype=F32) * CHUNK)[:, None] * inv_lane
        a_lo = jnp.arange(CHUNK, dtype=F32)[:, None] * inv_lane
        c_hi, s_hi = jnp.cos(a_hi)[:, None, :], jnp.sin(a_hi)[:, None, :]
        c_lo, s_lo = jnp.cos(a_lo)[None], jnp.sin(a_lo)[None]
        cos = (c_hi * c_lo - s_hi * s_lo).reshape(n, LANES)
        sin = (s_hi * c_lo + c_hi * s_lo).reshape(n, LANES)
    else:
        ang = pos.astype(F32)[:, None] * inv_lane
        cos, sin = jnp.cos(ang), jnp.sin(ang)
    return jnp.stack([cos, sin * np.asarray(second, np.float32), -sin * np.asarray(first, np.float32)])


A_ROPE_PATTERN = (("pad", A_NOPE), ("rot", A_ROPE), ("pad", LANES - A_QK))
C_ROPE_PATTERN = (("rot", C_ROT), ("pad", HALF - C_ROT)) * 2


def _prep_ab(pre, post, w_in, q_norm, kv_norm, w_uq, w_ukv, rel_bias, w_out):
    d = w_in.shape[0]
    q_lat, c_kv, k_r, g_a, q_b, k_b, v_b, g_b = jnp.split(
        w_in, [384, 640, 672, 1184, 1696, 2208, 2720], axis=1)
    kr_blk = jnp.concatenate([jnp.zeros((d, A_NOPE), F32), k_r, jnp.zeros((d, LANES - A_QK), F32)], axis=1)
    w_in_p = jnp.concatenate([q_lat, c_kv, kr_blk, g_a, g_b, q_b, k_b, v_b], axis=1).astype(BF16)
    w_uq_p = jnp.pad(w_uq.reshape(A_Q_RANK, A_HEADS, A_QK), ((0, 0), (0, 0), (0, LANES - A_QK)))
    w_uq_p = w_uq_p.reshape(A_Q_RANK, A_HEADS * LANES).astype(BF16)
    ukv = w_ukv.reshape(A_KV_RANK, A_HEADS, A_NOPE + A_V)
    w_uk, w_uv = ukv[..., :A_NOPE], ukv[..., A_NOPE:]
    pad_half = ((0, 0), (0, 0), (0, LANES - A_NOPE))
    w_k = jnp.pad(w_uk, pad_half).reshape(A_KV_RANK, A_HEADS * LANES).astype(BF16)
    w_vt = jnp.pad(w_uv, pad_half).reshape(A_KV_RANK, A_HEADS * LANES).T.astype(BF16)
    wkt = jnp.pad(jnp.transpose(w_uk, (1, 2, 0)), ((0, 0), (0, LANES - A_NOPE), (0, 0))).astype(BF16)
    eye = jnp.eye(A_HEADS, dtype=F32)
    wv_s = (jnp.transpose(w_uv, (1, 0, 2))[:, :, None, :] * eye[:, None, :, None]).reshape(
        A_HEADS, A_KV_RANK, A_WIDTH).astype(BF16)
    sel = (jnp.arange(LANES)[:, None] == A_NOPE + jnp.arange(A_ROPE)[None, :]).astype(BF16)
    rel_bias = rel_bias * LOG2E
    win_p, win_s = B_WIN_BLOCKS * LANES, B_SAMPLE_BLOCKS * LANES
    r0 = win_p - CHUNK
    x_len = r0 + win_p
    n_vec = x_len + CHUNK
    n_hi = r0 + CHUNK - 1 - B_MAX_REL
    n_lo = n_vec - n_hi - (2 * B_MAX_REL + 1)
    vec = jnp.concatenate([jnp.broadcast_to(rel_bias[:, -1:], (B_HEADS, n_hi)), rel_bias[:, ::-1],
                           jnp.broadcast_to(rel_bias[:, :1], (B_HEADS, n_lo))], axis=1)
    skew = jnp.tile(vec, (1, CHUNK))[:, :CHUNK * (n_vec - 1)].reshape(B_HEADS, CHUNK, n_vec - 1)
    toep = skew[:, :, CHUNK - 1:CHUNK - 1 + x_len]
    toep = jnp.transpose(toep.reshape(B_HEADS // 2, 2, CHUNK, x_len), (0, 3, 1, 2)).reshape(B_HEADS // 2, x_len, LANES)
    var_p = tuple(tuple((r0 - c * CHUNK, c - B_PAST_CHUNKS, c)
                        for c in range(B_GROUP_CHUNKS * v, B_GROUP_CHUNKS * (v + 1))) for v in range(B_VARIANTS))
    var_s = (((r0 - (win_s - CHUNK), 1, B_PAST_CHUNKS + 1),),)
    return dict(pre=pre[None], w_in=w_in_p, q_norm=q_norm[None], kv_norm=kv_norm[None], w_uq=w_uq_p, w_k=w_k,
                w_vt=w_vt, wkt=wkt, wv_s=wv_s, sel=sel, toep=toep, var_p=var_p, var_s=var_s,
                w_out=w_out.astype(BF16), post=post[None])


def _prep_c(pre, post, w_in, sinks, w_out):
    q, k, v, g = jnp.split(w_in, [1024, 1152, 1280], axis=1)
    k0, k1 = k[:, :C_HEAD_DIM], k[:, C_HEAD_DIM:]
    v0, v1 = v[:, :C_HEAD_DIM], v[:, C_HEAD_DIM:]
    zero = jnp.zeros_like(v0)
    w_in_p = jnp.concatenate([q, k0, k0, k1, k1, v0, zero, v1, zero, g], axis=1).astype(BF16)
    var_p = tuple(tuple((0, c - C_PAST_CHUNKS, c) for c in range(C_GROUP_CHUNKS * v, C_GROUP_CHUNKS * (v + 1)))
                  for v in range(C_VARIANTS))
    var_s = (((0, 1, C_PAST_CHUNKS + 1),),)
    sink_row = jnp.repeat((sinks * LOG2E).reshape(C_KV_HEADS, C_GROUP), CHUNK, axis=1)[:, None, :]
    return dict(pre=pre[None], w_in=w_in_p, var_p=var_p, var_s=var_s, sink_s=sink_row,
                sink_p=jnp.tile(sink_row, (1, 1, C_GROUP_CHUNKS)), w_out=w_out.astype(BF16), post=post[None])


def _dup_heads(x):
    return jnp.concatenate([x[:, :, 0], x[:, :, 0], x[:, :, 1], x[:, :, 1]], axis=-1)


def kernel(x_prompt, x_sample, cache_a_ckv, cache_a_krope, cache_b_k, cache_b_v, cache_c_k, cache_c_v,
           ab_pre_norm, ab_post_norm, ab_w_in, ab_q_norm, ab_kv_norm, ab_w_uq, ab_w_ukv, ab_rel_bias, ab_w_out,
           c_pre_norm, c_post_norm, c_w_in, c_sinks, c_w_out):
    bsz, seq, _ = x_prompt.shape
    dbs, dseq, _ = x_sample.shape
    past = cache_a_ckv.shape[2]
    n_s = dbs * dseq
    pos_p = jnp.arange(seq, dtype=jnp.int32)
    pos_s = jnp.tile(past + jnp.arange(dseq, dtype=jnp.int32), dbs)
    wab = _prep_ab(ab_pre_norm[0], ab_post_norm[0], ab_w_in[0], ab_q_norm[0], ab_kv_norm[0], ab_w_uq[0],
                   ab_w_ukv[0], ab_rel_bias[0], ab_w_out[0])
    wc = _prep_c(c_pre_norm[0], c_post_norm[0], c_w_in[0], c_sinks[0], c_w_out[0])
    b_tail = min(B_PAST_CHUNKS * CHUNK, seq)
    c_tail = min(C_WINDOW, seq)
    tile = 512

    rope_a_p = _rope_tables(pos_p, A_ROPE, A_ROPE_PATTERN, from_zero=True)
    (qa, ka, vt, qb, kb, vbt, sg, c_new_p, kr_new_p, kb_tail, vb_tail) = _ab_in_proj(
        x_prompt, rope_a_p, wab, tm=tile, tail_len=b_tail, tkv=tile)
    o_a = _mla_prompt(qa, ka, vt, tq=tile)
    o_b = _band_attention(qb, kb, vbt, wab["toep"], None, variants=wab["var_p"], n_blk=B_WIN_BLOCKS,
                          n_groups=B_HEADS // 2, n_qblk=1, cpg=B_GROUP_CHUNKS, ones_row=False)

    rope_a_s = _rope_tables(pos_s, A_ROPE, A_ROPE_PATTERN)
    xs = x_sample.reshape(1, n_s, D_MODEL)
    (qa_s, _, _, qb_s, _, _, sg_s, c_new_s, kr_new_s, kb_s32, vb_s32) = _ab_in_proj(
        xs, rope_a_s, wab, tm=n_s, tail_len=n_s, tkv=n_s)
    o_a_s = _mla_sample(qa_s.reshape(dbs, dseq, -1), cache_a_ckv[0], cache_a_krope[0],
                        c_new_s.reshape(dbs, dseq, -1), kr_new_s.reshape(dbs, dseq, -1),
                        wab["wkt"], wab["wv_s"], wab["sel"], tk=512)
    wb = cache_b_k.shape[2]
    o_b_s = _band_attention(qb_s.reshape(dbs, dseq, -1),
                            (cache_b_k[0].reshape(dbs, wb, B_WIDTH), kb_s32.reshape(dbs, dseq, -1)),
                            (cache_b_v[0].reshape(dbs, wb, B_WIDTH), vb_s32.reshape(dbs, dseq, -1)),
                            wab["toep"], None, variants=wab["var_s"], n_blk=B_SAMPLE_BLOCKS, n_groups=B_HEADS // 2,
                            n_qblk=1, cpg=1, ones_row=False)

    rope_c_p = _rope_tables(pos_p, C_ROT, C_ROPE_PATTERN, from_zero=True)
    h1_p, qc, kc, vct, sgc, kc_tail, vc_tail = _c_in_proj(
        [o_a, o_b], sg, wab["w_out"], wab["post"], x_prompt, rope_c_p, wc, tm=tile, tail_len=c_tail)
    o_c = _band_attention(qc, kc, vct, None, wc["sink_p"], variants=wc["var_p"], n_blk=C_WIN_BLOCKS,
                          n_groups=C_KV_HEADS, n_qblk=C_GROUP // 2, cpg=C_GROUP_CHUNKS, ones_row=True)
    h2_p = _out_proj([o_c], sgc, wc["w_out"], wc["post"], h1_p, tm=tile)

    rope_c_s = _rope_tables(pos_s, C_ROT, C_ROPE_PATTERN)
    h1_s, qc_s, kc_s, _, sgc_s, kc_s32, vc_s32 = _c_in_proj(
        [o_a_s.reshape(1, n_s, -1), o_b_s.reshape(1, n_s, -1)], sg_s, wab["w_out"], wab["post"], xs, rope_c_s, wc,
        tm=n_s, tail_len=n_s)
    wcw = cache_c_k.shape[2]
    win_c = C_SAMPLE_BLOCKS * LANES
    n_pad = win_c - wcw - dseq
    kcb = jnp.concatenate([jnp.zeros((dbs, n_pad, C_KV_HEADS * LANES), BF16), _dup_heads(cache_c_k[0]).astype(BF16),
                           kc_s.reshape(dbs, dseq, -1)], axis=1)
    vcb = jnp.concatenate([jnp.zeros((dbs, n_pad, C_KV_HEADS, C_HEAD_DIM), F32), cache_c_v[0],
                           vc_s32.reshape(dbs, dseq, C_KV_HEADS, C_HEAD_DIM)], axis=1)
    vcb_t = jnp.concatenate([jnp.transpose(vcb, (0, 2, 3, 1)), jnp.ones((dbs, C_KV_HEADS, 1, win_c), F32),
                             jnp.zeros((dbs, C_KV_HEADS, LANES - C_HEAD_DIM - 1, win_c), F32)], axis=2)
    vcb_t = jnp.transpose(vcb_t.astype(BF16).reshape(dbs, C_KV_HEADS, LANES, C_SAMPLE_BLOCKS, LANES), (0, 1, 3, 2, 4))
    o_c_s = _band_attention(qc_s.reshape(dbs, dseq, -1), kcb, vcb_t, None, wc["sink_s"], variants=wc["var_s"],
                            n_blk=C_SAMPLE_BLOCKS, n_groups=C_KV_HEADS, n_qblk=C_GROUP // 2, cpg=1, ones_row=True)
    h2_s = _out_proj([o_c_s.reshape(1, n_s, -1)], sgc_s, wc["w_out"], wc["post"], h1_s, tm=n_s)

    def roll_in(buf, new):
        return jnp.concatenate([buf, new], axis=1)[:, -buf.shape[1]:][None]

    return (h2_p, h2_s.reshape(dbs, dseq, D_MODEL),
            c_new_p[None], kr_new_p[None],
            kb_tail.reshape(1, bsz, b_tail, B_HEADS, B_HEAD_DIM), vb_tail.reshape(1, bsz, b_tail, B_HEADS, B_HEAD_DIM),
            kc_tail.reshape(1, bsz, c_tail, C_KV_HEADS, C_HEAD_DIM), vc_tail.reshape(1, bsz, c_tail, C_KV_HEADS, C_HEAD_DIM),
            c_new_s.reshape(1, dbs, dseq, A_KV_RANK), kr_new_s.reshape(1, dbs, dseq, A_ROPE),
            roll_in(cache_b_k[0], kb_s32.reshape(dbs, dseq, B_HEADS, B_HEAD_DIM)),
            roll_in(cache_b_v[0], vb_s32.reshape(dbs, dseq, B_HEADS, B_HEAD_DIM)),
            roll_in(cache_c_k[0], kc_s32.reshape(dbs, dseq, C_KV_HEADS, C_HEAD_DIM)),
            roll_in(cache_c_v[0], vc_s32.reshape(dbs, dseq, C_KV_HEADS, C_HEAD_DIM)))
```

```python
import functools

import jax
import jax.numpy as jnp
import numpy as np
from jax import lax
from jax.experimental import pallas as pl
from jax.experimental.pallas import tpu as pltpu

F32 = jnp.float32
BF16 = jnp.bfloat16

D_MODEL = 1024
CHUNK = 64
ROPE_THETA = 500000.0
RMS_EPS = 1e-6
NEG_INF = -1e30

A_HEADS = 8
A_NOPE = 64
A_ROPE = 32
A_QK = A_NOPE + A_ROPE
A_V = 64
A_Q_RANK = 384
A_KV_RANK = 256
A_WIDTH = A_HEADS * A_V
LOG2E = 1.4426950408889634
A_SCALE = A_QK ** -0.5 * LOG2E

B_HEADS = 8
B_HEAD_DIM = 64
B_WIDTH = B_HEADS * B_HEAD_DIM
B_PAST_CHUNKS = 8
B_MAX_REL = 128
B_SCALE = B_HEAD_DIM ** -0.5 * LOG2E

C_HEADS = 16
C_KV_HEADS = 2
C_GROUP = C_HEADS // C_KV_HEADS
C_HEAD_DIM = 64
C_WIDTH = C_HEADS * C_HEAD_DIM
C_WINDOW = 128
C_PAST_CHUNKS = C_WINDOW // CHUNK
C_ROT = C_HEAD_DIM // 4
C_SCALE = C_HEAD_DIM ** -0.5 * LOG2E

LANES = 128
HALF = LANES // 2
BF16_SUBLANES = 16
VMEM_LIMIT = 56 * 1024 * 1024
A_MASK_CHUNKS = 8
assert A_QK + A_MASK_CHUNKS <= LANES
A_PIPE_UNROLL = 4
BAND_PIPE_UNROLL = 4
B_GROUP_CHUNKS = 4
C_GROUP_CHUNKS = 2
B_WIN_BLOCKS = (B_PAST_CHUNKS + B_GROUP_CHUNKS) * CHUNK // LANES
C_WIN_BLOCKS = (C_PAST_CHUNKS + C_GROUP_CHUNKS) * CHUNK // LANES
B_VARIANTS = B_PAST_CHUNKS // B_GROUP_CHUNKS + 1
C_VARIANTS = C_PAST_CHUNKS // C_GROUP_CHUNKS + 1
B_SAMPLE_BLOCKS = (B_PAST_CHUNKS + 2) * CHUNK // LANES
C_SAMPLE_BLOCKS = (C_PAST_CHUNKS + 2) * CHUNK // LANES

AB_Q0, AB_C0, AB_KR0, AB_G0, AB_QB0, AB_KB0, AB_VB0, AB_NZ = 0, 384, 640, 768, 1792, 2304, 2816, 3328
C_Q0, C_K0, C_V0, C_G0, C_NZ = 0, 1024, 1280, 1536, 2560


def _params(n_axes):
    return pltpu.CompilerParams(dimension_semantics=("arbitrary",) * n_axes, vmem_limit_bytes=VMEM_LIMIT)


def _rms(x, g):
    return x * lax.rsqrt(jnp.mean(x * x, axis=-1, keepdims=True) + RMS_EPS) * g


def _rope_block(blk, rope_ref, shift):
    return (blk * rope_ref[0] + pltpu.roll(blk, shift, 1) * rope_ref[1]
            + pltpu.roll(blk, LANES - shift, 1) * rope_ref[2])


def _tail_spec(n_tiles, tm, tail_len, width):
    tb = min(tail_len, tm)
    n_blk = tail_len // tb
    return tb, pl.BlockSpec((1, tb, width), lambda b, i: (b, jnp.maximum(i - (n_tiles - n_blk), 0), 0))


def _ab_in_body(x_ref, pre_ref, w_ref, qn_ref, kvn_ref, wuq_ref, wk_ref, wvt_ref, rope_ref,
                qa_ref, ka_ref, vt_ref, qb_ref, kb_ref, vbt_ref, sg_ref, c_ref, kr_ref, kb_tail_ref, vb_tail_ref):
    tm = x_ref.shape[1]
    tb = kb_tail_ref.shape[1]
    xn = _rms(x_ref[0], pre_ref[...]).astype(BF16)
    z = jnp.dot(xn, w_ref[...], preferred_element_type=F32)

    qn = _rms(z[:, AB_Q0:AB_C0], qn_ref[...]).astype(BF16)
    qa = jnp.dot(qn, wuq_ref[...], preferred_element_type=F32) * A_SCALE
    for h in range(A_HEADS):
        blk = slice(h * LANES, (h + 1) * LANES)
        qa_ref[0, :, blk] = _rope_block(qa[:, blk], rope_ref, A_ROPE // 2).astype(BF16)

    c_new = _rms(z[:, AB_C0:AB_KR0], kvn_ref[...])
    c_ref[0] = c_new
    cb = c_new.astype(BF16)
    krot = _rope_block(z[:, AB_KR0:AB_G0], rope_ref, A_ROPE // 2)
    kr_ref[0] = krot[:, A_NOPE:A_NOPE + A_ROPE]
    kn = jnp.dot(cb, wk_ref[...], preferred_element_type=F32)
    row = pl.program_id(1) * tm + lax.broadcasted_iota(jnp.int32, (tm, LANES), 0)
    lane = lax.broadcasted_iota(jnp.int32, (tm, LANES), 1)
    k_shared = krot + jnp.where(lane - A_QK == (row // CHUNK) % A_MASK_CHUNKS, 1.0, 0.0)
    for h in range(A_HEADS):
        blk = slice(h * LANES, (h + 1) * LANES)
        ka_ref[0, :, blk] = (kn[:, blk] + k_shared).astype(BF16)
    vt = jnp.dot(wvt_ref[...], c_new.T.astype(BF16), preferred_element_type=F32)
    row = lax.broadcasted_iota(jnp.int32, vt.shape, 0)
    vt_ref[0, 0] = jnp.where((row & (LANES - 1)) == A_V, 1.0, vt).astype(BF16)

    g = z[:, AB_G0:AB_QB0]
    sg_ref[0] = (g * jax.nn.sigmoid(g)).astype(BF16)
    qb_ref[0] = (z[:, AB_QB0:AB_KB0] * B_SCALE).astype(BF16)
    kb = z[:, AB_KB0:AB_VB0]
    vb = z[:, AB_VB0:AB_NZ]
    kb_ref[0] = kb.astype(BF16)
    vbt = vb.T.astype(BF16)
    for hp in range(B_HEADS // 2):
        for j in range(tm // LANES):
            vbt_ref[0, hp, j] = vbt[hp * LANES:(hp + 1) * LANES, j * LANES:(j + 1) * LANES]
    kb_tail_ref[0] = kb[tm - tb:, :]
    vb_tail_ref[0] = vb[tm - tb:, :]


def _ab_in_proj(x, rope, w, tm, tail_len, tkv):
    bsz, s, _ = x.shape
    nt = s // tm
    per_kv = tkv // tm
    tb, tail_spec = _tail_spec(nt, tm, tail_len, B_WIDTH)

    def full(a):
        return pl.BlockSpec(a.shape, lambda b, i: (0,) * a.ndim, pipeline_mode=pl.Buffered(1))

    def rows(width):
        return pl.BlockSpec((1, tm, width), lambda b, i: (b, i, 0))

    weights = (w["pre"], w["w_in"], w["q_norm"], w["kv_norm"], w["w_uq"], w["w_k"], w["w_vt"])
    vt_spec = pl.BlockSpec((1, 1, A_HEADS * LANES, tm), lambda b, i: (b, i // per_kv, 0, i % per_kv))
    out_shape = (
        jax.ShapeDtypeStruct((bsz, s, A_HEADS * LANES), BF16),
        jax.ShapeDtypeStruct((bsz, s, A_HEADS * LANES), BF16),
        jax.ShapeDtypeStruct((bsz, s // tkv, A_HEADS * LANES, tkv), BF16),
        jax.ShapeDtypeStruct((bsz, s, B_WIDTH), BF16),
        jax.ShapeDtypeStruct((bsz, s, B_WIDTH), BF16),
        jax.ShapeDtypeStruct((bsz, B_HEADS // 2, s // LANES, LANES, LANES), BF16),
        jax.ShapeDtypeStruct((bsz, s, A_WIDTH + B_WIDTH), BF16),
        jax.ShapeDtypeStruct((bsz, s, A_KV_RANK), F32),
        jax.ShapeDtypeStruct((bsz, s, A_ROPE), F32),
        jax.ShapeDtypeStruct((bsz, tail_len, B_WIDTH), F32),
        jax.ShapeDtypeStruct((bsz, tail_len, B_WIDTH), F32),
    )
    vbt_spec = pl.BlockSpec((1, B_HEADS // 2, tm // LANES, LANES, LANES), lambda b, i: (b, 0, i, 0, 0))
    out_specs = (rows(1024), rows(1024), vt_spec, rows(512), rows(512), vbt_spec, rows(1024),
                 rows(A_KV_RANK), rows(A_ROPE), tail_spec, tail_spec)
    return pl.pallas_call(
        _ab_in_body,
        grid=(bsz, nt),
        in_specs=[rows(D_MODEL)] + [full(a) for a in weights]
        + [pl.BlockSpec((3, tm, LANES), lambda b, i: (0, i, 0))],
        out_specs=out_specs,
        out_shape=out_shape,
        compiler_params=_params(2),
        name="ab_in_proj",
    )(x, *weights, rope)


def _mix_out(o_refs, sg_ref, w_ref, g_ref, h_ref):
    o = jnp.concatenate([r[0].astype(F32) for r in o_refs], axis=-1) if len(o_refs) > 1 else o_refs[0][0].astype(F32)
    mixed = (o * sg_ref[0].astype(F32)).astype(BF16)
    y = jnp.dot(mixed, w_ref[...], preferred_element_type=F32)
    return h_ref[0] + _rms(y, g_ref[...])


def _c_in_body(*refs):
    n_o = len(refs) - 14
    o_refs = refs[:n_o]
    (sg0_ref, w0_ref, post0_ref, x_ref, pre_ref, w_ref, rope_ref,
     h_ref, q_ref, k_ref, vt_ref, sg_ref, k_tail_ref, v_tail_ref) = refs[n_o:]
    tm = x_ref.shape[1]
    tb = k_tail_ref.shape[1]
    h = _mix_out(o_refs, sg0_ref, w0_ref, post0_ref, x_ref)
    h_ref[0] = h
    xn = _rms(h, pre_ref[...]).astype(BF16)
    z = jnp.dot(xn, w_ref[...], preferred_element_type=F32)
    half_rot = C_ROT // 2
    for j in range(C_WIDTH // LANES):
        blk = slice(C_Q0 + j * LANES, C_Q0 + (j + 1) * LANES)
        q_ref[0, :, j * LANES:(j + 1) * LANES] = (_rope_block(z[:, blk], rope_ref, half_rot) * C_SCALE).astype(BF16)
    kd = [_rope_block(z[:, C_K0 + j * LANES:C_K0 + (j + 1) * LANES], rope_ref, half_rot) for j in range(C_KV_HEADS)]
    vz = [z[:, C_V0 + j * LANES:C_V0 + (j + 1) * LANES] for j in range(C_KV_HEADS)]
    row = lax.broadcasted_iota(jnp.int32, (LANES, tm), 0)
    for j in range(C_KV_HEADS):
        k_ref[0, :, j * LANES:(j + 1) * LANES] = kd[j].astype(BF16)
        vt = jnp.where(row == C_HEAD_DIM, 1.0, vz[j].T).astype(BF16)
        for i in range(tm // LANES):
            vt_ref[0, j, i] = vt[:, i * LANES:(i + 1) * LANES]
    g = z[:, C_G0:C_NZ]
    sg_ref[0] = (g * jax.nn.sigmoid(g)).astype(BF16)
    lo = lax.broadcasted_iota(jnp.int32, (1, LANES), 1) < HALF
    k_tail_ref[0] = jnp.where(lo, kd[0], kd[1])[tm - tb:, :]
    v_tail_ref[0] = (vz[0] + pltpu.roll(vz[1], HALF, 1))[tm - tb:, :]


def _c_in_proj(o_parts, sg0, w_out0, post0, x, rope, w, tm, tail_len):
    bsz, s, _ = x.shape
    nt = s // tm
    tb, tail_spec = _tail_spec(nt, tm, tail_len, LANES)

    def full(a):
        return pl.BlockSpec(a.shape, lambda b, i: (0,) * a.ndim)

    def rows(width):
        return pl.BlockSpec((1, tm, width), lambda b, i: (b, i, 0))

    out_shape = (
        jax.ShapeDtypeStruct((bsz, s, D_MODEL), F32),
        jax.ShapeDtypeStruct((bsz, s, C_WIDTH), BF16),
        jax.ShapeDtypeStruct((bsz, s, C_KV_HEADS * LANES), BF16),
        jax.ShapeDtypeStruct((bsz, C_KV_HEADS, s // LANES, LANES, LANES), BF16),
        jax.ShapeDtypeStruct((bsz, s, C_WIDTH), BF16),
        jax.ShapeDtypeStruct((bsz, tail_len, LANES), F32),
        jax.ShapeDtypeStruct((bsz, tail_len, LANES), F32),
    )
    return pl.pallas_call(
        _c_in_body,
        grid=(bsz, nt),
        in_specs=[rows(o.shape[-1]) for o in o_parts]
        + [rows(D_MODEL), full(w_out0), full(post0), rows(D_MODEL), full(w["pre"]), full(w["w_in"]),
           pl.BlockSpec((3, tm, LANES), lambda b, i: (0, i, 0))],
        out_specs=(rows(D_MODEL), rows(C_WIDTH), rows(256),
                   pl.BlockSpec((1, C_KV_HEADS, tm // LANES, LANES, LANES), lambda b, i: (b, 0, i, 0, 0)),
                   rows(C_WIDTH), tail_spec, tail_spec),
        out_shape=out_shape,
        compiler_params=_params(2),
        name="c_in_proj",
    )(*o_parts, sg0, w_out0, post0, x, w["pre"], w["w_in"], rope)


def _out_body(*refs):
    o_refs, (sg_ref, w_ref, g_ref, h_ref, out_ref) = refs[:-5], refs[-5:]
    out_ref[0] = _mix_out(o_refs, sg_ref, w_ref, g_ref, h_ref)


def _out_proj(o_parts, sg, w_out, post_g, h, tm):
    bsz, s, _ = h.shape

    def rows(width):
        return pl.BlockSpec((1, tm, width), lambda b, i: (b, i, 0))

    def full(a):
        return pl.BlockSpec(a.shape, lambda b, i: (0,) * a.ndim)

    return pl.pallas_call(
        _out_body,
        grid=(bsz, s // tm),
        in_specs=[rows(o.shape[-1]) for o in o_parts] + [rows(D_MODEL), full(w_out), full(post_g), rows(D_MODEL)],
        out_specs=rows(D_MODEL),
        out_shape=jax.ShapeDtypeStruct(h.shape, F32),
        compiler_params=_params(2),
        name="out_proj",
    )(*o_parts, sg, w_out, post_g, h)


def _mla_prompt_body(q_ref, k_ref, vt_ref, qmask_ref, o_ref, m_sc, acc_sc, s_sc, *, tq):
    n_tiles = q_ref.shape[1] // tq
    n_pairs = n_tiles * (n_tiles + 1) // 2
    assert A_PIPE_UNROLL % 2 == 0 and n_pairs % A_PIPE_UNROLL == 0 and n_pairs >= 2 * A_PIPE_UNROLL
    m_sc[...] = jnp.full(m_sc.shape, NEG_INF, F32)
    acc_sc[...] = jnp.zeros(acc_sc.shape, F32)
    contract_last = (((1,), (1,)), ((), ()))

    def scores(qi, j, slot):
        q_start = pl.multiple_of(qi * tq, tq)
        k_start = pl.multiple_of(j * tq, tq)
        qmask = jnp.where(j == qi, qmask_ref[...], jnp.zeros_like(qmask_ref[...]))
        for hh in range(2):
            blk = slice(hh * LANES, (hh + 1) * LANES)
            q = q_ref[0, pl.ds(q_start, tq), blk] + qmask
            s_sc[slot, hh] = lax.dot_general(k_ref[0, pl.ds(k_start, tq), blk], q, contract_last,
                                             preferred_element_type=F32)

    def consume(qi, j, slot):
        for hh in range(2):
            s = s_sc[slot, hh]
            vt = vt_ref[0, j, hh * LANES:(hh + 1) * LANES, :]
            m_old = m_sc[qi, hh]
            m_new = jnp.maximum(m_old, jnp.max(s, axis=0, keepdims=True))
            p = jnp.exp2(s - m_new).astype(BF16)
            acc_sc[qi, hh] = jnp.exp2(m_old - m_new) * acc_sc[qi, hh] + jnp.dot(vt, p, preferred_element_type=F32)
            m_sc[qi, hh] = m_new

    def advance(qi, j):
        last = j == qi
        return jnp.where(last, qi + 1, qi), jnp.where(last, 0, j + 1)

    def stages(cur, count):
        for st in range(count):
            nxt = advance(*cur)
            scores(*nxt, 1 - (st & 1))
            consume(*cur, st & 1)
            cur = nxt
        return cur

    zero = jnp.int32(0)
    scores(zero, zero, 0)
    cur = lax.fori_loop(0, n_pairs // A_PIPE_UNROLL - 1, lambda i, c: stages(c, A_PIPE_UNROLL), (zero, zero))
    cur = stages(cur, A_PIPE_UNROLL - 1)
    consume(*cur, (A_PIPE_UNROLL - 1) & 1)

    lo = lax.broadcasted_iota(jnp.int32, (1, LANES), 1) < HALF

    def write_tile(qi, carry):
        outs = []
        for hh in range(2):
            acc = acc_sc[qi, hh]
            outs.append((acc / acc[A_V:A_V + 1, :]).T)
        o_ref[0, pl.ds(pl.multiple_of(qi * tq, tq), tq), :] = jnp.where(
            lo, outs[0], pltpu.roll(outs[1], HALF, 1)).astype(o_ref.dtype)
        return carry

    lax.fori_loop(0, n_tiles, write_tile, 0)


def _mla_prompt(qa, ka, vt, tq):
    bsz, s, _ = qa.shape
    pairs = A_HEADS // 2
    n_tiles = s // tq
    assert vt.shape[3] == tq and tq == A_MASK_CHUNKS * CHUNK
    q_chunk = np.arange(tq)[:, None] // CHUNK
    lane_chunk = np.arange(LANES)[None, :] - A_QK
    qmask = np.where((lane_chunk > q_chunk) & (lane_chunk < A_MASK_CHUNKS), NEG_INF, 0.0).astype(np.float32)
    return pl.pallas_call(
        functools.partial(_mla_prompt_body, tq=tq),
        grid=(bsz, pairs),
        in_specs=[pl.BlockSpec((1, s, 2 * LANES), lambda b, h: (b, 0, h)),
                  pl.BlockSpec((1, s, 2 * LANES), lambda b, h: (b, 0, h)),
                  pl.BlockSpec((1, n_tiles, 2 * LANES, tq), lambda b, h: (b, 0, h, 0)),
                  pl.BlockSpec((tq, LANES), lambda b, h: (0, 0))],
        out_specs=pl.BlockSpec((1, s, LANES), lambda b, h: (b, 0, h)),
        out_shape=jax.ShapeDtypeStruct((bsz, s, A_WIDTH), BF16),
        scratch_shapes=[pltpu.VMEM((n_tiles, 2, 1, tq), F32), pltpu.VMEM((n_tiles, 2, LANES, tq), F32),
                        pltpu.VMEM((2, 2, tq, tq), F32)],
        compiler_params=_params(2),
        name="mla_prompt",
    )(qa, ka, vt, jnp.asarray(qmask, BF16))


def _mla_sample_body(q_ref, cc_ref, ckr_ref, cn_ref, krn_ref, wkt_ref, wv_ref, sel_ref, o_ref,
                     qabs_sc, qr_sc, m_sc, acc_sc, *, tk):
    t = q_ref.shape[1]
    past = cc_ref.shape[1]
    for h in range(A_HEADS):
        qh = q_ref[0, :, h * LANES:(h + 1) * LANES]
        rows = slice(h * t, (h + 1) * t)
        qabs_sc[rows, :] = jnp.dot(qh, wkt_ref[h], preferred_element_type=F32).astype(BF16)
        qr_sc[rows, :] = jnp.dot(qh, sel_ref[...], preferred_element_type=F32).astype(BF16)
    m_sc[...] = jnp.full(m_sc.shape, NEG_INF, F32)
    acc_sc[...] = jnp.zeros(acc_sc.shape, F32)
    contract_last = (((1,), (1,)), ((), ()))

    def scores(c_t, kr_t):
        return (lax.dot_general(c_t.astype(BF16), qabs_sc[...], contract_last, preferred_element_type=F32)
                + lax.dot_general(kr_t.astype(BF16), qr_sc[...], contract_last, preferred_element_type=F32))

    def update(s, c_t):
        n_keys = c_t.shape[0]
        m_old = m_sc[...]
        m_new = jnp.maximum(m_old, jnp.max(s, axis=0, keepdims=True))
        p = jnp.exp2(s - m_new).astype(BF16)
        ct = jnp.concatenate([c_t.T.astype(BF16), jnp.ones((BF16_SUBLANES, n_keys), BF16)], axis=0)
        acc_sc[...] = jnp.exp2(m_old - m_new) * acc_sc[...] + jnp.dot(ct, p, preferred_element_type=F32)
        m_sc[...] = m_new

    def tile(j):
        rows = pl.ds(j * tk, tk)
        return cc_ref[0, rows, :], ckr_ref[0, rows, :]

    n_tiles = past // tk
    cur = tile(0)
    s = scores(*cur)
    for j in range(n_tiles):
        nxt = tile(j + 1) if j + 1 < n_tiles else (cn_ref[0], krn_ref[0])
        s_next = scores(*nxt)
        update(s, cur[0])
        cur, s = nxt, s_next
    update(s, cur[0])

    acc = acc_sc[...]
    o_lat = (acc[:A_KV_RANK] / acc[A_KV_RANK:A_KV_RANK + 1]).T.astype(BF16)
    out = jnp.zeros((t, A_WIDTH), F32)
    for h in range(A_HEADS):
        out = out + jnp.dot(o_lat[h * t:(h + 1) * t, :], wv_ref[h], preferred_element_type=F32)
    o_ref[0] = out.astype(o_ref.dtype)


def _mla_sample(qa, cache_c, cache_kr, c_new, kr_new, wkt, wv, sel, tk):
    bsz, t, _ = qa.shape
    past = cache_c.shape[1]

    def per_b(shape):
        return pl.BlockSpec((1,) + shape, lambda b: (b, 0, 0))

    def full(a):
        return pl.BlockSpec(a.shape, lambda b: (0,) * a.ndim)

    rows = A_HEADS * t
    return pl.pallas_call(
        functools.partial(_mla_sample_body, tk=tk),
        grid=(bsz,),
        in_specs=[per_b((t, A_HEADS * LANES)), per_b((past, A_KV_RANK)), per_b((past, A_ROPE)),
                  per_b((t, A_KV_RANK)), per_b((t, A_ROPE)), full(wkt), full(wv), full(sel)],
        out_specs=per_b((t, A_WIDTH)),
        out_shape=jax.ShapeDtypeStruct((bsz, t, A_WIDTH), BF16),
        scratch_shapes=[pltpu.VMEM((rows, A_KV_RANK), BF16), pltpu.VMEM((rows, A_ROPE), BF16),
                        pltpu.VMEM((1, rows), F32), pltpu.VMEM((A_KV_RANK + BF16_SUBLANES, rows), F32)],
        compiler_params=_params(1),
        name="mla_sample",
    )(qa, cache_c, cache_kr, c_new, kr_new, wkt, wv, sel)


def _band_body(*refs, n_blk, n_qblk, cpg, ones_row, has_sink, variants, buffered=False):
    refs = list(refs)
    n_in = 5 if buffered else 3
    q_ref = refs[0]
    o_ref, s_sc, bias_sc = refs[-3:]
    extra = refs[n_in:-3]
    sink_ref = extra.pop() if has_sink else None
    toep_ref = extra.pop() if extra else None
    win = n_blk * LANES
    rows_g = cpg * CHUNK
    n_var = len(variants)
    qpc = 2 * n_qblk * CHUNK
    lo = lax.broadcasted_iota(jnp.int32, (CHUNK, LANES), 1) < HALF
    contract_last = (((1,), (1,)), ((), ()))

    if toep_ref is not None:
        key_chunk = lax.broadcasted_iota(jnp.int32, (win, LANES), 0) // CHUNK
        for v, chunks in enumerate(variants):
            for cc, (x0, lo_chunk, hi_chunk) in enumerate(chunks):
                tile = jnp.where((key_chunk >= lo_chunk) & (key_chunk <= hi_chunk), toep_ref[0, x0:x0 + win, :], NEG_INF)
                for j in range(qpc // LANES):
                    bias_sc[v, :, cc * qpc + j * LANES:cc * qpc + (j + 1) * LANES] = tile

    def store_scores(slot, variant, s):
        if toep_ref is not None:
            s_sc[slot] = s + bias_sc[variant]
            return
        s_sc[slot] = s
        for cc, (_, lo_chunk, hi_chunk) in enumerate(variants[variant]):
            cols = slice(cc * qpc, (cc + 1) * qpc)
            for r0, r1 in ((0, lo_chunk * CHUNK), ((hi_chunk + 1) * CHUNK, win)):
                if r1 > max(r0, 0):
                    s_sc[slot, max(r0, 0):r1, cols] = jnp.full((r1 - max(r0, 0), qpc), NEG_INF, F32)

    def aligned(x, n):
        return x if isinstance(x, int) else pl.multiple_of(x, n)

    def buffered_window(buf_ref, new_ref, bi):
        buf, new = buf_ref[bi], new_ref[bi]
        pad = jnp.zeros((win - buf.shape[0] - new.shape[0], LANES), F32)
        return jnp.concatenate([pad, buf, new], axis=0)

    def scores(a, wb, variant, slot, bi=0):
        pieces = []
        for cc in range(cpg):
            rows = pl.ds(aligned(a * rows_g + cc * CHUNK, CHUNK), CHUNK)
            for r in range(n_qblk):
                qblk = q_ref[bi, rows, r * LANES:(r + 1) * LANES]
                zero = jnp.zeros_like(qblk)
                pieces += [jnp.where(lo, qblk, zero), jnp.where(lo, zero, qblk)]
        qs = jnp.concatenate(pieces, axis=0)
        if buffered:
            kw = buffered_window(refs[1], refs[2], bi).astype(BF16)
        else:
            kw = refs[1][bi, pl.ds(aligned(wb * LANES, LANES), win), :]
        store_scores(slot, variant, lax.dot_general(kw, qs, contract_last, preferred_element_type=F32))

    def finish(a, wb, slot, bi=0):
        s = s_sc[slot]
        m = jnp.max(s, axis=0, keepdims=True)
        if has_sink:
            sink = sink_ref[0]
            m = jnp.maximum(m, sink)
        p = jnp.exp2(s - m)
        if buffered:
            vt = buffered_window(refs[3], refs[4], bi).T.astype(BF16)
        else:
            vt = jnp.concatenate([refs[2][bi, 0, wb + i] for i in range(n_blk)], axis=1)
        if not ones_row:
            vt = jnp.concatenate([vt, jnp.ones((BF16_SUBLANES, win), BF16)], axis=0)
        o = jnp.dot(vt, p.astype(BF16), preferred_element_type=F32)
        l = o[CHUNK:CHUNK + 1, :] if ones_row else o[LANES:LANES + 1, :]
        o = o[:LANES]
        if has_sink:
            l = l + jnp.exp2(sink - m)
        o = (o / l).T
        idx = 0
        for cc in range(cpg):
            rows = pl.ds(aligned(a * rows_g + cc * CHUNK, CHUNK), CHUNK)
            for r in range(n_qblk):
                top = o[idx * CHUNK:(idx + 1) * CHUNK]
                bot = o[(idx + 1) * CHUNK:(idx + 2) * CHUNK]
                if ones_row:
                    bot = pltpu.roll(bot, HALF, 1)
                o_ref[bi, rows, r * LANES:(r + 1) * LANES] = jnp.where(lo, top, bot).astype(o_ref.dtype)
                idx += 2

    n_total = q_ref.shape[1] // rows_g
    if n_total == 1:
        n_rows = q_ref.shape[0]
        scores(0, 0, 0, 0, 0)
        for bi in range(n_rows):
            if bi + 1 < n_rows:
                scores(0, 0, 0, (bi + 1) & 1, bi + 1)
            finish(0, 0, bi & 1, bi)
        return
    g_blk = rows_g // LANES
    n_lead = n_var - 1 + (n_var - 1) % 2
    assert (n_var - 1) * g_blk >= n_blk - g_blk and n_total - n_lead >= 2 and BAND_PIPE_UNROLL % 2 == 0

    def window_block(a):
        wb = (a + 1) * g_blk - n_blk
        return max(wb, 0) if isinstance(a, int) else wb

    def stage(a, slot):
        nxt = a + 1
        scores(nxt, window_block(nxt), min(nxt, n_var - 1) if isinstance(nxt, int) else n_var - 1, 1 - slot)
        finish(a, window_block(a), slot)

    scores(0, window_block(0), 0, 0)
    for a in range(n_lead):
        stage(a, a & 1)

    def trip(i, carry):
        for st in range(BAND_PIPE_UNROLL):
            stage(n_lead + BAND_PIPE_UNROLL * i + st, st & 1)
        return carry

    n_stages = n_total - 1 - n_lead
    lax.fori_loop(0, n_stages // BAND_PIPE_UNROLL, trip, 0)
    for a in range(n_total - 1 - n_stages % BAND_PIPE_UNROLL, n_total - 1):
        stage(a, (a - n_lead) & 1)
    finish(n_total - 1, window_block(n_total - 1), (n_total - 1 - n_lead) & 1)


def _band_attention(q, keys, values, toep, sink, *, variants, n_blk, n_groups, n_qblk, cpg, ones_row):
    bsz, s, _ = q.shape
    qw = n_qblk * LANES
    buffered = isinstance(keys, tuple)
    nb = bsz if s == cpg * CHUNK else 1

    def per_group(a):
        return pl.BlockSpec((nb, a.shape[1], LANES), lambda g, b: (b, 0, g))

    in_specs = [pl.BlockSpec((nb, s, qw), lambda g, b: (b, 0, g))]
    if buffered:
        args = [q, *keys, *values]
        in_specs += [per_group(a) for a in args[1:]]
    else:
        args = [q, keys, values]
        in_specs += [per_group(keys), pl.BlockSpec((nb, 1) + values.shape[2:], lambda g, b: (b, g, 0, 0, 0))]
    for extra in (toep, sink):
        if extra is not None:
            in_specs.append(pl.BlockSpec((1,) + extra.shape[1:], lambda g, b: (g, 0, 0)))
            args.append(extra)
    bias_shape = (n_blk * LANES, cpg * 2 * n_qblk * CHUNK)
    body = functools.partial(_band_body, n_blk=n_blk, n_qblk=n_qblk, cpg=cpg, ones_row=ones_row,
                             has_sink=sink is not None, variants=variants, buffered=buffered)
    return pl.pallas_call(
        body,
        grid=(n_groups, bsz // nb),
        in_specs=in_specs,
        out_specs=pl.BlockSpec((nb, s, qw), lambda g, b: (b, 0, g)),
        out_shape=jax.ShapeDtypeStruct(q.shape, BF16),
        scratch_shapes=[pltpu.VMEM((2,) + bias_shape, F32), pltpu.VMEM((len(variants),) + bias_shape, F32)],
        compiler_params=_params(2),
        name="band_attention",
    )(*args)


def _rope_tables(pos, rot, lane_pattern, from_zero=False):
    half = rot // 2
    inv = jnp.power(ROPE_THETA, -jnp.arange(half, dtype=F32) * 2.0 / rot)
    inv_lane, first, second = [], [], []
    for kind, width in lane_pattern:
        if kind == "rot":
            inv_lane += [inv, inv]
            first += [1.0] * half + [0.0] * half
            second += [0.0] * half + [1.0] * half
        else:
            inv_lane.append(jnp.zeros((width,), F32))
            first += [0.0] * width
            second += [0.0] * width
    inv_lane = jnp.concatenate(inv_lane)[None, :]
    n = pos.shape[0]
    if from_zero and n % CHUNK == 0:
        a_hi = (jnp.arange(n // CHUNK, dtype=F32) * CHUNK)[:, None] * inv_lane
        a_lo = jnp.arange(CHUNK, dtype=F32)[:, None] * inv_lane
        c_hi, s_hi = jnp.cos(a_hi)[:, None, :], jnp.sin(a_hi)[:, None, :]
        c_lo, s_lo = jnp.cos(a_lo)[None], jnp.sin(a_lo)[None]
        cos = (c_hi * c_lo - s_hi * s_lo).reshape(n, LANES)
        sin = (s_hi * c_lo + c_hi * s_lo).reshape(n, LANES)
    else:
        ang = pos.astype(F32)[:, None] * inv_lane
        cos, sin = jnp.cos(ang), jnp.sin(ang)
    return jnp.stack([cos, sin * np.asarray(second, np.float32), -sin * np.asarray(first, np.float32)])


A_ROPE_PATTERN = (("pad", A_NOPE), ("rot", A_ROPE), ("pad", LANES - A_QK))
C_ROPE_PATTERN = (("rot", C_ROT), ("pad", HALF - C_ROT)) * 2


def _prep_ab(pre, post, w_in, q_norm, kv_norm, w_uq, w_ukv, rel_bias, w_out):
    d = w_in.shape[0]
    q_lat, c_kv, k_r, g_a, q_b, k_b, v_b, g_b = jnp.split(
        w_in, [384, 640, 672, 1184, 1696, 2208, 2720], axis=1)
    kr_blk = jnp.concatenate([jnp.zeros((d, A_NOPE), F32), k_r, jnp.zeros((d, LANES - A_QK), F32)], axis=1)
    w_in_p = jnp.concatenate([q_lat, c_kv, kr_blk, g_a, g_b, q_b, k_b, v_b], axis=1).astype(BF16)
    w_uq_p = jnp.pad(w_uq.reshape(A_Q_RANK, A_HEADS, A_QK), ((0, 0), (0, 0), (0, LANES - A_QK)))
    w_uq_p = w_uq_p.reshape(A_Q_RANK, A_HEADS * LANES).astype(BF16)
    ukv = w_ukv.reshape(A_KV_RANK, A_HEADS, A_NOPE + A_V)
    w_uk, w_uv = ukv[..., :A_NOPE], ukv[..., A_NOPE:]
    pad_half = ((0, 0), (0, 0), (0, LANES - A_NOPE))
    w_k = jnp.pad(w_uk, pad_half).reshape(A_KV_RANK, A_HEADS * LANES).astype(BF16)
    w_vt = jnp.pad(w_uv, pad_half).reshape(A_KV_RANK, A_HEADS * LANES).T.astype(BF16)
    wkt = jnp.pad(jnp.transpose(w_uk, (1, 2, 0)), ((0, 0), (0, LANES - A_NOPE), (0, 0))).astype(BF16)
    eye = jnp.eye(A_HEADS, dtype=F32)
    wv_s = (jnp.transpose(w_uv, (1, 0, 2))[:, :, None, :] * eye[:, None, :, None]).reshape(
        A_HEADS, A_KV_RANK, A_WIDTH).astype(BF16)
    sel = (jnp.arange(LANES)[:, None] == A_NOPE + jnp.arange(A_ROPE)[None, :]).astype(BF16)
    rel_bias = rel_bias * LOG2E
    win_p, win_s = B_WIN_BLOCKS * LANES, B_SAMPLE_BLOCKS * LANES
    r0 = win_p - CHUNK
    x_len = r0 + win_p
    n_vec = x_len + CHUNK
    n_hi = r0 + CHUNK - 1 - B_MAX_REL
    n_lo = n_vec - n_hi - (2 * B_MAX_REL + 1)
    vec = jnp.concatenate([jnp.broadcast_to(rel_bias[:, -1:], (B_HEADS, n_hi)), rel_bias[:, ::-1],
                           jnp.broadcast_to(rel_bias[:, :1], (B_HEADS, n_lo))], axis=1)
    skew = jnp.tile(vec, (1, CHUNK))[:, :CHUNK * (n_vec - 1)].reshape(B_HEADS, CHUNK, n_vec - 1)
    toep = skew[:, :, CHUNK - 1:CHUNK - 1 + x_len]
    toep = jnp.transpose(toep.reshape(B_HEADS // 2, 2, CHUNK, x_len), (0, 3, 1, 2)).reshape(B_HEADS // 2, x_len, LANES)
    var_p = tuple(tuple((r0 - c * CHUNK, c - B_PAST_CHUNKS, c)
                        for c in range(B_GROUP_CHUNKS * v, B_GROUP_CHUNKS * (v + 1))) for v in range(B_VARIANTS))
    var_s = (((r0 - (win_s - CHUNK), 1, B_PAST_CHUNKS + 1),),)
    return dict(pre=pre[None], w_in=w_in_p, q_norm=q_norm[None], kv_norm=kv_norm[None], w_uq=w_uq_p, w_k=w_k,
                w_vt=w_vt, wkt=wkt, wv_s=wv_s, sel=sel, toep=toep, var_p=var_p, var_s=var_s,
                w_out=w_out.astype(BF16), post=post[None])


def _prep_c(pre, post, w_in, sinks, w_out):
    q, k, v, g = jnp.split(w_in, [1024, 1152, 1280], axis=1)
    k0, k1 = k[:, :C_HEAD_DIM], k[:, C_HEAD_DIM:]
    v0, v1 = v[:, :C_HEAD_DIM], v[:, C_HEAD_DIM:]
    zero = jnp.zeros_like(v0)
    w_in_p = jnp.concatenate([q, k0, k0, k1, k1, v0, zero, v1, zero, g], axis=1).astype(BF16)
    var_p = tuple(tuple((0, c - C_PAST_CHUNKS, c) for c in range(C_GROUP_CHUNKS * v, C_GROUP_CHUNKS * (v + 1)))
                  for v in range(C_VARIANTS))
    var_s = (((0, 1, C_PAST_CHUNKS + 1),),)
    sink_row = jnp.repeat((sinks * LOG2E).reshape(C_KV_HEADS, C_GROUP), CHUNK, axis=1)[:, None, :]
    return dict(pre=pre[None], w_in=w_in_p, var_p=var_p, var_s=var_s, sink_s=sink_row,
                sink_p=jnp.tile(sink_row, (1, 1, C_GROUP_CHUNKS)), w_out=w_out.astype(BF16), post=post[None])


def _dup_heads(x):
    return jnp.concatenate([x[:, :, 0], x[:, :, 0], x[:, :, 1], x[:, :, 1]], axis=-1)


def kernel(x_prompt, x_sample, cache_a_ckv, cache_a_krope, cache_b_k, cache_b_v, cache_c_k, cache_c_v,
           ab_pre_norm, ab_post_norm, ab_w_in, ab_q_norm, ab_kv_norm, ab_w_uq, ab_w_ukv, ab_rel_bias, ab_w_out,
           c_pre_norm, c_post_norm, c_w_in, c_sinks, c_w_out):
    bsz, seq, _ = x_prompt.shape
    dbs, dseq, _ = x_sample.shape
    past = cache_a_ckv.shape[2]
    n_s = dbs * dseq
    pos_p = jnp.arange(seq, dtype=jnp.int32)
    pos_s = jnp.tile(past + jnp.arange(dseq, dtype=jnp.int32), dbs)
    wab = _prep_ab(ab_pre_norm[0], ab_post_norm[0], ab_w_in[0], ab_q_norm[0], ab_kv_norm[0], ab_w_uq[0],
                   ab_w_ukv[0], ab_rel_bias[0], ab_w_out[0])
    wc = _prep_c(c_pre_norm[0], c_post_norm[0], c_w_in[0], c_sinks[0], c_w_out[0])
    b_tail = min(B_PAST_CHUNKS * CHUNK, seq)
    c_tail = min(C_WINDOW, seq)
    tile = 512

    rope_a_p = _rope_tables(pos_p, A_ROPE, A_ROPE_PATTERN, from_zero=True)
    (qa, ka, vt, qb, kb, vbt, sg, c_new_p, kr_new_p, kb_tail, vb_tail) = _ab_in_proj(
        x_prompt, rope_a_p, wab, tm=tile, tail_len=b_tail, tkv=tile)
    o_a = _mla_prompt(qa, ka, vt, tq=tile)
    o_b = _band_attention(qb, kb, vbt, wab["toep"], None, variants=wab["var_p"], n_blk=B_WIN_BLOCKS,
                          n_groups=B_HEADS // 2, n_qblk=1, cpg=B_GROUP_CHUNKS, ones_row=False)

    rope_a_s = _rope_tables(pos_s, A_ROPE, A_ROPE_PATTERN)
    xs = x_sample.reshape(1, n_s, D_MODEL)
    (qa_s, _, _, qb_s, _, _, sg_s, c_new_s, kr_new_s, kb_s32, vb_s32) = _ab_in_proj(
        xs, rope_a_s, wab, tm=n_s, tail_len=n_s, tkv=n_s)
    o_a_s = _mla_sample(qa_s.reshape(dbs, dseq, -1), cache_a_ckv[0], cache_a_krope[0],
                        c_new_s.reshape(dbs, dseq, -1), kr_new_s.reshape(dbs, dseq, -1),
                        wab["wkt"], wab["wv_s"], wab["sel"], tk=512)
    wb = cache_b_k.shape[2]
    o_b_s = _band_attention(qb_s.reshape(dbs, dseq, -1),
                            (cache_b_k[0].reshape(dbs, wb, B_WIDTH), kb_s32.reshape(dbs, dseq, -1)),
                            (cache_b_v[0].reshape(dbs, wb, B_WIDTH), vb_s32.reshape(dbs, dseq, -1)),
                            wab["toep"], None, variants=wab["var_s"], n_blk=B_SAMPLE_BLOCKS, n_groups=B_HEADS // 2,
                            n_qblk=1, cpg=1, ones_row=False)

    rope_c_p = _rope_tables(pos_p, C_ROT, C_ROPE_PATTERN, from_zero=True)
    h1_p, qc, kc, vct, sgc, kc_tail, vc_tail = _c_in_proj(
        [o_a, o_b], sg, wab["w_out"], wab["post"], x_prompt, rope_c_p, wc, tm=tile, tail_len=c_tail)
    o_c = _band_attention(qc, kc, vct, None, wc["sink_p"], variants=wc["var_p"], n_blk=C_WIN_BLOCKS,
                          n_groups=C_KV_HEADS, n_qblk=C_GROUP // 2, cpg=C_GROUP_CHUNKS, ones_row=True)
    h2_p = _out_proj([o_c], sgc, wc["w_out"], wc["post"], h1_p, tm=tile)

    rope_c_s = _rope_tables(pos_s, C_ROT, C_ROPE_PATTERN)
    h1_s, qc_s, kc_s, _, sgc_s, kc_s32, vc_s32 = _c_in_proj(
        [o_a_s.reshape(1, n_s, -1), o_b_s.reshape(1, n_s, -1)], sg_s, wab["w_out"], wab["post"], xs, rope_c_s, wc,
        tm=n_s, tail_len=n_s)
    wcw = cache_c_k.shape[2]
    win_c = C_SAMPLE_BLOCKS * LANES
    n_pad = win_c - wcw - dseq
    kcb = jnp.concatenate([jnp.zeros((dbs, n_pad, C_KV_HEADS * LANES), BF16), _dup_heads(cache_c_k[0]).astype(BF16),
                           kc_s.reshape(dbs, dseq, -1)], axis=1)
    vcb = jnp.concatenate([jnp.zeros((dbs, n_pad, C_KV_HEADS, C_HEAD_DIM), F32), cache_c_v[0],
                           vc_s32.reshape(dbs, dseq, C_KV_HEADS, C_HEAD_DIM)], axis=1)
    vcb_t = jnp.concatenate([jnp.transpose(vcb, (0, 2, 3, 1)), jnp.ones((dbs, C_KV_HEADS, 1, win_c), F32),
                             jnp.zeros((dbs, C_KV_HEADS, LANES - C_HEAD_DIM - 1, win_c), F32)], axis=2)
    vcb_t = jnp.transpose(vcb_t.astype(BF16).reshape(dbs, C_KV_HEADS, LANES, C_SAMPLE_BLOCKS, LANES), (0, 1, 3, 2, 4))
    o_c_s = _band_attention(qc_s.reshape(dbs, dseq, -1), kcb, vcb_t, None, wc["sink_s"], variants=wc["var_s"],
                            n_blk=C_SAMPLE_BLOCKS, n_groups=C_KV_HEADS, n_qblk=C_GROUP // 2, cpg=1, ones_row=True)
    h2_s = _out_proj([o_c_s.reshape(1, n_s, -1)], sgc_s, wc["w_out"], wc["post"], h1_s, tm=n_s)

    def roll_in(buf, new):
        return jnp.concatenate([buf, new], axis=1)[:, -buf.shape[1]:][None]

    return (h2_p, h2_s.reshape(dbs, dseq, D_MODEL),
            c_new_p[None], kr_new_p[None],
            kb_tail.reshape(1, bsz, b_tail, B_HEADS, B_HEAD_DIM), vb_tail.reshape(1, bsz, b_tail, B_HEADS, B_HEAD_DIM),
            kc_tail.reshape(1, bsz, c_tail, C_KV_HEADS, C_HEAD_DIM), vc_tail.reshape(1, bsz, c_tail, C_KV_HEADS, C_HEAD_DIM),
            c_new_s.reshape(1, dbs, dseq, A_KV_RANK), kr_new_s.reshape(1, dbs, dseq, A_ROPE),
            roll_in(cache_b_k[0], kb_s32.reshape(dbs, dseq, B_HEADS, B_HEAD_DIM)),
            roll_in(cache_b_v[0], vb_s32.reshape(dbs, dseq, B_HEADS, B_HEAD_DIM)),
            roll_in(cache_c_k[0], kc_s32.reshape(dbs, dseq, C_KV_HEADS, C_HEAD_DIM)),
            roll_in(cache_c_v[0], vc_s32.reshape(dbs, dseq, C_KV_HEADS, C_HEAD_DIM)))
```

```python
import functools

import jax
import jax.numpy as jnp
import numpy as np
from jax import lax
from jax.experimental import pallas as pl
from jax.experimental.pallas import tpu as pltpu

F32 = jnp.float32
BF16 = jnp.bfloat16

D_MODEL = 1024
CHUNK = 64
ROPE_THETA = 500000.0
RMS_EPS = 1e-6
NEG_INF = -1e30

A_HEADS = 8
A_NOPE = 64
A_ROPE = 32
A_QK = A_NOPE + A_ROPE
A_V = 64
A_Q_RANK = 384
A_KV_RANK = 256
A_WIDTH = A_HEADS * A_V
LOG2E = 1.4426950408889634
A_SCALE = A_QK ** -0.5 * LOG2E

B_HEADS = 8
B_HEAD_DIM = 64
B_WIDTH = B_HEADS * B_HEAD_DIM
B_PAST_CHUNKS = 8
B_MAX_REL = 128
B_SCALE = B_HEAD_DIM ** -0.5 * LOG2E

C_HEADS = 16
C_KV_HEADS = 2
C_GROUP = C_HEADS // C_KV_HEADS
C_HEAD_DIM = 64
C_WIDTH = C_HEADS * C_HEAD_DIM
C_WINDOW = 128
C_PAST_CHUNKS = C_WINDOW // CHUNK
C_ROT = C_HEAD_DIM // 4
C_SCALE = C_HEAD_DIM ** -0.5 * LOG2E

LANES = 128
HALF = LANES // 2
BF16_SUBLANES = 16
VMEM_LIMIT = 56 * 1024 * 1024
A_MASK_CHUNKS = 8
assert A_QK + A_MASK_CHUNKS <= LANES
A_PIPE_UNROLL = 4
BAND_PIPE_UNROLL = 4
B_GROUP_CHUNKS = 4
C_GROUP_CHUNKS = 2
B_WIN_BLOCKS = (B_PAST_CHUNKS + B_GROUP_CHUNKS) * CHUNK // LANES
C_WIN_BLOCKS = (C_PAST_CHUNKS + C_GROUP_CHUNKS) * CHUNK // LANES
B_VARIANTS = B_PAST_CHUNKS // B_GROUP_CHUNKS + 1
C_VARIANTS = C_PAST_CHUNKS // C_GROUP_CHUNKS + 1
B_SAMPLE_BLOCKS = (B_PAST_CHUNKS + 2) * CHUNK // LANES
C_SAMPLE_BLOCKS = (C_PAST_CHUNKS + 2) * CHUNK // LANES

AB_Q0, AB_C0, AB_KR0, AB_G0, AB_QB0, AB_KB0, AB_VB0, AB_NZ = 0, 384, 640, 768, 1792, 2304, 2816, 3328
C_Q0, C_K0, C_V0, C_G0, C_NZ = 0, 1024, 1280, 1536, 2560


def _params(n_axes):
    return pltpu.CompilerParams(dimension_semantics=("arbitrary",) * n_axes, vmem_limit_bytes=VMEM_LIMIT)


def _rms(x, g):
    return x * lax.rsqrt(jnp.mean(x * x, axis=-1, keepdims=True) + RMS_EPS) * g


def _rope_block(blk, rope_ref, shift):
    return (blk * rope_ref[0] + pltpu.roll(blk, shift, 1) * rope_ref[1]
            + pltpu.roll(blk, LANES - shift, 1) * rope_ref[2])


def _tail_spec(n_tiles, tm, tail_len, width):
    tb = min(tail_len, tm)
    n_blk = tail_len // tb
    return tb, pl.BlockSpec((1, tb, width), lambda b, i: (b, jnp.maximum(i - (n_tiles - n_blk), 0), 0))


def _ab_in_body(x_ref, pre_ref, w_ref, qn_ref, kvn_ref, wuq_ref, wk_ref, wvt_ref, rope_ref,
                qa_ref, ka_ref, vt_ref, qb_ref, kb_ref, vbt_ref, sg_ref, c_ref, kr_ref, kb_tail_ref, vb_tail_ref):
    tm = x_ref.shape[1]
    tb = kb_tail_ref.shape[1]
    xn = _rms(x_ref[0], pre_ref[...]).astype(BF16)
    z = jnp.dot(xn, w_ref[...], preferred_element_type=F32)

    qn = _rms(z[:, AB_Q0:AB_C0], qn_ref[...]).astype(BF16)
    qa = jnp.dot(qn, wuq_ref[...], preferred_element_type=F32) * A_SCALE
    for h in range(A_HEADS):
        blk = slice(h * LANES, (h + 1) * LANES)
        qa_ref[0, :, blk] = _rope_block(qa[:, blk], rope_ref, A_ROPE // 2).astype(BF16)

    c_new = _rms(z[:, AB_C0:AB_KR0], kvn_ref[...])
    c_ref[0] = c_new
    cb = c_new.astype(BF16)
    krot = _rope_block(z[:, AB_KR0:AB_G0], rope_ref, A_ROPE // 2)
    kr_ref[0] = krot.T[A_NOPE:A_NOPE + A_ROPE, :]
    kn = jnp.dot(cb, wk_ref[...], preferred_element_type=F32)
    row = pl.program_id(1) * tm + lax.broadcasted_iota(jnp.int32, (tm, LANES), 0)
    lane = lax.broadcasted_iota(jnp.int32, (tm, LANES), 1)
    k_shared = krot + jnp.where(lane - A_QK == (row // CHUNK) % A_MASK_CHUNKS, 1.0, 0.0)
    for h in range(A_HEADS):
        blk = slice(h * LANES, (h + 1) * LANES)
        ka_ref[0, :, blk] = (kn[:, blk] + k_shared).astype(BF16)
    vt = jnp.dot(wvt_ref[...], c_new.T.astype(BF16), preferred_element_type=F32)
    row = lax.broadcasted_iota(jnp.int32, vt.shape, 0)
    vt_ref[0, 0] = jnp.where((row & (LANES - 1)) == A_V, 1.0, vt).astype(BF16)

    g = z[:, AB_G0:AB_QB0]
    sg_ref[0] = (g * jax.nn.sigmoid(g)).astype(BF16)
    qb_ref[0] = (z[:, AB_QB0:AB_KB0] * B_SCALE).astype(BF16)
    kb = z[:, AB_KB0:AB_VB0]
    vb = z[:, AB_VB0:AB_NZ]
    kb_ref[0] = kb.astype(BF16)
    vbt = vb.T.astype(BF16)
    for hp in range(B_HEADS // 2):
        for j in range(tm // LANES):
            vbt_ref[0, hp, j] = vbt[hp * LANES:(hp + 1) * LANES, j * LANES:(j + 1) * LANES]
    kb_tail_ref[0] = kb[tm - tb:, :]
    vb_tail_ref[0] = vb[tm - tb:, :]


def _ab_in_proj(x, rope, w, tm, tail_len, tkv):
    bsz, s, _ = x.shape
    nt = s // tm
    per_kv = tkv // tm
    tb, tail_spec = _tail_spec(nt, tm, tail_len, B_WIDTH)

    def full(a):
        return pl.BlockSpec(a.shape, lambda b, i: (0,) * a.ndim, pipeline_mode=pl.Buffered(1))

    def rows(width):
        return pl.BlockSpec((1, tm, width), lambda b, i: (b, i, 0))

    weights = (w["pre"], w["w_in"], w["q_norm"], w["kv_norm"], w["w_uq"], w["w_k"], w["w_vt"])
    vt_spec = pl.BlockSpec((1, 1, A_HEADS * LANES, tm), lambda b, i: (b, i // per_kv, 0, i % per_kv))
    out_shape = (
        jax.ShapeDtypeStruct((bsz, s, A_HEADS * LANES), BF16),
        jax.ShapeDtypeStruct((bsz, s, A_HEADS * LANES), BF16),
        jax.ShapeDtypeStruct((bsz, s // tkv, A_HEADS * LANES, tkv), BF16),
        jax.ShapeDtypeStruct((bsz, s, B_WIDTH), BF16),
        jax.ShapeDtypeStruct((bsz, s, B_WIDTH), BF16),
        jax.ShapeDtypeStruct((bsz, B_HEADS // 2, s // LANES, LANES, LANES), BF16),
        jax.ShapeDtypeStruct((bsz, s, A_WIDTH + B_WIDTH), BF16),
        jax.ShapeDtypeStruct((bsz, s, A_KV_RANK), F32),
        jax.ShapeDtypeStruct((bsz, A_ROPE, s), F32),
        jax.ShapeDtypeStruct((bsz, tail_len, B_WIDTH), F32),
        jax.ShapeDtypeStruct((bsz, tail_len, B_WIDTH), F32),
    )
    vbt_spec = pl.BlockSpec((1, B_HEADS // 2, tm // LANES, LANES, LANES), lambda b, i: (b, 0, i, 0, 0))
    out_specs = (rows(1024), rows(1024), vt_spec, rows(512), rows(512), vbt_spec, rows(1024),
                 rows(A_KV_RANK), pl.BlockSpec((1, A_ROPE, tm), lambda b, i: (b, 0, i)), tail_spec, tail_spec)
    return pl.pallas_call(
        _ab_in_body,
        grid=(bsz, nt),
        in_specs=[rows(D_MODEL)] + [full(a) for a in weights]
        + [pl.BlockSpec((3, tm, LANES), lambda b, i: (0, i, 0))],
        out_specs=out_specs,
        out_shape=out_shape,
        compiler_params=_params(2),
        name="ab_in_proj",
    )(x, *weights, rope)


def _mix_out(o_refs, sg_ref, w_ref, g_ref, h_ref):
    o = jnp.concatenate([r[0].astype(F32) for r in o_refs], axis=-1) if len(o_refs) > 1 else o_refs[0][0].astype(F32)
    mixed = (o * sg_ref[0].astype(F32)).astype(BF16)
    y = jnp.dot(mixed, w_ref[...], preferred_element_type=F32)
    return h_ref[0] + _rms(y, g_ref[...])


def _c_in_body(*refs):
    n_o = len(refs) - 14
    o_refs = refs[:n_o]
    (sg0_ref, w0_ref, post0_ref, x_ref, pre_ref, w_ref, rope_ref,
     h_ref, q_ref, k_ref, vt_ref, sg_ref, k_tail_ref, v_tail_ref) = refs[n_o:]
    tm = x_ref.shape[1]
    tb = k_tail_ref.shape[1]
    h = _mix_out(o_refs, sg0_ref, w0_ref, post0_ref, x_ref)
    h_ref[0] = h
    xn = _rms(h, pre_ref[...]).astype(BF16)
    z = jnp.dot(xn, w_ref[...], preferred_element_type=F32)
    half_rot = C_ROT // 2
    for j in range(C_WIDTH // LANES):
        blk = slice(C_Q0 + j * LANES, C_Q0 + (j + 1) * LANES)
        q_ref[0, :, j * LANES:(j + 1) * LANES] = (_rope_block(z[:, blk], rope_ref, half_rot) * C_SCALE).astype(BF16)
    kd = [_rope_block(z[:, C_K0 + j * LANES:C_K0 + (j + 1) * LANES], rope_ref, half_rot) for j in range(C_KV_HEADS)]
    vz = [z[:, C_V0 + j * LANES:C_V0 + (j + 1) * LANES] for j in range(C_KV_HEADS)]
    row = lax.broadcasted_iota(jnp.int32, (LANES, tm), 0)
    for j in range(C_KV_HEADS):
        k_ref[0, :, j * LANES:(j + 1) * LANES] = kd[j].astype(BF16)
        vt = jnp.where(row == C_HEAD_DIM, 1.0, vz[j].T).astype(BF16)
        for i in range(tm // LANES):
            vt_ref[0, j, i] = vt[:, i * LANES:(i + 1) * LANES]
    g = z[:, C_G0:C_NZ]
    sg_ref[0] = (g * jax.nn.sigmoid(g)).astype(BF16)
    lo = lax.broadcasted_iota(jnp.int32, (1, LANES), 1) < HALF
    k_tail_ref[0] = jnp.where(lo, kd[0], kd[1])[tm - tb:, :]
    v_tail_ref[0] = (vz[0] + pltpu.roll(vz[1], HALF, 1))[tm - tb:, :]


def _c_in_proj(o_parts, sg0, w_out0, post0, x, rope, w, tm, tail_len):
    bsz, s, _ = x.shape
    nt = s // tm
    tb, tail_spec = _tail_spec(nt, tm, tail_len, LANES)

    def full(a):
        return pl.BlockSpec(a.shape, lambda b, i: (0,) * a.ndim)

    def rows(width):
        return pl.BlockSpec((1, tm, width), lambda b, i: (b, i, 0))

    out_shape = (
        jax.ShapeDtypeStruct((bsz, s, D_MODEL), F32),
        jax.ShapeDtypeStruct((bsz, s, C_WIDTH), BF16),
        jax.ShapeDtypeStruct((bsz, s, C_KV_HEADS * LANES), BF16),
        jax.ShapeDtypeStruct((bsz, C_KV_HEADS, s // LANES, LANES, LANES), BF16),
        jax.ShapeDtypeStruct((bsz, s, C_WIDTH), BF16),
        jax.ShapeDtypeStruct((bsz, tail_len, LANES), F32),
        jax.ShapeDtypeStruct((bsz, tail_len, LANES), F32),
    )
    return pl.pallas_call(
        _c_in_body,
        grid=(bsz, nt),
        in_specs=[rows(o.shape[-1]) for o in o_parts]
        + [rows(D_MODEL), full(w_out0), full(post0), rows(D_MODEL), full(w["pre"]), full(w["w_in"]),
           pl.BlockSpec((3, tm, LANES), lambda b, i: (0, i, 0))],
        out_specs=(rows(D_MODEL), rows(C_WIDTH), rows(256),
                   pl.BlockSpec((1, C_KV_HEADS, tm // LANES, LANES, LANES), lambda b, i: (b, 0, i, 0, 0)),
                   rows(C_WIDTH), tail_spec, tail_spec),
        out_shape=out_shape,
        compiler_params=_params(2),
        name="c_in_proj",
    )(*o_parts, sg0, w_out0, post0, x, w["pre"], w["w_in"], rope)


def _out_body(*refs):
    o_refs, (sg_ref, w_ref, g_ref, h_ref, out_ref) = refs[:-5], refs[-5:]
    out_ref[0] = _mix_out(o_refs, sg_ref, w_ref, g_ref, h_ref)


def _out_proj(o_parts, sg, w_out, post_g, h, tm):
    bsz, s, _ = h.shape

    def rows(width):
        return pl.BlockSpec((1, tm, width), lambda b, i: (b, i, 0))

    def full(a):
        return pl.BlockSpec(a.shape, lambda b, i: (0,) * a.ndim)

    return pl.pallas_call(
        _out_body,
        grid=(bsz, s // tm),
        in_specs=[rows(o.shape[-1]) for o in o_parts] + [rows(D_MODEL), full(w_out), full(post_g), rows(D_MODEL)],
        out_specs=rows(D_MODEL),
        out_shape=jax.ShapeDtypeStruct(h.shape, F32),
        compiler_params=_params(2),
        name="out_proj",
    )(*o_parts, sg, w_out, post_g, h)


def _mla_prompt_body(q_ref, k_ref, vt_ref, qmask_ref, o_ref, m_sc, acc_sc, s_sc, *, tq):
    n_tiles = q_ref.shape[1] // tq
    n_pairs = n_tiles * (n_tiles + 1) // 2
    assert A_PIPE_UNROLL % 2 == 0 and n_pairs % A_PIPE_UNROLL == 0 and n_pairs >= 2 * A_PIPE_UNROLL
    m_sc[...] = jnp.full(m_sc.shape, NEG_INF, F32)
    acc_sc[...] = jnp.zeros(acc_sc.shape, F32)
    contract_last = (((1,), (1,)), ((), ()))

    def scores(qi, j, slot):
        q_start = pl.multiple_of(qi * tq, tq)
        k_start = pl.multiple_of(j * tq, tq)
        qmask = jnp.where(j == qi, qmask_ref[...], jnp.zeros_like(qmask_ref[...]))
        for hh in range(2):
            blk = slice(hh * LANES, (hh + 1) * LANES)
            q = q_ref[0, pl.ds(q_start, tq), blk] + qmask
            s_sc[slot, hh] = lax.dot_general(k_ref[0, pl.ds(k_start, tq), blk], q, contract_last,
                                             preferred_element_type=F32)

    def consume(qi, j, slot):
        for hh in range(2):
            s = s_sc[slot, hh]
            vt = vt_ref[0, j, hh * LANES:(hh + 1) * LANES, :]
            m_old = m_sc[qi, hh]
            m_new = jnp.maximum(m_old, jnp.max(s, axis=0, keepdims=True))
            p = jnp.exp2(s - m_new).astype(BF16)
            acc_sc[qi, hh] = jnp.exp2(m_old - m_new) * acc_sc[qi, hh] + jnp.dot(vt, p, preferred_element_type=F32)
            m_sc[qi, hh] = m_new

    def advance(qi, j):
        last = j == qi
        return jnp.where(last, qi + 1, qi), jnp.where(last, 0, j + 1)

    def stages(cur, count):
        for st in range(count):
            nxt = advance(*cur)
            scores(*nxt, 1 - (st & 1))
            consume(*cur, st & 1)
            cur = nxt
        return cur

    zero = jnp.int32(0)
    scores(zero, zero, 0)
    cur = lax.fori_loop(0, n_pairs // A_PIPE_UNROLL - 1, lambda i, c: stages(c, A_PIPE_UNROLL), (zero, zero))
    cur = stages(cur, A_PIPE_UNROLL - 1)
    consume(*cur, (A_PIPE_UNROLL - 1) & 1)

    lo = lax.broadcasted_iota(jnp.int32, (1, LANES), 1) < HALF

    def write_tile(qi, carry):
        outs = []
        for hh in range(2):
            acc = acc_sc[qi, hh]
            outs.append((acc / acc[A_V:A_V + 1, :]).T)
        o_ref[0, pl.ds(pl.multiple_of(qi * tq, tq), tq), :] = jnp.where(
            lo, outs[0], pltpu.roll(outs[1], HALF, 1)).astype(o_ref.dtype)
        return carry

    lax.fori_loop(0, n_tiles, write_tile, 0)


def _mla_prompt(qa, ka, vt, tq):
    bsz, s, _ = qa.shape
    pairs = A_HEADS // 2
    n_tiles = s // tq
    assert vt.shape[3] == tq and tq == A_MASK_CHUNKS * CHUNK
    q_chunk = np.arange(tq)[:, None] // CHUNK
    lane_chunk = np.arange(LANES)[None, :] - A_QK
    qmask = np.where((lane_chunk > q_chunk) & (lane_chunk < A_MASK_CHUNKS), NEG_INF, 0.0).astype(np.float32)
    return pl.pallas_call(
        functools.partial(_mla_prompt_body, tq=tq),
        grid=(bsz, pairs),
        in_specs=[pl.BlockSpec((1, s, 2 * LANES), lambda b, h: (b, 0, h)),
                  pl.BlockSpec((1, s, 2 * LANES), lambda b, h: (b, 0, h)),
                  pl.BlockSpec((1, n_tiles, 2 * LANES, tq), lambda b, h: (b, 0, h, 0)),
                  pl.BlockSpec((tq, LANES), lambda b, h: (0, 0))],
        out_specs=pl.BlockSpec((1, s, LANES), lambda b, h: (b, 0, h)),
        out_shape=jax.ShapeDtypeStruct((bsz, s, A_WIDTH), BF16),
        scratch_shapes=[pltpu.VMEM((n_tiles, 2, 1, tq), F32), pltpu.VMEM((n_tiles, 2, LANES, tq), F32),
                        pltpu.VMEM((2, 2, tq, tq), F32)],
        compiler_params=_params(2),
        name="mla_prompt",
    )(qa, ka, vt, jnp.asarray(qmask, BF16))


def _mla_sample_body(q_ref, cc_ref, ckr_ref, cn_ref, krn_ref, wkt_ref, wv_ref, sel_ref, o_ref,
                     qabs_sc, qr_sc, m_sc, acc_sc, *, tk):
    t = q_ref.shape[1]
    past = cc_ref.shape[1]
    for h in range(A_HEADS):
        qh = q_ref[0, :, h * LANES:(h + 1) * LANES]
        rows = slice(h * t, (h + 1) * t)
        qabs_sc[rows, :] = jnp.dot(qh, wkt_ref[h], preferred_element_type=F32).astype(BF16)
        qr_sc[rows, :] = jnp.dot(qh, sel_ref[...], preferred_element_type=F32).astype(BF16)
    m_sc[...] = jnp.full(m_sc.shape, NEG_INF, F32)
    acc_sc[...] = jnp.zeros(acc_sc.shape, F32)
    contract_last = (((1,), (1,)), ((), ()))

    def scores(c_t, kr_tt):
        return (lax.dot_general(c_t.astype(BF16), qabs_sc[...], contract_last, preferred_element_type=F32)
                + lax.dot_general(kr_tt.T.astype(BF16), qr_sc[...], contract_last, preferred_element_type=F32))

    def update(s, c_t):
        n_keys = c_t.shape[0]
        m_old = m_sc[...]
        m_new = jnp.maximum(m_old, jnp.max(s, axis=0, keepdims=True))
        p = jnp.exp2(s - m_new).astype(BF16)
        ct = jnp.concatenate([c_t.T.astype(BF16), jnp.ones((BF16_SUBLANES, n_keys), BF16)], axis=0)
        acc_sc[...] = jnp.exp2(m_old - m_new) * acc_sc[...] + jnp.dot(ct, p, preferred_element_type=F32)
        m_sc[...] = m_new

    def tile(j):
        return cc_ref[0, j * tk:(j + 1) * tk, :], ckr_ref[0, :, j * tk:(j + 1) * tk]

    n_tiles = past // tk
    cur = tile(0)
    s = scores(*cur)
    for j in range(n_tiles):
        nxt = tile(j + 1) if j + 1 < n_tiles else (cn_ref[0], krn_ref[0])
        s_next = scores(*nxt)
        update(s, cur[0])
        cur, s = nxt, s_next
    update(s, cur[0])

    acc = acc_sc[...]
    o_lat = (acc[:A_KV_RANK] / acc[A_KV_RANK:A_KV_RANK + 1]).T.astype(BF16)
    out = jnp.zeros((t, A_WIDTH), F32)
    for h in range(A_HEADS):
        out = out + jnp.dot(o_lat[h * t:(h + 1) * t, :], wv_ref[h], preferred_element_type=F32)
    o_ref[0] = out.astype(o_ref.dtype)


def _mla_sample(qa, cache_c, cache_kr, c_new, kr_new, wkt, wv, sel, tk):
    bsz, t, _ = qa.shape
    past = cache_c.shape[1]

    def per_b(shape):
        return pl.BlockSpec((1,) + shape, lambda b: (b, 0, 0))

    def full(a):
        return pl.BlockSpec(a.shape, lambda b: (0,) * a.ndim)

    rows = A_HEADS * t
    return pl.pallas_call(
        functools.partial(_mla_sample_body, tk=tk),
        grid=(bsz,),
        in_specs=[per_b((t, A_HEADS * LANES)), per_b((past, A_KV_RANK)), per_b((A_ROPE, past)),
                  per_b((t, A_KV_RANK)), per_b((A_ROPE, t)), full(wkt), full(wv), full(sel)],
        out_specs=per_b((t, A_WIDTH)),
        out_shape=jax.ShapeDtypeStruct((bsz, t, A_WIDTH), BF16),
        scratch_shapes=[pltpu.VMEM((rows, A_KV_RANK), BF16), pltpu.VMEM((rows, A_ROPE), BF16),
                        pltpu.VMEM((1, rows), F32), pltpu.VMEM((A_KV_RANK + BF16_SUBLANES, rows), F32)],
        compiler_params=_params(1),
        name="mla_sample",
    )(qa, cache_c, cache_kr, c_new, kr_new, wkt, wv, sel)


def _band_body(*refs, n_blk, n_qblk, cpg, ones_row, has_sink, variants, buffered=False):
    refs = list(refs)
    n_in = 5 if buffered else 3
    q_ref = refs[0]
    o_ref, s_sc, bias_sc = refs[-3:]
    extra = refs[n_in:-3]
    sink_ref = extra.pop() if has_sink else None
    toep_ref = extra.pop() if extra else None
    win = n_blk * LANES
    rows_g = cpg * CHUNK
    n_var = len(variants)
    qpc = 2 * n_qblk * CHUNK
    lo = lax.broadcasted_iota(jnp.int32, (CHUNK, LANES), 1) < HALF
    contract_last = (((1,), (1,)), ((), ()))

    if toep_ref is not None:
        key_chunk = lax.broadcasted_iota(jnp.int32, (win, LANES), 0) // CHUNK
        for v, chunks in enumerate(variants):
            for cc, (x0, lo_chunk, hi_chunk) in enumerate(chunks):
                tile = jnp.where((key_chunk >= lo_chunk) & (key_chunk <= hi_chunk), toep_ref[0, x0:x0 + win, :], NEG_INF)
                for j in range(qpc // LANES):
                    bias_sc[v, :, cc * qpc + j * LANES:cc * qpc + (j + 1) * LANES] = tile

    def store_scores(slot, variant, s):
        if toep_ref is not None:
            s_sc[slot] = s + bias_sc[variant]
            return
        s_sc[slot] = s
        for cc, (_, lo_chunk, hi_chunk) in enumerate(variants[variant]):
            cols = slice(cc * qpc, (cc + 1) * qpc)
            for r0, r1 in ((0, lo_chunk * CHUNK), ((hi_chunk + 1) * CHUNK, win)):
                if r1 > max(r0, 0):
                    s_sc[slot, max(r0, 0):r1, cols] = jnp.full((r1 - max(r0, 0), qpc), NEG_INF, F32)

    def aligned(x, n):
        return x if isinstance(x, int) else pl.multiple_of(x, n)

    def buffered_window(buf_ref, new_ref, bi, transposed):
        buf_t, new = buf_ref[bi], new_ref[bi]
        pad = jnp.zeros((win - buf_t.shape[1] - new.shape[0], LANES), F32)
        if transposed:
            return jnp.concatenate([buf_t, jnp.concatenate([new, pad], axis=0).T], axis=1)
        return jnp.concatenate([buf_t.T, new, pad], axis=0)

    def scores(a, wb, variant, slot, bi=0):
        pieces = []
        for cc in range(cpg):
            rows = pl.ds(aligned(a * rows_g + cc * CHUNK, CHUNK), CHUNK)
            for r in range(n_qblk):
                qblk = q_ref[bi, rows, r * LANES:(r + 1) * LANES]
                zero = jnp.zeros_like(qblk)
                pieces += [jnp.where(lo, qblk, zero), jnp.where(lo, zero, qblk)]
        qs = jnp.concatenate(pieces, axis=0)
        if buffered:
            kw = buffered_window(refs[1], refs[2], bi, False).astype(BF16)
        else:
            kw = refs[1][bi, pl.ds(aligned(wb * LANES, LANES), win), :]
        store_scores(slot, variant, lax.dot_general(kw, qs, contract_last, preferred_element_type=F32))

    def finish(a, wb, slot, bi=0):
        s = s_sc[slot]
        m = jnp.max(s, axis=0, keepdims=True)
        if has_sink:
            sink = sink_ref[0]
            m = jnp.maximum(m, sink)
        p = jnp.exp2(s - m)
        if buffered:
            vt = buffered_window(refs[3], refs[4], bi, True).astype(BF16)
        else:
            vt = jnp.concatenate([refs[2][bi, 0, wb + i] for i in range(n_blk)], axis=1)
        if not ones_row:
            vt = jnp.concatenate([vt, jnp.ones((BF16_SUBLANES, win), BF16)], axis=0)
        o = jnp.dot(vt, p.astype(BF16), preferred_element_type=F32)
        l = o[CHUNK:CHUNK + 1, :] if ones_row else o[LANES:LANES + 1, :]
        o = o[:LANES]
        if has_sink:
            l = l + jnp.exp2(sink - m)
        o = (o / l).T
        idx = 0
        for cc in range(cpg):
            rows = pl.ds(aligned(a * rows_g + cc * CHUNK, CHUNK), CHUNK)
            for r in range(n_qblk):
                top = o[idx * CHUNK:(idx + 1) * CHUNK]
                bot = o[(idx + 1) * CHUNK:(idx + 2) * CHUNK]
                if ones_row:
                    bot = pltpu.roll(bot, HALF, 1)
                o_ref[bi, rows, r * LANES:(r + 1) * LANES] = jnp.where(lo, top, bot).astype(o_ref.dtype)
                idx += 2

    n_total = q_ref.shape[1] // rows_g
    if n_total == 1:
        n_rows = q_ref.shape[0]
        scores(0, 0, 0, 0, 0)
        for bi in range(n_rows):
            if bi + 1 < n_rows:
                scores(0, 0, 0, (bi + 1) & 1, bi + 1)
            finish(0, 0, bi & 1, bi)
        return
    g_blk = rows_g // LANES
    n_lead = n_var - 1 + (n_var - 1) % 2
    assert (n_var - 1) * g_blk >= n_blk - g_blk and n_total - n_lead >= 2 and BAND_PIPE_UNROLL % 2 == 0

    def window_block(a):
        wb = (a + 1) * g_blk - n_blk
        return max(wb, 0) if isinstance(a, int) else wb

    def stage(a, slot):
        nxt = a + 1
        scores(nxt, window_block(nxt), min(nxt, n_var - 1) if isinstance(nxt, int) else n_var - 1, 1 - slot)
        finish(a, window_block(a), slot)

    scores(0, window_block(0), 0, 0)
    for a in range(n_lead):
        stage(a, a & 1)

    def trip(i, carry):
        for st in range(BAND_PIPE_UNROLL):
            stage(n_lead + BAND_PIPE_UNROLL * i + st, st & 1)
        return carry

    n_stages = n_total - 1 - n_lead
    lax.fori_loop(0, n_stages // BAND_PIPE_UNROLL, trip, 0)
    for a in range(n_total - 1 - n_stages % BAND_PIPE_UNROLL, n_total - 1):
        stage(a, (a - n_lead) & 1)
    finish(n_total - 1, window_block(n_total - 1), (n_total - 1 - n_lead) & 1)


def _band_attention(q, keys, values, toep, sink, *, variants, n_blk, n_groups, n_qblk, cpg, ones_row):
    bsz, s, _ = q.shape
    qw = n_qblk * LANES
    buffered = isinstance(keys, tuple)
    nb = bsz if s == cpg * CHUNK else 1

    def per_group(a):
        return pl.BlockSpec((nb, a.shape[1], LANES), lambda g, b: (b, 0, g))

    in_specs = [pl.BlockSpec((nb, s, qw), lambda g, b: (b, 0, g))]
    if buffered:
        args = [q, *keys, *values]
        for buf_t, new in (keys, values):
            in_specs += [pl.BlockSpec((nb, LANES, buf_t.shape[2]), lambda g, b: (b, g, 0)), per_group(new)]
    else:
        args = [q, keys, values]
        in_specs += [per_group(keys), pl.BlockSpec((nb, 1) + values.shape[2:], lambda g, b: (b, g, 0, 0, 0))]
    for extra in (toep, sink):
        if extra is not None:
            in_specs.append(pl.BlockSpec((1,) + extra.shape[1:], lambda g, b: (g, 0, 0)))
            args.append(extra)
    bias_shape = (n_blk * LANES, cpg * 2 * n_qblk * CHUNK)
    body = functools.partial(_band_body, n_blk=n_blk, n_qblk=n_qblk, cpg=cpg, ones_row=ones_row,
                             has_sink=sink is not None, variants=variants, buffered=buffered)
    return pl.pallas_call(
        body,
        grid=(n_groups, bsz // nb),
        in_specs=in_specs,
        out_specs=pl.BlockSpec((nb, s, qw), lambda g, b: (b, 0, g)),
        out_shape=jax.ShapeDtypeStruct(q.shape, BF16),
        scratch_shapes=[pltpu.VMEM((2,) + bias_shape, F32), pltpu.VMEM((len(variants),) + bias_shape, F32)],
        compiler_params=_params(2),
        name="band_attention",
    )(*args)


def _rope_tables(pos, rot, lane_pattern, from_zero=False):
    half = rot // 2
    inv = jnp.power(ROPE_THETA, -jnp.arange(half, dtype=F32) * 2.0 / rot)
    inv_lane, first, second = [], [], []
    for kind, width in lane_pattern:
        if kind == "rot":
            inv_lane += [inv, inv]
            first += [1.0] * half + [0.0] * half
            second += [0.0] * half + [1.0] * half
        else:
            inv_lane.append(jnp.zeros((width,), F32))
            first += [0.0] * width
            second += [0.0] * width
    inv_lane = jnp.concatenate(inv_lane)[None, :]
    n = pos.shape[0]
    if from_zero and n % CHUNK == 0:
        a_hi = (jnp.arange(n // CHUNK, dtype=F32) * CHUNK)[:, None] * inv_lane
        a_lo = jnp.arange(CHUNK, dtype=F32)[:, None] * inv_lane
        c_hi, s_hi = jnp.cos(a_hi)[:, None, :], jnp.sin(a_hi)[:, None, :]
        c_lo, s_lo = jnp.cos(a_lo)[None], jnp.sin(a_lo)[None]
        cos = (c_hi * c_lo - s_hi * s_lo).reshape(n, LANES)
        sin = (s_hi * c_lo + c_hi * s_lo).reshape(n, LANES)
    else:
        ang = pos.astype(F32)[:, None] * inv_lane
        cos, sin = jnp.cos(ang), jnp.sin(ang)
    return jnp.stack([cos, sin * np.asarray(second, np.float32), -sin * np.asarray(first, np.float32)])


A_ROPE_PATTERN = (("pad", A_NOPE), ("rot", A_ROPE), ("pad", LANES - A_QK))
C_ROPE_PATTERN = (("rot", C_ROT), ("pad", HALF - C_ROT)) * 2


def _prep_ab(pre, post, w_in, q_norm, kv_norm, w_uq, w_ukv, rel_bias, w_out):
    d = w_in.shape[0]
    q_lat, c_kv, k_r, g_a, q_b, k_b, v_b, g_b = jnp.split(
        w_in, [384, 640, 672, 1184, 1696, 2208, 2720], axis=1)
    kr_blk = jnp.concatenate([jnp.zeros((d, A_NOPE), F32), k_r, jnp.zeros((d, LANES - A_QK), F32)], axis=1)
    w_in_p = jnp.concatenate([q_lat, c_kv, kr_blk, g_a, g_b, q_b, k_b, v_b], axis=1).astype(BF16)
    w_uq_p = jnp.pad(w_uq.reshape(A_Q_RANK, A_HEADS, A_QK), ((0, 0), (0, 0), (0, LANES - A_QK)))
    w_uq_p = w_uq_p.reshape(A_Q_RANK, A_HEADS * LANES).astype(BF16)
    ukv = w_ukv.reshape(A_KV_RANK, A_HEADS, A_NOPE + A_V)
    w_uk, w_uv = ukv[..., :A_NOPE], ukv[..., A_NOPE:]
    pad_half = ((0, 0), (0, 0), (0, LANES - A_NOPE))
    w_k = jnp.pad(w_uk, pad_half).reshape(A_KV_RANK, A_HEADS * LANES).astype(BF16)
    w_vt = jnp.pad(w_uv, pad_half).reshape(A_KV_RANK, A_HEADS * LANES).T.astype(BF16)
    wkt = jnp.pad(jnp.transpose(w_uk, (1, 2, 0)), ((0, 0), (0, LANES - A_NOPE), (0, 0))).astype(BF16)
    eye = jnp.eye(A_HEADS, dtype=F32)
    wv_s = (jnp.transpose(w_uv, (1, 0, 2))[:, :, None, :] * eye[:, None, :, None]).reshape(
        A_HEADS, A_KV_RANK, A_WIDTH).astype(BF16)
    sel = (jnp.arange(LANES)[:, None] == A_NOPE + jnp.arange(A_ROPE)[None, :]).astype(BF16)
    rel_bias = rel_bias * LOG2E
    win_p = B_WIN_BLOCKS * LANES
    r0 = win_p - CHUNK
    x_len = r0 + win_p
    n_vec = x_len + CHUNK
    n_hi = r0 + CHUNK - 1 - B_MAX_REL
    n_lo = n_vec - n_hi - (2 * B_MAX_REL + 1)
    vec = jnp.concatenate([jnp.broadcast_to(rel_bias[:, -1:], (B_HEADS, n_hi)), rel_bias[:, ::-1],
                           jnp.broadcast_to(rel_bias[:, :1], (B_HEADS, n_lo))], axis=1)
    skew = jnp.tile(vec, (1, CHUNK))[:, :CHUNK * (n_vec - 1)].reshape(B_HEADS, CHUNK, n_vec - 1)
    toep = skew[:, :, CHUNK - 1:CHUNK - 1 + x_len]
    toep = jnp.transpose(toep.reshape(B_HEADS // 2, 2, CHUNK, x_len), (0, 3, 1, 2)).reshape(B_HEADS // 2, x_len, LANES)
    var_p = tuple(tuple((r0 - c * CHUNK, c - B_PAST_CHUNKS, c)
                        for c in range(B_GROUP_CHUNKS * v, B_GROUP_CHUNKS * (v + 1))) for v in range(B_VARIANTS))
    var_s =(((r0 - B_PAST_CHUNKS * CHUNK, 0, B_PAST_CHUNKS),),)
    return dict(pre=pre[None], w_in=w_in_p, q_norm=q_norm[None], kv_norm=kv_norm[None], w_uq=w_uq_p, w_k=w_k,
                w_vt=w_vt, wkt=wkt, wv_s=wv_s, sel=sel, toep=toep, var_p=var_p, var_s=var_s,
                w_out=w_out.astype(BF16), post=post[None])


def _prep_c(pre, post, w_in, sinks, w_out):
    q, k, v, g = jnp.split(w_in, [1024, 1152, 1280], axis=1)
    k0, k1 = k[:, :C_HEAD_DIM], k[:, C_HEAD_DIM:]
    v0, v1 = v[:, :C_HEAD_DIM], v[:, C_HEAD_DIM:]
    zero = jnp.zeros_like(v0)
    w_in_p = jnp.concatenate([q, k0, k0, k1, k1, v0, zero, v1, zero, g], axis=1).astype(BF16)
    var_p = tuple(tuple((0, c - C_PAST_CHUNKS, c) for c in range(C_GROUP_CHUNKS * v, C_GROUP_CHUNKS * (v + 1)))
                  for v in range(C_VARIANTS))
    var_s = (((0, 1, C_PAST_CHUNKS + 1),),)
    sink_row = jnp.repeat((sinks * LOG2E).reshape(C_KV_HEADS, C_GROUP), CHUNK, axis=1)[:, None, :]
    return dict(pre=pre[None], w_in=w_in_p, var_p=var_p, var_s=var_s, sink_s=sink_row,
                sink_p=jnp.tile(sink_row, (1, 1, C_GROUP_CHUNKS)), w_out=w_out.astype(BF16), post=post[None])


def _dup_heads(x):
    return jnp.concatenate([x[:, :, 0], x[:, :, 0], x[:, :, 1], x[:, :, 1]], axis=-1)


def kernel(x_prompt, x_sample, cache_a_ckv, cache_a_krope, cache_b_k, cache_b_v, cache_c_k, cache_c_v,
           ab_pre_norm, ab_post_norm, ab_w_in, ab_q_norm, ab_kv_norm, ab_w_uq, ab_w_ukv, ab_rel_bias, ab_w_out,
           c_pre_norm, c_post_norm, c_w_in, c_sinks, c_w_out):
    bsz, seq, _ = x_prompt.shape
    dbs, dseq, _ = x_sample.shape
    past = cache_a_ckv.shape[2]
    n_s = dbs * dseq
    pos_p = jnp.arange(seq, dtype=jnp.int32)
    pos_s = jnp.tile(past + jnp.arange(dseq, dtype=jnp.int32), dbs)
    wab = _prep_ab(ab_pre_norm[0], ab_post_norm[0], ab_w_in[0], ab_q_norm[0], ab_kv_norm[0], ab_w_uq[0],
                   ab_w_ukv[0], ab_rel_bias[0], ab_w_out[0])
    wc = _prep_c(c_pre_norm[0], c_post_norm[0], c_w_in[0], c_sinks[0], c_w_out[0])
    b_tail = min(B_PAST_CHUNKS * CHUNK, seq)
    c_tail = min(C_WINDOW, seq)
    tile = 512

    rope_a_p = _rope_tables(pos_p, A_ROPE, A_ROPE_PATTERN, from_zero=True)
    (qa, ka, vt, qb, kb, vbt, sg, c_new_p, kr_new_p_t, kb_tail, vb_tail) = _ab_in_proj(
        x_prompt, rope_a_p, wab, tm=tile, tail_len=b_tail, tkv=tile)
    o_a = _mla_prompt(qa, ka, vt, tq=tile)
    o_b = _band_attention(qb, kb, vbt, wab["toep"], None, variants=wab["var_p"], n_blk=B_WIN_BLOCKS,
                          n_groups=B_HEADS // 2, n_qblk=1, cpg=B_GROUP_CHUNKS, ones_row=False)

    rope_a_s = _rope_tables(pos_s, A_ROPE, A_ROPE_PATTERN)
    xs = x_sample.reshape(1, n_s, D_MODEL)
    (qa_s, _, _, qb_s, _, _, sg_s, c_new_s, kr_new_s_t, kb_s32, vb_s32) = _ab_in_proj(
        xs, rope_a_s, wab, tm=n_s, tail_len=n_s, tkv=n_s)
    kr_new_s_t = jnp.transpose(kr_new_s_t.reshape(A_ROPE, dbs, dseq), (1, 0, 2))
    o_a_s = _mla_sample(qa_s.reshape(dbs, dseq, -1), cache_a_ckv[0], jnp.swapaxes(cache_a_krope[0], 1, 2),
                        c_new_s.reshape(dbs, dseq, -1), kr_new_s_t,
                        wab["wkt"], wab["wv_s"], wab["sel"], tk=512)
    wb = cache_b_k.shape[2]

    def buf_t(cache):
        return jnp.transpose(cache[0], (0, 2, 3, 1)).reshape(dbs, B_WIDTH, wb)

    o_b_s = _band_attention(qb_s.reshape(dbs, dseq, -1),
                            (buf_t(cache_b_k), kb_s32.reshape(dbs, dseq, -1)),
                            (buf_t(cache_b_v), vb_s32.reshape(dbs, dseq, -1)),
                            wab["toep"], None, variants=wab["var_s"], n_blk=B_SAMPLE_BLOCKS, n_groups=B_HEADS // 2,
                            n_qblk=1, cpg=1, ones_row=False)

    rope_c_p = _rope_tables(pos_p, C_ROT, C_ROPE_PATTERN, from_zero=True)
    h1_p, qc, kc, vct, sgc, kc_tail, vc_tail = _c_in_proj(
        [o_a, o_b], sg, wab["w_out"], wab["post"], x_prompt, rope_c_p, wc, tm=tile, tail_len=c_tail)
    o_c = _band_attention(qc, kc, vct, None, wc["sink_p"], variants=wc["var_p"], n_blk=C_WIN_BLOCKS,
                          n_groups=C_KV_HEADS, n_qblk=C_GROUP // 2, cpg=C_GROUP_CHUNKS, ones_row=True)
    h2_p = _out_proj([o_c], sgc, wc["w_out"], wc["post"], h1_p, tm=tile)

    rope_c_s = _rope_tables(pos_s, C_ROT, C_ROPE_PATTERN)
    h1_s, qc_s, kc_s, _, sgc_s, kc_s32, vc_s32 = _c_in_proj(
        [o_a_s.reshape(1, n_s, -1), o_b_s.reshape(1, n_s, -1)], sg_s, wab["w_out"], wab["post"], xs, rope_c_s, wc,
        tm=n_s, tail_len=n_s)
    wcw = cache_c_k.shape[2]
    win_c = C_SAMPLE_BLOCKS * LANES
    n_pad = win_c - wcw - dseq
    kcb = jnp.concatenate([jnp.zeros((dbs, n_pad, C_KV_HEADS * LANES), BF16), _dup_heads(cache_c_k[0]).astype(BF16),
                           kc_s.reshape(dbs, dseq, -1)], axis=1)
    vcb = jnp.concatenate([jnp.zeros((dbs, n_pad, C_KV_HEADS, C_HEAD_DIM), F32), cache_c_v[0],
                           vc_s32.reshape(dbs, dseq, C_KV_HEADS, C_HEAD_DIM)], axis=1)
    vcb_t = jnp.concatenate([jnp.transpose(vcb, (0, 2, 3, 1)), jnp.ones((dbs, C_KV_HEADS, 1, win_c), F32),
                             jnp.zeros((dbs, C_KV_HEADS, LANES - C_HEAD_DIM - 1, win_c), F32)], axis=2)
    vcb_t = jnp.transpose(vcb_t.astype(BF16).reshape(dbs, C_KV_HEADS, LANES, C_SAMPLE_BLOCKS, LANES), (0, 1, 3, 2, 4))
    o_c_s = _band_attention(qc_s.reshape(dbs, dseq, -1), kcb, vcb_t, None, wc["sink_s"], variants=wc["var_s"],
                            n_blk=C_SAMPLE_BLOCKS, n_groups=C_KV_HEADS, n_qblk=C_GROUP // 2, cpg=1, ones_row=True)
    h2_s = _out_proj([o_c_s.reshape(1, n_s, -1)], sgc_s, wc["w_out"], wc["post"], h1_s, tm=n_s)

    def roll_in(buf, new):
        return jnp.concatenate([buf, new], axis=1)[:, -buf.shape[1]:][None]

    return (h2_p, h2_s.reshape(dbs, dseq, D_MODEL),
            c_new_p[None], jnp.swapaxes(kr_new_p_t, 1, 2)[None],
            kb_tail.reshape(1, bsz, b_tail, B_HEADS, B_HEAD_DIM), vb_tail.reshape(1, bsz, b_tail, B_HEADS, B_HEAD_DIM),
            kc_tail.reshape(1, bsz, c_tail, C_KV_HEADS, C_HEAD_DIM), vc_tail.reshape(1, bsz, c_tail, C_KV_HEADS, C_HEAD_DIM),
            c_new_s.reshape(1, dbs, dseq, A_KV_RANK), jnp.swapaxes(kr_new_s_t, 1, 2)[None],
            roll_in(cache_b_k[0], kb_s32.reshape(dbs, dseq, B_HEADS, B_HEAD_DIM)),
            roll_in(cache_b_v[0], vb_s32.reshape(dbs, dseq, B_HEADS, B_HEAD_DIM)),
            roll_in(cache_c_k[0], kc_s32.reshape(dbs, dseq, C_KV_HEADS, C_HEAD_DIM)),
            roll_in(cache_c_v[0], vc_s32.reshape(dbs, dseq, C_KV_HEADS, C_HEAD_DIM)))
```

```python
import functools

import jax
import jax.numpy as jnp
import numpy as np
from jax import lax
from jax.experimental import pallas as pl
from jax.experimental.pallas import tpu as pltpu

F32 = jnp.float32
BF16 = jnp.bfloat16

D_MODEL = 1024
CHUNK = 64
ROPE_THETA = 500000.0
RMS_EPS = 1e-6
NEG_INF = -1e30

A_HEADS = 8
A_NOPE = 64
A_ROPE = 32
A_QK = A_NOPE + A_ROPE
A_V = 64
A_Q_RANK = 384
A_KV_RANK = 256
A_WIDTH = A_HEADS * A_V
LOG2E = 1.4426950408889634
A_SCALE = A_QK ** -0.5 * LOG2E

B_HEADS = 8
B_HEAD_DIM = 64
B_WIDTH = B_HEADS * B_HEAD_DIM
B_PAST_CHUNKS = 8
B_MAX_REL = 128
B_SCALE = B_HEAD_DIM ** -0.5 * LOG2E

C_HEADS = 16
C_KV_HEADS = 2
C_GROUP = C_HEADS // C_KV_HEADS
C_HEAD_DIM = 64
C_WIDTH = C_HEADS * C_HEAD_DIM
C_WINDOW = 128
C_PAST_CHUNKS = C_WINDOW // CHUNK
C_ROT = C_HEAD_DIM // 4
C_SCALE = C_HEAD_DIM ** -0.5 * LOG2E

LANES = 128
HALF = LANES // 2
BF16_SUBLANES = 16
VMEM_LIMIT = 56 * 1024 * 1024
A_MASK_CHUNKS = 8
assert A_QK + A_MASK_CHUNKS <= LANES
A_PIPE_UNROLL = 8
BAND_PIPE_UNROLL = 4
B_GROUP_CHUNKS = 4
C_GROUP_CHUNKS = 2
B_WIN_BLOCKS = (B_PAST_CHUNKS + B_GROUP_CHUNKS) * CHUNK // LANES
C_WIN_BLOCKS = (C_PAST_CHUNKS + C_GROUP_CHUNKS) * CHUNK // LANES
B_VARIANTS = B_PAST_CHUNKS // B_GROUP_CHUNKS + 1
C_VARIANTS = C_PAST_CHUNKS // C_GROUP_CHUNKS + 1
B_SAMPLE_BLOCKS = (B_PAST_CHUNKS + 2) * CHUNK // LANES
C_SAMPLE_BLOCKS = (C_PAST_CHUNKS + 2) * CHUNK // LANES

AB_Q0, AB_C0, AB_KR0, AB_G0, AB_QB0, AB_KB0, AB_VB0, AB_NZ = 0, 384, 640, 768, 1792, 2304, 2816, 3328
C_Q0, C_K0, C_V0, C_G0, C_NZ = 0, 1024, 1280, 1536, 2560


def _params(n_axes):
    return pltpu.CompilerParams(dimension_semantics=("arbitrary",) * n_axes, vmem_limit_bytes=VMEM_LIMIT)


def _rms(x, g):
    return x * lax.rsqrt(jnp.mean(x * x, axis=-1, keepdims=True) + RMS_EPS) * g


def _rope_block(blk, rope_ref, shift):
    return (blk * rope_ref[0] + pltpu.roll(blk, shift, 1) * rope_ref[1]
            + pltpu.roll(blk, LANES - shift, 1) * rope_ref[2])


def _tail_spec(n_tiles, tm, tail_len, width):
    tb = min(tail_len, tm)
    n_blk = tail_len // tb
    return tb, pl.BlockSpec((1, tb, width), lambda b, i: (b, jnp.maximum(i - (n_tiles - n_blk), 0), 0))


def _ab_in_body(x_ref, pre_ref, w_ref, qn_ref, kvn_ref, wuq_ref, wk_ref, wvt_ref, rope_ref,
                qa_ref, ka_ref, vt_ref, qb_ref, kb_ref, vbt_ref, sg_ref, c_ref, kr_ref, kb_tail_ref, vb_tail_ref):
    tm = x_ref.shape[1]
    tb = kb_tail_ref.shape[1]
    xn = _rms(x_ref[0], pre_ref[...]).astype(BF16)
    z = jnp.dot(xn, w_ref[...], preferred_element_type=F32)

    qn = _rms(z[:, AB_Q0:AB_C0], qn_ref[...]).astype(BF16)
    qa = jnp.dot(qn, wuq_ref[...], preferred_element_type=F32) * A_SCALE
    for h in range(A_HEADS):
        blk = slice(h * LANES, (h + 1) * LANES)
        qa_ref[0, :, blk] = _rope_block(qa[:, blk], rope_ref, A_ROPE // 2).astype(BF16)

    c_new = _rms(z[:, AB_C0:AB_KR0], kvn_ref[...])
    c_ref[0] = c_new
    cb = c_new.astype(BF16)
    krot = _rope_block(z[:, AB_KR0:AB_G0], rope_ref, A_ROPE // 2)
    kr_ref[0] = krot.T[A_NOPE:A_NOPE + A_ROPE, :]
    kn = jnp.dot(cb, wk_ref[...], preferred_element_type=F32)
    row = pl.program_id(1) * tm + lax.broadcasted_iota(jnp.int32, (tm, LANES), 0)
    lane = lax.broadcasted_iota(jnp.int32, (tm, LANES), 1)
    k_shared = krot + jnp.where(lane - A_QK == (row // CHUNK) % A_MASK_CHUNKS, 1.0, 0.0)
    for h in range(A_HEADS):
        blk = slice(h * LANES, (h + 1) * LANES)
        ka_ref[0, :, blk] = (kn[:, blk] + k_shared).astype(BF16)
    vt = jnp.dot(wvt_ref[...], c_new.T.astype(BF16), preferred_element_type=F32)
    row = lax.broadcasted_iota(jnp.int32, vt.shape, 0)
    vt_ref[0, 0] = jnp.where((row & (LANES - 1)) == A_V, 1.0, vt).astype(BF16)

    g = z[:, AB_G0:AB_QB0]
    sg_ref[0] = (g * jax.nn.sigmoid(g)).astype(BF16)
    qb_ref[0] = (z[:, AB_QB0:AB_KB0] * B_SCALE).astype(BF16)
    kb = z[:, AB_KB0:AB_VB0]
    vb = z[:, AB_VB0:AB_NZ]
    kb_ref[0] = kb.astype(BF16)
    vbt = vb.T.astype(BF16)
    for hp in range(B_HEADS // 2):
        for j in range(tm // LANES):
            vbt_ref[0, hp, j] = vbt[hp * LANES:(hp + 1) * LANES, j * LANES:(j + 1) * LANES]
    kb_tail_ref[0] = kb[tm - tb:, :]
    vb_tail_ref[0] = vb[tm - tb:, :]


def _ab_in_proj(x, rope, w, tm, tail_len, tkv):
    bsz, s, _ = x.shape
    nt = s // tm
    per_kv = tkv // tm
    tb, tail_spec = _tail_spec(nt, tm, tail_len, B_WIDTH)

    def full(a):
        return pl.BlockSpec(a.shape, lambda b, i: (0,) * a.ndim, pipeline_mode=pl.Buffered(1))

    def rows(width):
        return pl.BlockSpec((1, tm, width), lambda b, i: (b, i, 0))

    weights = (w["pre"], w["w_in"], w["q_norm"], w["kv_norm"], w["w_uq"], w["w_k"], w["w_vt"])
    vt_spec = pl.BlockSpec((1, 1, A_HEADS * LANES, tm), lambda b, i: (b, i // per_kv, 0, i % per_kv))
    out_shape = (
        jax.ShapeDtypeStruct((bsz, s, A_HEADS * LANES), BF16),
        jax.ShapeDtypeStruct((bsz, s, A_HEADS * LANES), BF16),
        jax.ShapeDtypeStruct((bsz, s // tkv, A_HEADS * LANES, tkv), BF16),
        jax.ShapeDtypeStruct((bsz, s, B_WIDTH), BF16),
        jax.ShapeDtypeStruct((bsz, s, B_WIDTH), BF16),
        jax.ShapeDtypeStruct((bsz, B_HEADS // 2, s // LANES, LANES, LANES), BF16),
        jax.ShapeDtypeStruct((bsz, s, A_WIDTH + B_WIDTH), BF16),
        jax.ShapeDtypeStruct((bsz, s, A_KV_RANK), F32),
        jax.ShapeDtypeStruct((bsz, A_ROPE, s), F32),
        jax.ShapeDtypeStruct((bsz, tail_len, B_WIDTH), F32),
        jax.ShapeDtypeStruct((bsz, tail_len, B_WIDTH), F32),
    )
    vbt_spec = pl.BlockSpec((1, B_HEADS // 2, tm // LANES, LANES, LANES), lambda b, i: (b, 0, i, 0, 0))
    out_specs = (rows(1024), rows(1024), vt_spec, rows(512), rows(512), vbt_spec, rows(1024),
                 rows(A_KV_RANK), pl.BlockSpec((1, A_ROPE, tm), lambda b, i: (b, 0, i)), tail_spec, tail_spec)
    return pl.pallas_call(
        _ab_in_body,
        grid=(bsz, nt),
        in_specs=[rows(D_MODEL)] + [full(a) for a in weights]
        + [pl.BlockSpec((3, tm, LANES), lambda b, i: (0, i, 0))],
        out_specs=out_specs,
        out_shape=out_shape,
        compiler_params=_params(2),
        name="ab_in_proj",
    )(x, *weights, rope)


def _mix_out(o_refs, sg_ref, w_ref, g_ref, h_ref):
    o = jnp.concatenate([r[0].astype(F32) for r in o_refs], axis=-1) if len(o_refs) > 1 else o_refs[0][0].astype(F32)
    mixed = (o * sg_ref[0].astype(F32)).astype(BF16)
    y = jnp.dot(mixed, w_ref[...], preferred_element_type=F32)
    return h_ref[0] + _rms(y, g_ref[...])


def _c_in_body(*refs):
    n_o = len(refs) - 14
    o_refs = refs[:n_o]
    (sg0_ref, w0_ref, post0_ref, x_ref, pre_ref, w_ref, rope_ref,
     h_ref, q_ref, k_ref, vt_ref, sg_ref, k_tail_ref, v_tail_ref) = refs[n_o:]
    tm = x_ref.shape[1]
    tb = k_tail_ref.shape[1]
    h = _mix_out(o_refs, sg0_ref, w0_ref, post0_ref, x_ref)
    h_ref[0] = h
    xn = _rms(h, pre_ref[...]).astype(BF16)
    z = jnp.dot(xn, w_ref[...], preferred_element_type=F32)
    half_rot = C_ROT // 2
    for j in range(C_WIDTH // LANES):
        blk = slice(C_Q0 + j * LANES, C_Q0 + (j + 1) * LANES)
        q_ref[0, :, j * LANES:(j + 1) * LANES] = (_rope_block(z[:, blk], rope_ref, half_rot) * C_SCALE).astype(BF16)
    kd = [_rope_block(z[:, C_K0 + j * LANES:C_K0 + (j + 1) * LANES], rope_ref, half_rot) for j in range(C_KV_HEADS)]
    vz = [z[:, C_V0 + j * LANES:C_V0 + (j + 1) * LANES] for j in range(C_KV_HEADS)]
    row = lax.broadcasted_iota(jnp.int32, (LANES, tm), 0)
    for j in range(C_KV_HEADS):
        k_ref[0, :, j * LANES:(j + 1) * LANES] = kd[j].astype(BF16)
        vt = jnp.where(row == C_HEAD_DIM, 1.0, vz[j].T).astype(BF16)
        for i in range(tm // LANES):
            vt_ref[0, j, i] = vt[:, i * LANES:(i + 1) * LANES]
    g = z[:, C_G0:C_NZ]
    sg_ref[0] = (g * jax.nn.sigmoid(g)).astype(BF16)
    lo = lax.broadcasted_iota(jnp.int32, (1, LANES), 1) < HALF
    k_tail_ref[0] = jnp.where(lo, kd[0], kd[1])[tm - tb:, :]
    v_tail_ref[0] = (vz[0] + pltpu.roll(vz[1], HALF, 1))[tm - tb:, :]


def _c_in_proj(o_parts, sg0, w_out0, post0, x, rope, w, tm, tail_len):
    bsz, s, _ = x.shape
    nt = s // tm
    tb, tail_spec = _tail_spec(nt, tm, tail_len, LANES)

    def full(a):
        return pl.BlockSpec(a.shape, lambda b, i: (0,) * a.ndim)

    def rows(width):
        return pl.BlockSpec((1, tm, width), lambda b, i: (b, i, 0))

    out_shape = (
        jax.ShapeDtypeStruct((bsz, s, D_MODEL), F32),
        jax.ShapeDtypeStruct((bsz, s, C_WIDTH), BF16),
        jax.ShapeDtypeStruct((bsz, s, C_KV_HEADS * LANES), BF16),
        jax.ShapeDtypeStruct((bsz, C_KV_HEADS, s // LANES, LANES, LANES), BF16),
        jax.ShapeDtypeStruct((bsz, s, C_WIDTH), BF16),
        jax.ShapeDtypeStruct((bsz, tail_len, LANES), F32),
        jax.ShapeDtypeStruct((bsz, tail_len, LANES), F32),
    )
    return pl.pallas_call(
        _c_in_body,
        grid=(bsz, nt),
        in_specs=[rows(o.shape[-1]) for o in o_parts]
        + [rows(D_MODEL), full(w_out0), full(post0), rows(D_MODEL), full(w["pre"]), full(w["w_in"]),
           pl.BlockSpec((3, tm, LANES), lambda b, i: (0, i, 0))],
        out_specs=(rows(D_MODEL), rows(C_WIDTH), rows(256),
                   pl.BlockSpec((1, C_KV_HEADS, tm // LANES, LANES, LANES), lambda b, i: (b, 0, i, 0, 0)),
                   rows(C_WIDTH), tail_spec, tail_spec),
        out_shape=out_shape,
        compiler_params=_params(2),
        name="c_in_proj",
    )(*o_parts, sg0, w_out0, post0, x, w["pre"], w["w_in"], rope)


def _out_body(*refs):
    o_refs, (sg_ref, w_ref, g_ref, h_ref, out_ref) = refs[:-5], refs[-5:]
    out_ref[0] = _mix_out(o_refs, sg_ref, w_ref, g_ref, h_ref)


def _out_proj(o_parts, sg, w_out, post_g, h, tm):
    bsz, s, _ = h.shape

    def rows(width):
        return pl.BlockSpec((1, tm, width), lambda b, i: (b, i, 0))

    def full(a):
        return pl.BlockSpec(a.shape, lambda b, i: (0,) * a.ndim)

    return pl.pallas_call(
        _out_body,
        grid=(bsz, s // tm),
        in_specs=[rows(o.shape[-1]) for o in o_parts] + [rows(D_MODEL), full(w_out), full(post_g), rows(D_MODEL)],
        out_specs=rows(D_MODEL),
        out_shape=jax.ShapeDtypeStruct(h.shape, F32),
        compiler_params=_params(2),
        name="out_proj",
    )(*o_parts, sg, w_out, post_g, h)


def _mla_prompt_body(q_ref, k_ref, vt_ref, qmask_ref, o_ref, m_sc, acc_sc, s_sc, *, tq):
    n_tiles = q_ref.shape[1] // tq
    n_pairs = n_tiles * (n_tiles + 1) // 2
    assert A_PIPE_UNROLL % 2 == 0 and n_pairs % A_PIPE_UNROLL == 0 and n_pairs >= 2 * A_PIPE_UNROLL
    m_sc[...] = jnp.full(m_sc.shape, NEG_INF, F32)
    acc_sc[...] = jnp.zeros(acc_sc.shape, F32)
    contract_last = (((1,), (1,)), ((), ()))

    def scores(qi, j, slot):
        q_start = pl.multiple_of(qi * tq, tq)
        k_start = pl.multiple_of(j * tq, tq)
        qmask = jnp.where(j == qi, qmask_ref[...], jnp.zeros_like(qmask_ref[...]))
        for hh in range(2):
            blk = slice(hh * LANES, (hh + 1) * LANES)
            q = q_ref[0, pl.ds(q_start, tq), blk] + qmask
            s_sc[slot, hh] = lax.dot_general(k_ref[0, pl.ds(k_start, tq), blk], q, contract_last,
                                             preferred_element_type=F32)

    def consume(qi, j, slot):
        for hh in range(2):
            s = s_sc[slot, hh]
            vt = vt_ref[0, j, hh * LANES:(hh + 1) * LANES, :]
            m_old = m_sc[qi, hh]
            m_new = jnp.maximum(m_old, jnp.max(s, axis=0, keepdims=True))
            p = jnp.exp2(s - m_new).astype(BF16)
            acc_sc[qi, hh] = jnp.exp2(m_old - m_new) * acc_sc[qi, hh] + jnp.dot(vt, p, preferred_element_type=F32)
            m_sc[qi, hh] = m_new

    def advance(qi, j):
        last = j == qi
        return jnp.where(last, qi + 1, qi), jnp.where(last, 0, j + 1)

    def stages(cur, count):
        for st in range(count):
            nxt = advance(*cur)
            scores(*nxt, 1 - (st & 1))
            consume(*cur, st & 1)
            cur = nxt
        return cur

    zero = jnp.int32(0)
    scores(zero, zero, 0)
    cur = lax.fori_loop(0, n_pairs // A_PIPE_UNROLL - 1, lambda i, c: stages(c, A_PIPE_UNROLL), (zero, zero))
    cur = stages(cur, A_PIPE_UNROLL - 1)
    consume(*cur, (A_PIPE_UNROLL - 1) & 1)

    lo = lax.broadcasted_iota(jnp.int32, (1, LANES), 1) < HALF

    def write_tile(qi, carry):
        outs = []
        for hh in range(2):
            acc = acc_sc[qi, hh]
            outs.append((acc / acc[A_V:A_V + 1, :]).T)
        o_ref[0, pl.ds(pl.multiple_of(qi * tq, tq), tq), :] = jnp.where(
            lo, outs[0], pltpu.roll(outs[1], HALF, 1)).astype(o_ref.dtype)
        return carry

    lax.fori_loop(0, n_tiles, write_tile, 0)


def _mla_prompt(qa, ka, vt, tq):
    bsz, s, _ = qa.shape
    pairs = A_HEADS // 2
    n_tiles = s // tq
    assert vt.shape[3] == tq and tq == A_MASK_CHUNKS * CHUNK
    q_chunk = np.arange(tq)[:, None] // CHUNK
    lane_chunk = np.arange(LANES)[None, :] - A_QK
    qmask = np.where((lane_chunk > q_chunk) & (lane_chunk < A_MASK_CHUNKS), NEG_INF, 0.0).astype(np.float32)
    return pl.pallas_call(
        functools.partial(_mla_prompt_body, tq=tq),
        grid=(bsz, pairs),
        in_specs=[pl.BlockSpec((1, s, 2 * LANES), lambda b, h: (b, 0, h)),
                  pl.BlockSpec((1, s, 2 * LANES), lambda b, h: (b, 0, h)),
                  pl.BlockSpec((1, n_tiles, 2 * LANES, tq), lambda b, h: (b, 0, h, 0)),
                  pl.BlockSpec((tq, LANES), lambda b, h: (0, 0))],
        out_specs=pl.BlockSpec((1, s, LANES), lambda b, h: (b, 0, h)),
        out_shape=jax.ShapeDtypeStruct((bsz, s, A_WIDTH), BF16),
        scratch_shapes=[pltpu.VMEM((n_tiles, 2, 1, tq), F32), pltpu.VMEM((n_tiles, 2, LANES, tq), F32),
                        pltpu.VMEM((2, 2, tq, tq), F32)],
        compiler_params=_params(2),
        name="mla_prompt",
    )(qa, ka, vt, jnp.asarray(qmask, BF16))


def _mla_sample_body(q_ref, cc_ref, ckr_ref, cn_ref, krn_ref, wkt_ref, wv_ref, sel_ref, o_ref,
                     qabs_sc, qr_sc, m_sc, acc_sc, *, tk):
    t = q_ref.shape[1]
    past = cc_ref.shape[1]
    for h in range(A_HEADS):
        qh = q_ref[0, :, h * LANES:(h + 1) * LANES]
        rows = slice(h * t, (h + 1) * t)
        qabs_sc[rows, :] = jnp.dot(qh, wkt_ref[h], preferred_element_type=F32).astype(BF16)
        qr_sc[rows, :] = jnp.dot(qh, sel_ref[...], preferred_element_type=F32).astype(BF16)
    m_sc[...] = jnp.full(m_sc.shape, NEG_INF, F32)
    acc_sc[...] = jnp.zeros(acc_sc.shape, F32)
    contract_last = (((1,), (1,)), ((), ()))

    def scores(c_t, kr_tt):
        return (lax.dot_general(c_t.astype(BF16), qabs_sc[...], contract_last, preferred_element_type=F32)
                + lax.dot_general(kr_tt.T.astype(BF16), qr_sc[...], contract_last, preferred_element_type=F32))

    def update(s, c_t):
        n_keys = c_t.shape[0]
        m_old = m_sc[...]
        m_new = jnp.maximum(m_old, jnp.max(s, axis=0, keepdims=True))
        p = jnp.exp2(s - m_new).astype(BF16)
        ct = jnp.concatenate([c_t.T.astype(BF16), jnp.ones((BF16_SUBLANES, n_keys), BF16)], axis=0)
        acc_sc[...] = jnp.exp2(m_old - m_new) * acc_sc[...] + jnp.dot(ct, p, preferred_element_type=F32)
        m_sc[...] = m_new

    def tile(j):
        return cc_ref[0, j * tk:(j + 1) * tk, :], ckr_ref[0, :, j * tk:(j + 1) * tk]

    n_tiles = past // tk
    cur = tile(0)
    s = scores(*cur)
    for j in range(n_tiles):
        nxt = tile(j + 1) if j + 1 < n_tiles else (cn_ref[0], krn_ref[0])
        s_next = scores(*nxt)
        update(s, cur[0])
        cur, s = nxt, s_next
    update(s, cur[0])

    acc = acc_sc[...]
    o_lat = (acc[:A_KV_RANK] / acc[A_KV_RANK:A_KV_RANK + 1]).T.astype(BF16)
    out = jnp.zeros((t, A_WIDTH), F32)
    for h in range(A_HEADS):
        out = out + jnp.dot(o_lat[h * t:(h + 1) * t, :], wv_ref[h], preferred_element_type=F32)
    o_ref[0] = out.astype(o_ref.dtype)


def _mla_sample(qa, cache_c, cache_kr, c_new, kr_new, wkt, wv, sel, tk):
    bsz, t, _ = qa.shape
    past = cache_c.shape[1]

    def per_b(shape):
        return pl.BlockSpec((1,) + shape, lambda b: (b, 0, 0))

    def full(a):
        return pl.BlockSpec(a.shape, lambda b: (0,) * a.ndim)

    rows = A_HEADS * t
    return pl.pallas_call(
        functools.partial(_mla_sample_body, tk=tk),
        grid=(bsz,),
        in_specs=[per_b((t, A_HEADS * LANES)), per_b((past, A_KV_RANK)), per_b((A_ROPE, past)),
                  per_b((t, A_KV_RANK)), per_b((A_ROPE, t)), full(wkt), full(wv), full(sel)],
        out_specs=per_b((t, A_WIDTH)),
        out_shape=jax.ShapeDtypeStruct((bsz, t, A_WIDTH), BF16),
        scratch_shapes=[pltpu.VMEM((rows, A_KV_RANK), BF16), pltpu.VMEM((rows, A_ROPE), BF16),
                        pltpu.VMEM((1, rows), F32), pltpu.VMEM((A_KV_RANK + BF16_SUBLANES, rows), F32)],
        compiler_params=_params(1),
        name="mla_sample",
    )(qa, cache_c, cache_kr, c_new, kr_new, wkt, wv, sel)


def _band_body(*refs, n_blk, n_qblk, cpg, ones_row, has_sink, variants, buffered=False):
    refs = list(refs)
    n_in = 5 if buffered else 3
    q_ref = refs[0]
    o_ref, s_sc, bias_sc = refs[-3:]
    extra = refs[n_in:-3]
    sink_ref = extra.pop() if has_sink else None
    toep_ref = extra.pop() if extra else None
    win = n_blk * LANES
    rows_g = cpg * CHUNK
    n_var = len(variants)
    qpc = 2 * n_qblk * CHUNK
    lo = lax.broadcasted_iota(jnp.int32, (CHUNK, LANES), 1) < HALF
    contract_last = (((1,), (1,)), ((), ()))

    if toep_ref is not None:
        key_chunk = lax.broadcasted_iota(jnp.int32, (win, LANES), 0) // CHUNK
        for v, chunks in enumerate(variants):
            for cc, (x0, lo_chunk, hi_chunk) in enumerate(chunks):
                tile = jnp.where((key_chunk >= lo_chunk) & (key_chunk <= hi_chunk), toep_ref[0, x0:x0 + win, :], NEG_INF)
                for j in range(qpc // LANES):
                    bias_sc[v, :, cc * qpc + j * LANES:cc * qpc + (j + 1) * LANES] = tile

    def store_scores(slot, variant, s):
        if toep_ref is not None:
            s_sc[slot] = s + bias_sc[variant]
            return
        s_sc[slot] = s
        for cc, (_, lo_chunk, hi_chunk) in enumerate(variants[variant]):
            cols = slice(cc * qpc, (cc + 1) * qpc)
            for r0, r1 in ((0, lo_chunk * CHUNK), ((hi_chunk + 1) * CHUNK, win)):
                if r1 > max(r0, 0):
                    s_sc[slot, max(r0, 0):r1, cols] = jnp.full((r1 - max(r0, 0), qpc), NEG_INF, F32)

    def aligned(x, n):
        return x if isinstance(x, int) else pl.multiple_of(x, n)

    def buffered_window(buf_ref, new_ref, bi, transposed):
        buf_t, new = buf_ref[bi], new_ref[bi]
        pad = jnp.zeros((win - buf_t.shape[1] - new.shape[0], LANES), F32)
        if transposed:
            return jnp.concatenate([buf_t, jnp.concatenate([new, pad], axis=0).T], axis=1)
        return jnp.concatenate([buf_t.T, new, pad], axis=0)

    def scores(a, wb, variant, slot, bi=0):
        pieces = []
        for cc in range(cpg):
            rows = pl.ds(aligned(a * rows_g + cc * CHUNK, CHUNK), CHUNK)
            for r in range(n_qblk):
                qblk = q_ref[bi, rows, r * LANES:(r + 1) * LANES]
                zero = jnp.zeros_like(qblk)
                pieces += [jnp.where(lo, qblk, zero), jnp.where(lo, zero, qblk)]
        qs = jnp.concatenate(pieces, axis=0)
        if buffered:
            kw = buffered_window(refs[1], refs[2], bi, False).astype(BF16)
        else:
            kw = refs[1][bi, pl.ds(aligned(wb * LANES, LANES), win), :]
        store_scores(slot, variant, lax.dot_general(kw, qs, contract_last, preferred_element_type=F32))

    def finish(a, wb, slot, bi=0):
        s = s_sc[slot]
        m = jnp.max(s, axis=0, keepdims=True)
        if has_sink:
            sink = sink_ref[0]
            m = jnp.maximum(m, sink)
        p = jnp.exp2(s - m)
        if buffered:
            vt = buffered_window(refs[3], refs[4], bi, True).astype(BF16)
        else:
            vt = jnp.concatenate([refs[2][bi, 0, wb + i] for i in range(n_blk)], axis=1)
        if not ones_row:
            vt = jnp.concatenate([vt, jnp.ones((BF16_SUBLANES, win), BF16)], axis=0)
        o = jnp.dot(vt, p.astype(BF16), preferred_element_type=F32)
        l = o[CHUNK:CHUNK + 1, :] if ones_row else o[LANES:LANES + 1, :]
        o = o[:LANES]
        if has_sink:
            l = l + jnp.exp2(sink - m)
        o = (o / l).T
        idx = 0
        for cc in range(cpg):
            rows = pl.ds(aligned(a * rows_g + cc * CHUNK, CHUNK), CHUNK)
            for r in range(n_qblk):
                top = o[idx * CHUNK:(idx + 1) * CHUNK]
                bot = o[(idx + 1) * CHUNK:(idx + 2) * CHUNK]
                if ones_row:
                    bot = pltpu.roll(bot, HALF, 1)
                o_ref[bi, rows, r * LANES:(r + 1) * LANES] = jnp.where(lo, top, bot).astype(o_ref.dtype)
                idx += 2

    n_total = q_ref.shape[1] // rows_g
    if n_total == 1:
        n_rows = q_ref.shape[0]
        scores(0, 0, 0, 0, 0)
        for bi in range(n_rows):
            if bi + 1 < n_rows:
                scores(0, 0, 0, (bi + 1) & 1, bi + 1)
            finish(0, 0, bi & 1, bi)
        return
    g_blk = rows_g // LANES
    n_lead = n_var - 1 + (n_var - 1) % 2
    assert (n_var - 1) * g_blk >= n_blk - g_blk and n_total - n_lead >= 2 and BAND_PIPE_UNROLL % 2 == 0

    def window_block(a):
        wb = (a + 1) * g_blk - n_blk
        return max(wb, 0) if isinstance(a, int) else wb

    def stage(a, slot):
        nxt = a + 1
        scores(nxt, window_block(nxt), min(nxt, n_var - 1) if isinstance(nxt, int) else n_var - 1, 1 - slot)
        finish(a, window_block(a), slot)

    scores(0, window_block(0), 0, 0)
    for a in range(n_lead):
        stage(a, a & 1)

    def trip(i, carry):
        for st in range(BAND_PIPE_UNROLL):
            stage(n_lead + BAND_PIPE_UNROLL * i + st, st & 1)
        return carry

    n_stages = n_total - 1 - n_lead
    lax.fori_loop(0, n_stages // BAND_PIPE_UNROLL, trip, 0)
    for a in range(n_total - 1 - n_stages % BAND_PIPE_UNROLL, n_total - 1):
        stage(a, (a - n_lead) & 1)
    finish(n_total - 1, window_block(n_total - 1), (n_total - 1 - n_lead) & 1)


def _band_attention(q, keys, values, toep, sink, *, variants, n_blk, n_groups, n_qblk, cpg, ones_row):
    bsz, s, _ = q.shape
    qw = n_qblk * LANES
    buffered = isinstance(keys, tuple)
    nb = bsz if s == cpg * CHUNK else 1

    def per_group(a):
        return pl.BlockSpec((nb, a.shape[1], LANES), lambda g, b: (b, 0, g))

    in_specs = [pl.BlockSpec((nb, s, qw), lambda g, b: (b, 0, g))]
    if buffered:
        args = [q, *keys, *values]
        for buf_t, new in (keys, values):
            in_specs += [pl.BlockSpec((nb, LANES, buf_t.shape[2]), lambda g, b: (b, g, 0)), per_group(new)]
    else:
        args = [q, keys, values]
        in_specs += [per_group(keys), pl.BlockSpec((nb, 1) + values.shape[2:], lambda g, b: (b, g, 0, 0, 0))]
    for extra in (toep, sink):
        if extra is not None:
            in_specs.append(pl.BlockSpec((1,) + extra.shape[1:], lambda g, b: (g, 0, 0)))
            args.append(extra)
    bias_shape = (n_blk * LANES, cpg * 2 * n_qblk * CHUNK)
    body = functools.partial(_band_body, n_blk=n_blk, n_qblk=n_qblk, cpg=cpg, ones_row=ones_row,
                             has_sink=sink is not None, variants=variants, buffered=buffered)
    return pl.pallas_call(
        body,
        grid=(n_groups, bsz // nb),
        in_specs=in_specs,
        out_specs=pl.BlockSpec((nb, s, qw), lambda g, b: (b, 0, g)),
        out_shape=jax.ShapeDtypeStruct(q.shape, BF16),
        scratch_shapes=[pltpu.VMEM((2,) + bias_shape, F32), pltpu.VMEM((len(variants),) + bias_shape, F32)],
        compiler_params=_params(2),
        name="band_attention",
    )(*args)


def _rope_tables(pos, rot, lane_pattern, from_zero=False):
    half = rot // 2
    inv = jnp.power(ROPE_THETA, -jnp.arange(half, dtype=F32) * 2.0 / rot)
    inv_lane, first, second = [], [], []
    for kind, width in lane_pattern:
        if kind == "rot":
            inv_lane += [inv, inv]
            first += [1.0] * half + [0.0] * half
            second += [0.0] * half + [1.0] * half
        else:
            inv_lane.append(jnp.zeros((width,), F32))
            first += [0.0] * width
            second += [0.0] * width
    inv_lane = jnp.concatenate(inv_lane)[None, :]
    n = pos.shape[0]
    if from_zero and n % CHUNK == 0:
        a_hi = (jnp.arange(n // CHUNK, dtype=F32) * CHUNK)[:, None] * inv_lane
        a_lo = jnp.arange(CHUNK, dtype=F32)[:, None] * inv_lane
        c_hi, s_hi = jnp.cos(a_hi)[:, None, :], jnp.sin(a_hi)[:, None, :]
        c_lo, s_lo = jnp.cos(a_lo)[None], jnp.sin(a_lo)[None]
        cos = (c_hi * c_lo - s_hi * s_lo).reshape(n, LANES)
        sin = (s_hi * c_lo + c_hi * s_lo).reshape(n, LANES)
    else:
        ang = pos.astype(F32)[:, None] * inv_lane
        cos, sin = jnp.cos(ang), jnp.sin(ang)
    return jnp.stack([cos, sin * np.asarray(second, np.float32), -sin * np.asarray(first, np.float32)])


A_ROPE_PATTERN = (("pad", A_NOPE), ("rot", A_ROPE), ("pad", LANES - A_QK))
C_ROPE_PATTERN = (("rot", C_ROT), ("pad", HALF - C_ROT)) * 2


def _prep_ab(pre, post, w_in, q_norm, kv_norm, w_uq, w_ukv, rel_bias, w_out):
    d = w_in.shape[0]
    q_lat, c_kv, k_r, g_a, q_b, k_b, v_b, g_b = jnp.split(
        w_in, [384, 640, 672, 1184, 1696, 2208, 2720], axis=1)
    kr_blk = jnp.concatenate([jnp.zeros((d, A_NOPE), F32), k_r, jnp.zeros((d, LANES - A_QK), F32)], axis=1)
    w_in_p = jnp.concatenate([q_lat, c_kv, kr_blk, g_a, g_b, q_b, k_b, v_b], axis=1).astype(BF16)
    w_uq_p = jnp.pad(w_uq.reshape(A_Q_RANK, A_HEADS, A_QK), ((0, 0), (0, 0), (0, LANES - A_QK)))
    w_uq_p = w_uq_p.reshape(A_Q_RANK, A_HEADS * LANES).astype(BF16)
    ukv = w_ukv.reshape(A_KV_RANK, A_HEADS, A_NOPE + A_V)
    w_uk, w_uv = ukv[..., :A_NOPE], ukv[..., A_NOPE:]
    pad_half = ((0, 0), (0, 0), (0, LANES - A_NOPE))
    w_k = jnp.pad(w_uk, pad_half).reshape(A_KV_RANK, A_HEADS * LANES).astype(BF16)
    w_vt = jnp.pad(w_uv, pad_half).reshape(A_KV_RANK, A_HEADS * LANES).T.astype(BF16)
    wkt = jnp.pad(jnp.transpose(w_uk, (1, 2, 0)), ((0, 0), (0, LANES - A_NOPE), (0, 0))).astype(BF16)
    eye = jnp.eye(A_HEADS, dtype=F32)
    wv_s = (jnp.transpose(w_uv, (1, 0, 2))[:, :, None, :] * eye[:, None, :, None]).reshape(
        A_HEADS, A_KV_RANK, A_WIDTH).astype(BF16)
    sel = (jnp.arange(LANES)[:, None] == A_NOPE + jnp.arange(A_ROPE)[None, :]).astype(BF16)
    rel_bias = rel_bias * LOG2E
    win_p = B_WIN_BLOCKS * LANES
    r0 = win_p - CHUNK
    x_len = r0 + win_p
    n_vec = x_len + CHUNK
    n_hi = r0 + CHUNK - 1 - B_MAX_REL
    n_lo = n_vec - n_hi - (2 * B_MAX_REL + 1)
    vec = jnp.concatenate([jnp.broadcast_to(rel_bias[:, -1:], (B_HEADS, n_hi)), rel_bias[:, ::-1],
                           jnp.broadcast_to(rel_bias[:, :1], (B_HEADS, n_lo))], axis=1)
    skew = jnp.tile(vec, (1, CHUNK))[:, :CHUNK * (n_vec - 1)].reshape(B_HEADS, CHUNK, n_vec - 1)
    toep = skew[:, :, CHUNK - 1:CHUNK - 1 + x_len]
    toep = jnp.transpose(toep.reshape(B_HEADS // 2, 2, CHUNK, x_len), (0, 3, 1, 2)).reshape(B_HEADS // 2, x_len, LANES)
    var_p = tuple(tuple((r0 - c * CHUNK, c - B_PAST_CHUNKS, c)
                        for c in range(B_GROUP_CHUNKS * v, B_GROUP_CHUNKS * (v + 1))) for v in range(B_VARIANTS))
    var_s =(((r0 - B_PAST_CHUNKS * CHUNK, 0, B_PAST_CHUNKS),),)
    return dict(pre=pre[None], w_in=w_in_p, q_norm=q_norm[None], kv_norm=kv_norm[None], w_uq=w_uq_p, w_k=w_k,
                w_vt=w_vt, wkt=wkt, wv_s=wv_s, sel=sel, toep=toep, var_p=var_p, var_s=var_s,
                w_out=w_out.astype(BF16), post=post[None])


def _prep_c(pre, post, w_in, sinks, w_out):
    q, k, v, g = jnp.split(w_in, [1024, 1152, 1280], axis=1)
    k0, k1 = k[:, :C_HEAD_DIM], k[:, C_HEAD_DIM:]
    v0, v1 = v[:, :C_HEAD_DIM], v[:, C_HEAD_DIM:]
    zero = jnp.zeros_like(v0)
    w_in_p = jnp.concatenate([q, k0, k0, k1, k1, v0, zero, v1, zero, g], axis=1).astype(BF16)
    var_p = tuple(tuple((0, c - C_PAST_CHUNKS, c) for c in range(C_GROUP_CHUNKS * v, C_GROUP_CHUNKS * (v + 1)))
                  for v in range(C_VARIANTS))
    var_s = (((0, 1, C_PAST_CHUNKS + 1),),)
    sink_row = jnp.repeat((sinks * LOG2E).reshape(C_KV_HEADS, C_GROUP), CHUNK, axis=1)[:, None, :]
    return dict(pre=pre[None], w_in=w_in_p, var_p=var_p, var_s=var_s, sink_s=sink_row,
                sink_p=jnp.tile(sink_row, (1, 1, C_GROUP_CHUNKS)), w_out=w_out.astype(BF16), post=post[None])


def _dup_heads(x):
    return jnp.concatenate([x[:, :, 0], x[:, :, 0], x[:, :, 1], x[:, :, 1]], axis=-1)


def kernel(x_prompt, x_sample, cache_a_ckv, cache_a_krope, cache_b_k, cache_b_v, cache_c_k, cache_c_v,
           ab_pre_norm, ab_post_norm, ab_w_in, ab_q_norm, ab_kv_norm, ab_w_uq, ab_w_ukv, ab_rel_bias, ab_w_out,
           c_pre_norm, c_post_norm, c_w_in, c_sinks, c_w_out):
    bsz, seq, _ = x_prompt.shape
    dbs, dseq, _ = x_sample.shape
    past = cache_a_ckv.shape[2]
    n_s = dbs * dseq
    pos_p = jnp.arange(seq, dtype=jnp.int32)
    pos_s = jnp.tile(past + jnp.arange(dseq, dtype=jnp.int32), dbs)
    wab = _prep_ab(ab_pre_norm[0], ab_post_norm[0], ab_w_in[0], ab_q_norm[0], ab_kv_norm[0], ab_w_uq[0],
                   ab_w_ukv[0], ab_rel_bias[0], ab_w_out[0])
    wc = _prep_c(c_pre_norm[0], c_post_norm[0], c_w_in[0], c_sinks[0], c_w_out[0])
    b_tail = min(B_PAST_CHUNKS * CHUNK, seq)
    c_tail = min(C_WINDOW, seq)
    tile = 512

    rope_a_p = _rope_tables(pos_p, A_ROPE, A_ROPE_PATTERN, from_zero=True)
    (qa, ka, vt, qb, kb, vbt, sg, c_new_p, kr_new_p_t, kb_tail, vb_tail) = _ab_in_proj(
        x_prompt, rope_a_p, wab, tm=tile, tail_len=b_tail, tkv=tile)
    o_a = _mla_prompt(qa, ka, vt, tq=tile)
    o_b = _band_attention(qb, kb, vbt, wab["toep"], None, variants=wab["var_p"], n_blk=B_WIN_BLOCKS,
                          n_groups=B_HEADS // 2, n_qblk=1, cpg=B_GROUP_CHUNKS, ones_row=False)

    rope_a_s = _rope_tables(pos_s, A_ROPE, A_ROPE_PATTERN)
    xs = x_sample.reshape(1, n_s, D_MODEL)
    (qa_s, _, _, qb_s, _, _, sg_s, c_new_s, kr_new_s_t, kb_s32, vb_s32) = _ab_in_proj(
        xs, rope_a_s, wab, tm=n_s, tail_len=n_s, tkv=n_s)
    kr_new_s_t = jnp.transpose(kr_new_s_t.reshape(A_ROPE, dbs, dseq), (1, 0, 2))
    o_a_s = _mla_sample(qa_s.reshape(dbs, dseq, -1), cache_a_ckv[0], jnp.swapaxes(cache_a_krope[0], 1, 2),
                        c_new_s.reshape(dbs, dseq, -1), kr_new_s_t,
                        wab["wkt"], wab["wv_s"], wab["sel"], tk=512)
    wb = cache_b_k.shape[2]

    def buf_t(cache):
        return jnp.transpose(cache[0], (0, 2, 3, 1)).reshape(dbs, B_WIDTH, wb)

    o_b_s = _band_attention(qb_s.reshape(dbs, dseq, -1),
                            (buf_t(cache_b_k), kb_s32.reshape(dbs, dseq, -1)),
                            (buf_t(cache_b_v), vb_s32.reshape(dbs, dseq, -1)),
                            wab["toep"], None, variants=wab["var_s"], n_blk=B_SAMPLE_BLOCKS, n_groups=B_HEADS // 2,
                            n_qblk=1, cpg=1, ones_row=False)

    rope_c_p = _rope_tables(pos_p, C_ROT, C_ROPE_PATTERN, from_zero=True)
    h1_p, qc, kc, vct, sgc, kc_tail, vc_tail = _c_in_proj(
        [o_a, o_b], sg, wab["w_out"], wab["post"], x_prompt, rope_c_p, wc, tm=tile, tail_len=c_tail)
    o_c = _band_attention(qc, kc, vct, None, wc["sink_p"], variants=wc["var_p"], n_blk=C_WIN_BLOCKS,
                          n_groups=C_KV_HEADS, n_qblk=C_GROUP // 2, cpg=C_GROUP_CHUNKS, ones_row=True)
    h2_p = _out_proj([o_c], sgc, wc["w_out"], wc["post"], h1_p, tm=tile)

    rope_c_s = _rope_tables(pos_s, C_ROT, C_ROPE_PATTERN)
    h1_s, qc_s, kc_s, _, sgc_s, kc_s32, vc_s32 = _c_in_proj(
        [o_a_s.reshape(1, n_s, -1), o_b_s.reshape(1, n_s, -1)], sg_s, wab["w_out"], wab["post"], xs, rope_c_s, wc,
        tm=n_s, tail_len=n_s)
    wcw = cache_c_k.shape[2]
    win_c = C_SAMPLE_BLOCKS * LANES
    n_pad = win_c - wcw - dseq
    kcb = jnp.concatenate([jnp.zeros((dbs, n_pad, C_KV_HEADS * LANES), BF16), _dup_heads(cache_c_k[0]).astype(BF16),
                           kc_s.reshape(dbs, dseq, -1)], axis=1)
    vcb = jnp.concatenate([jnp.zeros((dbs, n_pad, C_KV_HEADS, C_HEAD_DIM), F32), cache_c_v[0],
                           vc_s32.reshape(dbs, dseq, C_KV_HEADS, C_HEAD_DIM)], axis=1)
    vcb_t = jnp.concatenate([jnp.transpose(vcb, (0, 2, 3, 1)), jnp.ones((dbs, C_KV_HEADS, 1, win_c), F32),
                             jnp.zeros((dbs, C_KV_HEADS, LANES - C_HEAD_DIM - 1, win_c), F32)], axis=2)
    vcb_t = jnp.transpose(vcb_t.astype(BF16).reshape(dbs, C_KV_HEADS, LANES, C_SAMPLE_BLOCKS, LANES), (0, 1, 3, 2, 4))
    o_c_s = _band_attention(qc_s.reshape(dbs, dseq, -1), kcb, vcb_t, None, wc["sink_s"], variants=wc["var_s"],
                            n_blk=C_SAMPLE_BLOCKS, n_groups=C_KV_HEADS, n_qblk=C_GROUP // 2, cpg=1, ones_row=True)
    h2_s = _out_proj([o_c_s.reshape(1, n_s, -1)], sgc_s, wc["w_out"], wc["post"], h1_s, tm=n_s)

    def roll_in(buf, new):
        return jnp.concatenate([buf, new], axis=1)[:, -buf.shape[1]:][None]

    return (h2_p, h2_s.reshape(dbs, dseq, D_MODEL),
            c_new_p[None], jnp.swapaxes(kr_new_p_t, 1, 2)[None],
            kb_tail.reshape(1, bsz, b_tail, B_HEADS, B_HEAD_DIM), vb_tail.reshape(1, bsz, b_tail, B_HEADS, B_HEAD_DIM),
            kc_tail.reshape(1, bsz, c_tail, C_KV_HEADS, C_HEAD_DIM), vc_tail.reshape(1, bsz, c_tail, C_KV_HEADS, C_HEAD_DIM),
            c_new_s.reshape(1, dbs, dseq, A_KV_RANK), jnp.swapaxes(kr_new_s_t, 1, 2)[None],
            roll_in(cache_b_k[0], kb_s32.reshape(dbs, dseq, B_HEADS, B_HEAD_DIM)),
            roll_in(cache_b_v[0], vb_s32.reshape(dbs, dseq, B_HEADS, B_HEAD_DIM)),
            roll_in(cache_c_k[0], kc_s32.reshape(dbs, dseq, C_KV_HEADS, C_HEAD_DIM)),
            roll_in(cache_c_v[0], vc_s32.reshape(dbs, dseq, C_KV_HEADS, C_HEAD_DIM)))
```

```python
import functools

import jax
import jax.numpy as jnp
import numpy as np
from jax import lax
from jax.experimental import pallas as pl
from jax.experimental.pallas import tpu as pltpu

F32 = jnp.float32
BF16 = jnp.bfloat16

D_MODEL = 1024
CHUNK = 64
ROPE_THETA = 500000.0
RMS_EPS = 1e-6
NEG_INF = -1e30

A_HEADS = 8
A_NOPE = 64
A_ROPE = 32
A_QK = A_NOPE + A_ROPE
A_V = 64
A_Q_RANK = 384
A_KV_RANK = 256
A_WIDTH = A_HEADS * A_V
LOG2E = 1.4426950408889634
A_SCALE = A_QK ** -0.5 * LOG2E

B_HEADS = 8
B_HEAD_DIM = 64
B_WIDTH = B_HEADS * B_HEAD_DIM
B_PAST_CHUNKS = 8
B_MAX_REL = 128
B_SCALE = B_HEAD_DIM ** -0.5 * LOG2E

C_HEADS = 16
C_KV_HEADS = 2
C_GROUP = C_HEADS // C_KV_HEADS
C_HEAD_DIM = 64
C_WIDTH = C_HEADS * C_HEAD_DIM
C_WINDOW = 128
C_PAST_CHUNKS = C_WINDOW // CHUNK
C_ROT = C_HEAD_DIM // 4
C_SCALE = C_HEAD_DIM ** -0.5 * LOG2E

LANES = 128
HALF = LANES // 2
BF16_SUBLANES = 16
VMEM_LIMIT = 56 * 1024 * 1024
A_MASK_CHUNKS = 8
assert A_QK + A_MASK_CHUNKS <= LANES
A_PIPE_UNROLL = 8
BAND_PIPE_UNROLL = 8
B_GROUP_CHUNKS = 4
C_GROUP_CHUNKS = 2
B_WIN_BLOCKS = (B_PAST_CHUNKS + B_GROUP_CHUNKS) * CHUNK // LANES
C_WIN_BLOCKS = (C_PAST_CHUNKS + C_GROUP_CHUNKS) * CHUNK // LANES
B_VARIANTS = B_PAST_CHUNKS // B_GROUP_CHUNKS + 1
C_VARIANTS = C_PAST_CHUNKS // C_GROUP_CHUNKS + 1
B_SAMPLE_BLOCKS = (B_PAST_CHUNKS + 2) * CHUNK // LANES
C_SAMPLE_BLOCKS = (C_PAST_CHUNKS + 2) * CHUNK // LANES

AB_Q0, AB_C0, AB_KR0, AB_G0, AB_QB0, AB_KB0, AB_VB0, AB_NZ = 0, 384, 640, 768, 1792, 2304, 2816, 3328
C_Q0, C_K0, C_V0, C_G0, C_NZ = 0, 1024, 1280, 1536, 2560


def _params(n_axes):
    return pltpu.CompilerParams(dimension_semantics=("arbitrary",) * n_axes, vmem_limit_bytes=VMEM_LIMIT)


def _rms(x, g):
    return x * lax.rsqrt(jnp.mean(x * x, axis=-1, keepdims=True) + RMS_EPS) * g


def _rope_block(blk, rope_ref, shift):
    return (blk * rope_ref[0] + pltpu.roll(blk, shift, 1) * rope_ref[1]
            + pltpu.roll(blk, LANES - shift, 1) * rope_ref[2])


def _tail_spec(n_tiles, tm, tail_len, width):
    tb = min(tail_len, tm)
    n_blk = tail_len // tb
    return tb, pl.BlockSpec((1, tb, width), lambda b, i: (b, jnp.maximum(i - (n_tiles - n_blk), 0), 0))


def _ab_in_body(x_ref, pre_ref, w_ref, qn_ref, kvn_ref, wuq_ref, wk_ref, wvt_ref, rope_ref,
                qa_ref, ka_ref, vt_ref, qb_ref, kb_ref, vbt_ref, sg_ref, c_ref, kr_ref, kb_tail_ref, vb_tail_ref):
    tm = x_ref.shape[1]
    tb = kb_tail_ref.shape[1]
    xn = _rms(x_ref[0], pre_ref[...]).astype(BF16)
    z = jnp.dot(xn, w_ref[...], preferred_element_type=F32)

    qn = _rms(z[:, AB_Q0:AB_C0], qn_ref[...]).astype(BF16)
    qa = jnp.dot(qn, wuq_ref[...], preferred_element_type=F32) * A_SCALE
    for h in range(A_HEADS):
        blk = slice(h * LANES, (h + 1) * LANES)
        qa_ref[0, :, blk] = _rope_block(qa[:, blk], rope_ref, A_ROPE // 2).astype(BF16)

    c_new = _rms(z[:, AB_C0:AB_KR0], kvn_ref[...])
    c_ref[0] = c_new
    cb = c_new.astype(BF16)
    krot = _rope_block(z[:, AB_KR0:AB_G0], rope_ref, A_ROPE // 2)
    kr_ref[0] = krot.T[A_NOPE:A_NOPE + A_ROPE, :]
    kn = jnp.dot(cb, wk_ref[...], preferred_element_type=F32)
    row = pl.program_id(1) * tm + lax.broadcasted_iota(jnp.int32, (tm, LANES), 0)
    lane = lax.broadcasted_iota(jnp.int32, (tm, LANES), 1)
    k_shared = krot + jnp.where(lane - A_QK == (row // CHUNK) % A_MASK_CHUNKS, 1.0, 0.0)
    for h in range(A_HEADS):
        blk = slice(h * LANES, (h + 1) * LANES)
        ka_ref[0, :, blk] = (kn[:, blk] + k_shared).astype(BF16)
    vt = jnp.dot(wvt_ref[...], c_new.T.astype(BF16), preferred_element_type=F32)
    row = lax.broadcasted_iota(jnp.int32, vt.shape, 0)
    vt_ref[0, 0] = jnp.where((row & (LANES - 1)) == A_V, 1.0, vt).astype(BF16)

    g = z[:, AB_G0:AB_QB0]
    sg_ref[0] = (g * jax.nn.sigmoid(g)).astype(BF16)
    qb_ref[0] = (z[:, AB_QB0:AB_KB0] * B_SCALE).astype(BF16)
    kb = z[:, AB_KB0:AB_VB0]
    vb = z[:, AB_VB0:AB_NZ]
    kb_ref[0] = kb.astype(BF16)
    vbt = vb.T.astype(BF16)
    for hp in range(B_HEADS // 2):
        for j in range(tm // LANES):
            vbt_ref[0, hp, j] = vbt[hp * LANES:(hp + 1) * LANES, j * LANES:(j + 1) * LANES]
    kb_tail_ref[0] = kb[tm - tb:, :]
    vb_tail_ref[0] = vb[tm - tb:, :]


def _ab_in_proj(x, rope, w, tm, tail_len, tkv):
    bsz, s, _ = x.shape
    nt = s // tm
    per_kv = tkv // tm
    tb, tail_spec = _tail_spec(nt, tm, tail_len, B_WIDTH)

    def full(a):
        return pl.BlockSpec(a.shape, lambda b, i: (0,) * a.ndim, pipeline_mode=pl.Buffered(1))

    def rows(width):
        return pl.BlockSpec((1, tm, width), lambda b, i: (b, i, 0))

    weights = (w["pre"], w["w_in"], w["q_norm"], w["kv_norm"], w["w_uq"], w["w_k"], w["w_vt"])
    vt_spec = pl.BlockSpec((1, 1, A_HEADS * LANES, tm), lambda b, i: (b, i // per_kv, 0, i % per_kv))
    out_shape = (
        jax.ShapeDtypeStruct((bsz, s, A_HEADS * LANES), BF16),
        jax.ShapeDtypeStruct((bsz, s, A_HEADS * LANES), BF16),
        jax.ShapeDtypeStruct((bsz, s // tkv, A_HEADS * LANES, tkv), BF16),
        jax.ShapeDtypeStruct((bsz, s, B_WIDTH), BF16),
        jax.ShapeDtypeStruct((bsz, s, B_WIDTH), BF16),
        jax.ShapeDtypeStruct((bsz, B_HEADS // 2, s // LANES, LANES, LANES), BF16),
        jax.ShapeDtypeStruct((bsz, s, A_WIDTH + B_WIDTH), BF16),
        jax.ShapeDtypeStruct((bsz, s, A_KV_RANK), F32),
        jax.ShapeDtypeStruct((bsz, A_ROPE, s), F32),
        jax.ShapeDtypeStruct((bsz, tail_len, B_WIDTH), F32),
        jax.ShapeDtypeStruct((bsz, tail_len, B_WIDTH), F32),
    )
    vbt_spec = pl.BlockSpec((1, B_HEADS // 2, tm // LANES, LANES, LANES), lambda b, i: (b, 0, i, 0, 0))
    out_specs = (rows(1024), rows(1024), vt_spec, rows(512), rows(512), vbt_spec, rows(1024),
                 rows(A_KV_RANK), pl.BlockSpec((1, A_ROPE, tm), lambda b, i: (b, 0, i)), tail_spec, tail_spec)
    return pl.pallas_call(
        _ab_in_body,
        grid=(bsz, nt),
        in_specs=[rows(D_MODEL)] + [full(a) for a in weights]
        + [pl.BlockSpec((3, tm, LANES), lambda b, i: (0, i, 0))],
        out_specs=out_specs,
        out_shape=out_shape,
        compiler_params=_params(2),
        name="ab_in_proj",
    )(x, *weights, rope)


def _mix_out(o_refs, sg_ref, w_ref, g_ref, h_ref):
    o = jnp.concatenate([r[0].astype(F32) for r in o_refs], axis=-1) if len(o_refs) > 1 else o_refs[0][0].astype(F32)
    mixed = (o * sg_ref[0].astype(F32)).astype(BF16)
    y = jnp.dot(mixed, w_ref[...], preferred_element_type=F32)
    return h_ref[0] + _rms(y, g_ref[...])


def _c_in_body(*refs):
    n_o = len(refs) - 14
    o_refs = refs[:n_o]
    (sg0_ref, w0_ref, post0_ref, x_ref, pre_ref, w_ref, rope_ref,
     h_ref, q_ref, k_ref, vt_ref, sg_ref, k_tail_ref, v_tail_ref) = refs[n_o:]
    tm = x_ref.shape[1]
    tb = k_tail_ref.shape[1]
    h = _mix_out(o_refs, sg0_ref, w0_ref, post0_ref, x_ref)
    h_ref[0] = h
    xn = _rms(h, pre_ref[...]).astype(BF16)
    z = jnp.dot(xn, w_ref[...], preferred_element_type=F32)
    half_rot = C_ROT // 2
    for j in range(C_WIDTH // LANES):
        blk = slice(C_Q0 + j * LANES, C_Q0 + (j + 1) * LANES)
        q_ref[0, :, j * LANES:(j + 1) * LANES] = (_rope_block(z[:, blk], rope_ref, half_rot) * C_SCALE).astype(BF16)
    kd = [_rope_block(z[:, C_K0 + j * LANES:C_K0 + (j + 1) * LANES], rope_ref, half_rot) for j in range(C_KV_HEADS)]
    vz = [z[:, C_V0 + j * LANES:C_V0 + (j + 1) * LANES] for j in range(C_KV_HEADS)]
    row = lax.broadcasted_iota(jnp.int32, (LANES, tm), 0)
    for j in range(C_KV_HEADS):
        k_ref[0, :, j * LANES:(j + 1) * LANES] = kd[j].astype(BF16)
        vt = jnp.where(row == C_HEAD_DIM, 1.0, vz[j].T).astype(BF16)
        for i in range(tm // LANES):
            vt_ref[0, j, i] = vt[:, i * LANES:(i + 1) * LANES]
    g = z[:, C_G0:C_NZ]
    sg_ref[0] = (g * jax.nn.sigmoid(g)).astype(BF16)
    lo = lax.broadcasted_iota(jnp.int32, (1, LANES), 1) < HALF
    k_tail_ref[0] = jnp.where(lo, kd[0], kd[1])[tm - tb:, :]
    v_tail_ref[0] = (vz[0] + pltpu.roll(vz[1], HALF, 1))[tm - tb:, :]


def _c_in_proj(o_parts, sg0, w_out0, post0, x, rope, w, tm, tail_len):
    bsz, s, _ = x.shape
    nt = s // tm
    tb, tail_spec = _tail_spec(nt, tm, tail_len, LANES)

    def full(a):
        return pl.BlockSpec(a.shape, lambda b, i: (0,) * a.ndim)

    def rows(width):
        return pl.BlockSpec((1, tm, width), lambda b, i: (b, i, 0))

    out_shape = (
        jax.ShapeDtypeStruct((bsz, s, D_MODEL), F32),
        jax.ShapeDtypeStruct((bsz, s, C_WIDTH), BF16),
        jax.ShapeDtypeStruct((bsz, s, C_KV_HEADS * LANES), BF16),
        jax.ShapeDtypeStruct((bsz, C_KV_HEADS, s // LANES, LANES, LANES), BF16),
        jax.ShapeDtypeStruct((bsz, s, C_WIDTH), BF16),
        jax.ShapeDtypeStruct((bsz, tail_len, LANES), F32),
        jax.ShapeDtypeStruct((bsz, tail_len, LANES), F32),
    )
    return pl.pallas_call(
        _c_in_body,
        grid=(bsz, nt),
        in_specs=[rows(o.shape[-1]) for o in o_parts]
        + [rows(D_MODEL), full(w_out0), full(post0), rows(D_MODEL), full(w["pre"]), full(w["w_in"]),
           pl.BlockSpec((3, tm, LANES), lambda b, i: (0, i, 0))],
        out_specs=(rows(D_MODEL), rows(C_WIDTH), rows(256),
                   pl.BlockSpec((1, C_KV_HEADS, tm // LANES, LANES, LANES), lambda b, i: (b, 0, i, 0, 0)),
                   rows(C_WIDTH), tail_spec, tail_spec),
        out_shape=out_shape,
        compiler_params=_params(2),
        name="c_in_proj",
    )(*o_parts, sg0, w_out0, post0, x, w["pre"], w["w_in"], rope)


def _out_body(*refs):
    o_refs, (sg_ref, w_ref, g_ref, h_ref, out_ref) = refs[:-5], refs[-5:]
    out_ref[0] = _mix_out(o_refs, sg_ref, w_ref, g_ref, h_ref)


def _out_proj(o_parts, sg, w_out, post_g, h, tm):
    bsz, s, _ = h.shape

    def rows(width):
        return pl.BlockSpec((1, tm, width), lambda b, i: (b, i, 0))

    def full(a):
        return pl.BlockSpec(a.shape, lambda b, i: (0,) * a.ndim)

    return pl.pallas_call(
        _out_body,
        grid=(bsz, s // tm),
        in_specs=[rows(o.shape[-1]) for o in o_parts] + [rows(D_MODEL), full(w_out), full(post_g), rows(D_MODEL)],
        out_specs=rows(D_MODEL),
        out_shape=jax.ShapeDtypeStruct(h.shape, F32),
        compiler_params=_params(2),
        name="out_proj",
    )(*o_parts, sg, w_out, post_g, h)


def _mla_prompt_body(q_ref, k_ref, vt_ref, qmask_ref, o_ref, m_sc, acc_sc, s_sc, *, tq):
    n_tiles = q_ref.shape[1] // tq
    n_pairs = n_tiles * (n_tiles + 1) // 2
    assert A_PIPE_UNROLL % 2 == 0 and n_pairs % A_PIPE_UNROLL == 0 and n_pairs >= 2 * A_PIPE_UNROLL
    m_sc[...] = jnp.full(m_sc.shape, NEG_INF, F32)
    acc_sc[...] = jnp.zeros(acc_sc.shape, F32)
    contract_last = (((1,), (1,)), ((), ()))

    def scores(qi, j, slot):
        q_start = pl.multiple_of(qi * tq, tq)
        k_start = pl.multiple_of(j * tq, tq)
        qmask = jnp.where(j == qi, qmask_ref[...], jnp.zeros_like(qmask_ref[...]))
        for hh in range(2):
            blk = slice(hh * LANES, (hh + 1) * LANES)
            q = q_ref[0, pl.ds(q_start, tq), blk] + qmask
            s_sc[slot, hh] = lax.dot_general(k_ref[0, pl.ds(k_start, tq), blk], q, contract_last,
                                             preferred_element_type=F32)

    def consume(qi, j, slot):
        for hh in range(2):
            s = s_sc[slot, hh]
            vt = vt_ref[0, j, hh * LANES:(hh + 1) * LANES, :]
            m_old = m_sc[qi, hh]
            m_new = jnp.maximum(m_old, jnp.max(s, axis=0, keepdims=True))
            p = jnp.exp2(s - m_new).astype(BF16)
            acc_sc[qi, hh] = jnp.exp2(m_old - m_new) * acc_sc[qi, hh] + jnp.dot(vt, p, preferred_element_type=F32)
            m_sc[qi, hh] = m_new

    def advance(qi, j):
        last = j == qi
        return jnp.where(last, qi + 1, qi), jnp.where(last, 0, j + 1)

    def stages(cur, count):
        for st in range(count):
            nxt = advance(*cur)
            scores(*nxt, 1 - (st & 1))
            consume(*cur, st & 1)
            cur = nxt
        return cur

    zero = jnp.int32(0)
    scores(zero, zero, 0)
    cur = lax.fori_loop(0, n_pairs // A_PIPE_UNROLL - 1, lambda i, c: stages(c, A_PIPE_UNROLL), (zero, zero))
    cur = stages(cur, A_PIPE_UNROLL - 1)
    consume(*cur, (A_PIPE_UNROLL - 1) & 1)

    lo = lax.broadcasted_iota(jnp.int32, (1, LANES), 1) < HALF

    def write_tile(qi, carry):
        outs = []
        for hh in range(2):
            acc = acc_sc[qi, hh]
            outs.append((acc / acc[A_V:A_V + 1, :]).T)
        o_ref[0, pl.ds(pl.multiple_of(qi * tq, tq), tq), :] = jnp.where(
            lo, outs[0], pltpu.roll(outs[1], HALF, 1)).astype(o_ref.dtype)
        return carry

    lax.fori_loop(0, n_tiles, write_tile, 0)


def _mla_prompt(qa, ka, vt, tq):
    bsz, s, _ = qa.shape
    pairs = A_HEADS // 2
    n_tiles = s // tq
    assert vt.shape[3] == tq and tq == A_MASK_CHUNKS * CHUNK
    q_chunk = np.arange(tq)[:, None] // CHUNK
    lane_chunk = np.arange(LANES)[None, :] - A_QK
    qmask = np.where((lane_chunk > q_chunk) & (lane_chunk < A_MASK_CHUNKS), NEG_INF, 0.0).astype(np.float32)
    return pl.pallas_call(
        functools.partial(_mla_prompt_body, tq=tq),
        grid=(bsz, pairs),
        in_specs=[pl.BlockSpec((1, s, 2 * LANES), lambda b, h: (b, 0, h)),
                  pl.BlockSpec((1, s, 2 * LANES), lambda b, h: (b, 0, h)),
                  pl.BlockSpec((1, n_tiles, 2 * LANES, tq), lambda b, h: (b, 0, h, 0)),
                  pl.BlockSpec((tq, LANES), lambda b, h: (0, 0))],
        out_specs=pl.BlockSpec((1, s, LANES), lambda b, h: (b, 0, h)),
        out_shape=jax.ShapeDtypeStruct((bsz, s, A_WIDTH), BF16),
        scratch_shapes=[pltpu.VMEM((n_tiles, 2, 1, tq), F32), pltpu.VMEM((n_tiles, 2, LANES, tq), F32),
                        pltpu.VMEM((2, 2, tq, tq), F32)],
        compiler_params=_params(2),
        name="mla_prompt",
    )(qa, ka, vt, jnp.asarray(qmask, BF16))


def _mla_sample_body(q_ref, cc_ref, ckr_ref, cn_ref, krn_ref, wkt_ref, wv_ref, sel_ref, o_ref,
                     qabs_sc, qr_sc, m_sc, acc_sc, *, tk):
    t = q_ref.shape[1]
    past = cc_ref.shape[1]
    for h in range(A_HEADS):
        qh = q_ref[0, :, h * LANES:(h + 1) * LANES]
        rows = slice(h * t, (h + 1) * t)
        qabs_sc[rows, :] = jnp.dot(qh, wkt_ref[h], preferred_element_type=F32).astype(BF16)
        qr_sc[rows, :] = jnp.dot(qh, sel_ref[...], preferred_element_type=F32).astype(BF16)
    m_sc[...] = jnp.full(m_sc.shape, NEG_INF, F32)
    acc_sc[...] = jnp.zeros(acc_sc.shape, F32)
    contract_last = (((1,), (1,)), ((), ()))

    def scores(c_t, kr_tt):
        return (lax.dot_general(c_t.astype(BF16), qabs_sc[...], contract_last, preferred_element_type=F32)
                + lax.dot_general(kr_tt.T.astype(BF16), qr_sc[...], contract_last, preferred_element_type=F32))

    def update(s, c_t):
        n_keys = c_t.shape[0]
        m_old = m_sc[...]
        m_new = jnp.maximum(m_old, jnp.max(s, axis=0, keepdims=True))
        p = jnp.exp2(s - m_new).astype(BF16)
        ct = jnp.concatenate([c_t.T.astype(BF16), jnp.ones((BF16_SUBLANES, n_keys), BF16)], axis=0)
        acc_sc[...] = jnp.exp2(m_old - m_new) * acc_sc[...] + jnp.dot(ct, p, preferred_element_type=F32)
        m_sc[...] = m_new

    def tile(j):
        return cc_ref[0, j * tk:(j + 1) * tk, :], ckr_ref[0, :, j * tk:(j + 1) * tk]

    n_tiles = past // tk
    cur = tile(0)
    s = scores(*cur)
    for j in range(n_tiles):
        nxt = tile(j + 1) if j + 1 < n_tiles else (cn_ref[0], krn_ref[0])
        s_next = scores(*nxt)
        update(s, cur[0])
        cur, s = nxt, s_next
    update(s, cur[0])

    acc = acc_sc[...]
    o_lat = (acc[:A_KV_RANK] / acc[A_KV_RANK:A_KV_RANK + 1]).T.astype(BF16)
    out = jnp.zeros((t, A_WIDTH), F32)
    for h in range(A_HEADS):
        out = out + jnp.dot(o_lat[h * t:(h + 1) * t, :], wv_ref[h], preferred_element_type=F32)
    o_ref[0] = out.astype(o_ref.dtype)


def _mla_sample(qa, cache_c, cache_kr, c_new, kr_new, wkt, wv, sel, tk):
    bsz, t, _ = qa.shape
    past = cache_c.shape[1]

    def per_b(shape):
        return pl.BlockSpec((1,) + shape, lambda b: (b, 0, 0))

    def full(a):
        return pl.BlockSpec(a.shape, lambda b: (0,) * a.ndim)

    rows = A_HEADS * t
    return pl.pallas_call(
        functools.partial(_mla_sample_body, tk=tk),
        grid=(bsz,),
        in_specs=[per_b((t, A_HEADS * LANES)), per_b((past, A_KV_RANK)), per_b((A_ROPE, past)),
                  per_b((t, A_KV_RANK)), per_b((A_ROPE, t)), full(wkt), full(wv), full(sel)],
        out_specs=per_b((t, A_WIDTH)),
        out_shape=jax.ShapeDtypeStruct((bsz, t, A_WIDTH), BF16),
        scratch_shapes=[pltpu.VMEM((rows, A_KV_RANK), BF16), pltpu.VMEM((rows, A_ROPE), BF16),
                        pltpu.VMEM((1, rows), F32), pltpu.VMEM((A_KV_RANK + BF16_SUBLANES, rows), F32)],
        compiler_params=_params(1),
        name="mla_sample",
    )(qa, cache_c, cache_kr, c_new, kr_new, wkt, wv, sel)


def _band_body(*refs, n_blk, n_qblk, cpg, ones_row, has_sink, variants, buffered=False):
    refs = list(refs)
    n_in = 5 if buffered else 3
    q_ref = refs[0]
    o_ref, s_sc, bias_sc = refs[-3:]
    extra = refs[n_in:-3]
    sink_ref = extra.pop() if has_sink else None
    toep_ref = extra.pop() if extra else None
    win = n_blk * LANES
    rows_g = cpg * CHUNK
    n_var = len(variants)
    qpc = 2 * n_qblk * CHUNK
    lo = lax.broadcasted_iota(jnp.int32, (CHUNK, LANES), 1) < HALF
    contract_last = (((1,), (1,)), ((), ()))

    if toep_ref is not None:
        key_chunk = lax.broadcasted_iota(jnp.int32, (win, LANES), 0) // CHUNK
        for v, chunks in enumerate(variants):
            for cc, (x0, lo_chunk, hi_chunk) in enumerate(chunks):
                tile = jnp.where((key_chunk >= lo_chunk) & (key_chunk <= hi_chunk), toep_ref[0, x0:x0 + win, :], NEG_INF)
                for j in range(qpc // LANES):
                    bias_sc[v, :, cc * qpc + j * LANES:cc * qpc + (j + 1) * LANES] = tile

    def store_scores(slot, variant, s):
        if toep_ref is not None:
            s_sc[slot] = s + bias_sc[variant]
            return
        s_sc[slot] = s
        for cc, (_, lo_chunk, hi_chunk) in enumerate(variants[variant]):
            cols = slice(cc * qpc, (cc + 1) * qpc)
            for r0, r1 in ((0, lo_chunk * CHUNK), ((hi_chunk + 1) * CHUNK, win)):
                if r1 > max(r0, 0):
                    s_sc[slot, max(r0, 0):r1, cols] = jnp.full((r1 - max(r0, 0), qpc), NEG_INF, F32)

    def aligned(x, n):
        return x if isinstance(x, int) else pl.multiple_of(x, n)

    def buffered_window(buf_ref, new_ref, bi, transposed):
        buf_t, new = buf_ref[bi], new_ref[bi]
        pad = jnp.zeros((win - buf_t.shape[1] - new.shape[0], LANES), F32)
        if transposed:
            return jnp.concatenate([buf_t, jnp.concatenate([new, pad], axis=0).T], axis=1)
        return jnp.concatenate([buf_t.T, new, pad], axis=0)

    def scores(a, wb, variant, slot, bi=0):
        pieces = []
        for cc in range(cpg):
            rows = pl.ds(aligned(a * rows_g + cc * CHUNK, CHUNK), CHUNK)
            for r in range(n_qblk):
                qblk = q_ref[bi, rows, r * LANES:(r + 1) * LANES]
                zero = jnp.zeros_like(qblk)
                pieces += [jnp.where(lo, qblk, zero), jnp.where(lo, zero, qblk)]
        qs = jnp.concatenate(pieces, axis=0)
        if buffered:
            kw = buffered_window(refs[1], refs[2], bi, False).astype(BF16)
        else:
            kw = refs[1][bi, pl.ds(aligned(wb * LANES, LANES), win), :]
        store_scores(slot, variant, lax.dot_general(kw, qs, contract_last, preferred_element_type=F32))

    def finish(a, wb, slot, bi=0):
        s = s_sc[slot]
        m = jnp.max(s, axis=0, keepdims=True)
        if has_sink:
            sink = sink_ref[0]
            m = jnp.maximum(m, sink)
        p = jnp.exp2(s - m)
        if buffered:
            vt = buffered_window(refs[3], refs[4], bi, True).astype(BF16)
        else:
            vt = jnp.concatenate([refs[2][bi, 0, wb + i] for i in range(n_blk)], axis=1)
        if not ones_row:
            vt = jnp.concatenate([vt, jnp.ones((BF16_SUBLANES, win), BF16)], axis=0)
        o = jnp.dot(vt, p.astype(BF16), preferred_element_type=F32)
        l = o[CHUNK:CHUNK + 1, :] if ones_row else o[LANES:LANES + 1, :]
        o = o[:LANES]
        if has_sink:
            l = l + jnp.exp2(sink - m)
        o = (o / l).T
        idx = 0
        for cc in range(cpg):
            rows = pl.ds(aligned(a * rows_g + cc * CHUNK, CHUNK), CHUNK)
            for r in range(n_qblk):
                top = o[idx * CHUNK:(idx + 1) * CHUNK]
                bot = o[(idx + 1) * CHUNK:(idx + 2) * CHUNK]
                if ones_row:
                    bot = pltpu.roll(bot, HALF, 1)
                o_ref[bi, rows, r * LANES:(r + 1) * LANES] = jnp.where(lo, top, bot).astype(o_ref.dtype)
                idx += 2

    n_total = q_ref.shape[1] // rows_g
    if n_total == 1:
        n_rows = q_ref.shape[0]
        scores(0, 0, 0, 0, 0)
        for bi in range(n_rows):
            if bi + 1 < n_rows:
                scores(0, 0, 0, (bi + 1) & 1, bi + 1)
            finish(0, 0, bi & 1, bi)
        return
    g_blk = rows_g // LANES
    n_lead = n_var - 1 + (n_var - 1) % 2
    assert (n_var - 1) * g_blk >= n_blk - g_blk and n_total - n_lead >= 2 and BAND_PIPE_UNROLL % 2 == 0

    def window_block(a):
        wb = (a + 1) * g_blk - n_blk
        return max(wb, 0) if isinstance(a, int) else wb

    def stage(a, slot):
        nxt = a + 1
        scores(nxt, window_block(nxt), min(nxt, n_var - 1) if isinstance(nxt, int) else n_var - 1, 1 - slot)
        finish(a, window_block(a), slot)

    scores(0, window_block(0), 0, 0)
    for a in range(n_lead):
        stage(a, a & 1)

    def trip(i, carry):
        for st in range(BAND_PIPE_UNROLL):
            stage(n_lead + BAND_PIPE_UNROLL * i + st, st & 1)
        return carry

    n_stages = n_total - 1 - n_lead
    lax.fori_loop(0, n_stages // BAND_PIPE_UNROLL, trip, 0)
    for a in range(n_total - 1 - n_stages % BAND_PIPE_UNROLL, n_total - 1):
        stage(a, (a - n_lead) & 1)
    finish(n_total - 1, window_block(n_total - 1), (n_total - 1 - n_lead) & 1)


def _band_attention(q, keys, values, toep, sink, *, variants, n_blk, n_groups, n_qblk, cpg, ones_row):
    bsz, s, _ = q.shape
    qw = n_qblk * LANES
    buffered = isinstance(keys, tuple)
    nb = bsz if s == cpg * CHUNK else 1

    def per_group(a):
        return pl.BlockSpec((nb, a.shape[1], LANES), lambda g, b: (b, 0, g))

    in_specs = [pl.BlockSpec((nb, s, qw), lambda g, b: (b, 0, g))]
    if buffered:
        args = [q, *keys, *values]
        for buf_t, new in (keys, values):
            in_specs += [pl.BlockSpec((nb, LANES, buf_t.shape[2]), lambda g, b: (b, g, 0)), per_group(new)]
    else:
        args = [q, keys, values]
        in_specs += [per_group(keys), pl.BlockSpec((nb, 1) + values.shape[2:], lambda g, b: (b, g, 0, 0, 0))]
    for extra in (toep, sink):
        if extra is not None:
            in_specs.append(pl.BlockSpec((1,) + extra.shape[1:], lambda g, b: (g, 0, 0)))
            args.append(extra)
    bias_shape = (n_blk * LANES, cpg * 2 * n_qblk * CHUNK)
    body = functools.partial(_band_body, n_blk=n_blk, n_qblk=n_qblk, cpg=cpg, ones_row=ones_row,
                             has_sink=sink is not None, variants=variants, buffered=buffered)
    return pl.pallas_call(
        body,
        grid=(n_groups, bsz // nb),
        in_specs=in_specs,
        out_specs=pl.BlockSpec((nb, s, qw), lambda g, b: (b, 0, g)),
        out_shape=jax.ShapeDtypeStruct(q.shape, BF16),
        scratch_shapes=[pltpu.VMEM((2,) + bias_shape, F32), pltpu.VMEM((len(variants),) + bias_shape, F32)],
        compiler_params=_params(2),
        name="band_attention",
    )(*args)


def _rope_tables(pos, rot, lane_pattern, from_zero=False):
    half = rot // 2
    inv = jnp.power(ROPE_THETA, -jnp.arange(half, dtype=F32) * 2.0 / rot)
    inv_lane, first, second = [], [], []
    for kind, width in lane_pattern:
        if kind == "rot":
            inv_lane += [inv, inv]
            first += [1.0] * half + [0.0] * half
            second += [0.0] * half + [1.0] * half
        else:
            inv_lane.append(jnp.zeros((width,), F32))
            first += [0.0] * width
            second += [0.0] * width
    inv_lane = jnp.concatenate(inv_lane)[None, :]
    n = pos.shape[0]
    if from_zero and n % CHUNK == 0:
        a_hi = (jnp.arange(n // CHUNK, dtype=F32) * CHUNK)[:, None] * inv_lane
        a_lo = jnp.arange(CHUNK, dtype=F32)[:, None] * inv_lane
        c_hi, s_hi = jnp.cos(a_hi)[:, None, :], jnp.sin(a_hi)[:, None, :]
        c_lo, s_lo = jnp.cos(a_lo)[None], jnp.sin(a_lo)[None]
        cos = (c_hi * c_lo - s_hi * s_lo).reshape(n, LANES)
        sin = (s_hi * c_lo + c_hi * s_lo).reshape(n, LANES)
    else:
        ang = pos.astype(F32)[:, None] * inv_lane
        cos, sin = jnp.cos(ang), jnp.sin(ang)
    return jnp.stack([cos, sin * np.asarray(second, np.float32), -sin * np.asarray(first, np.float32)])


A_ROPE_PATTERN = (("pad", A_NOPE), ("rot", A_ROPE), ("pad", LANES - A_QK))
C_ROPE_PATTERN = (("rot", C_ROT), ("pad", HALF - C_ROT)) * 2


def _prep_ab(pre, post, w_in, q_norm, kv_norm, w_uq, w_ukv, rel_bias, w_out):
    d = w_in.shape[0]
    q_lat, c_kv, k_r, g_a, q_b, k_b, v_b, g_b = jnp.split(
        w_in, [384, 640, 672, 1184, 1696, 2208, 2720], axis=1)
    kr_blk = jnp.concatenate([jnp.zeros((d, A_NOPE), F32), k_r, jnp.zeros((d, LANES - A_QK), F32)], axis=1)
    w_in_p = jnp.concatenate([q_lat, c_kv, kr_blk, g_a, g_b, q_b, k_b, v_b], axis=1).astype(BF16)
    w_uq_p = jnp.pad(w_uq.reshape(A_Q_RANK, A_HEADS, A_QK), ((0, 0), (0, 0), (0, LANES - A_QK)))
    w_uq_p = w_uq_p.reshape(A_Q_RANK, A_HEADS * LANES).astype(BF16)
    ukv = w_ukv.reshape(A_KV_RANK, A_HEADS, A_NOPE + A_V)
    w_uk, w_uv = ukv[..., :A_NOPE], ukv[..., A_NOPE:]
    pad_half = ((0, 0), (0, 0), (0, LANES - A_NOPE))
    w_k = jnp.pad(w_uk, pad_half).reshape(A_KV_RANK, A_HEADS * LANES).astype(BF16)
    w_vt = jnp.pad(w_uv, pad_half).reshape(A_KV_RANK, A_HEADS * LANES).T.astype(BF16)
    wkt = jnp.pad(jnp.transpose(w_uk, (1, 2, 0)), ((0, 0), (0, LANES - A_NOPE), (0, 0))).astype(BF16)
    eye = jnp.eye(A_HEADS, dtype=F32)
    wv_s = (jnp.transpose(w_uv, (1, 0, 2))[:, :, None, :] * eye[:, None, :, None]).reshape(
        A_HEADS, A_KV_RANK, A_WIDTH).astype(BF16)
    sel = (jnp.arange(LANES)[:, None] == A_NOPE + jnp.arange(A_ROPE)[None, :]).astype(BF16)
    rel_bias = rel_bias * LOG2E
    win_p = B_WIN_BLOCKS * LANES
    r0 = win_p - CHUNK
    x_len = r0 + win_p
    n_vec = x_len + CHUNK
    n_hi = r0 + CHUNK - 1 - B_MAX_REL
    n_lo = n_vec - n_hi - (2 * B_MAX_REL + 1)
    vec = jnp.concatenate([jnp.broadcast_to(rel_bias[:, -1:], (B_HEADS, n_hi)), rel_bias[:, ::-1],
                           jnp.broadcast_to(rel_bias[:, :1], (B_HEADS, n_lo))], axis=1)
    skew = jnp.tile(vec, (1, CHUNK))[:, :CHUNK * (n_vec - 1)].reshape(B_HEADS, CHUNK, n_vec - 1)
    toep = skew[:, :, CHUNK - 1:CHUNK - 1 + x_len]
    toep = jnp.transpose(toep.reshape(B_HEADS // 2, 2, CHUNK, x_len), (0, 3, 1, 2)).reshape(B_HEADS // 2, x_len, LANES)
    var_p = tuple(tuple((r0 - c * CHUNK, c - B_PAST_CHUNKS, c)
                        for c in range(B_GROUP_CHUNKS * v, B_GROUP_CHUNKS * (v + 1))) for v in range(B_VARIANTS))
    var_s =(((r0 - B_PAST_CHUNKS * CHUNK, 0, B_PAST_CHUNKS),),)
    return dict(pre=pre[None], w_in=w_in_p, q_norm=q_norm[None], kv_norm=kv_norm[None], w_uq=w_uq_p, w_k=w_k,
                w_vt=w_vt, wkt=wkt, wv_s=wv_s, sel=sel, toep=toep, var_p=var_p, var_s=var_s,
                w_out=w_out.astype(BF16), post=post[None])


def _prep_c(pre, post, w_in, sinks, w_out):
    q, k, v, g = jnp.split(w_in, [1024, 1152, 1280], axis=1)
    k0, k1 = k[:, :C_HEAD_DIM], k[:, C_HEAD_DIM:]
    v0, v1 = v[:, :C_HEAD_DIM], v[:, C_HEAD_DIM:]
    zero = jnp.zeros_like(v0)
    w_in_p = jnp.concatenate([q, k0, k0, k1, k1, v0, zero, v1, zero, g], axis=1).astype(BF16)
    var_p = tuple(tuple((0, c - C_PAST_CHUNKS, c) for c in range(C_GROUP_CHUNKS * v, C_GROUP_CHUNKS * (v + 1)))
                  for v in range(C_VARIANTS))
    var_s = (((0, 1, C_PAST_CHUNKS + 1),),)
    sink_row = jnp.repeat((sinks * LOG2E).reshape(C_KV_HEADS, C_GROUP), CHUNK, axis=1)[:, None, :]
    return dict(pre=pre[None], w_in=w_in_p, var_p=var_p, var_s=var_s, sink_s=sink_row,
                sink_p=jnp.tile(sink_row, (1, 1, C_GROUP_CHUNKS)), w_out=w_out.astype(BF16), post=post[None])


def _dup_heads(x):
    return jnp.concatenate([x[:, :, 0], x[:, :, 0], x[:, :, 1], x[:, :, 1]], axis=-1)


def kernel(x_prompt, x_sample, cache_a_ckv, cache_a_krope, cache_b_k, cache_b_v, cache_c_k, cache_c_v,
           ab_pre_norm, ab_post_norm, ab_w_in, ab_q_norm, ab_kv_norm, ab_w_uq, ab_w_ukv, ab_rel_bias, ab_w_out,
           c_pre_norm, c_post_norm, c_w_in, c_sinks, c_w_out):
    bsz, seq, _ = x_prompt.shape
    dbs, dseq, _ = x_sample.shape
    past = cache_a_ckv.shape[2]
    n_s = dbs * dseq
    pos_p = jnp.arange(seq, dtype=jnp.int32)
    pos_s = jnp.tile(past + jnp.arange(dseq, dtype=jnp.int32), dbs)
    wab = _prep_ab(ab_pre_norm[0], ab_post_norm[0], ab_w_in[0], ab_q_norm[0], ab_kv_norm[0], ab_w_uq[0],
                   ab_w_ukv[0], ab_rel_bias[0], ab_w_out[0])
    wc = _prep_c(c_pre_norm[0], c_post_norm[0], c_w_in[0], c_sinks[0], c_w_out[0])
    b_tail = min(B_PAST_CHUNKS * CHUNK, seq)
    c_tail = min(C_WINDOW, seq)
    tile = 512

    rope_a_p = _rope_tables(pos_p, A_ROPE, A_ROPE_PATTERN, from_zero=True)
    (qa, ka, vt, qb, kb, vbt, sg, c_new_p, kr_new_p_t, kb_tail, vb_tail) = _ab_in_proj(
        x_prompt, rope_a_p, wab, tm=tile, tail_len=b_tail, tkv=tile)
    o_a = _mla_prompt(qa, ka, vt, tq=tile)
    o_b = _band_attention(qb, kb, vbt, wab["toep"], None, variants=wab["var_p"], n_blk=B_WIN_BLOCKS,
                          n_groups=B_HEADS // 2, n_qblk=1, cpg=B_GROUP_CHUNKS, ones_row=False)

    rope_a_s = _rope_tables(pos_s, A_ROPE, A_ROPE_PATTERN)
    xs = x_sample.reshape(1, n_s, D_MODEL)
    (qa_s, _, _, qb_s, _, _, sg_s, c_new_s, kr_new_s_t, kb_s32, vb_s32) = _ab_in_proj(
        xs, rope_a_s, wab, tm=n_s, tail_len=n_s, tkv=n_s)
    kr_new_s_t = jnp.transpose(kr_new_s_t.reshape(A_ROPE, dbs, dseq), (1, 0, 2))
    o_a_s = _mla_sample(qa_s.reshape(dbs, dseq, -1), cache_a_ckv[0], jnp.swapaxes(cache_a_krope[0], 1, 2),
                        c_new_s.reshape(dbs, dseq, -1), kr_new_s_t,
                        wab["wkt"], wab["wv_s"], wab["sel"], tk=512)
    wb = cache_b_k.shape[2]

    def buf_t(cache):
        return jnp.transpose(cache[0], (0, 2, 3, 1)).reshape(dbs, B_WIDTH, wb)

    o_b_s = _band_attention(qb_s.reshape(dbs, dseq, -1),
                            (buf_t(cache_b_k), kb_s32.reshape(dbs, dseq, -1)),
                            (buf_t(cache_b_v), vb_s32.reshape(dbs, dseq, -1)),
                            wab["toep"], None, variants=wab["var_s"], n_blk=B_SAMPLE_BLOCKS, n_groups=B_HEADS // 2,
                            n_qblk=1, cpg=1, ones_row=False)

    rope_c_p = _rope_tables(pos_p, C_ROT, C_ROPE_PATTERN, from_zero=True)
    h1_p, qc, kc, vct, sgc, kc_tail, vc_tail = _c_in_proj(
        [o_a, o_b], sg, wab["w_out"], wab["post"], x_prompt, rope_c_p, wc, tm=tile, tail_len=c_tail)
    o_c = _band_attention(qc, kc, vct, None, wc["sink_p"], variants=wc["var_p"], n_blk=C_WIN_BLOCKS,
                          n_groups=C_KV_HEADS, n_qblk=C_GROUP // 2, cpg=C_GROUP_CHUNKS, ones_row=True)
    h2_p = _out_proj([o_c], sgc, wc["w_out"], wc["post"], h1_p, tm=tile)

    rope_c_s = _rope_tables(pos_s, C_ROT, C_ROPE_PATTERN)
    h1_s, qc_s, kc_s, _, sgc_s, kc_s32, vc_s32 = _c_in_proj(
        [o_a_s.reshape(1, n_s, -1), o_b_s.reshape(1, n_s, -1)], sg_s, wab["w_out"], wab["post"], xs, rope_c_s, wc,
        tm=n_s, tail_len=n_s)
    wcw = cache_c_k.shape[2]
    win_c = C_SAMPLE_BLOCKS * LANES
    n_pad = win_c - wcw - dseq
    kcb = jnp.concatenate([jnp.zeros((dbs, n_pad, C_KV_HEADS * LANES), BF16), _dup_heads(cache_c_k[0]).astype(BF16),
                           kc_s.reshape(dbs, dseq, -1)], axis=1)
    vcb = jnp.concatenate([jnp.zeros((dbs, n_pad, C_KV_HEADS, C_HEAD_DIM), F32), cache_c_v[0],
                           vc_s32.reshape(dbs, dseq, C_KV_HEADS, C_HEAD_DIM)], axis=1)
    vcb_t = jnp.concatenate([jnp.transpose(vcb, (0, 2, 3, 1)), jnp.ones((dbs, C_KV_HEADS, 1, win_c), F32),
                             jnp.zeros((dbs, C_KV_HEADS, LANES - C_HEAD_DIM - 1, win_c), F32)], axis=2)
    vcb_t = jnp.transpose(vcb_t.astype(BF16).reshape(dbs, C_KV_HEADS, LANES, C_SAMPLE_BLOCKS, LANES), (0, 1, 3, 2, 4))
    o_c_s = _band_attention(qc_s.reshape(dbs, dseq, -1), kcb, vcb_t, None, wc["sink_s"], variants=wc["var_s"],
                            n_blk=C_SAMPLE_BLOCKS, n_groups=C_KV_HEADS, n_qblk=C_GROUP // 2, cpg=1, ones_row=True)
    h2_s = _out_proj([o_c_s.reshape(1, n_s, -1)], sgc_s, wc["w_out"], wc["post"], h1_s, tm=n_s)

    def roll_in(buf, new):
        return jnp.concatenate([buf, new], axis=1)[:, -buf.shape[1]:][None]

    return (h2_p, h2_s.reshape(dbs, dseq, D_MODEL),
            c_new_p[None], jnp.swapaxes(kr_new_p_t, 1, 2)[None],
            kb_tail.reshape(1, bsz, b_tail, B_HEADS, B_HEAD_DIM), vb_tail.reshape(1, bsz, b_tail, B_HEADS, B_HEAD_DIM),
            kc_tail.reshape(1, bsz, c_tail, C_KV_HEADS, C_HEAD_DIM), vc_tail.reshape(1, bsz, c_tail, C_KV_HEADS, C_HEAD_DIM),
            c_new_s.reshape(1, dbs, dseq, A_KV_RANK), jnp.swapaxes(kr_new_s_t, 1, 2)[None],
            roll_in(cache_b_k[0], kb_s32.reshape(dbs, dseq, B_HEADS, B_HEAD_DIM)),
            roll_in(cache_b_v[0], vb_s32.reshape(dbs, dseq, B_HEADS, B_HEAD_DIM)),
            roll_in(cache_c_k[0], kc_s32.reshape(dbs, dseq, C_KV_HEADS, C_HEAD_DIM)),
            roll_in(cache_c_v[0], vc_s32.reshape(dbs, dseq, C_KV_HEADS, C_HEAD_DIM)))
```

```python
import functools

import jax
import jax.numpy as jnp
import numpy as np
from jax import lax
from jax.experimental import pallas as pl
from jax.experimental.pallas import tpu as pltpu

F32 = jnp.float32
BF16 = jnp.bfloat16

D_MODEL = 1024
CHUNK = 64
ROPE_THETA = 500000.0
RMS_EPS = 1e-6
NEG_INF = -1e30

A_HEADS = 8
A_NOPE = 64
A_ROPE = 32
A_QK = A_NOPE + A_ROPE
A_V = 64
A_Q_RANK = 384
A_KV_RANK = 256
A_WIDTH = A_HEADS * A_V
LOG2E = 1.4426950408889634
A_SCALE = A_QK ** -0.5 * LOG2E

B_HEADS = 8
B_HEAD_DIM = 64
B_WIDTH = B_HEADS * B_HEAD_DIM
B_PAST_CHUNKS = 8
B_MAX_REL = 128
B_SCALE = B_HEAD_DIM ** -0.5 * LOG2E

C_HEADS = 16
C_KV_HEADS = 2
C_GROUP = C_HEADS // C_KV_HEADS
C_HEAD_DIM = 64
C_WIDTH = C_HEADS * C_HEAD_DIM
C_WINDOW = 128
C_PAST_CHUNKS = C_WINDOW // CHUNK
C_ROT = C_HEAD_DIM // 4
C_SCALE = C_HEAD_DIM ** -0.5 * LOG2E

LANES = 128
HALF = LANES // 2
BF16_SUBLANES = 16
VMEM_LIMIT = 56 * 1024 * 1024
A_MASK_CHUNKS = 8
assert A_QK + A_MASK_CHUNKS <= LANES
A_PIPE_UNROLL = 8
BAND_PIPE_UNROLL = 8
B_GROUP_CHUNKS = 4
C_GROUP_CHUNKS = 2
B_WIN_BLOCKS = (B_PAST_CHUNKS + B_GROUP_CHUNKS) * CHUNK // LANES
C_WIN_BLOCKS = (C_PAST_CHUNKS + C_GROUP_CHUNKS) * CHUNK // LANES
B_VARIANTS = B_PAST_CHUNKS // B_GROUP_CHUNKS + 1
C_VARIANTS = C_PAST_CHUNKS // C_GROUP_CHUNKS + 1
B_SAMPLE_BLOCKS = (B_PAST_CHUNKS + 2) * CHUNK // LANES
C_SAMPLE_BLOCKS = (C_PAST_CHUNKS + 2) * CHUNK // LANES

AB_Q0, AB_C0, AB_KR0, AB_G0, AB_QB0, AB_KB0, AB_VB0, AB_NZ = 0, 384, 640, 768, 1792, 2304, 2816, 3328
C_Q0, C_K0, C_V0, C_G0, C_NZ = 0, 1024, 1280, 1536, 2560


def _params(n_axes):
    return pltpu.CompilerParams(dimension_semantics=("arbitrary",) * n_axes, vmem_limit_bytes=VMEM_LIMIT)


def _rms(x, g):
    return x * lax.rsqrt(jnp.mean(x * x, axis=-1, keepdims=True) + RMS_EPS) * g


def _rope_block(blk, rope_ref, shift):
    return (blk * rope_ref[0] + pltpu.roll(blk, shift, 1) * rope_ref[1]
            + pltpu.roll(blk, LANES - shift, 1) * rope_ref[2])


def _tail_spec(n_tiles, tm, tail_len, width):
    tb = min(tail_len, tm)
    n_blk = tail_len // tb
    return tb, pl.BlockSpec((1, tb, width), lambda b, i: (b, jnp.maximum(i - (n_tiles - n_blk), 0), 0))


def _ab_in_body(x_ref, pre_ref, w_ref, qn_ref, kvn_ref, wuq_ref, wk_ref, wvt_ref, rope_ref,
                qa_ref, ka_ref, vt_ref, qb_ref, kb_ref, vbt_ref, sg_ref, c_ref, kr_ref, kb_tail_ref, vb_tail_ref):
    tm = x_ref.shape[1]
    tb = kb_tail_ref.shape[1]
    xn = _rms(x_ref[0], pre_ref[...]).astype(BF16)
    z = jnp.dot(xn, w_ref[...], preferred_element_type=F32)

    qn = _rms(z[:, AB_Q0:AB_C0], qn_ref[...]).astype(BF16)
    qa = jnp.dot(qn, wuq_ref[...], preferred_element_type=F32) * A_SCALE
    for h in range(A_HEADS):
        blk = slice(h * LANES, (h + 1) * LANES)
        qa_ref[0, :, blk] = _rope_block(qa[:, blk], rope_ref, A_ROPE // 2).astype(BF16)

    c_new = _rms(z[:, AB_C0:AB_KR0], kvn_ref[...])
    c_ref[0] = c_new
    cb = c_new.astype(BF16)
    krot = _rope_block(z[:, AB_KR0:AB_G0], rope_ref, A_ROPE // 2)
    kr_ref[0] = krot.T[A_NOPE:A_NOPE + A_ROPE, :]
    kn = jnp.dot(cb, wk_ref[...], preferred_element_type=F32)
    row = pl.program_id(1) * tm + lax.broadcasted_iota(jnp.int32, (tm, LANES), 0)
    lane = lax.broadcasted_iota(jnp.int32, (tm, LANES), 1)
    k_shared = krot + jnp.where(lane - A_QK == (row // CHUNK) % A_MASK_CHUNKS, 1.0, 0.0)
    for h in range(A_HEADS):
        blk = slice(h * LANES, (h + 1) * LANES)
        ka_ref[0, :, blk] = (kn[:, blk] + k_shared).astype(BF16)
    vt = jnp.dot(wvt_ref[...], c_new.T.astype(BF16), preferred_element_type=F32)
    row = lax.broadcasted_iota(jnp.int32, vt.shape, 0)
    vt_ref[0, 0] = jnp.where((row & (LANES - 1)) == A_V, 1.0, vt).astype(BF16)

    g = z[:, AB_G0:AB_QB0]
    sg_ref[0] = (g * jax.nn.sigmoid(g)).astype(BF16)
    qb_ref[0] = (z[:, AB_QB0:AB_KB0] * B_SCALE).astype(BF16)
    kb = z[:, AB_KB0:AB_VB0]
    vb = z[:, AB_VB0:AB_NZ]
    kb_ref[0] = kb.astype(BF16)
    vbt = vb.T.astype(BF16)
    for hp in range(B_HEADS // 2):
        for j in range(tm // LANES):
            vbt_ref[0, hp, j] = vbt[hp * LANES:(hp + 1) * LANES, j * LANES:(j + 1) * LANES]
    kb_tail_ref[0] = kb[tm - tb:, :]
    vb_tail_ref[0] = vb[tm - tb:, :]


def _ab_in_proj(x, rope, w, tm, tail_len, tkv):
    bsz, s, _ = x.shape
    nt = s // tm
    per_kv = tkv // tm
    tb, tail_spec = _tail_spec(nt, tm, tail_len, B_WIDTH)

    def full(a):
        return pl.BlockSpec(a.shape, lambda b, i: (0,) * a.ndim, pipeline_mode=pl.Buffered(1))

    def rows(width):
        return pl.BlockSpec((1, tm, width), lambda b, i: (b, i, 0))

    weights = (w["pre"], w["w_in"], w["q_norm"], w["kv_norm"], w["w_uq"], w["w_k"], w["w_vt"])
    vt_spec = pl.BlockSpec((1, 1, A_HEADS * LANES, tm), lambda b, i: (b, i // per_kv, 0, i % per_kv))
    out_shape = (
        jax.ShapeDtypeStruct((bsz, s, A_HEADS * LANES), BF16),
        jax.ShapeDtypeStruct((bsz, s, A_HEADS * LANES), BF16),
        jax.ShapeDtypeStruct((bsz, s // tkv, A_HEADS * LANES, tkv), BF16),
        jax.ShapeDtypeStruct((bsz, s, B_WIDTH), BF16),
        jax.ShapeDtypeStruct((bsz, s, B_WIDTH), BF16),
        jax.ShapeDtypeStruct((bsz, B_HEADS // 2, s // LANES, LANES, LANES), BF16),
        jax.ShapeDtypeStruct((bsz, s, A_WIDTH + B_WIDTH), BF16),
        jax.ShapeDtypeStruct((bsz, s, A_KV_RANK), F32),
        jax.ShapeDtypeStruct((bsz, A_ROPE, s), F32),
        jax.ShapeDtypeStruct((bsz, tail_len, B_WIDTH), F32),
        jax.ShapeDtypeStruct((bsz, tail_len, B_WIDTH), F32),
    )
    vbt_spec = pl.BlockSpec((1, B_HEADS // 2, tm // LANES, LANES, LANES), lambda b, i: (b, 0, i, 0, 0))
    out_specs = (rows(1024), rows(1024), vt_spec, rows(512), rows(512), vbt_spec, rows(1024),
                 rows(A_KV_RANK), pl.BlockSpec((1, A_ROPE, tm), lambda b, i: (b, 0, i)), tail_spec, tail_spec)
    return pl.pallas_call(
        _ab_in_body,
        grid=(bsz, nt),
        in_specs=[rows(D_MODEL)] + [full(a) for a in weights]
        + [pl.BlockSpec((3, tm, LANES), lambda b, i: (0, i, 0))],
        out_specs=out_specs,
        out_shape=out_shape,
        compiler_params=_params(2),
        name="ab_in_proj",
    )(x, *weights, rope)


def _mix_out(o_refs, sg_ref, w_ref, g_ref, h_ref):
    o = jnp.concatenate([r[0].astype(F32) for r in o_refs], axis=-1) if len(o_refs) > 1 else o_refs[0][0].astype(F32)
    mixed = (o * sg_ref[0].astype(F32)).astype(BF16)
    y = jnp.dot(mixed, w_ref[...], preferred_element_type=F32)
    return h_ref[0] + _rms(y, g_ref[...])


def _c_in_body(*refs):
    n_o = len(refs) - 14
    o_refs = refs[:n_o]
    (sg0_ref, w0_ref, post0_ref, x_ref, pre_ref, w_ref, rope_ref,
     h_ref, q_ref, k_ref, vt_ref, sg_ref, k_tail_ref, v_tail_ref) = refs[n_o:]
    tm = x_ref.shape[1]
    tb = k_tail_ref.shape[1]
    h = _mix_out(o_refs, sg0_ref, w0_ref, post0_ref, x_ref)
    h_ref[0] = h
    xn = _rms(h, pre_ref[...]).astype(BF16)
    z = jnp.dot(xn, w_ref[...], preferred_element_type=F32)
    half_rot = C_ROT // 2
    for j in range(C_WIDTH // LANES):
        blk = slice(C_Q0 + j * LANES, C_Q0 + (j + 1) * LANES)
        q_ref[0, :, j * LANES:(j + 1) * LANES] = (_rope_block(z[:, blk], rope_ref, half_rot) * C_SCALE).astype(BF16)
    kd = [_rope_block(z[:, C_K0 + j * LANES:C_K0 + (j + 1) * LANES], rope_ref, half_rot) for j in range(C_KV_HEADS)]
    vz = [z[:, C_V0 + j * LANES:C_V0 + (j + 1) * LANES] for j in range(C_KV_HEADS)]
    row = lax.broadcasted_iota(jnp.int32, (LANES, tm), 0)
    for j in range(C_KV_HEADS):
        k_ref[0, :, j * LANES:(j + 1) * LANES] = kd[j].astype(BF16)
        vt = jnp.where(row == C_HEAD_DIM, 1.0, vz[j].T).astype(BF16)
        for i in range(tm // LANES):
            vt_ref[0, j, i] = vt[:, i * LANES:(i + 1) * LANES]
    g = z[:, C_G0:C_NZ]
    sg_ref[0] = (g * jax.nn.sigmoid(g)).astype(BF16)
    lo = lax.broadcasted_iota(jnp.int32, (1, LANES), 1) < HALF
    k_tail_ref[0] = jnp.where(lo, kd[0], kd[1])[tm - tb:, :]
    v_tail_ref[0] = (vz[0] + pltpu.roll(vz[1], HALF, 1))[tm - tb:, :]


def _c_in_proj(o_parts, sg0, w_out0, post0, x, rope, w, tm, tail_len):
    bsz, s, _ = x.shape
    nt = s // tm
    tb, tail_spec = _tail_spec(nt, tm, tail_len, LANES)

    def full(a):
        return pl.BlockSpec(a.shape, lambda b, i: (0,) * a.ndim)

    def rows(width):
        return pl.BlockSpec((1, tm, width), lambda b, i: (b, i, 0))

    out_shape = (
        jax.ShapeDtypeStruct((bsz, s, D_MODEL), F32),
        jax.ShapeDtypeStruct((bsz, s, C_WIDTH), BF16),
        jax.ShapeDtypeStruct((bsz, s, C_KV_HEADS * LANES), BF16),
        jax.ShapeDtypeStruct((bsz, C_KV_HEADS, s // LANES, LANES, LANES), BF16),
        jax.ShapeDtypeStruct((bsz, s, C_WIDTH), BF16),
        jax.ShapeDtypeStruct((bsz, tail_len, LANES), F32),
        jax.ShapeDtypeStruct((bsz, tail_len, LANES), F32),
    )
    return pl.pallas_call(
        _c_in_body,
        grid=(bsz, nt),
        in_specs=[rows(o.shape[-1]) for o in o_parts]
        + [rows(D_MODEL), full(w_out0), full(post0), rows(D_MODEL), full(w["pre"]), full(w["w_in"]),
           pl.BlockSpec((3, tm, LANES), lambda b, i: (0, i, 0))],
        out_specs=(rows(D_MODEL), rows(C_WIDTH), rows(256),
                   pl.BlockSpec((1, C_KV_HEADS, tm // LANES, LANES, LANES), lambda b, i: (b, 0, i, 0, 0)),
                   rows(C_WIDTH), tail_spec, tail_spec),
        out_shape=out_shape,
        compiler_params=_params(2),
        name="c_in_proj",
    )(*o_parts, sg0, w_out0, post0, x, w["pre"], w["w_in"], rope)


def _out_body(*refs):
    o_refs, (sg_ref, w_ref, g_ref, h_ref, out_ref) = refs[:-5], refs[-5:]
    out_ref[0] = _mix_out(o_refs, sg_ref, w_ref, g_ref, h_ref)


def _out_proj(o_parts, sg, w_out, post_g, h, tm):
    bsz, s, _ = h.shape

    def rows(width):
        return pl.BlockSpec((1, tm, width), lambda b, i: (b, i, 0))

    def full(a):
        return pl.BlockSpec(a.shape, lambda b, i: (0,) * a.ndim)

    return pl.pallas_call(
        _out_body,
        grid=(bsz, s // tm),
        in_specs=[rows(o.shape[-1]) for o in o_parts] + [rows(D_MODEL), full(w_out), full(post_g), rows(D_MODEL)],
        out_specs=rows(D_MODEL),
        out_shape=jax.ShapeDtypeStruct(h.shape, F32),
        compiler_params=_params(2),
        name="out_proj",
    )(*o_parts, sg, w_out, post_g, h)


def _mla_prompt_body(q_ref, k_ref, vt_ref, qmask_ref, o_ref, m_sc, acc_sc, s_sc, *, tq):
    n_tiles = q_ref.shape[1] // tq
    n_pairs = n_tiles * (n_tiles + 1) // 2
    assert A_PIPE_UNROLL % 2 == 0 and n_pairs % A_PIPE_UNROLL == 0 and n_pairs >= 2 * A_PIPE_UNROLL
    m_sc[...] = jnp.full(m_sc.shape, NEG_INF, F32)
    acc_sc[...] = jnp.zeros(acc_sc.shape, F32)
    contract_last = (((1,), (1,)), ((), ()))

    def scores(qi, j, slot, heads=(0, 1)):
        q_start = pl.multiple_of(qi * tq, tq)
        k_start = pl.multiple_of(j * tq, tq)
        qmask = jnp.where(j == qi, qmask_ref[...], jnp.zeros_like(qmask_ref[...]))
        for hh in heads:
            blk = slice(hh * LANES, (hh + 1) * LANES)
            q = q_ref[0, pl.ds(q_start, tq), blk] + qmask
            s_sc[slot, hh] = lax.dot_general(k_ref[0, pl.ds(k_start, tq), blk], q, contract_last,
                                             preferred_element_type=F32)

    def consume(qi, j, slot, heads=(0, 1)):
        for hh in heads:
            s = s_sc[slot, hh]
            vt = vt_ref[0, j, hh * LANES:(hh + 1) * LANES, :]
            m_old = m_sc[qi, hh]
            m_new = jnp.maximum(m_old, jnp.max(s, axis=0, keepdims=True))
            p = jnp.exp2(s - m_new).astype(BF16)
            acc_sc[qi, hh] = jnp.exp2(m_old - m_new) * acc_sc[qi, hh] + jnp.dot(vt, p, preferred_element_type=F32)
            m_sc[qi, hh] = m_new

    def advance(qi, j):
        last = j == qi
        return jnp.where(last, qi + 1, qi), jnp.where(last, 0, j + 1)

    def stages(cur, count):
        for st in range(count):
            nxt = advance(*cur)
            for hh in range(2):
                scores(*nxt, 1 - (st & 1), heads=(hh,))
                consume(*cur, st & 1, heads=(hh,))
            cur = nxt
        return cur

    zero = jnp.int32(0)
    scores(zero, zero, 0)
    cur = lax.fori_loop(0, n_pairs // A_PIPE_UNROLL - 1, lambda i, c: stages(c, A_PIPE_UNROLL), (zero, zero))
    cur = stages(cur, A_PIPE_UNROLL - 1)
    consume(*cur, (A_PIPE_UNROLL - 1) & 1)

    lo = lax.broadcasted_iota(jnp.int32, (1, LANES), 1) < HALF

    def write_tile(qi, carry):
        outs = []
        for hh in range(2):
            acc = acc_sc[qi, hh]
            outs.append((acc / acc[A_V:A_V + 1, :]).T)
        o_ref[0, pl.ds(pl.multiple_of(qi * tq, tq), tq), :] = jnp.where(
            lo, outs[0], pltpu.roll(outs[1], HALF, 1)).astype(o_ref.dtype)
        return carry

    lax.fori_loop(0, n_tiles, write_tile, 0)


def _mla_prompt(qa, ka, vt, tq):
    bsz, s, _ = qa.shape
    pairs = A_HEADS // 2
    n_tiles = s // tq
    assert vt.shape[3] == tq and tq == A_MASK_CHUNKS * CHUNK
    q_chunk = np.arange(tq)[:, None] // CHUNK
    lane_chunk = np.arange(LANES)[None, :] - A_QK
    qmask = np.where((lane_chunk > q_chunk) & (lane_chunk < A_MASK_CHUNKS), NEG_INF, 0.0).astype(np.float32)
    return pl.pallas_call(
        functools.partial(_mla_prompt_body, tq=tq),
        grid=(bsz, pairs),
        in_specs=[pl.BlockSpec((1, s, 2 * LANES), lambda b, h: (b, 0, h)),
                  pl.BlockSpec((1, s, 2 * LANES), lambda b, h: (b, 0, h)),
                  pl.BlockSpec((1, n_tiles, 2 * LANES, tq), lambda b, h: (b, 0, h, 0)),
                  pl.BlockSpec((tq, LANES), lambda b, h: (0, 0))],
        out_specs=pl.BlockSpec((1, s, LANES), lambda b, h: (b, 0, h)),
        out_shape=jax.ShapeDtypeStruct((bsz, s, A_WIDTH), BF16),
        scratch_shapes=[pltpu.VMEM((n_tiles, 2, 1, tq), F32), pltpu.VMEM((n_tiles, 2, LANES, tq), F32),
                        pltpu.VMEM((2, 2, tq, tq), F32)],
        compiler_params=_params(2),
        name="mla_prompt",
    )(qa, ka, vt, jnp.asarray(qmask, BF16))


def _mla_sample_body(q_ref, cc_ref, ckr_ref, cn_ref, krn_ref, wkt_ref, wv_ref, sel_ref, o_ref,
                     qabs_sc, qr_sc, m_sc, acc_sc, *, tk):
    t = q_ref.shape[1]
    past = cc_ref.shape[1]
    for h in range(A_HEADS):
        qh = q_ref[0, :, h * LANES:(h + 1) * LANES]
        rows = slice(h * t, (h + 1) * t)
        qabs_sc[rows, :] = jnp.dot(qh, wkt_ref[h], preferred_element_type=F32).astype(BF16)
        qr_sc[rows, :] = jnp.dot(qh, sel_ref[...], preferred_element_type=F32).astype(BF16)
    m_sc[...] = jnp.full(m_sc.shape, NEG_INF, F32)
    acc_sc[...] = jnp.zeros(acc_sc.shape, F32)
    contract_last = (((1,), (1,)), ((), ()))

    def scores(c_t, kr_tt):
        return (lax.dot_general(c_t.astype(BF16), qabs_sc[...], contract_last, preferred_element_type=F32)
                + lax.dot_general(kr_tt.T.astype(BF16), qr_sc[...], contract_last, preferred_element_type=F32))

    def update(s, c_t):
        n_keys = c_t.shape[0]
        m_old = m_sc[...]
        m_new = jnp.maximum(m_old, jnp.max(s, axis=0, keepdims=True))
        p = jnp.exp2(s - m_new).astype(BF16)
        ct = jnp.concatenate([c_t.T.astype(BF16), jnp.ones((BF16_SUBLANES, n_keys), BF16)], axis=0)
        acc_sc[...] = jnp.exp2(m_old - m_new) * acc_sc[...] + jnp.dot(ct, p, preferred_element_type=F32)
        m_sc[...] = m_new

    def tile(j):
        return cc_ref[0, j * tk:(j + 1) * tk, :], ckr_ref[0, :, j * tk:(j + 1) * tk]

    n_tiles = past // tk
    cur = tile(0)
    s = scores(*cur)
    for j in range(n_tiles):
        nxt = tile(j + 1) if j + 1 < n_tiles else (cn_ref[0], krn_ref[0])
        s_next = scores(*nxt)
        update(s, cur[0])
        cur, s = nxt, s_next
    update(s, cur[0])

    acc = acc_sc[...]
    o_lat = (acc[:A_KV_RANK] / acc[A_KV_RANK:A_KV_RANK + 1]).T.astype(BF16)
    out = jnp.zeros((t, A_WIDTH), F32)
    for h in range(A_HEADS):
        out = out + jnp.dot(o_lat[h * t:(h + 1) * t, :], wv_ref[h], preferred_element_type=F32)
    o_ref[0] = out.astype(o_ref.dtype)


def _mla_sample(qa, cache_c, cache_kr, c_new, kr_new, wkt, wv, sel, tk):
    bsz, t, _ = qa.shape
    past = cache_c.shape[1]

    def per_b(shape):
        return pl.BlockSpec((1,) + shape, lambda b: (b, 0, 0))

    def full(a):
        return pl.BlockSpec(a.shape, lambda b: (0,) * a.ndim)

    rows = A_HEADS * t
    return pl.pallas_call(
        functools.partial(_mla_sample_body, tk=tk),
        grid=(bsz,),
        in_specs=[per_b((t, A_HEADS * LANES)), per_b((past, A_KV_RANK)), per_b((A_ROPE, past)),
                  per_b((t, A_KV_RANK)), per_b((A_ROPE, t)), full(wkt), full(wv), full(sel)],
        out_specs=per_b((t, A_WIDTH)),
        out_shape=jax.ShapeDtypeStruct((bsz, t, A_WIDTH), BF16),
        scratch_shapes=[pltpu.VMEM((rows, A_KV_RANK), BF16), pltpu.VMEM((rows, A_ROPE), BF16),
                        pltpu.VMEM((1, rows), F32), pltpu.VMEM((A_KV_RANK + BF16_SUBLANES, rows), F32)],
        compiler_params=_params(1),
        name="mla_sample",
    )(qa, cache_c, cache_kr, c_new, kr_new, wkt, wv, sel)


def _band_body(*refs, n_blk, n_qblk, cpg, ones_row, has_sink, variants, buffered=False):
    refs = list(refs)
    n_in = 5 if buffered else 3
    q_ref = refs[0]
    o_ref, s_sc, bias_sc = refs[-3:]
    extra = refs[n_in:-3]
    sink_ref = extra.pop() if has_sink else None
    toep_ref = extra.pop() if extra else None
    win = n_blk * LANES
    rows_g = cpg * CHUNK
    n_var = len(variants)
    qpc = 2 * n_qblk * CHUNK
    lo = lax.broadcasted_iota(jnp.int32, (CHUNK, LANES), 1) < HALF
    contract_last = (((1,), (1,)), ((), ()))

    if toep_ref is not None:
        key_chunk = lax.broadcasted_iota(jnp.int32, (win, LANES), 0) // CHUNK
        for v, chunks in enumerate(variants):
            for cc, (x0, lo_chunk, hi_chunk) in enumerate(chunks):
                tile = jnp.where((key_chunk >= lo_chunk) & (key_chunk <= hi_chunk), toep_ref[0, x0:x0 + win, :], NEG_INF)
                for j in range(qpc // LANES):
                    bias_sc[v, :, cc * qpc + j * LANES:cc * qpc + (j + 1) * LANES] = tile

    def store_scores(slot, variant, s):
        if toep_ref is not None:
            s_sc[slot] = s + bias_sc[variant]
            return
        s_sc[slot] = s
        for cc, (_, lo_chunk, hi_chunk) in enumerate(variants[variant]):
            cols = slice(cc * qpc, (cc + 1) * qpc)
            for r0, r1 in ((0, lo_chunk * CHUNK), ((hi_chunk + 1) * CHUNK, win)):
                if r1 > max(r0, 0):
                    s_sc[slot, max(r0, 0):r1, cols] = jnp.full((r1 - max(r0, 0), qpc), NEG_INF, F32)

    def aligned(x, n):
        return x if isinstance(x, int) else pl.multiple_of(x, n)

    def buffered_window(buf_ref, new_ref, bi, transposed):
        buf_t, new = buf_ref[bi], new_ref[bi]
        pad = jnp.zeros((win - buf_t.shape[1] - new.shape[0], LANES), F32)
        if transposed:
            return jnp.concatenate([buf_t, jnp.concatenate([new, pad], axis=0).T], axis=1)
        return jnp.concatenate([buf_t.T, new, pad], axis=0)

    def scores(a, wb, variant, slot, bi=0):
        pieces = []
        for cc in range(cpg):
            rows = pl.ds(aligned(a * rows_g + cc * CHUNK, CHUNK), CHUNK)
            for r in range(n_qblk):
                qblk = q_ref[bi, rows, r * LANES:(r + 1) * LANES]
                zero = jnp.zeros_like(qblk)
                pieces += [jnp.where(lo, qblk, zero), jnp.where(lo, zero, qblk)]
        qs = jnp.concatenate(pieces, axis=0)
        if buffered:
            kw = buffered_window(refs[1], refs[2], bi, False).astype(BF16)
        else:
            kw = refs[1][bi, pl.ds(aligned(wb * LANES, LANES), win), :]
        store_scores(slot, variant, lax.dot_general(kw, qs, contract_last, preferred_element_type=F32))

    def finish(a, wb, slot, bi=0):
        s = s_sc[slot]
        m = jnp.max(s, axis=0, keepdims=True)
        if has_sink:
            sink = sink_ref[0]
            m = jnp.maximum(m, sink)
        p = jnp.exp2(s - m)
        if buffered:
            vt = buffered_window(refs[3], refs[4], bi, True).astype(BF16)
        else:
            vt = jnp.concatenate([refs[2][bi, 0, wb + i] for i in range(n_blk)], axis=1)
        if not ones_row:
            vt = jnp.concatenate([vt, jnp.ones((BF16_SUBLANES, win), BF16)], axis=0)
        o = jnp.dot(vt, p.astype(BF16), preferred_element_type=F32)
        l = o[CHUNK:CHUNK + 1, :] if ones_row else o[LANES:LANES + 1, :]
        o = o[:LANES]
        if has_sink:
            l = l + jnp.exp2(sink - m)
        o = (o / l).T
        idx = 0
        for cc in range(cpg):
            rows = pl.ds(aligned(a * rows_g + cc * CHUNK, CHUNK), CHUNK)
            for r in range(n_qblk):
                top = o[idx * CHUNK:(idx + 1) * CHUNK]
                bot = o[(idx + 1) * CHUNK:(idx + 2) * CHUNK]
                if ones_row:
                    bot = pltpu.roll(bot, HALF, 1)
                o_ref[bi, rows, r * LANES:(r + 1) * LANES] = jnp.where(lo, top, bot).astype(o_ref.dtype)
                idx += 2

    n_total = q_ref.shape[1] // rows_g
    if n_total == 1:
        n_rows = q_ref.shape[0]
        scores(0, 0, 0, 0, 0)
        for bi in range(n_rows):
            if bi + 1 < n_rows:
                scores(0, 0, 0, (bi + 1) & 1, bi + 1)
            finish(0, 0, bi & 1, bi)
        return
    g_blk = rows_g // LANES
    n_lead = n_var - 1 + (n_var - 1) % 2
    assert (n_var - 1) * g_blk >= n_blk - g_blk and n_total - n_lead >= 2 and BAND_PIPE_UNROLL % 2 == 0

    def window_block(a):
        wb = (a + 1) * g_blk - n_blk
        return max(wb, 0) if isinstance(a, int) else wb

    def stage(a, slot):
        nxt = a + 1
        scores(nxt, window_block(nxt), min(nxt, n_var - 1) if isinstance(nxt, int) else n_var - 1, 1 - slot)
        finish(a, window_block(a), slot)

    scores(0, window_block(0), 0, 0)
    for a in range(n_lead):
        stage(a, a & 1)

    def trip(i, carry):
        for st in range(BAND_PIPE_UNROLL):
            stage(n_lead + BAND_PIPE_UNROLL * i + st, st & 1)
        return carry

    n_stages = n_total - 1 - n_lead
    lax.fori_loop(0, n_stages // BAND_PIPE_UNROLL, trip, 0)
    for a in range(n_total - 1 - n_stages % BAND_PIPE_UNROLL, n_total - 1):
        stage(a, (a - n_lead) & 1)
    finish(n_total - 1, window_block(n_total - 1), (n_total - 1 - n_lead) & 1)


def _band_attention(q, keys, values, toep, sink, *, variants, n_blk, n_groups, n_qblk, cpg, ones_row):
    bsz, s, _ = q.shape
    qw = n_qblk * LANES
    buffered = isinstance(keys, tuple)
    nb = bsz if s == cpg * CHUNK else 1

    def per_group(a):
        return pl.BlockSpec((nb, a.shape[1], LANES), lambda g, b: (b, 0, g))

    in_specs = [pl.BlockSpec((nb, s, qw), lambda g, b: (b, 0, g))]
    if buffered:
        args = [q, *keys, *values]
        for buf_t, new in (keys, values):
            in_specs += [pl.BlockSpec((nb, LANES, buf_t.shape[2]), lambda g, b: (b, g, 0)), per_group(new)]
    else:
        args = [q, keys, values]
        in_specs += [per_group(keys), pl.BlockSpec((nb, 1) + values.shape[2:], lambda g, b: (b, g, 0, 0, 0))]
    for extra in (toep, sink):
        if extra is not None:
            in_specs.append(pl.BlockSpec((1,) + extra.shape[1:], lambda g, b: (g, 0, 0)))
            args.append(extra)
    bias_shape = (n_blk * LANES, cpg * 2 * n_qblk * CHUNK)
    body = functools.partial(_band_body, n_blk=n_blk, n_qblk=n_qblk, cpg=cpg, ones_row=ones_row,
                             has_sink=sink is not None, variants=variants, buffered=buffered)
    return pl.pallas_call(
        body,
        grid=(n_groups, bsz // nb),
        in_specs=in_specs,
        out_specs=pl.BlockSpec((nb, s, qw), lambda g, b: (b, 0, g)),
        out_shape=jax.ShapeDtypeStruct(q.shape, BF16),
        scratch_shapes=[pltpu.VMEM((2,) + bias_shape, F32), pltpu.VMEM((len(variants),) + bias_shape, F32)],
        compiler_params=_params(2),
        name="band_attention",
    )(*args)


def _rope_tables(pos, rot, lane_pattern, from_zero=False):
    half = rot // 2
    inv = jnp.power(ROPE_THETA, -jnp.arange(half, dtype=F32) * 2.0 / rot)
    inv_lane, first, second = [], [], []
    for kind, width in lane_pattern:
        if kind == "rot":
            inv_lane += [inv, inv]
            first += [1.0] * half + [0.0] * half
            second += [0.0] * half + [1.0] * half
        else:
            inv_lane.append(jnp.zeros((width,), F32))
            first += [0.0] * width
            second += [0.0] * width
    inv_lane = jnp.concatenate(inv_lane)[None, :]
    n = pos.shape[0]
    if from_zero and n % CHUNK == 0:
        a_hi = (jnp.arange(n // CHUNK, dtype=F32) * CHUNK)[:, None] * inv_lane
        a_lo = jnp.arange(CHUNK, dtype=F32)[:, None] * inv_lane
        c_hi, s_hi = jnp.cos(a_hi)[:, None, :], jnp.sin(a_hi)[:, None, :]
        c_lo, s_lo = jnp.cos(a_lo)[None], jnp.sin(a_lo)[None]
        cos = (c_hi * c_lo - s_hi * s_lo).reshape(n, LANES)
        sin = (s_hi * c_lo + c_hi * s_lo).reshape(n, LANES)
    else:
        ang = pos.astype(F32)[:, None] * inv_lane
        cos, sin = jnp.cos(ang), jnp.sin(ang)
    return jnp.stack([cos, sin * np.asarray(second, np.float32), -sin * np.asarray(first, np.float32)])


A_ROPE_PATTERN = (("pad", A_NOPE), ("rot", A_ROPE), ("pad", LANES - A_QK))
C_ROPE_PATTERN = (("rot", C_ROT), ("pad", HALF - C_ROT)) * 2


def _prep_ab(pre, post, w_in, q_norm, kv_norm, w_uq, w_ukv, rel_bias, w_out):
    d = w_in.shape[0]
    q_lat, c_kv, k_r, g_a, q_b, k_b, v_b, g_b = jnp.split(
        w_in, [384, 640, 672, 1184, 1696, 2208, 2720], axis=1)
    kr_blk = jnp.concatenate([jnp.zeros((d, A_NOPE), F32), k_r, jnp.zeros((d, LANES - A_QK), F32)], axis=1)
    w_in_p = jnp.concatenate([q_lat, c_kv, kr_blk, g_a, g_b, q_b, k_b, v_b], axis=1).astype(BF16)
    w_uq_p = jnp.pad(w_uq.reshape(A_Q_RANK, A_HEADS, A_QK), ((0, 0), (0, 0), (0, LANES - A_QK)))
    w_uq_p = w_uq_p.reshape(A_Q_RANK, A_HEADS * LANES).astype(BF16)
    ukv = w_ukv.reshape(A_KV_RANK, A_HEADS, A_NOPE + A_V)
    w_uk, w_uv = ukv[..., :A_NOPE], ukv[..., A_NOPE:]
    pad_half = ((0, 0), (0, 0), (0, LANES - A_NOPE))
    w_k = jnp.pad(w_uk, pad_half).reshape(A_KV_RANK, A_HEADS * LANES).astype(BF16)
    w_vt = jnp.pad(w_uv, pad_half).reshape(A_KV_RANK, A_HEADS * LANES).T.astype(BF16)
    wkt = jnp.pad(jnp.transpose(w_uk, (1, 2, 0)), ((0, 0), (0, LANES - A_NOPE), (0, 0))).astype(BF16)
    eye = jnp.eye(A_HEADS, dtype=F32)
    wv_s = (jnp.transpose(w_uv, (1, 0, 2))[:, :, None, :] * eye[:, None, :, None]).reshape(
        A_HEADS, A_KV_RANK, A_WIDTH).astype(BF16)
    sel = (jnp.arange(LANES)[:, None] == A_NOPE + jnp.arange(A_ROPE)[None, :]).astype(BF16)
    rel_bias = rel_bias * LOG2E
    win_p = B_WIN_BLOCKS * LANES
    r0 = win_p - CHUNK
    x_len = r0 + win_p
    n_vec = x_len + CHUNK
    n_hi = r0 + CHUNK - 1 - B_MAX_REL
    n_lo = n_vec - n_hi - (2 * B_MAX_REL + 1)
    vec = jnp.concatenate([jnp.broadcast_to(rel_bias[:, -1:], (B_HEADS, n_hi)), rel_bias[:, ::-1],
                           jnp.broadcast_to(rel_bias[:, :1], (B_HEADS, n_lo))], axis=1)
    skew = jnp.tile(vec, (1, CHUNK))[:, :CHUNK * (n_vec - 1)].reshape(B_HEADS, CHUNK, n_vec - 1)
    toep = skew[:, :, CHUNK - 1:CHUNK - 1 + x_len]
    toep = jnp.transpose(toep.reshape(B_HEADS // 2, 2, CHUNK, x_len), (0, 3, 1, 2)).reshape(B_HEADS // 2, x_len, LANES)
    var_p = tuple(tuple((r0 - c * CHUNK, c - B_PAST_CHUNKS, c)
                        for c in range(B_GROUP_CHUNKS * v, B_GROUP_CHUNKS * (v + 1))) for v in range(B_VARIANTS))
    var_s =(((r0 - B_PAST_CHUNKS * CHUNK, 0, B_PAST_CHUNKS),),)
    return dict(pre=pre[None], w_in=w_in_p, q_norm=q_norm[None], kv_norm=kv_norm[None], w_uq=w_uq_p, w_k=w_k,
                w_vt=w_vt, wkt=wkt, wv_s=wv_s, sel=sel, toep=toep, var_p=var_p, var_s=var_s,
                w_out=w_out.astype(BF16), post=post[None])


def _prep_c(pre, post, w_in, sinks, w_out):
    q, k, v, g = jnp.split(w_in, [1024, 1152, 1280], axis=1)
    k0, k1 = k[:, :C_HEAD_DIM], k[:, C_HEAD_DIM:]
    v0, v1 = v[:, :C_HEAD_DIM], v[:, C_HEAD_DIM:]
    zero = jnp.zeros_like(v0)
    w_in_p = jnp.concatenate([q, k0, k0, k1, k1, v0, zero, v1, zero, g], axis=1).astype(BF16)
    var_p = tuple(tuple((0, c - C_PAST_CHUNKS, c) for c in range(C_GROUP_CHUNKS * v, C_GROUP_CHUNKS * (v + 1)))
                  for v in range(C_VARIANTS))
    var_s = (((0, 1, C_PAST_CHUNKS + 1),),)
    sink_row = jnp.repeat((sinks * LOG2E).reshape(C_KV_HEADS, C_GROUP), CHUNK, axis=1)[:, None, :]
    return dict(pre=pre[None], w_in=w_in_p, var_p=var_p, var_s=var_s, sink_s=sink_row,
                sink_p=jnp.tile(sink_row, (1, 1, C_GROUP_CHUNKS)), w_out=w_out.astype(BF16), post=post[None])


def _dup_heads(x):
    return jnp.concatenate([x[:, :, 0], x[:, :, 0], x[:, :, 1], x[:, :, 1]], axis=-1)


def kernel(x_prompt, x_sample, cache_a_ckv, cache_a_krope, cache_b_k, cache_b_v, cache_c_k, cache_c_v,
           ab_pre_norm, ab_post_norm, ab_w_in, ab_q_norm, ab_kv_norm, ab_w_uq, ab_w_ukv, ab_rel_bias, ab_w_out,
           c_pre_norm, c_post_norm, c_w_in, c_sinks, c_w_out):
    bsz, seq, _ = x_prompt.shape
    dbs, dseq, _ = x_sample.shape
    past = cache_a_ckv.shape[2]
    n_s = dbs * dseq
    pos_p = jnp.arange(seq, dtype=jnp.int32)
    pos_s = jnp.tile(past + jnp.arange(dseq, dtype=jnp.int32), dbs)
    wab = _prep_ab(ab_pre_norm[0], ab_post_norm[0], ab_w_in[0], ab_q_norm[0], ab_kv_norm[0], ab_w_uq[0],
                   ab_w_ukv[0], ab_rel_bias[0], ab_w_out[0])
    wc = _prep_c(c_pre_norm[0], c_post_norm[0], c_w_in[0], c_sinks[0], c_w_out[0])
    b_tail = min(B_PAST_CHUNKS * CHUNK, seq)
    c_tail = min(C_WINDOW, seq)
    tile = 512

    rope_a_p = _rope_tables(pos_p, A_ROPE, A_ROPE_PATTERN, from_zero=True)
    (qa, ka, vt, qb, kb, vbt, sg, c_new_p, kr_new_p_t, kb_tail, vb_tail) = _ab_in_proj(
        x_prompt, rope_a_p, wab, tm=tile, tail_len=b_tail, tkv=tile)
    o_a = _mla_prompt(qa, ka, vt, tq=tile)
    o_b = _band_attention(qb, kb, vbt, wab["toep"], None, variants=wab["var_p"], n_blk=B_WIN_BLOCKS,
                          n_groups=B_HEADS // 2, n_qblk=1, cpg=B_GROUP_CHUNKS, ones_row=False)

    rope_a_s = _rope_tables(pos_s, A_ROPE, A_ROPE_PATTERN)
    xs = x_sample.reshape(1, n_s, D_MODEL)
    (qa_s, _, _, qb_s, _, _, sg_s, c_new_s, kr_new_s_t, kb_s32, vb_s32) = _ab_in_proj(
        xs, rope_a_s, wab, tm=n_s, tail_len=n_s, tkv=n_s)
    kr_new_s_t = jnp.transpose(kr_new_s_t.reshape(A_ROPE, dbs, dseq), (1, 0, 2))
    o_a_s = _mla_sample(qa_s.reshape(dbs, dseq, -1), cache_a_ckv[0], jnp.swapaxes(cache_a_krope[0], 1, 2),
                        c_new_s.reshape(dbs, dseq, -1), kr_new_s_t,
                        wab["wkt"], wab["wv_s"], wab["sel"], tk=512)
    wb = cache_b_k.shape[2]

    def buf_t(cache):
        return jnp.transpose(cache[0], (0, 2, 3, 1)).reshape(dbs, B_WIDTH, wb)

    o_b_s = _band_attention(qb_s.reshape(dbs, dseq, -1),
                            (buf_t(cache_b_k), kb_s32.reshape(dbs, dseq, -1)),
                            (buf_t(cache_b_v), vb_s32.reshape(dbs, dseq, -1)),
                            wab["toep"], None, variants=wab["var_s"], n_blk=B_SAMPLE_BLOCKS, n_groups=B_HEADS // 2,
                            n_qblk=1, cpg=1, ones_row=False)

    rope_c_p = _rope_tables(pos_p, C_ROT, C_ROPE_PATTERN, from_zero=True)
    h1_p, qc, kc, vct, sgc, kc_tail, vc_tail = _c_in_proj(
        [o_a, o_b], sg, wab["w_out"], wab["post"], x_prompt, rope_c_p, wc, tm=tile, tail_len=c_tail)
    o_c = _band_attention(qc, kc, vct, None, wc["sink_p"], variants=wc["var_p"], n_blk=C_WIN_BLOCKS,
                          n_groups=C_KV_HEADS, n_qblk=C_GROUP // 2, cpg=C_GROUP_CHUNKS, ones_row=True)
    h2_p = _out_proj([o_c], sgc, wc["w_out"], wc["post"], h1_p, tm=tile)

    rope_c_s = _rope_tables(pos_s, C_ROT, C_ROPE_PATTERN)
    h1_s, qc_s, kc_s, _, sgc_s, kc_s32, vc_s32 = _c_in_proj(
        [o_a_s.reshape(1, n_s, -1), o_b_s.reshape(1, n_s, -1)], sg_s, wab["w_out"], wab["post"], xs, rope_c_s, wc,
        tm=n_s, tail_len=n_s)
    wcw = cache_c_k.shape[2]
    win_c = C_SAMPLE_BLOCKS * LANES
    n_pad = win_c - wcw - dseq
    kcb = jnp.concatenate([jnp.zeros((dbs, n_pad, C_KV_HEADS * LANES), BF16), _dup_heads(cache_c_k[0]).astype(BF16),
                           kc_s.reshape(dbs, dseq, -1)], axis=1)
    vcb = jnp.concatenate([jnp.zeros((dbs, n_pad, C_KV_HEADS, C_HEAD_DIM), F32), cache_c_v[0],
                           vc_s32.reshape(dbs, dseq, C_KV_HEADS, C_HEAD_DIM)], axis=1)
    vcb_t = jnp.concatenate([jnp.transpose(vcb, (0, 2, 3, 1)), jnp.ones((dbs, C_KV_HEADS, 1, win_c), F32),
                             jnp.zeros((dbs, C_KV_HEADS, LANES - C_HEAD_DIM - 1, win_c), F32)], axis=2)
    vcb_t = jnp.transpose(vcb_t.astype(BF16).reshape(dbs, C_KV_HEADS, LANES, C_SAMPLE_BLOCKS, LANES), (0, 1, 3, 2, 4))
    o_c_s = _band_attention(qc_s.reshape(dbs, dseq, -1), kcb, vcb_t, None, wc["sink_s"], variants=wc["var_s"],
                            n_blk=C_SAMPLE_BLOCKS, n_groups=C_KV_HEADS, n_qblk=C_GROUP // 2, cpg=1, ones_row=True)
    h2_s = _out_proj([o_c_s.reshape(1, n_s, -1)], sgc_s, wc["w_out"], wc["post"], h1_s, tm=n_s)

    def roll_in(buf, new):
        return jnp.concatenate([buf, new], axis=1)[:, -buf.shape[1]:][None]

    return (h2_p, h2_s.reshape(dbs, dseq, D_MODEL),
            c_new_p[None], jnp.swapaxes(kr_new_p_t, 1, 2)[None],
            kb_tail.reshape(1, bsz, b_tail, B_HEADS, B_HEAD_DIM), vb_tail.reshape(1, bsz, b_tail, B_HEADS, B_HEAD_DIM),
            kc_tail.reshape(1, bsz, c_tail, C_KV_HEADS, C_HEAD_DIM), vc_tail.reshape(1, bsz, c_tail, C_KV_HEADS, C_HEAD_DIM),
            c_new_s.reshape(1, dbs, dseq, A_KV_RANK), jnp.swapaxes(kr_new_s_t, 1, 2)[None],
            roll_in(cache_b_k[0], kb_s32.reshape(dbs, dseq, B_HEADS, B_HEAD_DIM)),
            roll_in(cache_b_v[0], vb_s32.reshape(dbs, dseq, B_HEADS, B_HEAD_DIM)),
            roll_in(cache_c_k[0], kc_s32.reshape(dbs, dseq, C_KV_HEADS, C_HEAD_DIM)),
            roll_in(cache_c_v[0], vc_s32.reshape(dbs, dseq, C_KV_HEADS, C_HEAD_DIM)))
```

```python
import functools

import jax
import jax.numpy as jnp
import numpy as np
from jax import lax
from jax.experimental import pallas as pl
from jax.experimental.pallas import tpu as pltpu

F32 = jnp.float32
BF16 = jnp.bfloat16

D_MODEL = 1024
CHUNK = 64
ROPE_THETA = 500000.0
RMS_EPS = 1e-6
NEG_INF = -1e30

A_HEADS = 8
A_NOPE = 64
A_ROPE = 32
A_QK = A_NOPE + A_ROPE
A_V = 64
A_Q_RANK = 384
A_KV_RANK = 256
A_WIDTH = A_HEADS * A_V
LOG2E = 1.4426950408889634
A_SCALE = A_QK ** -0.5 * LOG2E

B_HEADS = 8
B_HEAD_DIM = 64
B_WIDTH = B_HEADS * B_HEAD_DIM
B_PAST_CHUNKS = 8
B_MAX_REL = 128
B_SCALE = B_HEAD_DIM ** -0.5 * LOG2E

C_HEADS = 16
C_KV_HEADS = 2
C_GROUP = C_HEADS // C_KV_HEADS
C_HEAD_DIM = 64
C_WIDTH = C_HEADS * C_HEAD_DIM
C_WINDOW = 128
C_PAST_CHUNKS = C_WINDOW // CHUNK
C_ROT = C_HEAD_DIM // 4
C_SCALE = C_HEAD_DIM ** -0.5 * LOG2E

LANES = 128
HALF = LANES // 2
BF16_SUBLANES = 16
VMEM_LIMIT = 56 * 1024 * 1024
A_MASK_CHUNKS = 8
assert A_QK + A_MASK_CHUNKS <= LANES
A_PIPE_UNROLL = 8
BAND_PIPE_UNROLL = 8
B_GROUP_CHUNKS = 4
C_GROUP_CHUNKS = 2
B_WIN_BLOCKS = (B_PAST_CHUNKS + B_GROUP_CHUNKS) * CHUNK // LANES
C_WIN_BLOCKS = (C_PAST_CHUNKS + C_GROUP_CHUNKS) * CHUNK // LANES
B_VARIANTS = B_PAST_CHUNKS // B_GROUP_CHUNKS + 1
C_VARIANTS = C_PAST_CHUNKS // C_GROUP_CHUNKS + 1
B_SAMPLE_BLOCKS = (B_PAST_CHUNKS + 2) * CHUNK // LANES
C_SAMPLE_BLOCKS = (C_PAST_CHUNKS + 2) * CHUNK // LANES

AB_Q0, AB_C0, AB_KR0, AB_G0, AB_QB0, AB_KB0, AB_VB0, AB_NZ = 0, 384, 640, 768, 1792, 2304, 2816, 3328
C_Q0, C_K0, C_V0, C_G0, C_NZ = 0, 1024, 1280, 1536, 2560


def _params(n_axes):
    return pltpu.CompilerParams(dimension_semantics=("arbitrary",) * n_axes, vmem_limit_bytes=VMEM_LIMIT)


def _rms(x, g):
    return x * lax.rsqrt(jnp.mean(x * x, axis=-1, keepdims=True) + RMS_EPS) * g


def _rope_block(blk, rope_ref, shift):
    return (blk * rope_ref[0] + pltpu.roll(blk, shift, 1) * rope_ref[1]
            + pltpu.roll(blk, LANES - shift, 1) * rope_ref[2])


def _tail_spec(n_tiles, tm, tail_len, width):
    tb = min(tail_len, tm)
    n_blk = tail_len // tb
    return tb, pl.BlockSpec((1, tb, width), lambda b, i: (b, jnp.maximum(i - (n_tiles - n_blk), 0), 0))


def _ab_in_body(x_ref, pre_ref, w_ref, qn_ref, kvn_ref, wuq_ref, wk_ref, wvt_ref, rope_ref,
                qa_ref, ka_ref, vt_ref, qb_ref, kb_ref, vbt_ref, sg_ref, c_ref, kr_ref, kb_tail_ref, vb_tail_ref):
    tm = x_ref.shape[1]
    tb = kb_tail_ref.shape[1]
    xn = _rms(x_ref[0], pre_ref[...]).astype(BF16)
    z = jnp.dot(xn, w_ref[...], preferred_element_type=F32)

    qn = _rms(z[:, AB_Q0:AB_C0], qn_ref[...]).astype(BF16)
    qa = jnp.dot(qn, wuq_ref[...], preferred_element_type=F32) * A_SCALE
    for h in range(A_HEADS):
        blk = slice(h * LANES, (h + 1) * LANES)
        qa_ref[0, :, blk] = _rope_block(qa[:, blk], rope_ref, A_ROPE // 2).astype(BF16)

    c_new = _rms(z[:, AB_C0:AB_KR0], kvn_ref[...])
    c_ref[0] = c_new
    cb = c_new.astype(BF16)
    krot = _rope_block(z[:, AB_KR0:AB_G0], rope_ref, A_ROPE // 2)
    kr_ref[0] = krot.T[A_NOPE:A_NOPE + A_ROPE, :]
    kn = jnp.dot(cb, wk_ref[...], preferred_element_type=F32)
    row = pl.program_id(1) * tm + lax.broadcasted_iota(jnp.int32, (tm, LANES), 0)
    lane = lax.broadcasted_iota(jnp.int32, (tm, LANES), 1)
    k_shared = krot + jnp.where(lane - A_QK == (row // CHUNK) % A_MASK_CHUNKS, 1.0, 0.0)
    for h in range(A_HEADS):
        blk = slice(h * LANES, (h + 1) * LANES)
        ka_ref[0, :, blk] = (kn[:, blk] + k_shared).astype(BF16)
    vt = jnp.dot(wvt_ref[...], c_new.T.astype(BF16), preferred_element_type=F32)
    row = lax.broadcasted_iota(jnp.int32, vt.shape, 0)
    vt_ref[0, 0] = jnp.where((row & (LANES - 1)) == A_V, 1.0, vt).astype(BF16)

    g = z[:, AB_G0:AB_QB0]
    sg_ref[0] = (g * jax.nn.sigmoid(g)).astype(BF16)
    qb_ref[0] = (z[:, AB_QB0:AB_KB0] * B_SCALE).astype(BF16)
    kb = z[:, AB_KB0:AB_VB0]
    vb = z[:, AB_VB0:AB_NZ]
    kb_ref[0] = kb.astype(BF16)
    vbt = vb.T.astype(BF16)
    for hp in range(B_HEADS // 2):
        for j in range(tm // LANES):
            vbt_ref[0, hp, j] = vbt[hp * LANES:(hp + 1) * LANES, j * LANES:(j + 1) * LANES]
    kb_tail_ref[0] = kb[tm - tb:, :]
    vb_tail_ref[0] = vb[tm - tb:, :]


def _ab_in_proj(x, rope, w, tm, tail_len, tkv):
    bsz, s, _ = x.shape
    nt = s // tm
    per_kv = tkv // tm
    tb, tail_spec = _tail_spec(nt, tm, tail_len, B_WIDTH)

    def full(a):
        return pl.BlockSpec(a.shape, lambda b, i: (0,) * a.ndim, pipeline_mode=pl.Buffered(1))

    def rows(width):
        return pl.BlockSpec((1, tm, width), lambda b, i: (b, i, 0))

    weights = (w["pre"], w["w_in"], w["q_norm"], w["kv_norm"], w["w_uq"], w["w_k"], w["w_vt"])
    vt_spec = pl.BlockSpec((1, 1, A_HEADS * LANES, tm), lambda b, i: (b, i // per_kv, 0, i % per_kv))
    out_shape = (
        jax.ShapeDtypeStruct((bsz, s, A_HEADS * LANES), BF16),
        jax.ShapeDtypeStruct((bsz, s, A_HEADS * LANES), BF16),
        jax.ShapeDtypeStruct((bsz, s // tkv, A_HEADS * LANES, tkv), BF16),
        jax.ShapeDtypeStruct((bsz, s, B_WIDTH), BF16),
        jax.ShapeDtypeStruct((bsz, s, B_WIDTH), BF16),
        jax.ShapeDtypeStruct((bsz, B_HEADS // 2, s // LANES, LANES, LANES), BF16),
        jax.ShapeDtypeStruct((bsz, s, A_WIDTH + B_WIDTH), BF16),
        jax.ShapeDtypeStruct((bsz, s, A_KV_RANK), F32),
        jax.ShapeDtypeStruct((bsz, A_ROPE, s), F32),
        jax.ShapeDtypeStruct((bsz, tail_len, B_WIDTH), F32),
        jax.ShapeDtypeStruct((bsz, tail_len, B_WIDTH), F32),
    )
    vbt_spec = pl.BlockSpec((1, B_HEADS // 2, tm // LANES, LANES, LANES), lambda b, i: (b, 0, i, 0, 0))
    out_specs = (rows(1024), rows(1024), vt_spec, rows(512), rows(512), vbt_spec, rows(1024),
                 rows(A_KV_RANK), pl.BlockSpec((1, A_ROPE, tm), lambda b, i: (b, 0, i)), tail_spec, tail_spec)
    return pl.pallas_call(
        _ab_in_body,
        grid=(bsz, nt),
        in_specs=[rows(D_MODEL)] + [full(a) for a in weights]
        + [pl.BlockSpec((3, tm, LANES), lambda b, i: (0, i, 0))],
        out_specs=out_specs,
        out_shape=out_shape,
        compiler_params=_params(2),
        name="ab_in_proj",
    )(x, *weights, rope)


def _mix_out(o_refs, sg_ref, w_ref, g_ref, h_ref):
    o = jnp.concatenate([r[0].astype(F32) for r in o_refs], axis=-1) if len(o_refs) > 1 else o_refs[0][0].astype(F32)
    mixed = (o * sg_ref[0].astype(F32)).astype(BF16)
    y = jnp.dot(mixed, w_ref[...], preferred_element_type=F32)
    return h_ref[0] + _rms(y, g_ref[...])


def _c_in_body(*refs):
    n_o = len(refs) - 14
    o_refs = refs[:n_o]
    (sg0_ref, w0_ref, post0_ref, x_ref, pre_ref, w_ref, rope_ref,
     h_ref, q_ref, k_ref, vt_ref, sg_ref, k_tail_ref, v_tail_ref) = refs[n_o:]
    tm = x_ref.shape[1]
    tb = k_tail_ref.shape[1]
    h = _mix_out(o_refs, sg0_ref, w0_ref, post0_ref, x_ref)
    h_ref[0] = h
    xn = _rms(h, pre_ref[...]).astype(BF16)
    z = jnp.dot(xn, w_ref[...], preferred_element_type=F32)
    half_rot = C_ROT // 2
    for j in range(C_WIDTH // LANES):
        blk = slice(C_Q0 + j * LANES, C_Q0 + (j + 1) * LANES)
        q_ref[0, :, j * LANES:(j + 1) * LANES] = (_rope_block(z[:, blk], rope_ref, half_rot) * C_SCALE).astype(BF16)
    kd = [_rope_block(z[:, C_K0 + j * LANES:C_K0 + (j + 1) * LANES], rope_ref, half_rot) for j in range(C_KV_HEADS)]
    vz = [z[:, C_V0 + j * LANES:C_V0 + (j + 1) * LANES] for j in range(C_KV_HEADS)]
    row = lax.broadcasted_iota(jnp.int32, (LANES, tm), 0)
    for j in range(C_KV_HEADS):
        k_ref[0, :, j * LANES:(j + 1) * LANES] = kd[j].astype(BF16)
        vt = jnp.where(row == C_HEAD_DIM, 1.0, vz[j].T).astype(BF16)
        for i in range(tm // LANES):
            vt_ref[0, j, i] = vt[:, i * LANES:(i + 1) * LANES]
    g = z[:, C_G0:C_NZ]
    sg_ref[0] = (g * jax.nn.sigmoid(g)).astype(BF16)
    lo = lax.broadcasted_iota(jnp.int32, (1, LANES), 1) < HALF
    k_tail_ref[0] = jnp.where(lo, kd[0], kd[1])[tm - tb:, :]
    v_tail_ref[0] = (vz[0] + pltpu.roll(vz[1], HALF, 1))[tm - tb:, :]


def _c_in_proj(o_parts, sg0, w_out0, post0, x, rope, w, tm, tail_len):
    bsz, s, _ = x.shape
    nt = s // tm
    tb, tail_spec = _tail_spec(nt, tm, tail_len, LANES)

    def full(a):
        return pl.BlockSpec(a.shape, lambda b, i: (0,) * a.ndim)

    def rows(width):
        return pl.BlockSpec((1, tm, width), lambda b, i: (b, i, 0))

    out_shape = (
        jax.ShapeDtypeStruct((bsz, s, D_MODEL), F32),
        jax.ShapeDtypeStruct((bsz, s, C_WIDTH), BF16),
        jax.ShapeDtypeStruct((bsz, s, C_KV_HEADS * LANES), BF16),
        jax.ShapeDtypeStruct((bsz, C_KV_HEADS, s // LANES, LANES, LANES), BF16),
        jax.ShapeDtypeStruct((bsz, s, C_WIDTH), BF16),
        jax.ShapeDtypeStruct((bsz, tail_len, LANES), F32),
        jax.ShapeDtypeStruct((bsz, tail_len, LANES), F32),
    )
    return pl.pallas_call(
        _c_in_body,
        grid=(bsz, nt),
        in_specs=[rows(o.shape[-1]) for o in o_parts]
        + [rows(D_MODEL), full(w_out0), full(post0), rows(D_MODEL), full(w["pre"]), full(w["w_in"]),
           pl.BlockSpec((3, tm, LANES), lambda b, i: (0, i, 0))],
        out_specs=(rows(D_MODEL), rows(C_WIDTH), rows(256),
                   pl.BlockSpec((1, C_KV_HEADS, tm // LANES, LANES, LANES), lambda b, i: (b, 0, i, 0, 0)),
                   rows(C_WIDTH), tail_spec, tail_spec),
        out_shape=out_shape,
        compiler_params=_params(2),
        name="c_in_proj",
    )(*o_parts, sg0, w_out0, post0, x, w["pre"], w["w_in"], rope)


def _out_body(*refs):
    o_refs, (sg_ref, w_ref, g_ref, h_ref, out_ref) = refs[:-5], refs[-5:]
    out_ref[0] = _mix_out(o_refs, sg_ref, w_ref, g_ref, h_ref)


def _out_proj(o_parts, sg, w_out, post_g, h, tm):
    bsz, s, _ = h.shape

    def rows(width):
        return pl.BlockSpec((1, tm, width), lambda b, i: (b, i, 0))

    def full(a):
        return pl.BlockSpec(a.shape, lambda b, i: (0,) * a.ndim)

    return pl.pallas_call(
        _out_body,
        grid=(bsz, s // tm),
        in_specs=[rows(o.shape[-1]) for o in o_parts] + [rows(D_MODEL), full(w_out), full(post_g), rows(D_MODEL)],
        out_specs=rows(D_MODEL),
        out_shape=jax.ShapeDtypeStruct(h.shape, F32),
        compiler_params=_params(2),
        name="out_proj",
    )(*o_parts, sg, w_out, post_g, h)


def _mla_prompt_body(q_ref, k_ref, vt_ref, qmask_ref, o_ref, m_sc, acc_sc, s_sc, *, tq):
    n_tiles = q_ref.shape[1] // tq
    n_pairs = n_tiles * (n_tiles + 1) // 2
    assert A_PIPE_UNROLL % 2 == 0 and n_pairs % A_PIPE_UNROLL == 0 and n_pairs >= 2 * A_PIPE_UNROLL
    m_sc[...] = jnp.full(m_sc.shape, NEG_INF, F32)
    acc_sc[...] = jnp.zeros(acc_sc.shape, F32)
    contract_last = (((1,), (1,)), ((), ()))

    def scores(qi, j, slot, heads=(0, 1)):
        q_start = pl.multiple_of(qi * tq, tq)
        k_start = pl.multiple_of(j * tq, tq)
        qmask = jnp.where(j == qi, qmask_ref[...], jnp.zeros_like(qmask_ref[...]))
        for hh in heads:
            blk = slice(hh * LANES, (hh + 1) * LANES)
            q = q_ref[0, pl.ds(q_start, tq), blk] + qmask
            s_sc[slot, hh] = lax.dot_general(k_ref[0, pl.ds(k_start, tq), blk], q, contract_last,
                                             preferred_element_type=F32)

    def consume(qi, j, slot, heads=(0, 1)):
        for hh in heads:
            s = s_sc[slot, hh]
            vt = vt_ref[0, j, hh * LANES:(hh + 1) * LANES, :]
            m_old = m_sc[qi, hh]
            m_new = jnp.maximum(m_old, jnp.max(s, axis=0, keepdims=True))
            p = jnp.exp2(s - m_new).astype(BF16)
            acc_sc[qi, hh] = jnp.exp2(m_old - m_new) * acc_sc[qi, hh] + jnp.dot(vt, p, preferred_element_type=F32)
            m_sc[qi, hh] = m_new

    def advance(qi, j):
        last = j == qi
        return jnp.where(last, qi + 1, qi), jnp.where(last, 0, j + 1)

    def stages(cur, count):
        for st in range(count):
            nxt = advance(*cur)
            for hh in range(2):
                scores(*nxt, 1 - (st & 1), heads=(hh,))
                consume(*cur, st & 1, heads=(hh,))
            cur = nxt
        return cur

    zero = jnp.int32(0)
    scores(zero, zero, 0)
    cur = lax.fori_loop(0, n_pairs // A_PIPE_UNROLL - 1, lambda i, c: stages(c, A_PIPE_UNROLL), (zero, zero))
    cur = stages(cur, A_PIPE_UNROLL - 1)
    consume(*cur, (A_PIPE_UNROLL - 1) & 1)

    lo = lax.broadcasted_iota(jnp.int32, (1, LANES), 1) < HALF

    def write_tile(qi, carry):
        outs = []
        for hh in range(2):
            acc = acc_sc[qi, hh]
            outs.append((acc / acc[A_V:A_V + 1, :]).T)
        o_ref[0, pl.ds(pl.multiple_of(qi * tq, tq), tq), :] = jnp.where(
            lo, outs[0], pltpu.roll(outs[1], HALF, 1)).astype(o_ref.dtype)
        return carry

    lax.fori_loop(0, n_tiles, write_tile, 0)


def _mla_prompt(qa, ka, vt, tq):
    bsz, s, _ = qa.shape
    pairs = A_HEADS // 2
    n_tiles = s // tq
    assert vt.shape[3] == tq and tq == A_MASK_CHUNKS * CHUNK
    q_chunk = np.arange(tq)[:, None] // CHUNK
    lane_chunk = np.arange(LANES)[None, :] - A_QK
    qmask = np.where((lane_chunk > q_chunk) & (lane_chunk < A_MASK_CHUNKS), NEG_INF, 0.0).astype(np.float32)
    return pl.pallas_call(
        functools.partial(_mla_prompt_body, tq=tq),
        grid=(bsz, pairs),
        in_specs=[pl.BlockSpec((1, s, 2 * LANES), lambda b, h: (b, 0, h)),
                  pl.BlockSpec((1, s, 2 * LANES), lambda b, h: (b, 0, h)),
                  pl.BlockSpec((1, n_tiles, 2 * LANES, tq), lambda b, h: (b, 0, h, 0)),
                  pl.BlockSpec((tq, LANES), lambda b, h: (0, 0))],
        out_specs=pl.BlockSpec((1, s, LANES), lambda b, h: (b, 0, h)),
        out_shape=jax.ShapeDtypeStruct((bsz, s, A_WIDTH), BF16),
        scratch_shapes=[pltpu.VMEM((n_tiles, 2, 1, tq), F32), pltpu.VMEM((n_tiles, 2, LANES, tq), F32),
                        pltpu.VMEM((2, 2, tq, tq), F32)],
        compiler_params=_params(2),
        name="mla_prompt",
    )(qa, ka, vt, jnp.asarray(qmask, BF16))


def _mla_sample_body(q_ref, cc_ref, ckr_ref, cn_ref, krn_ref, wkt_ref, wv_ref, sel_ref, o_ref,
                     qabs_sc, qr_sc, m_sc, acc_sc, *, tk):
    t = q_ref.shape[1]
    past = cc_ref.shape[1]
    for h in range(A_HEADS):
        qh = q_ref[0, :, h * LANES:(h + 1) * LANES]
        rows = slice(h * t, (h + 1) * t)
        qabs_sc[rows, :] = jnp.dot(qh, wkt_ref[h], preferred_element_type=F32).astype(BF16)
        qr_sc[rows, :] = jnp.dot(qh, sel_ref[...], preferred_element_type=F32).astype(BF16)
    m_sc[...] = jnp.full(m_sc.shape, NEG_INF, F32)
    acc_sc[...] = jnp.zeros(acc_sc.shape, F32)
    contract_last = (((1,), (1,)), ((), ()))

    def scores(c_t, kr_tt):
        return (lax.dot_general(c_t.astype(BF16), qabs_sc[...], contract_last, preferred_element_type=F32)
                + lax.dot_general(kr_tt.T.astype(BF16), qr_sc[...], contract_last, preferred_element_type=F32))

    def update(s, c_t):
        n_keys = c_t.shape[0]
        m_old = m_sc[...]
        m_new = jnp.maximum(m_old, jnp.max(s, axis=0, keepdims=True))
        p = jnp.exp2(s - m_new).astype(BF16)
        ct = jnp.concatenate([c_t.T.astype(BF16), jnp.ones((BF16_SUBLANES, n_keys), BF16)], axis=0)
        acc_sc[...] = jnp.exp2(m_old - m_new) * acc_sc[...] + jnp.dot(ct, p, preferred_element_type=F32)
        m_sc[...] = m_new

    def tile(j):
        return cc_ref[0, j * tk:(j + 1) * tk, :], ckr_ref[0, :, j * tk:(j + 1) * tk]

    n_tiles = past // tk
    cur = tile(0)
    s = scores(*cur)
    for j in range(n_tiles):
        nxt = tile(j + 1) if j + 1 < n_tiles else (cn_ref[0], krn_ref[0])
        s_next = scores(*nxt)
        update(s, cur[0])
        cur, s = nxt, s_next
    update(s, cur[0])

    acc = acc_sc[...]
    o_lat = (acc[:A_KV_RANK] / acc[A_KV_RANK:A_KV_RANK + 1]).T.astype(BF16)
    out = jnp.zeros((t, A_WIDTH), F32)
    for h in range(A_HEADS):
        out = out + jnp.dot(o_lat[h * t:(h + 1) * t, :], wv_ref[h], preferred_element_type=F32)
    o_ref[0] = out.astype(o_ref.dtype)


def _mla_sample(qa, cache_c, cache_kr, c_new, kr_new, wkt, wv, sel, tk):
    bsz, t, _ = qa.shape
    past = cache_c.shape[1]

    def per_b(shape):
        return pl.BlockSpec((1,) + shape, lambda b: (b, 0, 0))

    def full(a):
        return pl.BlockSpec(a.shape, lambda b: (0,) * a.ndim)

    rows = A_HEADS * t
    return pl.pallas_call(
        functools.partial(_mla_sample_body, tk=tk),
        grid=(bsz,),
        in_specs=[per_b((t, A_HEADS * LANES)), per_b((past, A_KV_RANK)), per_b((A_ROPE, past)),
                  per_b((t, A_KV_RANK)), per_b((A_ROPE, t)), full(wkt), full(wv), full(sel)],
        out_specs=per_b((t, A_WIDTH)),
        out_shape=jax.ShapeDtypeStruct((bsz, t, A_WIDTH), BF16),
        scratch_shapes=[pltpu.VMEM((rows, A_KV_RANK), BF16), pltpu.VMEM((rows, A_ROPE), BF16),
                        pltpu.VMEM((1, rows), F32), pltpu.VMEM((A_KV_RANK + BF16_SUBLANES, rows), F32)],
        compiler_params=_params(1),
        name="mla_sample",
    )(qa, cache_c, cache_kr, c_new, kr_new, wkt, wv, sel)


def _band_body(*refs, n_blk, n_qblk, cpg, ones_row, has_sink, variants, buffered=False):
    refs = list(refs)
    n_in = 5 if buffered else 3
    q_ref = refs[0]
    o_ref, s_sc, bias_sc = refs[-3:]
    extra = refs[n_in:-3]
    sink_ref = extra.pop() if has_sink else None
    toep_ref = extra.pop() if extra else None
    win = n_blk * LANES
    rows_g = cpg * CHUNK
    n_var = len(variants)
    qpc = 2 * n_qblk * CHUNK
    lo = lax.broadcasted_iota(jnp.int32, (CHUNK, LANES), 1) < HALF
    contract_last = (((1,), (1,)), ((), ()))

    if toep_ref is not None:
        key_chunk = lax.broadcasted_iota(jnp.int32, (win, LANES), 0) // CHUNK
        for v, chunks in enumerate(variants):
            for cc, (x0, lo_chunk, hi_chunk) in enumerate(chunks):
                tile = jnp.where((key_chunk >= lo_chunk) & (key_chunk <= hi_chunk), toep_ref[0, x0:x0 + win, :], NEG_INF)
                for j in range(qpc // LANES):
                    bias_sc[v, :, cc * qpc + j * LANES:cc * qpc + (j + 1) * LANES] = tile

    def store_scores(slot, variant, s):
        if toep_ref is not None:
            s_sc[slot] = s + bias_sc[variant]
            return
        s_sc[slot] = s
        for cc, (_, lo_chunk, hi_chunk) in enumerate(variants[variant]):
            cols = slice(cc * qpc, (cc + 1) * qpc)
            for r0, r1 in ((0, lo_chunk * CHUNK), ((hi_chunk + 1) * CHUNK, win)):
                if r1 > max(r0, 0):
                    s_sc[slot, max(r0, 0):r1, cols] = jnp.full((r1 - max(r0, 0), qpc), NEG_INF, F32)

    def aligned(x, n):
        return x if isinstance(x, int) else pl.multiple_of(x, n)

    def buffered_window(buf_ref, new_ref, bi, transposed):
        buf_t, new = buf_ref[bi], new_ref[bi]
        pad = jnp.zeros((win - buf_t.shape[1] - new.shape[0], LANES), F32)
        if transposed:
            return jnp.concatenate([buf_t, jnp.concatenate([new, pad], axis=0).T], axis=1)
        return jnp.concatenate([buf_t.T, new, pad], axis=0)

    def scores(a, wb, variant, slot, bi=0):
        pieces = []
        for cc in range(cpg):
            rows = pl.ds(aligned(a * rows_g + cc * CHUNK, CHUNK), CHUNK)
            for r in range(n_qblk):
                qblk = q_ref[bi, rows, r * LANES:(r + 1) * LANES]
                zero = jnp.zeros_like(qblk)
                pieces += [jnp.where(lo, qblk, zero), jnp.where(lo, zero, qblk)]
        qs = jnp.concatenate(pieces, axis=0)
        if buffered:
            kw = buffered_window(refs[1], refs[2], bi, False).astype(BF16)
        else:
            kw = refs[1][bi, pl.ds(aligned(wb * LANES, LANES), win), :]
        store_scores(slot, variant, lax.dot_general(kw, qs, contract_last, preferred_element_type=F32))

    def finish(a, wb, slot, bi=0):
        s = s_sc[slot]
        m = jnp.max(s, axis=0, keepdims=True)
        if has_sink:
            sink = sink_ref[0]
            m = jnp.maximum(m, sink)
        p = jnp.exp2(s - m)
        if buffered:
            vt = buffered_window(refs[3], refs[4], bi, True).astype(BF16)
        else:
            vt = jnp.concatenate([refs[2][bi, 0, wb + i] for i in range(n_blk)], axis=1)
        if not ones_row:
            vt = jnp.concatenate([vt, jnp.ones((BF16_SUBLANES, win), BF16)], axis=0)
        o = jnp.dot(vt, p.astype(BF16), preferred_element_type=F32)
        l = o[CHUNK:CHUNK + 1, :] if ones_row else o[LANES:LANES + 1, :]
        o = o[:LANES]
        if has_sink:
            l = l + jnp.exp2(sink - m)
        o = (o / l).T
        idx = 0
        for cc in range(cpg):
            rows = pl.ds(aligned(a * rows_g + cc * CHUNK, CHUNK), CHUNK)
            for r in range(n_qblk):
                top = o[idx * CHUNK:(idx + 1) * CHUNK]
                bot = o[(idx + 1) * CHUNK:(idx + 2) * CHUNK]
                if ones_row:
                    bot = pltpu.roll(bot, HALF, 1)
                o_ref[bi, rows, r * LANES:(r + 1) * LANES] = jnp.where(lo, top, bot).astype(o_ref.dtype)
                idx += 2

    n_total = q_ref.shape[1] // rows_g
    if n_total == 1:
        n_rows = q_ref.shape[0]
        scores(0, 0, 0, 0, 0)
        for bi in range(n_rows):
            if bi + 1 < n_rows:
                scores(0, 0, 0, (bi + 1) & 1, bi + 1)
            finish(0, 0, bi & 1, bi)
        return
    g_blk = rows_g // LANES
    n_lead = n_var - 1 + (n_var - 1) % 2
    assert (n_var - 1) * g_blk >= n_blk - g_blk and n_total - n_lead >= 2 and BAND_PIPE_UNROLL % 2 == 0

    def window_block(a):
        wb = (a + 1) * g_blk - n_blk
        return max(wb, 0) if isinstance(a, int) else wb

    def stage(a, slot):
        nxt = a + 1
        scores(nxt, window_block(nxt), min(nxt, n_var - 1) if isinstance(nxt, int) else n_var - 1, 1 - slot)
        finish(a, window_block(a), slot)

    scores(0, window_block(0), 0, 0)
    for a in range(n_lead):
        stage(a, a & 1)

    def trip(i, carry):
        for st in range(BAND_PIPE_UNROLL):
            stage(n_lead + BAND_PIPE_UNROLL * i + st, st & 1)
        return carry

    n_stages = n_total - 1 - n_lead
    lax.fori_loop(0, n_stages // BAND_PIPE_UNROLL, trip, 0)
    for a in range(n_total - 1 - n_stages % BAND_PIPE_UNROLL, n_total - 1):
        stage(a, (a - n_lead) & 1)
    finish(n_total - 1, window_block(n_total - 1), (n_total - 1 - n_lead) & 1)


def _band_attention(q, keys, values, toep, sink, *, variants, n_blk, n_groups, n_qblk, cpg, ones_row):
    bsz, s, _ = q.shape
    qw = n_qblk * LANES
    buffered = isinstance(keys, tuple)
    nb = bsz if s == cpg * CHUNK else 1

    def per_group(a):
        return pl.BlockSpec((nb, a.shape[1], LANES), lambda g, b: (b, 0, g))

    in_specs = [pl.BlockSpec((nb, s, qw), lambda g, b: (b, 0, g))]
    if buffered:
        args = [q, *keys, *values]
        for buf_t, new in (keys, values):
            in_specs += [pl.BlockSpec((nb, LANES, buf_t.shape[2]), lambda g, b: (b, g, 0)), per_group(new)]
    else:
        args = [q, keys, values]
        in_specs += [per_group(keys), pl.BlockSpec((nb, 1) + values.shape[2:], lambda g, b: (b, g, 0, 0, 0))]
    for extra in (toep, sink):
        if extra is not None:
            in_specs.append(pl.BlockSpec((1,) + extra.shape[1:], lambda g, b: (g, 0, 0)))
            args.append(extra)
    bias_shape = (n_blk * LANES, cpg * 2 * n_qblk * CHUNK)
    body = functools.partial(_band_body, n_blk=n_blk, n_qblk=n_qblk, cpg=cpg, ones_row=ones_row,
                             has_sink=sink is not None, variants=variants, buffered=buffered)
    return pl.pallas_call(
        body,
        grid=(n_groups, bsz // nb),
        in_specs=in_specs,
        out_specs=pl.BlockSpec((nb, s, qw), lambda g, b: (b, 0, g)),
        out_shape=jax.ShapeDtypeStruct(q.shape, BF16),
        scratch_shapes=[pltpu.VMEM((2,) + bias_shape, F32), pltpu.VMEM((len(variants),) + bias_shape, F32)],
        compiler_params=_params(2),
        name="band_attention",
    )(*args)


def _rope_tables(pos, rot, lane_pattern, from_zero=False):
    half = rot // 2
    inv = jnp.power(ROPE_THETA, -jnp.arange(half, dtype=F32) * 2.0 / rot)
    inv_lane, first, second = [], [], []
    for kind, width in lane_pattern:
        if kind == "rot":
            inv_lane += [inv, inv]
            first += [1.0] * half + [0.0] * half
            second += [0.0] * half + [1.0] * half
        else:
            inv_lane.append(jnp.zeros((width,), F32))
            first += [0.0] * width
            second += [0.0] * width
    inv_lane = jnp.concatenate(inv_lane)[None, :]
    n = pos.shape[0]
    if from_zero and n % CHUNK == 0:
        a_hi = (jnp.arange(n // CHUNK, dtype=F32) * CHUNK)[:, None] * inv_lane
        a_lo = jnp.arange(CHUNK, dtype=F32)[:, None] * inv_lane
        c_hi, s_hi = jnp.cos(a_hi)[:, None, :], jnp.sin(a_hi)[:, None, :]
        c_lo, s_lo = jnp.cos(a_lo)[None], jnp.sin(a_lo)[None]
        cos = (c_hi * c_lo - s_hi * s_lo).reshape(n, LANES)
        sin = (s_hi * c_lo + c_hi * s_lo).reshape(n, LANES)
    else:
        ang = pos.astype(F32)[:, None] * inv_lane
        cos, sin = jnp.cos(ang), jnp.sin(ang)
    return jnp.stack([cos, sin * np.asarray(second, np.float32), -sin * np.asarray(first, np.float32)])


A_ROPE_PATTERN = (("pad", A_NOPE), ("rot", A_ROPE), ("pad", LANES - A_QK))
C_ROPE_PATTERN = (("rot", C_ROT), ("pad", HALF - C_ROT)) * 2


def _prep_ab(pre, post, w_in, q_norm, kv_norm, w_uq, w_ukv, rel_bias, w_out):
    d = w_in.shape[0]
    q_lat, c_kv, k_r, g_a, q_b, k_b, v_b, g_b = jnp.split(
        w_in, [384, 640, 672, 1184, 1696, 2208, 2720], axis=1)
    kr_blk = jnp.concatenate([jnp.zeros((d, A_NOPE), F32), k_r, jnp.zeros((d, LANES - A_QK), F32)], axis=1)
    w_in_p = jnp.concatenate([q_lat, c_kv, kr_blk, g_a, g_b, q_b, k_b, v_b], axis=1).astype(BF16)
    w_uq_p = jnp.pad(w_uq.reshape(A_Q_RANK, A_HEADS, A_QK), ((0, 0), (0, 0), (0, LANES - A_QK)))
    w_uq_p = w_uq_p.reshape(A_Q_RANK, A_HEADS * LANES).astype(BF16)
    ukv = w_ukv.reshape(A_KV_RANK, A_HEADS, A_NOPE + A_V)
    w_uk, w_uv = ukv[..., :A_NOPE], ukv[..., A_NOPE:]
    pad_half = ((0, 0), (0, 0), (0, LANES - A_NOPE))
    w_k = jnp.pad(w_uk, pad_half).reshape(A_KV_RANK, A_HEADS * LANES).astype(BF16)
    w_vt = jnp.pad(w_uv, pad_half).reshape(A_KV_RANK, A_HEADS * LANES).T.astype(BF16)
    wkt = jnp.pad(jnp.transpose(w_uk, (1, 2, 0)), ((0, 0), (0, LANES - A_NOPE), (0, 0))).astype(BF16)
    eye = jnp.eye(A_HEADS, dtype=F32)
    wv_s = (jnp.transpose(w_uv, (1, 0, 2))[:, :, None, :] * eye[:, None, :, None]).reshape(
        A_HEADS, A_KV_RANK, A_WIDTH).astype(BF16)
    sel = (jnp.arange(LANES)[:, None] == A_NOPE + jnp.arange(A_ROPE)[None, :]).astype(BF16)
    rel_bias = rel_bias * LOG2E
    win_p = B_WIN_BLOCKS * LANES
    r0 = win_p - CHUNK
    x_len = r0 + win_p
    n_vec = x_len + CHUNK
    n_hi = r0 + CHUNK - 1 - B_MAX_REL
    n_lo = n_vec - n_hi - (2 * B_MAX_REL + 1)
    vec = jnp.concatenate([jnp.broadcast_to(rel_bias[:, -1:], (B_HEADS, n_hi)), rel_bias[:, ::-1],
                           jnp.broadcast_to(rel_bias[:, :1], (B_HEADS, n_lo))], axis=1)
    skew = jnp.tile(vec, (1, CHUNK))[:, :CHUNK * (n_vec - 1)].reshape(B_HEADS, CHUNK, n_vec - 1)
    toep = skew[:, :, CHUNK - 1:CHUNK - 1 + x_len]
    toep = jnp.transpose(toep.reshape(B_HEADS // 2, 2, CHUNK, x_len), (0, 3, 1, 2)).reshape(B_HEADS // 2, x_len, LANES)
    var_p = tuple(tuple((r0 - c * CHUNK, c - B_PAST_CHUNKS, c)
                        for c in range(B_GROUP_CHUNKS * v, B_GROUP_CHUNKS * (v + 1))) for v in range(B_VARIANTS))
    var_s =(((r0 - B_PAST_CHUNKS * CHUNK, 0, B_PAST_CHUNKS),),)
    return dict(pre=pre[None], w_in=w_in_p, q_norm=q_norm[None], kv_norm=kv_norm[None], w_uq=w_uq_p, w_k=w_k,
                w_vt=w_vt, wkt=wkt, wv_s=wv_s, sel=sel, toep=toep, var_p=var_p, var_s=var_s,
                w_out=w_out.astype(BF16), post=post[None])


def _prep_c(pre, post, w_in, sinks, w_out):
    q, k, v, g = jnp.split(w_in, [1024, 1152, 1280], axis=1)
    k0, k1 = k[:, :C_HEAD_DIM], k[:, C_HEAD_DIM:]
    v0, v1 = v[:, :C_HEAD_DIM], v[:, C_HEAD_DIM:]
    zero = jnp.zeros_like(v0)
    w_in_p = jnp.concatenate([q, k0, k0, k1, k1, v0, zero, v1, zero, g], axis=1).astype(BF16)
    var_p = tuple(tuple((0, c - C_PAST_CHUNKS, c) for c in range(C_GROUP_CHUNKS * v, C_GROUP_CHUNKS * (v + 1)))
                  for v in range(C_VARIANTS))
    var_s = (((0, 1, C_PAST_CHUNKS + 1),),)
    sink_row = jnp.repeat((sinks * LOG2E).reshape(C_KV_HEADS, C_GROUP), CHUNK, axis=1)[:, None, :]
    return dict(pre=pre[None], w_in=w_in_p, var_p=var_p, var_s=var_s, sink_s=sink_row,
                sink_p=jnp.tile(sink_row, (1, 1, C_GROUP_CHUNKS)), w_out=w_out.astype(BF16), post=post[None])


def _dup_heads(x):
    return jnp.concatenate([x[:, :, 0], x[:, :, 0], x[:, :, 1], x[:, :, 1]], axis=-1)


def kernel(x_prompt, x_sample, cache_a_ckv, cache_a_krope, cache_b_k, cache_b_v, cache_c_k, cache_c_v,
           ab_pre_norm, ab_post_norm, ab_w_in, ab_q_norm, ab_kv_norm, ab_w_uq, ab_w_ukv, ab_rel_bias, ab_w_out,
           c_pre_norm, c_post_norm, c_w_in, c_sinks, c_w_out):
    bsz, seq, _ = x_prompt.shape
    dbs, dseq, _ = x_sample.shape
    past = cache_a_ckv.shape[2]
    n_s = dbs * dseq
    pos_p = jnp.arange(seq, dtype=jnp.int32)
    pos_s = jnp.tile(past + jnp.arange(dseq, dtype=jnp.int32), dbs)
    wab = _prep_ab(ab_pre_norm[0], ab_post_norm[0], ab_w_in[0], ab_q_norm[0], ab_kv_norm[0], ab_w_uq[0],
                   ab_w_ukv[0], ab_rel_bias[0], ab_w_out[0])
    wc = _prep_c(c_pre_norm[0], c_post_norm[0], c_w_in[0], c_sinks[0], c_w_out[0])
    b_tail = min(B_PAST_CHUNKS * CHUNK, seq)
    c_tail = min(C_WINDOW, seq)
    tile = 512

    rope_a_p = _rope_tables(pos_p, A_ROPE, A_ROPE_PATTERN, from_zero=True)
    (qa, ka, vt, qb, kb, vbt, sg, c_new_p, kr_new_p_t, kb_tail, vb_tail) = _ab_in_proj(
        x_prompt, rope_a_p, wab, tm=tile, tail_len=b_tail, tkv=tile)
    o_a = _mla_prompt(qa, ka, vt, tq=tile)
    o_b = _band_attention(qb, kb, vbt, wab["toep"], None, variants=wab["var_p"], n_blk=B_WIN_BLOCKS,
                          n_groups=B_HEADS // 2, n_qblk=1, cpg=B_GROUP_CHUNKS, ones_row=False)

    rope_a_s = _rope_tables(pos_s, A_ROPE, A_ROPE_PATTERN)
    xs = x_sample.reshape(1, n_s, D_MODEL)
    (qa_s, _, _, qb_s, _, _, sg_s, c_new_s, kr_new_s_t, kb_s32, vb_s32) = _ab_in_proj(
        xs, rope_a_s, wab, tm=n_s, tail_len=n_s, tkv=n_s)
    kr_new_s_t = jnp.transpose(kr_new_s_t.reshape(A_ROPE, dbs, dseq), (1, 0, 2))
    o_a_s = _mla_sample(qa_s.reshape(dbs, dseq, -1), cache_a_ckv[0], jnp.swapaxes(cache_a_krope[0], 1, 2),
                        c_new_s.reshape(dbs, dseq, -1), kr_new_s_t,
                        wab["wkt"], wab["wv_s"], wab["sel"], tk=512)
    wb = cache_b_k.shape[2]

    def buf_t(cache):
        return jnp.transpose(cache[0], (0, 2, 3, 1)).reshape(dbs, B_WIDTH, wb)

    o_b_s = _band_attention(qb_s.reshape(dbs, dseq, -1),
                            (buf_t(cache_b_k), kb_s32.reshape(dbs, dseq, -1)),
                            (buf_t(cache_b_v), vb_s32.reshape(dbs, dseq, -1)),
                            wab["toep"], None, variants=wab["var_s"], n_blk=B_SAMPLE_BLOCKS, n_groups=B_HEADS // 2,
                            n_qblk=1, cpg=1, ones_row=False)

    rope_c_p = _rope_tables(pos_p, C_ROT, C_ROPE_PATTERN, from_zero=True)
    h1_p, qc, kc, vct, sgc, kc_tail, vc_tail = _c_in_proj(
        [o_a, o_b], sg, wab["w_out"], wab["post"], x_prompt, rope_c_p, wc, tm=tile, tail_len=c_tail)
    o_c = _band_attention(qc, kc, vct, None, wc["sink_p"], variants=wc["var_p"], n_blk=C_WIN_BLOCKS,
                          n_groups=C_KV_HEADS, n_qblk=C_GROUP // 2, cpg=C_GROUP_CHUNKS, ones_row=True)
    h2_p = _out_proj([o_c], sgc, wc["w_out"], wc["post"], h1_p, tm=2 * tile)

    rope_c_s = _rope_tables(pos_s, C_ROT, C_ROPE_PATTERN)
    h1_s, qc_s, kc_s, _, sgc_s, kc_s32, vc_s32 = _c_in_proj(
        [o_a_s.reshape(1, n_s, -1), o_b_s.reshape(1, n_s, -1)], sg_s, wab["w_out"], wab["post"], xs, rope_c_s, wc,
        tm=n_s, tail_len=n_s)
    wcw = cache_c_k.shape[2]
    win_c = C_SAMPLE_BLOCKS * LANES
    n_pad = win_c - wcw - dseq
    kcb = jnp.concatenate([jnp.zeros((dbs, n_pad, C_KV_HEADS * LANES), BF16), _dup_heads(cache_c_k[0]).astype(BF16),
                           kc_s.reshape(dbs, dseq, -1)], axis=1)
    vcb = jnp.concatenate([jnp.zeros((dbs, n_pad, C_KV_HEADS, C_HEAD_DIM), F32), cache_c_v[0],
                           vc_s32.reshape(dbs, dseq, C_KV_HEADS, C_HEAD_DIM)], axis=1)
    vcb_t = jnp.concatenate([jnp.transpose(vcb, (0, 2, 3, 1)), jnp.ones((dbs, C_KV_HEADS, 1, win_c), F32),
                             jnp.zeros((dbs, C_KV_HEADS, LANES - C_HEAD_DIM - 1, win_c), F32)], axis=2)
    vcb_t = jnp.transpose(vcb_t.astype(BF16).reshape(dbs, C_KV_HEADS, LANES, C_SAMPLE_BLOCKS, LANES), (0, 1, 3, 2, 4))
    o_c_s = _band_attention(qc_s.reshape(dbs, dseq, -1), kcb, vcb_t, None, wc["sink_s"], variants=wc["var_s"],
                            n_blk=C_SAMPLE_BLOCKS, n_groups=C_KV_HEADS, n_qblk=C_GROUP // 2, cpg=1, ones_row=True)
    h2_s = _out_proj([o_c_s.reshape(1, n_s, -1)], sgc_s, wc["w_out"], wc["post"], h1_s, tm=n_s)

    def roll_in(buf, new):
        return jnp.concatenate([buf, new], axis=1)[:, -buf.shape[1]:][None]

    return (h2_p, h2_s.reshape(dbs, dseq, D_MODEL),
            c_new_p[None], jnp.swapaxes(kr_new_p_t, 1, 2)[None],
            kb_tail.reshape(1, bsz, b_tail, B_HEADS, B_HEAD_DIM), vb_tail.reshape(1, bsz, b_tail, B_HEADS, B_HEAD_DIM),
            kc_tail.reshape(1, bsz, c_tail, C_KV_HEADS, C_HEAD_DIM), vc_tail.reshape(1, bsz, c_tail, C_KV_HEADS, C_HEAD_DIM),
            c_new_s.reshape(1, dbs, dseq, A_KV_RANK), jnp.swapaxes(kr_new_s_t, 1, 2)[None],
            roll_in(cache_b_k[0], kb_s32.reshape(dbs, dseq, B_HEADS, B_HEAD_DIM)),
            roll_in(cache_b_v[0], vb_s32.reshape(dbs, dseq, B_HEADS, B_HEAD_DIM)),
            roll_in(cache_c_k[0], kc_s32.reshape(dbs, dseq, C_KV_HEADS, C_HEAD_DIM)),
            roll_in(cache_c_v[0], vc_s32.reshape(dbs, dseq, C_KV_HEADS, C_HEAD_DIM)))
```
